```python
import math
import jax, jax.numpy as jnp
from jax import lax
import numpy as np

D_MODEL = 1024
BATCH = 32
SEQ = 2048
DEPTH = 1

N_META = 16
SSM_WIDTH = 1024
SSM_GROUP = 16
SSM_GROUPS = SSM_WIDTH // SSM_GROUP
SSM_STATE = 64
DT_MIN = 1e-3
DT_MAX = 1e-1
HGRN_WIDTH = 1024
HGRN_HEAD_DIM = 128
HGRN_HEADS = HGRN_WIDTH // HGRN_HEAD_DIM
HGRN_CHUNK = 16
D_FF = 2816
CONV_WIDTH = 3
EPS = 1e-6
IN_COLS = SSM_WIDTH + 4 * HGRN_WIDTH + 2 * D_MODEL

kernel_name = 'hybrid_s5_hgrn2_gated_merge_block'


def rmsnorm(x, g):
    xf = x.astype(jnp.float32)
    y = xf * lax.rsqrt(jnp.mean(xf * xf, axis=-1, keepdims=True) + EPS)
    return (y * g.astype(jnp.float32)).astype(x.dtype)


def _complex_affine_combine(e1, e2):
    a1r, a1i, b1r, b1i = e1
    a2r, a2i, b2r, b2i = e2
    ar = a1r * a2r - a1i * a2i
    ai = a1r * a2i + a1i * a2r
    br = a2r * b1r - a2i * b1i + b2r
    bi = a2r * b1i + a2i * b1r + b2i
    return ar, ai, br, bi


def s5_mixer(u, lam_re, lam_im, log_dt, b_re, b_im, c_re, c_im, d_skip, w_glu):
    bsz, L, _ = u.shape
    uf = u.astype(jnp.float32).reshape(bsz, L, SSM_GROUPS, SSM_GROUP)
    lr = lam_re.astype(jnp.float32)
    li = lam_im.astype(jnp.float32)
    dt = jnp.exp(log_dt.astype(jnp.float32))[:, None]
    mag = jnp.exp(lr * dt)
    ab_re = mag * jnp.cos(li * dt)
    ab_im = mag * jnp.sin(li * dt)
    den = lr * lr + li * li
    nr = ab_re - 1.0
    coef_re = (nr * lr + ab_im * li) / den
    coef_im = (ab_im * lr - nr * li) / den
    br = b_re.astype(jnp.float32)
    bi = b_im.astype(jnp.float32)
    bb_re = coef_re[..., None] * br - coef_im[..., None] * bi
    bb_im = coef_re[..., None] * bi + coef_im[..., None] * br
    v_re = jnp.einsum('blgh,gph->blgp', uf, bb_re)
    v_im = jnp.einsum('blgh,gph->blgp', uf, bb_im)
    a_re = jnp.broadcast_to(ab_re[None, None], (1, L, SSM_GROUPS, SSM_STATE))
    a_im = jnp.broadcast_to(ab_im[None, None], (1, L, SSM_GROUPS, SSM_STATE))
    _, _, s_re, s_im = lax.associative_scan(_complex_affine_combine, (a_re, a_im, v_re, v_im), axis=1)
    y = (jnp.einsum('blgp,ghp->blgh', s_re, c_re.astype(jnp.float32))
         - jnp.einsum('blgp,ghp->blgh', s_im, c_im.astype(jnp.float32))
         + d_skip.astype(jnp.float32).reshape(SSM_GROUPS, SSM_GROUP) * uf)
    y = jax.nn.gelu(y.reshape(bsz, L, SSM_WIDTH)).astype(u.dtype)
    return y * jax.nn.sigmoid(y @ w_glu)


def hgrn2_mixer(q, f_logit, i_in, og, lb, norm_g):
    bsz, L, _ = q.shape
    n_chunks = L // HGRN_CHUNK

    def heads(t):
        return t.reshape(bsz, n_chunks, HGRN_CHUNK, HGRN_HEADS, HGRN_HEAD_DIM).transpose(1, 0, 3, 2, 4)

    lbf = lb.astype(jnp.float32)
    f = lbf + (1.0 - lbf) * jax.nn.sigmoid(f_logit.astype(jnp.float32))
    log_f = jnp.log(f)
    qh = heads(q.astype(jnp.float32))
    kh = heads(1.0 - f)
    vh = heads(i_in.astype(jnp.float32))
    cum = jnp.cumsum(heads(log_f), axis=3)
    q_in = qh * jnp.exp(cum)
    k_in = kh * jnp.exp(-cum)
    k_out = kh * jnp.exp(cum[..., -1:, :] - cum)
    chunk_decay = jnp.exp(cum[..., -1, :])
    causal = jnp.tril(jnp.ones((HGRN_CHUNK, HGRN_CHUNK), dtype=bool))

    def chunk_step(state, xs):
        qi, ki, ko, v, dec = xs
        scores = jnp.where(causal, jnp.einsum('bhtd,bhsd->bhts', qi, ki), 0.0)
        o = jnp.einsum('bhts,bhsv->bhtv', scores, v) + jnp.einsum('bhtd,bhdv->bhtv', qi, state)
        state = dec[..., None] * state + jnp.einsum('bhsd,bhsv->bhdv', ko, v)
        return state, o

    init = jnp.zeros((bsz, HGRN_HEADS, HGRN_HEAD_DIM, HGRN_HEAD_DIM), jnp.float32)
    _, o = lax.scan(chunk_step, init, (q_in, k_in, k_out, vh, chunk_decay))
    o = o.transpose(1, 0, 3, 2, 4).reshape(bsz, L, HGRN_HEADS, HGRN_HEAD_DIM)
    o = o * lax.rsqrt(jnp.mean(o * o, axis=-1, keepdims=True) + EPS) * norm_g.astype(jnp.float32)
    o = o.reshape(bsz, L, HGRN_WIDTH).astype(q.dtype)
    return o * jax.nn.silu(og)


def causal_dwconv(u, w, b):
    L = u.shape[1]
    up = jnp.pad(u, ((0, 0), (CONV_WIDTH - 1, 0), (0, 0)))
    out = b
    for j in range(CONV_WIDTH):
        out = out + up[:, j:j + L, :] * w[j]
    return out


def _fwd_setup_inputs(seed: int = 0) -> dict:
    key = jax.random.key(seed)
    ks = jax.random.split(key, 24)
    f32 = jnp.float32

    def nrm(k, shape, scale):
        return jax.random.normal(k, shape, f32) * scale

    n_idx = jnp.arange(SSM_STATE, dtype=f32)
    return {
        'x': nrm(ks[0], (BATCH, SEQ, D_MODEL), 1.0),
        'meta_tokens': nrm(ks[1], (N_META, D_MODEL), 1.0),
        'mix_norm_g': 1.0 + nrm(ks[2], (DEPTH, D_MODEL), 0.02),
        'w_in': nrm(ks[3], (DEPTH, D_MODEL, IN_COLS), D_MODEL ** -0.5),
        'ssm_lambda_re': -0.5 + nrm(ks[4], (DEPTH, SSM_GROUPS, SSM_STATE), 0.01),
        'ssm_lambda_im': math.pi * n_idx + nrm(ks[5], (DEPTH, SSM_GROUPS, SSM_STATE), 0.01),
        'ssm_log_dt': jax.random.uniform(ks[6], (DEPTH, SSM_GROUPS), f32, math.log(DT_MIN), math.log(DT_MAX)),
        'ssm_b_re': nrm(ks[7], (DEPTH, SSM_GROUPS, SSM_STATE, SSM_GROUP), (2 * SSM_GROUP) ** -0.5),
        'ssm_b_im': nrm(ks[8], (DEPTH, SSM_GROUPS, SSM_STATE, SSM_GROUP), (2 * SSM_GROUP) ** -0.5),
        'ssm_c_re': nrm(ks[9], (DEPTH, SSM_GROUPS, SSM_GROUP, SSM_STATE), SSM_STATE ** -0.5),
        'ssm_c_im': nrm(ks[10], (DEPTH, SSM_GROUPS, SSM_GROUP, SSM_STATE), SSM_STATE ** -0.5),
        'ssm_d': nrm(ks[11], (DEPTH, SSM_WIDTH), 1.0),
        'ssm_w_glu': nrm(ks[12], (DEPTH, SSM_WIDTH, SSM_WIDTH), SSM_WIDTH ** -0.5),
        'w_ssm_proj': nrm(ks[13], (DEPTH, SSM_WIDTH, D_MODEL), SSM_WIDTH ** -0.5),
        'hgrn_lb_logits': nrm(ks[14], (DEPTH + 1, HGRN_WIDTH), 0.1),
        'hgrn_norm_g': 1.0 + nrm(ks[15], (DEPTH, HGRN_HEAD_DIM), 0.02),
        'w_hgrn_proj': nrm(ks[16], (DEPTH, HGRN_WIDTH, D_MODEL), HGRN_WIDTH ** -0.5),
        'w_out': nrm(ks[17], (DEPTH, D_MODEL, D_MODEL), D_MODEL ** -0.5),
        'ffn_norm_g': 1.0 + nrm(ks[18], (DEPTH, D_MODEL), 0.02),
        'w_up': nrm(ks[19], (DEPTH, D_MODEL, 2 * D_FF), D_MODEL ** -0.5),
        'conv_w': nrm(ks[20], (DEPTH, CONV_WIDTH, 2 * D_FF), CONV_WIDTH ** -0.5),
        'conv_b': nrm(ks[21], (DEPTH, 2 * D_FF), 0.01),
        'w_down': nrm(ks[22], (DEPTH, D_FF, D_MODEL), D_FF ** -0.5),
        'final_norm_g': 1.0 + nrm(ks[23], (D_MODEL,), 0.02),
    }


def _fwd_reference(x, meta_tokens, mix_norm_g, w_in, ssm_lambda_re, ssm_lambda_im, ssm_log_dt,
              ssm_b_re, ssm_b_im, ssm_c_re, ssm_c_im, ssm_d, ssm_w_glu, w_ssm_proj,
              hgrn_lb_logits, hgrn_norm_g, w_hgrn_proj, w_out, ffn_norm_g, w_up, conv_w,
              conv_b, w_down, final_norm_g):
    bsz = x.shape[0]
    meta = jnp.broadcast_to(meta_tokens.astype(x.dtype)[None], (bsz, N_META, D_MODEL))
    h = jnp.concatenate([meta, x], axis=1)
    lower_bounds = jnp.cumsum(jax.nn.softmax(hgrn_lb_logits.astype(jnp.float32), axis=0), axis=0)
    o1 = SSM_WIDTH
    o2 = o1 + HGRN_WIDTH
    o3 = o2 + HGRN_WIDTH
    o4 = o3 + HGRN_WIDTH
    o5 = o4 + HGRN_WIDTH
    o6 = o5 + D_MODEL
    for l in range(DEPTH):
        z = rmsnorm(h, mix_norm_g[l])
        p = z @ w_in[l]
        y_a = s5_mixer(p[..., :o1], ssm_lambda_re[l], ssm_lambda_im[l], ssm_log_dt[l],
                       ssm_b_re[l], ssm_b_im[l], ssm_c_re[l], ssm_c_im[l], ssm_d[l], ssm_w_glu[l])
        y_b = hgrn2_mixer(p[..., o1:o2], p[..., o2:o3], p[..., o3:o4], p[..., o4:o5],
                          lower_bounds[l], hgrn_norm_g[l])
        merged = (jax.nn.sigmoid(p[..., o5:o6]) * (y_a @ w_ssm_proj[l])
                  + jax.nn.sigmoid(p[..., o6:]) * (y_b @ w_hgrn_proj[l]))
        h = h + merged @ w_out[l]
        z = rmsnorm(h, ffn_norm_g[l])
        u = causal_dwconv(z @ w_up[l], conv_w[l], conv_b[l])
        h = h + (jax.nn.silu(u[..., :D_FF]) * u[..., D_FF:]) @ w_down[l]
    return rmsnorm(h[:, N_META:], final_norm_g)


import jax as _jax
import jax.numpy as _jnp

TWIN_FORMAT = 'train_step'
FWD_PARAMS = ['x', 'meta_tokens', 'mix_norm_g', 'w_in', 'ssm_lambda_re', 'ssm_lambda_im', 'ssm_log_dt', 'ssm_b_re', 'ssm_b_im', 'ssm_c_re', 'ssm_c_im', 'ssm_d', 'ssm_w_glu', 'w_ssm_proj', 'hgrn_lb_logits', 'hgrn_norm_g', 'w_hgrn_proj', 'w_out', 'ffn_norm_g', 'w_up', 'conv_w', 'conv_b', 'w_down', 'final_norm_g']
TWIN_WEIGHTS = ['meta_tokens', 'mix_norm_g', 'w_in', 'ssm_lambda_re', 'ssm_lambda_im', 'ssm_log_dt', 'ssm_b_re', 'ssm_b_im', 'ssm_c_re', 'ssm_c_im', 'ssm_d', 'ssm_w_glu', 'w_ssm_proj', 'hgrn_lb_logits', 'hgrn_norm_g', 'w_hgrn_proj', 'w_out', 'ffn_norm_g', 'w_up', 'conv_w', 'conv_b', 'w_down', 'final_norm_g']
TWIN_DIFF_INPUT = 'x'
TWIN_INPUTS = ['x', 'meta_tokens', 'mix_norm_g', 'w_in', 'ssm_lambda_re', 'ssm_lambda_im', 'ssm_log_dt', 'ssm_b_re', 'ssm_b_im', 'ssm_c_re', 'ssm_c_im', 'ssm_d', 'ssm_w_glu', 'w_ssm_proj', 'hgrn_lb_logits', 'hgrn_norm_g', 'w_hgrn_proj', 'w_out', 'ffn_norm_g', 'w_up', 'conv_w', 'conv_b', 'w_down', 'final_norm_g', 'loss_target', 'm_meta_tokens', 'm_mix_norm_g', 'm_w_in', 'm_ssm_lambda_re', 'm_ssm_lambda_im', 'm_ssm_log_dt', 'm_ssm_b_re', 'm_ssm_b_im', 'm_ssm_c_re', 'm_ssm_c_im', 'm_ssm_d', 'm_ssm_w_glu', 'm_w_ssm_proj', 'm_hgrn_lb_logits', 'm_hgrn_norm_g', 'm_w_hgrn_proj', 'm_w_out', 'm_ffn_norm_g', 'm_w_up', 'm_conv_w', 'm_conv_b', 'm_w_down', 'm_final_norm_g', 'v_meta_tokens', 'v_mix_norm_g', 'v_w_in', 'v_ssm_lambda_re', 'v_ssm_lambda_im', 'v_ssm_log_dt', 'v_ssm_b_re', 'v_ssm_b_im', 'v_ssm_c_re', 'v_ssm_c_im', 'v_ssm_d', 'v_ssm_w_glu', 'v_w_ssm_proj', 'v_hgrn_lb_logits', 'v_hgrn_norm_g', 'v_w_hgrn_proj', 'v_w_out', 'v_ffn_norm_g', 'v_w_up', 'v_conv_w', 'v_conv_b', 'v_w_down', 'v_final_norm_g']
TWIN_OUTPUTS = ['loss', 'grad_x', 'grad_meta_tokens', 'grad_mix_norm_g', 'grad_w_in', 'grad_ssm_lambda_re', 'grad_ssm_lambda_im', 'grad_ssm_log_dt', 'grad_ssm_b_re', 'grad_ssm_b_im', 'grad_ssm_c_re', 'grad_ssm_c_im', 'grad_ssm_d', 'grad_ssm_w_glu', 'grad_w_ssm_proj', 'grad_hgrn_lb_logits', 'grad_hgrn_norm_g', 'grad_w_hgrn_proj', 'grad_w_out', 'grad_ffn_norm_g', 'grad_w_up', 'grad_conv_w', 'grad_conv_b', 'grad_w_down', 'grad_final_norm_g', 'delta_meta_tokens', 'delta_mix_norm_g', 'delta_w_in', 'delta_ssm_lambda_re', 'delta_ssm_lambda_im', 'delta_ssm_log_dt', 'delta_ssm_b_re', 'delta_ssm_b_im', 'delta_ssm_c_re', 'delta_ssm_c_im', 'delta_ssm_d', 'delta_ssm_w_glu', 'delta_w_ssm_proj', 'delta_hgrn_lb_logits', 'delta_hgrn_norm_g', 'delta_w_hgrn_proj', 'delta_w_out', 'delta_ffn_norm_g', 'delta_w_up', 'delta_conv_w', 'delta_conv_b', 'delta_w_down', 'delta_final_norm_g', 'new_m_meta_tokens', 'new_m_mix_norm_g', 'new_m_w_in', 'new_m_ssm_lambda_re', 'new_m_ssm_lambda_im', 'new_m_ssm_log_dt', 'new_m_ssm_b_re', 'new_m_ssm_b_im', 'new_m_ssm_c_re', 'new_m_ssm_c_im', 'new_m_ssm_d', 'new_m_ssm_w_glu', 'new_m_w_ssm_proj', 'new_m_hgrn_lb_logits', 'new_m_hgrn_norm_g', 'new_m_w_hgrn_proj', 'new_m_w_out', 'new_m_ffn_norm_g', 'new_m_w_up', 'new_m_conv_w', 'new_m_conv_b', 'new_m_w_down', 'new_m_final_norm_g', 'new_v_meta_tokens', 'new_v_mix_norm_g', 'new_v_w_in', 'new_v_ssm_lambda_re', 'new_v_ssm_lambda_im', 'new_v_ssm_log_dt', 'new_v_ssm_b_re', 'new_v_ssm_b_im', 'new_v_ssm_c_re', 'new_v_ssm_c_im', 'new_v_ssm_d', 'new_v_ssm_w_glu', 'new_v_w_ssm_proj', 'new_v_hgrn_lb_logits', 'new_v_hgrn_norm_g', 'new_v_w_hgrn_proj', 'new_v_w_out', 'new_v_ffn_norm_g', 'new_v_w_up', 'new_v_conv_w', 'new_v_conv_b', 'new_v_w_down', 'new_v_final_norm_g']
TWIN_LEAF_KINDS = {'loss': 'loss', 'grad_x': 'grad_x', 'grad_meta_tokens': 'grad_w', 'grad_mix_norm_g': 'grad_w', 'grad_w_in': 'grad_w', 'grad_ssm_lambda_re': 'grad_w', 'grad_ssm_lambda_im': 'grad_w', 'grad_ssm_log_dt': 'grad_w', 'grad_ssm_b_re': 'grad_w', 'grad_ssm_b_im': 'grad_w', 'grad_ssm_c_re': 'grad_w', 'grad_ssm_c_im': 'grad_w', 'grad_ssm_d': 'grad_w', 'grad_ssm_w_glu': 'grad_w', 'grad_w_ssm_proj': 'grad_w', 'grad_hgrn_lb_logits': 'grad_w', 'grad_hgrn_norm_g': 'grad_w', 'grad_w_hgrn_proj': 'grad_w', 'grad_w_out': 'grad_w', 'grad_ffn_norm_g': 'grad_w', 'grad_w_up': 'grad_w', 'grad_conv_w': 'grad_w', 'grad_conv_b': 'grad_w', 'grad_w_down': 'grad_w', 'grad_final_norm_g': 'grad_w', 'delta_meta_tokens': 'delta_w', 'delta_mix_norm_g': 'delta_w', 'delta_w_in': 'delta_w', 'delta_ssm_lambda_re': 'delta_w', 'delta_ssm_lambda_im': 'delta_w', 'delta_ssm_log_dt': 'delta_w', 'delta_ssm_b_re': 'delta_w', 'delta_ssm_b_im': 'delta_w', 'delta_ssm_c_re': 'delta_w', 'delta_ssm_c_im': 'delta_w', 'delta_ssm_d': 'delta_w', 'delta_ssm_w_glu': 'delta_w', 'delta_w_ssm_proj': 'delta_w', 'delta_hgrn_lb_logits': 'delta_w', 'delta_hgrn_norm_g': 'delta_w', 'delta_w_hgrn_proj': 'delta_w', 'delta_w_out': 'delta_w', 'delta_ffn_norm_g': 'delta_w', 'delta_w_up': 'delta_w', 'delta_conv_w': 'delta_w', 'delta_conv_b': 'delta_w', 'delta_w_down': 'delta_w', 'delta_final_norm_g': 'delta_w', 'new_m_meta_tokens': 'new_m', 'new_m_mix_norm_g': 'new_m', 'new_m_w_in': 'new_m', 'new_m_ssm_lambda_re': 'new_m', 'new_m_ssm_lambda_im': 'new_m', 'new_m_ssm_log_dt': 'new_m', 'new_m_ssm_b_re': 'new_m', 'new_m_ssm_b_im': 'new_m', 'new_m_ssm_c_re': 'new_m', 'new_m_ssm_c_im': 'new_m', 'new_m_ssm_d': 'new_m', 'new_m_ssm_w_glu': 'new_m', 'new_m_w_ssm_proj': 'new_m', 'new_m_hgrn_lb_logits': 'new_m', 'new_m_hgrn_norm_g': 'new_m', 'new_m_w_hgrn_proj': 'new_m', 'new_m_w_out': 'new_m', 'new_m_ffn_norm_g': 'new_m', 'new_m_w_up': 'new_m', 'new_m_conv_w': 'new_m', 'new_m_conv_b': 'new_m', 'new_m_w_down': 'new_m', 'new_m_final_norm_g': 'new_m', 'new_v_meta_tokens': 'new_v', 'new_v_mix_norm_g': 'new_v', 'new_v_w_in': 'new_v', 'new_v_ssm_lambda_re': 'new_v', 'new_v_ssm_lambda_im': 'new_v', 'new_v_ssm_log_dt': 'new_v', 'new_v_ssm_b_re': 'new_v', 'new_v_ssm_b_im': 'new_v', 'new_v_ssm_c_re': 'new_v', 'new_v_ssm_c_im': 'new_v', 'new_v_ssm_d': 'new_v', 'new_v_ssm_w_glu': 'new_v', 'new_v_w_ssm_proj': 'new_v', 'new_v_hgrn_lb_logits': 'new_v', 'new_v_hgrn_norm_g': 'new_v', 'new_v_w_hgrn_proj': 'new_v', 'new_v_w_out': 'new_v', 'new_v_ffn_norm_g': 'new_v', 'new_v_w_up': 'new_v', 'new_v_conv_w': 'new_v', 'new_v_conv_b': 'new_v', 'new_v_w_down': 'new_v', 'new_v_final_norm_g': 'new_v'}


def _forward(args):
    return _fwd_reference(*[args[k] for k in FWD_PARAMS])


def _output_shape():
    out = _jax.eval_shape(lambda: _forward(_fwd_setup_inputs(0)))
    return out.shape, out.dtype

N_MICROBATCH = 1
ADAM_LR = 0.001
ADAM_B1 = 0.9
ADAM_B2 = 0.999
ADAM_EPS = 1e-08
ADAM_WD = 0.01
ADAM_STEP = 10
PER_EXAMPLE_BATCH_AXIS = {'x': 0, 'loss_target': 0}
SHARED_INPUTS = []
_WEIGHT_DTYPES = {'meta_tokens': _jnp.float32, 'mix_norm_g': _jnp.float32, 'w_in': _jnp.float32, 'ssm_lambda_re': _jnp.float32, 'ssm_lambda_im': _jnp.float32, 'ssm_log_dt': _jnp.float32, 'ssm_b_re': _jnp.float32, 'ssm_b_im': _jnp.float32, 'ssm_c_re': _jnp.float32, 'ssm_c_im': _jnp.float32, 'ssm_d': _jnp.float32, 'ssm_w_glu': _jnp.float32, 'w_ssm_proj': _jnp.float32, 'hgrn_lb_logits': _jnp.float32, 'hgrn_norm_g': _jnp.float32, 'w_hgrn_proj': _jnp.float32, 'w_out': _jnp.float32, 'ffn_norm_g': _jnp.float32, 'w_up': _jnp.float32, 'conv_w': _jnp.float32, 'conv_b': _jnp.float32, 'w_down': _jnp.float32, 'final_norm_g': _jnp.float32}
MOMENT_SCALE = {'meta_tokens': 3.792657e-03, 'mix_norm_g': 1.997442e-01, 'w_in': 7.473405e-02, 'ssm_lambda_re': 3.823375e-03, 'ssm_lambda_im': 4.413653e-03, 'ssm_log_dt': 2.462904e+00, 'ssm_b_re': 2.594180e-03, 'ssm_b_im': 2.500681e-03, 'ssm_c_re': 3.619992e-03, 'ssm_c_im': 3.620412e-03, 'ssm_d': 5.810545e-02, 'ssm_w_glu': 1.495497e-02, 'w_ssm_proj': 4.989544e-02, 'hgrn_lb_logits': 5.755278e-02, 'hgrn_norm_g': 2.696044e-01, 'w_hgrn_proj': 8.521275e-02, 'w_out': 9.859094e-02, 'ffn_norm_g': 1.771737e-01, 'w_up': 7.547515e-02, 'conv_w': 7.803057e-02, 'conv_b': 7.801190e-02, 'w_down': 1.233463e-01, 'final_norm_g': 6.393162e+01}


def _to_microbatches(a, axis):
    t = _jnp.moveaxis(a, axis, 0)
    t = t.reshape((N_MICROBATCH, t.shape[0] // N_MICROBATCH) + t.shape[1:])
    return _jnp.moveaxis(t, 1, axis + 1)


def setup_inputs(seed: int = 0) -> dict:
    inp = _fwd_setup_inputs(seed)
    key = _jax.random.fold_in(_jax.random.key(seed), 7919)
    shape, _ = _output_shape()
    out = dict(inp)
    out["loss_target"] = _jax.random.normal(_jax.random.fold_in(key, 0), shape, _jnp.float32)
    for i, name in enumerate(TWIN_WEIGHTS):
        w = inp[name].astype(_jnp.float32)
        if MOMENT_SCALE is None:
            s = _jnp.sqrt(_jnp.mean(_jnp.square(w)) + 1e-30)
        else:
            s = MOMENT_SCALE[name]
        km, kv = _jax.random.split(_jax.random.fold_in(key, i + 1))
        out[name] = w
        out["m_" + name] = s * _jax.random.normal(km, w.shape, _jnp.float32)
        out["v_" + name] = (s * s) * _jax.random.uniform(kv, w.shape, _jnp.float32, 0.5, 1.5)
    if N_MICROBATCH > 1:
        for name, axis in PER_EXAMPLE_BATCH_AXIS.items():
            out[name] = _to_microbatches(out[name], axis)
    return {'x': out['x'], 'meta_tokens': out['meta_tokens'], 'mix_norm_g': out['mix_norm_g'], 'w_in': out['w_in'], 'ssm_lambda_re': out['ssm_lambda_re'], 'ssm_lambda_im': out['ssm_lambda_im'], 'ssm_log_dt': out['ssm_log_dt'], 'ssm_b_re': out['ssm_b_re'], 'ssm_b_im': out['ssm_b_im'], 'ssm_c_re': out['ssm_c_re'], 'ssm_c_im': out['ssm_c_im'], 'ssm_d': out['ssm_d'], 'ssm_w_glu': out['ssm_w_glu'], 'w_ssm_proj': out['w_ssm_proj'], 'hgrn_lb_logits': out['hgrn_lb_logits'], 'hgrn_norm_g': out['hgrn_norm_g'], 'w_hgrn_proj': out['w_hgrn_proj'], 'w_out': out['w_out'], 'ffn_norm_g': out['ffn_norm_g'], 'w_up': out['w_up'], 'conv_w': out['conv_w'], 'conv_b': out['conv_b'], 'w_down': out['w_down'], 'final_norm_g': out['final_norm_g'], 'loss_target': out['loss_target'], 'm_meta_tokens': out['m_meta_tokens'], 'm_mix_norm_g': out['m_mix_norm_g'], 'm_w_in': out['m_w_in'], 'm_ssm_lambda_re': out['m_ssm_lambda_re'], 'm_ssm_lambda_im': out['m_ssm_lambda_im'], 'm_ssm_log_dt': out['m_ssm_log_dt'], 'm_ssm_b_re': out['m_ssm_b_re'], 'm_ssm_b_im': out['m_ssm_b_im'], 'm_ssm_c_re': out['m_ssm_c_re'], 'm_ssm_c_im': out['m_ssm_c_im'], 'm_ssm_d': out['m_ssm_d'], 'm_ssm_w_glu': out['m_ssm_w_glu'], 'm_w_ssm_proj': out['m_w_ssm_proj'], 'm_hgrn_lb_logits': out['m_hgrn_lb_logits'], 'm_hgrn_norm_g': out['m_hgrn_norm_g'], 'm_w_hgrn_proj': out['m_w_hgrn_proj'], 'm_w_out': out['m_w_out'], 'm_ffn_norm_g': out['m_ffn_norm_g'], 'm_w_up': out['m_w_up'], 'm_conv_w': out['m_conv_w'], 'm_conv_b': out['m_conv_b'], 'm_w_down': out['m_w_down'], 'm_final_norm_g': out['m_final_norm_g'], 'v_meta_tokens': out['v_meta_tokens'], 'v_mix_norm_g': out['v_mix_norm_g'], 'v_w_in': out['v_w_in'], 'v_ssm_lambda_re': out['v_ssm_lambda_re'], 'v_ssm_lambda_im': out['v_ssm_lambda_im'], 'v_ssm_log_dt': out['v_ssm_log_dt'], 'v_ssm_b_re': out['v_ssm_b_re'], 'v_ssm_b_im': out['v_ssm_b_im'], 'v_ssm_c_re': out['v_ssm_c_re'], 'v_ssm_c_im': out['v_ssm_c_im'], 'v_ssm_d': out['v_ssm_d'], 'v_ssm_w_glu': out['v_ssm_w_glu'], 'v_w_ssm_proj': out['v_w_ssm_proj'], 'v_hgrn_lb_logits': out['v_hgrn_lb_logits'], 'v_hgrn_norm_g': out['v_hgrn_norm_g'], 'v_w_hgrn_proj': out['v_w_hgrn_proj'], 'v_w_out': out['v_w_out'], 'v_ffn_norm_g': out['v_ffn_norm_g'], 'v_w_up': out['v_w_up'], 'v_conv_w': out['v_conv_w'], 'v_conv_b': out['v_conv_b'], 'v_w_down': out['v_w_down'], 'v_final_norm_g': out['v_final_norm_g']}


def _loss(weights, diff, rest, loss_target):
    with _jax.named_scope("forward"):
        args = {**rest, TWIN_DIFF_INPUT: diff, **{k: w.astype(_WEIGHT_DTYPES[k]) for k, w in weights.items()}}
        y = _forward(args)
    with _jax.named_scope("loss_head"):
        err = _jnp.square(y.astype(_jnp.float32) - loss_target)
        return 0.5 * _jnp.sum(_jnp.mean(err, axis=-1)) if err.ndim else 0.5 * err


def _adamw(w, g, m, v):
    m = ADAM_B1 * m + (1.0 - ADAM_B1) * g
    v = ADAM_B2 * v + (1.0 - ADAM_B2) * _jnp.square(g)
    m_hat = m / (1.0 - ADAM_B1 ** ADAM_STEP)
    v_hat = v / (1.0 - ADAM_B2 ** ADAM_STEP)
    delta = -ADAM_LR * (m_hat / (_jnp.sqrt(v_hat) + ADAM_EPS) + ADAM_WD * w)
    return delta, m, v


def reference(x, meta_tokens, mix_norm_g, w_in, ssm_lambda_re, ssm_lambda_im, ssm_log_dt, ssm_b_re, ssm_b_im, ssm_c_re, ssm_c_im, ssm_d, ssm_w_glu, w_ssm_proj, hgrn_lb_logits, hgrn_norm_g, w_hgrn_proj, w_out, ffn_norm_g, w_up, conv_w, conv_b, w_down, final_norm_g, loss_target, m_meta_tokens, m_mix_norm_g, m_w_in, m_ssm_lambda_re, m_ssm_lambda_im, m_ssm_log_dt, m_ssm_b_re, m_ssm_b_im, m_ssm_c_re, m_ssm_c_im, m_ssm_d, m_ssm_w_glu, m_w_ssm_proj, m_hgrn_lb_logits, m_hgrn_norm_g, m_w_hgrn_proj, m_w_out, m_ffn_norm_g, m_w_up, m_conv_w, m_conv_b, m_w_down, m_final_norm_g, v_meta_tokens, v_mix_norm_g, v_w_in, v_ssm_lambda_re, v_ssm_lambda_im, v_ssm_log_dt, v_ssm_b_re, v_ssm_b_im, v_ssm_c_re, v_ssm_c_im, v_ssm_d, v_ssm_w_glu, v_w_ssm_proj, v_hgrn_lb_logits, v_hgrn_norm_g, v_w_hgrn_proj, v_w_out, v_ffn_norm_g, v_w_up, v_conv_w, v_conv_b, v_w_down, v_final_norm_g):
    given = dict(x=x, meta_tokens=meta_tokens, mix_norm_g=mix_norm_g, w_in=w_in, ssm_lambda_re=ssm_lambda_re, ssm_lambda_im=ssm_lambda_im, ssm_log_dt=ssm_log_dt, ssm_b_re=ssm_b_re, ssm_b_im=ssm_b_im, ssm_c_re=ssm_c_re, ssm_c_im=ssm_c_im, ssm_d=ssm_d, ssm_w_glu=ssm_w_glu, w_ssm_proj=w_ssm_proj, hgrn_lb_logits=hgrn_lb_logits, hgrn_norm_g=hgrn_norm_g, w_hgrn_proj=w_hgrn_proj, w_out=w_out, ffn_norm_g=ffn_norm_g, w_up=w_up, conv_w=conv_w, conv_b=conv_b, w_down=w_down, final_norm_g=final_norm_g, loss_target=loss_target, m_meta_tokens=m_meta_tokens, m_mix_norm_g=m_mix_norm_g, m_w_in=m_w_in, m_ssm_lambda_re=m_ssm_lambda_re, m_ssm_lambda_im=m_ssm_lambda_im, m_ssm_log_dt=m_ssm_log_dt, m_ssm_b_re=m_ssm_b_re, m_ssm_b_im=m_ssm_b_im, m_ssm_c_re=m_ssm_c_re, m_ssm_c_im=m_ssm_c_im, m_ssm_d=m_ssm_d, m_ssm_w_glu=m_ssm_w_glu, m_w_ssm_proj=m_w_ssm_proj, m_hgrn_lb_logits=m_hgrn_lb_logits, m_hgrn_norm_g=m_hgrn_norm_g, m_w_hgrn_proj=m_w_hgrn_proj, m_w_out=m_w_out, m_ffn_norm_g=m_ffn_norm_g, m_w_up=m_w_up, m_conv_w=m_conv_w, m_conv_b=m_conv_b, m_w_down=m_w_down, m_final_norm_g=m_final_norm_g, v_meta_tokens=v_meta_tokens, v_mix_norm_g=v_mix_norm_g, v_w_in=v_w_in, v_ssm_lambda_re=v_ssm_lambda_re, v_ssm_lambda_im=v_ssm_lambda_im, v_ssm_log_dt=v_ssm_log_dt, v_ssm_b_re=v_ssm_b_re, v_ssm_b_im=v_ssm_b_im, v_ssm_c_re=v_ssm_c_re, v_ssm_c_im=v_ssm_c_im, v_ssm_d=v_ssm_d, v_ssm_w_glu=v_ssm_w_glu, v_w_ssm_proj=v_w_ssm_proj, v_hgrn_lb_logits=v_hgrn_lb_logits, v_hgrn_norm_g=v_hgrn_norm_g, v_w_hgrn_proj=v_w_hgrn_proj, v_w_out=v_w_out, v_ffn_norm_g=v_ffn_norm_g, v_w_up=v_w_up, v_conv_w=v_conv_w, v_conv_b=v_conv_b, v_w_down=v_w_down, v_final_norm_g=v_final_norm_g)
    weights = {n: given[n] for n in TWIN_WEIGHTS}
    shared = {n: given[n] for n in SHARED_INPUTS}
    per_example = {n: given[n] for n in ['x']}
    grad_fn = _jax.value_and_grad(_loss, argnums=(0, 1))

    def one_microbatch(ex, loss_target):
        ex = dict(ex)
        diff = ex.pop(TWIN_DIFF_INPUT)
        return grad_fn(weights, diff, {**shared, **ex}, loss_target)

    if N_MICROBATCH == 1:
        loss, (grad_w, grad_x) = one_microbatch(per_example, given["loss_target"])
    else:
        def body(carry, xs):
            loss_sum, grad_sum = carry
            l_k, (gw_k, gx_k) = one_microbatch(xs[0], xs[1])
            with _jax.named_scope("update"):
                return (loss_sum + l_k, _jax.tree.map(_jnp.add, grad_sum, gw_k)), gx_k

        init = (_jnp.zeros((), _jnp.float32), _jax.tree.map(_jnp.zeros_like, weights))
        (loss, grad_w), grad_x = _jax.lax.scan(body, init, (per_example, given["loss_target"]))
    with _jax.named_scope("update"):
        delta_w, new_m, new_v = {}, {}, {}
        for n in TWIN_WEIGHTS:
            delta_w[n], new_m[n], new_v[n] = _adamw(weights[n], grad_w[n], given["m_" + n], given["v_" + n])
    return (loss, grad_x, *[grad_w[n] for n in TWIN_WEIGHTS], *[delta_w[n] for n in TWIN_WEIGHTS],
            *[new_m[n] for n in TWIN_WEIGHTS], *[new_v[n] for n in TWIN_WEIGHTS])
```

```python
import functools

import jax
import jax.numpy as jnp
from jax import lax
from jax.experimental import pallas as pl
from jax.experimental.pallas import tpu as pltpu

F32 = jnp.float32
MXU = jnp.bfloat16
ACT = jnp.bfloat16
N_DEV = 8
N_META = 16
CHUNK = 16
EPS = 1e-6
ADAM_LR, ADAM_B1, ADAM_B2, ADAM_EPS, ADAM_WD, ADAM_STEP = 0.001, 0.9, 0.999, 1e-08, 0.01, 10
SUBLANES = 8
LANES = 128
ROW_TILE_CAP = 700
VMEM_LIMIT = 60 * 1024 * 1024


def _cparams(**kw):
    return pltpu.CompilerParams(vmem_limit_bytes=VMEM_LIMIT, **kw)


def _tile(n, cap):
    best = None
    for t in range(16, min(n, cap) + 1, 16):
        if n % t == 0:
            best = t
    assert best is not None, (n, cap)
    return best


def _dot(a, b):
    return lax.dot_general(a.astype(MXU), b.astype(MXU), (((1,), (0,)), ((), ())), preferred_element_type=F32)


def _dot_nt(a, b):
    return lax.dot_general(a.astype(MXU), b.astype(MXU), (((1,), (1,)), ((), ())), preferred_element_type=F32)


def _dot_tn(a, b):
    return lax.dot_general(a.astype(MXU), b.astype(MXU), (((0,), (0,)), ((), ())), preferred_element_type=F32)


def _rms(x, g):
    return x * lax.rsqrt(jnp.mean(x * x, axis=-1, keepdims=True) + EPS) * g


def _silu(x):
    return x * jax.nn.sigmoid(x)


def _small_call(fn, ins, out_shapes, name):
    n_in = len(ins)

    def body(*refs):
        outs = fn(*[r[...] for r in refs[:n_in]])
        outs = outs if isinstance(outs, (tuple, list)) else (outs,)
        for r, o in zip(refs[n_in:], outs):
            r[...] = o.astype(r.dtype)

    vm = pl.BlockSpec(memory_space=pltpu.VMEM)
    return pl.pallas_call(
        body, name=name, out_shape=tuple(jax.ShapeDtypeStruct(s, d) for s, d in out_shapes),
        in_specs=[vm] * n_in, out_specs=tuple([vm] * len(out_shapes)), compiler_params=_cparams())(*ins)


def _disc_a(lr, li, ldt):
    dt = jnp.exp(ldt)
    mag = jnp.exp(lr * dt)
    ab_re = mag * jnp.cos(li * dt)
    ab_im = mag * jnp.sin(li * dt)
    den = lr * lr + li * li
    nr = ab_re - 1.0
    coef_re = (nr * lr + ab_im * li) / den
    coef_im = (ab_im * lr - nr * li) / den
    return ab_re, ab_im, coef_re, coef_im


def _disc_a_powers(lr, li, ldt):
    ab_re, ab_im, coef_re, coef_im = _disc_a(lr, li, ldt)
    pr, pi = [ab_re], [ab_im]
    for _ in range(SUBLANES - 1):
        pr, pi = pr + [pr[-1] * ab_re - pi[-1] * ab_im], pi + [pr[-1] * ab_im + pi[-1] * ab_re]
    return (*pr, *pi, coef_re, coef_im)


def _disc_b(coef_re, coef_im, bt_re, bt_im):
    return coef_re * bt_re - coef_im * bt_im, coef_re * bt_im + coef_im * bt_re


def _lb_fn(logits):
    return jax.nn.softmax(logits, axis=0)[0:1]


def _adamw(w, g, m, v):
    m = ADAM_B1 * m + (1.0 - ADAM_B1) * g
    v = ADAM_B2 * v + (1.0 - ADAM_B2) * jnp.square(g)
    m_hat = m / (1.0 - ADAM_B1 ** ADAM_STEP)
    v_hat = v / (1.0 - ADAM_B2 ** ADAM_STEP)
    delta = -ADAM_LR * (m_hat / (jnp.sqrt(v_hat) + ADAM_EPS) + ADAM_WD * w)
    return delta, m, v


def _norm_call(h, g, tm, name):
    T, D = h.shape

    def body(h_ref, g_ref, z_ref):
        z_ref[...] = _rms(h_ref[...], g_ref[...]).astype(ACT)

    return pl.pallas_call(
        body, name=name, out_shape=jax.ShapeDtypeStruct((T, D), ACT), grid=(T // tm,),
        in_specs=[pl.BlockSpec((tm, D), lambda i: (i, 0)), pl.BlockSpec((1, D), lambda i: (0, 0))],
        out_specs=pl.BlockSpec((tm, D), lambda i: (i, 0)), compiler_params=_cparams())(h, g)


def _mm_shard(x, w, tm, name, major):
    T, K = x.shape
    S, _, N = w.shape

    def body(x_ref, w_ref, o_ref):
        o_ref[...] = _dot(x_ref[...], w_ref[...]).astype(o_ref.dtype)

    if major:
        out_shape = jax.ShapeDtypeStruct((S, T, N), ACT)
        out_spec = pl.BlockSpec((None, tm, N), lambda j, i: (j, i, 0))
    else:
        out_shape = jax.ShapeDtypeStruct((T, S * N), ACT)
        out_spec = pl.BlockSpec((tm, N), lambda j, i: (i, j))
    return pl.pallas_call(
        body, name=name, out_shape=out_shape, grid=(S, T // tm),
        in_specs=[pl.BlockSpec((tm, K), lambda j, i: (i, 0)), pl.BlockSpec((None, K, N), lambda j, i: (j, 0, 0))],
        out_specs=out_spec, compiler_params=_cparams())(x, w)


def _mm_tn(x, y, n_shards, tm, name, major):
    T, K = x.shape
    S = n_shards
    N = y.shape[-1] if major else y.shape[-1] // S

    def body(x_ref, y_ref, o_ref):
        @pl.when(pl.program_id(1) == 0)
        def _():
            o_ref[...] = jnp.zeros_like(o_ref)
        o_ref[...] += _dot_tn(x_ref[...], y_ref[...])

    y_spec = (pl.BlockSpec((None, tm, N), lambda j, i: (j, i, 0)) if major
              else pl.BlockSpec((tm, N), lambda j, i: (i, j)))
    return pl.pallas_call(
        body, name=name, out_shape=jax.ShapeDtypeStruct((S, K, N), F32), grid=(S, T // tm),
        in_specs=[pl.BlockSpec((tm, K), lambda j, i: (i, 0)), y_spec],
        out_specs=pl.BlockSpec((None, K, N), lambda j, i: (j, 0, 0)), compiler_params=_cparams())(x, y)


def _lin_bwd(x, dy, w, tm, name):
    T, K = x.shape
    N = dy.shape[1]

    def body(x_ref, dy_ref, w_ref, dx_ref, dw_ref):
        @pl.when(pl.program_id(0) == 0)
        def _():
            dw_ref[...] = jnp.zeros_like(dw_ref)
        dy = dy_ref[...]
        dx_ref[...] = _dot_nt(dy, w_ref[...]).astype(dx_ref.dtype)
        dw_ref[...] += _dot_tn(x_ref[...], dy)

    return pl.pallas_call(
        body, name=name,
        out_shape=(jax.ShapeDtypeStruct((T, K), ACT), jax.ShapeDtypeStruct((K, N), F32)), grid=(T // tm,),
        in_specs=[pl.BlockSpec((tm, K), lambda i: (i, 0)), pl.BlockSpec((tm, N), lambda i: (i, 0)),
                  pl.BlockSpec((K, N), lambda i: (0, 0))],
        out_specs=(pl.BlockSpec((tm, K), lambda i: (i, 0)), pl.BlockSpec((K, N), lambda i: (0, 0))),
        compiler_params=_cparams())(x, dy, w)


def _scan_slabs(x_ref, tab_ref, n_slabs, reverse):
    hw = x_ref.shape[1] // 2
    tabs = [tab_ref[s] for s in range(4)]

    def cmul(t, xr, xi):
        tr, ti = t[:, :hw], t[:, hw:]
        return tr * xr - ti * xi, tr * xi + ti * xr

    def step(k, carry):
        cr, ci = carry
        kk = (n_slabs - 1 - k) if reverse else k
        r0 = pl.multiple_of(kk * SUBLANES, SUBLANES)
        x = x_ref[pl.ds(r0, SUBLANES), :]
        xr, xi = x[:, :hw], x[:, hw:]
        for s, d in enumerate((1, 2, 4)):
            sh = (SUBLANES - d) if reverse else d
            ar, ai = cmul(tabs[s], pltpu.roll(xr, sh, 0), pltpu.roll(xi, sh, 0))
            xr, xi = xr + ar, xi + ai
        pr, pi = cmul(tabs[3], cr, ci)
        xr, xi = xr + pr, xi + pi
        x_ref[pl.ds(r0, SUBLANES), 0:hw] = xr
        x_ref[pl.ds(r0, SUBLANES), hw:2 * hw] = xi
        e = 0 if reverse else SUBLANES - 1
        return xr[e:e + 1], xi[e:e + 1]

    z = jnp.zeros((1, hw), F32)
    lax.fori_loop(0, n_slabs, step, (z, z))


def _s5_fwd_call(p3, wb, wc, tab_f, dsk, name):
    B, L, _ = p3.shape
    n_cb, cw, sw = wb.shape

    def body(u_ref, wb_ref, wc_ref, tab_ref, d_ref, ya_ref, s_ref):
        u = u_ref[...]
        s_ref[...] = _dot(u, wb_ref[...])
        _scan_slabs(s_ref, tab_ref, L // SUBLANES, False)
        y = _dot(s_ref[...], wc_ref[...]) + d_ref[...] * u.astype(F32)
        ya_ref[...] = jax.nn.gelu(y).astype(ACT)

    return pl.pallas_call(
        body, name=name, out_shape=jax.ShapeDtypeStruct((B, L, n_cb * cw), ACT), grid=(B, n_cb),
        in_specs=[pl.BlockSpec((None, L, cw), lambda b, c: (b, 0, c)),
                  pl.BlockSpec((None, cw, sw), lambda b, c: (c, 0, 0)),
                  pl.BlockSpec((None, sw, cw), lambda b, c: (c, 0, 0)),
                  pl.BlockSpec((None, 4, SUBLANES, sw), lambda b, c: (c, 0, 0, 0)),
                  pl.BlockSpec((None, 1, cw), lambda b, c: (c, 0, 0))],
        out_specs=pl.BlockSpec((None, L, cw), lambda b, c: (b, 0, c)),
        scratch_shapes=[pltpu.VMEM((L, sw), F32)], compiler_params=_cparams())(p3, wb, wc, tab_f, dsk)


def _s5_bwd_call(p3, dya, wb, wc, tab_f, tab_r, dsk, name):
    B, L, _ = p3.shape
    n_cb, cw, sw = wb.shape
    hw = sw // 2
    n_slabs = L // SUBLANES

    def body(u_ref, dya_ref, wb_ref, wc_ref, tf_ref, tr_ref, d_ref,
             du_ref, dwb_ref, dwc_ref, da_ref, dd_ref, s_ref, l_ref):
        @pl.when(pl.program_id(1) == 0)
        def _():
            dwb_ref[...] = jnp.zeros_like(dwb_ref)
            dwc_ref[...] = jnp.zeros_like(dwc_ref)
            da_ref[...] = jnp.zeros_like(da_ref)
            dd_ref[...] = jnp.zeros_like(dd_ref)

        u = u_ref[...]
        uf = u.astype(F32)
        s_ref[...] = _dot(u, wb_ref[...])
        _scan_slabs(s_ref, tf_ref, n_slabs, False)
        y = _dot(s_ref[...], wc_ref[...]) + d_ref[...] * uf
        _, gelu_vjp = jax.vjp(jax.nn.gelu, y)
        dy = gelu_vjp(dya_ref[...].astype(F32))[0]
        dd_ref[...] += jnp.sum(dy * uf, axis=0, keepdims=True)
        l_ref[...] = _dot_nt(dy, wc_ref[...])
        _scan_slabs(l_ref, tr_ref, n_slabs, True)
        du_ref[...] = (_dot_nt(l_ref[...], wb_ref[...]) + d_ref[...] * dy).astype(ACT)
        dwb_ref[...] += _dot_tn(u, l_ref[...])
        dwc_ref[...] += _dot_tn(s_ref[...], dy)

        row = lax.broadcasted_iota(jnp.int32, (SUBLANES, hw), 0)

        def step(k, carry):
            pr, pi, accr, acci = carry
            r0 = pl.multiple_of(k * SUBLANES, SUBLANES)
            s = s_ref[pl.ds(r0, SUBLANES), :]
            lam = l_ref[pl.ds(r0, SUBLANES), :]
            sr, si = s[:, :hw], s[:, hw:]
            lr, li = lam[:, :hw], lam[:, hw:]
            qr = jnp.where(row == 0, pr, pltpu.roll(sr, 1, 0))
            qi = jnp.where(row == 0, pi, pltpu.roll(si, 1, 0))
            accr = accr + lr * qr + li * qi
            acci = acci + li * qr - lr * qi
            return sr[SUBLANES - 1:], si[SUBLANES - 1:], accr, acci

        z1 = jnp.zeros((1, hw), F32)
        z8 = jnp.zeros((SUBLANES, hw), F32)
        _, _, accr, acci = lax.fori_loop(0, n_slabs, step, (z1, z1, z8, z8))
        da_ref[...] += jnp.concatenate([jnp.sum(accr, axis=0, keepdims=True),
                                        jnp.sum(acci, axis=0, keepdims=True)], axis=1)

    W = n_cb * cw
    return pl.pallas_call(
        body, name=name,
        out_shape=(jax.ShapeDtypeStruct((B, L, W), ACT), jax.ShapeDtypeStruct((n_cb, cw, sw), F32),
                   jax.ShapeDtypeStruct((n_cb, sw, cw), F32), jax.ShapeDtypeStruct((n_cb, 1, sw), F32),
                   jax.ShapeDtypeStruct((n_cb, 1, cw), F32)),
        grid=(n_cb, B),
        in_specs=[pl.BlockSpec((None, L, cw), lambda c, b: (b, 0, c)),
                  pl.BlockSpec((None, L, cw), lambda c, b: (b, 0, c)),
                  pl.BlockSpec((None, cw, sw), lambda c, b: (c, 0, 0)),
                  pl.BlockSpec((None, sw, cw), lambda c, b: (c, 0, 0)),
                  pl.BlockSpec((None, 4, SUBLANES, sw), lambda c, b: (c, 0, 0, 0)),
                  pl.BlockSpec((None, 4, SUBLANES, sw), lambda c, b: (c, 0, 0, 0)),
                  pl.BlockSpec((None, 1, cw), lambda c, b: (c, 0, 0))],
        out_specs=(pl.BlockSpec((None, L, cw), lambda c, b: (b, 0, c)),
                   pl.BlockSpec((None, cw, sw), lambda c, b: (c, 0, 0)),
                   pl.BlockSpec((None, sw, cw), lambda c, b: (c, 0, 0)),
                   pl.BlockSpec((None, 1, sw), lambda c, b: (c, 0, 0)),
                   pl.BlockSpec((None, 1, cw), lambda c, b: (c, 0, 0))),
        scratch_shapes=[pltpu.VMEM((L, sw), F32), pltpu.VMEM((L, sw), F32)],
        compiler_params=_cparams())(p3, dya, wb, wc, tab_f, tab_r, dsk)


def _glu_proj_call(ya, wglu, wproj, tm, name):
    T, W = ya.shape
    D = wproj.shape[1]

    def body(ya_ref, wg_ref, wp_ref, yo_ref, a_ref):
        ya = ya_ref[...]
        yo = ya.astype(F32) * jax.nn.sigmoid(_dot(ya, wg_ref[...]))
        yo_ref[...] = yo.astype(ACT)
        a_ref[...] = _dot(yo, wp_ref[...]).astype(ACT)

    return pl.pallas_call(
        body, name=name, out_shape=(jax.ShapeDtypeStruct((T, W), ACT), jax.ShapeDtypeStruct((T, D), ACT)),
        grid=(T // tm,),
        in_specs=[pl.BlockSpec((tm, W), lambda i: (i, 0)), pl.BlockSpec((W, W), lambda i: (0, 0)),
                  pl.BlockSpec((W, D), lambda i: (0, 0))],
        out_specs=(pl.BlockSpec((tm, W), lambda i: (i, 0)), pl.BlockSpec((tm, D), lambda i: (i, 0))),
        compiler_params=_cparams())(ya, wglu, wproj)


def _glu_bwd_call(ya, dyo, wglu, tm, name):
    T, W = ya.shape

    def body(ya_ref, dyo_ref, wg_ref, dya_ref, dwg_ref):
        @pl.when(pl.program_id(0) == 0)
        def _():
            dwg_ref[...] = jnp.zeros_like(dwg_ref)
        ya = ya_ref[...]
        yaf = ya.astype(F32)
        dyo = dyo_ref[...].astype(F32)
        sg = jax.nn.sigmoid(_dot(ya, wg_ref[...]))
        dt = dyo * yaf * sg * (1.0 - sg)
        dya_ref[...] = (dyo * sg + _dot_nt(dt, wg_ref[...])).astype(ACT)
        dwg_ref[...] += _dot_tn(ya, dt)

    return pl.pallas_call(
        body, name=name, out_shape=(jax.ShapeDtypeStruct((T, W), ACT), jax.ShapeDtypeStruct((W, W), F32)),
        grid=(T // tm,),
        in_specs=[pl.BlockSpec((tm, W), lambda i: (i, 0)), pl.BlockSpec((tm, W), lambda i: (i, 0)),
                  pl.BlockSpec((W, W), lambda i: (0, 0))],
        out_specs=(pl.BlockSpec((tm, W), lambda i: (i, 0)), pl.BlockSpec((W, W), lambda i: (0, 0))),
        compiler_params=_cparams())(ya, dyo, wglu)


PAD = 16


def _chunk_cumsums(x, pad_ref, L):
    row = lax.broadcasted_iota(jnp.int32, x.shape, 0) % CHUNK
    zeros = jnp.zeros((PAD, x.shape[1]), F32)
    pad_ref[0:PAD, :] = zeros
    pad_ref[PAD + L:2 * PAD + L, :] = zeros
    c = x
    r = x
    d = 1
    while d < CHUNK:
        pad_ref[PAD:PAD + L, :] = c
        c = c + jnp.where(row >= d, pad_ref[PAD - d:PAD - d + L, :], 0.0)
        pad_ref[PAD:PAD + L, :] = r
        r = r + jnp.where(row + d < CHUNK, pad_ref[PAD + d:PAD + d + L, :], 0.0)
        d *= 2
    return c, r - x


def _hgrn_prep(q_ref, fl_ref, lb_ref, pad_ref, L):
    lb = lb_ref[...]
    sig = jax.nn.sigmoid(fl_ref[...].astype(F32))
    f = lb + (1.0 - lb) * sig
    k = 1.0 - f
    c, rc = _chunk_cumsums(jnp.log(f), pad_ref, L)
    e_in, e_inv, e_out = jnp.exp(c), jnp.exp(-c), jnp.exp(rc)
    q = q_ref[...].astype(F32)
    return dict(sig=sig, f=f, k=k, q=q, e_in=e_in, e_inv=e_inv, e_out=e_out, dec=jnp.exp(c + rc))


def _chunk_mask(rb):
    r = lax.broadcasted_iota(jnp.int32, (rb, rb), 0)
    c = lax.broadcasted_iota(jnp.int32, (rb, rb), 1)
    return (r // CHUNK == c // CHUNK) & (c <= r)


def _hg_out(o, og, g):
    on = o * lax.rsqrt(jnp.mean(o * o, axis=-1, keepdims=True) + EPS) * g
    return on * _silu(og)


def _hgrn_specs(L, hd, col_q, n_heads, order):
    def spec(sec):
        return pl.BlockSpec((None, L, hd), lambda *g: (order(*g)[0], 0, col_q + sec * n_heads + order(*g)[1]))
    return [spec(0), spec(1), spec(2), spec(3)]


def _hgrn_fwd_call(p3, lb, ng, n_heads, col_q, name):
    B, L, _ = p3.shape
    hd = ng.shape[1]
    rb = _tile(L, ROW_TILE_CAP)
    n_chunks = L // CHUNK

    def body(q_ref, fl_ref, v_ref, og_ref, lb_ref, ng_ref, yb_ref,
             pad_ref, qin_ref, kin_ref, kout_ref, dec_ref, o_ref):
        pp = _hgrn_prep(q_ref, fl_ref, lb_ref, pad_ref, L)
        qin_ref[...] = (pp["q"] * pp["e_in"]).astype(MXU)
        kin_ref[...] = (pp["k"] * pp["e_inv"]).astype(MXU)
        kout_ref[...] = (pp["k"] * pp["e_out"]).astype(MXU)
        dec_ref[...] = pp["dec"]
        mask = _chunk_mask(rb)
        for blk in range(L // rb):
            rs = slice(blk * rb, (blk + 1) * rb)
            a = jnp.where(mask, _dot_nt(qin_ref[rs, :], kin_ref[rs, :]), 0.0)
            o_ref[rs, :] = _dot(a, v_ref[rs, :])

        def step(n, st):
            r0 = pl.multiple_of(n * CHUNK, CHUNK)
            o_ref[pl.ds(r0, CHUNK), :] += _dot_nt(qin_ref[pl.ds(r0, CHUNK), :], st)
            dec = dec_ref[pl.ds(r0, SUBLANES), :][0:1]
            return st * dec + _dot_tn(v_ref[pl.ds(r0, CHUNK), :], kout_ref[pl.ds(r0, CHUNK), :])

        lax.fori_loop(0, n_chunks, step, jnp.zeros((hd, hd), F32))
        yb_ref[...] = _hg_out(o_ref[...], og_ref[...].astype(F32), ng_ref[...]).astype(ACT)

    order = lambda b, h: (b, h)
    return pl.pallas_call(
        body, name=name, out_shape=jax.ShapeDtypeStruct((B, L, n_heads * hd), ACT), grid=(B, n_heads),
        in_specs=_hgrn_specs(L, hd, col_q, n_heads, order) + [
            pl.BlockSpec((1, hd), lambda b, h: (0, h)), pl.BlockSpec((1, hd), lambda b, h: (0, 0))],
        out_specs=pl.BlockSpec((None, L, hd), lambda b, h: (b, 0, h)),
        scratch_shapes=[pltpu.VMEM((L + 2 * PAD, hd), F32), pltpu.VMEM((L, hd), MXU), pltpu.VMEM((L, hd), MXU),
                        pltpu.VMEM((L, hd), MXU), pltpu.VMEM((L, hd), F32), pltpu.VMEM((L, hd), F32)],
        compiler_params=_cparams())(p3, p3, p3, p3, lb, ng)


def _hgrn_bwd_call(p3, dyb, lb, ng, n_heads, col_q, name):
    B, L, _ = p3.shape
    hd = ng.shape[1]
    rb = _tile(L, ROW_TILE_CAP)
    n_chunks = L // CHUNK

    def body(q_ref, fl_ref, v_ref, og_ref, dyb_ref, lb_ref, ng_ref,
             dq_ref, dfl_ref, dv_ref, dog_ref, dlb_ref, dng_ref,
             pad_ref, qin_ref, kin_ref, kout_ref, dec_ref, o_ref, do_ref, sall_ref,
             dqi_ref, dki_ref, dko_ref, dvv_ref, dct_ref):
        @pl.when(pl.program_id(1) == 0)
        def _():
            dlb_ref[...] = jnp.zeros_like(dlb_ref)

        @pl.when((pl.program_id(0) == 0) & (pl.program_id(1) == 0))
        def _():
            dng_ref[...] = jnp.zeros_like(dng_ref)

        pp = _hgrn_prep(q_ref, fl_ref, lb_ref, pad_ref, L)
        qin_ref[...] = (pp["q"] * pp["e_in"]).astype(MXU)
        kin_ref[...] = (pp["k"] * pp["e_inv"]).astype(MXU)
        kout_ref[...] = (pp["k"] * pp["e_out"]).astype(MXU)
        dec_ref[...] = pp["dec"]
        mask = _chunk_mask(rb)
        for blk in range(L // rb):
            rs = slice(blk * rb, (blk + 1) * rb)
            a = jnp.where(mask, _dot_nt(qin_ref[rs, :], kin_ref[rs, :]), 0.0)
            o_ref[rs, :] = _dot(a, v_ref[rs, :])

        def fwd_step(n, st):
            r0 = pl.multiple_of(n * CHUNK, CHUNK)
            sall_ref[n] = st
            o_ref[pl.ds(r0, CHUNK), :] += _dot_nt(qin_ref[pl.ds(r0, CHUNK), :], st)
            dec = dec_ref[pl.ds(r0, SUBLANES), :][0:1]
            return st * dec + _dot_tn(v_ref[pl.ds(r0, CHUNK), :], kout_ref[pl.ds(r0, CHUNK), :])

        lax.fori_loop(0, n_chunks, fwd_step, jnp.zeros((hd, hd), F32))

        og = og_ref[...].astype(F32)
        _, out_vjp = jax.vjp(_hg_out, o_ref[...], og, ng_ref[...])
        d_o, d_og, d_ng = out_vjp(dyb_ref[...].astype(F32))
        dog_ref[...] = d_og.astype(ACT)
        dng_ref[...] += d_ng
        do_ref[...] = d_o.astype(MXU)

        for blk in range(L // rb):
            rs = slice(blk * rb, (blk + 1) * rb)
            a = jnp.where(mask, _dot_nt(qin_ref[rs, :], kin_ref[rs, :]), 0.0)
            da = jnp.where(mask, _dot_nt(do_ref[rs, :], v_ref[rs, :]), 0.0)
            dqi_ref[rs, :] = _dot(da, kin_ref[rs, :])
            dki_ref[rs, :] = _dot_tn(da, qin_ref[rs, :])
            dvv_ref[rs, :] = _dot_tn(a, do_ref[rs, :])

        def dq_step(n, carry):
            r0 = pl.multiple_of(n * CHUNK, CHUNK)
            dqi_ref[pl.ds(r0, CHUNK), :] += _dot(do_ref[pl.ds(r0, CHUNK), :], sall_ref[n])
            return carry

        lax.fori_loop(0, n_chunks, dq_step, 0)

        def bwd_step(k, dst):
            n = n_chunks - 1 - k
            r0 = pl.multiple_of(n * CHUNK, CHUNK)
            dvv_ref[pl.ds(r0, CHUNK), :] += _dot_nt(kout_ref[pl.ds(r0, CHUNK), :], dst)
            dko_ref[pl.ds(r0, CHUNK), :] = _dot(v_ref[pl.ds(r0, CHUNK), :], dst)
            dec = dec_ref[pl.ds(r0, SUBLANES), :][0:1]
            ddec = dec * jnp.sum(dst * sall_ref[n], axis=0, keepdims=True)
            dct_ref[pl.ds(r0, CHUNK), :] = jnp.broadcast_to(ddec, (CHUNK, hd))
            return dst * dec + _dot_tn(do_ref[pl.ds(r0, CHUNK), :], qin_ref[pl.ds(r0, CHUNK), :])

        lax.fori_loop(0, n_chunks, bwd_step, jnp.zeros((hd, hd), F32))

        dqi, dki, dko = dqi_ref[...], dki_ref[...], dko_ref[...]
        dq = dqi * pp["e_in"]
        dk = dki * pp["e_inv"] + dko * pp["e_out"]
        dq_ref[...] = dq.astype(ACT)
        dv_ref[...] = dvv_ref[...].astype(ACT)
        t_out = pp["k"] * pp["e_out"] * dko
        dc = pp["q"] * pp["e_in"] * dqi - pp["k"] * pp["e_inv"] * dki - t_out
        _, dc_later = _chunk_cumsums(dc, pad_ref, L)
        t_incl, t_later = _chunk_cumsums(t_out, pad_ref, L)
        dlogf = dc + dc_later + t_incl + t_later + dct_ref[...]
        df = dlogf / pp["f"] - dk
        lbv = lb_ref[...]
        sig = pp["sig"]
        dfl_ref[...] = (df * (1.0 - lbv) * sig * (1.0 - sig)).astype(ACT)
        dlb_ref[...] += jnp.sum(df * (1.0 - sig), axis=0, keepdims=True)

    order = lambda h, b: (b, h)
    W = n_heads * hd
    act_out = jax.ShapeDtypeStruct((B, L, W), ACT)
    blk_out = pl.BlockSpec((None, L, hd), lambda h, b: (b, 0, h))
    return pl.pallas_call(
        body, name=name,
        out_shape=(act_out, act_out, act_out, act_out, jax.ShapeDtypeStruct((1, W), F32),
                   jax.ShapeDtypeStruct((1, hd), F32)),
        grid=(n_heads, B),
        in_specs=_hgrn_specs(L, hd, col_q, n_heads, order) + [
            pl.BlockSpec((None, L, hd), lambda h, b: (b, 0, h)),
            pl.BlockSpec((1, hd), lambda h, b: (0, h)), pl.BlockSpec((1, hd), lambda h, b: (0, 0))],
        out_specs=(blk_out, blk_out, blk_out, blk_out, pl.BlockSpec((1, hd), lambda h, b: (0, h)),
                   pl.BlockSpec((1, hd), lambda h, b: (0, 0))),
        scratch_shapes=[pltpu.VMEM((L + 2 * PAD, hd), F32), pltpu.VMEM((L, hd), MXU), pltpu.VMEM((L, hd), MXU),
                        pltpu.VMEM((L, hd), MXU), pltpu.VMEM((L, hd), F32), pltpu.VMEM((L, hd), F32),
                        pltpu.VMEM((L, hd), MXU), pltpu.VMEM((n_chunks, hd, hd), F32),
                        pltpu.VMEM((L, hd), F32), pltpu.VMEM((L, hd), F32), pltpu.VMEM((L, hd), F32),
                        pltpu.VMEM((L, hd), F32), pltpu.VMEM((L, hd), F32)],
        compiler_params=_cparams())(p3, p3, p3, p3, dyb, lb, ng)


def _merge_fn(a, bm, ga, gb):
    return jax.nn.sigmoid(ga) * a + jax.nn.sigmoid(gb) * bm


def _merge_call(yb, a, p, h0, whp, wout, g2, col_ga, tm, name):
    T, D = h0.shape

    def body(yb_ref, a_ref, ga_ref, gb_ref, h0_ref, whp_ref, wout_ref, g2_ref, h1_ref, mg_ref, bm_ref, z2_ref):
        bm = _dot(yb_ref[...], whp_ref[...])
        mg = _merge_fn(a_ref[...].astype(F32), bm, ga_ref[...].astype(F32), gb_ref[...].astype(F32))
        h1 = h0_ref[...] + _dot(mg, wout_ref[...])
        h1_ref[...] = h1
        mg_ref[...] = mg.astype(ACT)
        bm_ref[...] = bm.astype(ACT)
        z2_ref[...] = _rms(h1, g2_ref[...]).astype(ACT)

    tile = pl.BlockSpec((tm, D), lambda i: (i, 0))
    full = pl.BlockSpec((D, D), lambda i: (0, 0))
    act = jax.ShapeDtypeStruct((T, D), ACT)
    return pl.pallas_call(
        body, name=name, out_shape=(jax.ShapeDtypeStruct((T, D), F32), act, act, act), grid=(T // tm,),
        in_specs=[tile, tile, pl.BlockSpec((tm, D), lambda i: (i, col_ga)),
                  pl.BlockSpec((tm, D), lambda i: (i, col_ga + 1)), tile, full, full,
                  pl.BlockSpec((1, D), lambda i: (0, 0))],
        out_specs=(tile, tile, tile, tile), compiler_params=_cparams())(yb, a, p, p, h0, whp, wout, g2)


def _merge_bwd_call(dmg, a, bm, p, col_ga, tm, name):
    T, D = dmg.shape

    def body(dmg_ref, a_ref, bm_ref, ga_ref, gb_ref, da_ref, dbm_ref, dga_ref, dgb_ref):
        args = [r[...].astype(F32) for r in (a_ref, bm_ref, ga_ref, gb_ref)]
        _, vjp = jax.vjp(_merge_fn, *args)
        for r, o in zip((da_ref, dbm_ref, dga_ref, dgb_ref), vjp(dmg_ref[...].astype(F32))):
            r[...] = o.astype(ACT)

    tile = pl.BlockSpec((tm, D), lambda i: (i, 0))
    act = jax.ShapeDtypeStruct((T, D), ACT)
    return pl.pallas_call(
        body, name=name, out_shape=(act, act, act, act), grid=(T // tm,),
        in_specs=[tile, tile, tile, pl.BlockSpec((tm, D), lambda i: (i, col_ga)),
                  pl.BlockSpec((tm, D), lambda i: (i, col_ga + 1))],
        out_specs=(tile, tile, tile, tile), compiler_params=_cparams())(dmg, a, bm, p, p)


def _conv_taps(x_ref, halo_ref, ext_ref, edge, tm, before):
    halo = jnp.where(edge, 0.0, halo_ref[...].astype(F32))
    x = x_ref[...].astype(F32)
    if before:
        ext_ref[0:PAD, :] = halo
        ext_ref[PAD:PAD + tm, :] = x
        return [ext_ref[PAD - 2 + k:PAD - 2 + k + tm, :] for k in range(3)]
    ext_ref[0:tm, :] = x
    ext_ref[tm:tm + PAD, :] = halo
    return [ext_ref[k:k + tm, :] for k in range(3)]


def _conv(taps, cw, cb):
    return cb + cw[0:1] * taps[0] + cw[1:2] * taps[1] + cw[2:3] * taps[2]


def _ffn_pair_specs(tm, F, T, n_pairs, order, before):
    hb = tm // PAD
    last = T // PAD - 1

    def halo_row(i):
        return jnp.maximum(i * hb - 1, 0) if before else jnp.minimum((i + 1) * hb, last)

    specs = []
    for off in (0, n_pairs):
        specs.append(pl.BlockSpec((None, tm, F), lambda *g, off=off: (order(*g)[1] + off, order(*g)[0], 0)))
        specs.append(pl.BlockSpec((None, PAD, F), lambda *g, off=off: (order(*g)[1] + off, halo_row(order(*g)[0]), 0)))
    return specs


def _ffn_fwd_call(up, cw, cb, wd, h1, tgt, g3, tm, tps, name):
    S, T, F = up.shape
    n_pairs = S // 2
    D = h1.shape[1]

    def body(ua_ref, ha_ref, ub_ref, hb_ref, cwa_ref, cwb_ref, cba_ref, cbb_ref, wd_ref, h1_ref, tgt_ref, g3_ref,
             act_ref, dh2_ref, loss_ref, dg3_ref, acc_ref, ext_ref):
        i, j = pl.program_id(0), pl.program_id(1)
        edge = (i % tps) == 0
        ua = _conv(_conv_taps(ua_ref, ha_ref, ext_ref, edge, tm, True), cwa_ref[...], cba_ref[...])
        ub = _conv(_conv_taps(ub_ref, hb_ref, ext_ref, edge, tm, True), cwb_ref[...], cbb_ref[...])
        act = _silu(ua) * ub
        act_ref[...] = act.astype(ACT)
        contrib = _dot(act, wd_ref[...])

        @pl.when(j == 0)
        def _():
            acc_ref[...] = h1_ref[...] + contrib

        @pl.when(j > 0)
        def _():
            acc_ref[...] += contrib

        @pl.when((i == 0) & (j == 0))
        def _():
            loss_ref[...] = jnp.zeros_like(loss_ref)
            dg3_ref[...] = jnp.zeros_like(dg3_ref)

        @pl.when(j == n_pairs - 1)
        def _():
            row = lax.broadcasted_iota(jnp.int32, (tm, 1), 0) + (i % tps) * tm
            valid = row >= N_META
            tgt = tgt_ref[...]

            def loss_fn(h2, g):
                err = _rms(h2, g) - tgt
                return 0.5 * jnp.sum(jnp.where(valid, err * err, 0.0)) / D

            loss, vjp = jax.vjp(loss_fn, acc_ref[...], g3_ref[...])
            dh2, dg3 = vjp(jnp.ones((), F32))
            dh2_ref[...] = dh2
            loss_ref[...] += loss
            dg3_ref[...] += dg3

    order = lambda i, j: (i, j)
    tile = pl.BlockSpec((tm, D), lambda i, j: (i, 0))
    vec = pl.BlockSpec((1, D), lambda i, j: (0, 0))
    return pl.pallas_call(
        body, name=name,
        out_shape=(jax.ShapeDtypeStruct((n_pairs, T, F), ACT), jax.ShapeDtypeStruct((T, D), F32),
                   jax.ShapeDtypeStruct((1, LANES), F32), jax.ShapeDtypeStruct((1, D), F32)),
        grid=(T // tm, n_pairs),
        in_specs=_ffn_pair_specs(tm, F, T, n_pairs, order, True) + [
            pl.BlockSpec((None, 3, F), lambda i, j: (j, 0, 0)), pl.BlockSpec((None, 3, F), lambda i, j: (j + n_pairs, 0, 0)),
            pl.BlockSpec((None, 1, F), lambda i, j: (j, 0, 0)), pl.BlockSpec((None, 1, F), lambda i, j: (j + n_pairs, 0, 0)),
            pl.BlockSpec((None, F, D), lambda i, j: (j, 0, 0)), tile, tile, vec],
        out_specs=(pl.BlockSpec((None, tm, F), lambda i, j: (j, i, 0)), tile,
                   pl.BlockSpec((1, LANES), lambda i, j: (0, 0)), vec),
        scratch_shapes=[pltpu.VMEM((tm, D), F32), pltpu.VMEM((tm + PAD, F), F32)],
        compiler_params=_cparams())(up, up, up, up, cw, cw, cb, cb, wd, h1, tgt, g3)


def _ffn_bwd_a_call(dh2, up, act, cw, cb, wd, tm, tps, name):
    S, T, F = up.shape
    n_pairs = S // 2
    D = dh2.shape[1]

    def body(dh2_ref, ua_ref, ha_ref, ub_ref, hb_ref, act_ref, cwa_ref, cwb_ref, cba_ref, cbb_ref, wd_ref,
             dua_ref, dub_ref, dwd_ref, dcwa_ref, dcwb_ref, dcba_ref, dcbb_ref, ext_ref):
        i = pl.program_id(1)
        edge = (i % tps) == 0

        @pl.when(i == 0)
        def _():
            for r in (dwd_ref, dcwa_ref, dcwb_ref, dcba_ref, dcbb_ref):
                r[...] = jnp.zeros_like(r)

        dh2 = dh2_ref[...]
        dact = _dot_nt(dh2, wd_ref[...])
        dwd_ref[...] += _dot_tn(act_ref[...], dh2)
        taps_a = _conv_taps(ua_ref, ha_ref, ext_ref, edge, tm, True)
        ua = _conv(taps_a, cwa_ref[...], cba_ref[...])
        sa = jax.nn.sigmoid(ua)
        dub = dact * ua * sa
        dcbb_ref[...] += jnp.sum(dub, axis=0, keepdims=True)
        taps_b = _conv_taps(ub_ref, hb_ref, ext_ref, edge, tm, True)
        dcwb_ref[...] += jnp.concatenate([jnp.sum(dub * t, axis=0, keepdims=True) for t in taps_b], axis=0)
        ub = _conv(taps_b, cwb_ref[...], cbb_ref[...])
        dua = dact * ub * sa * (1.0 + ua * (1.0 - sa))
        dcba_ref[...] += jnp.sum(dua, axis=0, keepdims=True)
        taps_a = _conv_taps(ua_ref, ha_ref, ext_ref, edge, tm, True)
        dcwa_ref[...] += jnp.concatenate([jnp.sum(dua * t, axis=0, keepdims=True) for t in taps_a], axis=0)
        dua_ref[...] = dua.astype(ACT)
        dub_ref[...] = dub.astype(ACT)

    order = lambda j, i: (i, j)
    sh = lambda rows: jax.ShapeDtypeStruct((n_pairs, rows, F), F32)
    par = lambda rows: pl.BlockSpec((None, rows, F), lambda j, i: (j, 0, 0))
    return pl.pallas_call(
        body, name=name,
        out_shape=(jax.ShapeDtypeStruct((n_pairs, T, F), ACT), jax.ShapeDtypeStruct((n_pairs, T, F), ACT),
                   jax.ShapeDtypeStruct((n_pairs, F, D), F32), sh(3), sh(3), sh(1), sh(1)),
        grid=(n_pairs, T // tm),
        in_specs=[pl.BlockSpec((tm, D), lambda j, i: (i, 0))] + _ffn_pair_specs(tm, F, T, n_pairs, order, True) + [
            pl.BlockSpec((None, tm, F), lambda j, i: (j, i, 0)),
            pl.BlockSpec((None, 3, F), lambda j, i: (j, 0, 0)), pl.BlockSpec((None, 3, F), lambda j, i: (j + n_pairs, 0, 0)),
            pl.BlockSpec((None, 1, F), lambda j, i: (j, 0, 0)), pl.BlockSpec((None, 1, F), lambda j, i: (j + n_pairs, 0, 0)),
            pl.BlockSpec((None, F, D), lambda j, i: (j, 0, 0))],
        out_specs=(pl.BlockSpec((None, tm, F), lambda j, i: (j, i, 0)), pl.BlockSpec((None, tm, F), lambda j, i: (j, i, 0)),
                   pl.BlockSpec((None, F, D), lambda j, i: (j, 0, 0)), par(3), par(3), par(1), par(1)),
        scratch_shapes=[pltpu.VMEM((tm + PAD, F), F32)],
        compiler_params=_cparams())(dh2, up, up, up, up, act, cw, cw, cb, cb, wd)


def _ffn_bwd_b_call(dua, dub, cw, wup, h1, g2, dh2, tm, tps, name):
    n_pairs, T, F = dua.shape
    D = h1.shape[1]
    hb = tm // PAD
    last = T // PAD - 1

    def body(da_ref, na_ref, db_ref, nb_ref, cwa_ref, cwb_ref, wa_ref, wb_ref, h1_ref, g2_ref, dh2_ref,
             dupa_ref, dupb_ref, dh1_ref, dg2_ref, acc_ref, ext_ref):
        i, j = pl.program_id(0), pl.program_id(1)
        edge = (i % tps) == tps - 1
        outs = []
        for d_ref, n_ref, cw_ref, o_ref in ((da_ref, na_ref, cwa_ref, dupa_ref), (db_ref, nb_ref, cwb_ref, dupb_ref)):
            t = _conv_taps(d_ref, n_ref, ext_ref, edge, tm, False)
            cwv = cw_ref[...]
            dup = cwv[2:3] * t[0] + cwv[1:2] * t[1] + cwv[0:1] * t[2]
            o_ref[...] = dup.astype(ACT)
            outs.append(dup)
        contrib = _dot_nt(outs[0], wa_ref[...]) + _dot_nt(outs[1], wb_ref[...])

        @pl.when(j == 0)
        def _():
            acc_ref[...] = contrib

        @pl.when(j > 0)
        def _():
            acc_ref[...] += contrib

        @pl.when((i == 0) & (j == 0))
        def _():
            dg2_ref[...] = jnp.zeros_like(dg2_ref)

        @pl.when(j == n_pairs - 1)
        def _():
            _, vjp = jax.vjp(_rms, h1_ref[...], g2_ref[...])
            dh, dg = vjp(acc_ref[...])
            dh1_ref[...] = dh2_ref[...] + dh
            dg2_ref[...] += dg

    tile = pl.BlockSpec((tm, D), lambda i, j: (i, 0))
    vec = pl.BlockSpec((1, D), lambda i, j: (0, 0))
    pair = lambda: [pl.BlockSpec((None, tm, F), lambda i, j: (j, i, 0)),
                    pl.BlockSpec((None, PAD, F), lambda i, j: (j, jnp.minimum((i + 1) * hb, last), 0))]
    act = jax.ShapeDtypeStruct((n_pairs, T, F), ACT)
    return pl.pallas_call(
        body, name=name,
        out_shape=(act, act, jax.ShapeDtypeStruct((T, D), F32), jax.ShapeDtypeStruct((1, D), F32)),
        grid=(T // tm, n_pairs),
        in_specs=pair() + pair() + [
            pl.BlockSpec((None, 3, F), lambda i, j: (j, 0, 0)), pl.BlockSpec((None, 3, F), lambda i, j: (j + n_pairs, 0, 0)),
            pl.BlockSpec((None, D, F), lambda i, j: (j, 0, 0)), pl.BlockSpec((None, D, F), lambda i, j: (j + n_pairs, 0, 0)),
            tile, vec, tile],
        out_specs=(pl.BlockSpec((None, tm, F), lambda i, j: (j, i, 0)), pl.BlockSpec((None, tm, F), lambda i, j: (j, i, 0)),
                   tile, vec),
        scratch_shapes=[pltpu.VMEM((tm, D), F32), pltpu.VMEM((tm + PAD, F), F32)],
        compiler_params=_cparams())(dua, dua, dub, dub, cw, cw, wup, wup, h1, g2, dh2)


def _in_bwd_call(dp, w_in, h0, g1, dh1, tm, name):
    T, D = h0.shape
    S, _, N = w_in.shape

    def body(dp_ref, w_ref, h0_ref, g1_ref, dh1_ref, dh0_ref, dg1_ref, acc_ref):
        i, j = pl.program_id(0), pl.program_id(1)
        contrib = _dot_nt(dp_ref[...], w_ref[...])

        @pl.when(j == 0)
        def _():
            acc_ref[...] = contrib

        @pl.when(j > 0)
        def _():
            acc_ref[...] += contrib

        @pl.when((i == 0) & (j == 0))
        def _():
            dg1_ref[...] = jnp.zeros_like(dg1_ref)

        @pl.when(j == S - 1)
        def _():
            _, vjp = jax.vjp(_rms, h0_ref[...], g1_ref[...])
            dh, dg = vjp(acc_ref[...])
            dh0_ref[...] = dh1_ref[...] + dh
            dg1_ref[...] += dg

    tile = pl.BlockSpec((tm, D), lambda i, j: (i, 0))
    vec = pl.BlockSpec((1, D), lambda i, j: (0, 0))
    return pl.pallas_call(
        body, name=name, out_shape=(jax.ShapeDtypeStruct((T, D), F32), jax.ShapeDtypeStruct((1, D), F32)),
        grid=(T // tm, S),
        in_specs=[pl.BlockSpec((tm, N), lambda i, j: (i, j)), pl.BlockSpec((None, D, N), lambda i, j: (j, 0, 0)),
                  tile, vec, tile],
        out_specs=(tile, vec), scratch_shapes=[pltpu.VMEM((tm, D), F32)],
        compiler_params=_cparams())(dp, w_in, h0, g1, dh1)


def _meta_grad_call(dh0_3, name):
    B, L, D = dh0_3.shape

    def body(d_ref, o_ref):
        o_ref[...] = jnp.sum(d_ref[...], axis=0)

    return pl.pallas_call(
        body, name=name, out_shape=jax.ShapeDtypeStruct((N_META, D), F32), grid=(1,),
        in_specs=[pl.BlockSpec((B, N_META, D), lambda i: (0, 0, 0))],
        out_specs=pl.BlockSpec((N_META, D), lambda i: (0, 0)), compiler_params=_cparams())(dh0_3)


_RELS = [(dx, dy, dc) for dx in (0, 1) for dy in (0, 1) for dc in (0, 1)][1:]


def _exchange_call(arrs, scatter, name):
    n = len(arrs)
    n_rel = len(_RELS)

    def body(*refs):
        ins, outs = refs[:n], refs[n:2 * n]
        send_sems, recv_sems, loc_sems = refs[2 * n:]
        x, y, c = lax.axis_index("x"), lax.axis_index("y"), lax.axis_index("c")
        me = 4 * x + 2 * y + c
        started = []
        for k in range(n):
            src_me = ins[k].at[me] if scatter else ins[k]
            loc = pltpu.make_async_copy(src_me, outs[k].at[me], loc_sems.at[k])
            loc.start()
            started.append(loc)
        waits = []
        for r, (dx, dy, dc) in enumerate(_RELS):
            px, py, pc = (x + dx) % 2, (y + dy) % 2, (c + dc) % 2
            pid = 4 * px + 2 * py + pc
            for k in range(n):
                s = k * n_rel + r
                src = ins[k].at[pid] if scatter else ins[k]
                cp = pltpu.make_async_remote_copy(
                    src_ref=src, dst_ref=outs[k].at[me], send_sem=send_sems.at[s], recv_sem=recv_sems.at[s],
                    device_id=(px, py, pc), device_id_type=pl.DeviceIdType.MESH)
                cp.start()
                waits.append(pltpu.make_async_remote_copy(
                    src_ref=src, dst_ref=outs[k].at[pid], send_sem=send_sems.at[s], recv_sem=recv_sems.at[s],
                    device_id=(px, py, pc), device_id_type=pl.DeviceIdType.MESH))
        for w in waits:
            w.wait_send()
            w.wait_recv()
        for loc in started:
            loc.wait()

    out_shape = tuple(jax.ShapeDtypeStruct(a.shape if scatter else (N_DEV,) + a.shape, a.dtype) for a in arrs)
    hbm = pl.BlockSpec(memory_space=pl.ANY)
    return pl.pallas_call(
        body, name=name, out_shape=out_shape, in_specs=[hbm] * n, out_specs=tuple([hbm] * n),
        scratch_shapes=[pltpu.SemaphoreType.DMA((n * n_rel,)), pltpu.SemaphoreType.DMA((n * n_rel,)),
                        pltpu.SemaphoreType.DMA((n,))],
        compiler_params=pltpu.CompilerParams(has_side_effects=True))(*arrs)


def _adamw_shard_call(w, parts, m, v, name):
    R, C = w.shape
    tr = _tile(R, 128) if R % 16 == 0 else R

    def body(w_ref, p_ref, m_ref, v_ref, g_ref, d_ref, nm_ref, nv_ref):
        g = p_ref[0]
        for s in range(1, N_DEV):
            g = g + p_ref[s]
        d, nm, nv = _adamw(w_ref[...], g, m_ref[...], v_ref[...])
        g_ref[...] = g
        d_ref[...] = d
        nm_ref[...] = nm
        nv_ref[...] = nv

    tile = pl.BlockSpec((tr, C), lambda i: (i, 0))
    sh = jax.ShapeDtypeStruct((R, C), F32)
    return pl.pallas_call(
        body, name=name, out_shape=(sh, sh, sh, sh), grid=(R // tr,),
        in_specs=[tile, pl.BlockSpec((N_DEV, tr, C), lambda i: (0, i, 0)), tile, tile],
        out_specs=(tile, tile, tile, tile), compiler_params=_cparams())(w, parts, m, v)


def _pack(arrs, rows_mult=SUBLANES):
    flat = jnp.concatenate([a.reshape(-1).astype(F32) for a in arrs])
    n = flat.shape[0]
    per = rows_mult * LANES
    total = -(-n // per) * per
    return jnp.pad(flat, (0, total - n)).reshape(total // LANES, LANES)


def _unpack(pack, shapes):
    flat = pack.reshape(-1)
    out, off = [], 0
    for s in shapes:
        n = 1
        for d in s:
            n *= d
        out.append(flat[off:off + n].reshape(s))
        off += n
    return out


def kernel(x, meta_tokens, mix_norm_g, w_in, ssm_lambda_re, ssm_lambda_im, ssm_log_dt, ssm_b_re, ssm_b_im, ssm_c_re, ssm_c_im, ssm_d, ssm_w_glu, w_ssm_proj, hgrn_lb_logits, hgrn_norm_g, w_hgrn_proj, w_out, ffn_norm_g, w_up, conv_w, conv_b, w_down, final_norm_g, loss_target, m_meta_tokens, m_mix_norm_g, m_w_in, m_ssm_lambda_re, m_ssm_lambda_im, m_ssm_log_dt, m_ssm_b_re, m_ssm_b_im, m_ssm_c_re, m_ssm_c_im, m_ssm_d, m_ssm_w_glu, m_w_ssm_proj, m_hgrn_lb_logits, m_hgrn_norm_g, m_w_hgrn_proj, m_w_out, m_ffn_norm_g, m_w_up, m_conv_w, m_conv_b, m_w_down, m_final_norm_g, v_meta_tokens, v_mix_norm_g, v_w_in, v_ssm_lambda_re, v_ssm_lambda_im, v_ssm_log_dt, v_ssm_b_re, v_ssm_b_im, v_ssm_c_re, v_ssm_c_im, v_ssm_d, v_ssm_w_glu, v_w_ssm_proj, v_hgrn_lb_logits, v_hgrn_norm_g, v_w_hgrn_proj, v_w_out, v_ffn_norm_g, v_w_up, v_conv_w, v_conv_b, v_w_down, v_final_norm_g):
    args = dict(locals())
    B, S_len, D = x.shape
    L = S_len + N_META
    T = B * L
    tm = _tile(L, ROW_TILE_CAP)
    tps = L // tm
    G, P = ssm_lambda_re.shape[1:]
    H = ssm_b_re.shape[-1]
    W = G * H
    n_cb = W // LANES
    gpb = G // n_cb
    hd = hgrn_norm_g.shape[1]
    n_heads = D // hd
    n_in = w_in.shape[2]
    F = w_up.shape[2]
    assert W == D and n_in % LANES == 0

    gathered = _exchange_call(
        [w_in[0].astype(MXU), w_up[0].astype(MXU), ssm_w_glu[0].astype(MXU), w_ssm_proj[0].astype(MXU),
         w_hgrn_proj[0].astype(MXU), w_out[0].astype(MXU), w_down[0].astype(MXU), meta_tokens, conv_w[0]],
        False, "gather_weights")
    win_g, wup_g = gathered[0], gathered[1]
    wglu_g, wsp_g, whp_g, wout_g = [g.reshape(D, D) for g in gathered[2:6]]
    wdn_g = gathered[6].reshape(N_DEV // 2, 2 * w_down.shape[1], D)
    meta_full = gathered[7].transpose(1, 0, 2).reshape(N_META, D)
    cw_g = gathered[8]
    cb_g = conv_b.reshape(N_DEV, 1, F)

    h0 = jnp.concatenate([jnp.broadcast_to(meta_full[None], (B, N_META, D)), x], axis=1).reshape(T, D)
    tgt = jnp.concatenate([jnp.zeros((B, N_META, D), F32), loss_target], axis=1).reshape(T, D)

    lr, li = ssm_lambda_re[0], ssm_lambda_im[0]
    ldt = ssm_log_dt[0].reshape(G, 1)
    bt_re = ssm_b_re[0].transpose(2, 0, 1).reshape(H, G * P)
    bt_im = ssm_b_im[0].transpose(2, 0, 1).reshape(H, G * P)
    disc = _small_call(_disc_a_powers, [lr, li, ldt], [((G, P), F32)] * (2 * SUBLANES + 2), "s5_discretise")
    pw_re, pw_im = jnp.stack(disc[:SUBLANES]), jnp.stack(disc[SUBLANES:2 * SUBLANES])
    coef_re, coef_im = disc[2 * SUBLANES:]
    bbt_re, bbt_im = _small_call(
        _disc_b, [coef_re.reshape(1, G * P), coef_im.reshape(1, G * P), bt_re, bt_im],
        [((H, G * P), F32)] * 2, "s5_input_matrix")
    eye = jnp.eye(gpb, dtype=F32)
    hw = gpb * P

    def expand_b(bbt):
        return jnp.einsum("hcgp,Gg->cGhgp", bbt.reshape(H, n_cb, gpb, P), eye).reshape(n_cb, gpb * H, hw)

    def expand_c(cm):
        return jnp.einsum("cghp,gG->cgpGh", cm.reshape(n_cb, gpb, H, P), eye).reshape(n_cb, hw, gpb * H)

    wb = jnp.concatenate([expand_b(bbt_re), expand_b(bbt_im)], axis=2).astype(MXU)
    wc = jnp.concatenate([expand_c(ssm_c_re[0]), -expand_c(ssm_c_im[0])], axis=1).astype(MXU)
    pwr = pw_re.reshape(SUBLANES, n_cb, hw).transpose(1, 0, 2)
    pwi = pw_im.reshape(SUBLANES, n_cb, hw).transpose(1, 0, 2)
    rows = jnp.arange(SUBLANES)[None, :, None]

    def table(sign, reverse):
        tabs = []
        for d in (1, 2, 4):
            keep = (rows + d < SUBLANES) if reverse else (rows >= d)
            tabs.append(jnp.concatenate([jnp.where(keep, pwr[:, d - 1:d], 0.0),
                                         jnp.where(keep, sign * pwi[:, d - 1:d], 0.0)], axis=2))
        cr, ci = (pwr[:, ::-1], pwi[:, ::-1]) if reverse else (pwr, pwi)
        tabs.append(jnp.concatenate([cr, sign * ci], axis=2))
        return jnp.stack(tabs, axis=1)

    tab_f, tab_r = table(1.0, False), table(-1.0, True)
    dsk = ssm_d.reshape(n_cb, 1, LANES)
    lb = _small_call(_lb_fn, [hgrn_lb_logits], [((1, D), F32)], "hgrn_lower_bound")[0]

    z1 = _norm_call(h0, mix_norm_g, tm, "mix_norm")
    p = _mm_shard(z1, win_g, tm, "in_proj", False)
    p3 = p.reshape(B, L, p.shape[1])
    ya = _s5_fwd_call(p3, wb, wc, tab_f, dsk, "s5_fwd").reshape(T, W)
    yo, a_br = _glu_proj_call(ya, wglu_g, wsp_g, tm, "s5_glu_proj")
    yb = _hgrn_fwd_call(p3, lb, hgrn_norm_g, n_heads, n_cb, "hgrn_fwd").reshape(T, D)
    col_ga = 5
    h1, mg, bm, z2 = _merge_call(yb, a_br, p, h0, whp_g, wout_g, ffn_norm_g, col_ga, tm, "merge")
    up = _mm_shard(z2, wup_g, tm, "up_proj", True)
    act, dh2, loss_part, dg3 = _ffn_fwd_call(up, cw_g, cb_g, wdn_g, h1, tgt, final_norm_g.reshape(1, D),
                                             tm, tps, "ffn_out_loss")

    dua, dub, dwd, dcwa, dcwb, dcba, dcbb = _ffn_bwd_a_call(dh2, up, act, cw_g, cb_g, wdn_g, tm, tps, "ffn_bwd_gate")
    dupa, dupb, dh1, dg2 = _ffn_bwd_b_call(dua, dub, cw_g, wup_g, h1, ffn_norm_g, dh2, tm, tps, "ffn_bwd_up")
    dwup = jnp.concatenate([_mm_tn(z2, dupa, N_DEV // 2, tm, "dw_up_a", True),
                            _mm_tn(z2, dupb, N_DEV // 2, tm, "dw_up_b", True)], axis=0)
    dmg, dwout = _lin_bwd(mg, dh1, wout_g, tm, "out_proj_bwd")
    da_br, dbm, dga, dgb = _merge_bwd_call(dmg, a_br, bm, p, col_ga, tm, "merge_bwd")
    dyo, dwsp = _lin_bwd(yo, da_br, wsp_g, tm, "ssm_proj_bwd")
    dyb, dwhp = _lin_bwd(yb, dbm, whp_g, tm, "hgrn_proj_bwd")
    dya, dwglu = _glu_bwd_call(ya, dyo, wglu_g, tm, "s5_glu_bwd")
    du, dwb, dwc, dab, ddsk = _s5_bwd_call(p3, dya.reshape(B, L, W), wb, wc, tab_f, tab_r, dsk, "s5_bwd")
    dq, dfl, di, dog, dlb, dng = _hgrn_bwd_call(p3, dyb.reshape(B, L, D), lb, hgrn_norm_g, n_heads, n_cb, "hgrn_bwd")
    dp = jnp.concatenate([du.reshape(T, W), dq.reshape(T, D), dfl.reshape(T, D), di.reshape(T, D),
                          dog.reshape(T, D), dga, dgb], axis=1)
    dh0, dg1 = _in_bwd_call(dp, win_g, h0, mix_norm_g, dh1, tm, "in_proj_bwd")
    dwin = _mm_tn(z1, dp, N_DEV, tm, "dw_in", False)
    dh0_3 = dh0.reshape(B, L, D)
    grad_x = dh0_3[:, N_META:]
    dmeta = _meta_grad_call(dh0_3, "meta_grad")

    def diag_b(dw):
        return jnp.einsum("cGhgp,Gg->hcgp", dw.reshape(n_cb, gpb, H, gpb, P), eye).reshape(H, G * P)

    def diag_c(dw):
        return jnp.einsum("cgpGh,gG->cghp", dw.reshape(n_cb, gpb, P, gpb, H), eye).reshape(G, H, P)

    small_parts = [dg1, dab[:, 0, :hw].reshape(G, P), dab[:, 0, hw:].reshape(G, P),
                   diag_b(dwb[:, :, :hw]), diag_b(dwb[:, :, hw:]),
                   diag_c(dwc[:, :hw]), -diag_c(dwc[:, hw:]), ddsk.reshape(1, D), dlb,
                   dng, dg2,
                   jnp.concatenate([dcba, dcbb], axis=0).reshape(1, N_DEV * F), dg3, loss_part]
    small_shapes = [a.shape for a in small_parts]
    small_pack = _pack(small_parts)

    dcw = jnp.concatenate([dcwa, dcwb], axis=0)
    dmeta_s = dmeta.reshape(N_META, N_DEV, D // N_DEV).transpose(1, 0, 2)
    sh_rows = D // N_DEV
    parts = _exchange_call(
        [dwin, dwup, dwglu.reshape(N_DEV, sh_rows, D), dwsp.reshape(N_DEV, sh_rows, D),
         dwhp.reshape(N_DEV, sh_rows, D), dwout.reshape(N_DEV, sh_rows, D),
         dwd.reshape(N_DEV, w_down.shape[1], D), dmeta_s, dcw], True, "scatter_grads")
    small_all = _exchange_call([small_pack], False, "gather_small_grads")[0]

    def sum8(a):
        t = a[0]
        for s in range(1, N_DEV):
            t = t + a[s]
        return (t,)

    small_sum = _small_call(sum8, [small_all], [(small_pack.shape, F32)], "sum_small_grads")[0]
    (g_g1, t_abr, t_abi, t_bbr, t_bbi, g_cre, g_cim, g_dsk, t_lb, g_ng, g_g2, g_cb, g_g3, loss_v) = _unpack(
        small_sum, small_shapes)

    def disc_b_bwd(cr, ci, br, bi, dbr, dbi):
        _, vjp = jax.vjp(_disc_b, cr, ci, br, bi)
        return vjp((dbr, dbi))

    t_cr, t_ci, g_btr, g_bti = _small_call(
        disc_b_bwd, [coef_re.reshape(1, G * P), coef_im.reshape(1, G * P), bt_re, bt_im, t_bbr, t_bbi],
        [((1, G * P), F32)] * 2 + [((H, G * P), F32)] * 2, "s5_input_matrix_bwd")

    def disc_a_bwd(lr_, li_, ldt_, dar, dai, dcr, dci):
        _, vjp = jax.vjp(_disc_a, lr_, li_, ldt_)
        return vjp((dar, dai, dcr, dci))

    g_lr, g_li, g_ldt = _small_call(
        disc_a_bwd, [lr, li, ldt, t_abr, t_abi, t_cr.reshape(G, P), t_ci.reshape(G, P)],
        [((G, P), F32)] * 2 + [((G, 1), F32)], "s5_discretise_bwd")

    def lb_bwd(logits, d):
        _, vjp = jax.vjp(_lb_fn, logits)
        return vjp(d)

    g_lbl = _small_call(lb_bwd, [hgrn_lb_logits, t_lb], [(hgrn_lb_logits.shape, F32)], "hgrn_lower_bound_bwd")[0]

    grads = dict(
        mix_norm_g=g_g1, ssm_lambda_re=g_lr[None], ssm_lambda_im=g_li[None], ssm_log_dt=g_ldt.reshape(1, G),
        ssm_b_re=g_btr.reshape(H, G, P).transpose(1, 2, 0)[None], ssm_b_im=g_bti.reshape(H, G, P).transpose(1, 2, 0)[None],
        ssm_c_re=g_cre[None], ssm_c_im=g_cim[None], ssm_d=g_dsk, hgrn_lb_logits=g_lbl, hgrn_norm_g=g_ng,
        ffn_norm_g=g_g2, conv_b=g_cb.reshape(1, N_DEV * F), final_norm_g=g_g3.reshape(D))
    loss = loss_v[0, 0]

    delta, new_m, new_v = {}, {}, {}
    sharded = [("w_in", parts[0], (D, n_in)), ("w_up", parts[1], (D, F)), ("ssm_w_glu", parts[2], (sh_rows, D)),
               ("w_ssm_proj", parts[3], (sh_rows, D)), ("w_hgrn_proj", parts[4], (sh_rows, D)),
               ("w_out", parts[5], (sh_rows, D)), ("w_down", parts[6], (w_down.shape[1], D)),
               ("meta_tokens", parts[7], (N_META, D // N_DEV)), ("conv_w", parts[8], (3, F))]
    for name, part, shp in sharded:
        full = args[name].shape
        g, d_, nm, nv = _adamw_shard_call(args[name].reshape(shp), part, args["m_" + name].reshape(shp),
                                          args["v_" + name].reshape(shp), "adamw_" + name)
        grads[name], delta[name], new_m[name], new_v[name] = [t.reshape(full) for t in (g, d_, nm, nv)]

    rep = ["mix_norm_g", "ssm_lambda_re", "ssm_lambda_im", "ssm_log_dt", "ssm_b_re", "ssm_b_im", "ssm_c_re",
           "ssm_c_im", "ssm_d", "hgrn_lb_logits", "hgrn_norm_g", "ffn_norm_g", "conv_b", "final_norm_g"]
    rep_shapes = [args[n].shape for n in rep]
    packs = [_pack([args[pre + n] for n in rep]) for pre in ("", "m_", "v_")]
    g_pack = _pack([grads[n] for n in rep])
    outs = _small_call(lambda w, g, m, v: _adamw(w, g, m, v), [packs[0], g_pack, packs[1], packs[2]],
                       [(g_pack.shape, F32)] * 3, "adamw_replicated")
    for n, d_, nm, nv in zip(rep, *[_unpack(o, rep_shapes) for o in outs]):
        delta[n], new_m[n], new_v[n] = d_, nm, nv

    names = ["meta_tokens", "mix_norm_g", "w_in", "ssm_lambda_re", "ssm_lambda_im", "ssm_log_dt", "ssm_b_re",
             "ssm_b_im", "ssm_c_re", "ssm_c_im", "ssm_d", "ssm_w_glu", "w_ssm_proj", "hgrn_lb_logits", "hgrn_norm_g",
             "w_hgrn_proj", "w_out", "ffn_norm_g", "w_up", "conv_w", "conv_b", "w_down", "final_norm_g"]
    return (loss, grad_x, *[grads[n] for n in names], *[delta[n] for n in names],
            *[new_m[n] for n in names], *[new_v[n] for n in names])
```

```python
import functools

import jax
import jax.numpy as jnp
from jax import lax
from jax.experimental import pallas as pl
from jax.experimental.pallas import tpu as pltpu

F32 = jnp.float32
MXU = jnp.bfloat16
ACT = jnp.bfloat16
WIRE = jnp.bfloat16
N_DEV = 8
N_META = 16
CHUNK = 16
EPS = 1e-6
ADAM_LR, ADAM_B1, ADAM_B2, ADAM_EPS, ADAM_WD, ADAM_STEP = 0.001, 0.9, 0.999, 1e-08, 0.01, 10
SUBLANES = 8
LANES = 128
ROW_TILE_CAP = 700
VMEM_LIMIT = 60 * 1024 * 1024


def _cparams(**kw):
    return pltpu.CompilerParams(vmem_limit_bytes=VMEM_LIMIT, **kw)


def _tile(n, cap):
    best = None
    for t in range(16, min(n, cap) + 1, 16):
        if n % t == 0:
            best = t
    assert best is not None, (n, cap)
    return best


def _dot(a, b):
    return lax.dot_general(a.astype(MXU), b.astype(MXU), (((1,), (0,)), ((), ())), preferred_element_type=F32)


def _dot_nt(a, b):
    return lax.dot_general(a.astype(MXU), b.astype(MXU), (((1,), (1,)), ((), ())), preferred_element_type=F32)


def _dot_tn(a, b):
    return lax.dot_general(a.astype(MXU), b.astype(MXU), (((0,), (0,)), ((), ())), preferred_element_type=F32)


def _rms(x, g):
    return x * lax.rsqrt(jnp.mean(x * x, axis=-1, keepdims=True) + EPS) * g


def _silu(x):
    return x * jax.nn.sigmoid(x)


def _small_call(fn, ins, out_shapes, name):
    n_in = len(ins)

    def body(*refs):
        outs = fn(*[r[...] for r in refs[:n_in]])
        outs = outs if isinstance(outs, (tuple, list)) else (outs,)
        for r, o in zip(refs[n_in:], outs):
            r[...] = o.astype(r.dtype)

    vm = pl.BlockSpec(memory_space=pltpu.VMEM)
    return pl.pallas_call(
        body, name=name, out_shape=tuple(jax.ShapeDtypeStruct(s, d) for s, d in out_shapes),
        in_specs=[vm] * n_in, out_specs=tuple([vm] * len(out_shapes)), compiler_params=_cparams())(*ins)


def _disc_a(lr, li, ldt):
    dt = jnp.exp(ldt)
    mag = jnp.exp(lr * dt)
    ab_re = mag * jnp.cos(li * dt)
    ab_im = mag * jnp.sin(li * dt)
    den = lr * lr + li * li
    nr = ab_re - 1.0
    coef_re = (nr * lr + ab_im * li) / den
    coef_im = (ab_im * lr - nr * li) / den
    return ab_re, ab_im, coef_re, coef_im


def _disc_a_powers(lr, li, ldt):
    ab_re, ab_im, coef_re, coef_im = _disc_a(lr, li, ldt)
    pr, pi = [ab_re], [ab_im]
    for _ in range(SUBLANES - 1):
        pr, pi = pr + [pr[-1] * ab_re - pi[-1] * ab_im], pi + [pr[-1] * ab_im + pi[-1] * ab_re]
    return (*pr, *pi, coef_re, coef_im)


def _disc_b(coef_re, coef_im, bt_re, bt_im):
    return coef_re * bt_re - coef_im * bt_im, coef_re * bt_im + coef_im * bt_re


def _lb_fn(logits):
    return jax.nn.softmax(logits, axis=0)[0:1]


def _adamw(w, g, m, v):
    m = ADAM_B1 * m + (1.0 - ADAM_B1) * g
    v = ADAM_B2 * v + (1.0 - ADAM_B2) * jnp.square(g)
    m_hat = m / (1.0 - ADAM_B1 ** ADAM_STEP)
    v_hat = v / (1.0 - ADAM_B2 ** ADAM_STEP)
    delta = -ADAM_LR * (m_hat / (jnp.sqrt(v_hat) + ADAM_EPS) + ADAM_WD * w)
    return delta, m, v


def _norm_call(h, g, tm, name):
    T, D = h.shape

    def body(h_ref, g_ref, z_ref):
        z_ref[...] = _rms(h_ref[...], g_ref[...]).astype(ACT)

    return pl.pallas_call(
        body, name=name, out_shape=jax.ShapeDtypeStruct((T, D), ACT), grid=(T // tm,),
        in_specs=[pl.BlockSpec((tm, D), lambda i: (i, 0)), pl.BlockSpec((1, D), lambda i: (0, 0))],
        out_specs=pl.BlockSpec((tm, D), lambda i: (i, 0)), compiler_params=_cparams())(h, g)


def _mm_shard(x, w, tm, name, major):
    T, K = x.shape
    S, _, N = w.shape

    def body(x_ref, w_ref, o_ref):
        o_ref[...] = _dot(x_ref[...], w_ref[...]).astype(o_ref.dtype)

    if major:
        out_shape = jax.ShapeDtypeStruct((S, T, N), ACT)
        out_spec = pl.BlockSpec((None, tm, N), lambda j, i: (j, i, 0))
    else:
        out_shape = jax.ShapeDtypeStruct((T, S * N), ACT)
        out_spec = pl.BlockSpec((tm, N), lambda j, i: (i, j))
    return pl.pallas_call(
        body, name=name, out_shape=out_shape, grid=(S, T // tm),
        in_specs=[pl.BlockSpec((tm, K), lambda j, i: (i, 0)), pl.BlockSpec((None, K, N), lambda j, i: (j, 0, 0))],
        out_specs=out_spec, compiler_params=_cparams())(x, w)


def _mm_tn(x, y, n_shards, tm, name, major):
    T, K = x.shape
    S = n_shards
    N = y.shape[-1] if major else y.shape[-1] // S

    def body(x_ref, y_ref, o_ref):
        @pl.when(pl.program_id(1) == 0)
        def _():
            o_ref[...] = jnp.zeros_like(o_ref)
        o_ref[...] += _dot_tn(x_ref[...], y_ref[...])

    y_spec = (pl.BlockSpec((None, tm, N), lambda j, i: (j, i, 0)) if major
              else pl.BlockSpec((tm, N), lambda j, i: (i, j)))
    return pl.pallas_call(
        body, name=name, out_shape=jax.ShapeDtypeStruct((S, K, N), F32), grid=(S, T // tm),
        in_specs=[pl.BlockSpec((tm, K), lambda j, i: (i, 0)), y_spec],
        out_specs=pl.BlockSpec((None, K, N), lambda j, i: (j, 0, 0)), compiler_params=_cparams())(x, y)


def _lin_bwd(x, dy, w, tm, name):
    T, K = x.shape
    N = dy.shape[1]

    def body(x_ref, dy_ref, w_ref, dx_ref, dw_ref):
        @pl.when(pl.program_id(0) == 0)
        def _():
            dw_ref[...] = jnp.zeros_like(dw_ref)
        dy = dy_ref[...]
        dx_ref[...] = _dot_nt(dy, w_ref[...]).astype(dx_ref.dtype)
        dw_ref[...] += _dot_tn(x_ref[...], dy)

    return pl.pallas_call(
        body, name=name,
        out_shape=(jax.ShapeDtypeStruct((T, K), ACT), jax.ShapeDtypeStruct((K, N), F32)), grid=(T // tm,),
        in_specs=[pl.BlockSpec((tm, K), lambda i: (i, 0)), pl.BlockSpec((tm, N), lambda i: (i, 0)),
                  pl.BlockSpec((K, N), lambda i: (0, 0))],
        out_specs=(pl.BlockSpec((tm, K), lambda i: (i, 0)), pl.BlockSpec((K, N), lambda i: (0, 0))),
        compiler_params=_cparams())(x, dy, w)


def _scan_slabs(x_ref, tab_ref, n_slabs, reverse):
    hw = x_ref.shape[1] // 2
    tabs = [tab_ref[s] for s in range(4)]

    def cmul(t, xr, xi):
        tr, ti = t[:, :hw], t[:, hw:]
        return tr * xr - ti * xi, tr * xi + ti * xr

    def step(k, carry):
        cr, ci = carry
        kk = (n_slabs - 1 - k) if reverse else k
        r0 = pl.multiple_of(kk * SUBLANES, SUBLANES)
        x = x_ref[pl.ds(r0, SUBLANES), :]
        xr, xi = x[:, :hw], x[:, hw:]
        for s, d in enumerate((1, 2, 4)):
            sh = (SUBLANES - d) if reverse else d
            ar, ai = cmul(tabs[s], pltpu.roll(xr, sh, 0), pltpu.roll(xi, sh, 0))
            xr, xi = xr + ar, xi + ai
        pr, pi = cmul(tabs[3], cr, ci)
        xr, xi = xr + pr, xi + pi
        x_ref[pl.ds(r0, SUBLANES), 0:hw] = xr
        x_ref[pl.ds(r0, SUBLANES), hw:2 * hw] = xi
        e = 0 if reverse else SUBLANES - 1
        return xr[e:e + 1], xi[e:e + 1]

    z = jnp.zeros((1, hw), F32)
    lax.fori_loop(0, n_slabs, step, (z, z))


def _s5_fwd_call(p3, wb, wc, tab_f, dsk, name):
    B, L, _ = p3.shape
    n_cb, cw, sw = wb.shape

    def body(u_ref, wb_ref, wc_ref, tab_ref, d_ref, ya_ref, s_ref):
        u = u_ref[...]
        s_ref[...] = _dot(u, wb_ref[...])
        _scan_slabs(s_ref, tab_ref, L // SUBLANES, False)
        y = _dot(s_ref[...], wc_ref[...]) + d_ref[...] * u.astype(F32)
        ya_ref[...] = jax.nn.gelu(y).astype(ACT)

    return pl.pallas_call(
        body, name=name, out_shape=jax.ShapeDtypeStruct((B, L, n_cb * cw), ACT), grid=(B, n_cb),
        in_specs=[pl.BlockSpec((None, L, cw), lambda b, c: (b, 0, c)),
                  pl.BlockSpec((None, cw, sw), lambda b, c: (c, 0, 0)),
                  pl.BlockSpec((None, sw, cw), lambda b, c: (c, 0, 0)),
                  pl.BlockSpec((None, 4, SUBLANES, sw), lambda b, c: (c, 0, 0, 0)),
                  pl.BlockSpec((None, 1, cw), lambda b, c: (c, 0, 0))],
        out_specs=pl.BlockSpec((None, L, cw), lambda b, c: (b, 0, c)),
        scratch_shapes=[pltpu.VMEM((L, sw), F32)], compiler_params=_cparams())(p3, wb, wc, tab_f, dsk)


def _s5_bwd_call(p3, dya, wb, wc, tab_f, tab_r, dsk, name):
    B, L, _ = p3.shape
    n_cb, cw, sw = wb.shape
    hw = sw // 2
    n_slabs = L // SUBLANES

    def body(u_ref, dya_ref, wb_ref, wc_ref, tf_ref, tr_ref, d_ref,
             du_ref, dwb_ref, dwc_ref, da_ref, dd_ref, s_ref, l_ref):
        @pl.when(pl.program_id(1) == 0)
        def _():
            dwb_ref[...] = jnp.zeros_like(dwb_ref)
            dwc_ref[...] = jnp.zeros_like(dwc_ref)
            da_ref[...] = jnp.zeros_like(da_ref)
            dd_ref[...] = jnp.zeros_like(dd_ref)

        u = u_ref[...]
        uf = u.astype(F32)
        s_ref[...] = _dot(u, wb_ref[...])
        _scan_slabs(s_ref, tf_ref, n_slabs, False)
        y = _dot(s_ref[...], wc_ref[...]) + d_ref[...] * uf
        _, gelu_vjp = jax.vjp(jax.nn.gelu, y)
        dy = gelu_vjp(dya_ref[...].astype(F32))[0]
        dd_ref[...] += jnp.sum(dy * uf, axis=0, keepdims=True)
        l_ref[...] = _dot_nt(dy, wc_ref[...])
        _scan_slabs(l_ref, tr_ref, n_slabs, True)
        du_ref[...] = (_dot_nt(l_ref[...], wb_ref[...]) + d_ref[...] * dy).astype(ACT)
        dwb_ref[...] += _dot_tn(u, l_ref[...])
        dwc_ref[...] += _dot_tn(s_ref[...], dy)

        row = lax.broadcasted_iota(jnp.int32, (SUBLANES, hw), 0)

        def step(k, carry):
            pr, pi, accr, acci = carry
            r0 = pl.multiple_of(k * SUBLANES, SUBLANES)
            s = s_ref[pl.ds(r0, SUBLANES), :]
            lam = l_ref[pl.ds(r0, SUBLANES), :]
            sr, si = s[:, :hw], s[:, hw:]
            lr, li = lam[:, :hw], lam[:, hw:]
            qr = jnp.where(row == 0, pr, pltpu.roll(sr, 1, 0))
            qi = jnp.where(row == 0, pi, pltpu.roll(si, 1, 0))
            accr = accr + lr * qr + li * qi
            acci = acci + li * qr - lr * qi
            return sr[SUBLANES - 1:], si[SUBLANES - 1:], accr, acci

        z1 = jnp.zeros((1, hw), F32)
        z8 = jnp.zeros((SUBLANES, hw), F32)
        _, _, accr, acci = lax.fori_loop(0, n_slabs, step, (z1, z1, z8, z8))
        da_ref[...] += jnp.concatenate([jnp.sum(accr, axis=0, keepdims=True),
                                        jnp.sum(acci, axis=0, keepdims=True)], axis=1)

    W = n_cb * cw
    return pl.pallas_call(
        body, name=name,
        out_shape=(jax.ShapeDtypeStruct((B, L, W), ACT), jax.ShapeDtypeStruct((n_cb, cw, sw), F32),
                   jax.ShapeDtypeStruct((n_cb, sw, cw), F32), jax.ShapeDtypeStruct((n_cb, 1, sw), F32),
                   jax.ShapeDtypeStruct((n_cb, 1, cw), F32)),
        grid=(n_cb, B),
        in_specs=[pl.BlockSpec((None, L, cw), lambda c, b: (b, 0, c)),
                  pl.BlockSpec((None, L, cw), lambda c, b: (b, 0, c)),
                  pl.BlockSpec((None, cw, sw), lambda c, b: (c, 0, 0)),
                  pl.BlockSpec((None, sw, cw), lambda c, b: (c, 0, 0)),
                  pl.BlockSpec((None, 4, SUBLANES, sw), lambda c, b: (c, 0, 0, 0)),
                  pl.BlockSpec((None, 4, SUBLANES, sw), lambda c, b: (c, 0, 0, 0)),
                  pl.BlockSpec((None, 1, cw), lambda c, b: (c, 0, 0))],
        out_specs=(pl.BlockSpec((None, L, cw), lambda c, b: (b, 0, c)),
                   pl.BlockSpec((None, cw, sw), lambda c, b: (c, 0, 0)),
                   pl.BlockSpec((None, sw, cw), lambda c, b: (c, 0, 0)),
                   pl.BlockSpec((None, 1, sw), lambda c, b: (c, 0, 0)),
                   pl.BlockSpec((None, 1, cw), lambda c, b: (c, 0, 0))),
        scratch_shapes=[pltpu.VMEM((L, sw), F32), pltpu.VMEM((L, sw), F32)],
        compiler_params=_cparams())(p3, dya, wb, wc, tab_f, tab_r, dsk)


def _glu_proj_call(ya, wglu, wproj, tm, name):
    T, W = ya.shape
    D = wproj.shape[1]

    def body(ya_ref, wg_ref, wp_ref, yo_ref, a_ref):
        ya = ya_ref[...]
        yo = ya.astype(F32) * jax.nn.sigmoid(_dot(ya, wg_ref[...]))
        yo_ref[...] = yo.astype(ACT)
        a_ref[...] = _dot(yo, wp_ref[...]).astype(ACT)

    return pl.pallas_call(
        body, name=name, out_shape=(jax.ShapeDtypeStruct((T, W), ACT), jax.ShapeDtypeStruct((T, D), ACT)),
        grid=(T // tm,),
        in_specs=[pl.BlockSpec((tm, W), lambda i: (i, 0)), pl.BlockSpec((W, W), lambda i: (0, 0)),
                  pl.BlockSpec((W, D), lambda i: (0, 0))],
        out_specs=(pl.BlockSpec((tm, W), lambda i: (i, 0)), pl.BlockSpec((tm, D), lambda i: (i, 0))),
        compiler_params=_cparams())(ya, wglu, wproj)


def _glu_bwd_call(ya, dyo, wglu, tm, name):
    T, W = ya.shape

    def body(ya_ref, dyo_ref, wg_ref, dya_ref, dwg_ref):
        @pl.when(pl.program_id(0) == 0)
        def _():
            dwg_ref[...] = jnp.zeros_like(dwg_ref)
        ya = ya_ref[...]
        yaf = ya.astype(F32)
        dyo = dyo_ref[...].astype(F32)
        sg = jax.nn.sigmoid(_dot(ya, wg_ref[...]))
        dt = dyo * yaf * sg * (1.0 - sg)
        dya_ref[...] = (dyo * sg + _dot_nt(dt, wg_ref[...])).astype(ACT)
        dwg_ref[...] += _dot_tn(ya, dt)

    return pl.pallas_call(
        body, name=name, out_shape=(jax.ShapeDtypeStruct((T, W), ACT), jax.ShapeDtypeStruct((W, W), F32)),
        grid=(T // tm,),
        in_specs=[pl.BlockSpec((tm, W), lambda i: (i, 0)), pl.BlockSpec((tm, W), lambda i: (i, 0)),
                  pl.BlockSpec((W, W), lambda i: (0, 0))],
        out_specs=(pl.BlockSpec((tm, W), lambda i: (i, 0)), pl.BlockSpec((W, W), lambda i: (0, 0))),
        compiler_params=_cparams())(ya, dyo, wglu)


PAD = 16


def _chunk_cumsums(x, pad_ref, L):
    row = lax.broadcasted_iota(jnp.int32, x.shape, 0) % CHUNK
    zeros = jnp.zeros((PAD, x.shape[1]), F32)
    pad_ref[0:PAD, :] = zeros
    pad_ref[PAD + L:2 * PAD + L, :] = zeros
    c = x
    r = x
    d = 1
    while d < CHUNK:
        pad_ref[PAD:PAD + L, :] = c
        c = c + jnp.where(row >= d, pad_ref[PAD - d:PAD - d + L, :], 0.0)
        pad_ref[PAD:PAD + L, :] = r
        r = r + jnp.where(row + d < CHUNK, pad_ref[PAD + d:PAD + d + L, :], 0.0)
        d *= 2
    return c, r - x


def _hgrn_prep(q_ref, fl_ref, lb_ref, pad_ref, L):
    lb = lb_ref[...]
    sig = jax.nn.sigmoid(fl_ref[...].astype(F32))
    f = lb + (1.0 - lb) * sig
    k = 1.0 - f
    c, rc = _chunk_cumsums(jnp.log(f), pad_ref, L)
    e_in, e_inv, e_out = jnp.exp(c), jnp.exp(-c), jnp.exp(rc)
    q = q_ref[...].astype(F32)
    return dict(sig=sig, f=f, k=k, q=q, e_in=e_in, e_inv=e_inv, e_out=e_out, dec=jnp.exp(c + rc))


def _chunk_mask(rb):
    r = lax.broadcasted_iota(jnp.int32, (rb, rb), 0)
    c = lax.broadcasted_iota(jnp.int32, (rb, rb), 1)
    return (r // CHUNK == c // CHUNK) & (c <= r)


def _hg_out(o, og, g):
    on = o * lax.rsqrt(jnp.mean(o * o, axis=-1, keepdims=True) + EPS) * g
    return on * _silu(og)


def _hgrn_specs(L, hd, col_q, n_heads, order):
    def spec(sec):
        return pl.BlockSpec((None, L, hd), lambda *g: (order(*g)[0], 0, col_q + sec * n_heads + order(*g)[1]))
    return [spec(0), spec(1), spec(2), spec(3)]


GROUP = 128
CPG = GROUP // CHUNK


def _expand(x):
    xf = x.astype(F32)
    chunk = lax.broadcasted_iota(jnp.int32, xf.shape, 0) // CHUNK
    return jnp.concatenate([jnp.where(chunk == j, xf, 0.0) for j in range(CPG)], axis=1)


def _store_padded(ref, val, L, fill):
    ref[0:L, :] = val.astype(ref.dtype)
    if ref.shape[0] > L:
        ref[L:ref.shape[0], :] = jnp.full((ref.shape[0] - L, ref.shape[1]), fill, ref.dtype)


def _hgrn_forward_core(q_ref, fl_ref, v_ref, lb_ref, pad_ref, qin_ref, kin_ref, kout_ref, vp_ref, dec_ref, o_ref,
                       s_ref, L):
    hd = qin_ref.shape[1]
    n_groups = qin_ref.shape[0] // GROUP
    pp = _hgrn_prep(q_ref, fl_ref, lb_ref, pad_ref, L)
    _store_padded(qin_ref, pp["q"] * pp["e_in"], L, 0.0)
    _store_padded(kin_ref, pp["k"] * pp["e_inv"], L, 0.0)
    _store_padded(kout_ref, pp["k"] * pp["e_out"], L, 0.0)
    _store_padded(vp_ref, v_ref[...], L, 0.0)
    _store_padded(dec_ref, pp["dec"], L, 1.0)
    mask = _chunk_mask(GROUP)

    def intra(g, carry):
        rows = pl.ds(pl.multiple_of(g * GROUP, GROUP), GROUP)
        a = jnp.where(mask, _dot_nt(qin_ref[rows, :], kin_ref[rows, :]), 0.0)
        o_ref[rows, :] = _dot(a, vp_ref[rows, :])
        kv = _dot_tn(vp_ref[rows, :], _expand(kout_ref[rows, :]))
        for j in range(CPG):
            s_ref[g * CPG + j] = kv[:, j * hd:(j + 1) * hd]
        return carry

    lax.fori_loop(0, n_groups, intra, 0)

    def rec(n, st):
        kv = s_ref[n]
        s_ref[n] = st
        dec = dec_ref[pl.ds(pl.multiple_of(n * CHUNK, CHUNK), SUBLANES), :][0:1]
        return st * dec + kv

    lax.fori_loop(0, L // CHUNK, rec, jnp.zeros((hd, hd), F32))

    def inter(g, carry):
        rows = pl.ds(pl.multiple_of(g * GROUP, GROUP), GROUP)
        scat = jnp.concatenate([s_ref[g * CPG + j] for j in range(CPG)], axis=1)
        o_ref[rows, :] += _dot_nt(_expand(qin_ref[rows, :]), scat)
        return carry

    lax.fori_loop(0, n_groups, inter, 0)
    return pp


def _hgrn_scratch(L, hd):
    lp = -(-L // GROUP) * GROUP
    return lp, [pltpu.VMEM((L + 2 * PAD, hd), F32), pltpu.VMEM((lp, hd), MXU), pltpu.VMEM((lp, hd), MXU),
                pltpu.VMEM((lp, hd), MXU), pltpu.VMEM((lp, hd), MXU), pltpu.VMEM((lp, hd), F32),
                pltpu.VMEM((lp, hd), F32), pltpu.VMEM((lp // CHUNK, hd, hd), F32)]


def _hgrn_fwd_call(p3, lb, ng, n_heads, col_q, name):
    B, L, _ = p3.shape
    hd = ng.shape[1]
    _, scratch = _hgrn_scratch(L, hd)

    def body(q_ref, fl_ref, v_ref, og_ref, lb_ref, ng_ref, yb_ref,
             pad_ref, qin_ref, kin_ref, kout_ref, vp_ref, dec_ref, o_ref, s_ref):
        _hgrn_forward_core(q_ref, fl_ref, v_ref, lb_ref, pad_ref, qin_ref, kin_ref, kout_ref, vp_ref, dec_ref,
                           o_ref, s_ref, L)
        yb_ref[...] = _hg_out(o_ref[0:L, :], og_ref[...].astype(F32), ng_ref[...]).astype(ACT)

    order = lambda b, h: (b, h)
    return pl.pallas_call(
        body, name=name, out_shape=jax.ShapeDtypeStruct((B, L, n_heads * hd), ACT), grid=(B, n_heads),
        in_specs=_hgrn_specs(L, hd, col_q, n_heads, order) + [
            pl.BlockSpec((1, hd), lambda b, h: (0, h)), pl.BlockSpec((1, hd), lambda b, h: (0, 0))],
        out_specs=pl.BlockSpec((None, L, hd), lambda b, h: (b, 0, h)),
        scratch_shapes=scratch, compiler_params=_cparams())(p3, p3, p3, p3, lb, ng)


def _hgrn_bwd_call(p3, dyb, lb, ng, n_heads, col_q, name):
    B, L, _ = p3.shape
    hd = ng.shape[1]
    n_chunks = L // CHUNK
    lp, scratch = _hgrn_scratch(L, hd)
    n_groups = lp // GROUP

    def body(q_ref, fl_ref, v_ref, og_ref, dyb_ref, lb_ref, ng_ref,
             dq_ref, dfl_ref, dv_ref, dog_ref, dlb_ref, dng_ref,
             pad_ref, qin_ref, kin_ref, kout_ref, vp_ref, dec_ref, o_ref, s_ref,
             do_ref, ds_ref, dqi_ref, dki_ref, dko_ref, dvv_ref, dct_ref):
        @pl.when(pl.program_id(1) == 0)
        def _():
            dlb_ref[...] = jnp.zeros_like(dlb_ref)

        @pl.when((pl.program_id(0) == 0) & (pl.program_id(1) == 0))
        def _():
            dng_ref[...] = jnp.zeros_like(dng_ref)

        pp = _hgrn_forward_core(q_ref, fl_ref, v_ref, lb_ref, pad_ref, qin_ref, kin_ref, kout_ref, vp_ref, dec_ref,
                                o_ref, s_ref, L)

        og = og_ref[...].astype(F32)
        _, out_vjp = jax.vjp(_hg_out, o_ref[0:L, :], og, ng_ref[...])
        d_o, d_og, d_ng = out_vjp(dyb_ref[...].astype(F32))
        dog_ref[...] = d_og.astype(ACT)
        dng_ref[...] += d_ng
        _store_padded(do_ref, d_o, L, 0.0)
        mask = _chunk_mask(GROUP)

        def grads_a(g, carry):
            rows = pl.ds(pl.multiple_of(g * GROUP, GROUP), GROUP)
            qi, ki, vv, do = qin_ref[rows, :], kin_ref[rows, :], vp_ref[rows, :], do_ref[rows, :]
            a = jnp.where(mask, _dot_nt(qi, ki), 0.0)
            da = jnp.where(mask, _dot_nt(do, vv), 0.0)
            sstack = s_ref[pl.ds(g * CPG, CPG)].reshape(CPG * hd, hd)
            dqi_ref[rows, :] = _dot(da, ki) + _dot(_expand(do), sstack)
            dki_ref[rows, :] = _dot_tn(da, qi)
            dvv_ref[rows, :] = _dot_tn(a, do)
            x = _dot_tn(do, _expand(qi))
            for j in range(CPG):
                ds_ref[g * CPG + j] = x[:, j * hd:(j + 1) * hd]
            return carry

        lax.fori_loop(0, n_groups, grads_a, 0)

        def rec_bwd(k, dst):
            n = n_chunks - 1 - k
            r0 = pl.multiple_of(n * CHUNK, CHUNK)
            x = ds_ref[n]
            ds_ref[n] = dst
            dec = dec_ref[pl.ds(r0, SUBLANES), :][0:1]
            ddec = dec * jnp.sum(dst * s_ref[n], axis=0, keepdims=True)
            dct_ref[pl.ds(r0, CHUNK), :] = jnp.broadcast_to(ddec, (CHUNK, hd))
            return dst * dec + x

        lax.fori_loop(0, n_chunks, rec_bwd, jnp.zeros((hd, hd), F32))

        def grads_b(g, carry):
            rows = pl.ds(pl.multiple_of(g * GROUP, GROUP), GROUP)
            dscat = jnp.concatenate([ds_ref[g * CPG + j] for j in range(CPG)], axis=1)
            dvv_ref[rows, :] += _dot_nt(_expand(kout_ref[rows, :]), dscat)
            dstack = ds_ref[pl.ds(g * CPG, CPG)].reshape(CPG * hd, hd)
            dko_ref[rows, :] = _dot(_expand(vp_ref[rows, :]), dstack)
            return carry

        lax.fori_loop(0, n_groups, grads_b, 0)

        dqi, dki, dko = dqi_ref[0:L, :], dki_ref[0:L, :], dko_ref[0:L, :]
        dq = dqi * pp["e_in"]
        dk = dki * pp["e_inv"] + dko * pp["e_out"]
        dq_ref[...] = dq.astype(ACT)
        dv_ref[...] = dvv_ref[0:L, :].astype(ACT)
        t_out = pp["k"] * pp["e_out"] * dko
        dc = pp["q"] * pp["e_in"] * dqi - pp["k"] * pp["e_inv"] * dki - t_out
        _, dc_later = _chunk_cumsums(dc, pad_ref, L)
        t_incl, t_later = _chunk_cumsums(t_out, pad_ref, L)
        dlogf = dc + dc_later + t_incl + t_later + dct_ref[0:L, :]
        df = dlogf / pp["f"] - dk
        lbv = lb_ref[...]
        sig = pp["sig"]
        dfl_ref[...] = (df * (1.0 - lbv) * sig * (1.0 - sig)).astype(ACT)
        dlb_ref[...] += jnp.sum(df * (1.0 - sig), axis=0, keepdims=True)

    order = lambda h, b: (b, h)
    W = n_heads * hd
    act_out = jax.ShapeDtypeStruct((B, L, W), ACT)
    blk_out = pl.BlockSpec((None, L, hd), lambda h, b: (b, 0, h))
    return pl.pallas_call(
        body, name=name,
        out_shape=(act_out, act_out, act_out, act_out, jax.ShapeDtypeStruct((1, W), F32),
                   jax.ShapeDtypeStruct((1, hd), F32)),
        grid=(n_heads, B),
        in_specs=_hgrn_specs(L, hd, col_q, n_heads, order) + [
            pl.BlockSpec((None, L, hd), lambda h, b: (b, 0, h)),
            pl.BlockSpec((1, hd), lambda h, b: (0, h)), pl.BlockSpec((1, hd), lambda h, b: (0, 0))],
        out_specs=(blk_out, blk_out, blk_out, blk_out, pl.BlockSpec((1, hd), lambda h, b: (0, h)),
                   pl.BlockSpec((1, hd), lambda h, b: (0, 0))),
        scratch_shapes=scratch + [
            pltpu.VMEM((lp, hd), MXU), pltpu.VMEM((lp // CHUNK, hd, hd), F32)] + [pltpu.VMEM((lp, hd), F32)] * 5,
        compiler_params=_cparams())(p3, p3, p3, p3, dyb, lb, ng)


def _merge_fn(a, bm, ga, gb):
    return jax.nn.sigmoid(ga) * a + jax.nn.sigmoid(gb) * bm


def _merge_call(yb, a, p, h0, whp, wout, g2, col_ga, tm, name):
    T, D = h0.shape

    def body(yb_ref, a_ref, ga_ref, gb_ref, h0_ref, whp_ref, wout_ref, g2_ref, h1_ref, mg_ref, bm_ref, z2_ref):
        bm = _dot(yb_ref[...], whp_ref[...])
        mg = _merge_fn(a_ref[...].astype(F32), bm, ga_ref[...].astype(F32), gb_ref[...].astype(F32))
        h1 = h0_ref[...] + _dot(mg, wout_ref[...])
        h1_ref[...] = h1
        mg_ref[...] = mg.astype(ACT)
        bm_ref[...] = bm.astype(ACT)
        z2_ref[...] = _rms(h1, g2_ref[...]).astype(ACT)

    tile = pl.BlockSpec((tm, D), lambda i: (i, 0))
    full = pl.BlockSpec((D, D), lambda i: (0, 0))
    act = jax.ShapeDtypeStruct((T, D), ACT)
    return pl.pallas_call(
        body, name=name, out_shape=(jax.ShapeDtypeStruct((T, D), F32), act, act, act), grid=(T // tm,),
        in_specs=[tile, tile, pl.BlockSpec((tm, D), lambda i: (i, col_ga)),
                  pl.BlockSpec((tm, D), lambda i: (i, col_ga + 1)), tile, full, full,
                  pl.BlockSpec((1, D), lambda i: (0, 0))],
        out_specs=(tile, tile, tile, tile), compiler_params=_cparams())(yb, a, p, p, h0, whp, wout, g2)


def _merge_bwd_call(dmg, a, bm, p, col_ga, tm, name):
    T, D = dmg.shape

    def body(dmg_ref, a_ref, bm_ref, ga_ref, gb_ref, da_ref, dbm_ref, dga_ref, dgb_ref):
        args = [r[...].astype(F32) for r in (a_ref, bm_ref, ga_ref, gb_ref)]
        _, vjp = jax.vjp(_merge_fn, *args)
        for r, o in zip((da_ref, dbm_ref, dga_ref, dgb_ref), vjp(dmg_ref[...].astype(F32))):
            r[...] = o.astype(ACT)

    tile = pl.BlockSpec((tm, D), lambda i: (i, 0))
    act = jax.ShapeDtypeStruct((T, D), ACT)
    return pl.pallas_call(
        body, name=name, out_shape=(act, act, act, act), grid=(T // tm,),
        in_specs=[tile, tile, tile, pl.BlockSpec((tm, D), lambda i: (i, col_ga)),
                  pl.BlockSpec((tm, D), lambda i: (i, col_ga + 1))],
        out_specs=(tile, tile, tile, tile), compiler_params=_cparams())(dmg, a, bm, p, p)


def _conv_taps(x_ref, halo_ref, ext_ref, edge, tm, before):
    halo = jnp.where(edge, 0.0, halo_ref[...].astype(F32))
    x = x_ref[...].astype(F32)
    if before:
        ext_ref[0:PAD, :] = halo
        ext_ref[PAD:PAD + tm, :] = x
        return [ext_ref[PAD - 2 + k:PAD - 2 + k + tm, :] for k in range(3)]
    ext_ref[0:tm, :] = x
    ext_ref[tm:tm + PAD, :] = halo
    return [ext_ref[k:k + tm, :] for k in range(3)]


def _conv(taps, cw, cb):
    return cb + cw[0:1] * taps[0] + cw[1:2] * taps[1] + cw[2:3] * taps[2]


def _ffn_pair_specs(tm, F, T, n_pairs, order, before):
    hb = tm // PAD
    last = T // PAD - 1

    def halo_row(i):
        return jnp.maximum(i * hb - 1, 0) if before else jnp.minimum((i + 1) * hb, last)

    specs = []
    for off in (0, n_pairs):
        specs.append(pl.BlockSpec((None, tm, F), lambda *g, off=off: (order(*g)[1] + off, order(*g)[0], 0)))
        specs.append(pl.BlockSpec((None, PAD, F), lambda *g, off=off: (order(*g)[1] + off, halo_row(order(*g)[0]), 0)))
    return specs


def _ffn_fwd_call(up, cw, cb, wd, h1, tgt, g3, tm, tps, name):
    S, T, F = up.shape
    n_pairs = S // 2
    D = h1.shape[1]

    def body(ua_ref, ha_ref, ub_ref, hb_ref, cwa_ref, cwb_ref, cba_ref, cbb_ref, wd_ref, h1_ref, tgt_ref, g3_ref,
             act_ref, dh2_ref, loss_ref, dg3_ref, acc_ref, ext_ref):
        i, j = pl.program_id(0), pl.program_id(1)
        edge = (i % tps) == 0
        ua = _conv(_conv_taps(ua_ref, ha_ref, ext_ref, edge, tm, True), cwa_ref[...], cba_ref[...])
        ub = _conv(_conv_taps(ub_ref, hb_ref, ext_ref, edge, tm, True), cwb_ref[...], cbb_ref[...])
        act = _silu(ua) * ub
        act_ref[...] = act.astype(ACT)
        contrib = _dot(act, wd_ref[...])

        @pl.when(j == 0)
        def _():
            acc_ref[...] = h1_ref[...] + contrib

        @pl.when(j > 0)
        def _():
            acc_ref[...] += contrib

        @pl.when((i == 0) & (j == 0))
        def _():
            loss_ref[...] = jnp.zeros_like(loss_ref)
            dg3_ref[...] = jnp.zeros_like(dg3_ref)

        @pl.when(j == n_pairs - 1)
        def _():
            row = lax.broadcasted_iota(jnp.int32, (tm, 1), 0) + (i % tps) * tm
            valid = row >= N_META
            tgt = tgt_ref[...]

            def loss_fn(h2, g):
                err = _rms(h2, g) - tgt
                return 0.5 * jnp.sum(jnp.where(valid, err * err, 0.0)) / D

            loss, vjp = jax.vjp(loss_fn, acc_ref[...], g3_ref[...])
            dh2, dg3 = vjp(jnp.ones((), F32))
            dh2_ref[...] = dh2
            loss_ref[...] += loss
            dg3_ref[...] += dg3

    order = lambda i, j: (i, j)
    tile = pl.BlockSpec((tm, D), lambda i, j: (i, 0))
    vec = pl.BlockSpec((1, D), lambda i, j: (0, 0))
    return pl.pallas_call(
        body, name=name,
        out_shape=(jax.ShapeDtypeStruct((n_pairs, T, F), ACT), jax.ShapeDtypeStruct((T, D), F32),
                   jax.ShapeDtypeStruct((1, LANES), F32), jax.ShapeDtypeStruct((1, D), F32)),
        grid=(T // tm, n_pairs),
        in_specs=_ffn_pair_specs(tm, F, T, n_pairs, order, True) + [
            pl.BlockSpec((None, 3, F), lambda i, j: (j, 0, 0)), pl.BlockSpec((None, 3, F), lambda i, j: (j + n_pairs, 0, 0)),
            pl.BlockSpec((None, 1, F), lambda i, j: (j, 0, 0)), pl.BlockSpec((None, 1, F), lambda i, j: (j + n_pairs, 0, 0)),
            pl.BlockSpec((None, F, D), lambda i, j: (j, 0, 0)), tile, tile, vec],
        out_specs=(pl.BlockSpec((None, tm, F), lambda i, j: (j, i, 0)), tile,
                   pl.BlockSpec((1, LANES), lambda i, j: (0, 0)), vec),
        scratch_shapes=[pltpu.VMEM((tm, D), F32), pltpu.VMEM((tm + PAD, F), F32)],
        compiler_params=_cparams())(up, up, up, up, cw, cw, cb, cb, wd, h1, tgt, g3)


def _ffn_bwd_a_call(dh2, up, act, cw, cb, wd, tm, tps, name):
    S, T, F = up.shape
    n_pairs = S // 2
    D = dh2.shape[1]

    def body(dh2_ref, ua_ref, ha_ref, ub_ref, hb_ref, act_ref, cwa_ref, cwb_ref, cba_ref, cbb_ref, wd_ref,
             dua_ref, dub_ref, dwd_ref, dcwa_ref, dcwb_ref, dcba_ref, dcbb_ref, ext_ref):
        i = pl.program_id(1)
        edge = (i % tps) == 0

        @pl.when(i == 0)
        def _():
            for r in (dwd_ref, dcwa_ref, dcwb_ref, dcba_ref, dcbb_ref):
                r[...] = jnp.zeros_like(r)

        dh2 = dh2_ref[...]
        dact = _dot_nt(dh2, wd_ref[...])
        dwd_ref[...] += _dot_tn(act_ref[...], dh2)
        taps_a = _conv_taps(ua_ref, ha_ref, ext_ref, edge, tm, True)
        ua = _conv(taps_a, cwa_ref[...], cba_ref[...])
        sa = jax.nn.sigmoid(ua)
        dub = dact * ua * sa
        dcbb_ref[...] += jnp.sum(dub, axis=0, keepdims=True)
        taps_b = _conv_taps(ub_ref, hb_ref, ext_ref, edge, tm, True)
        dcwb_ref[...] += jnp.concatenate([jnp.sum(dub * t, axis=0, keepdims=True) for t in taps_b], axis=0)
        ub = _conv(taps_b, cwb_ref[...], cbb_ref[...])
        dua = dact * ub * sa * (1.0 + ua * (1.0 - sa))
        dcba_ref[...] += jnp.sum(dua, axis=0, keepdims=True)
        taps_a = _conv_taps(ua_ref, ha_ref, ext_ref, edge, tm, True)
        dcwa_ref[...] += jnp.concatenate([jnp.sum(dua * t, axis=0, keepdims=True) for t in taps_a], axis=0)
        dua_ref[...] = dua.astype(ACT)
        dub_ref[...] = dub.astype(ACT)

    order = lambda j, i: (i, j)
    sh = lambda rows: jax.ShapeDtypeStruct((n_pairs, rows, F), F32)
    par = lambda rows: pl.BlockSpec((None, rows, F), lambda j, i: (j, 0, 0))
    return pl.pallas_call(
        body, name=name,
        out_shape=(jax.ShapeDtypeStruct((n_pairs, T, F), ACT), jax.ShapeDtypeStruct((n_pairs, T, F), ACT),
                   jax.ShapeDtypeStruct((n_pairs, F, D), F32), sh(3), sh(3), sh(1), sh(1)),
        grid=(n_pairs, T // tm),
        in_specs=[pl.BlockSpec((tm, D), lambda j, i: (i, 0))] + _ffn_pair_specs(tm, F, T, n_pairs, order, True) + [
            pl.BlockSpec((None, tm, F), lambda j, i: (j, i, 0)),
            pl.BlockSpec((None, 3, F), lambda j, i: (j, 0, 0)), pl.BlockSpec((None, 3, F), lambda j, i: (j + n_pairs, 0, 0)),
            pl.BlockSpec((None, 1, F), lambda j, i: (j, 0, 0)), pl.BlockSpec((None, 1, F), lambda j, i: (j + n_pairs, 0, 0)),
            pl.BlockSpec((None, F, D), lambda j, i: (j, 0, 0))],
        out_specs=(pl.BlockSpec((None, tm, F), lambda j, i: (j, i, 0)), pl.BlockSpec((None, tm, F), lambda j, i: (j, i, 0)),
                   pl.BlockSpec((None, F, D), lambda j, i: (j, 0, 0)), par(3), par(3), par(1), par(1)),
        scratch_shapes=[pltpu.VMEM((tm + PAD, F), F32)],
        compiler_params=_cparams())(dh2, up, up, up, up, act, cw, cw, cb, cb, wd)


def _ffn_bwd_b_call(dua, dub, cw, wup, h1, g2, dh2, tm, tps, name):
    n_pairs, T, F = dua.shape
    D = h1.shape[1]
    hb = tm // PAD
    last = T // PAD - 1

    def body(da_ref, na_ref, db_ref, nb_ref, cwa_ref, cwb_ref, wa_ref, wb_ref, h1_ref, g2_ref, dh2_ref,
             dupa_ref, dupb_ref, dh1_ref, dg2_ref, acc_ref, ext_ref):
        i, j = pl.program_id(0), pl.program_id(1)
        edge = (i % tps) == tps - 1
        outs = []
        for d_ref, n_ref, cw_ref, o_ref in ((da_ref, na_ref, cwa_ref, dupa_ref), (db_ref, nb_ref, cwb_ref, dupb_ref)):
            t = _conv_taps(d_ref, n_ref, ext_ref, edge, tm, False)
            cwv = cw_ref[...]
            dup = cwv[2:3] * t[0] + cwv[1:2] * t[1] + cwv[0:1] * t[2]
            o_ref[...] = dup.astype(ACT)
            outs.append(dup)
        contrib = _dot_nt(outs[0], wa_ref[...]) + _dot_nt(outs[1], wb_ref[...])

        @pl.when(j == 0)
        def _():
            acc_ref[...] = contrib

        @pl.when(j > 0)
        def _():
            acc_ref[...] += contrib

        @pl.when((i == 0) & (j == 0))
        def _():
            dg2_ref[...] = jnp.zeros_like(dg2_ref)

        @pl.when(j == n_pairs - 1)
        def _():
            _, vjp = jax.vjp(_rms, h1_ref[...], g2_ref[...])
            dh, dg = vjp(acc_ref[...])
            dh1_ref[...] = dh2_ref[...] + dh
            dg2_ref[...] += dg

    tile = pl.BlockSpec((tm, D), lambda i, j: (i, 0))
    vec = pl.BlockSpec((1, D), lambda i, j: (0, 0))
    pair = lambda: [pl.BlockSpec((None, tm, F), lambda i, j: (j, i, 0)),
                    pl.BlockSpec((None, PAD, F), lambda i, j: (j, jnp.minimum((i + 1) * hb, last), 0))]
    act = jax.ShapeDtypeStruct((n_pairs, T, F), ACT)
    return pl.pallas_call(
        body, name=name,
        out_shape=(act, act, jax.ShapeDtypeStruct((T, D), F32), jax.ShapeDtypeStruct((1, D), F32)),
        grid=(T // tm, n_pairs),
        in_specs=pair() + pair() + [
            pl.BlockSpec((None, 3, F), lambda i, j: (j, 0, 0)), pl.BlockSpec((None, 3, F), lambda i, j: (j + n_pairs, 0, 0)),
            pl.BlockSpec((None, D, F), lambda i, j: (j, 0, 0)), pl.BlockSpec((None, D, F), lambda i, j: (j + n_pairs, 0, 0)),
            tile, vec, tile],
        out_specs=(pl.BlockSpec((None, tm, F), lambda i, j: (j, i, 0)), pl.BlockSpec((None, tm, F), lambda i, j: (j, i, 0)),
                   tile, vec),
        scratch_shapes=[pltpu.VMEM((tm, D), F32), pltpu.VMEM((tm + PAD, F), F32)],
        compiler_params=_cparams())(dua, dua, dub, dub, cw, cw, wup, wup, h1, g2, dh2)


def _in_bwd_call(dp, w_in, h0, g1, dh1, tm, name):
    T, D = h0.shape
    S, _, N = w_in.shape

    def body(dp_ref, w_ref, h0_ref, g1_ref, dh1_ref, dh0_ref, dg1_ref, acc_ref):
        i, j = pl.program_id(0), pl.program_id(1)
        contrib = _dot_nt(dp_ref[...], w_ref[...])

        @pl.when(j == 0)
        def _():
            acc_ref[...] = contrib

        @pl.when(j > 0)
        def _():
            acc_ref[...] += contrib

        @pl.when((i == 0) & (j == 0))
        def _():
            dg1_ref[...] = jnp.zeros_like(dg1_ref)

        @pl.when(j == S - 1)
        def _():
            _, vjp = jax.vjp(_rms, h0_ref[...], g1_ref[...])
            dh, dg = vjp(acc_ref[...])
            dh0_ref[...] = dh1_ref[...] + dh
            dg1_ref[...] += dg

    tile = pl.BlockSpec((tm, D), lambda i, j: (i, 0))
    vec = pl.BlockSpec((1, D), lambda i, j: (0, 0))
    return pl.pallas_call(
        body, name=name, out_shape=(jax.ShapeDtypeStruct((T, D), F32), jax.ShapeDtypeStruct((1, D), F32)),
        grid=(T // tm, S),
        in_specs=[pl.BlockSpec((tm, N), lambda i, j: (i, j)), pl.BlockSpec((None, D, N), lambda i, j: (j, 0, 0)),
                  tile, vec, tile],
        out_specs=(tile, vec), scratch_shapes=[pltpu.VMEM((tm, D), F32)],
        compiler_params=_cparams())(dp, w_in, h0, g1, dh1)


def _meta_grad_call(dh0_3, name):
    B, L, D = dh0_3.shape

    def body(d_ref, o_ref):
        o_ref[...] = jnp.sum(d_ref[...], axis=0)

    return pl.pallas_call(
        body, name=name, out_shape=jax.ShapeDtypeStruct((N_META, D), F32), grid=(1,),
        in_specs=[pl.BlockSpec((B, N_META, D), lambda i: (0, 0, 0))],
        out_specs=pl.BlockSpec((N_META, D), lambda i: (0, 0)), compiler_params=_cparams())(dh0_3)


_RELS = [(dx, dy, dc) for dx in (0, 1) for dy in (0, 1) for dc in (0, 1)][1:]


def _exchange_call(arrs, scatter, name):
    n = len(arrs)
    n_rel = len(_RELS)

    def body(*refs):
        ins, outs = refs[:n], refs[n:2 * n]
        send_sems, recv_sems, loc_sems = refs[2 * n:]
        x, y, c = lax.axis_index("x"), lax.axis_index("y"), lax.axis_index("c")
        me = 4 * x + 2 * y + c
        started = []
        for k in range(n):
            src_me = ins[k].at[me] if scatter else ins[k]
            loc = pltpu.make_async_copy(src_me, outs[k].at[me], loc_sems.at[k])
            loc.start()
            started.append(loc)
        waits = []
        for r, (dx, dy, dc) in enumerate(_RELS):
            px, py, pc = (x + dx) % 2, (y + dy) % 2, (c + dc) % 2
            pid = 4 * px + 2 * py + pc
            for k in range(n):
                s = k * n_rel + r
                src = ins[k].at[pid] if scatter else ins[k]
                cp = pltpu.make_async_remote_copy(
                    src_ref=src, dst_ref=outs[k].at[me], send_sem=send_sems.at[s], recv_sem=recv_sems.at[s],
                    device_id=(px, py, pc), device_id_type=pl.DeviceIdType.MESH)
                cp.start()
                waits.append(pltpu.make_async_remote_copy(
                    src_ref=src, dst_ref=outs[k].at[pid], send_sem=send_sems.at[s], recv_sem=recv_sems.at[s],
                    device_id=(px, py, pc), device_id_type=pl.DeviceIdType.MESH))
        for w in waits:
            w.wait_send()
            w.wait_recv()
        for loc in started:
            loc.wait()

    out_shape = tuple(jax.ShapeDtypeStruct(a.shape if scatter else (N_DEV,) + a.shape, a.dtype) for a in arrs)
    hbm = pl.BlockSpec(memory_space=pl.ANY)
    return pl.pallas_call(
        body, name=name, out_shape=out_shape, in_specs=[hbm] * n, out_specs=tuple([hbm] * n),
        scratch_shapes=[pltpu.SemaphoreType.DMA((n * n_rel,)), pltpu.SemaphoreType.DMA((n * n_rel,)),
                        pltpu.SemaphoreType.DMA((n,))],
        compiler_params=pltpu.CompilerParams(has_side_effects=True))(*arrs)


def _adamw_shard_call(w, parts, m, v, name):
    R, C = w.shape
    tr = _tile(R, 128) if R % 16 == 0 else R

    def body(w_ref, p_ref, m_ref, v_ref, g_ref, d_ref, nm_ref, nv_ref):
        g = p_ref[0].astype(F32)
        for s in range(1, N_DEV):
            g = g + p_ref[s].astype(F32)
        d, nm, nv = _adamw(w_ref[...], g, m_ref[...], v_ref[...])
        g_ref[...] = g
        d_ref[...] = d
        nm_ref[...] = nm
        nv_ref[...] = nv

    tile = pl.BlockSpec((tr, C), lambda i: (i, 0))
    sh = jax.ShapeDtypeStruct((R, C), F32)
    return pl.pallas_call(
        body, name=name, out_shape=(sh, sh, sh, sh), grid=(R // tr,),
        in_specs=[tile, pl.BlockSpec((N_DEV, tr, C), lambda i: (0, i, 0)), tile, tile],
        out_specs=(tile, tile, tile, tile), compiler_params=_cparams())(w, parts, m, v)


def _pack(arrs, rows_mult=SUBLANES):
    flat = jnp.concatenate([a.reshape(-1).astype(F32) for a in arrs])
    n = flat.shape[0]
    per = rows_mult * LANES
    total = -(-n // per) * per
    return jnp.pad(flat, (0, total - n)).reshape(total // LANES, LANES)


def _unpack(pack, shapes):
    flat = pack.reshape(-1)
    out, off = [], 0
    for s in shapes:
        n = 1
        for d in s:
            n *= d
        out.append(flat[off:off + n].reshape(s))
        off += n
    return out


def kernel(x, meta_tokens, mix_norm_g, w_in, ssm_lambda_re, ssm_lambda_im, ssm_log_dt, ssm_b_re, ssm_b_im, ssm_c_re, ssm_c_im, ssm_d, ssm_w_glu, w_ssm_proj, hgrn_lb_logits, hgrn_norm_g, w_hgrn_proj, w_out, ffn_norm_g, w_up, conv_w, conv_b, w_down, final_norm_g, loss_target, m_meta_tokens, m_mix_norm_g, m_w_in, m_ssm_lambda_re, m_ssm_lambda_im, m_ssm_log_dt, m_ssm_b_re, m_ssm_b_im, m_ssm_c_re, m_ssm_c_im, m_ssm_d, m_ssm_w_glu, m_w_ssm_proj, m_hgrn_lb_logits, m_hgrn_norm_g, m_w_hgrn_proj, m_w_out, m_ffn_norm_g, m_w_up, m_conv_w, m_conv_b, m_w_down, m_final_norm_g, v_meta_tokens, v_mix_norm_g, v_w_in, v_ssm_lambda_re, v_ssm_lambda_im, v_ssm_log_dt, v_ssm_b_re, v_ssm_b_im, v_ssm_c_re, v_ssm_c_im, v_ssm_d, v_ssm_w_glu, v_w_ssm_proj, v_hgrn_lb_logits, v_hgrn_norm_g, v_w_hgrn_proj, v_w_out, v_ffn_norm_g, v_w_up, v_conv_w, v_conv_b, v_w_down, v_final_norm_g):
    args = dict(locals())
    B, S_len, D = x.shape
    L = S_len + N_META
    T = B * L
    tm = _tile(L, ROW_TILE_CAP)
    tps = L // tm
    G, P = ssm_lambda_re.shape[1:]
    H = ssm_b_re.shape[-1]
    W = G * H
    n_cb = W // LANES
    gpb = G // n_cb
    hd = hgrn_norm_g.shape[1]
    n_heads = D // hd
    n_in = w_in.shape[2]
    F = w_up.shape[2]
    assert W == D and n_in % LANES == 0

    gathered = _exchange_call(
        [w_in[0].astype(MXU), w_up[0].astype(MXU), ssm_w_glu[0].astype(MXU), w_ssm_proj[0].astype(MXU),
         w_hgrn_proj[0].astype(MXU), w_out[0].astype(MXU), w_down[0].astype(MXU), meta_tokens, conv_w[0]],
        False, "gather_weights")
    win_g, wup_g = gathered[0], gathered[1]
    wglu_g, wsp_g, whp_g, wout_g = [g.reshape(D, D) for g in gathered[2:6]]
    wdn_g = gathered[6].reshape(N_DEV // 2, 2 * w_down.shape[1], D)
    meta_full = gathered[7].transpose(1, 0, 2).reshape(N_META, D)
    cw_g = gathered[8]
    cb_g = conv_b.reshape(N_DEV, 1, F)

    h0 = jnp.concatenate([jnp.broadcast_to(meta_full[None], (B, N_META, D)), x], axis=1).reshape(T, D)
    tgt = jnp.concatenate([jnp.zeros((B, N_META, D), F32), loss_target], axis=1).reshape(T, D)

    lr, li = ssm_lambda_re[0], ssm_lambda_im[0]
    ldt = ssm_log_dt[0].reshape(G, 1)
    bt_re = ssm_b_re[0].transpose(2, 0, 1).reshape(H, G * P)
    bt_im = ssm_b_im[0].transpose(2, 0, 1).reshape(H, G * P)
    disc = _small_call(_disc_a_powers, [lr, li, ldt], [((G, P), F32)] * (2 * SUBLANES + 2), "s5_discretise")
    pw_re, pw_im = jnp.stack(disc[:SUBLANES]), jnp.stack(disc[SUBLANES:2 * SUBLANES])
    coef_re, coef_im = disc[2 * SUBLANES:]
    bbt_re, bbt_im = _small_call(
        _disc_b, [coef_re.reshape(1, G * P), coef_im.reshape(1, G * P), bt_re, bt_im],
        [((H, G * P), F32)] * 2, "s5_input_matrix")
    eye = jnp.eye(gpb, dtype=F32)
    hw = gpb * P

    def expand_b(bbt):
        return jnp.einsum("hcgp,Gg->cGhgp", bbt.reshape(H, n_cb, gpb, P), eye).reshape(n_cb, gpb * H, hw)

    def expand_c(cm):
        return jnp.einsum("cghp,gG->cgpGh", cm.reshape(n_cb, gpb, H, P), eye).reshape(n_cb, hw, gpb * H)

    wb = jnp.concatenate([expand_b(bbt_re), expand_b(bbt_im)], axis=2).astype(MXU)
    wc = jnp.concatenate([expand_c(ssm_c_re[0]), -expand_c(ssm_c_im[0])], axis=1).astype(MXU)
    pwr = pw_re.reshape(SUBLANES, n_cb, hw).transpose(1, 0, 2)
    pwi = pw_im.reshape(SUBLANES, n_cb, hw).transpose(1, 0, 2)
    rows = jnp.arange(SUBLANES)[None, :, None]

    def table(sign, reverse):
        tabs = []
        for d in (1, 2, 4):
            keep = (rows + d < SUBLANES) if reverse else (rows >= d)
            tabs.append(jnp.concatenate([jnp.where(keep, pwr[:, d - 1:d], 0.0),
                                         jnp.where(keep, sign * pwi[:, d - 1:d], 0.0)], axis=2))
        cr, ci = (pwr[:, ::-1], pwi[:, ::-1]) if reverse else (pwr, pwi)
        tabs.append(jnp.concatenate([cr, sign * ci], axis=2))
        return jnp.stack(tabs, axis=1)

    tab_f, tab_r = table(1.0, False), table(-1.0, True)
    dsk = ssm_d.reshape(n_cb, 1, LANES)
    lb = _small_call(_lb_fn, [hgrn_lb_logits], [((1, D), F32)], "hgrn_lower_bound")[0]

    z1 = _norm_call(h0, mix_norm_g, tm, "mix_norm")
    p = _mm_shard(z1, win_g, tm, "in_proj", False)
    p3 = p.reshape(B, L, p.shape[1])
    ya = _s5_fwd_call(p3, wb, wc, tab_f, dsk, "s5_fwd").reshape(T, W)
    yo, a_br = _glu_proj_call(ya, wglu_g, wsp_g, tm, "s5_glu_proj")
    yb = _hgrn_fwd_call(p3, lb, hgrn_norm_g, n_heads, n_cb, "hgrn_fwd").reshape(T, D)
    col_ga = 5
    h1, mg, bm, z2 = _merge_call(yb, a_br, p, h0, whp_g, wout_g, ffn_norm_g, col_ga, tm, "merge")
    up = _mm_shard(z2, wup_g, tm, "up_proj", True)
    act, dh2, loss_part, dg3 = _ffn_fwd_call(up, cw_g, cb_g, wdn_g, h1, tgt, final_norm_g.reshape(1, D),
                                             tm, tps, "ffn_out_loss")

    dua, dub, dwd, dcwa, dcwb, dcba, dcbb = _ffn_bwd_a_call(dh2, up, act, cw_g, cb_g, wdn_g, tm, tps, "ffn_bwd_gate")
    dupa, dupb, dh1, dg2 = _ffn_bwd_b_call(dua, dub, cw_g, wup_g, h1, ffn_norm_g, dh2, tm, tps, "ffn_bwd_up")
    dwup = jnp.concatenate([_mm_tn(z2, dupa, N_DEV // 2, tm, "dw_up_a", True),
                            _mm_tn(z2, dupb, N_DEV // 2, tm, "dw_up_b", True)], axis=0)
    dmg, dwout = _lin_bwd(mg, dh1, wout_g, tm, "out_proj_bwd")
    da_br, dbm, dga, dgb = _merge_bwd_call(dmg, a_br, bm, p, col_ga, tm, "merge_bwd")
    dyo, dwsp = _lin_bwd(yo, da_br, wsp_g, tm, "ssm_proj_bwd")
    dyb, dwhp = _lin_bwd(yb, dbm, whp_g, tm, "hgrn_proj_bwd")
    dya, dwglu = _glu_bwd_call(ya, dyo, wglu_g, tm, "s5_glu_bwd")
    du, dwb, dwc, dab, ddsk = _s5_bwd_call(p3, dya.reshape(B, L, W), wb, wc, tab_f, tab_r, dsk, "s5_bwd")
    dq, dfl, di, dog, dlb, dng = _hgrn_bwd_call(p3, dyb.reshape(B, L, D), lb, hgrn_norm_g, n_heads, n_cb, "hgrn_bwd")
    dp = jnp.concatenate([du.reshape(T, W), dq.reshape(T, D), dfl.reshape(T, D), di.reshape(T, D),
                          dog.reshape(T, D), dga, dgb], axis=1)
    dh0, dg1 = _in_bwd_call(dp, win_g, h0, mix_norm_g, dh1, tm, "in_proj_bwd")
    dwin = _mm_tn(z1, dp, N_DEV, tm, "dw_in", False)
    dh0_3 = dh0.reshape(B, L, D)
    grad_x = dh0_3[:, N_META:]
    dmeta = _meta_grad_call(dh0_3, "meta_grad")

    def diag_b(dw):
        return jnp.einsum("cGhgp,Gg->hcgp", dw.reshape(n_cb, gpb, H, gpb, P), eye).reshape(H, G * P)

    def diag_c(dw):
        return jnp.einsum("cgpGh,gG->cghp", dw.reshape(n_cb, gpb, P, gpb, H), eye).reshape(G, H, P)

    small_parts = [dg1, dab[:, 0, :hw].reshape(G, P), dab[:, 0, hw:].reshape(G, P),
                   diag_b(dwb[:, :, :hw]), diag_b(dwb[:, :, hw:]),
                   diag_c(dwc[:, :hw]), -diag_c(dwc[:, hw:]), ddsk.reshape(1, D), dlb,
                   dng, dg2,
                   jnp.concatenate([dcba, dcbb], axis=0).reshape(1, N_DEV * F), dg3, loss_part]
    small_shapes = [a.shape for a in small_parts]
    small_pack = _pack(small_parts)

    dcw = jnp.concatenate([dcwa, dcwb], axis=0)
    dmeta_s = dmeta.reshape(N_META, N_DEV, D // N_DEV).transpose(1, 0, 2)
    sh_rows = D // N_DEV
    parts = _exchange_call(
        [dwin.astype(WIRE), dwup.astype(WIRE), dwglu.reshape(N_DEV, sh_rows, D).astype(WIRE),
         dwsp.reshape(N_DEV, sh_rows, D).astype(WIRE), dwhp.reshape(N_DEV, sh_rows, D).astype(WIRE),
         dwout.reshape(N_DEV, sh_rows, D).astype(WIRE), dwd.reshape(N_DEV, w_down.shape[1], D).astype(WIRE),
         dmeta_s, dcw], True, "scatter_grads")
    small_all = _exchange_call([small_pack], False, "gather_small_grads")[0]

    def sum8(a):
        t = a[0]
        for s in range(1, N_DEV):
            t = t + a[s]
        return (t,)

    small_sum = _small_call(sum8, [small_all], [(small_pack.shape, F32)], "sum_small_grads")[0]
    (g_g1, t_abr, t_abi, t_bbr, t_bbi, g_cre, g_cim, g_dsk, t_lb, g_ng, g_g2, g_cb, g_g3, loss_v) = _unpack(
        small_sum, small_shapes)

    def disc_b_bwd(cr, ci, br, bi, dbr, dbi):
        _, vjp = jax.vjp(_disc_b, cr, ci, br, bi)
        return vjp((dbr, dbi))

    t_cr, t_ci, g_btr, g_bti = _small_call(
        disc_b_bwd, [coef_re.reshape(1, G * P), coef_im.reshape(1, G * P), bt_re, bt_im, t_bbr, t_bbi],
        [((1, G * P), F32)] * 2 + [((H, G * P), F32)] * 2, "s5_input_matrix_bwd")

    def disc_a_bwd(lr_, li_, ldt_, dar, dai, dcr, dci):
        _, vjp = jax.vjp(_disc_a, lr_, li_, ldt_)
        return vjp((dar, dai, dcr, dci))

    g_lr, g_li, g_ldt = _small_call(
        disc_a_bwd, [lr, li, ldt, t_abr, t_abi, t_cr.reshape(G, P), t_ci.reshape(G, P)],
        [((G, P), F32)] * 2 + [((G, 1), F32)], "s5_discretise_bwd")

    def lb_bwd(logits, d):
        _, vjp = jax.vjp(_lb_fn, logits)
        return vjp(d)

    g_lbl = _small_call(lb_bwd, [hgrn_lb_logits, t_lb], [(hgrn_lb_logits.shape, F32)], "hgrn_lower_bound_bwd")[0]

    grads = dict(
        mix_norm_g=g_g1, ssm_lambda_re=g_lr[None], ssm_lambda_im=g_li[None], ssm_log_dt=g_ldt.reshape(1, G),
        ssm_b_re=g_btr.reshape(H, G, P).transpose(1, 2, 0)[None], ssm_b_im=g_bti.reshape(H, G, P).transpose(1, 2, 0)[None],
        ssm_c_re=g_cre[None], ssm_c_im=g_cim[None], ssm_d=g_dsk, hgrn_lb_logits=g_lbl, hgrn_norm_g=g_ng,
        ffn_norm_g=g_g2, conv_b=g_cb.reshape(1, N_DEV * F), final_norm_g=g_g3.reshape(D))
    loss = loss_v[0, 0]

    delta, new_m, new_v = {}, {}, {}
    sharded = [("w_in", parts[0], (D, n_in)), ("w_up", parts[1], (D, F)), ("ssm_w_glu", parts[2], (sh_rows, D)),
               ("w_ssm_proj", parts[3], (sh_rows, D)), ("w_hgrn_proj", parts[4], (sh_rows, D)),
               ("w_out", parts[5], (sh_rows, D)), ("w_down", parts[6], (w_down.shape[1], D)),
               ("meta_tokens", parts[7], (N_META, D // N_DEV)), ("conv_w", parts[8], (3, F))]
    for name, part, shp in sharded:
        full = args[name].shape
        g, d_, nm, nv = _adamw_shard_call(args[name].reshape(shp), part, args["m_" + name].reshape(shp),
                                          args["v_" + name].reshape(shp), "adamw_" + name)
        grads[name], delta[name], new_m[name], new_v[name] = [t.reshape(full) for t in (g, d_, nm, nv)]

    rep = ["mix_norm_g", "ssm_lambda_re", "ssm_lambda_im", "ssm_log_dt", "ssm_b_re", "ssm_b_im", "ssm_c_re",
           "ssm_c_im", "ssm_d", "hgrn_lb_logits", "hgrn_norm_g", "ffn_norm_g", "conv_b", "final_norm_g"]
    rep_shapes = [args[n].shape for n in rep]
    packs = [_pack([args[pre + n] for n in rep]) for pre in ("", "m_", "v_")]
    g_pack = _pack([grads[n] for n in rep])
    outs = _small_call(lambda w, g, m, v: _adamw(w, g, m, v), [packs[0], g_pack, packs[1], packs[2]],
                       [(g_pack.shape, F32)] * 3, "adamw_replicated")
    for n, d_, nm, nv in zip(rep, *[_unpack(o, rep_shapes) for o in outs]):
        delta[n], new_m[n], new_v[n] = d_, nm, nv

    names = ["meta_tokens", "mix_norm_g", "w_in", "ssm_lambda_re", "ssm_lambda_im", "ssm_log_dt", "ssm_b_re",
             "ssm_b_im", "ssm_c_re", "ssm_c_im", "ssm_d", "ssm_w_glu", "w_ssm_proj", "hgrn_lb_logits", "hgrn_norm_g",
             "w_hgrn_proj", "w_out", "ffn_norm_g", "w_up", "conv_w", "conv_b", "w_down", "final_norm_g"]
    return (loss, grad_x, *[grads[n] for n in names], *[delta[n] for n in names],
            *[new_m[n] for n in names], *[new_v[n] for n in names])
```

```python
import functools

import jax
import jax.numpy as jnp
from jax import lax
from jax.experimental import pallas as pl
from jax.experimental.pallas import tpu as pltpu

F32 = jnp.float32
MXU = jnp.bfloat16
ACT = jnp.bfloat16
WIRE = jnp.bfloat16
N_DEV = 8
N_META = 16
CHUNK = 16
EPS = 1e-6
ADAM_LR, ADAM_B1, ADAM_B2, ADAM_EPS, ADAM_WD, ADAM_STEP = 0.001, 0.9, 0.999, 1e-08, 0.01, 10
SUBLANES = 8
LANES = 128
ROW_TILE_CAP = 700
VMEM_LIMIT = 60 * 1024 * 1024


def _cparams(**kw):
    return pltpu.CompilerParams(vmem_limit_bytes=VMEM_LIMIT, **kw)


def _tile(n, cap):
    best = None
    for t in range(16, min(n, cap) + 1, 16):
        if n % t == 0:
            best = t
    assert best is not None, (n, cap)
    return best


def _dot(a, b):
    return lax.dot_general(a.astype(MXU), b.astype(MXU), (((1,), (0,)), ((), ())), preferred_element_type=F32)


def _dot_nt(a, b):
    return lax.dot_general(a.astype(MXU), b.astype(MXU), (((1,), (1,)), ((), ())), preferred_element_type=F32)


def _dot_tn(a, b):
    return lax.dot_general(a.astype(MXU), b.astype(MXU), (((0,), (0,)), ((), ())), preferred_element_type=F32)


def _rms(x, g):
    return x * lax.rsqrt(jnp.mean(x * x, axis=-1, keepdims=True) + EPS) * g


def _silu(x):
    return x * jax.nn.sigmoid(x)


def _small_call(fn, ins, out_shapes, name):
    n_in = len(ins)

    def body(*refs):
        outs = fn(*[r[...] for r in refs[:n_in]])
        outs = outs if isinstance(outs, (tuple, list)) else (outs,)
        for r, o in zip(refs[n_in:], outs):
            r[...] = o.astype(r.dtype)

    vm = pl.BlockSpec(memory_space=pltpu.VMEM)
    return pl.pallas_call(
        body, name=name, out_shape=tuple(jax.ShapeDtypeStruct(s, d) for s, d in out_shapes),
        in_specs=[vm] * n_in, out_specs=tuple([vm] * len(out_shapes)), compiler_params=_cparams())(*ins)


def _disc_a(lr, li, ldt):
    dt = jnp.exp(ldt)
    mag = jnp.exp(lr * dt)
    ab_re = mag * jnp.cos(li * dt)
    ab_im = mag * jnp.sin(li * dt)
    den = lr * lr + li * li
    nr = ab_re - 1.0
    coef_re = (nr * lr + ab_im * li) / den
    coef_im = (ab_im * lr - nr * li) / den
    return ab_re, ab_im, coef_re, coef_im


def _disc_a_powers(lr, li, ldt):
    ab_re, ab_im, coef_re, coef_im = _disc_a(lr, li, ldt)
    pr, pi = [ab_re], [ab_im]
    for _ in range(SUBLANES - 1):
        pr, pi = pr + [pr[-1] * ab_re - pi[-1] * ab_im], pi + [pr[-1] * ab_im + pi[-1] * ab_re]
    return (*pr, *pi, coef_re, coef_im)


def _disc_b(coef_re, coef_im, bt_re, bt_im):
    return coef_re * bt_re - coef_im * bt_im, coef_re * bt_im + coef_im * bt_re


def _lb_fn(logits):
    return jax.nn.softmax(logits, axis=0)[0:1]


def _adamw(w, g, m, v):
    m = ADAM_B1 * m + (1.0 - ADAM_B1) * g
    v = ADAM_B2 * v + (1.0 - ADAM_B2) * jnp.square(g)
    m_hat = m / (1.0 - ADAM_B1 ** ADAM_STEP)
    v_hat = v / (1.0 - ADAM_B2 ** ADAM_STEP)
    delta = -ADAM_LR * (m_hat / (jnp.sqrt(v_hat) + ADAM_EPS) + ADAM_WD * w)
    return delta, m, v


def _norm_call(h, g, tm, name):
    T, D = h.shape

    def body(h_ref, g_ref, z_ref):
        z_ref[...] = _rms(h_ref[...], g_ref[...]).astype(ACT)

    return pl.pallas_call(
        body, name=name, out_shape=jax.ShapeDtypeStruct((T, D), ACT), grid=(T // tm,),
        in_specs=[pl.BlockSpec((tm, D), lambda i: (i, 0)), pl.BlockSpec((1, D), lambda i: (0, 0))],
        out_specs=pl.BlockSpec((tm, D), lambda i: (i, 0)), compiler_params=_cparams())(h, g)


def _mm_shard(x, w, tm, name, major):
    T, K = x.shape
    S, _, N = w.shape

    def body(x_ref, w_ref, o_ref):
        o_ref[...] = _dot(x_ref[...], w_ref[...]).astype(o_ref.dtype)

    if major:
        out_shape = jax.ShapeDtypeStruct((S, T, N), ACT)
        out_spec = pl.BlockSpec((None, tm, N), lambda j, i: (j, i, 0))
    else:
        out_shape = jax.ShapeDtypeStruct((T, S * N), ACT)
        out_spec = pl.BlockSpec((tm, N), lambda j, i: (i, j))
    return pl.pallas_call(
        body, name=name, out_shape=out_shape, grid=(S, T // tm),
        in_specs=[pl.BlockSpec((tm, K), lambda j, i: (i, 0)), pl.BlockSpec((None, K, N), lambda j, i: (j, 0, 0))],
        out_specs=out_spec, compiler_params=_cparams())(x, w)


def _mm_tn(x, y, n_shards, tm, name, major):
    T, K = x.shape
    S = n_shards
    N = y.shape[-1] if major else y.shape[-1] // S

    def body(x_ref, y_ref, o_ref):
        @pl.when(pl.program_id(1) == 0)
        def _():
            o_ref[...] = jnp.zeros_like(o_ref)
        o_ref[...] += _dot_tn(x_ref[...], y_ref[...])

    y_spec = (pl.BlockSpec((None, tm, N), lambda j, i: (j, i, 0)) if major
              else pl.BlockSpec((tm, N), lambda j, i: (i, j)))
    return pl.pallas_call(
        body, name=name, out_shape=jax.ShapeDtypeStruct((S, K, N), F32), grid=(S, T // tm),
        in_specs=[pl.BlockSpec((tm, K), lambda j, i: (i, 0)), y_spec],
        out_specs=pl.BlockSpec((None, K, N), lambda j, i: (j, 0, 0)), compiler_params=_cparams())(x, y)


def _lin_bwd(x, dy, w, tm, name):
    T, K = x.shape
    N = dy.shape[1]

    def body(x_ref, dy_ref, w_ref, dx_ref, dw_ref):
        @pl.when(pl.program_id(0) == 0)
        def _():
            dw_ref[...] = jnp.zeros_like(dw_ref)
        dy = dy_ref[...]
        dx_ref[...] = _dot_nt(dy, w_ref[...]).astype(dx_ref.dtype)
        dw_ref[...] += _dot_tn(x_ref[...], dy)

    return pl.pallas_call(
        body, name=name,
        out_shape=(jax.ShapeDtypeStruct((T, K), ACT), jax.ShapeDtypeStruct((K, N), F32)), grid=(T // tm,),
        in_specs=[pl.BlockSpec((tm, K), lambda i: (i, 0)), pl.BlockSpec((tm, N), lambda i: (i, 0)),
                  pl.BlockSpec((K, N), lambda i: (0, 0))],
        out_specs=(pl.BlockSpec((tm, K), lambda i: (i, 0)), pl.BlockSpec((K, N), lambda i: (0, 0))),
        compiler_params=_cparams())(x, dy, w)


def _scan_slabs(x_ref, tab_ref, n_slabs, reverse):
    hw = x_ref.shape[1] // 2
    tabs = [tab_ref[s] for s in range(4)]

    def cmul(t, xr, xi):
        tr, ti = t[:, :hw], t[:, hw:]
        return tr * xr - ti * xi, tr * xi + ti * xr

    def step(k, carry):
        cr, ci = carry
        kk = (n_slabs - 1 - k) if reverse else k
        r0 = pl.multiple_of(kk * SUBLANES, SUBLANES)
        x = x_ref[pl.ds(r0, SUBLANES), :]
        xr, xi = x[:, :hw], x[:, hw:]
        for s, d in enumerate((1, 2, 4)):
            sh = (SUBLANES - d) if reverse else d
            ar, ai = cmul(tabs[s], pltpu.roll(xr, sh, 0), pltpu.roll(xi, sh, 0))
            xr, xi = xr + ar, xi + ai
        pr, pi = cmul(tabs[3], cr, ci)
        xr, xi = xr + pr, xi + pi
        x_ref[pl.ds(r0, SUBLANES), 0:hw] = xr
        x_ref[pl.ds(r0, SUBLANES), hw:2 * hw] = xi
        e = 0 if reverse else SUBLANES - 1
        return xr[e:e + 1], xi[e:e + 1]

    z = jnp.zeros((1, hw), F32)
    lax.fori_loop(0, n_slabs, step, (z, z))


def _s5_fwd_call(p3, wb, wc, tab_f, dsk, name):
    B, L, _ = p3.shape
    n_cb, cw, sw = wb.shape

    def body(u_ref, wb_ref, wc_ref, tab_ref, d_ref, ya_ref, s_ref):
        u = u_ref[...]
        s_ref[...] = _dot(u, wb_ref[...])
        _scan_slabs(s_ref, tab_ref, L // SUBLANES, False)
        y = _dot(s_ref[...], wc_ref[...]) + d_ref[...] * u.astype(F32)
        ya_ref[...] = jax.nn.gelu(y).astype(ACT)

    return pl.pallas_call(
        body, name=name, out_shape=jax.ShapeDtypeStruct((B, L, n_cb * cw), ACT), grid=(B, n_cb),
        in_specs=[pl.BlockSpec((None, L, cw), lambda b, c: (b, 0, c)),
                  pl.BlockSpec((None, cw, sw), lambda b, c: (c, 0, 0)),
                  pl.BlockSpec((None, sw, cw), lambda b, c: (c, 0, 0)),
                  pl.BlockSpec((None, 4, SUBLANES, sw), lambda b, c: (c, 0, 0, 0)),
                  pl.BlockSpec((None, 1, cw), lambda b, c: (c, 0, 0))],
        out_specs=pl.BlockSpec((None, L, cw), lambda b, c: (b, 0, c)),
        scratch_shapes=[pltpu.VMEM((L, sw), F32)], compiler_params=_cparams())(p3, wb, wc, tab_f, dsk)


def _s5_bwd_call(p3, dya, wb, wc, tab_f, tab_r, dsk, name):
    B, L, _ = p3.shape
    n_cb, cw, sw = wb.shape
    hw = sw // 2
    n_slabs = L // SUBLANES

    def body(u_ref, dya_ref, wb_ref, wc_ref, tf_ref, tr_ref, d_ref,
             du_ref, dwb_ref, dwc_ref, da_ref, dd_ref, s_ref, l_ref):
        @pl.when(pl.program_id(1) == 0)
        def _():
            dwb_ref[...] = jnp.zeros_like(dwb_ref)
            dwc_ref[...] = jnp.zeros_like(dwc_ref)
            da_ref[...] = jnp.zeros_like(da_ref)
            dd_ref[...] = jnp.zeros_like(dd_ref)

        u = u_ref[...]
        uf = u.astype(F32)
        s_ref[...] = _dot(u, wb_ref[...])
        _scan_slabs(s_ref, tf_ref, n_slabs, False)
        y = _dot(s_ref[...], wc_ref[...]) + d_ref[...] * uf
        _, gelu_vjp = jax.vjp(jax.nn.gelu, y)
        dy = gelu_vjp(dya_ref[...].astype(F32))[0]
        dd_ref[...] += jnp.sum(dy * uf, axis=0, keepdims=True)
        l_ref[...] = _dot_nt(dy, wc_ref[...])
        _scan_slabs(l_ref, tr_ref, n_slabs, True)
        du_ref[...] = (_dot_nt(l_ref[...], wb_ref[...]) + d_ref[...] * dy).astype(ACT)
        dwb_ref[...] += _dot_tn(u, l_ref[...])
        dwc_ref[...] += _dot_tn(s_ref[...], dy)

        row = lax.broadcasted_iota(jnp.int32, (SUBLANES, hw), 0)

        def step(k, carry):
            pr, pi, accr, acci = carry
            r0 = pl.multiple_of(k * SUBLANES, SUBLANES)
            s = s_ref[pl.ds(r0, SUBLANES), :]
            lam = l_ref[pl.ds(r0, SUBLANES), :]
            sr, si = s[:, :hw], s[:, hw:]
            lr, li = lam[:, :hw], lam[:, hw:]
            qr = jnp.where(row == 0, pr, pltpu.roll(sr, 1, 0))
            qi = jnp.where(row == 0, pi, pltpu.roll(si, 1, 0))
            accr = accr + lr * qr + li * qi
            acci = acci + li * qr - lr * qi
            return sr[SUBLANES - 1:], si[SUBLANES - 1:], accr, acci

        z1 = jnp.zeros((1, hw), F32)
        z8 = jnp.zeros((SUBLANES, hw), F32)
        _, _, accr, acci = lax.fori_loop(0, n_slabs, step, (z1, z1, z8, z8))
        da_ref[...] += jnp.concatenate([jnp.sum(accr, axis=0, keepdims=True),
                                        jnp.sum(acci, axis=0, keepdims=True)], axis=1)

    W = n_cb * cw
    return pl.pallas_call(
        body, name=name,
        out_shape=(jax.ShapeDtypeStruct((B, L, W), ACT), jax.ShapeDtypeStruct((n_cb, cw, sw), F32),
                   jax.ShapeDtypeStruct((n_cb, sw, cw), F32), jax.ShapeDtypeStruct((n_cb, 1, sw), F32),
                   jax.ShapeDtypeStruct((n_cb, 1, cw), F32)),
        grid=(n_cb, B),
        in_specs=[pl.BlockSpec((None, L, cw), lambda c, b: (b, 0, c)),
                  pl.BlockSpec((None, L, cw), lambda c, b: (b, 0, c)),
                  pl.BlockSpec((None, cw, sw), lambda c, b: (c, 0, 0)),
                  pl.BlockSpec((None, sw, cw), lambda c, b: (c, 0, 0)),
                  pl.BlockSpec((None, 4, SUBLANES, sw), lambda c, b: (c, 0, 0, 0)),
                  pl.BlockSpec((None, 4, SUBLANES, sw), lambda c, b: (c, 0, 0, 0)),
                  pl.BlockSpec((None, 1, cw), lambda c, b: (c, 0, 0))],
        out_specs=(pl.BlockSpec((None, L, cw), lambda c, b: (b, 0, c)),
                   pl.BlockSpec((None, cw, sw), lambda c, b: (c, 0, 0)),
                   pl.BlockSpec((None, sw, cw), lambda c, b: (c, 0, 0)),
                   pl.BlockSpec((None, 1, sw), lambda c, b: (c, 0, 0)),
                   pl.BlockSpec((None, 1, cw), lambda c, b: (c, 0, 0))),
        scratch_shapes=[pltpu.VMEM((L, sw), F32), pltpu.VMEM((L, sw), F32)],
        compiler_params=_cparams())(p3, dya, wb, wc, tab_f, tab_r, dsk)


def _glu_proj_call(ya, wglu, wproj, tm, name):
    T, W = ya.shape
    D = wproj.shape[1]

    def body(ya_ref, wg_ref, wp_ref, yo_ref, a_ref):
        ya = ya_ref[...]
        yo = ya.astype(F32) * jax.nn.sigmoid(_dot(ya, wg_ref[...]))
        yo_ref[...] = yo.astype(ACT)
        a_ref[...] = _dot(yo, wp_ref[...]).astype(ACT)

    return pl.pallas_call(
        body, name=name, out_shape=(jax.ShapeDtypeStruct((T, W), ACT), jax.ShapeDtypeStruct((T, D), ACT)),
        grid=(T // tm,),
        in_specs=[pl.BlockSpec((tm, W), lambda i: (i, 0)), pl.BlockSpec((W, W), lambda i: (0, 0)),
                  pl.BlockSpec((W, D), lambda i: (0, 0))],
        out_specs=(pl.BlockSpec((tm, W), lambda i: (i, 0)), pl.BlockSpec((tm, D), lambda i: (i, 0))),
        compiler_params=_cparams())(ya, wglu, wproj)


def _glu_bwd_call(ya, dyo, wglu, tm, name):
    T, W = ya.shape

    def body(ya_ref, dyo_ref, wg_ref, dya_ref, dwg_ref):
        @pl.when(pl.program_id(0) == 0)
        def _():
            dwg_ref[...] = jnp.zeros_like(dwg_ref)
        ya = ya_ref[...]
        yaf = ya.astype(F32)
        dyo = dyo_ref[...].astype(F32)
        sg = jax.nn.sigmoid(_dot(ya, wg_ref[...]))
        dt = dyo * yaf * sg * (1.0 - sg)
        dya_ref[...] = (dyo * sg + _dot_nt(dt, wg_ref[...])).astype(ACT)
        dwg_ref[...] += _dot_tn(ya, dt)

    return pl.pallas_call(
        body, name=name, out_shape=(jax.ShapeDtypeStruct((T, W), ACT), jax.ShapeDtypeStruct((W, W), F32)),
        grid=(T // tm,),
        in_specs=[pl.BlockSpec((tm, W), lambda i: (i, 0)), pl.BlockSpec((tm, W), lambda i: (i, 0)),
                  pl.BlockSpec((W, W), lambda i: (0, 0))],
        out_specs=(pl.BlockSpec((tm, W), lambda i: (i, 0)), pl.BlockSpec((W, W), lambda i: (0, 0))),
        compiler_params=_cparams())(ya, dyo, wglu)


PAD = 16


def _chunk_cumsums(x, pad_ref, L):
    row = lax.broadcasted_iota(jnp.int32, x.shape, 0) % CHUNK
    zeros = jnp.zeros((PAD, x.shape[1]), F32)
    pad_ref[0:PAD, :] = zeros
    pad_ref[PAD + L:2 * PAD + L, :] = zeros
    c = x
    r = x
    d = 1
    while d < CHUNK:
        pad_ref[PAD:PAD + L, :] = c
        c = c + jnp.where(row >= d, pad_ref[PAD - d:PAD - d + L, :], 0.0)
        pad_ref[PAD:PAD + L, :] = r
        r = r + jnp.where(row + d < CHUNK, pad_ref[PAD + d:PAD + d + L, :], 0.0)
        d *= 2
    return c, r - x


def _hgrn_prep(q_ref, fl_ref, lb_ref, pad_ref, L):
    lb = lb_ref[...]
    sig = jax.nn.sigmoid(fl_ref[...].astype(F32))
    f = lb + (1.0 - lb) * sig
    k = 1.0 - f
    c, rc = _chunk_cumsums(jnp.log(f), pad_ref, L)
    e_in, e_inv, e_out = jnp.exp(c), jnp.exp(-c), jnp.exp(rc)
    q = q_ref[...].astype(F32)
    return dict(sig=sig, f=f, k=k, q=q, e_in=e_in, e_inv=e_inv, e_out=e_out, dec=jnp.exp(c + rc))


def _chunk_mask(rb):
    r = lax.broadcasted_iota(jnp.int32, (rb, rb), 0)
    c = lax.broadcasted_iota(jnp.int32, (rb, rb), 1)
    return (r // CHUNK == c // CHUNK) & (c <= r)


def _hg_out(o, og, g):
    on = o * lax.rsqrt(jnp.mean(o * o, axis=-1, keepdims=True) + EPS) * g
    return on * _silu(og)


def _hgrn_specs(L, hd, col_q, n_heads, order):
    def spec(sec):
        return pl.BlockSpec((None, L, hd), lambda *g: (order(*g)[0], 0, col_q + sec * n_heads + order(*g)[1]))
    return [spec(0), spec(1), spec(2), spec(3)]


GROUP = 128
CPG = GROUP // CHUNK


def _expand(x):
    xf = x.astype(F32)
    chunk = lax.broadcasted_iota(jnp.int32, xf.shape, 0) // CHUNK
    return jnp.concatenate([jnp.where(chunk == j, xf, 0.0) for j in range(CPG)], axis=1)


def _store_padded(ref, val, L, fill):
    ref[0:L, :] = val.astype(ref.dtype)
    if ref.shape[0] > L:
        ref[L:ref.shape[0], :] = jnp.full((ref.shape[0] - L, ref.shape[1]), fill, ref.dtype)


def _hgrn_forward_core(q_ref, fl_ref, v_ref, lb_ref, pad_ref, qin_ref, kin_ref, kout_ref, vp_ref, dec_ref, o_ref,
                       s_ref, L):
    hd = qin_ref.shape[1]
    n_groups = qin_ref.shape[0] // GROUP
    pp = _hgrn_prep(q_ref, fl_ref, lb_ref, pad_ref, L)
    _store_padded(qin_ref, pp["q"] * pp["e_in"], L, 0.0)
    _store_padded(kin_ref, pp["k"] * pp["e_inv"], L, 0.0)
    _store_padded(kout_ref, pp["k"] * pp["e_out"], L, 0.0)
    _store_padded(vp_ref, v_ref[...], L, 0.0)
    _store_padded(dec_ref, pp["dec"], L, 1.0)
    mask = _chunk_mask(GROUP)

    def intra(g, carry):
        rows = pl.ds(pl.multiple_of(g * GROUP, GROUP), GROUP)
        a = jnp.where(mask, _dot_nt(qin_ref[rows, :], kin_ref[rows, :]), 0.0)
        o_ref[rows, :] = _dot(a, vp_ref[rows, :])
        kv = _dot_tn(vp_ref[rows, :], _expand(kout_ref[rows, :]))
        for j in range(CPG):
            s_ref[g * CPG + j] = kv[:, j * hd:(j + 1) * hd]
        return carry

    lax.fori_loop(0, n_groups, intra, 0)

    def rec(n, st):
        kv = s_ref[n]
        s_ref[n] = st
        dec = dec_ref[pl.ds(pl.multiple_of(n * CHUNK, CHUNK), SUBLANES), :][0:1]
        return st * dec + kv

    lax.fori_loop(0, L // CHUNK, rec, jnp.zeros((hd, hd), F32))

    def inter(g, carry):
        rows = pl.ds(pl.multiple_of(g * GROUP, GROUP), GROUP)
        scat = jnp.concatenate([s_ref[g * CPG + j] for j in range(CPG)], axis=1)
        o_ref[rows, :] += _dot_nt(_expand(qin_ref[rows, :]), scat)
        return carry

    lax.fori_loop(0, n_groups, inter, 0)
    return pp


def _hgrn_scratch(L, hd):
    lp = -(-L // GROUP) * GROUP
    return lp, [pltpu.VMEM((L + 2 * PAD, hd), F32), pltpu.VMEM((lp, hd), MXU), pltpu.VMEM((lp, hd), MXU),
                pltpu.VMEM((lp, hd), MXU), pltpu.VMEM((lp, hd), MXU), pltpu.VMEM((lp, hd), F32),
                pltpu.VMEM((lp, hd), F32), pltpu.VMEM((lp // CHUNK, hd, hd), F32)]


def _hgrn_fwd_call(p3, lb, ng, n_heads, col_q, name):
    B, L, _ = p3.shape
    hd = ng.shape[1]
    _, scratch = _hgrn_scratch(L, hd)

    def body(q_ref, fl_ref, v_ref, og_ref, lb_ref, ng_ref, yb_ref,
             pad_ref, qin_ref, kin_ref, kout_ref, vp_ref, dec_ref, o_ref, s_ref):
        _hgrn_forward_core(q_ref, fl_ref, v_ref, lb_ref, pad_ref, qin_ref, kin_ref, kout_ref, vp_ref, dec_ref,
                           o_ref, s_ref, L)
        yb_ref[...] = _hg_out(o_ref[0:L, :], og_ref[...].astype(F32), ng_ref[...]).astype(ACT)

    order = lambda b, h: (b, h)
    return pl.pallas_call(
        body, name=name, out_shape=jax.ShapeDtypeStruct((B, L, n_heads * hd), ACT), grid=(B, n_heads),
        in_specs=_hgrn_specs(L, hd, col_q, n_heads, order) + [
            pl.BlockSpec((1, hd), lambda b, h: (0, h)), pl.BlockSpec((1, hd), lambda b, h: (0, 0))],
        out_specs=pl.BlockSpec((None, L, hd), lambda b, h: (b, 0, h)),
        scratch_shapes=scratch, compiler_params=_cparams())(p3, p3, p3, p3, lb, ng)


def _hgrn_bwd_call(p3, dyb, lb, ng, n_heads, col_q, name):
    B, L, _ = p3.shape
    hd = ng.shape[1]
    n_chunks = L // CHUNK
    lp, scratch = _hgrn_scratch(L, hd)
    n_groups = lp // GROUP

    def body(q_ref, fl_ref, v_ref, og_ref, dyb_ref, lb_ref, ng_ref,
             dq_ref, dfl_ref, dv_ref, dog_ref, dlb_ref, dng_ref,
             pad_ref, qin_ref, kin_ref, kout_ref, vp_ref, dec_ref, o_ref, s_ref,
             do_ref, ds_ref, dqi_ref, dki_ref, dko_ref, dvv_ref, dct_ref):
        @pl.when(pl.program_id(1) == 0)
        def _():
            dlb_ref[...] = jnp.zeros_like(dlb_ref)

        @pl.when((pl.program_id(0) == 0) & (pl.program_id(1) == 0))
        def _():
            dng_ref[...] = jnp.zeros_like(dng_ref)

        pp = _hgrn_forward_core(q_ref, fl_ref, v_ref, lb_ref, pad_ref, qin_ref, kin_ref, kout_ref, vp_ref, dec_ref,
                                o_ref, s_ref, L)

        og = og_ref[...].astype(F32)
        _, out_vjp = jax.vjp(_hg_out, o_ref[0:L, :], og, ng_ref[...])
        d_o, d_og, d_ng = out_vjp(dyb_ref[...].astype(F32))
        dog_ref[...] = d_og.astype(ACT)
        dng_ref[...] += d_ng
        _store_padded(do_ref, d_o, L, 0.0)
        mask = _chunk_mask(GROUP)

        def grads_a(g, carry):
            rows = pl.ds(pl.multiple_of(g * GROUP, GROUP), GROUP)
            qi, ki, vv, do = qin_ref[rows, :], kin_ref[rows, :], vp_ref[rows, :], do_ref[rows, :]
            a = jnp.where(mask, _dot_nt(qi, ki), 0.0)
            da = jnp.where(mask, _dot_nt(do, vv), 0.0)
            sstack = s_ref[pl.ds(g * CPG, CPG)].reshape(CPG * hd, hd)
            dqi_ref[rows, :] = _dot(da, ki) + _dot(_expand(do), sstack)
            dki_ref[rows, :] = _dot_tn(da, qi)
            dvv_ref[rows, :] = _dot_tn(a, do)
            x = _dot_tn(do, _expand(qi))
            for j in range(CPG):
                ds_ref[g * CPG + j] = x[:, j * hd:(j + 1) * hd]
            return carry

        lax.fori_loop(0, n_groups, grads_a, 0)

        def rec_bwd(k, dst):
            n = n_chunks - 1 - k
            r0 = pl.multiple_of(n * CHUNK, CHUNK)
            x = ds_ref[n]
            ds_ref[n] = dst
            dec = dec_ref[pl.ds(r0, SUBLANES), :][0:1]
            ddec = dec * jnp.sum(dst * s_ref[n], axis=0, keepdims=True)
            dct_ref[pl.ds(r0, CHUNK), :] = jnp.broadcast_to(ddec, (CHUNK, hd))
            return dst * dec + x

        lax.fori_loop(0, n_chunks, rec_bwd, jnp.zeros((hd, hd), F32))

        def grads_b(g, carry):
            rows = pl.ds(pl.multiple_of(g * GROUP, GROUP), GROUP)
            dscat = jnp.concatenate([ds_ref[g * CPG + j] for j in range(CPG)], axis=1)
            dvv_ref[rows, :] += _dot_nt(_expand(kout_ref[rows, :]), dscat)
            dstack = ds_ref[pl.ds(g * CPG, CPG)].reshape(CPG * hd, hd)
            dko_ref[rows, :] = _dot(_expand(vp_ref[rows, :]), dstack)
            return carry

        lax.fori_loop(0, n_groups, grads_b, 0)

        dqi, dki, dko = dqi_ref[0:L, :], dki_ref[0:L, :], dko_ref[0:L, :]
        dq = dqi * pp["e_in"]
        dk = dki * pp["e_inv"] + dko * pp["e_out"]
        dq_ref[...] = dq.astype(ACT)
        dv_ref[...] = dvv_ref[0:L, :].astype(ACT)
        t_out = pp["k"] * pp["e_out"] * dko
        dc = pp["q"] * pp["e_in"] * dqi - pp["k"] * pp["e_inv"] * dki - t_out
        _, dc_later = _chunk_cumsums(dc, pad_ref, L)
        t_incl, t_later = _chunk_cumsums(t_out, pad_ref, L)
        dlogf = dc + dc_later + t_incl + t_later + dct_ref[0:L, :]
        df = dlogf / pp["f"] - dk
        lbv = lb_ref[...]
        sig = pp["sig"]
        dfl_ref[...] = (df * (1.0 - lbv) * sig * (1.0 - sig)).astype(ACT)
        dlb_ref[...] += jnp.sum(df * (1.0 - sig), axis=0, keepdims=True)

    order = lambda h, b: (b, h)
    W = n_heads * hd
    act_out = jax.ShapeDtypeStruct((B, L, W), ACT)
    blk_out = pl.BlockSpec((None, L, hd), lambda h, b: (b, 0, h))
    return pl.pallas_call(
        body, name=name,
        out_shape=(act_out, act_out, act_out, act_out, jax.ShapeDtypeStruct((1, W), F32),
                   jax.ShapeDtypeStruct((1, hd), F32)),
        grid=(n_heads, B),
        in_specs=_hgrn_specs(L, hd, col_q, n_heads, order) + [
            pl.BlockSpec((None, L, hd), lambda h, b: (b, 0, h)),
            pl.BlockSpec((1, hd), lambda h, b: (0, h)), pl.BlockSpec((1, hd), lambda h, b: (0, 0))],
        out_specs=(blk_out, blk_out, blk_out, blk_out, pl.BlockSpec((1, hd), lambda h, b: (0, h)),
                   pl.BlockSpec((1, hd), lambda h, b: (0, 0))),
        scratch_shapes=scratch + [
            pltpu.VMEM((lp, hd), MXU), pltpu.VMEM((lp // CHUNK, hd, hd), F32)] + [pltpu.VMEM((lp, hd), F32)] * 5,
        compiler_params=_cparams())(p3, p3, p3, p3, dyb, lb, ng)


def _merge_fn(a, bm, ga, gb):
    return jax.nn.sigmoid(ga) * a + jax.nn.sigmoid(gb) * bm


def _merge_call(yb, a, p, h0, whp, wout, g2, col_ga, tm, name):
    T, D = h0.shape

    def body(yb_ref, a_ref, ga_ref, gb_ref, h0_ref, whp_ref, wout_ref, g2_ref, h1_ref, mg_ref, bm_ref, z2_ref):
        bm = _dot(yb_ref[...], whp_ref[...])
        mg = _merge_fn(a_ref[...].astype(F32), bm, ga_ref[...].astype(F32), gb_ref[...].astype(F32))
        h1 = h0_ref[...] + _dot(mg, wout_ref[...])
        h1_ref[...] = h1
        mg_ref[...] = mg.astype(ACT)
        bm_ref[...] = bm.astype(ACT)
        z2_ref[...] = _rms(h1, g2_ref[...]).astype(ACT)

    tile = pl.BlockSpec((tm, D), lambda i: (i, 0))
    full = pl.BlockSpec((D, D), lambda i: (0, 0))
    act = jax.ShapeDtypeStruct((T, D), ACT)
    return pl.pallas_call(
        body, name=name, out_shape=(jax.ShapeDtypeStruct((T, D), F32), act, act, act), grid=(T // tm,),
        in_specs=[tile, tile, pl.BlockSpec((tm, D), lambda i: (i, col_ga)),
                  pl.BlockSpec((tm, D), lambda i: (i, col_ga + 1)), tile, full, full,
                  pl.BlockSpec((1, D), lambda i: (0, 0))],
        out_specs=(tile, tile, tile, tile), compiler_params=_cparams())(yb, a, p, p, h0, whp, wout, g2)


def _merge_bwd_call(dmg, a, bm, p, col_ga, tm, name):
    T, D = dmg.shape

    def body(dmg_ref, a_ref, bm_ref, ga_ref, gb_ref, da_ref, dbm_ref, dga_ref, dgb_ref):
        args = [r[...].astype(F32) for r in (a_ref, bm_ref, ga_ref, gb_ref)]
        _, vjp = jax.vjp(_merge_fn, *args)
        for r, o in zip((da_ref, dbm_ref, dga_ref, dgb_ref), vjp(dmg_ref[...].astype(F32))):
            r[...] = o.astype(ACT)

    tile = pl.BlockSpec((tm, D), lambda i: (i, 0))
    act = jax.ShapeDtypeStruct((T, D), ACT)
    return pl.pallas_call(
        body, name=name, out_shape=(act, act, act, act), grid=(T // tm,),
        in_specs=[tile, tile, tile, pl.BlockSpec((tm, D), lambda i: (i, col_ga)),
                  pl.BlockSpec((tm, D), lambda i: (i, col_ga + 1))],
        out_specs=(tile, tile, tile, tile), compiler_params=_cparams())(dmg, a, bm, p, p)


def _conv_taps(x_ref, halo_ref, ext_ref, edge, tm, before):
    halo = jnp.where(edge, 0.0, halo_ref[...].astype(F32))
    x = x_ref[...].astype(F32)
    if before:
        ext_ref[0:PAD, :] = halo
        ext_ref[PAD:PAD + tm, :] = x
        return [ext_ref[PAD - 2 + k:PAD - 2 + k + tm, :] for k in range(3)]
    ext_ref[0:tm, :] = x
    ext_ref[tm:tm + PAD, :] = halo
    return [ext_ref[k:k + tm, :] for k in range(3)]


def _conv(taps, cw, cb):
    return cb + cw[0:1] * taps[0] + cw[1:2] * taps[1] + cw[2:3] * taps[2]


def _ffn_pair_specs(tm, F, T, n_pairs, order, before):
    hb = tm // PAD
    last = T // PAD - 1

    def halo_row(i):
        return jnp.maximum(i * hb - 1, 0) if before else jnp.minimum((i + 1) * hb, last)

    specs = []
    for off in (0, n_pairs):
        specs.append(pl.BlockSpec((None, tm, F), lambda *g, off=off: (order(*g)[1] + off, order(*g)[0], 0)))
        specs.append(pl.BlockSpec((None, PAD, F), lambda *g, off=off: (order(*g)[1] + off, halo_row(order(*g)[0]), 0)))
    return specs


def _ffn_fwd_call(up, cw, cb, wd, h1, tgt, g3, tm, tps, name):
    S, T, F = up.shape
    n_pairs = S // 2
    D = h1.shape[1]

    def body(ua_ref, ha_ref, ub_ref, hb_ref, cwa_ref, cwb_ref, cba_ref, cbb_ref, wd_ref, h1_ref, tgt_ref, g3_ref,
             act_ref, dh2_ref, loss_ref, dg3_ref, acc_ref, ext_ref):
        i, j = pl.program_id(0), pl.program_id(1)
        edge = (i % tps) == 0
        ua = _conv(_conv_taps(ua_ref, ha_ref, ext_ref, edge, tm, True), cwa_ref[...], cba_ref[...])
        ub = _conv(_conv_taps(ub_ref, hb_ref, ext_ref, edge, tm, True), cwb_ref[...], cbb_ref[...])
        act = _silu(ua) * ub
        act_ref[...] = act.astype(ACT)
        contrib = _dot(act, wd_ref[...])

        @pl.when(j == 0)
        def _():
            acc_ref[...] = h1_ref[...] + contrib

        @pl.when(j > 0)
        def _():
            acc_ref[...] += contrib

        @pl.when((i == 0) & (j == 0))
        def _():
            loss_ref[...] = jnp.zeros_like(loss_ref)
            dg3_ref[...] = jnp.zeros_like(dg3_ref)

        @pl.when(j == n_pairs - 1)
        def _():
            row = lax.broadcasted_iota(jnp.int32, (tm, 1), 0) + (i % tps) * tm
            valid = row >= N_META
            tgt = tgt_ref[...]

            def loss_fn(h2, g):
                err = _rms(h2, g) - tgt
                return 0.5 * jnp.sum(jnp.where(valid, err * err, 0.0)) / D

            loss, vjp = jax.vjp(loss_fn, acc_ref[...], g3_ref[...])
            dh2, dg3 = vjp(jnp.ones((), F32))
            dh2_ref[...] = dh2
            loss_ref[...] += loss
            dg3_ref[...] += dg3

    order = lambda i, j: (i, j)
    tile = pl.BlockSpec((tm, D), lambda i, j: (i, 0))
    vec = pl.BlockSpec((1, D), lambda i, j: (0, 0))
    return pl.pallas_call(
        body, name=name,
        out_shape=(jax.ShapeDtypeStruct((n_pairs, T, F), ACT), jax.ShapeDtypeStruct((T, D), F32),
                   jax.ShapeDtypeStruct((1, LANES), F32), jax.ShapeDtypeStruct((1, D), F32)),
        grid=(T // tm, n_pairs),
        in_specs=_ffn_pair_specs(tm, F, T, n_pairs, order, True) + [
            pl.BlockSpec((None, 3, F), lambda i, j: (j, 0, 0)), pl.BlockSpec((None, 3, F), lambda i, j: (j + n_pairs, 0, 0)),
            pl.BlockSpec((None, 1, F), lambda i, j: (j, 0, 0)), pl.BlockSpec((None, 1, F), lambda i, j: (j + n_pairs, 0, 0)),
            pl.BlockSpec((None, F, D), lambda i, j: (j, 0, 0)), tile, tile, vec],
        out_specs=(pl.BlockSpec((None, tm, F), lambda i, j: (j, i, 0)), tile,
                   pl.BlockSpec((1, LANES), lambda i, j: (0, 0)), vec),
        scratch_shapes=[pltpu.VMEM((tm, D), F32), pltpu.VMEM((tm + PAD, F), F32)],
        compiler_params=_cparams())(up, up, up, up, cw, cw, cb, cb, wd, h1, tgt, g3)


def _ffn_bwd_a_call(dh2, up, act, cw, cb, wd, tm, tps, name):
    S, T, F = up.shape
    n_pairs = S // 2
    D = dh2.shape[1]

    def body(dh2_ref, ua_ref, ha_ref, ub_ref, hb_ref, act_ref, cwa_ref, cwb_ref, cba_ref, cbb_ref, wd_ref,
             dua_ref, dub_ref, dwd_ref, dcwa_ref, dcwb_ref, dcba_ref, dcbb_ref, ext_ref):
        i = pl.program_id(1)
        edge = (i % tps) == 0

        @pl.when(i == 0)
        def _():
            for r in (dwd_ref, dcwa_ref, dcwb_ref, dcba_ref, dcbb_ref):
                r[...] = jnp.zeros_like(r)

        dh2 = dh2_ref[...]
        dact = _dot_nt(dh2, wd_ref[...])
        dwd_ref[...] += _dot_tn(act_ref[...], dh2)
        taps_a = _conv_taps(ua_ref, ha_ref, ext_ref, edge, tm, True)
        ua = _conv(taps_a, cwa_ref[...], cba_ref[...])
        sa = jax.nn.sigmoid(ua)
        dub = dact * ua * sa
        dcbb_ref[...] += jnp.sum(dub, axis=0, keepdims=True)
        taps_b = _conv_taps(ub_ref, hb_ref, ext_ref, edge, tm, True)
        dcwb_ref[...] += jnp.concatenate([jnp.sum(dub * t, axis=0, keepdims=True) for t in taps_b], axis=0)
        ub = _conv(taps_b, cwb_ref[...], cbb_ref[...])
        dua = dact * ub * sa * (1.0 + ua * (1.0 - sa))
        dcba_ref[...] += jnp.sum(dua, axis=0, keepdims=True)
        taps_a = _conv_taps(ua_ref, ha_ref, ext_ref, edge, tm, True)
        dcwa_ref[...] += jnp.concatenate([jnp.sum(dua * t, axis=0, keepdims=True) for t in taps_a], axis=0)
        dua_ref[...] = dua.astype(ACT)
        dub_ref[...] = dub.astype(ACT)

    order = lambda j, i: (i, j)
    sh = lambda rows: jax.ShapeDtypeStruct((n_pairs, rows, F), F32)
    par = lambda rows: pl.BlockSpec((None, rows, F), lambda j, i: (j, 0, 0))
    return pl.pallas_call(
        body, name=name,
        out_shape=(jax.ShapeDtypeStruct((n_pairs, T, F), ACT), jax.ShapeDtypeStruct((n_pairs, T, F), ACT),
                   jax.ShapeDtypeStruct((n_pairs, F, D), F32), sh(3), sh(3), sh(1), sh(1)),
        grid=(n_pairs, T // tm),
        in_specs=[pl.BlockSpec((tm, D), lambda j, i: (i, 0))] + _ffn_pair_specs(tm, F, T, n_pairs, order, True) + [
            pl.BlockSpec((None, tm, F), lambda j, i: (j, i, 0)),
            pl.BlockSpec((None, 3, F), lambda j, i: (j, 0, 0)), pl.BlockSpec((None, 3, F), lambda j, i: (j + n_pairs, 0, 0)),
            pl.BlockSpec((None, 1, F), lambda j, i: (j, 0, 0)), pl.BlockSpec((None, 1, F), lambda j, i: (j + n_pairs, 0, 0)),
            pl.BlockSpec((None, F, D), lambda j, i: (j, 0, 0))],
        out_specs=(pl.BlockSpec((None, tm, F), lambda j, i: (j, i, 0)), pl.BlockSpec((None, tm, F), lambda j, i: (j, i, 0)),
                   pl.BlockSpec((None, F, D), lambda j, i: (j, 0, 0)), par(3), par(3), par(1), par(1)),
        scratch_shapes=[pltpu.VMEM((tm + PAD, F), F32)],
        compiler_params=_cparams())(dh2, up, up, up, up, act, cw, cw, cb, cb, wd)


def _ffn_bwd_b_call(dua, dub, cw, wup, h1, g2, dh2, tm, tps, name):
    n_pairs, T, F = dua.shape
    D = h1.shape[1]
    hb = tm // PAD
    last = T // PAD - 1

    def body(da_ref, na_ref, db_ref, nb_ref, cwa_ref, cwb_ref, wa_ref, wb_ref, h1_ref, g2_ref, dh2_ref,
             dupa_ref, dupb_ref, dh1_ref, dg2_ref, acc_ref, ext_ref):
        i, j = pl.program_id(0), pl.program_id(1)
        edge = (i % tps) == tps - 1
        outs = []
        for d_ref, n_ref, cw_ref, o_ref in ((da_ref, na_ref, cwa_ref, dupa_ref), (db_ref, nb_ref, cwb_ref, dupb_ref)):
            t = _conv_taps(d_ref, n_ref, ext_ref, edge, tm, False)
            cwv = cw_ref[...]
            dup = cwv[2:3] * t[0] + cwv[1:2] * t[1] + cwv[0:1] * t[2]
            o_ref[...] = dup.astype(ACT)
            outs.append(dup)
        contrib = _dot_nt(outs[0], wa_ref[...]) + _dot_nt(outs[1], wb_ref[...])

        @pl.when(j == 0)
        def _():
            acc_ref[...] = contrib

        @pl.when(j > 0)
        def _():
            acc_ref[...] += contrib

        @pl.when((i == 0) & (j == 0))
        def _():
            dg2_ref[...] = jnp.zeros_like(dg2_ref)

        @pl.when(j == n_pairs - 1)
        def _():
            _, vjp = jax.vjp(_rms, h1_ref[...], g2_ref[...])
            dh, dg = vjp(acc_ref[...])
            dh1_ref[...] = dh2_ref[...] + dh
            dg2_ref[...] += dg

    tile = pl.BlockSpec((tm, D), lambda i, j: (i, 0))
    vec = pl.BlockSpec((1, D), lambda i, j: (0, 0))
    pair = lambda: [pl.BlockSpec((None, tm, F), lambda i, j: (j, i, 0)),
                    pl.BlockSpec((None, PAD, F), lambda i, j: (j, jnp.minimum((i + 1) * hb, last), 0))]
    act = jax.ShapeDtypeStruct((n_pairs, T, F), ACT)
    return pl.pallas_call(
        body, name=name,
        out_shape=(act, act, jax.ShapeDtypeStruct((T, D), F32), jax.ShapeDtypeStruct((1, D), F32)),
        grid=(T // tm, n_pairs),
        in_specs=pair() + pair() + [
            pl.BlockSpec((None, 3, F), lambda i, j: (j, 0, 0)), pl.BlockSpec((None, 3, F), lambda i, j: (j + n_pairs, 0, 0)),
            pl.BlockSpec((None, D, F), lambda i, j: (j, 0, 0)), pl.BlockSpec((None, D, F), lambda i, j: (j + n_pairs, 0, 0)),
            tile, vec, tile],
        out_specs=(pl.BlockSpec((None, tm, F), lambda i, j: (j, i, 0)), pl.BlockSpec((None, tm, F), lambda i, j: (j, i, 0)),
                   tile, vec),
        scratch_shapes=[pltpu.VMEM((tm, D), F32), pltpu.VMEM((tm + PAD, F), F32)],
        compiler_params=_cparams())(dua, dua, dub, dub, cw, cw, wup, wup, h1, g2, dh2)


def _in_bwd_call(dp, w_in, h0, g1, dh1, tm, name):
    T, D = h0.shape
    S, _, N = w_in.shape

    def body(dp_ref, w_ref, h0_ref, g1_ref, dh1_ref, dh0_ref, dg1_ref, acc_ref):
        i, j = pl.program_id(0), pl.program_id(1)
        contrib = _dot_nt(dp_ref[...], w_ref[...])

        @pl.when(j == 0)
        def _():
            acc_ref[...] = contrib

        @pl.when(j > 0)
        def _():
            acc_ref[...] += contrib

        @pl.when((i == 0) & (j == 0))
        def _():
            dg1_ref[...] = jnp.zeros_like(dg1_ref)

        @pl.when(j == S - 1)
        def _():
            _, vjp = jax.vjp(_rms, h0_ref[...], g1_ref[...])
            dh, dg = vjp(acc_ref[...])
            dh0_ref[...] = dh1_ref[...] + dh
            dg1_ref[...] += dg

    tile = pl.BlockSpec((tm, D), lambda i, j: (i, 0))
    vec = pl.BlockSpec((1, D), lambda i, j: (0, 0))
    return pl.pallas_call(
        body, name=name, out_shape=(jax.ShapeDtypeStruct((T, D), F32), jax.ShapeDtypeStruct((1, D), F32)),
        grid=(T // tm, S),
        in_specs=[pl.BlockSpec((tm, N), lambda i, j: (i, j)), pl.BlockSpec((None, D, N), lambda i, j: (j, 0, 0)),
                  tile, vec, tile],
        out_specs=(tile, vec), scratch_shapes=[pltpu.VMEM((tm, D), F32)],
        compiler_params=_cparams())(dp, w_in, h0, g1, dh1)


def _meta_grad_call(dh0_3, name):
    B, L, D = dh0_3.shape

    def body(d_ref, o_ref):
        o_ref[...] = jnp.sum(d_ref[...], axis=0)

    return pl.pallas_call(
        body, name=name, out_shape=jax.ShapeDtypeStruct((N_META, D), F32), grid=(1,),
        in_specs=[pl.BlockSpec((B, N_META, D), lambda i: (0, 0, 0))],
        out_specs=pl.BlockSpec((N_META, D), lambda i: (0, 0)), compiler_params=_cparams())(dh0_3)


_RELS = [(dx, dy, dc) for dx in (0, 1) for dy in (0, 1) for dc in (0, 1)][1:]


def _exchange_call(arrs, scatter, name):
    n = len(arrs)
    n_rel = len(_RELS)

    def body(*refs):
        ins, outs = refs[:n], refs[n:2 * n]
        send_sems, recv_sems, loc_sems = refs[2 * n:]
        x, y, c = lax.axis_index("x"), lax.axis_index("y"), lax.axis_index("c")
        me = 4 * x + 2 * y + c
        started = []
        for k in range(n):
            src_me = ins[k].at[me] if scatter else ins[k]
            loc = pltpu.make_async_copy(src_me, outs[k].at[me], loc_sems.at[k])
            loc.start()
            started.append(loc)
        waits = []
        for r, (dx, dy, dc) in enumerate(_RELS):
            px, py, pc = (x + dx) % 2, (y + dy) % 2, (c + dc) % 2
            pid = 4 * px + 2 * py + pc
            for k in range(n):
                s = k * n_rel + r
                src = ins[k].at[pid] if scatter else ins[k]
                cp = pltpu.make_async_remote_copy(
                    src_ref=src, dst_ref=outs[k].at[me], send_sem=send_sems.at[s], recv_sem=recv_sems.at[s],
                    device_id=(px, py, pc), device_id_type=pl.DeviceIdType.MESH)
                cp.start()
                waits.append(pltpu.make_async_remote_copy(
                    src_ref=src, dst_ref=outs[k].at[pid], send_sem=send_sems.at[s], recv_sem=recv_sems.at[s],
                    device_id=(px, py, pc), device_id_type=pl.DeviceIdType.MESH))
        for w in waits:
            w.wait_send()
            w.wait_recv()
        for loc in started:
            loc.wait()

    out_shape = tuple(jax.ShapeDtypeStruct(a.shape if scatter else (N_DEV,) + a.shape, a.dtype) for a in arrs)
    hbm = pl.BlockSpec(memory_space=pl.ANY)
    return pl.pallas_call(
        body, name=name, out_shape=out_shape, in_specs=[hbm] * n, out_specs=tuple([hbm] * n),
        scratch_shapes=[pltpu.SemaphoreType.DMA((n * n_rel,)), pltpu.SemaphoreType.DMA((n * n_rel,)),
                        pltpu.SemaphoreType.DMA((n,))],
        compiler_params=pltpu.CompilerParams(has_side_effects=True))(*arrs)


_HBM = pl.BlockSpec(memory_space=pltpu.HBM)
_SEM = pl.BlockSpec(memory_space=pltpu.SEMAPHORE)
_DATAFLOW = pltpu.SideEffectType.DATAFLOW_SIDE_EFFECTING


def _peer_copies(ins, lands, send_sems, recv_sems, scatter):
    n = len(ins)
    x, y, c = lax.axis_index("x"), lax.axis_index("y"), lax.axis_index("c")
    me = 4 * x + 2 * y + c
    sends, arrivals = [], []
    for r, (dx, dy, dc) in enumerate(_RELS):
        px, py, pc = (x + dx) % 2, (y + dy) % 2, (c + dc) % 2
        pid = 4 * px + 2 * py + pc
        for k in range(n):
            s = k * len(_RELS) + r
            src = ins[k].at[pid] if scatter else ins[k]
            for dst, out in ((lands[k].at[me], sends), (lands[k].at[pid], arrivals)):
                out.append(pltpu.make_async_remote_copy(
                    src_ref=src, dst_ref=dst, send_sem=send_sems.at[s], recv_sem=recv_sems.at[s],
                    device_id=(px, py, pc), device_id_type=pl.DeviceIdType.MESH))
    return sends, arrivals


def _exchange_start(arrs, scatter, name):
    n = len(arrs)
    n_sem = n * len(_RELS)

    def body(*refs):
        ins, lands = refs[:n], refs[n:2 * n]
        send_sems, recv_sems = refs[2 * n], refs[2 * n + 1]
        token = refs[-1]
        sends, _ = _peer_copies(ins, lands, send_sems, recv_sems, scatter)
        for cp in sends:
            cp.start()
        token[...] = jnp.zeros_like(token)

    land_shapes = [a.shape if scatter else (N_DEV,) + a.shape for a in arrs]
    ops = [pltpu.with_memory_space_constraint(a, pltpu.HBM) for a in arrs]
    ops += [pltpu.with_memory_space_constraint(lax.empty(s, a.dtype), pltpu.HBM) for s, a in zip(land_shapes, arrs)]
    out = pl.pallas_call(
        body, name=name,
        out_shape=(pltpu.SemaphoreType.DMA((n_sem,)), pltpu.SemaphoreType.DMA((n_sem,)),
                   *[pltpu.HBM(a.shape, a.dtype) for a in arrs],
                   *[pltpu.HBM(s, a.dtype) for s, a in zip(land_shapes, arrs)],
                   jax.ShapeDtypeStruct((SUBLANES, LANES), F32)),
        in_specs=[_HBM] * (2 * n),
        out_specs=(_SEM, _SEM, *[_HBM] * (2 * n), pl.BlockSpec(memory_space=pltpu.VMEM)),
        input_output_aliases={i: 2 + i for i in range(2 * n)},
        compiler_params=pltpu.CompilerParams(has_side_effects=_DATAFLOW))(*ops)
    return out[0], out[1], list(out[2:2 + n]), list(out[2 + n:2 + 2 * n]), out[-1]


def _exchange_wait(started, after, scatter, name):
    send_sems, recv_sems, srcs, lands, _ = started
    n = len(srcs)

    def body(*refs):
        ins, lands_ = refs[:n], refs[n:2 * n]
        _, arrivals = _peer_copies(ins, lands_, refs[2 * n], refs[2 * n + 1], scatter)
        for cp in arrivals:
            cp.wait_send()
            cp.wait_recv()

    out = pl.pallas_call(
        body, name=name,
        out_shape=(*[pltpu.HBM(a.shape, a.dtype) for a in srcs], *[pltpu.HBM(a.shape, a.dtype) for a in lands]),
        in_specs=[_HBM] * (2 * n) + [_SEM, _SEM, pl.BlockSpec(memory_space=pl.ANY)],
        out_specs=tuple([_HBM] * (2 * n)), input_output_aliases={i: i for i in range(2 * n)},
        compiler_params=pltpu.CompilerParams(has_side_effects=_DATAFLOW))(*srcs, *lands, send_sems, recv_sems, after)
    return list(out[:n]), list(out[n:])


def _place_own_call(srcs, lands, scatter, name):
    n = len(srcs)

    def body(*refs):
        ins, sems = refs[:n], refs[-1]
        outs = refs[2 * n:3 * n]
        me = 4 * lax.axis_index("x") + 2 * lax.axis_index("y") + lax.axis_index("c")
        copies = [pltpu.make_async_copy(ins[k].at[me] if scatter else ins[k], outs[k].at[me], sems.at[k])
                  for k in range(n)]
        for cp in copies:
            cp.start()
        for cp in copies:
            cp.wait()

    hbm = pl.BlockSpec(memory_space=pl.ANY)
    out = pl.pallas_call(
        body, name=name, out_shape=tuple(jax.ShapeDtypeStruct(a.shape, a.dtype) for a in lands),
        in_specs=[hbm] * (2 * n), out_specs=tuple([hbm] * n), input_output_aliases={n + k: k for k in range(n)},
        scratch_shapes=[pltpu.SemaphoreType.DMA((n,))],
        compiler_params=pltpu.CompilerParams(has_side_effects=True))(*srcs, *lands)
    return list(out)


def _adamw_shard_call(w, parts, m, v, name):
    R, C = w.shape
    tr = _tile(R, 128) if R % 16 == 0 else R

    def body(w_ref, p_ref, m_ref, v_ref, g_ref, d_ref, nm_ref, nv_ref):
        g = p_ref[0].astype(F32)
        for s in range(1, N_DEV):
            g = g + p_ref[s].astype(F32)
        d, nm, nv = _adamw(w_ref[...], g, m_ref[...], v_ref[...])
        g_ref[...] = g
        d_ref[...] = d
        nm_ref[...] = nm
        nv_ref[...] = nv

    tile = pl.BlockSpec((tr, C), lambda i: (i, 0))
    sh = jax.ShapeDtypeStruct((R, C), F32)
    return pl.pallas_call(
        body, name=name, out_shape=(sh, sh, sh, sh), grid=(R // tr,),
        in_specs=[tile, pl.BlockSpec((N_DEV, tr, C), lambda i: (0, i, 0)), tile, tile],
        out_specs=(tile, tile, tile, tile), compiler_params=_cparams())(w, parts, m, v)


def _pack(arrs, rows_mult=SUBLANES):
    flat = jnp.concatenate([a.reshape(-1).astype(F32) for a in arrs])
    n = flat.shape[0]
    per = rows_mult * LANES
    total = -(-n // per) * per
    return jnp.pad(flat, (0, total - n)).reshape(total // LANES, LANES)


def _unpack(pack, shapes):
    flat = pack.reshape(-1)
    out, off = [], 0
    for s in shapes:
        n = 1
        for d in s:
            n *= d
        out.append(flat[off:off + n].reshape(s))
        off += n
    return out


def kernel(x, meta_tokens, mix_norm_g, w_in, ssm_lambda_re, ssm_lambda_im, ssm_log_dt, ssm_b_re, ssm_b_im, ssm_c_re, ssm_c_im, ssm_d, ssm_w_glu, w_ssm_proj, hgrn_lb_logits, hgrn_norm_g, w_hgrn_proj, w_out, ffn_norm_g, w_up, conv_w, conv_b, w_down, final_norm_g, loss_target, m_meta_tokens, m_mix_norm_g, m_w_in, m_ssm_lambda_re, m_ssm_lambda_im, m_ssm_log_dt, m_ssm_b_re, m_ssm_b_im, m_ssm_c_re, m_ssm_c_im, m_ssm_d, m_ssm_w_glu, m_w_ssm_proj, m_hgrn_lb_logits, m_hgrn_norm_g, m_w_hgrn_proj, m_w_out, m_ffn_norm_g, m_w_up, m_conv_w, m_conv_b, m_w_down, m_final_norm_g, v_meta_tokens, v_mix_norm_g, v_w_in, v_ssm_lambda_re, v_ssm_lambda_im, v_ssm_log_dt, v_ssm_b_re, v_ssm_b_im, v_ssm_c_re, v_ssm_c_im, v_ssm_d, v_ssm_w_glu, v_w_ssm_proj, v_hgrn_lb_logits, v_hgrn_norm_g, v_w_hgrn_proj, v_w_out, v_ffn_norm_g, v_w_up, v_conv_w, v_conv_b, v_w_down, v_final_norm_g):
    args = dict(locals())
    B, S_len, D = x.shape
    L = S_len + N_META
    T = B * L
    tm = _tile(L, ROW_TILE_CAP)
    tps = L // tm
    G, P = ssm_lambda_re.shape[1:]
    H = ssm_b_re.shape[-1]
    W = G * H
    n_cb = W // LANES
    gpb = G // n_cb
    hd = hgrn_norm_g.shape[1]
    n_heads = D // hd
    n_in = w_in.shape[2]
    F = w_up.shape[2]
    assert W == D and n_in % LANES == 0

    ga = _exchange_start([w_in[0].astype(MXU), meta_tokens, conv_w[0]], False, "gather_a_start")
    gb = _exchange_start(
        [w_up[0].astype(MXU), ssm_w_glu[0].astype(MXU), w_ssm_proj[0].astype(MXU), w_hgrn_proj[0].astype(MXU),
         w_out[0].astype(MXU), w_down[0].astype(MXU)], False, "gather_b_start")
    ga_src, ga_land = _exchange_wait(ga, gb[4], False, "gather_a_wait")
    win_g, meta_g, cw_g = _place_own_call(ga_src, ga_land, False, "gather_a_own")
    meta_full = meta_g.transpose(1, 0, 2).reshape(N_META, D)
    cb_g = conv_b.reshape(N_DEV, 1, F)

    h0 = jnp.concatenate([jnp.broadcast_to(meta_full[None], (B, N_META, D)), x], axis=1).reshape(T, D)
    tgt = jnp.concatenate([jnp.zeros((B, N_META, D), F32), loss_target], axis=1).reshape(T, D)

    lr, li = ssm_lambda_re[0], ssm_lambda_im[0]
    ldt = ssm_log_dt[0].reshape(G, 1)
    bt_re = ssm_b_re[0].transpose(2, 0, 1).reshape(H, G * P)
    bt_im = ssm_b_im[0].transpose(2, 0, 1).reshape(H, G * P)
    disc = _small_call(_disc_a_powers, [lr, li, ldt], [((G, P), F32)] * (2 * SUBLANES + 2), "s5_discretise")
    pw_re, pw_im = jnp.stack(disc[:SUBLANES]), jnp.stack(disc[SUBLANES:2 * SUBLANES])
    coef_re, coef_im = disc[2 * SUBLANES:]
    bbt_re, bbt_im = _small_call(
        _disc_b, [coef_re.reshape(1, G * P), coef_im.reshape(1, G * P), bt_re, bt_im],
        [((H, G * P), F32)] * 2, "s5_input_matrix")
    eye = jnp.eye(gpb, dtype=F32)
    hw = gpb * P

    def expand_b(bbt):
        return jnp.einsum("hcgp,Gg->cGhgp", bbt.reshape(H, n_cb, gpb, P), eye).reshape(n_cb, gpb * H, hw)

    def expand_c(cm):
        return jnp.einsum("cghp,gG->cgpGh", cm.reshape(n_cb, gpb, H, P), eye).reshape(n_cb, hw, gpb * H)

    wb = jnp.concatenate([expand_b(bbt_re), expand_b(bbt_im)], axis=2).astype(MXU)
    wc = jnp.concatenate([expand_c(ssm_c_re[0]), -expand_c(ssm_c_im[0])], axis=1).astype(MXU)
    pwr = pw_re.reshape(SUBLANES, n_cb, hw).transpose(1, 0, 2)
    pwi = pw_im.reshape(SUBLANES, n_cb, hw).transpose(1, 0, 2)
    rows = jnp.arange(SUBLANES)[None, :, None]

    def table(sign, reverse):
        tabs = []
        for d in (1, 2, 4):
            keep = (rows + d < SUBLANES) if reverse else (rows >= d)
            tabs.append(jnp.concatenate([jnp.where(keep, pwr[:, d - 1:d], 0.0),
                                         jnp.where(keep, sign * pwi[:, d - 1:d], 0.0)], axis=2))
        cr, ci = (pwr[:, ::-1], pwi[:, ::-1]) if reverse else (pwr, pwi)
        tabs.append(jnp.concatenate([cr, sign * ci], axis=2))
        return jnp.stack(tabs, axis=1)

    tab_f, tab_r = table(1.0, False), table(-1.0, True)
    dsk = ssm_d.reshape(n_cb, 1, LANES)
    lb = _small_call(_lb_fn, [hgrn_lb_logits], [((1, D), F32)], "hgrn_lower_bound")[0]

    z1 = _norm_call(h0, mix_norm_g, tm, "mix_norm")
    p = _mm_shard(z1, win_g, tm, "in_proj", False)
    p3 = p.reshape(B, L, p.shape[1])
    ya = _s5_fwd_call(p3, wb, wc, tab_f, dsk, "s5_fwd").reshape(T, W)
    gb_src, gb_land = _exchange_wait(gb, ya, False, "gather_b_wait")
    gathered = _place_own_call(gb_src, gb_land, False, "gather_b_own")
    wup_g = gathered[0]
    wglu_g, wsp_g, whp_g, wout_g = [g.reshape(D, D) for g in gathered[1:5]]
    wdn_g = gathered[5].reshape(N_DEV // 2, 2 * w_down.shape[1], D)
    yo, a_br = _glu_proj_call(ya, wglu_g, wsp_g, tm, "s5_glu_proj")
    yb = _hgrn_fwd_call(p3, lb, hgrn_norm_g, n_heads, n_cb, "hgrn_fwd").reshape(T, D)
    col_ga = 5
    h1, mg, bm, z2 = _merge_call(yb, a_br, p, h0, whp_g, wout_g, ffn_norm_g, col_ga, tm, "merge")
    up = _mm_shard(z2, wup_g, tm, "up_proj", True)
    act, dh2, loss_part, dg3 = _ffn_fwd_call(up, cw_g, cb_g, wdn_g, h1, tgt, final_norm_g.reshape(1, D),
                                             tm, tps, "ffn_out_loss")

    dua, dub, dwd, dcwa, dcwb, dcba, dcbb = _ffn_bwd_a_call(dh2, up, act, cw_g, cb_g, wdn_g, tm, tps, "ffn_bwd_gate")
    dupa, dupb, dh1, dg2 = _ffn_bwd_b_call(dua, dub, cw_g, wup_g, h1, ffn_norm_g, dh2, tm, tps, "ffn_bwd_up")
    dwup = jnp.concatenate([_mm_tn(z2, dupa, N_DEV // 2, tm, "dw_up_a", True),
                            _mm_tn(z2, dupb, N_DEV // 2, tm, "dw_up_b", True)], axis=0)
    sh_rows = D // N_DEV
    sa = _exchange_start([dwup.astype(WIRE), dwd.reshape(N_DEV, w_down.shape[1], D).astype(WIRE)], True,
                         "scatter_a_start")
    dmg, dwout = _lin_bwd(mg, dh1, wout_g + sa[4][0:1, 0:1].astype(MXU), tm, "out_proj_bwd")
    da_br, dbm, dga, dgb = _merge_bwd_call(dmg, a_br, bm, p, col_ga, tm, "merge_bwd")
    dyo, dwsp = _lin_bwd(yo, da_br, wsp_g, tm, "ssm_proj_bwd")
    dyb, dwhp = _lin_bwd(yb, dbm, whp_g, tm, "hgrn_proj_bwd")
    dya, dwglu = _glu_bwd_call(ya, dyo, wglu_g, tm, "s5_glu_bwd")
    sb = _exchange_start([t.reshape(N_DEV, sh_rows, D).astype(WIRE) for t in (dwglu, dwsp, dwhp, dwout)], True,
                         "scatter_b_start")
    tok_b = sb[4][0:1, :]
    du, dwb, dwc, dab, ddsk = _s5_bwd_call(p3, dya.reshape(B, L, W), wb, wc, tab_f, tab_r, dsk + tok_b[None],
                                           "s5_bwd")
    dq, dfl, di, dog, dlb, dng = _hgrn_bwd_call(p3, dyb.reshape(B, L, D), lb, hgrn_norm_g + tok_b, n_heads, n_cb,
                                                "hgrn_bwd")
    dp = jnp.concatenate([du.reshape(T, W), dq.reshape(T, D), dfl.reshape(T, D), di.reshape(T, D),
                          dog.reshape(T, D), dga, dgb], axis=1)
    dh0, dg1 = _in_bwd_call(dp, win_g, h0, mix_norm_g, dh1, tm, "in_proj_bwd")
    dwin = _mm_tn(z1, dp, N_DEV, tm, "dw_in", False)
    dh0_3 = dh0.reshape(B, L, D)
    grad_x = dh0_3[:, N_META:]
    dmeta = _meta_grad_call(dh0_3, "meta_grad")

    def diag_b(dw):
        return jnp.einsum("cGhgp,Gg->hcgp", dw.reshape(n_cb, gpb, H, gpb, P), eye).reshape(H, G * P)

    def diag_c(dw):
        return jnp.einsum("cgpGh,gG->cghp", dw.reshape(n_cb, gpb, P, gpb, H), eye).reshape(G, H, P)

    small_parts = [dg1, dab[:, 0, :hw].reshape(G, P), dab[:, 0, hw:].reshape(G, P),
                   diag_b(dwb[:, :, :hw]), diag_b(dwb[:, :, hw:]),
                   diag_c(dwc[:, :hw]), -diag_c(dwc[:, hw:]), ddsk.reshape(1, D), dlb,
                   dng, dg2,
                   jnp.concatenate([dcba, dcbb], axis=0).reshape(1, N_DEV * F), dg3, loss_part]
    small_shapes = [a.shape for a in small_parts]
    small_pack = _pack(small_parts)

    dcw = jnp.concatenate([dcwa, dcwb], axis=0)
    dmeta_s = dmeta.reshape(N_META, N_DEV, D // N_DEV).transpose(1, 0, 2)
    parts_c = _exchange_call([dwin.astype(WIRE), dmeta_s, dcw], True, "scatter_grads")
    small_all = _exchange_call([small_pack], False, "gather_small_grads")[0]
    sa_src, sa_land = _exchange_wait(sa, small_all, True, "scatter_a_wait")
    sb_src, sb_land = _exchange_wait(sb, small_all, True, "scatter_b_wait")
    parts_a = _place_own_call(sa_src, sa_land, True, "scatter_a_own")
    parts_b = _place_own_call(sb_src, sb_land, True, "scatter_b_own")
    parts = [parts_c[0], parts_a[0], *parts_b, parts_a[1], parts_c[1], parts_c[2]]

    def sum8(a):
        t = a[0]
        for s in range(1, N_DEV):
            t = t + a[s]
        return (t,)

    small_sum = _small_call(sum8, [small_all], [(small_pack.shape, F32)], "sum_small_grads")[0]
    (g_g1, t_abr, t_abi, t_bbr, t_bbi, g_cre, g_cim, g_dsk, t_lb, g_ng, g_g2, g_cb, g_g3, loss_v) = _unpack(
        small_sum, small_shapes)

    def disc_b_bwd(cr, ci, br, bi, dbr, dbi):
        _, vjp = jax.vjp(_disc_b, cr, ci, br, bi)
        return vjp((dbr, dbi))

    t_cr, t_ci, g_btr, g_bti = _small_call(
        disc_b_bwd, [coef_re.reshape(1, G * P), coef_im.reshape(1, G * P), bt_re, bt_im, t_bbr, t_bbi],
        [((1, G * P), F32)] * 2 + [((H, G * P), F32)] * 2, "s5_input_matrix_bwd")

    def disc_a_bwd(lr_, li_, ldt_, dar, dai, dcr, dci):
        _, vjp = jax.vjp(_disc_a, lr_, li_, ldt_)
        return vjp((dar, dai, dcr, dci))

    g_lr, g_li, g_ldt = _small_call(
        disc_a_bwd, [lr, li, ldt, t_abr, t_abi, t_cr.reshape(G, P), t_ci.reshape(G, P)],
        [((G, P), F32)] * 2 + [((G, 1), F32)], "s5_discretise_bwd")

    def lb_bwd(logits, d):
        _, vjp = jax.vjp(_lb_fn, logits)
        return vjp(d)

    g_lbl = _small_call(lb_bwd, [hgrn_lb_logits, t_lb], [(hgrn_lb_logits.shape, F32)], "hgrn_lower_bound_bwd")[0]

    grads = dict(
        mix_norm_g=g_g1, ssm_lambda_re=g_lr[None], ssm_lambda_im=g_li[None], ssm_log_dt=g_ldt.reshape(1, G),
        ssm_b_re=g_btr.reshape(H, G, P).transpose(1, 2, 0)[None], ssm_b_im=g_bti.reshape(H, G, P).transpose(1, 2, 0)[None],
        ssm_c_re=g_cre[None], ssm_c_im=g_cim[None], ssm_d=g_dsk, hgrn_lb_logits=g_lbl, hgrn_norm_g=g_ng,
        ffn_norm_g=g_g2, conv_b=g_cb.reshape(1, N_DEV * F), final_norm_g=g_g3.reshape(D))
    loss = loss_v[0, 0]

    delta, new_m, new_v = {}, {}, {}
    sharded = [("w_in", parts[0], (D, n_in)), ("w_up", parts[1], (D, F)), ("ssm_w_glu", parts[2], (sh_rows, D)),
               ("w_ssm_proj", parts[3], (sh_rows, D)), ("w_hgrn_proj", parts[4], (sh_rows, D)),
               ("w_out", parts[5], (sh_rows, D)), ("w_down", parts[6], (w_down.shape[1], D)),
               ("meta_tokens", parts[7], (N_META, D // N_DEV)), ("conv_w", parts[8], (3, F))]
    for name, part, shp in sharded:
        full = args[name].shape
        g, d_, nm, nv = _adamw_shard_call(args[name].reshape(shp), part, args["m_" + name].reshape(shp),
                                          args["v_" + name].reshape(shp), "adamw_" + name)
        grads[name], delta[name], new_m[name], new_v[name] = [t.reshape(full) for t in (g, d_, nm, nv)]

    rep = ["mix_norm_g", "ssm_lambda_re", "ssm_lambda_im", "ssm_log_dt", "ssm_b_re", "ssm_b_im", "ssm_c_re",
           "ssm_c_im", "ssm_d", "hgrn_lb_logits", "hgrn_norm_g", "ffn_norm_g", "conv_b", "final_norm_g"]
    rep_shapes = [args[n].shape for n in rep]
    packs = [_pack([args[pre + n] for n in rep]) for pre in ("", "m_", "v_")]
    g_pack = _pack([grads[n] for n in rep])
    outs = _small_call(lambda w, g, m, v: _adamw(w, g, m, v), [packs[0], g_pack, packs[1], packs[2]],
                       [(g_pack.shape, F32)] * 3, "adamw_replicated")
    for n, d_, nm, nv in zip(rep, *[_unpack(o, rep_shapes) for o in outs]):
        delta[n], new_m[n], new_v[n] = d_, nm, nv

    names = ["meta_tokens", "mix_norm_g", "w_in", "ssm_lambda_re", "ssm_lambda_im", "ssm_log_dt", "ssm_b_re",
             "ssm_b_im", "ssm_c_re", "ssm_c_im", "ssm_d", "ssm_w_glu", "w_ssm_proj", "hgrn_lb_logits", "hgrn_norm_g",
             "w_hgrn_proj", "w_out", "ffn_norm_g", "w_up", "conv_w", "conv_b", "w_down", "final_norm_g"]
    return (loss, grad_x, *[grads[n] for n in names], *[delta[n] for n in names],
            *[new_m[n] for n in names], *[new_v[n] for n in names])
```

```python
import functools

import jax
import jax.numpy as jnp
from jax import lax
from jax.experimental import pallas as pl
from jax.experimental.pallas import tpu as pltpu

F32 = jnp.float32
MXU = jnp.bfloat16
ACT = jnp.bfloat16
WIRE = jnp.bfloat16
N_DEV = 8
N_META = 16
CHUNK = 16
EPS = 1e-6
ADAM_LR, ADAM_B1, ADAM_B2, ADAM_EPS, ADAM_WD, ADAM_STEP = 0.001, 0.9, 0.999, 1e-08, 0.01, 10
SUBLANES = 8
LANES = 128
ROW_TILE_CAP = 700
VMEM_LIMIT = 60 * 1024 * 1024


def _cparams(**kw):
    return pltpu.CompilerParams(vmem_limit_bytes=VMEM_LIMIT, **kw)


def _tile(n, cap):
    best = None
    for t in range(16, min(n, cap) + 1, 16):
        if n % t == 0:
            best = t
    assert best is not None, (n, cap)
    return best


def _dot(a, b):
    return lax.dot_general(a.astype(MXU), b.astype(MXU), (((1,), (0,)), ((), ())), preferred_element_type=F32)


def _dot_nt(a, b):
    return lax.dot_general(a.astype(MXU), b.astype(MXU), (((1,), (1,)), ((), ())), preferred_element_type=F32)


def _dot_tn(a, b):
    return lax.dot_general(a.astype(MXU), b.astype(MXU), (((0,), (0,)), ((), ())), preferred_element_type=F32)


def _rms(x, g):
    return x * lax.rsqrt(jnp.mean(x * x, axis=-1, keepdims=True) + EPS) * g


def _silu(x):
    return x * jax.nn.sigmoid(x)


def _small_call(fn, ins, out_shapes, name):
    n_in = len(ins)

    def body(*refs):
        outs = fn(*[r[...] for r in refs[:n_in]])
        outs = outs if isinstance(outs, (tuple, list)) else (outs,)
        for r, o in zip(refs[n_in:], outs):
            r[...] = o.astype(r.dtype)

    vm = pl.BlockSpec(memory_space=pltpu.VMEM)
    return pl.pallas_call(
        body, name=name, out_shape=tuple(jax.ShapeDtypeStruct(s, d) for s, d in out_shapes),
        in_specs=[vm] * n_in, out_specs=tuple([vm] * len(out_shapes)), compiler_params=_cparams())(*ins)


def _disc_a(lr, li, ldt):
    dt = jnp.exp(ldt)
    mag = jnp.exp(lr * dt)
    ab_re = mag * jnp.cos(li * dt)
    ab_im = mag * jnp.sin(li * dt)
    den = lr * lr + li * li
    nr = ab_re - 1.0
    coef_re = (nr * lr + ab_im * li) / den
    coef_im = (ab_im * lr - nr * li) / den
    return ab_re, ab_im, coef_re, coef_im


def _disc_a_powers(lr, li, ldt):
    ab_re, ab_im, coef_re, coef_im = _disc_a(lr, li, ldt)
    pr, pi = [ab_re], [ab_im]
    for _ in range(SUBLANES - 1):
        pr, pi = pr + [pr[-1] * ab_re - pi[-1] * ab_im], pi + [pr[-1] * ab_im + pi[-1] * ab_re]
    return (*pr, *pi, coef_re, coef_im)


def _disc_b(coef_re, coef_im, bt_re, bt_im):
    return coef_re * bt_re - coef_im * bt_im, coef_re * bt_im + coef_im * bt_re


def _lb_fn(logits):
    return jax.nn.softmax(logits, axis=0)[0:1]


def _adamw(w, g, m, v):
    m = ADAM_B1 * m + (1.0 - ADAM_B1) * g
    v = ADAM_B2 * v + (1.0 - ADAM_B2) * jnp.square(g)
    m_hat = m / (1.0 - ADAM_B1 ** ADAM_STEP)
    v_hat = v / (1.0 - ADAM_B2 ** ADAM_STEP)
    delta = -ADAM_LR * (m_hat / (jnp.sqrt(v_hat) + ADAM_EPS) + ADAM_WD * w)
    return delta, m, v


def _norm_call(h, g, tm, name):
    T, D = h.shape

    def body(h_ref, g_ref, z_ref):
        z_ref[...] = _rms(h_ref[...], g_ref[...]).astype(ACT)

    return pl.pallas_call(
        body, name=name, out_shape=jax.ShapeDtypeStruct((T, D), ACT), grid=(T // tm,),
        in_specs=[pl.BlockSpec((tm, D), lambda i: (i, 0)), pl.BlockSpec((1, D), lambda i: (0, 0))],
        out_specs=pl.BlockSpec((tm, D), lambda i: (i, 0)), compiler_params=_cparams())(h, g)


def _mm_shard(x, w, tm, name, major):
    T, K = x.shape
    S, _, N = w.shape

    def body(x_ref, w_ref, o_ref):
        o_ref[...] = _dot(x_ref[...], w_ref[...]).astype(o_ref.dtype)

    if major:
        out_shape = jax.ShapeDtypeStruct((S, T, N), ACT)
        out_spec = pl.BlockSpec((None, tm, N), lambda j, i: (j, i, 0))
    else:
        out_shape = jax.ShapeDtypeStruct((T, S * N), ACT)
        out_spec = pl.BlockSpec((tm, N), lambda j, i: (i, j))
    return pl.pallas_call(
        body, name=name, out_shape=out_shape, grid=(S, T // tm),
        in_specs=[pl.BlockSpec((tm, K), lambda j, i: (i, 0)), pl.BlockSpec((None, K, N), lambda j, i: (j, 0, 0))],
        out_specs=out_spec, compiler_params=_cparams())(x, w)


def _mm_tn(x, y, n_shards, tm, name, major):
    T, K = x.shape
    S = n_shards
    N = y.shape[-1] if major else y.shape[-1] // S

    def body(x_ref, y_ref, o_ref):
        @pl.when(pl.program_id(1) == 0)
        def _():
            o_ref[...] = jnp.zeros_like(o_ref)
        o_ref[...] += _dot_tn(x_ref[...], y_ref[...])

    y_spec = (pl.BlockSpec((None, tm, N), lambda j, i: (j, i, 0)) if major
              else pl.BlockSpec((tm, N), lambda j, i: (i, j)))
    return pl.pallas_call(
        body, name=name, out_shape=jax.ShapeDtypeStruct((S, K, N), F32), grid=(S, T // tm),
        in_specs=[pl.BlockSpec((tm, K), lambda j, i: (i, 0)), y_spec],
        out_specs=pl.BlockSpec((None, K, N), lambda j, i: (j, 0, 0)), compiler_params=_cparams())(x, y)


def _lin_bwd(x, dy, w, tm, name):
    T, K = x.shape
    N = dy.shape[1]

    def body(x_ref, dy_ref, w_ref, dx_ref, dw_ref):
        @pl.when(pl.program_id(0) == 0)
        def _():
            dw_ref[...] = jnp.zeros_like(dw_ref)
        dy = dy_ref[...]
        dx_ref[...] = _dot_nt(dy, w_ref[...]).astype(dx_ref.dtype)
        dw_ref[...] += _dot_tn(x_ref[...], dy)

    return pl.pallas_call(
        body, name=name,
        out_shape=(jax.ShapeDtypeStruct((T, K), ACT), jax.ShapeDtypeStruct((K, N), F32)), grid=(T // tm,),
        in_specs=[pl.BlockSpec((tm, K), lambda i: (i, 0)), pl.BlockSpec((tm, N), lambda i: (i, 0)),
                  pl.BlockSpec((K, N), lambda i: (0, 0))],
        out_specs=(pl.BlockSpec((tm, K), lambda i: (i, 0)), pl.BlockSpec((K, N), lambda i: (0, 0))),
        compiler_params=_cparams())(x, dy, w)


def _scan_slabs(x_ref, tab_ref, n_slabs, reverse):
    hw = x_ref.shape[1] // 2
    tabs = [tab_ref[s] for s in range(4)]

    def cmul(t, xr, xi):
        tr, ti = t[:, :hw], t[:, hw:]
        return tr * xr - ti * xi, tr * xi + ti * xr

    def step(k, carry):
        cr, ci = carry
        kk = (n_slabs - 1 - k) if reverse else k
        r0 = pl.multiple_of(kk * SUBLANES, SUBLANES)
        x = x_ref[pl.ds(r0, SUBLANES), :]
        xr, xi = x[:, :hw], x[:, hw:]
        for s, d in enumerate((1, 2, 4)):
            sh = (SUBLANES - d) if reverse else d
            ar, ai = cmul(tabs[s], pltpu.roll(xr, sh, 0), pltpu.roll(xi, sh, 0))
            xr, xi = xr + ar, xi + ai
        pr, pi = cmul(tabs[3], cr, ci)
        xr, xi = xr + pr, xi + pi
        x_ref[pl.ds(r0, SUBLANES), 0:hw] = xr
        x_ref[pl.ds(r0, SUBLANES), hw:2 * hw] = xi
        e = 0 if reverse else SUBLANES - 1
        return xr[e:e + 1], xi[e:e + 1]

    z = jnp.zeros((1, hw), F32)
    lax.fori_loop(0, n_slabs, step, (z, z))


def _s5_fwd_call(p3, wb, wc, tab_f, dsk, name):
    B, L, _ = p3.shape
    n_cb, cw, sw = wb.shape

    def body(u_ref, wb_ref, wc_ref, tab_ref, d_ref, ya_ref, s_ref):
        u = u_ref[...]
        s_ref[...] = _dot(u, wb_ref[...])
        _scan_slabs(s_ref, tab_ref, L // SUBLANES, False)
        y = _dot(s_ref[...], wc_ref[...]) + d_ref[...] * u.astype(F32)
        ya_ref[...] = jax.nn.gelu(y).astype(ACT)

    return pl.pallas_call(
        body, name=name, out_shape=jax.ShapeDtypeStruct((B, L, n_cb * cw), ACT), grid=(B, n_cb),
        in_specs=[pl.BlockSpec((None, L, cw), lambda b, c: (b, 0, c)),
                  pl.BlockSpec((None, cw, sw), lambda b, c: (c, 0, 0)),
                  pl.BlockSpec((None, sw, cw), lambda b, c: (c, 0, 0)),
                  pl.BlockSpec((None, 4, SUBLANES, sw), lambda b, c: (c, 0, 0, 0)),
                  pl.BlockSpec((None, 1, cw), lambda b, c: (c, 0, 0))],
        out_specs=pl.BlockSpec((None, L, cw), lambda b, c: (b, 0, c)),
        scratch_shapes=[pltpu.VMEM((L, sw), F32)], compiler_params=_cparams())(p3, wb, wc, tab_f, dsk)


def _s5_bwd_call(p3, dya, wb, wc, tab_f, tab_r, dsk, name):
    B, L, _ = p3.shape
    n_cb, cw, sw = wb.shape
    hw = sw // 2
    n_slabs = L // SUBLANES

    def body(u_ref, dya_ref, wb_ref, wc_ref, tf_ref, tr_ref, d_ref,
             du_ref, dwb_ref, dwc_ref, da_ref, dd_ref, s_ref, l_ref):
        @pl.when(pl.program_id(1) == 0)
        def _():
            dwb_ref[...] = jnp.zeros_like(dwb_ref)
            dwc_ref[...] = jnp.zeros_like(dwc_ref)
            da_ref[...] = jnp.zeros_like(da_ref)
            dd_ref[...] = jnp.zeros_like(dd_ref)

        u = u_ref[...]
        uf = u.astype(F32)
        s_ref[...] = _dot(u, wb_ref[...])
        _scan_slabs(s_ref, tf_ref, n_slabs, False)
        y = _dot(s_ref[...], wc_ref[...]) + d_ref[...] * uf
        _, gelu_vjp = jax.vjp(jax.nn.gelu, y)
        dy = gelu_vjp(dya_ref[...].astype(F32))[0]
        dd_ref[...] += jnp.sum(dy * uf, axis=0, keepdims=True)
        l_ref[...] = _dot_nt(dy, wc_ref[...])
        _scan_slabs(l_ref, tr_ref, n_slabs, True)
        du_ref[...] = (_dot_nt(l_ref[...], wb_ref[...]) + d_ref[...] * dy).astype(ACT)
        dwb_ref[...] += _dot_tn(u, l_ref[...])
        dwc_ref[...] += _dot_tn(s_ref[...], dy)

        row = lax.broadcasted_iota(jnp.int32, (SUBLANES, hw), 0)

        def step(k, carry):
            pr, pi, accr, acci = carry
            r0 = pl.multiple_of(k * SUBLANES, SUBLANES)
            s = s_ref[pl.ds(r0, SUBLANES), :]
            lam = l_ref[pl.ds(r0, SUBLANES), :]
            sr, si = s[:, :hw], s[:, hw:]
            lr, li = lam[:, :hw], lam[:, hw:]
            qr = jnp.where(row == 0, pr, pltpu.roll(sr, 1, 0))
            qi = jnp.where(row == 0, pi, pltpu.roll(si, 1, 0))
            accr = accr + lr * qr + li * qi
            acci = acci + li * qr - lr * qi
            return sr[SUBLANES - 1:], si[SUBLANES - 1:], accr, acci

        z1 = jnp.zeros((1, hw), F32)
        z8 = jnp.zeros((SUBLANES, hw), F32)
        _, _, accr, acci = lax.fori_loop(0, n_slabs, step, (z1, z1, z8, z8))
        da_ref[...] += jnp.concatenate([jnp.sum(accr, axis=0, keepdims=True),
                                        jnp.sum(acci, axis=0, keepdims=True)], axis=1)

    W = n_cb * cw
    return pl.pallas_call(
        body, name=name,
        out_shape=(jax.ShapeDtypeStruct((B, L, W), ACT), jax.ShapeDtypeStruct((n_cb, cw, sw), F32),
                   jax.ShapeDtypeStruct((n_cb, sw, cw), F32), jax.ShapeDtypeStruct((n_cb, 1, sw), F32),
                   jax.ShapeDtypeStruct((n_cb, 1, cw), F32)),
        grid=(n_cb, B),
        in_specs=[pl.BlockSpec((None, L, cw), lambda c, b: (b, 0, c)),
                  pl.BlockSpec((None, L, cw), lambda c, b: (b, 0, c)),
                  pl.BlockSpec((None, cw, sw), lambda c, b: (c, 0, 0)),
                  pl.BlockSpec((None, sw, cw), lambda c, b: (c, 0, 0)),
                  pl.BlockSpec((None, 4, SUBLANES, sw), lambda c, b: (c, 0, 0, 0)),
                  pl.BlockSpec((None, 4, SUBLANES, sw), lambda c, b: (c, 0, 0, 0)),
                  pl.BlockSpec((None, 1, cw), lambda c, b: (c, 0, 0))],
        out_specs=(pl.BlockSpec((None, L, cw), lambda c, b: (b, 0, c)),
                   pl.BlockSpec((None, cw, sw), lambda c, b: (c, 0, 0)),
                   pl.BlockSpec((None, sw, cw), lambda c, b: (c, 0, 0)),
                   pl.BlockSpec((None, 1, sw), lambda c, b: (c, 0, 0)),
                   pl.BlockSpec((None, 1, cw), lambda c, b: (c, 0, 0))),
        scratch_shapes=[pltpu.VMEM((L, sw), F32), pltpu.VMEM((L, sw), F32)],
        compiler_params=_cparams())(p3, dya, wb, wc, tab_f, tab_r, dsk)


def _glu_proj_call(ya, wglu, wproj, tm, name):
    T, W = ya.shape
    D = wproj.shape[1]

    def body(ya_ref, wg_ref, wp_ref, yo_ref, a_ref):
        ya = ya_ref[...]
        yo = ya.astype(F32) * jax.nn.sigmoid(_dot(ya, wg_ref[...]))
        yo_ref[...] = yo.astype(ACT)
        a_ref[...] = _dot(yo, wp_ref[...]).astype(ACT)

    return pl.pallas_call(
        body, name=name, out_shape=(jax.ShapeDtypeStruct((T, W), ACT), jax.ShapeDtypeStruct((T, D), ACT)),
        grid=(T // tm,),
        in_specs=[pl.BlockSpec((tm, W), lambda i: (i, 0)), pl.BlockSpec((W, W), lambda i: (0, 0)),
                  pl.BlockSpec((W, D), lambda i: (0, 0))],
        out_specs=(pl.BlockSpec((tm, W), lambda i: (i, 0)), pl.BlockSpec((tm, D), lambda i: (i, 0))),
        compiler_params=_cparams())(ya, wglu, wproj)


def _glu_bwd_call(ya, dyo, wglu, tm, name):
    T, W = ya.shape

    def body(ya_ref, dyo_ref, wg_ref, dya_ref, dwg_ref):
        @pl.when(pl.program_id(0) == 0)
        def _():
            dwg_ref[...] = jnp.zeros_like(dwg_ref)
        ya = ya_ref[...]
        yaf = ya.astype(F32)
        dyo = dyo_ref[...].astype(F32)
        sg = jax.nn.sigmoid(_dot(ya, wg_ref[...]))
        dt = dyo * yaf * sg * (1.0 - sg)
        dya_ref[...] = (dyo * sg + _dot_nt(dt, wg_ref[...])).astype(ACT)
        dwg_ref[...] += _dot_tn(ya, dt)

    return pl.pallas_call(
        body, name=name, out_shape=(jax.ShapeDtypeStruct((T, W), ACT), jax.ShapeDtypeStruct((W, W), F32)),
        grid=(T // tm,),
        in_specs=[pl.BlockSpec((tm, W), lambda i: (i, 0)), pl.BlockSpec((tm, W), lambda i: (i, 0)),
                  pl.BlockSpec((W, W), lambda i: (0, 0))],
        out_specs=(pl.BlockSpec((tm, W), lambda i: (i, 0)), pl.BlockSpec((W, W), lambda i: (0, 0))),
        compiler_params=_cparams())(ya, dyo, wglu)


PAD = 16


def _chunk_cumsums(x, pad_ref, L):
    row = lax.broadcasted_iota(jnp.int32, x.shape, 0) % CHUNK
    zeros = jnp.zeros((PAD, x.shape[1]), F32)
    pad_ref[0:PAD, :] = zeros
    pad_ref[PAD + L:2 * PAD + L, :] = zeros
    c = x
    r = x
    d = 1
    while d < CHUNK:
        pad_ref[PAD:PAD + L, :] = c
        c = c + jnp.where(row >= d, pad_ref[PAD - d:PAD - d + L, :], 0.0)
        pad_ref[PAD:PAD + L, :] = r
        r = r + jnp.where(row + d < CHUNK, pad_ref[PAD + d:PAD + d + L, :], 0.0)
        d *= 2
    return c, r - x


def _hgrn_prep(q_ref, fl_ref, lb_ref, pad_ref, L):
    lb = lb_ref[...]
    sig = jax.nn.sigmoid(fl_ref[...].astype(F32))
    f = lb + (1.0 - lb) * sig
    k = 1.0 - f
    c, rc = _chunk_cumsums(jnp.log(f), pad_ref, L)
    e_in, e_inv, e_out = jnp.exp(c), jnp.exp(-c), jnp.exp(rc)
    q = q_ref[...].astype(F32)
    return dict(sig=sig, f=f, k=k, q=q, e_in=e_in, e_inv=e_inv, e_out=e_out, dec=jnp.exp(c + rc))


def _chunk_mask(rb):
    r = lax.broadcasted_iota(jnp.int32, (rb, rb), 0)
    c = lax.broadcasted_iota(jnp.int32, (rb, rb), 1)
    return (r // CHUNK == c // CHUNK) & (c <= r)


def _hg_out(o, og, g):
    on = o * lax.rsqrt(jnp.mean(o * o, axis=-1, keepdims=True) + EPS) * g
    return on * _silu(og)


def _hgrn_specs(L, hd, col_q, n_heads, order):
    def spec(sec):
        return pl.BlockSpec((None, L, hd), lambda *g: (order(*g)[0], 0, col_q + sec * n_heads + order(*g)[1]))
    return [spec(0), spec(1), spec(2), spec(3)]


GROUP = 128
CPG = GROUP // CHUNK


def _expand(x):
    xf = x.astype(F32)
    chunk = lax.broadcasted_iota(jnp.int32, xf.shape, 0) // CHUNK
    return jnp.concatenate([jnp.where(chunk == j, xf, 0.0) for j in range(CPG)], axis=1)


def _store_padded(ref, val, L, fill):
    ref[0:L, :] = val.astype(ref.dtype)
    if ref.shape[0] > L:
        ref[L:ref.shape[0], :] = jnp.full((ref.shape[0] - L, ref.shape[1]), fill, ref.dtype)


def _hgrn_forward_core(q_ref, fl_ref, v_ref, lb_ref, pad_ref, qin_ref, kin_ref, kout_ref, vp_ref, dec_ref, o_ref,
                       s_ref, L):
    hd = qin_ref.shape[1]
    n_groups = qin_ref.shape[0] // GROUP
    pp = _hgrn_prep(q_ref, fl_ref, lb_ref, pad_ref, L)
    _store_padded(qin_ref, pp["q"] * pp["e_in"], L, 0.0)
    _store_padded(kin_ref, pp["k"] * pp["e_inv"], L, 0.0)
    _store_padded(kout_ref, pp["k"] * pp["e_out"], L, 0.0)
    _store_padded(vp_ref, v_ref[...], L, 0.0)
    _store_padded(dec_ref, pp["dec"], L, 1.0)
    mask = _chunk_mask(GROUP)

    def intra(g, carry):
        rows = pl.ds(pl.multiple_of(g * GROUP, GROUP), GROUP)
        a = jnp.where(mask, _dot_nt(qin_ref[rows, :], kin_ref[rows, :]), 0.0)
        o_ref[rows, :] = _dot(a, vp_ref[rows, :])
        kv = _dot_tn(vp_ref[rows, :], _expand(kout_ref[rows, :]))
        for j in range(CPG):
            s_ref[g * CPG + j] = kv[:, j * hd:(j + 1) * hd]
        return carry

    lax.fori_loop(0, n_groups, intra, 0)

    def rec(n, st):
        kv = s_ref[n]
        s_ref[n] = st
        dec = dec_ref[pl.ds(pl.multiple_of(n * CHUNK, CHUNK), SUBLANES), :][0:1]
        return st * dec + kv

    lax.fori_loop(0, L // CHUNK, rec, jnp.zeros((hd, hd), F32))

    def inter(g, carry):
        rows = pl.ds(pl.multiple_of(g * GROUP, GROUP), GROUP)
        scat = jnp.concatenate([s_ref[g * CPG + j] for j in range(CPG)], axis=1)
        o_ref[rows, :] += _dot_nt(_expand(qin_ref[rows, :]), scat)
        return carry

    lax.fori_loop(0, n_groups, inter, 0)
    return pp


def _hgrn_scratch(L, hd):
    lp = -(-L // GROUP) * GROUP
    return lp, [pltpu.VMEM((L + 2 * PAD, hd), F32), pltpu.VMEM((lp, hd), MXU), pltpu.VMEM((lp, hd), MXU),
                pltpu.VMEM((lp, hd), MXU), pltpu.VMEM((lp, hd), MXU), pltpu.VMEM((lp, hd), F32),
                pltpu.VMEM((lp, hd), F32), pltpu.VMEM((lp // CHUNK, hd, hd), F32)]


def _hgrn_fwd_call(p3, lb, ng, n_heads, col_q, name):
    B, L, _ = p3.shape
    hd = ng.shape[1]
    _, scratch = _hgrn_scratch(L, hd)

    def body(q_ref, fl_ref, v_ref, og_ref, lb_ref, ng_ref, yb_ref,
             pad_ref, qin_ref, kin_ref, kout_ref, vp_ref, dec_ref, o_ref, s_ref):
        _hgrn_forward_core(q_ref, fl_ref, v_ref, lb_ref, pad_ref, qin_ref, kin_ref, kout_ref, vp_ref, dec_ref,
                           o_ref, s_ref, L)
        yb_ref[...] = _hg_out(o_ref[0:L, :], og_ref[...].astype(F32), ng_ref[...]).astype(ACT)

    order = lambda b, h: (b, h)
    return pl.pallas_call(
        body, name=name, out_shape=jax.ShapeDtypeStruct((B, L, n_heads * hd), ACT), grid=(B, n_heads),
        in_specs=_hgrn_specs(L, hd, col_q, n_heads, order) + [
            pl.BlockSpec((1, hd), lambda b, h: (0, h)), pl.BlockSpec((1, hd), lambda b, h: (0, 0))],
        out_specs=pl.BlockSpec((None, L, hd), lambda b, h: (b, 0, h)),
        scratch_shapes=scratch, compiler_params=_cparams())(p3, p3, p3, p3, lb, ng)


def _hgrn_bwd_call(p3, dyb, lb, ng, n_heads, col_q, name):
    B, L, _ = p3.shape
    hd = ng.shape[1]
    n_chunks = L // CHUNK
    lp, scratch = _hgrn_scratch(L, hd)
    n_groups = lp // GROUP

    def body(q_ref, fl_ref, v_ref, og_ref, dyb_ref, lb_ref, ng_ref,
             dq_ref, dfl_ref, dv_ref, dog_ref, dlb_ref, dng_ref,
             pad_ref, qin_ref, kin_ref, kout_ref, vp_ref, dec_ref, o_ref, s_ref,
             do_ref, ds_ref, dqi_ref, dki_ref, dko_ref, dvv_ref, dct_ref):
        @pl.when(pl.program_id(1) == 0)
        def _():
            dlb_ref[...] = jnp.zeros_like(dlb_ref)

        @pl.when((pl.program_id(0) == 0) & (pl.program_id(1) == 0))
        def _():
            dng_ref[...] = jnp.zeros_like(dng_ref)

        pp = _hgrn_forward_core(q_ref, fl_ref, v_ref, lb_ref, pad_ref, qin_ref, kin_ref, kout_ref, vp_ref, dec_ref,
                                o_ref, s_ref, L)

        og = og_ref[...].astype(F32)
        _, out_vjp = jax.vjp(_hg_out, o_ref[0:L, :], og, ng_ref[...])
        d_o, d_og, d_ng = out_vjp(dyb_ref[...].astype(F32))
        dog_ref[...] = d_og.astype(ACT)
        dng_ref[...] += d_ng
        _store_padded(do_ref, d_o, L, 0.0)
        mask = _chunk_mask(GROUP)

        def grads_a(g, carry):
            rows = pl.ds(pl.multiple_of(g * GROUP, GROUP), GROUP)
            qi, ki, vv, do = qin_ref[rows, :], kin_ref[rows, :], vp_ref[rows, :], do_ref[rows, :]
            a = jnp.where(mask, _dot_nt(qi, ki), 0.0)
            da = jnp.where(mask, _dot_nt(do, vv), 0.0)
            sstack = s_ref[pl.ds(g * CPG, CPG)].reshape(CPG * hd, hd)
            dqi_ref[rows, :] = _dot(da, ki) + _dot(_expand(do), sstack)
            dki_ref[rows, :] = _dot_tn(da, qi)
            dvv_ref[rows, :] = _dot_tn(a, do)
            x = _dot_tn(do, _expand(qi))
            for j in range(CPG):
                ds_ref[g * CPG + j] = x[:, j * hd:(j + 1) * hd]
            return carry

        lax.fori_loop(0, n_groups, grads_a, 0)

        def rec_bwd(k, dst):
            n = n_chunks - 1 - k
            r0 = pl.multiple_of(n * CHUNK, CHUNK)
            x = ds_ref[n]
            ds_ref[n] = dst
            dec = dec_ref[pl.ds(r0, SUBLANES), :][0:1]
            ddec = dec * jnp.sum(dst * s_ref[n], axis=0, keepdims=True)
            dct_ref[pl.ds(r0, CHUNK), :] = jnp.broadcast_to(ddec, (CHUNK, hd))
            return dst * dec + x

        lax.fori_loop(0, n_chunks, rec_bwd, jnp.zeros((hd, hd), F32))

        def grads_b(g, carry):
            rows = pl.ds(pl.multiple_of(g * GROUP, GROUP), GROUP)
            dscat = jnp.concatenate([ds_ref[g * CPG + j] for j in range(CPG)], axis=1)
            dvv_ref[rows, :] += _dot_nt(_expand(kout_ref[rows, :]), dscat)
            dstack = ds_ref[pl.ds(g * CPG, CPG)].reshape(CPG * hd, hd)
            dko_ref[rows, :] = _dot(_expand(vp_ref[rows, :]), dstack)
            return carry

        lax.fori_loop(0, n_groups, grads_b, 0)

        dqi, dki, dko = dqi_ref[0:L, :], dki_ref[0:L, :], dko_ref[0:L, :]
        dq = dqi * pp["e_in"]
        dk = dki * pp["e_inv"] + dko * pp["e_out"]
        dq_ref[...] = dq.astype(ACT)
        dv_ref[...] = dvv_ref[0:L, :].astype(ACT)
        t_out = pp["k"] * pp["e_out"] * dko
        dc = pp["q"] * pp["e_in"] * dqi - pp["k"] * pp["e_inv"] * dki - t_out
        _, dc_later = _chunk_cumsums(dc, pad_ref, L)
        t_incl, t_later = _chunk_cumsums(t_out, pad_ref, L)
        dlogf = dc + dc_later + t_incl + t_later + dct_ref[0:L, :]
        df = dlogf / pp["f"] - dk
        lbv = lb_ref[...]
        sig = pp["sig"]
        dfl_ref[...] = (df * (1.0 - lbv) * sig * (1.0 - sig)).astype(ACT)
        dlb_ref[...] += jnp.sum(df * (1.0 - sig), axis=0, keepdims=True)

    order = lambda h, b: (b, h)
    W = n_heads * hd
    act_out = jax.ShapeDtypeStruct((B, L, W), ACT)
    blk_out = pl.BlockSpec((None, L, hd), lambda h, b: (b, 0, h))
    return pl.pallas_call(
        body, name=name,
        out_shape=(act_out, act_out, act_out, act_out, jax.ShapeDtypeStruct((1, W), F32),
                   jax.ShapeDtypeStruct((1, hd), F32)),
        grid=(n_heads, B),
        in_specs=_hgrn_specs(L, hd, col_q, n_heads, order) + [
            pl.BlockSpec((None, L, hd), lambda h, b: (b, 0, h)),
            pl.BlockSpec((1, hd), lambda h, b: (0, h)), pl.BlockSpec((1, hd), lambda h, b: (0, 0))],
        out_specs=(blk_out, blk_out, blk_out, blk_out, pl.BlockSpec((1, hd), lambda h, b: (0, h)),
                   pl.BlockSpec((1, hd), lambda h, b: (0, 0))),
        scratch_shapes=scratch + [
            pltpu.VMEM((lp, hd), MXU), pltpu.VMEM((lp // CHUNK, hd, hd), F32)] + [pltpu.VMEM((lp, hd), F32)] * 5,
        compiler_params=_cparams())(p3, p3, p3, p3, dyb, lb, ng)


def _merge_fn(a, bm, ga, gb):
    return jax.nn.sigmoid(ga) * a + jax.nn.sigmoid(gb) * bm


def _merge_call(yb, a, p, h0, whp, wout, g2, col_ga, tm, name):
    T, D = h0.shape

    def body(yb_ref, a_ref, ga_ref, gb_ref, h0_ref, whp_ref, wout_ref, g2_ref, h1_ref, mg_ref, bm_ref, z2_ref):
        bm = _dot(yb_ref[...], whp_ref[...])
        mg = _merge_fn(a_ref[...].astype(F32), bm, ga_ref[...].astype(F32), gb_ref[...].astype(F32))
        h1 = h0_ref[...] + _dot(mg, wout_ref[...])
        h1_ref[...] = h1
        mg_ref[...] = mg.astype(ACT)
        bm_ref[...] = bm.astype(ACT)
        z2_ref[...] = _rms(h1, g2_ref[...]).astype(ACT)

    tile = pl.BlockSpec((tm, D), lambda i: (i, 0))
    full = pl.BlockSpec((D, D), lambda i: (0, 0))
    act = jax.ShapeDtypeStruct((T, D), ACT)
    return pl.pallas_call(
        body, name=name, out_shape=(jax.ShapeDtypeStruct((T, D), F32), act, act, act), grid=(T // tm,),
        in_specs=[tile, tile, pl.BlockSpec((tm, D), lambda i: (i, col_ga)),
                  pl.BlockSpec((tm, D), lambda i: (i, col_ga + 1)), tile, full, full,
                  pl.BlockSpec((1, D), lambda i: (0, 0))],
        out_specs=(tile, tile, tile, tile), compiler_params=_cparams())(yb, a, p, p, h0, whp, wout, g2)


def _merge_bwd_call(dmg, a, bm, p, col_ga, tm, name):
    T, D = dmg.shape

    def body(dmg_ref, a_ref, bm_ref, ga_ref, gb_ref, da_ref, dbm_ref, dga_ref, dgb_ref):
        args = [r[...].astype(F32) for r in (a_ref, bm_ref, ga_ref, gb_ref)]
        _, vjp = jax.vjp(_merge_fn, *args)
        for r, o in zip((da_ref, dbm_ref, dga_ref, dgb_ref), vjp(dmg_ref[...].astype(F32))):
            r[...] = o.astype(ACT)

    tile = pl.BlockSpec((tm, D), lambda i: (i, 0))
    act = jax.ShapeDtypeStruct((T, D), ACT)
    return pl.pallas_call(
        body, name=name, out_shape=(act, act, act, act), grid=(T // tm,),
        in_specs=[tile, tile, tile, pl.BlockSpec((tm, D), lambda i: (i, col_ga)),
                  pl.BlockSpec((tm, D), lambda i: (i, col_ga + 1))],
        out_specs=(tile, tile, tile, tile), compiler_params=_cparams())(dmg, a, bm, p, p)


def _conv_taps(x_ref, halo_ref, ext_ref, edge, tm, before):
    halo = jnp.where(edge, 0.0, halo_ref[...].astype(F32))
    x = x_ref[...].astype(F32)
    if before:
        ext_ref[0:PAD, :] = halo
        ext_ref[PAD:PAD + tm, :] = x
        return [ext_ref[PAD - 2 + k:PAD - 2 + k + tm, :] for k in range(3)]
    ext_ref[0:tm, :] = x
    ext_ref[tm:tm + PAD, :] = halo
    return [ext_ref[k:k + tm, :] for k in range(3)]


def _conv(taps, cw, cb):
    return cb + cw[0:1] * taps[0] + cw[1:2] * taps[1] + cw[2:3] * taps[2]


def _ffn_pair_specs(tm, F, T, n_pairs, order, before):
    hb = tm // PAD
    last = T // PAD - 1

    def halo_row(i):
        return jnp.maximum(i * hb - 1, 0) if before else jnp.minimum((i + 1) * hb, last)

    specs = []
    for off in (0, n_pairs):
        specs.append(pl.BlockSpec((None, tm, F), lambda *g, off=off: (order(*g)[1] + off, order(*g)[0], 0)))
        specs.append(pl.BlockSpec((None, PAD, F), lambda *g, off=off: (order(*g)[1] + off, halo_row(order(*g)[0]), 0)))
    return specs


def _ffn_fwd_call(up, cw, cb, wd, h1, tgt, g3, tm, tps, name):
    S, T, F = up.shape
    n_pairs = S // 2
    D = h1.shape[1]

    def body(ua_ref, ha_ref, ub_ref, hb_ref, cwa_ref, cwb_ref, cba_ref, cbb_ref, wd_ref, h1_ref, tgt_ref, g3_ref,
             act_ref, dh2_ref, loss_ref, dg3_ref, acc_ref, ext_ref):
        i, j = pl.program_id(0), pl.program_id(1)
        edge = (i % tps) == 0
        ua = _conv(_conv_taps(ua_ref, ha_ref, ext_ref, edge, tm, True), cwa_ref[...], cba_ref[...])
        ub = _conv(_conv_taps(ub_ref, hb_ref, ext_ref, edge, tm, True), cwb_ref[...], cbb_ref[...])
        act = _silu(ua) * ub
        act_ref[...] = act.astype(ACT)
        contrib = _dot(act, wd_ref[...])

        @pl.when(j == 0)
        def _():
            acc_ref[...] = h1_ref[...] + contrib

        @pl.when(j > 0)
        def _():
            acc_ref[...] += contrib

        @pl.when((i == 0) & (j == 0))
        def _():
            loss_ref[...] = jnp.zeros_like(loss_ref)
            dg3_ref[...] = jnp.zeros_like(dg3_ref)

        @pl.when(j == n_pairs - 1)
        def _():
            row = lax.broadcasted_iota(jnp.int32, (tm, 1), 0) + (i % tps) * tm
            valid = row >= N_META
            tgt = tgt_ref[...]

            def loss_fn(h2, g):
                err = _rms(h2, g) - tgt
                return 0.5 * jnp.sum(jnp.where(valid, err * err, 0.0)) / D

            loss, vjp = jax.vjp(loss_fn, acc_ref[...], g3_ref[...])
            dh2, dg3 = vjp(jnp.ones((), F32))
            dh2_ref[...] = dh2
            loss_ref[...] += loss
            dg3_ref[...] += dg3

    order = lambda i, j: (i, j)
    tile = pl.BlockSpec((tm, D), lambda i, j: (i, 0))
    vec = pl.BlockSpec((1, D), lambda i, j: (0, 0))
    return pl.pallas_call(
        body, name=name,
        out_shape=(jax.ShapeDtypeStruct((n_pairs, T, F), ACT), jax.ShapeDtypeStruct((T, D), F32),
                   jax.ShapeDtypeStruct((1, LANES), F32), jax.ShapeDtypeStruct((1, D), F32)),
        grid=(T // tm, n_pairs),
        in_specs=_ffn_pair_specs(tm, F, T, n_pairs, order, True) + [
            pl.BlockSpec((None, 3, F), lambda i, j: (j, 0, 0)), pl.BlockSpec((None, 3, F), lambda i, j: (j + n_pairs, 0, 0)),
            pl.BlockSpec((None, 1, F), lambda i, j: (j, 0, 0)), pl.BlockSpec((None, 1, F), lambda i, j: (j + n_pairs, 0, 0)),
            pl.BlockSpec((None, F, D), lambda i, j: (j, 0, 0)), tile, tile, vec],
        out_specs=(pl.BlockSpec((None, tm, F), lambda i, j: (j, i, 0)), tile,
                   pl.BlockSpec((1, LANES), lambda i, j: (0, 0)), vec),
        scratch_shapes=[pltpu.VMEM((tm, D), F32), pltpu.VMEM((tm + PAD, F), F32)],
        compiler_params=_cparams())(up, up, up, up, cw, cw, cb, cb, wd, h1, tgt, g3)


def _ffn_bwd_a_call(dh2, up, act, cw, cb, wd, tm, tps, name):
    S, T, F = up.shape
    n_pairs = S // 2
    D = dh2.shape[1]

    def body(dh2_ref, ua_ref, ha_ref, ub_ref, hb_ref, act_ref, cwa_ref, cwb_ref, cba_ref, cbb_ref, wd_ref,
             dua_ref, dub_ref, dwd_ref, dcwa_ref, dcwb_ref, dcba_ref, dcbb_ref, ext_ref):
        i = pl.program_id(1)
        edge = (i % tps) == 0

        @pl.when(i == 0)
        def _():
            for r in (dwd_ref, dcwa_ref, dcwb_ref, dcba_ref, dcbb_ref):
                r[...] = jnp.zeros_like(r)

        dh2 = dh2_ref[...]
        dact = _dot_nt(dh2, wd_ref[...])
        dwd_ref[...] += _dot_tn(act_ref[...], dh2)
        taps_a = _conv_taps(ua_ref, ha_ref, ext_ref, edge, tm, True)
        ua = _conv(taps_a, cwa_ref[...], cba_ref[...])
        sa = jax.nn.sigmoid(ua)
        dub = dact * ua * sa
        dcbb_ref[...] += jnp.sum(dub, axis=0, keepdims=True)
        taps_b = _conv_taps(ub_ref, hb_ref, ext_ref, edge, tm, True)
        dcwb_ref[...] += jnp.concatenate([jnp.sum(dub * t, axis=0, keepdims=True) for t in taps_b], axis=0)
        ub = _conv(taps_b, cwb_ref[...], cbb_ref[...])
        dua = dact * ub * sa * (1.0 + ua * (1.0 - sa))
        dcba_ref[...] += jnp.sum(dua, axis=0, keepdims=True)
        taps_a = _conv_taps(ua_ref, ha_ref, ext_ref, edge, tm, True)
        dcwa_ref[...] += jnp.concatenate([jnp.sum(dua * t, axis=0, keepdims=True) for t in taps_a], axis=0)
        dua_ref[...] = dua.astype(ACT)
        dub_ref[...] = dub.astype(ACT)

    order = lambda j, i: (i, j)
    sh = lambda rows: jax.ShapeDtypeStruct((n_pairs, rows, F), F32)
    par = lambda rows: pl.BlockSpec((None, rows, F), lambda j, i: (j, 0, 0))
    return pl.pallas_call(
        body, name=name,
        out_shape=(jax.ShapeDtypeStruct((n_pairs, T, F), ACT), jax.ShapeDtypeStruct((n_pairs, T, F), ACT),
                   jax.ShapeDtypeStruct((n_pairs, F, D), F32), sh(3), sh(3), sh(1), sh(1)),
        grid=(n_pairs, T // tm),
        in_specs=[pl.BlockSpec((tm, D), lambda j, i: (i, 0))] + _ffn_pair_specs(tm, F, T, n_pairs, order, True) + [
            pl.BlockSpec((None, tm, F), lambda j, i: (j, i, 0)),
            pl.BlockSpec((None, 3, F), lambda j, i: (j, 0, 0)), pl.BlockSpec((None, 3, F), lambda j, i: (j + n_pairs, 0, 0)),
            pl.BlockSpec((None, 1, F), lambda j, i: (j, 0, 0)), pl.BlockSpec((None, 1, F), lambda j, i: (j + n_pairs, 0, 0)),
            pl.BlockSpec((None, F, D), lambda j, i: (j, 0, 0))],
        out_specs=(pl.BlockSpec((None, tm, F), lambda j, i: (j, i, 0)), pl.BlockSpec((None, tm, F), lambda j, i: (j, i, 0)),
                   pl.BlockSpec((None, F, D), lambda j, i: (j, 0, 0)), par(3), par(3), par(1), par(1)),
        scratch_shapes=[pltpu.VMEM((tm + PAD, F), F32)],
        compiler_params=_cparams())(dh2, up, up, up, up, act, cw, cw, cb, cb, wd)


def _ffn_bwd_b_call(dua, dub, cw, wup, h1, g2, dh2, tm, tps, name):
    n_pairs, T, F = dua.shape
    D = h1.shape[1]
    hb = tm // PAD
    last = T // PAD - 1

    def body(da_ref, na_ref, db_ref, nb_ref, cwa_ref, cwb_ref, wa_ref, wb_ref, h1_ref, g2_ref, dh2_ref,
             dupa_ref, dupb_ref, dh1_ref, dg2_ref, acc_ref, ext_ref):
        i, j = pl.program_id(0), pl.program_id(1)
        edge = (i % tps) == tps - 1
        outs = []
        for d_ref, n_ref, cw_ref, o_ref in ((da_ref, na_ref, cwa_ref, dupa_ref), (db_ref, nb_ref, cwb_ref, dupb_ref)):
            t = _conv_taps(d_ref, n_ref, ext_ref, edge, tm, False)
            cwv = cw_ref[...]
            dup = cwv[2:3] * t[0] + cwv[1:2] * t[1] + cwv[0:1] * t[2]
            o_ref[...] = dup.astype(ACT)
            outs.append(dup)
        contrib = _dot_nt(outs[0], wa_ref[...]) + _dot_nt(outs[1], wb_ref[...])

        @pl.when(j == 0)
        def _():
            acc_ref[...] = contrib

        @pl.when(j > 0)
        def _():
            acc_ref[...] += contrib

        @pl.when((i == 0) & (j == 0))
        def _():
            dg2_ref[...] = jnp.zeros_like(dg2_ref)

        @pl.when(j == n_pairs - 1)
        def _():
            _, vjp = jax.vjp(_rms, h1_ref[...], g2_ref[...])
            dh, dg = vjp(acc_ref[...])
            dh1_ref[...] = dh2_ref[...] + dh
            dg2_ref[...] += dg

    tile = pl.BlockSpec((tm, D), lambda i, j: (i, 0))
    vec = pl.BlockSpec((1, D), lambda i, j: (0, 0))
    pair = lambda: [pl.BlockSpec((None, tm, F), lambda i, j: (j, i, 0)),
                    pl.BlockSpec((None, PAD, F), lambda i, j: (j, jnp.minimum((i + 1) * hb, last), 0))]
    act = jax.ShapeDtypeStruct((n_pairs, T, F), ACT)
    return pl.pallas_call(
        body, name=name,
        out_shape=(act, act, jax.ShapeDtypeStruct((T, D), F32), jax.ShapeDtypeStruct((1, D), F32)),
        grid=(T // tm, n_pairs),
        in_specs=pair() + pair() + [
            pl.BlockSpec((None, 3, F), lambda i, j: (j, 0, 0)), pl.BlockSpec((None, 3, F), lambda i, j: (j + n_pairs, 0, 0)),
            pl.BlockSpec((None, D, F), lambda i, j: (j, 0, 0)), pl.BlockSpec((None, D, F), lambda i, j: (j + n_pairs, 0, 0)),
            tile, vec, tile],
        out_specs=(pl.BlockSpec((None, tm, F), lambda i, j: (j, i, 0)), pl.BlockSpec((None, tm, F), lambda i, j: (j, i, 0)),
                   tile, vec),
        scratch_shapes=[pltpu.VMEM((tm, D), F32), pltpu.VMEM((tm + PAD, F), F32)],
        compiler_params=_cparams())(dua, dua, dub, dub, cw, cw, wup, wup, h1, g2, dh2)


def _in_bwd_call(dp, w_in, h0, g1, dh1, tm, name):
    T, D = h0.shape
    S, _, N = w_in.shape

    def body(dp_ref, w_ref, h0_ref, g1_ref, dh1_ref, dh0_ref, dg1_ref, acc_ref):
        i, j = pl.program_id(0), pl.program_id(1)
        contrib = _dot_nt(dp_ref[...], w_ref[...])

        @pl.when(j == 0)
        def _():
            acc_ref[...] = contrib

        @pl.when(j > 0)
        def _():
            acc_ref[...] += contrib

        @pl.when((i == 0) & (j == 0))
        def _():
            dg1_ref[...] = jnp.zeros_like(dg1_ref)

        @pl.when(j == S - 1)
        def _():
            _, vjp = jax.vjp(_rms, h0_ref[...], g1_ref[...])
            dh, dg = vjp(acc_ref[...])
            dh0_ref[...] = dh1_ref[...] + dh
            dg1_ref[...] += dg

    tile = pl.BlockSpec((tm, D), lambda i, j: (i, 0))
    vec = pl.BlockSpec((1, D), lambda i, j: (0, 0))
    return pl.pallas_call(
        body, name=name, out_shape=(jax.ShapeDtypeStruct((T, D), F32), jax.ShapeDtypeStruct((1, D), F32)),
        grid=(T // tm, S),
        in_specs=[pl.BlockSpec((tm, N), lambda i, j: (i, j)), pl.BlockSpec((None, D, N), lambda i, j: (j, 0, 0)),
                  tile, vec, tile],
        out_specs=(tile, vec), scratch_shapes=[pltpu.VMEM((tm, D), F32)],
        compiler_params=_cparams())(dp, w_in, h0, g1, dh1)


def _meta_grad_call(dh0_3, name):
    B, L, D = dh0_3.shape

    def body(d_ref, o_ref):
        o_ref[...] = jnp.sum(d_ref[...], axis=0)

    return pl.pallas_call(
        body, name=name, out_shape=jax.ShapeDtypeStruct((N_META, D), F32), grid=(1,),
        in_specs=[pl.BlockSpec((B, N_META, D), lambda i: (0, 0, 0))],
        out_specs=pl.BlockSpec((N_META, D), lambda i: (0, 0)), compiler_params=_cparams())(dh0_3)


_RELS = [(dx, dy, dc) for dx in (0, 1) for dy in (0, 1) for dc in (0, 1)][1:]


def _exchange_call(arrs, scatter, name):
    n = len(arrs)
    n_rel = len(_RELS)

    def body(*refs):
        ins, outs = refs[:n], refs[n:2 * n]
        send_sems, recv_sems, loc_sems = refs[2 * n:]
        x, y, c = lax.axis_index("x"), lax.axis_index("y"), lax.axis_index("c")
        me = 4 * x + 2 * y + c
        started = []
        for k in range(n):
            src_me = ins[k].at[me] if scatter else ins[k]
            loc = pltpu.make_async_copy(src_me, outs[k].at[me], loc_sems.at[k])
            loc.start()
            started.append(loc)
        waits = []
        for r, (dx, dy, dc) in enumerate(_RELS):
            px, py, pc = (x + dx) % 2, (y + dy) % 2, (c + dc) % 2
            pid = 4 * px + 2 * py + pc
            for k in range(n):
                s = k * n_rel + r
                src = ins[k].at[pid] if scatter else ins[k]
                cp = pltpu.make_async_remote_copy(
                    src_ref=src, dst_ref=outs[k].at[me], send_sem=send_sems.at[s], recv_sem=recv_sems.at[s],
                    device_id=(px, py, pc), device_id_type=pl.DeviceIdType.MESH)
                cp.start()
                waits.append(pltpu.make_async_remote_copy(
                    src_ref=src, dst_ref=outs[k].at[pid], send_sem=send_sems.at[s], recv_sem=recv_sems.at[s],
                    device_id=(px, py, pc), device_id_type=pl.DeviceIdType.MESH))
        for w in waits:
            w.wait_send()
            w.wait_recv()
        for loc in started:
            loc.wait()

    out_shape = tuple(jax.ShapeDtypeStruct(a.shape if scatter else (N_DEV,) + a.shape, a.dtype) for a in arrs)
    hbm = pl.BlockSpec(memory_space=pl.ANY)
    return pl.pallas_call(
        body, name=name, out_shape=out_shape, in_specs=[hbm] * n, out_specs=tuple([hbm] * n),
        scratch_shapes=[pltpu.SemaphoreType.DMA((n * n_rel,)), pltpu.SemaphoreType.DMA((n * n_rel,)),
                        pltpu.SemaphoreType.DMA((n,))],
        compiler_params=pltpu.CompilerParams(has_side_effects=True))(*arrs)


_HBM = pl.BlockSpec(memory_space=pltpu.HBM)
_SEM = pl.BlockSpec(memory_space=pltpu.SEMAPHORE)
_DATAFLOW = pltpu.SideEffectType.DATAFLOW_SIDE_EFFECTING


def _peer_copies(ins, lands, send_sems, recv_sems, scatter):
    n = len(ins)
    x, y, c = lax.axis_index("x"), lax.axis_index("y"), lax.axis_index("c")
    me = 4 * x + 2 * y + c
    sends, arrivals = [], []
    for r, (dx, dy, dc) in enumerate(_RELS):
        px, py, pc = (x + dx) % 2, (y + dy) % 2, (c + dc) % 2
        pid = 4 * px + 2 * py + pc
        for k in range(n):
            s = k * len(_RELS) + r
            src = ins[k].at[pid] if scatter else ins[k]
            for dst, out in ((lands[k].at[me], sends), (lands[k].at[pid], arrivals)):
                out.append(pltpu.make_async_remote_copy(
                    src_ref=src, dst_ref=dst, send_sem=send_sems.at[s], recv_sem=recv_sems.at[s],
                    device_id=(px, py, pc), device_id_type=pl.DeviceIdType.MESH))
    return sends, arrivals


def _exchange_start(arrs, scatter, name):
    n = len(arrs)
    n_sem = n * len(_RELS)

    def body(*refs):
        ins, lands = refs[:n], refs[n:2 * n]
        send_sems, recv_sems = refs[2 * n], refs[2 * n + 1]
        token = refs[-1]
        sends, _ = _peer_copies(ins, lands, send_sems, recv_sems, scatter)
        for cp in sends:
            cp.start()
        token[...] = jnp.zeros_like(token)

    land_shapes = [a.shape if scatter else (N_DEV,) + a.shape for a in arrs]
    ops = [pltpu.with_memory_space_constraint(a, pltpu.HBM) for a in arrs]
    ops += [pltpu.with_memory_space_constraint(lax.empty(s, a.dtype), pltpu.HBM) for s, a in zip(land_shapes, arrs)]
    out = pl.pallas_call(
        body, name=name,
        out_shape=(pltpu.SemaphoreType.DMA((n_sem,)), pltpu.SemaphoreType.DMA((n_sem,)),
                   *[pltpu.HBM(a.shape, a.dtype) for a in arrs],
                   *[pltpu.HBM(s, a.dtype) for s, a in zip(land_shapes, arrs)],
                   jax.ShapeDtypeStruct((SUBLANES, LANES), F32)),
        in_specs=[_HBM] * (2 * n),
        out_specs=(_SEM, _SEM, *[_HBM] * (2 * n), pl.BlockSpec(memory_space=pltpu.VMEM)),
        input_output_aliases={i: 2 + i for i in range(2 * n)},
        compiler_params=pltpu.CompilerParams(has_side_effects=_DATAFLOW))(*ops)
    return out[0], out[1], list(out[2:2 + n]), list(out[2 + n:2 + 2 * n]), out[-1]


def _exchange_wait(started, after, scatter, name):
    send_sems, recv_sems, srcs, lands, _ = started
    n = len(srcs)

    def body(*refs):
        ins, lands_ = refs[:n], refs[n:2 * n]
        _, arrivals = _peer_copies(ins, lands_, refs[2 * n], refs[2 * n + 1], scatter)
        for cp in arrivals:
            cp.wait_send()
            cp.wait_recv()

    out = pl.pallas_call(
        body, name=name,
        out_shape=(*[pltpu.HBM(a.shape, a.dtype) for a in srcs], *[pltpu.HBM(a.shape, a.dtype) for a in lands]),
        in_specs=[_HBM] * (2 * n) + [_SEM, _SEM, pl.BlockSpec(memory_space=pl.ANY)],
        out_specs=tuple([_HBM] * (2 * n)), input_output_aliases={i: i for i in range(2 * n)},
        compiler_params=pltpu.CompilerParams(has_side_effects=_DATAFLOW))(*srcs, *lands, send_sems, recv_sems, after)
    return list(out[:n]), list(out[n:])


def _place_own_call(srcs, lands, scatter, me, name):
    outs = []
    for k, (src, land) in enumerate(zip(srcs, lands)):
        R, C = land.shape[1:]
        tr = R
        while tr % 32 == 0 and tr * C * land.dtype.itemsize > 2 * 1024 * 1024:
            tr //= 2

        def body(me_ref, s_ref, l_ref, o_ref):
            o_ref[...] = s_ref[...]

        src_spec = (pl.BlockSpec((None, tr, C), lambda i, me_ref: (me_ref[0], i, 0)) if scatter
                    else pl.BlockSpec((tr, C), lambda i, me_ref: (i, 0)))
        outs.append(pl.pallas_call(
            body, name=f"{name}_{k}", out_shape=jax.ShapeDtypeStruct(land.shape, land.dtype),
            grid_spec=pltpu.PrefetchScalarGridSpec(
                num_scalar_prefetch=1, grid=(R // tr,),
                in_specs=[src_spec, pl.BlockSpec(memory_space=pl.ANY)],
                out_specs=pl.BlockSpec((None, tr, C), lambda i, me_ref: (me_ref[0], i, 0))),
            input_output_aliases={2: 0}, compiler_params=_cparams())(me, src, land))
    return outs


def _adamw_shard_call(w, parts, m, v, name):
    R, C = w.shape
    tr = _tile(R, 128) if R % 16 == 0 else R

    def body(w_ref, p_ref, m_ref, v_ref, g_ref, d_ref, nm_ref, nv_ref):
        g = p_ref[0].astype(F32)
        for s in range(1, N_DEV):
            g = g + p_ref[s].astype(F32)
        d, nm, nv = _adamw(w_ref[...], g, m_ref[...], v_ref[...])
        g_ref[...] = g
        d_ref[...] = d
        nm_ref[...] = nm
        nv_ref[...] = nv

    tile = pl.BlockSpec((tr, C), lambda i: (i, 0))
    sh = jax.ShapeDtypeStruct((R, C), F32)
    return pl.pallas_call(
        body, name=name, out_shape=(sh, sh, sh, sh), grid=(R // tr,),
        in_specs=[tile, pl.BlockSpec((N_DEV, tr, C), lambda i: (0, i, 0)), tile, tile],
        out_specs=(tile, tile, tile, tile), compiler_params=_cparams())(w, parts, m, v)


def _pack(arrs, rows_mult=SUBLANES):
    flat = jnp.concatenate([a.reshape(-1).astype(F32) for a in arrs])
    n = flat.shape[0]
    per = rows_mult * LANES
    total = -(-n // per) * per
    return jnp.pad(flat, (0, total - n)).reshape(total // LANES, LANES)


def _unpack(pack, shapes):
    flat = pack.reshape(-1)
    out, off = [], 0
    for s in shapes:
        n = 1
        for d in s:
            n *= d
        out.append(flat[off:off + n].reshape(s))
        off += n
    return out


def kernel(x, meta_tokens, mix_norm_g, w_in, ssm_lambda_re, ssm_lambda_im, ssm_log_dt, ssm_b_re, ssm_b_im, ssm_c_re, ssm_c_im, ssm_d, ssm_w_glu, w_ssm_proj, hgrn_lb_logits, hgrn_norm_g, w_hgrn_proj, w_out, ffn_norm_g, w_up, conv_w, conv_b, w_down, final_norm_g, loss_target, m_meta_tokens, m_mix_norm_g, m_w_in, m_ssm_lambda_re, m_ssm_lambda_im, m_ssm_log_dt, m_ssm_b_re, m_ssm_b_im, m_ssm_c_re, m_ssm_c_im, m_ssm_d, m_ssm_w_glu, m_w_ssm_proj, m_hgrn_lb_logits, m_hgrn_norm_g, m_w_hgrn_proj, m_w_out, m_ffn_norm_g, m_w_up, m_conv_w, m_conv_b, m_w_down, m_final_norm_g, v_meta_tokens, v_mix_norm_g, v_w_in, v_ssm_lambda_re, v_ssm_lambda_im, v_ssm_log_dt, v_ssm_b_re, v_ssm_b_im, v_ssm_c_re, v_ssm_c_im, v_ssm_d, v_ssm_w_glu, v_w_ssm_proj, v_hgrn_lb_logits, v_hgrn_norm_g, v_w_hgrn_proj, v_w_out, v_ffn_norm_g, v_w_up, v_conv_w, v_conv_b, v_w_down, v_final_norm_g):
    args = dict(locals())
    B, S_len, D = x.shape
    L = S_len + N_META
    T = B * L
    tm = _tile(L, ROW_TILE_CAP)
    tps = L // tm
    G, P = ssm_lambda_re.shape[1:]
    H = ssm_b_re.shape[-1]
    W = G * H
    n_cb = W // LANES
    gpb = G // n_cb
    hd = hgrn_norm_g.shape[1]
    n_heads = D // hd
    n_in = w_in.shape[2]
    F = w_up.shape[2]
    assert W == D and n_in % LANES == 0

    me = (4 * lax.axis_index("x") + 2 * lax.axis_index("y") + lax.axis_index("c")).astype(jnp.int32).reshape(1)
    meta_g, cw_g = _exchange_call([meta_tokens, conv_w[0]], False, "gather_small_params")
    ga = _exchange_start([w_in[0].astype(MXU)], False, "gather_a_start")
    gb = _exchange_start(
        [w_up[0].astype(MXU), ssm_w_glu[0].astype(MXU), w_ssm_proj[0].astype(MXU), w_hgrn_proj[0].astype(MXU),
         w_out[0].astype(MXU), w_down[0].astype(MXU)], False, "gather_b_start")
    started_tok = (ga[4] + gb[4])[0:1, 0:1]
    meta_full = meta_g.transpose(1, 0, 2).reshape(N_META, D)
    cb_g = conv_b.reshape(N_DEV, 1, F)

    h0 = jnp.concatenate([jnp.broadcast_to(meta_full[None], (B, N_META, D)), x], axis=1).reshape(T, D)
    tgt = jnp.concatenate([jnp.zeros((B, N_META, D), F32), loss_target], axis=1).reshape(T, D)

    lr, li = ssm_lambda_re[0], ssm_lambda_im[0]
    ldt = ssm_log_dt[0].reshape(G, 1)
    bt_re = ssm_b_re[0].transpose(2, 0, 1).reshape(H, G * P)
    bt_im = ssm_b_im[0].transpose(2, 0, 1).reshape(H, G * P)
    disc = _small_call(_disc_a_powers, [lr, li, ldt], [((G, P), F32)] * (2 * SUBLANES + 2), "s5_discretise")
    pw_re, pw_im = jnp.stack(disc[:SUBLANES]), jnp.stack(disc[SUBLANES:2 * SUBLANES])
    coef_re, coef_im = disc[2 * SUBLANES:]
    bbt_re, bbt_im = _small_call(
        _disc_b, [coef_re.reshape(1, G * P), coef_im.reshape(1, G * P), bt_re, bt_im],
        [((H, G * P), F32)] * 2, "s5_input_matrix")
    eye = jnp.eye(gpb, dtype=F32)
    hw = gpb * P

    def expand_b(bbt):
        return jnp.einsum("hcgp,Gg->cGhgp", bbt.reshape(H, n_cb, gpb, P), eye).reshape(n_cb, gpb * H, hw)

    def expand_c(cm):
        return jnp.einsum("cghp,gG->cgpGh", cm.reshape(n_cb, gpb, H, P), eye).reshape(n_cb, hw, gpb * H)

    wb = jnp.concatenate([expand_b(bbt_re), expand_b(bbt_im)], axis=2).astype(MXU)
    wc = jnp.concatenate([expand_c(ssm_c_re[0]), -expand_c(ssm_c_im[0])], axis=1).astype(MXU)
    pwr = pw_re.reshape(SUBLANES, n_cb, hw).transpose(1, 0, 2)
    pwi = pw_im.reshape(SUBLANES, n_cb, hw).transpose(1, 0, 2)
    rows = jnp.arange(SUBLANES)[None, :, None]

    def table(sign, reverse):
        tabs = []
        for d in (1, 2, 4):
            keep = (rows + d < SUBLANES) if reverse else (rows >= d)
            tabs.append(jnp.concatenate([jnp.where(keep, pwr[:, d - 1:d], 0.0),
                                         jnp.where(keep, sign * pwi[:, d - 1:d], 0.0)], axis=2))
        cr, ci = (pwr[:, ::-1], pwi[:, ::-1]) if reverse else (pwr, pwi)
        tabs.append(jnp.concatenate([cr, sign * ci], axis=2))
        return jnp.stack(tabs, axis=1)

    tab_f, tab_r = table(1.0, False), table(-1.0, True)
    dsk = ssm_d.reshape(n_cb, 1, LANES)
    lb = _small_call(_lb_fn, [hgrn_lb_logits], [((1, D), F32)], "hgrn_lower_bound")[0]

    z1 = _norm_call(h0, mix_norm_g + started_tok, tm, "mix_norm")
    ga_src, ga_land = _exchange_wait(ga, z1, False, "gather_a_wait")
    win_g = _place_own_call(ga_src, ga_land, False, me, "gather_a_own")[0]
    p = _mm_shard(z1, win_g, tm, "in_proj", False)
    p3 = p.reshape(B, L, p.shape[1])
    ya = _s5_fwd_call(p3, wb, wc, tab_f, dsk, "s5_fwd").reshape(T, W)
    gb_src, gb_land = _exchange_wait(gb, ya, False, "gather_b_wait")
    gathered = _place_own_call(gb_src, gb_land, False, me, "gather_b_own")
    wup_g = gathered[0]
    wglu_g, wsp_g, whp_g, wout_g = [g.reshape(D, D) for g in gathered[1:5]]
    wdn_g = gathered[5].reshape(N_DEV // 2, 2 * w_down.shape[1], D)
    yo, a_br = _glu_proj_call(ya, wglu_g, wsp_g, tm, "s5_glu_proj")
    yb = _hgrn_fwd_call(p3, lb, hgrn_norm_g, n_heads, n_cb, "hgrn_fwd").reshape(T, D)
    col_ga = 5
    h1, mg, bm, z2 = _merge_call(yb, a_br, p, h0, whp_g, wout_g, ffn_norm_g, col_ga, tm, "merge")
    up = _mm_shard(z2, wup_g, tm, "up_proj", True)
    act, dh2, loss_part, dg3 = _ffn_fwd_call(up, cw_g, cb_g, wdn_g, h1, tgt, final_norm_g.reshape(1, D),
                                             tm, tps, "ffn_out_loss")

    dua, dub, dwd, dcwa, dcwb, dcba, dcbb = _ffn_bwd_a_call(dh2, up, act, cw_g, cb_g, wdn_g, tm, tps, "ffn_bwd_gate")
    dupa, dupb, dh1, dg2 = _ffn_bwd_b_call(dua, dub, cw_g, wup_g, h1, ffn_norm_g, dh2, tm, tps, "ffn_bwd_up")
    dwup = jnp.concatenate([_mm_tn(z2, dupa, N_DEV // 2, tm, "dw_up_a", True),
                            _mm_tn(z2, dupb, N_DEV // 2, tm, "dw_up_b", True)], axis=0)
    sh_rows = D // N_DEV
    sa = _exchange_start([dwup.astype(WIRE), dwd.reshape(N_DEV, w_down.shape[1], D).astype(WIRE)], True,
                         "scatter_a_start")
    dmg, dwout = _lin_bwd(mg, dh1, wout_g + sa[4][0:1, 0:1].astype(MXU), tm, "out_proj_bwd")
    da_br, dbm, dga, dgb = _merge_bwd_call(dmg, a_br, bm, p, col_ga, tm, "merge_bwd")
    dyo, dwsp = _lin_bwd(yo, da_br, wsp_g, tm, "ssm_proj_bwd")
    dyb, dwhp = _lin_bwd(yb, dbm, whp_g, tm, "hgrn_proj_bwd")
    dya, dwglu = _glu_bwd_call(ya, dyo, wglu_g, tm, "s5_glu_bwd")
    sb = _exchange_start([t.reshape(N_DEV, sh_rows, D).astype(WIRE) for t in (dwglu, dwsp, dwhp, dwout)], True,
                         "scatter_b_start")
    tok_b = sb[4][0:1, :]
    du, dwb, dwc, dab, ddsk = _s5_bwd_call(p3, dya.reshape(B, L, W), wb, wc, tab_f, tab_r, dsk + tok_b[None],
                                           "s5_bwd")

    def diag_b(dw):
        return jnp.einsum("cGhgp,Gg->hcgp", dw.reshape(n_cb, gpb, H, gpb, P), eye).reshape(H, G * P)

    def diag_c(dw):
        return jnp.einsum("cgpGh,gG->cghp", dw.reshape(n_cb, gpb, P, gpb, H), eye).reshape(G, H, P)

    early_parts = [dab[:, 0, :hw].reshape(G, P), dab[:, 0, hw:].reshape(G, P),
                   diag_b(dwb[:, :, :hw]), diag_b(dwb[:, :, hw:]),
                   diag_c(dwc[:, :hw]), -diag_c(dwc[:, hw:]), ddsk.reshape(1, D)]
    early_pack = _pack(early_parts)
    se = _exchange_start([early_pack], False, "gather_s5_grads_start")
    dq, dfl, di, dog, dlb, dng = _hgrn_bwd_call(p3, dyb.reshape(B, L, D), lb, hgrn_norm_g + tok_b + se[4][0:1, :],
                                                n_heads, n_cb, "hgrn_bwd")
    dp = jnp.concatenate([du.reshape(T, W), dq.reshape(T, D), dfl.reshape(T, D), di.reshape(T, D),
                          dog.reshape(T, D), dga, dgb], axis=1)
    dwin = _mm_tn(z1, dp, N_DEV, tm, "dw_in", False)
    sc = _exchange_start([dwin.astype(WIRE)], True, "scatter_c_start")
    dh0, dg1 = _in_bwd_call(dp, win_g, h0, mix_norm_g + sc[4][0:1, 0:1], dh1, tm, "in_proj_bwd")
    dh0_3 = dh0.reshape(B, L, D)
    grad_x = dh0_3[:, N_META:]
    dmeta = _meta_grad_call(dh0_3, "meta_grad")

    late_parts = [dg1, dlb, dng, dg2, jnp.concatenate([dcba, dcbb], axis=0).reshape(1, N_DEV * F), dg3, loss_part]
    late_pack = _pack(late_parts)

    dcw = jnp.concatenate([dcwa, dcwb], axis=0)
    dmeta_s = dmeta.reshape(N_META, N_DEV, D // N_DEV).transpose(1, 0, 2)
    parts_d = _exchange_call([dmeta_s, dcw], True, "scatter_small_grads")
    late_all = _exchange_call([late_pack], False, "gather_small_grads")[0]
    early_all = _place_own_call(*_exchange_wait(se, late_all, False, "gather_s5_grads_wait"), False, me,
                                "gather_s5_grads_own")[0]
    parts_a = _place_own_call(*_exchange_wait(sa, late_all, True, "scatter_a_wait"), True, me, "scatter_a_own")
    parts_b = _place_own_call(*_exchange_wait(sb, late_all, True, "scatter_b_wait"), True, me, "scatter_b_own")
    parts_c = _place_own_call(*_exchange_wait(sc, late_all, True, "scatter_c_wait"), True, me, "scatter_c_own")
    parts = [parts_c[0], parts_a[0], *parts_b, parts_a[1], parts_d[0], parts_d[1]]

    def sum8(a, b):
        ta, tb = a[0], b[0]
        for s in range(1, N_DEV):
            ta, tb = ta + a[s], tb + b[s]
        return ta, tb

    early_sum, late_sum = _small_call(sum8, [early_all, late_all], [(early_pack.shape, F32), (late_pack.shape, F32)],
                                      "sum_small_grads")
    t_abr, t_abi, t_bbr, t_bbi, g_cre, g_cim, g_dsk = _unpack(early_sum, [a.shape for a in early_parts])
    g_g1, t_lb, g_ng, g_g2, g_cb, g_g3, loss_v = _unpack(late_sum, [a.shape for a in late_parts])

    def disc_b_bwd(cr, ci, br, bi, dbr, dbi):
        _, vjp = jax.vjp(_disc_b, cr, ci, br, bi)
        return vjp((dbr, dbi))

    t_cr, t_ci, g_btr, g_bti = _small_call(
        disc_b_bwd, [coef_re.reshape(1, G * P), coef_im.reshape(1, G * P), bt_re, bt_im, t_bbr, t_bbi],
        [((1, G * P), F32)] * 2 + [((H, G * P), F32)] * 2, "s5_input_matrix_bwd")

    def disc_a_bwd(lr_, li_, ldt_, dar, dai, dcr, dci):
        _, vjp = jax.vjp(_disc_a, lr_, li_, ldt_)
        return vjp((dar, dai, dcr, dci))

    g_lr, g_li, g_ldt = _small_call(
        disc_a_bwd, [lr, li, ldt, t_abr, t_abi, t_cr.reshape(G, P), t_ci.reshape(G, P)],
        [((G, P), F32)] * 2 + [((G, 1), F32)], "s5_discretise_bwd")

    def lb_bwd(logits, d):
        _, vjp = jax.vjp(_lb_fn, logits)
        return vjp(d)

    g_lbl = _small_call(lb_bwd, [hgrn_lb_logits, t_lb], [(hgrn_lb_logits.shape, F32)], "hgrn_lower_bound_bwd")[0]

    grads = dict(
        mix_norm_g=g_g1, ssm_lambda_re=g_lr[None], ssm_lambda_im=g_li[None], ssm_log_dt=g_ldt.reshape(1, G),
        ssm_b_re=g_btr.reshape(H, G, P).transpose(1, 2, 0)[None], ssm_b_im=g_bti.reshape(H, G, P).transpose(1, 2, 0)[None],
        ssm_c_re=g_cre[None], ssm_c_im=g_cim[None], ssm_d=g_dsk, hgrn_lb_logits=g_lbl, hgrn_norm_g=g_ng,
        ffn_norm_g=g_g2, conv_b=g_cb.reshape(1, N_DEV * F), final_norm_g=g_g3.reshape(D))
    loss = loss_v[0, 0]

    delta, new_m, new_v = {}, {}, {}
    sharded = [("w_in", parts[0], (D, n_in)), ("w_up", parts[1], (D, F)), ("ssm_w_glu", parts[2], (sh_rows, D)),
               ("w_ssm_proj", parts[3], (sh_rows, D)), ("w_hgrn_proj", parts[4], (sh_rows, D)),
               ("w_out", parts[5], (sh_rows, D)), ("w_down", parts[6], (w_down.shape[1], D)),
               ("meta_tokens", parts[7], (N_META, D // N_DEV)), ("conv_w", parts[8], (3, F))]
    for name, part, shp in sharded:
        full = args[name].shape
        g, d_, nm, nv = _adamw_shard_call(args[name].reshape(shp), part, args["m_" + name].reshape(shp),
                                          args["v_" + name].reshape(shp), "adamw_" + name)
        grads[name], delta[name], new_m[name], new_v[name] = [t.reshape(full) for t in (g, d_, nm, nv)]

    rep = ["mix_norm_g", "ssm_lambda_re", "ssm_lambda_im", "ssm_log_dt", "ssm_b_re", "ssm_b_im", "ssm_c_re",
           "ssm_c_im", "ssm_d", "hgrn_lb_logits", "hgrn_norm_g", "ffn_norm_g", "conv_b", "final_norm_g"]
    rep_shapes = [args[n].shape for n in rep]
    packs = [_pack([args[pre + n] for n in rep]) for pre in ("", "m_", "v_")]
    g_pack = _pack([grads[n] for n in rep])
    outs = _small_call(lambda w, g, m, v: _adamw(w, g, m, v), [packs[0], g_pack, packs[1], packs[2]],
                       [(g_pack.shape, F32)] * 3, "adamw_replicated")
    for n, d_, nm, nv in zip(rep, *[_unpack(o, rep_shapes) for o in outs]):
        delta[n], new_m[n], new_v[n] = d_, nm, nv

    names = ["meta_tokens", "mix_norm_g", "w_in", "ssm_lambda_re", "ssm_lambda_im", "ssm_log_dt", "ssm_b_re",
             "ssm_b_im", "ssm_c_re", "ssm_c_im", "ssm_d", "ssm_w_glu", "w_ssm_proj", "hgrn_lb_logits", "hgrn_norm_g",
             "w_hgrn_proj", "w_out", "ffn_norm_g", "w_up", "conv_w", "conv_b", "w_down", "final_norm_g"]
    return (loss, grad_x, *[grads[n] for n in names], *[delta[n] for n in names],
            *[new_m[n] for n in names], *[new_v[n] for n in names])
```

```python
import functools

import jax
import jax.numpy as jnp
from jax import lax
from jax.experimental import pallas as pl
from jax.experimental.pallas import tpu as pltpu

F32 = jnp.float32
MXU = jnp.bfloat16
ACT = jnp.bfloat16
WIRE = jnp.bfloat16
N_DEV = 8
N_META = 16
CHUNK = 16
EPS = 1e-6
ADAM_LR, ADAM_B1, ADAM_B2, ADAM_EPS, ADAM_WD, ADAM_STEP = 0.001, 0.9, 0.999, 1e-08, 0.01, 10
SUBLANES = 8
LANES = 128
ROW_TILE_CAP = 700
VMEM_LIMIT = 60 * 1024 * 1024


def _cparams(**kw):
    return pltpu.CompilerParams(vmem_limit_bytes=VMEM_LIMIT, **kw)


def _tile(n, cap):
    best = None
    for t in range(16, min(n, cap) + 1, 16):
        if n % t == 0:
            best = t
    assert best is not None, (n, cap)
    return best


def _dot(a, b):
    return lax.dot_general(a.astype(MXU), b.astype(MXU), (((1,), (0,)), ((), ())), preferred_element_type=F32)


def _dot_nt(a, b):
    return lax.dot_general(a.astype(MXU), b.astype(MXU), (((1,), (1,)), ((), ())), preferred_element_type=F32)


def _dot_tn(a, b):
    return lax.dot_general(a.astype(MXU), b.astype(MXU), (((0,), (0,)), ((), ())), preferred_element_type=F32)


def _rms(x, g):
    return x * lax.rsqrt(jnp.mean(x * x, axis=-1, keepdims=True) + EPS) * g


def _silu(x):
    return x * jax.nn.sigmoid(x)


def _small_call(fn, ins, out_shapes, name):
    n_in = len(ins)

    def body(*refs):
        outs = fn(*[r[...] for r in refs[:n_in]])
        outs = outs if isinstance(outs, (tuple, list)) else (outs,)
        for r, o in zip(refs[n_in:], outs):
            r[...] = o.astype(r.dtype)

    vm = pl.BlockSpec(memory_space=pltpu.VMEM)
    return pl.pallas_call(
        body, name=name, out_shape=tuple(jax.ShapeDtypeStruct(s, d) for s, d in out_shapes),
        in_specs=[vm] * n_in, out_specs=tuple([vm] * len(out_shapes)), compiler_params=_cparams())(*ins)


def _disc_a(lr, li, ldt):
    dt = jnp.exp(ldt)
    mag = jnp.exp(lr * dt)
    ab_re = mag * jnp.cos(li * dt)
    ab_im = mag * jnp.sin(li * dt)
    den = lr * lr + li * li
    nr = ab_re - 1.0
    coef_re = (nr * lr + ab_im * li) / den
    coef_im = (ab_im * lr - nr * li) / den
    return ab_re, ab_im, coef_re, coef_im


def _disc_a_powers(lr, li, ldt):
    ab_re, ab_im, coef_re, coef_im = _disc_a(lr, li, ldt)
    pr, pi = [ab_re], [ab_im]
    for _ in range(SUBLANES - 1):
        pr, pi = pr + [pr[-1] * ab_re - pi[-1] * ab_im], pi + [pr[-1] * ab_im + pi[-1] * ab_re]
    return (*pr, *pi, coef_re, coef_im)


def _disc_b(coef_re, coef_im, bt_re, bt_im):
    return coef_re * bt_re - coef_im * bt_im, coef_re * bt_im + coef_im * bt_re


def _lb_fn(logits):
    return jax.nn.softmax(logits, axis=0)[0:1]


def _adamw(w, g, m, v):
    m = ADAM_B1 * m + (1.0 - ADAM_B1) * g
    v = ADAM_B2 * v + (1.0 - ADAM_B2) * jnp.square(g)
    m_hat = m / (1.0 - ADAM_B1 ** ADAM_STEP)
    v_hat = v / (1.0 - ADAM_B2 ** ADAM_STEP)
    delta = -ADAM_LR * (m_hat / (jnp.sqrt(v_hat) + ADAM_EPS) + ADAM_WD * w)
    return delta, m, v


def _norm_call(h, g, tm, name):
    T, D = h.shape

    def body(h_ref, g_ref, z_ref):
        z_ref[...] = _rms(h_ref[...], g_ref[...]).astype(ACT)

    return pl.pallas_call(
        body, name=name, out_shape=jax.ShapeDtypeStruct((T, D), ACT), grid=(T // tm,),
        in_specs=[pl.BlockSpec((tm, D), lambda i: (i, 0)), pl.BlockSpec((1, D), lambda i: (0, 0))],
        out_specs=pl.BlockSpec((tm, D), lambda i: (i, 0)), compiler_params=_cparams())(h, g)


def _mm_shard(x, w, tm, name, major):
    T, K = x.shape
    S, _, N = w.shape

    def body(x_ref, w_ref, o_ref):
        o_ref[...] = _dot(x_ref[...], w_ref[...]).astype(o_ref.dtype)

    if major:
        out_shape = jax.ShapeDtypeStruct((S, T, N), ACT)
        out_spec = pl.BlockSpec((None, tm, N), lambda j, i: (j, i, 0))
    else:
        out_shape = jax.ShapeDtypeStruct((T, S * N), ACT)
        out_spec = pl.BlockSpec((tm, N), lambda j, i: (i, j))
    return pl.pallas_call(
        body, name=name, out_shape=out_shape, grid=(S, T // tm),
        in_specs=[pl.BlockSpec((tm, K), lambda j, i: (i, 0)), pl.BlockSpec((None, K, N), lambda j, i: (j, 0, 0))],
        out_specs=out_spec, compiler_params=_cparams())(x, w)


def _mm_tn(x, y, n_shards, tm, name, major):
    T, K = x.shape
    S = n_shards
    N = y.shape[-1] if major else y.shape[-1] // S

    def body(x_ref, y_ref, o_ref):
        @pl.when(pl.program_id(1) == 0)
        def _():
            o_ref[...] = jnp.zeros_like(o_ref)
        o_ref[...] += _dot_tn(x_ref[...], y_ref[...])

    y_spec = (pl.BlockSpec((None, tm, N), lambda j, i: (j, i, 0)) if major
              else pl.BlockSpec((tm, N), lambda j, i: (i, j)))
    return pl.pallas_call(
        body, name=name, out_shape=jax.ShapeDtypeStruct((S, K, N), F32), grid=(S, T // tm),
        in_specs=[pl.BlockSpec((tm, K), lambda j, i: (i, 0)), y_spec],
        out_specs=pl.BlockSpec((None, K, N), lambda j, i: (j, 0, 0)), compiler_params=_cparams())(x, y)


def _lin_bwd(x, dy, w, tm, name):
    T, K = x.shape
    N = dy.shape[1]

    def body(x_ref, dy_ref, w_ref, dx_ref, dw_ref):
        @pl.when(pl.program_id(0) == 0)
        def _():
            dw_ref[...] = jnp.zeros_like(dw_ref)
        dy = dy_ref[...]
        dx_ref[...] = _dot_nt(dy, w_ref[...]).astype(dx_ref.dtype)
        dw_ref[...] += _dot_tn(x_ref[...], dy)

    return pl.pallas_call(
        body, name=name,
        out_shape=(jax.ShapeDtypeStruct((T, K), ACT), jax.ShapeDtypeStruct((K, N), F32)), grid=(T // tm,),
        in_specs=[pl.BlockSpec((tm, K), lambda i: (i, 0)), pl.BlockSpec((tm, N), lambda i: (i, 0)),
                  pl.BlockSpec((K, N), lambda i: (0, 0))],
        out_specs=(pl.BlockSpec((tm, K), lambda i: (i, 0)), pl.BlockSpec((K, N), lambda i: (0, 0))),
        compiler_params=_cparams())(x, dy, w)


def _scan_slabs(x_ref, tab_ref, n_slabs, reverse):
    hw = x_ref.shape[1] // 2
    tabs = [tab_ref[s] for s in range(4)]

    def cmul(t, xr, xi):
        tr, ti = t[:, :hw], t[:, hw:]
        return tr * xr - ti * xi, tr * xi + ti * xr

    def step(k, carry):
        cr, ci = carry
        kk = (n_slabs - 1 - k) if reverse else k
        r0 = pl.multiple_of(kk * SUBLANES, SUBLANES)
        x = x_ref[pl.ds(r0, SUBLANES), :]
        xr, xi = x[:, :hw], x[:, hw:]
        for s, d in enumerate((1, 2, 4)):
            sh = (SUBLANES - d) if reverse else d
            ar, ai = cmul(tabs[s], pltpu.roll(xr, sh, 0), pltpu.roll(xi, sh, 0))
            xr, xi = xr + ar, xi + ai
        pr, pi = cmul(tabs[3], cr, ci)
        xr, xi = xr + pr, xi + pi
        x_ref[pl.ds(r0, SUBLANES), 0:hw] = xr
        x_ref[pl.ds(r0, SUBLANES), hw:2 * hw] = xi
        e = 0 if reverse else SUBLANES - 1
        return xr[e:e + 1], xi[e:e + 1]

    z = jnp.zeros((1, hw), F32)
    lax.fori_loop(0, n_slabs, step, (z, z))


def _s5_fwd_call(p3, wb, wc, tab_f, dsk, name):
    B, L, _ = p3.shape
    n_cb, cw, sw = wb.shape

    def body(u_ref, wb_ref, wc_ref, tab_ref, d_ref, ya_ref, s_ref):
        u = u_ref[...]
        s_ref[...] = _dot(u, wb_ref[...])
        _scan_slabs(s_ref, tab_ref, L // SUBLANES, False)
        y = _dot(s_ref[...], wc_ref[...]) + d_ref[...] * u.astype(F32)
        ya_ref[...] = jax.nn.gelu(y).astype(ACT)

    return pl.pallas_call(
        body, name=name, out_shape=jax.ShapeDtypeStruct((B, L, n_cb * cw), ACT), grid=(B, n_cb),
        in_specs=[pl.BlockSpec((None, L, cw), lambda b, c: (b, 0, c)),
                  pl.BlockSpec((None, cw, sw), lambda b, c: (c, 0, 0)),
                  pl.BlockSpec((None, sw, cw), lambda b, c: (c, 0, 0)),
                  pl.BlockSpec((None, 4, SUBLANES, sw), lambda b, c: (c, 0, 0, 0)),
                  pl.BlockSpec((None, 1, cw), lambda b, c: (c, 0, 0))],
        out_specs=pl.BlockSpec((None, L, cw), lambda b, c: (b, 0, c)),
        scratch_shapes=[pltpu.VMEM((L, sw), F32)], compiler_params=_cparams())(p3, wb, wc, tab_f, dsk)


def _s5_bwd_call(p3, dya, wb, wc, tab_f, tab_r, dsk, name):
    B, L, _ = p3.shape
    n_cb, cw, sw = wb.shape
    hw = sw // 2
    n_slabs = L // SUBLANES

    def body(u_ref, dya_ref, wb_ref, wc_ref, tf_ref, tr_ref, d_ref,
             du_ref, dwb_ref, dwc_ref, da_ref, dd_ref, s_ref, l_ref):
        @pl.when(pl.program_id(1) == 0)
        def _():
            dwb_ref[...] = jnp.zeros_like(dwb_ref)
            dwc_ref[...] = jnp.zeros_like(dwc_ref)
            da_ref[...] = jnp.zeros_like(da_ref)
            dd_ref[...] = jnp.zeros_like(dd_ref)

        u = u_ref[...]
        uf = u.astype(F32)
        s_ref[...] = _dot(u, wb_ref[...])
        _scan_slabs(s_ref, tf_ref, n_slabs, False)
        y = _dot(s_ref[...], wc_ref[...]) + d_ref[...] * uf
        _, gelu_vjp = jax.vjp(jax.nn.gelu, y)
        dy = gelu_vjp(dya_ref[...].astype(F32))[0]
        dd_ref[...] += jnp.sum(dy * uf, axis=0, keepdims=True)
        l_ref[...] = _dot_nt(dy, wc_ref[...])
        _scan_slabs(l_ref, tr_ref, n_slabs, True)
        du_ref[...] = (_dot_nt(l_ref[...], wb_ref[...]) + d_ref[...] * dy).astype(ACT)
        dwb_ref[...] += _dot_tn(u, l_ref[...])
        dwc_ref[...] += _dot_tn(s_ref[...], dy)

        row = lax.broadcasted_iota(jnp.int32, (SUBLANES, hw), 0)

        def step(k, carry):
            pr, pi, accr, acci = carry
            r0 = pl.multiple_of(k * SUBLANES, SUBLANES)
            s = s_ref[pl.ds(r0, SUBLANES), :]
            lam = l_ref[pl.ds(r0, SUBLANES), :]
            sr, si = s[:, :hw], s[:, hw:]
            lr, li = lam[:, :hw], lam[:, hw:]
            qr = jnp.where(row == 0, pr, pltpu.roll(sr, 1, 0))
            qi = jnp.where(row == 0, pi, pltpu.roll(si, 1, 0))
            accr = accr + lr * qr + li * qi
            acci = acci + li * qr - lr * qi
            return sr[SUBLANES - 1:], si[SUBLANES - 1:], accr, acci

        z1 = jnp.zeros((1, hw), F32)
        z8 = jnp.zeros((SUBLANES, hw), F32)
        _, _, accr, acci = lax.fori_loop(0, n_slabs, step, (z1, z1, z8, z8))
        da_ref[...] += jnp.concatenate([jnp.sum(accr, axis=0, keepdims=True),
                                        jnp.sum(acci, axis=0, keepdims=True)], axis=1)

    W = n_cb * cw
    return pl.pallas_call(
        body, name=name,
        out_shape=(jax.ShapeDtypeStruct((B, L, W), ACT), jax.ShapeDtypeStruct((n_cb, cw, sw), F32),
                   jax.ShapeDtypeStruct((n_cb, sw, cw), F32), jax.ShapeDtypeStruct((n_cb, 1, sw), F32),
                   jax.ShapeDtypeStruct((n_cb, 1, cw), F32)),
        grid=(n_cb, B),
        in_specs=[pl.BlockSpec((None, L, cw), lambda c, b: (b, 0, c)),
                  pl.BlockSpec((None, L, cw), lambda c, b: (b, 0, c)),
                  pl.BlockSpec((None, cw, sw), lambda c, b: (c, 0, 0)),
                  pl.BlockSpec((None, sw, cw), lambda c, b: (c, 0, 0)),
                  pl.BlockSpec((None, 4, SUBLANES, sw), lambda c, b: (c, 0, 0, 0)),
                  pl.BlockSpec((None, 4, SUBLANES, sw), lambda c, b: (c, 0, 0, 0)),
                  pl.BlockSpec((None, 1, cw), lambda c, b: (c, 0, 0))],
        out_specs=(pl.BlockSpec((None, L, cw), lambda c, b: (b, 0, c)),
                   pl.BlockSpec((None, cw, sw), lambda c, b: (c, 0, 0)),
                   pl.BlockSpec((None, sw, cw), lambda c, b: (c, 0, 0)),
                   pl.BlockSpec((None, 1, sw), lambda c, b: (c, 0, 0)),
                   pl.BlockSpec((None, 1, cw), lambda c, b: (c, 0, 0))),
        scratch_shapes=[pltpu.VMEM((L, sw), F32), pltpu.VMEM((L, sw), F32)],
        compiler_params=_cparams())(p3, dya, wb, wc, tab_f, tab_r, dsk)


def _glu_proj_call(ya, wglu, wproj, tm, name):
    T, W = ya.shape
    D = wproj.shape[1]

    def body(ya_ref, wg_ref, wp_ref, yo_ref, a_ref):
        ya = ya_ref[...]
        yo = ya.astype(F32) * jax.nn.sigmoid(_dot(ya, wg_ref[...]))
        yo_ref[...] = yo.astype(ACT)
        a_ref[...] = _dot(yo, wp_ref[...]).astype(ACT)

    return pl.pallas_call(
        body, name=name, out_shape=(jax.ShapeDtypeStruct((T, W), ACT), jax.ShapeDtypeStruct((T, D), ACT)),
        grid=(T // tm,),
        in_specs=[pl.BlockSpec((tm, W), lambda i: (i, 0)), pl.BlockSpec((W, W), lambda i: (0, 0)),
                  pl.BlockSpec((W, D), lambda i: (0, 0))],
        out_specs=(pl.BlockSpec((tm, W), lambda i: (i, 0)), pl.BlockSpec((tm, D), lambda i: (i, 0))),
        compiler_params=_cparams())(ya, wglu, wproj)


def _glu_bwd_call(ya, dyo, wglu, tm, name):
    T, W = ya.shape

    def body(ya_ref, dyo_ref, wg_ref, dya_ref, dwg_ref):
        @pl.when(pl.program_id(0) == 0)
        def _():
            dwg_ref[...] = jnp.zeros_like(dwg_ref)
        ya = ya_ref[...]
        yaf = ya.astype(F32)
        dyo = dyo_ref[...].astype(F32)
        sg = jax.nn.sigmoid(_dot(ya, wg_ref[...]))
        dt = dyo * yaf * sg * (1.0 - sg)
        dya_ref[...] = (dyo * sg + _dot_nt(dt, wg_ref[...])).astype(ACT)
        dwg_ref[...] += _dot_tn(ya, dt)

    return pl.pallas_call(
        body, name=name, out_shape=(jax.ShapeDtypeStruct((T, W), ACT), jax.ShapeDtypeStruct((W, W), F32)),
        grid=(T // tm,),
        in_specs=[pl.BlockSpec((tm, W), lambda i: (i, 0)), pl.BlockSpec((tm, W), lambda i: (i, 0)),
                  pl.BlockSpec((W, W), lambda i: (0, 0))],
        out_specs=(pl.BlockSpec((tm, W), lambda i: (i, 0)), pl.BlockSpec((W, W), lambda i: (0, 0))),
        compiler_params=_cparams())(ya, dyo, wglu)


PAD = 16


def _chunk_cumsums(x, pad_ref, L):
    row = lax.broadcasted_iota(jnp.int32, x.shape, 0) % CHUNK
    zeros = jnp.zeros((PAD, x.shape[1]), F32)
    pad_ref[0:PAD, :] = zeros
    pad_ref[PAD + L:2 * PAD + L, :] = zeros
    c = x
    r = x
    d = 1
    while d < CHUNK:
        pad_ref[PAD:PAD + L, :] = c
        c = c + jnp.where(row >= d, pad_ref[PAD - d:PAD - d + L, :], 0.0)
        pad_ref[PAD:PAD + L, :] = r
        r = r + jnp.where(row + d < CHUNK, pad_ref[PAD + d:PAD + d + L, :], 0.0)
        d *= 2
    return c, r - x


def _hgrn_prep(q_ref, fl_ref, lb_ref, pad_ref, r0, n):
    rows = pl.ds(r0, n)
    lb = lb_ref[...]
    sig = jax.nn.sigmoid(fl_ref[rows, :].astype(F32))
    f = lb + (1.0 - lb) * sig
    k = 1.0 - f
    c, rc = _chunk_cumsums(jnp.log(f), pad_ref, n)
    e_in, e_inv, e_out = jnp.exp(c), jnp.exp(-c), jnp.exp(rc)
    q = q_ref[rows, :].astype(F32)
    return dict(sig=sig, f=f, k=k, q=q, e_in=e_in, e_inv=e_inv, e_out=e_out, dec=jnp.exp(c + rc))


def _for_row_blocks(L, fn):
    full = L // GROUP
    if full:
        def step(g, carry):
            fn(pl.multiple_of(g * GROUP, GROUP), GROUP)
            return carry
        lax.fori_loop(0, full, step, 0)
    if L % GROUP:
        fn(full * GROUP, L % GROUP)


def _chunk_mask(rb):
    r = lax.broadcasted_iota(jnp.int32, (rb, rb), 0)
    c = lax.broadcasted_iota(jnp.int32, (rb, rb), 1)
    return (r // CHUNK == c // CHUNK) & (c <= r)


def _hg_out(o, og, g):
    on = o * lax.rsqrt(jnp.mean(o * o, axis=-1, keepdims=True) + EPS) * g
    return on * _silu(og)


def _hgrn_specs(L, hd, col_q, n_heads, order):
    def spec(sec):
        return pl.BlockSpec((None, L, hd), lambda *g: (order(*g)[0], 0, col_q + sec * n_heads + order(*g)[1]))
    return [spec(0), spec(1), spec(2), spec(3)]


GROUP = 128
CPG = GROUP // CHUNK


def _expand(x):
    xf = x.astype(F32)
    chunk = lax.broadcasted_iota(jnp.int32, xf.shape, 0) // CHUNK
    return jnp.concatenate([jnp.where(chunk == j, xf, 0.0) for j in range(CPG)], axis=1)


def _fill_tail(refs_fills, L):
    for ref, fill in refs_fills:
        if ref.shape[0] > L:
            ref[L:ref.shape[0], :] = jnp.full((ref.shape[0] - L, ref.shape[1]), fill, ref.dtype)


GROUP_UNROLL = 4


def _hgrn_forward_core(q_ref, fl_ref, v_ref, lb_ref, pad_ref, qin_ref, kin_ref, kout_ref, vp_ref, dec_ref, o_ref,
                       s_ref, L):
    hd = qin_ref.shape[1]
    n_groups = qin_ref.shape[0] // GROUP

    def prep(r0, n):
        pp = _hgrn_prep(q_ref, fl_ref, lb_ref, pad_ref, r0, n)
        rows = pl.ds(r0, n)
        qin_ref[rows, :] = (pp["q"] * pp["e_in"]).astype(MXU)
        kin_ref[rows, :] = (pp["k"] * pp["e_inv"]).astype(MXU)
        kout_ref[rows, :] = (pp["k"] * pp["e_out"]).astype(MXU)
        vp_ref[rows, :] = v_ref[rows, :].astype(MXU)
        dec_ref[rows, :] = pp["dec"]

    _for_row_blocks(L, prep)
    _fill_tail(((qin_ref, 0.0), (kin_ref, 0.0), (kout_ref, 0.0), (vp_ref, 0.0), (dec_ref, 1.0)), L)
    mask = _chunk_mask(GROUP)

    def intra(g, carry):
        rows = pl.ds(pl.multiple_of(g * GROUP, GROUP), GROUP)
        a = jnp.where(mask, _dot_nt(qin_ref[rows, :], kin_ref[rows, :]), 0.0)
        o_ref[rows, :] = _dot(a, vp_ref[rows, :])
        kv = _dot_tn(vp_ref[rows, :], _expand(kout_ref[rows, :]))
        for j in range(CPG):
            s_ref[g * CPG + j] = kv[:, j * hd:(j + 1) * hd]
        return carry

    lax.fori_loop(0, n_groups, intra, 0, unroll=GROUP_UNROLL)

    def rec(n, st):
        kv = s_ref[n]
        s_ref[n] = st
        dec = dec_ref[pl.ds(pl.multiple_of(n * CHUNK, CHUNK), SUBLANES), :][0:1]
        return st * dec + kv

    lax.fori_loop(0, L // CHUNK, rec, jnp.zeros((hd, hd), F32))

    def inter(g, carry):
        rows = pl.ds(pl.multiple_of(g * GROUP, GROUP), GROUP)
        scat = jnp.concatenate([s_ref[g * CPG + j] for j in range(CPG)], axis=1)
        o_ref[rows, :] += _dot_nt(_expand(qin_ref[rows, :]), scat)
        return carry

    lax.fori_loop(0, n_groups, inter, 0, unroll=GROUP_UNROLL)


def _hgrn_scratch(L, hd):
    lp = -(-L // GROUP) * GROUP
    return lp, [pltpu.VMEM((GROUP + 2 * PAD, hd), F32), pltpu.VMEM((lp, hd), MXU), pltpu.VMEM((lp, hd), MXU),
                pltpu.VMEM((lp, hd), MXU), pltpu.VMEM((lp, hd), MXU), pltpu.VMEM((lp, hd), F32),
                pltpu.VMEM((lp, hd), F32), pltpu.VMEM((lp // CHUNK, hd, hd), F32)]


def _hgrn_fwd_call(p3, lb, ng, n_heads, col_q, name):
    B, L, _ = p3.shape
    hd = ng.shape[1]
    _, scratch = _hgrn_scratch(L, hd)

    def body(q_ref, fl_ref, v_ref, og_ref, lb_ref, ng_ref, yb_ref,
             pad_ref, qin_ref, kin_ref, kout_ref, vp_ref, dec_ref, o_ref, s_ref):
        _hgrn_forward_core(q_ref, fl_ref, v_ref, lb_ref, pad_ref, qin_ref, kin_ref, kout_ref, vp_ref, dec_ref,
                           o_ref, s_ref, L)

        def out(r0, n):
            rows = pl.ds(r0, n)
            yb_ref[rows, :] = _hg_out(o_ref[rows, :], og_ref[rows, :].astype(F32), ng_ref[...]).astype(ACT)

        _for_row_blocks(L, out)

    order = lambda b, h: (b, h)
    return pl.pallas_call(
        body, name=name, out_shape=jax.ShapeDtypeStruct((B, L, n_heads * hd), ACT), grid=(B, n_heads),
        in_specs=_hgrn_specs(L, hd, col_q, n_heads, order) + [
            pl.BlockSpec((1, hd), lambda b, h: (0, h)), pl.BlockSpec((1, hd), lambda b, h: (0, 0))],
        out_specs=pl.BlockSpec((None, L, hd), lambda b, h: (b, 0, h)),
        scratch_shapes=scratch, compiler_params=_cparams())(p3, p3, p3, p3, lb, ng)


def _hgrn_bwd_call(p3, dyb, lb, ng, n_heads, col_q, name):
    B, L, _ = p3.shape
    hd = ng.shape[1]
    n_chunks = L // CHUNK
    lp, scratch = _hgrn_scratch(L, hd)
    n_groups = lp // GROUP

    def body(q_ref, fl_ref, v_ref, og_ref, dyb_ref, lb_ref, ng_ref,
             dq_ref, dfl_ref, dv_ref, dog_ref, dlb_ref, dng_ref,
             pad_ref, qin_ref, kin_ref, kout_ref, vp_ref, dec_ref, o_ref, s_ref,
             do_ref, ds_ref, dqi_ref, dki_ref, dko_ref, dvv_ref, dct_ref):
        @pl.when(pl.program_id(1) == 0)
        def _():
            dlb_ref[...] = jnp.zeros_like(dlb_ref)

        @pl.when((pl.program_id(0) == 0) & (pl.program_id(1) == 0))
        def _():
            dng_ref[...] = jnp.zeros_like(dng_ref)

        _hgrn_forward_core(q_ref, fl_ref, v_ref, lb_ref, pad_ref, qin_ref, kin_ref, kout_ref, vp_ref, dec_ref,
                           o_ref, s_ref, L)

        def out_bwd(r0, n):
            rows = pl.ds(r0, n)
            _, out_vjp = jax.vjp(_hg_out, o_ref[rows, :], og_ref[rows, :].astype(F32), ng_ref[...])
            d_o, d_og, d_ng = out_vjp(dyb_ref[rows, :].astype(F32))
            dog_ref[rows, :] = d_og.astype(ACT)
            dng_ref[...] += d_ng
            do_ref[rows, :] = d_o.astype(MXU)

        _for_row_blocks(L, out_bwd)
        _fill_tail(((do_ref, 0.0),), L)
        mask = _chunk_mask(GROUP)

        def grads_a(g, carry):
            rows = pl.ds(pl.multiple_of(g * GROUP, GROUP), GROUP)
            qi, ki, vv, do = qin_ref[rows, :], kin_ref[rows, :], vp_ref[rows, :], do_ref[rows, :]
            a = jnp.where(mask, _dot_nt(qi, ki), 0.0)
            da = jnp.where(mask, _dot_nt(do, vv), 0.0)
            sstack = s_ref[pl.ds(g * CPG, CPG)].reshape(CPG * hd, hd)
            dqi_ref[rows, :] = _dot(da, ki) + _dot(_expand(do), sstack)
            dki_ref[rows, :] = _dot_tn(da, qi)
            dvv_ref[rows, :] = _dot_tn(a, do)
            x = _dot_tn(do, _expand(qi))
            for j in range(CPG):
                ds_ref[g * CPG + j] = x[:, j * hd:(j + 1) * hd]
            return carry

        lax.fori_loop(0, n_groups, grads_a, 0, unroll=GROUP_UNROLL)

        def rec_bwd(k, dst):
            n = n_chunks - 1 - k
            r0 = pl.multiple_of(n * CHUNK, CHUNK)
            x = ds_ref[n]
            ds_ref[n] = dst
            dec = dec_ref[pl.ds(r0, SUBLANES), :][0:1]
            ddec = dec * jnp.sum(dst * s_ref[n], axis=0, keepdims=True)
            dct_ref[pl.ds(r0, CHUNK), :] = jnp.broadcast_to(ddec, (CHUNK, hd))
            return dst * dec + x

        lax.fori_loop(0, n_chunks, rec_bwd, jnp.zeros((hd, hd), F32))

        def grads_b(g, carry):
            rows = pl.ds(pl.multiple_of(g * GROUP, GROUP), GROUP)
            dscat = jnp.concatenate([ds_ref[g * CPG + j] for j in range(CPG)], axis=1)
            dvv_ref[rows, :] += _dot_nt(_expand(kout_ref[rows, :]), dscat)
            dstack = ds_ref[pl.ds(g * CPG, CPG)].reshape(CPG * hd, hd)
            dko_ref[rows, :] = _dot(_expand(vp_ref[rows, :]), dstack)
            return carry

        lax.fori_loop(0, n_groups, grads_b, 0, unroll=GROUP_UNROLL)

        def finish(r0, n):
            rows = pl.ds(r0, n)
            pp = _hgrn_prep(q_ref, fl_ref, lb_ref, pad_ref, r0, n)
            dqi, dki, dko = dqi_ref[rows, :], dki_ref[rows, :], dko_ref[rows, :]
            dq = dqi * pp["e_in"]
            dk = dki * pp["e_inv"] + dko * pp["e_out"]
            dq_ref[rows, :] = dq.astype(ACT)
            dv_ref[rows, :] = dvv_ref[rows, :].astype(ACT)
            t_out = pp["k"] * pp["e_out"] * dko
            dc = pp["q"] * pp["e_in"] * dqi - pp["k"] * pp["e_inv"] * dki - t_out
            _, dc_later = _chunk_cumsums(dc, pad_ref, n)
            t_incl, t_later = _chunk_cumsums(t_out, pad_ref, n)
            dlogf = dc + dc_later + t_incl + t_later + dct_ref[rows, :]
            df = dlogf / pp["f"] - dk
            sig = pp["sig"]
            dfl_ref[rows, :] = (df * (1.0 - lb_ref[...]) * sig * (1.0 - sig)).astype(ACT)
            dlb_ref[...] += jnp.sum(df * (1.0 - sig), axis=0, keepdims=True)

        _for_row_blocks(L, finish)

    order = lambda h, b: (b, h)
    W = n_heads * hd
    act_out = jax.ShapeDtypeStruct((B, L, W), ACT)
    blk_out = pl.BlockSpec((None, L, hd), lambda h, b: (b, 0, h))
    return pl.pallas_call(
        body, name=name,
        out_shape=(act_out, act_out, act_out, act_out, jax.ShapeDtypeStruct((1, W), F32),
                   jax.ShapeDtypeStruct((1, hd), F32)),
        grid=(n_heads, B),
        in_specs=_hgrn_specs(L, hd, col_q, n_heads, order) + [
            pl.BlockSpec((None, L, hd), lambda h, b: (b, 0, h)),
            pl.BlockSpec((1, hd), lambda h, b: (0, h)), pl.BlockSpec((1, hd), lambda h, b: (0, 0))],
        out_specs=(blk_out, blk_out, blk_out, blk_out, pl.BlockSpec((1, hd), lambda h, b: (0, h)),
                   pl.BlockSpec((1, hd), lambda h, b: (0, 0))),
        scratch_shapes=scratch + [
            pltpu.VMEM((lp, hd), MXU), pltpu.VMEM((lp // CHUNK, hd, hd), F32)] + [pltpu.VMEM((lp, hd), F32)] * 5,
        compiler_params=_cparams())(p3, p3, p3, p3, dyb, lb, ng)


def _merge_fn(a, bm, ga, gb):
    return jax.nn.sigmoid(ga) * a + jax.nn.sigmoid(gb) * bm


def _merge_call(yb, a, p, h0, whp, wout, g2, col_ga, tm, name):
    T, D = h0.shape

    def body(yb_ref, a_ref, ga_ref, gb_ref, h0_ref, whp_ref, wout_ref, g2_ref, h1_ref, mg_ref, bm_ref, z2_ref):
        bm = _dot(yb_ref[...], whp_ref[...])
        mg = _merge_fn(a_ref[...].astype(F32), bm, ga_ref[...].astype(F32), gb_ref[...].astype(F32))
        h1 = h0_ref[...] + _dot(mg, wout_ref[...])
        h1_ref[...] = h1
        mg_ref[...] = mg.astype(ACT)
        bm_ref[...] = bm.astype(ACT)
        z2_ref[...] = _rms(h1, g2_ref[...]).astype(ACT)

    tile = pl.BlockSpec((tm, D), lambda i: (i, 0))
    full = pl.BlockSpec((D, D), lambda i: (0, 0))
    act = jax.ShapeDtypeStruct((T, D), ACT)
    return pl.pallas_call(
        body, name=name, out_shape=(jax.ShapeDtypeStruct((T, D), F32), act, act, act), grid=(T // tm,),
        in_specs=[tile, tile, pl.BlockSpec((tm, D), lambda i: (i, col_ga)),
                  pl.BlockSpec((tm, D), lambda i: (i, col_ga + 1)), tile, full, full,
                  pl.BlockSpec((1, D), lambda i: (0, 0))],
        out_specs=(tile, tile, tile, tile), compiler_params=_cparams())(yb, a, p, p, h0, whp, wout, g2)


def _merge_bwd_call(dmg, a, bm, p, col_ga, tm, name):
    T, D = dmg.shape

    def body(dmg_ref, a_ref, bm_ref, ga_ref, gb_ref, da_ref, dbm_ref, dga_ref, dgb_ref):
        args = [r[...].astype(F32) for r in (a_ref, bm_ref, ga_ref, gb_ref)]
        _, vjp = jax.vjp(_merge_fn, *args)
        for r, o in zip((da_ref, dbm_ref, dga_ref, dgb_ref), vjp(dmg_ref[...].astype(F32))):
            r[...] = o.astype(ACT)

    tile = pl.BlockSpec((tm, D), lambda i: (i, 0))
    act = jax.ShapeDtypeStruct((T, D), ACT)
    return pl.pallas_call(
        body, name=name, out_shape=(act, act, act, act), grid=(T // tm,),
        in_specs=[tile, tile, tile, pl.BlockSpec((tm, D), lambda i: (i, col_ga)),
                  pl.BlockSpec((tm, D), lambda i: (i, col_ga + 1))],
        out_specs=(tile, tile, tile, tile), compiler_params=_cparams())(dmg, a, bm, p, p)


def _conv_taps(x_ref, halo_ref, ext_ref, edge, tm, before):
    halo = jnp.where(edge, 0.0, halo_ref[...].astype(F32))
    x = x_ref[...].astype(F32)
    if before:
        ext_ref[0:PAD, :] = halo
        ext_ref[PAD:PAD + tm, :] = x
        return [ext_ref[PAD - 2 + k:PAD - 2 + k + tm, :] for k in range(3)]
    ext_ref[0:tm, :] = x
    ext_ref[tm:tm + PAD, :] = halo
    return [ext_ref[k:k + tm, :] for k in range(3)]


def _conv(taps, cw, cb):
    return cb + cw[0:1] * taps[0] + cw[1:2] * taps[1] + cw[2:3] * taps[2]


def _ffn_pair_specs(tm, F, T, n_pairs, order, before):
    hb = tm // PAD
    last = T // PAD - 1

    def halo_row(i):
        return jnp.maximum(i * hb - 1, 0) if before else jnp.minimum((i + 1) * hb, last)

    specs = []
    for off in (0, n_pairs):
        specs.append(pl.BlockSpec((None, tm, F), lambda *g, off=off: (order(*g)[1] + off, order(*g)[0], 0)))
        specs.append(pl.BlockSpec((None, PAD, F), lambda *g, off=off: (order(*g)[1] + off, halo_row(order(*g)[0]), 0)))
    return specs


def _ffn_fwd_call(up, cw, cb, wd, h1, tgt, g3, tm, tps, name):
    S, T, F = up.shape
    n_pairs = S // 2
    D = h1.shape[1]

    def body(ua_ref, ha_ref, ub_ref, hb_ref, cwa_ref, cwb_ref, cba_ref, cbb_ref, wd_ref, h1_ref, tgt_ref, g3_ref,
             act_ref, dh2_ref, loss_ref, dg3_ref, acc_ref, ext_ref):
        i, j = pl.program_id(0), pl.program_id(1)
        edge = (i % tps) == 0
        ua = _conv(_conv_taps(ua_ref, ha_ref, ext_ref, edge, tm, True), cwa_ref[...], cba_ref[...])
        ub = _conv(_conv_taps(ub_ref, hb_ref, ext_ref, edge, tm, True), cwb_ref[...], cbb_ref[...])
        act = _silu(ua) * ub
        act_ref[...] = act.astype(ACT)
        contrib = _dot(act, wd_ref[...])

        @pl.when(j == 0)
        def _():
            acc_ref[...] = h1_ref[...] + contrib

        @pl.when(j > 0)
        def _():
            acc_ref[...] += contrib

        @pl.when((i == 0) & (j == 0))
        def _():
            loss_ref[...] = jnp.zeros_like(loss_ref)
            dg3_ref[...] = jnp.zeros_like(dg3_ref)

        @pl.when(j == n_pairs - 1)
        def _():
            row = lax.broadcasted_iota(jnp.int32, (tm, 1), 0) + (i % tps) * tm
            valid = row >= N_META
            tgt = tgt_ref[...]

            def loss_fn(h2, g):
                err = _rms(h2, g) - tgt
                return 0.5 * jnp.sum(jnp.where(valid, err * err, 0.0)) / D

            loss, vjp = jax.vjp(loss_fn, acc_ref[...], g3_ref[...])
            dh2, dg3 = vjp(jnp.ones((), F32))
            dh2_ref[...] = dh2
            loss_ref[...] += loss
            dg3_ref[...] += dg3

    order = lambda i, j: (i, j)
    tile = pl.BlockSpec((tm, D), lambda i, j: (i, 0))
    vec = pl.BlockSpec((1, D), lambda i, j: (0, 0))
    return pl.pallas_call(
        body, name=name,
        out_shape=(jax.ShapeDtypeStruct((n_pairs, T, F), ACT), jax.ShapeDtypeStruct((T, D), F32),
                   jax.ShapeDtypeStruct((1, LANES), F32), jax.ShapeDtypeStruct((1, D), F32)),
        grid=(T // tm, n_pairs),
        in_specs=_ffn_pair_specs(tm, F, T, n_pairs, order, True) + [
            pl.BlockSpec((None, 3, F), lambda i, j: (j, 0, 0)), pl.BlockSpec((None, 3, F), lambda i, j: (j + n_pairs, 0, 0)),
            pl.BlockSpec((None, 1, F), lambda i, j: (j, 0, 0)), pl.BlockSpec((None, 1, F), lambda i, j: (j + n_pairs, 0, 0)),
            pl.BlockSpec((None, F, D), lambda i, j: (j, 0, 0)), tile, tile, vec],
        out_specs=(pl.BlockSpec((None, tm, F), lambda i, j: (j, i, 0)), tile,
                   pl.BlockSpec((1, LANES), lambda i, j: (0, 0)), vec),
        scratch_shapes=[pltpu.VMEM((tm, D), F32), pltpu.VMEM((tm + PAD, F), F32)],
        compiler_params=_cparams())(up, up, up, up, cw, cw, cb, cb, wd, h1, tgt, g3)


def _ffn_bwd_a_call(dh2, up, act, cw, cb, wd, tm, tps, name):
    S, T, F = up.shape
    n_pairs = S // 2
    D = dh2.shape[1]

    def body(dh2_ref, ua_ref, ha_ref, ub_ref, hb_ref, act_ref, cwa_ref, cwb_ref, cba_ref, cbb_ref, wd_ref,
             dua_ref, dub_ref, dwd_ref, dcwa_ref, dcwb_ref, dcba_ref, dcbb_ref, ext_ref):
        i = pl.program_id(1)
        edge = (i % tps) == 0

        @pl.when(i == 0)
        def _():
            for r in (dwd_ref, dcwa_ref, dcwb_ref, dcba_ref, dcbb_ref):
                r[...] = jnp.zeros_like(r)

        dh2 = dh2_ref[...]
        dact = _dot_nt(dh2, wd_ref[...])
        dwd_ref[...] += _dot_tn(act_ref[...], dh2)
        taps_a = _conv_taps(ua_ref, ha_ref, ext_ref, edge, tm, True)
        ua = _conv(taps_a, cwa_ref[...], cba_ref[...])
        sa = jax.nn.sigmoid(ua)
        dub = dact * ua * sa
        dcbb_ref[...] += jnp.sum(dub, axis=0, keepdims=True)
        taps_b = _conv_taps(ub_ref, hb_ref, ext_ref, edge, tm, True)
        dcwb_ref[...] += jnp.concatenate([jnp.sum(dub * t, axis=0, keepdims=True) for t in taps_b], axis=0)
        ub = _conv(taps_b, cwb_ref[...], cbb_ref[...])
        dua = dact * ub * sa * (1.0 + ua * (1.0 - sa))
        dcba_ref[...] += jnp.sum(dua, axis=0, keepdims=True)
        taps_a = _conv_taps(ua_ref, ha_ref, ext_ref, edge, tm, True)
        dcwa_ref[...] += jnp.concatenate([jnp.sum(dua * t, axis=0, keepdims=True) for t in taps_a], axis=0)
        dua_ref[...] = dua.astype(ACT)
        dub_ref[...] = dub.astype(ACT)

    order = lambda j, i: (i, j)
    sh = lambda rows: jax.ShapeDtypeStruct((n_pairs, rows, F), F32)
    par = lambda rows: pl.BlockSpec((None, rows, F), lambda j, i: (j, 0, 0))
    return pl.pallas_call(
        body, name=name,
        out_shape=(jax.ShapeDtypeStruct((n_pairs, T, F), ACT), jax.ShapeDtypeStruct((n_pairs, T, F), ACT),
                   jax.ShapeDtypeStruct((n_pairs, F, D), F32), sh(3), sh(3), sh(1), sh(1)),
        grid=(n_pairs, T // tm),
        in_specs=[pl.BlockSpec((tm, D), lambda j, i: (i, 0))] + _ffn_pair_specs(tm, F, T, n_pairs, order, True) + [
            pl.BlockSpec((None, tm, F), lambda j, i: (j, i, 0)),
            pl.BlockSpec((None, 3, F), lambda j, i: (j, 0, 0)), pl.BlockSpec((None, 3, F), lambda j, i: (j + n_pairs, 0, 0)),
            pl.BlockSpec((None, 1, F), lambda j, i: (j, 0, 0)), pl.BlockSpec((None, 1, F), lambda j, i: (j + n_pairs, 0, 0)),
            pl.BlockSpec((None, F, D), lambda j, i: (j, 0, 0))],
        out_specs=(pl.BlockSpec((None, tm, F), lambda j, i: (j, i, 0)), pl.BlockSpec((None, tm, F), lambda j, i: (j, i, 0)),
                   pl.BlockSpec((None, F, D), lambda j, i: (j, 0, 0)), par(3), par(3), par(1), par(1)),
        scratch_shapes=[pltpu.VMEM((tm + PAD, F), F32)],
        compiler_params=_cparams())(dh2, up, up, up, up, act, cw, cw, cb, cb, wd)


def _ffn_bwd_b_call(dua, dub, cw, wup, h1, g2, dh2, tm, tps, name):
    n_pairs, T, F = dua.shape
    D = h1.shape[1]
    hb = tm // PAD
    last = T // PAD - 1

    def body(da_ref, na_ref, db_ref, nb_ref, cwa_ref, cwb_ref, wa_ref, wb_ref, h1_ref, g2_ref, dh2_ref,
             dupa_ref, dupb_ref, dh1_ref, dg2_ref, acc_ref, ext_ref):
        i, j = pl.program_id(0), pl.program_id(1)
        edge = (i % tps) == tps - 1
        outs = []
        for d_ref, n_ref, cw_ref, o_ref in ((da_ref, na_ref, cwa_ref, dupa_ref), (db_ref, nb_ref, cwb_ref, dupb_ref)):
            t = _conv_taps(d_ref, n_ref, ext_ref, edge, tm, False)
            cwv = cw_ref[...]
            dup = cwv[2:3] * t[0] + cwv[1:2] * t[1] + cwv[0:1] * t[2]
            o_ref[...] = dup.astype(ACT)
            outs.append(dup)
        contrib = _dot_nt(outs[0], wa_ref[...]) + _dot_nt(outs[1], wb_ref[...])

        @pl.when(j == 0)
        def _():
            acc_ref[...] = contrib

        @pl.when(j > 0)
        def _():
            acc_ref[...] += contrib

        @pl.when((i == 0) & (j == 0))
        def _():
            dg2_ref[...] = jnp.zeros_like(dg2_ref)

        @pl.when(j == n_pairs - 1)
        def _():
            _, vjp = jax.vjp(_rms, h1_ref[...], g2_ref[...])
            dh, dg = vjp(acc_ref[...])
            dh1_ref[...] = dh2_ref[...] + dh
            dg2_ref[...] += dg

    tile = pl.BlockSpec((tm, D), lambda i, j: (i, 0))
    vec = pl.BlockSpec((1, D), lambda i, j: (0, 0))
    pair = lambda: [pl.BlockSpec((None, tm, F), lambda i, j: (j, i, 0)),
                    pl.BlockSpec((None, PAD, F), lambda i, j: (j, jnp.minimum((i + 1) * hb, last), 0))]
    act = jax.ShapeDtypeStruct((n_pairs, T, F), ACT)
    return pl.pallas_call(
        body, name=name,
        out_shape=(act, act, jax.ShapeDtypeStruct((T, D), F32), jax.ShapeDtypeStruct((1, D), F32)),
        grid=(T // tm, n_pairs),
        in_specs=pair() + pair() + [
            pl.BlockSpec((None, 3, F), lambda i, j: (j, 0, 0)), pl.BlockSpec((None, 3, F), lambda i, j: (j + n_pairs, 0, 0)),
            pl.BlockSpec((None, D, F), lambda i, j: (j, 0, 0)), pl.BlockSpec((None, D, F), lambda i, j: (j + n_pairs, 0, 0)),
            tile, vec, tile],
        out_specs=(pl.BlockSpec((None, tm, F), lambda i, j: (j, i, 0)), pl.BlockSpec((None, tm, F), lambda i, j: (j, i, 0)),
                   tile, vec),
        scratch_shapes=[pltpu.VMEM((tm, D), F32), pltpu.VMEM((tm + PAD, F), F32)],
        compiler_params=_cparams())(dua, dua, dub, dub, cw, cw, wup, wup, h1, g2, dh2)


def _in_bwd_call(dp, w_in, h0, g1, dh1, tm, name):
    T, D = h0.shape
    S, _, N = w_in.shape

    def body(dp_ref, w_ref, h0_ref, g1_ref, dh1_ref, dh0_ref, dg1_ref, acc_ref):
        i, j = pl.program_id(0), pl.program_id(1)
        contrib = _dot_nt(dp_ref[...], w_ref[...])

        @pl.when(j == 0)
        def _():
            acc_ref[...] = contrib

        @pl.when(j > 0)
        def _():
            acc_ref[...] += contrib

        @pl.when((i == 0) & (j == 0))
        def _():
            dg1_ref[...] = jnp.zeros_like(dg1_ref)

        @pl.when(j == S - 1)
        def _():
            _, vjp = jax.vjp(_rms, h0_ref[...], g1_ref[...])
            dh, dg = vjp(acc_ref[...])
            dh0_ref[...] = dh1_ref[...] + dh
            dg1_ref[...] += dg

    tile = pl.BlockSpec((tm, D), lambda i, j: (i, 0))
    vec = pl.BlockSpec((1, D), lambda i, j: (0, 0))
    return pl.pallas_call(
        body, name=name, out_shape=(jax.ShapeDtypeStruct((T, D), F32), jax.ShapeDtypeStruct((1, D), F32)),
        grid=(T // tm, S),
        in_specs=[pl.BlockSpec((tm, N), lambda i, j: (i, j)), pl.BlockSpec((None, D, N), lambda i, j: (j, 0, 0)),
                  tile, vec, tile],
        out_specs=(tile, vec), scratch_shapes=[pltpu.VMEM((tm, D), F32)],
        compiler_params=_cparams())(dp, w_in, h0, g1, dh1)


def _meta_grad_call(dh0_3, name):
    B, L, D = dh0_3.shape

    def body(d_ref, o_ref):
        o_ref[...] = jnp.sum(d_ref[...], axis=0)

    return pl.pallas_call(
        body, name=name, out_shape=jax.ShapeDtypeStruct((N_META, D), F32), grid=(1,),
        in_specs=[pl.BlockSpec((B, N_META, D), lambda i: (0, 0, 0))],
        out_specs=pl.BlockSpec((N_META, D), lambda i: (0, 0)), compiler_params=_cparams())(dh0_3)


_RELS = [(dx, dy, dc) for dx in (0, 1) for dy in (0, 1) for dc in (0, 1)][1:]


def _exchange_call(arrs, scatter, name):
    n = len(arrs)
    n_rel = len(_RELS)

    def body(*refs):
        ins, outs = refs[:n], refs[n:2 * n]
        send_sems, recv_sems, loc_sems = refs[2 * n:]
        x, y, c = lax.axis_index("x"), lax.axis_index("y"), lax.axis_index("c")
        me = 4 * x + 2 * y + c
        started = []
        for k in range(n):
            src_me = ins[k].at[me] if scatter else ins[k]
            loc = pltpu.make_async_copy(src_me, outs[k].at[me], loc_sems.at[k])
            loc.start()
            started.append(loc)
        waits = []
        for r, (dx, dy, dc) in enumerate(_RELS):
            px, py, pc = (x + dx) % 2, (y + dy) % 2, (c + dc) % 2
            pid = 4 * px + 2 * py + pc
            for k in range(n):
                s = k * n_rel + r
                src = ins[k].at[pid] if scatter else ins[k]
                cp = pltpu.make_async_remote_copy(
                    src_ref=src, dst_ref=outs[k].at[me], send_sem=send_sems.at[s], recv_sem=recv_sems.at[s],
                    device_id=(px, py, pc), device_id_type=pl.DeviceIdType.MESH)
                cp.start()
                waits.append(pltpu.make_async_remote_copy(
                    src_ref=src, dst_ref=outs[k].at[pid], send_sem=send_sems.at[s], recv_sem=recv_sems.at[s],
                    device_id=(px, py, pc), device_id_type=pl.DeviceIdType.MESH))
        for w in waits:
            w.wait_send()
            w.wait_recv()
        for loc in started:
            loc.wait()

    out_shape = tuple(jax.ShapeDtypeStruct(a.shape if scatter else (N_DEV,) + a.shape, a.dtype) for a in arrs)
    hbm = pl.BlockSpec(memory_space=pl.ANY)
    return pl.pallas_call(
        body, name=name, out_shape=out_shape, in_specs=[hbm] * n, out_specs=tuple([hbm] * n),
        scratch_shapes=[pltpu.SemaphoreType.DMA((n * n_rel,)), pltpu.SemaphoreType.DMA((n * n_rel,)),
                        pltpu.SemaphoreType.DMA((n,))],
        compiler_params=pltpu.CompilerParams(has_side_effects=True))(*arrs)


_HBM = pl.BlockSpec(memory_space=pltpu.HBM)
_SEM = pl.BlockSpec(memory_space=pltpu.SEMAPHORE)
_DATAFLOW = pltpu.SideEffectType.DATAFLOW_SIDE_EFFECTING


def _peer_copies(ins, lands, send_sems, recv_sems, scatter):
    n = len(ins)
    x, y, c = lax.axis_index("x"), lax.axis_index("y"), lax.axis_index("c")
    me = 4 * x + 2 * y + c
    sends, arrivals = [], []
    for r, (dx, dy, dc) in enumerate(_RELS):
        px, py, pc = (x + dx) % 2, (y + dy) % 2, (c + dc) % 2
        pid = 4 * px + 2 * py + pc
        for k in range(n):
            s = k * len(_RELS) + r
            src = ins[k].at[pid] if scatter else ins[k]
            for dst, out in ((lands[k].at[me], sends), (lands[k].at[pid], arrivals)):
                out.append(pltpu.make_async_remote_copy(
                    src_ref=src, dst_ref=dst, send_sem=send_sems.at[s], recv_sem=recv_sems.at[s],
                    device_id=(px, py, pc), device_id_type=pl.DeviceIdType.MESH))
    return sends, arrivals


def _exchange_start(arrs, scatter, name):
    n = len(arrs)
    n_sem = n * len(_RELS)

    def body(*refs):
        ins, lands = refs[:n], refs[n:2 * n]
        send_sems, recv_sems = refs[2 * n], refs[2 * n + 1]
        token = refs[-1]
        sends, _ = _peer_copies(ins, lands, send_sems, recv_sems, scatter)
        for cp in sends:
            cp.start()
        token[...] = jnp.zeros_like(token)

    land_shapes = [a.shape if scatter else (N_DEV,) + a.shape for a in arrs]
    ops = [pltpu.with_memory_space_constraint(a, pltpu.HBM) for a in arrs]
    ops += [pltpu.with_memory_space_constraint(lax.empty(s, a.dtype), pltpu.HBM) for s, a in zip(land_shapes, arrs)]
    out = pl.pallas_call(
        body, name=name,
        out_shape=(pltpu.SemaphoreType.DMA((n_sem,)), pltpu.SemaphoreType.DMA((n_sem,)),
                   *[pltpu.HBM(a.shape, a.dtype) for a in arrs],
                   *[pltpu.HBM(s, a.dtype) for s, a in zip(land_shapes, arrs)],
                   jax.ShapeDtypeStruct((SUBLANES, LANES), F32)),
        in_specs=[_HBM] * (2 * n),
        out_specs=(_SEM, _SEM, *[_HBM] * (2 * n), pl.BlockSpec(memory_space=pltpu.VMEM)),
        input_output_aliases={i: 2 + i for i in range(2 * n)},
        compiler_params=pltpu.CompilerParams(has_side_effects=_DATAFLOW))(*ops)
    return out[0], out[1], list(out[2:2 + n]), list(out[2 + n:2 + 2 * n]), out[-1]


def _exchange_wait(started, after, scatter, name):
    send_sems, recv_sems, srcs, lands, _ = started
    n = len(srcs)

    def body(*refs):
        ins, lands_ = refs[:n], refs[n:2 * n]
        _, arrivals = _peer_copies(ins, lands_, refs[2 * n], refs[2 * n + 1], scatter)
        for cp in arrivals:
            cp.wait_send()
            cp.wait_recv()

    out = pl.pallas_call(
        body, name=name,
        out_shape=(*[pltpu.HBM(a.shape, a.dtype) for a in srcs], *[pltpu.HBM(a.shape, a.dtype) for a in lands]),
        in_specs=[_HBM] * (2 * n) + [_SEM, _SEM, pl.BlockSpec(memory_space=pl.ANY)],
        out_specs=tuple([_HBM] * (2 * n)), input_output_aliases={i: i for i in range(2 * n)},
        compiler_params=pltpu.CompilerParams(has_side_effects=_DATAFLOW))(*srcs, *lands, send_sems, recv_sems, after)
    return list(out[:n]), list(out[n:])


def _place_own_call(srcs, lands, scatter, me, name):
    outs = []
    for k, (src, land) in enumerate(zip(srcs, lands)):
        R, C = land.shape[1:]
        tr = R
        while tr % 32 == 0 and tr * C * land.dtype.itemsize > 2 * 1024 * 1024:
            tr //= 2

        def body(me_ref, s_ref, l_ref, o_ref):
            o_ref[...] = s_ref[...]

        src_spec = (pl.BlockSpec((None, tr, C), lambda i, me_ref: (me_ref[0], i, 0)) if scatter
                    else pl.BlockSpec((tr, C), lambda i, me_ref: (i, 0)))
        outs.append(pl.pallas_call(
            body, name=f"{name}_{k}", out_shape=jax.ShapeDtypeStruct(land.shape, land.dtype),
            grid_spec=pltpu.PrefetchScalarGridSpec(
                num_scalar_prefetch=1, grid=(R // tr,),
                in_specs=[src_spec, pl.BlockSpec(memory_space=pl.ANY)],
                out_specs=pl.BlockSpec((None, tr, C), lambda i, me_ref: (me_ref[0], i, 0))),
            input_output_aliases={2: 0}, compiler_params=_cparams())(me, src, land))
    return outs


def _adamw_shard_call(w, parts, m, v, name):
    R, C = w.shape
    tr = _tile(R, 128) if R % 16 == 0 else R

    def body(w_ref, p_ref, m_ref, v_ref, g_ref, d_ref, nm_ref, nv_ref):
        g = p_ref[0].astype(F32)
        for s in range(1, N_DEV):
            g = g + p_ref[s].astype(F32)
        d, nm, nv = _adamw(w_ref[...], g, m_ref[...], v_ref[...])
        g_ref[...] = g
        d_ref[...] = d
        nm_ref[...] = nm
        nv_ref[...] = nv

    tile = pl.BlockSpec((tr, C), lambda i: (i, 0))
    sh = jax.ShapeDtypeStruct((R, C), F32)
    return pl.pallas_call(
        body, name=name, out_shape=(sh, sh, sh, sh), grid=(R // tr,),
        in_specs=[tile, pl.BlockSpec((N_DEV, tr, C), lambda i: (0, i, 0)), tile, tile],
        out_specs=(tile, tile, tile, tile), compiler_params=_cparams())(w, parts, m, v)


def _pack(arrs, rows_mult=SUBLANES):
    flat = jnp.concatenate([a.reshape(-1).astype(F32) for a in arrs])
    n = flat.shape[0]
    per = rows_mult * LANES
    total = -(-n // per) * per
    return jnp.pad(flat, (0, total - n)).reshape(total // LANES, LANES)


def _unpack(pack, shapes):
    flat = pack.reshape(-1)
    out, off = [], 0
    for s in shapes:
        n = 1
        for d in s:
            n *= d
        out.append(flat[off:off + n].reshape(s))
        off += n
    return out


def kernel(x, meta_tokens, mix_norm_g, w_in, ssm_lambda_re, ssm_lambda_im, ssm_log_dt, ssm_b_re, ssm_b_im, ssm_c_re, ssm_c_im, ssm_d, ssm_w_glu, w_ssm_proj, hgrn_lb_logits, hgrn_norm_g, w_hgrn_proj, w_out, ffn_norm_g, w_up, conv_w, conv_b, w_down, final_norm_g, loss_target, m_meta_tokens, m_mix_norm_g, m_w_in, m_ssm_lambda_re, m_ssm_lambda_im, m_ssm_log_dt, m_ssm_b_re, m_ssm_b_im, m_ssm_c_re, m_ssm_c_im, m_ssm_d, m_ssm_w_glu, m_w_ssm_proj, m_hgrn_lb_logits, m_hgrn_norm_g, m_w_hgrn_proj, m_w_out, m_ffn_norm_g, m_w_up, m_conv_w, m_conv_b, m_w_down, m_final_norm_g, v_meta_tokens, v_mix_norm_g, v_w_in, v_ssm_lambda_re, v_ssm_lambda_im, v_ssm_log_dt, v_ssm_b_re, v_ssm_b_im, v_ssm_c_re, v_ssm_c_im, v_ssm_d, v_ssm_w_glu, v_w_ssm_proj, v_hgrn_lb_logits, v_hgrn_norm_g, v_w_hgrn_proj, v_w_out, v_ffn_norm_g, v_w_up, v_conv_w, v_conv_b, v_w_down, v_final_norm_g):
    args = dict(locals())
    B, S_len, D = x.shape
    L = S_len + N_META
    T = B * L
    tm = _tile(L, ROW_TILE_CAP)
    tps = L // tm
    G, P = ssm_lambda_re.shape[1:]
    H = ssm_b_re.shape[-1]
    W = G * H
    n_cb = W // LANES
    gpb = G // n_cb
    hd = hgrn_norm_g.shape[1]
    n_heads = D // hd
    n_in = w_in.shape[2]
    F = w_up.shape[2]
    assert W == D and n_in % LANES == 0

    me = (4 * lax.axis_index("x") + 2 * lax.axis_index("y") + lax.axis_index("c")).astype(jnp.int32).reshape(1)
    meta_g, cw_g = _exchange_call([meta_tokens, conv_w[0]], False, "gather_small_params")
    ga = _exchange_start([w_in[0].astype(MXU)], False, "gather_a_start")
    gb = _exchange_start(
        [w_up[0].astype(MXU), ssm_w_glu[0].astype(MXU), w_ssm_proj[0].astype(MXU), w_hgrn_proj[0].astype(MXU),
         w_out[0].astype(MXU), w_down[0].astype(MXU)], False, "gather_b_start")
    started_tok = (ga[4] + gb[4])[0:1, 0:1]
    meta_full = meta_g.transpose(1, 0, 2).reshape(N_META, D)
    cb_g = conv_b.reshape(N_DEV, 1, F)

    h0 = jnp.concatenate([jnp.broadcast_to(meta_full[None], (B, N_META, D)), x], axis=1).reshape(T, D)
    tgt = jnp.concatenate([jnp.zeros((B, N_META, D), F32), loss_target], axis=1).reshape(T, D)

    lr, li = ssm_lambda_re[0], ssm_lambda_im[0]
    ldt = ssm_log_dt[0].reshape(G, 1)
    bt_re = ssm_b_re[0].transpose(2, 0, 1).reshape(H, G * P)
    bt_im = ssm_b_im[0].transpose(2, 0, 1).reshape(H, G * P)
    disc = _small_call(_disc_a_powers, [lr, li, ldt], [((G, P), F32)] * (2 * SUBLANES + 2), "s5_discretise")
    pw_re, pw_im = jnp.stack(disc[:SUBLANES]), jnp.stack(disc[SUBLANES:2 * SUBLANES])
    coef_re, coef_im = disc[2 * SUBLANES:]
    bbt_re, bbt_im = _small_call(
        _disc_b, [coef_re.reshape(1, G * P), coef_im.reshape(1, G * P), bt_re, bt_im],
        [((H, G * P), F32)] * 2, "s5_input_matrix")
    eye = jnp.eye(gpb, dtype=F32)
    hw = gpb * P

    def expand_b(bbt):
        return jnp.einsum("hcgp,Gg->cGhgp", bbt.reshape(H, n_cb, gpb, P), eye).reshape(n_cb, gpb * H, hw)

    def expand_c(cm):
        return jnp.einsum("cghp,gG->cgpGh", cm.reshape(n_cb, gpb, H, P), eye).reshape(n_cb, hw, gpb * H)

    wb = jnp.concatenate([expand_b(bbt_re), expand_b(bbt_im)], axis=2).astype(MXU)
    wc = jnp.concatenate([expand_c(ssm_c_re[0]), -expand_c(ssm_c_im[0])], axis=1).astype(MXU)
    pwr = pw_re.reshape(SUBLANES, n_cb, hw).transpose(1, 0, 2)
    pwi = pw_im.reshape(SUBLANES, n_cb, hw).transpose(1, 0, 2)
    rows = jnp.arange(SUBLANES)[None, :, None]

    def table(sign, reverse):
        tabs = []
        for d in (1, 2, 4):
            keep = (rows + d < SUBLANES) if reverse else (rows >= d)
            tabs.append(jnp.concatenate([jnp.where(keep, pwr[:, d - 1:d], 0.0),
                                         jnp.where(keep, sign * pwi[:, d - 1:d], 0.0)], axis=2))
        cr, ci = (pwr[:, ::-1], pwi[:, ::-1]) if reverse else (pwr, pwi)
        tabs.append(jnp.concatenate([cr, sign * ci], axis=2))
        return jnp.stack(tabs, axis=1)

    tab_f, tab_r = table(1.0, False), table(-1.0, True)
    dsk = ssm_d.reshape(n_cb, 1, LANES)
    lb = _small_call(_lb_fn, [hgrn_lb_logits], [((1, D), F32)], "hgrn_lower_bound")[0]

    z1 = _norm_call(h0, mix_norm_g + started_tok, tm, "mix_norm")
    ga_src, ga_land = _exchange_wait(ga, z1, False, "gather_a_wait")
    win_g = _place_own_call(ga_src, ga_land, False, me, "gather_a_own")[0]
    p = _mm_shard(z1, win_g, tm, "in_proj", False)
    p3 = p.reshape(B, L, p.shape[1])
    ya = _s5_fwd_call(p3, wb, wc, tab_f, dsk, "s5_fwd").reshape(T, W)
    gb_src, gb_land = _exchange_wait(gb, ya, False, "gather_b_wait")
    gathered = _place_own_call(gb_src, gb_land, False, me, "gather_b_own")
    wup_g = gathered[0]
    wglu_g, wsp_g, whp_g, wout_g = [g.reshape(D, D) for g in gathered[1:5]]
    wdn_g = gathered[5].reshape(N_DEV // 2, 2 * w_down.shape[1], D)
    yo, a_br = _glu_proj_call(ya, wglu_g, wsp_g, tm, "s5_glu_proj")
    yb = _hgrn_fwd_call(p3, lb, hgrn_norm_g, n_heads, n_cb, "hgrn_fwd").reshape(T, D)
    col_ga = 5
    h1, mg, bm, z2 = _merge_call(yb, a_br, p, h0, whp_g, wout_g, ffn_norm_g, col_ga, tm, "merge")
    up = _mm_shard(z2, wup_g, tm, "up_proj", True)
    act, dh2, loss_part, dg3 = _ffn_fwd_call(up, cw_g, cb_g, wdn_g, h1, tgt, final_norm_g.reshape(1, D),
                                             tm, tps, "ffn_out_loss")

    dua, dub, dwd, dcwa, dcwb, dcba, dcbb = _ffn_bwd_a_call(dh2, up, act, cw_g, cb_g, wdn_g, tm, tps, "ffn_bwd_gate")
    dupa, dupb, dh1, dg2 = _ffn_bwd_b_call(dua, dub, cw_g, wup_g, h1, ffn_norm_g, dh2, tm, tps, "ffn_bwd_up")
    dwup = jnp.concatenate([_mm_tn(z2, dupa, N_DEV // 2, tm, "dw_up_a", True),
                            _mm_tn(z2, dupb, N_DEV // 2, tm, "dw_up_b", True)], axis=0)
    sh_rows = D // N_DEV
    sa = _exchange_start([dwup.astype(WIRE), dwd.reshape(N_DEV, w_down.shape[1], D).astype(WIRE)], True,
                         "scatter_a_start")
    dmg, dwout = _lin_bwd(mg, dh1, wout_g + sa[4][0:1, 0:1].astype(MXU), tm, "out_proj_bwd")
    da_br, dbm, dga, dgb = _merge_bwd_call(dmg, a_br, bm, p, col_ga, tm, "merge_bwd")
    dyo, dwsp = _lin_bwd(yo, da_br, wsp_g, tm, "ssm_proj_bwd")
    dyb, dwhp = _lin_bwd(yb, dbm, whp_g, tm, "hgrn_proj_bwd")
    dya, dwglu = _glu_bwd_call(ya, dyo, wglu_g, tm, "s5_glu_bwd")
    sb = _exchange_start([t.reshape(N_DEV, sh_rows, D).astype(WIRE) for t in (dwglu, dwsp, dwhp, dwout)], True,
                         "scatter_b_start")
    tok_b = sb[4][0:1, :]
    du, dwb, dwc, dab, ddsk = _s5_bwd_call(p3, dya.reshape(B, L, W), wb, wc, tab_f, tab_r, dsk + tok_b[None],
                                           "s5_bwd")

    def diag_b(dw):
        return jnp.einsum("cGhgp,Gg->hcgp", dw.reshape(n_cb, gpb, H, gpb, P), eye).reshape(H, G * P)

    def diag_c(dw):
        return jnp.einsum("cgpGh,gG->cghp", dw.reshape(n_cb, gpb, P, gpb, H), eye).reshape(G, H, P)

    early_parts = [dab[:, 0, :hw].reshape(G, P), dab[:, 0, hw:].reshape(G, P),
                   diag_b(dwb[:, :, :hw]), diag_b(dwb[:, :, hw:]),
                   diag_c(dwc[:, :hw]), -diag_c(dwc[:, hw:]), ddsk.reshape(1, D)]
    early_pack = _pack(early_parts)
    se = _exchange_start([early_pack], False, "gather_s5_grads_start")
    dq, dfl, di, dog, dlb, dng = _hgrn_bwd_call(p3, dyb.reshape(B, L, D), lb, hgrn_norm_g + tok_b + se[4][0:1, :],
                                                n_heads, n_cb, "hgrn_bwd")
    dp = jnp.concatenate([du.reshape(T, W), dq.reshape(T, D), dfl.reshape(T, D), di.reshape(T, D),
                          dog.reshape(T, D), dga, dgb], axis=1)
    dwin = _mm_tn(z1, dp, N_DEV, tm, "dw_in", False)
    sc = _exchange_start([dwin.astype(WIRE)], True, "scatter_c_start")
    dh0, dg1 = _in_bwd_call(dp, win_g, h0, mix_norm_g + sc[4][0:1, 0:1], dh1, tm, "in_proj_bwd")
    dh0_3 = dh0.reshape(B, L, D)
    grad_x = dh0_3[:, N_META:]
    dmeta = _meta_grad_call(dh0_3, "meta_grad")

    late_parts = [dg1, dlb, dng, dg2, jnp.concatenate([dcba, dcbb], axis=0).reshape(1, N_DEV * F), dg3, loss_part]
    late_pack = _pack(late_parts)

    dcw = jnp.concatenate([dcwa, dcwb], axis=0)
    dmeta_s = dmeta.reshape(N_META, N_DEV, D // N_DEV).transpose(1, 0, 2)
    parts_d = _exchange_call([dmeta_s, dcw], True, "scatter_small_grads")
    late_all = _exchange_call([late_pack], False, "gather_small_grads")[0]
    early_all = _place_own_call(*_exchange_wait(se, late_all, False, "gather_s5_grads_wait"), False, me,
                                "gather_s5_grads_own")[0]
    parts_a = _place_own_call(*_exchange_wait(sa, late_all, True, "scatter_a_wait"), True, me, "scatter_a_own")
    parts_b = _place_own_call(*_exchange_wait(sb, late_all, True, "scatter_b_wait"), True, me, "scatter_b_own")
    parts_c = _place_own_call(*_exchange_wait(sc, late_all, True, "scatter_c_wait"), True, me, "scatter_c_own")
    parts = [parts_c[0], parts_a[0], *parts_b, parts_a[1], parts_d[0], parts_d[1]]

    def sum8(a, b):
        ta, tb = a[0], b[0]
        for s in range(1, N_DEV):
            ta, tb = ta + a[s], tb + b[s]
        return ta, tb

    early_sum, late_sum = _small_call(sum8, [early_all, late_all], [(early_pack.shape, F32), (late_pack.shape, F32)],
                                      "sum_small_grads")
    t_abr, t_abi, t_bbr, t_bbi, g_cre, g_cim, g_dsk = _unpack(early_sum, [a.shape for a in early_parts])
    g_g1, t_lb, g_ng, g_g2, g_cb, g_g3, loss_v = _unpack(late_sum, [a.shape for a in late_parts])

    def disc_b_bwd(cr, ci, br, bi, dbr, dbi):
        _, vjp = jax.vjp(_disc_b, cr, ci, br, bi)
        return vjp((dbr, dbi))

    t_cr, t_ci, g_btr, g_bti = _small_call(
        disc_b_bwd, [coef_re.reshape(1, G * P), coef_im.reshape(1, G * P), bt_re, bt_im, t_bbr, t_bbi],
        [((1, G * P), F32)] * 2 + [((H, G * P), F32)] * 2, "s5_input_matrix_bwd")

    def disc_a_bwd(lr_, li_, ldt_, dar, dai, dcr, dci):
        _, vjp = jax.vjp(_disc_a, lr_, li_, ldt_)
        return vjp((dar, dai, dcr, dci))

    g_lr, g_li, g_ldt = _small_call(
        disc_a_bwd, [lr, li, ldt, t_abr, t_abi, t_cr.reshape(G, P), t_ci.reshape(G, P)],
        [((G, P), F32)] * 2 + [((G, 1), F32)], "s5_discretise_bwd")

    def lb_bwd(logits, d):
        _, vjp = jax.vjp(_lb_fn, logits)
        return vjp(d)

    g_lbl = _small_call(lb_bwd, [hgrn_lb_logits, t_lb], [(hgrn_lb_logits.shape, F32)], "hgrn_lower_bound_bwd")[0]

    grads = dict(
        mix_norm_g=g_g1, ssm_lambda_re=g_lr[None], ssm_lambda_im=g_li[None], ssm_log_dt=g_ldt.reshape(1, G),
        ssm_b_re=g_btr.reshape(H, G, P).transpose(1, 2, 0)[None], ssm_b_im=g_bti.reshape(H, G, P).transpose(1, 2, 0)[None],
        ssm_c_re=g_cre[None], ssm_c_im=g_cim[None], ssm_d=g_dsk, hgrn_lb_logits=g_lbl, hgrn_norm_g=g_ng,
        ffn_norm_g=g_g2, conv_b=g_cb.reshape(1, N_DEV * F), final_norm_g=g_g3.reshape(D))
    loss = loss_v[0, 0]

    delta, new_m, new_v = {}, {}, {}
    sharded = [("w_in", parts[0], (D, n_in)), ("w_up", parts[1], (D, F)), ("ssm_w_glu", parts[2], (sh_rows, D)),
               ("w_ssm_proj", parts[3], (sh_rows, D)), ("w_hgrn_proj", parts[4], (sh_rows, D)),
               ("w_out", parts[5], (sh_rows, D)), ("w_down", parts[6], (w_down.shape[1], D)),
               ("meta_tokens", parts[7], (N_META, D // N_DEV)), ("conv_w", parts[8], (3, F))]
    for name, part, shp in sharded:
        full = args[name].shape
        g, d_, nm, nv = _adamw_shard_call(args[name].reshape(shp), part, args["m_" + name].reshape(shp),
                                          args["v_" + name].reshape(shp), "adamw_" + name)
        grads[name], delta[name], new_m[name], new_v[name] = [t.reshape(full) for t in (g, d_, nm, nv)]

    rep = ["mix_norm_g", "ssm_lambda_re", "ssm_lambda_im", "ssm_log_dt", "ssm_b_re", "ssm_b_im", "ssm_c_re",
           "ssm_c_im", "ssm_d", "hgrn_lb_logits", "hgrn_norm_g", "ffn_norm_g", "conv_b", "final_norm_g"]
    rep_shapes = [args[n].shape for n in rep]
    packs = [_pack([args[pre + n] for n in rep]) for pre in ("", "m_", "v_")]
    g_pack = _pack([grads[n] for n in rep])
    outs = _small_call(lambda w, g, m, v: _adamw(w, g, m, v), [packs[0], g_pack, packs[1], packs[2]],
                       [(g_pack.shape, F32)] * 3, "adamw_replicated")
    for n, d_, nm, nv in zip(rep, *[_unpack(o, rep_shapes) for o in outs]):
        delta[n], new_m[n], new_v[n] = d_, nm, nv

    names = ["meta_tokens", "mix_norm_g", "w_in", "ssm_lambda_re", "ssm_lambda_im", "ssm_log_dt", "ssm_b_re",
             "ssm_b_im", "ssm_c_re", "ssm_c_im", "ssm_d", "ssm_w_glu", "w_ssm_proj", "hgrn_lb_logits", "hgrn_norm_g",
             "w_hgrn_proj", "w_out", "ffn_norm_g", "w_up", "conv_w", "conv_b", "w_down", "final_norm_g"]
    return (loss, grad_x, *[grads[n] for n in names], *[delta[n] for n in names],
            *[new_m[n] for n in names], *[new_v[n] for n in names])
```

```python
import functools

import jax
import jax.numpy as jnp
from jax import lax
from jax.experimental import pallas as pl
from jax.experimental.pallas import tpu as pltpu

F32 = jnp.float32
MXU = jnp.bfloat16
ACT = jnp.bfloat16
WIRE = jnp.bfloat16
N_DEV = 8
N_META = 16
CHUNK = 16
EPS = 1e-6
ADAM_LR, ADAM_B1, ADAM_B2, ADAM_EPS, ADAM_WD, ADAM_STEP = 0.001, 0.9, 0.999, 1e-08, 0.01, 10
SUBLANES = 8
LANES = 128
ROW_TILE_CAP = 700
VMEM_LIMIT = 60 * 1024 * 1024


def _cparams(**kw):
    return pltpu.CompilerParams(vmem_limit_bytes=VMEM_LIMIT, **kw)


def _tile(n, cap):
    best = None
    for t in range(16, min(n, cap) + 1, 16):
        if n % t == 0:
            best = t
    assert best is not None, (n, cap)
    return best


def _dot(a, b):
    return lax.dot_general(a.astype(MXU), b.astype(MXU), (((1,), (0,)), ((), ())), preferred_element_type=F32)


def _dot_nt(a, b):
    return lax.dot_general(a.astype(MXU), b.astype(MXU), (((1,), (1,)), ((), ())), preferred_element_type=F32)


def _dot_tn(a, b):
    return lax.dot_general(a.astype(MXU), b.astype(MXU), (((0,), (0,)), ((), ())), preferred_element_type=F32)


def _rms(x, g):
    return x * lax.rsqrt(jnp.mean(x * x, axis=-1, keepdims=True) + EPS) * g


def _silu(x):
    return x * jax.nn.sigmoid(x)


def _small_call(fn, ins, out_shapes, name):
    n_in = len(ins)

    def body(*refs):
        outs = fn(*[r[...] for r in refs[:n_in]])
        outs = outs if isinstance(outs, (tuple, list)) else (outs,)
        for r, o in zip(refs[n_in:], outs):
            r[...] = o.astype(r.dtype)

    vm = pl.BlockSpec(memory_space=pltpu.VMEM)
    return pl.pallas_call(
        body, name=name, out_shape=tuple(jax.ShapeDtypeStruct(s, d) for s, d in out_shapes),
        in_specs=[vm] * n_in, out_specs=tuple([vm] * len(out_shapes)), compiler_params=_cparams())(*ins)


def _disc_a(lr, li, ldt):
    dt = jnp.exp(ldt)
    mag = jnp.exp(lr * dt)
    ab_re = mag * jnp.cos(li * dt)
    ab_im = mag * jnp.sin(li * dt)
    den = lr * lr + li * li
    nr = ab_re - 1.0
    coef_re = (nr * lr + ab_im * li) / den
    coef_im = (ab_im * lr - nr * li) / den
    return ab_re, ab_im, coef_re, coef_im


def _disc_a_powers(lr, li, ldt):
    ab_re, ab_im, coef_re, coef_im = _disc_a(lr, li, ldt)
    pr, pi = [ab_re], [ab_im]
    for _ in range(SUBLANES - 1):
        pr, pi = pr + [pr[-1] * ab_re - pi[-1] * ab_im], pi + [pr[-1] * ab_im + pi[-1] * ab_re]
    return (*pr, *pi, coef_re, coef_im)


def _disc_b(coef_re, coef_im, bt_re, bt_im):
    return coef_re * bt_re - coef_im * bt_im, coef_re * bt_im + coef_im * bt_re


def _lb_fn(logits):
    return jax.nn.softmax(logits, axis=0)[0:1]


def _adamw(w, g, m, v):
    m = ADAM_B1 * m + (1.0 - ADAM_B1) * g
    v = ADAM_B2 * v + (1.0 - ADAM_B2) * jnp.square(g)
    m_hat = m / (1.0 - ADAM_B1 ** ADAM_STEP)
    v_hat = v / (1.0 - ADAM_B2 ** ADAM_STEP)
    delta = -ADAM_LR * (m_hat / (jnp.sqrt(v_hat) + ADAM_EPS) + ADAM_WD * w)
    return delta, m, v


def _norm_call(h, g, tm, name):
    T, D = h.shape

    def body(h_ref, g_ref, z_ref):
        z_ref[...] = _rms(h_ref[...], g_ref[...]).astype(ACT)

    return pl.pallas_call(
        body, name=name, out_shape=jax.ShapeDtypeStruct((T, D), ACT), grid=(T // tm,),
        in_specs=[pl.BlockSpec((tm, D), lambda i: (i, 0)), pl.BlockSpec((1, D), lambda i: (0, 0))],
        out_specs=pl.BlockSpec((tm, D), lambda i: (i, 0)), compiler_params=_cparams())(h, g)


def _mm_shard(x, w, tm, name, major):
    T, K = x.shape
    S, _, N = w.shape

    def body(x_ref, w_ref, o_ref):
        o_ref[...] = _dot(x_ref[...], w_ref[...]).astype(o_ref.dtype)

    if major:
        out_shape = jax.ShapeDtypeStruct((S, T, N), ACT)
        out_spec = pl.BlockSpec((None, tm, N), lambda j, i: (j, i, 0))
    else:
        out_shape = jax.ShapeDtypeStruct((T, S * N), ACT)
        out_spec = pl.BlockSpec((tm, N), lambda j, i: (i, j))
    return pl.pallas_call(
        body, name=name, out_shape=out_shape, grid=(S, T // tm),
        in_specs=[pl.BlockSpec((tm, K), lambda j, i: (i, 0)), pl.BlockSpec((None, K, N), lambda j, i: (j, 0, 0))],
        out_specs=out_spec, compiler_params=_cparams())(x, w)


def _mm_tn(x, y, n_shards, tm, name, major):
    T, K = x.shape
    S = n_shards
    N = y.shape[-1] if major else y.shape[-1] // S

    def body(x_ref, y_ref, o_ref):
        @pl.when(pl.program_id(1) == 0)
        def _():
            o_ref[...] = jnp.zeros_like(o_ref)
        o_ref[...] += _dot_tn(x_ref[...], y_ref[...])

    y_spec = (pl.BlockSpec((None, tm, N), lambda j, i: (j, i, 0)) if major
              else pl.BlockSpec((tm, N), lambda j, i: (i, j)))
    return pl.pallas_call(
        body, name=name, out_shape=jax.ShapeDtypeStruct((S, K, N), F32), grid=(S, T // tm),
        in_specs=[pl.BlockSpec((tm, K), lambda j, i: (i, 0)), y_spec],
        out_specs=pl.BlockSpec((None, K, N), lambda j, i: (j, 0, 0)), compiler_params=_cparams())(x, y)


def _lin_bwd(x, dy, w, tm, name):
    T, K = x.shape
    N = dy.shape[1]

    def body(x_ref, dy_ref, w_ref, dx_ref, dw_ref):
        @pl.when(pl.program_id(0) == 0)
        def _():
            dw_ref[...] = jnp.zeros_like(dw_ref)
        dy = dy_ref[...]
        dx_ref[...] = _dot_nt(dy, w_ref[...]).astype(dx_ref.dtype)
        dw_ref[...] += _dot_tn(x_ref[...], dy)

    return pl.pallas_call(
        body, name=name,
        out_shape=(jax.ShapeDtypeStruct((T, K), ACT), jax.ShapeDtypeStruct((K, N), F32)), grid=(T // tm,),
        in_specs=[pl.BlockSpec((tm, K), lambda i: (i, 0)), pl.BlockSpec((tm, N), lambda i: (i, 0)),
                  pl.BlockSpec((K, N), lambda i: (0, 0))],
        out_specs=(pl.BlockSpec((tm, K), lambda i: (i, 0)), pl.BlockSpec((K, N), lambda i: (0, 0))),
        compiler_params=_cparams())(x, dy, w)


def _scan_slabs(x_ref, tab_ref, n_slabs, reverse):
    hw = x_ref.shape[1] // 2
    tabs = [tab_ref[s] for s in range(4)]

    def cmul(t, xr, xi):
        tr, ti = t[:, :hw], t[:, hw:]
        return tr * xr - ti * xi, tr * xi + ti * xr

    def step(k, carry):
        cr, ci = carry
        kk = (n_slabs - 1 - k) if reverse else k
        r0 = pl.multiple_of(kk * SUBLANES, SUBLANES)
        x = x_ref[pl.ds(r0, SUBLANES), :]
        xr, xi = x[:, :hw], x[:, hw:]
        for s, d in enumerate((1, 2, 4)):
            sh = (SUBLANES - d) if reverse else d
            ar, ai = cmul(tabs[s], pltpu.roll(xr, sh, 0), pltpu.roll(xi, sh, 0))
            xr, xi = xr + ar, xi + ai
        pr, pi = cmul(tabs[3], cr, ci)
        xr, xi = xr + pr, xi + pi
        x_ref[pl.ds(r0, SUBLANES), 0:hw] = xr
        x_ref[pl.ds(r0, SUBLANES), hw:2 * hw] = xi
        e = 0 if reverse else SUBLANES - 1
        return xr[e:e + 1], xi[e:e + 1]

    z = jnp.zeros((1, hw), F32)
    lax.fori_loop(0, n_slabs, step, (z, z))


def _s5_fwd_call(p3, wb, wc, tab_f, dsk, name):
    B, L, _ = p3.shape
    n_cb, cw, sw = wb.shape

    def body(u_ref, wb_ref, wc_ref, tab_ref, d_ref, ya_ref, so_ref, s_ref):
        u = u_ref[...]
        s_ref[...] = _dot(u, wb_ref[...])
        _scan_slabs(s_ref, tab_ref, L // SUBLANES, False)
        s = s_ref[...].astype(MXU)
        so_ref[...] = s
        y = _dot(s, wc_ref[...]) + d_ref[...] * u.astype(F32)
        ya_ref[...] = jax.nn.gelu(y).astype(ACT)

    return pl.pallas_call(
        body, name=name,
        out_shape=(jax.ShapeDtypeStruct((B, L, n_cb * cw), ACT), jax.ShapeDtypeStruct((B, n_cb, L, sw), MXU)),
        grid=(B, n_cb),
        in_specs=[pl.BlockSpec((None, L, cw), lambda b, c: (b, 0, c)),
                  pl.BlockSpec((None, cw, sw), lambda b, c: (c, 0, 0)),
                  pl.BlockSpec((None, sw, cw), lambda b, c: (c, 0, 0)),
                  pl.BlockSpec((None, 4, SUBLANES, sw), lambda b, c: (c, 0, 0, 0)),
                  pl.BlockSpec((None, 1, cw), lambda b, c: (c, 0, 0))],
        out_specs=(pl.BlockSpec((None, L, cw), lambda b, c: (b, 0, c)),
                   pl.BlockSpec((None, None, L, sw), lambda b, c: (b, c, 0, 0))),
        scratch_shapes=[pltpu.VMEM((L, sw), F32)], compiler_params=_cparams())(p3, wb, wc, tab_f, dsk)


def _s5_bwd_call(p3, s_all, dya, wb, wc, tab_r, dsk, name):
    B, L, _ = p3.shape
    n_cb, cw, sw = wb.shape
    hw = sw // 2
    n_slabs = L // SUBLANES

    def body(u_ref, si_ref, dya_ref, wb_ref, wc_ref, tr_ref, d_ref,
             du_ref, dwb_ref, dwc_ref, da_ref, dd_ref, s_ref, l_ref):
        @pl.when(pl.program_id(1) == 0)
        def _():
            dwb_ref[...] = jnp.zeros_like(dwb_ref)
            dwc_ref[...] = jnp.zeros_like(dwc_ref)
            da_ref[...] = jnp.zeros_like(da_ref)
            dd_ref[...] = jnp.zeros_like(dd_ref)

        u = u_ref[...]
        uf = u.astype(F32)
        s_in = si_ref[...]
        s_ref[...] = s_in.astype(F32)
        y = _dot(s_in, wc_ref[...]) + d_ref[...] * uf
        _, gelu_vjp = jax.vjp(jax.nn.gelu, y)
        dy = gelu_vjp(dya_ref[...].astype(F32))[0]
        dd_ref[...] += jnp.sum(dy * uf, axis=0, keepdims=True)
        l_ref[...] = _dot_nt(dy, wc_ref[...])
        _scan_slabs(l_ref, tr_ref, n_slabs, True)
        du_ref[...] = (_dot_nt(l_ref[...], wb_ref[...]) + d_ref[...] * dy).astype(ACT)
        dwb_ref[...] += _dot_tn(u, l_ref[...])
        dwc_ref[...] += _dot_tn(s_in, dy)

        row = lax.broadcasted_iota(jnp.int32, (SUBLANES, hw), 0)

        def step(k, carry):
            pr, pi, accr, acci = carry
            r0 = pl.multiple_of(k * SUBLANES, SUBLANES)
            s = s_ref[pl.ds(r0, SUBLANES), :]
            lam = l_ref[pl.ds(r0, SUBLANES), :]
            sr, si = s[:, :hw], s[:, hw:]
            lr, li = lam[:, :hw], lam[:, hw:]
            qr = jnp.where(row == 0, pr, pltpu.roll(sr, 1, 0))
            qi = jnp.where(row == 0, pi, pltpu.roll(si, 1, 0))
            accr = accr + lr * qr + li * qi
            acci = acci + li * qr - lr * qi
            return sr[SUBLANES - 1:], si[SUBLANES - 1:], accr, acci

        z1 = jnp.zeros((1, hw), F32)
        z8 = jnp.zeros((SUBLANES, hw), F32)
        _, _, accr, acci = lax.fori_loop(0, n_slabs, step, (z1, z1, z8, z8))
        da_ref[...] += jnp.concatenate([jnp.sum(accr, axis=0, keepdims=True),
                                        jnp.sum(acci, axis=0, keepdims=True)], axis=1)

    W = n_cb * cw
    return pl.pallas_call(
        body, name=name,
        out_shape=(jax.ShapeDtypeStruct((B, L, W), ACT), jax.ShapeDtypeStruct((n_cb, cw, sw), F32),
                   jax.ShapeDtypeStruct((n_cb, sw, cw), F32), jax.ShapeDtypeStruct((n_cb, 1, sw), F32),
                   jax.ShapeDtypeStruct((n_cb, 1, cw), F32)),
        grid=(n_cb, B),
        in_specs=[pl.BlockSpec((None, L, cw), lambda c, b: (b, 0, c)),
                  pl.BlockSpec((None, None, L, sw), lambda c, b: (b, c, 0, 0)),
                  pl.BlockSpec((None, L, cw), lambda c, b: (b, 0, c)),
                  pl.BlockSpec((None, cw, sw), lambda c, b: (c, 0, 0)),
                  pl.BlockSpec((None, sw, cw), lambda c, b: (c, 0, 0)),
                  pl.BlockSpec((None, 4, SUBLANES, sw), lambda c, b: (c, 0, 0, 0)),
                  pl.BlockSpec((None, 1, cw), lambda c, b: (c, 0, 0))],
        out_specs=(pl.BlockSpec((None, L, cw), lambda c, b: (b, 0, c)),
                   pl.BlockSpec((None, cw, sw), lambda c, b: (c, 0, 0)),
                   pl.BlockSpec((None, sw, cw), lambda c, b: (c, 0, 0)),
                   pl.BlockSpec((None, 1, sw), lambda c, b: (c, 0, 0)),
                   pl.BlockSpec((None, 1, cw), lambda c, b: (c, 0, 0))),
        scratch_shapes=[pltpu.VMEM((L, sw), F32), pltpu.VMEM((L, sw), F32)],
        compiler_params=_cparams())(p3, s_all, dya, wb, wc, tab_r, dsk)


def _glu_proj_call(ya, wglu, wproj, tm, name):
    T, W = ya.shape
    D = wproj.shape[1]

    def body(ya_ref, wg_ref, wp_ref, yo_ref, a_ref):
        ya = ya_ref[...]
        yo = ya.astype(F32) * jax.nn.sigmoid(_dot(ya, wg_ref[...]))
        yo_ref[...] = yo.astype(ACT)
        a_ref[...] = _dot(yo, wp_ref[...]).astype(ACT)

    return pl.pallas_call(
        body, name=name, out_shape=(jax.ShapeDtypeStruct((T, W), ACT), jax.ShapeDtypeStruct((T, D), ACT)),
        grid=(T // tm,),
        in_specs=[pl.BlockSpec((tm, W), lambda i: (i, 0)), pl.BlockSpec((W, W), lambda i: (0, 0)),
                  pl.BlockSpec((W, D), lambda i: (0, 0))],
        out_specs=(pl.BlockSpec((tm, W), lambda i: (i, 0)), pl.BlockSpec((tm, D), lambda i: (i, 0))),
        compiler_params=_cparams())(ya, wglu, wproj)


def _glu_bwd_call(ya, dyo, wglu, tm, name):
    T, W = ya.shape

    def body(ya_ref, dyo_ref, wg_ref, dya_ref, dwg_ref):
        @pl.when(pl.program_id(0) == 0)
        def _():
            dwg_ref[...] = jnp.zeros_like(dwg_ref)
        ya = ya_ref[...]
        yaf = ya.astype(F32)
        dyo = dyo_ref[...].astype(F32)
        sg = jax.nn.sigmoid(_dot(ya, wg_ref[...]))
        dt = dyo * yaf * sg * (1.0 - sg)
        dya_ref[...] = (dyo * sg + _dot_nt(dt, wg_ref[...])).astype(ACT)
        dwg_ref[...] += _dot_tn(ya, dt)

    return pl.pallas_call(
        body, name=name, out_shape=(jax.ShapeDtypeStruct((T, W), ACT), jax.ShapeDtypeStruct((W, W), F32)),
        grid=(T // tm,),
        in_specs=[pl.BlockSpec((tm, W), lambda i: (i, 0)), pl.BlockSpec((tm, W), lambda i: (i, 0)),
                  pl.BlockSpec((W, W), lambda i: (0, 0))],
        out_specs=(pl.BlockSpec((tm, W), lambda i: (i, 0)), pl.BlockSpec((W, W), lambda i: (0, 0))),
        compiler_params=_cparams())(ya, dyo, wglu)


PAD = 16


def _chunk_cumsums(x, pad_ref, L):
    row = lax.broadcasted_iota(jnp.int32, x.shape, 0) % CHUNK
    zeros = jnp.zeros((PAD, x.shape[1]), F32)
    pad_ref[0:PAD, :] = zeros
    pad_ref[PAD + L:2 * PAD + L, :] = zeros
    c = x
    r = x
    d = 1
    while d < CHUNK:
        pad_ref[PAD:PAD + L, :] = c
        c = c + jnp.where(row >= d, pad_ref[PAD - d:PAD - d + L, :], 0.0)
        pad_ref[PAD:PAD + L, :] = r
        r = r + jnp.where(row + d < CHUNK, pad_ref[PAD + d:PAD + d + L, :], 0.0)
        d *= 2
    return c, r - x


def _hgrn_prep(q_ref, fl_ref, lb_ref, pad_ref, r0, n):
    rows = pl.ds(r0, n)
    lb = lb_ref[...]
    sig = jax.nn.sigmoid(fl_ref[rows, :].astype(F32))
    f = lb + (1.0 - lb) * sig
    k = 1.0 - f
    c, rc = _chunk_cumsums(jnp.log(f), pad_ref, n)
    e_in, e_inv, e_out = jnp.exp(c), jnp.exp(-c), jnp.exp(rc)
    q = q_ref[rows, :].astype(F32)
    return dict(sig=sig, f=f, k=k, q=q, e_in=e_in, e_inv=e_inv, e_out=e_out, dec=jnp.exp(c + rc))


def _for_row_blocks(L, fn):
    full = L // GROUP
    if full:
        def step(g, carry):
            fn(pl.multiple_of(g * GROUP, GROUP), GROUP)
            return carry
        lax.fori_loop(0, full, step, 0)
    if L % GROUP:
        fn(full * GROUP, L % GROUP)


def _chunk_mask(rb):
    r = lax.broadcasted_iota(jnp.int32, (rb, rb), 0)
    c = lax.broadcasted_iota(jnp.int32, (rb, rb), 1)
    return (r // CHUNK == c // CHUNK) & (c <= r)


def _hg_out(o, og, g):
    on = o * lax.rsqrt(jnp.mean(o * o, axis=-1, keepdims=True) + EPS) * g
    return on * _silu(og)


def _hgrn_specs(L, hd, col_q, n_heads, order):
    def spec(sec):
        return pl.BlockSpec((None, L, hd), lambda *g: (order(*g)[0], 0, col_q + sec * n_heads + order(*g)[1]))
    return [spec(0), spec(1), spec(2), spec(3)]


GROUP = 128
CPG = GROUP // CHUNK


def _expand(x):
    xf = x.astype(F32)
    chunk = lax.broadcasted_iota(jnp.int32, xf.shape, 0) // CHUNK
    return jnp.concatenate([jnp.where(chunk == j, xf, 0.0) for j in range(CPG)], axis=1)


def _fill_tail(refs_fills, L):
    for ref, fill in refs_fills:
        if ref.shape[0] > L:
            ref[L:ref.shape[0], :] = jnp.full((ref.shape[0] - L, ref.shape[1]), fill, ref.dtype)


GROUP_UNROLL = 4


def _hgrn_forward_core(q_ref, fl_ref, v_ref, lb_ref, pad_ref, qin_ref, kin_ref, kout_ref, vp_ref, dec_ref, o_ref,
                       s_ref, L):
    hd = qin_ref.shape[1]
    n_groups = qin_ref.shape[0] // GROUP

    def prep(r0, n):
        pp = _hgrn_prep(q_ref, fl_ref, lb_ref, pad_ref, r0, n)
        rows = pl.ds(r0, n)
        qin_ref[rows, :] = (pp["q"] * pp["e_in"]).astype(MXU)
        kin_ref[rows, :] = (pp["k"] * pp["e_inv"]).astype(MXU)
        kout_ref[rows, :] = (pp["k"] * pp["e_out"]).astype(MXU)
        vp_ref[rows, :] = v_ref[rows, :].astype(MXU)
        dec_ref[rows, :] = pp["dec"]

    _for_row_blocks(L, prep)
    _fill_tail(((qin_ref, 0.0), (kin_ref, 0.0), (kout_ref, 0.0), (vp_ref, 0.0), (dec_ref, 1.0)), L)
    mask = _chunk_mask(GROUP)

    def intra(g, carry):
        rows = pl.ds(pl.multiple_of(g * GROUP, GROUP), GROUP)
        a = jnp.where(mask, _dot_nt(qin_ref[rows, :], kin_ref[rows, :]), 0.0)
        o_ref[rows, :] = _dot(a, vp_ref[rows, :])
        kv = _dot_tn(vp_ref[rows, :], _expand(kout_ref[rows, :]))
        for j in range(CPG):
            s_ref[g * CPG + j] = kv[:, j * hd:(j + 1) * hd]
        return carry

    lax.fori_loop(0, n_groups, intra, 0, unroll=GROUP_UNROLL)

    def rec(n, st):
        kv = s_ref[n]
        s_ref[n] = st
        dec = dec_ref[pl.ds(pl.multiple_of(n * CHUNK, CHUNK), SUBLANES), :][0:1]
        return st * dec + kv

    lax.fori_loop(0, L // CHUNK, rec, jnp.zeros((hd, hd), F32))

    def inter(g, carry):
        rows = pl.ds(pl.multiple_of(g * GROUP, GROUP), GROUP)
        scat = jnp.concatenate([s_ref[g * CPG + j] for j in range(CPG)], axis=1)
        o_ref[rows, :] += _dot_nt(_expand(qin_ref[rows, :]), scat)
        return carry

    lax.fori_loop(0, n_groups, inter, 0, unroll=GROUP_UNROLL)


def _hgrn_scratch(L, hd):
    lp = -(-L // GROUP) * GROUP
    return lp, [pltpu.VMEM((GROUP + 2 * PAD, hd), F32), pltpu.VMEM((lp, hd), MXU), pltpu.VMEM((lp, hd), MXU),
                pltpu.VMEM((lp, hd), MXU), pltpu.VMEM((lp, hd), MXU), pltpu.VMEM((lp, hd), F32),
                pltpu.VMEM((lp, hd), F32), pltpu.VMEM((lp // CHUNK, hd, hd), F32)]


def _hgrn_fwd_call(p3, lb, ng, n_heads, col_q, name):
    B, L, _ = p3.shape
    hd = ng.shape[1]
    _, scratch = _hgrn_scratch(L, hd)

    def body(q_ref, fl_ref, v_ref, og_ref, lb_ref, ng_ref, yb_ref,
             pad_ref, qin_ref, kin_ref, kout_ref, vp_ref, dec_ref, o_ref, s_ref):
        _hgrn_forward_core(q_ref, fl_ref, v_ref, lb_ref, pad_ref, qin_ref, kin_ref, kout_ref, vp_ref, dec_ref,
                           o_ref, s_ref, L)

        def out(r0, n):
            rows = pl.ds(r0, n)
            yb_ref[rows, :] = _hg_out(o_ref[rows, :], og_ref[rows, :].astype(F32), ng_ref[...]).astype(ACT)

        _for_row_blocks(L, out)

    order = lambda b, h: (b, h)
    return pl.pallas_call(
        body, name=name, out_shape=jax.ShapeDtypeStruct((B, L, n_heads * hd), ACT), grid=(B, n_heads),
        in_specs=_hgrn_specs(L, hd, col_q, n_heads, order) + [
            pl.BlockSpec((1, hd), lambda b, h: (0, h)), pl.BlockSpec((1, hd), lambda b, h: (0, 0))],
        out_specs=pl.BlockSpec((None, L, hd), lambda b, h: (b, 0, h)),
        scratch_shapes=scratch, compiler_params=_cparams())(p3, p3, p3, p3, lb, ng)


def _hgrn_bwd_call(p3, dyb, lb, ng, n_heads, col_q, name):
    B, L, _ = p3.shape
    hd = ng.shape[1]
    n_chunks = L // CHUNK
    lp, scratch = _hgrn_scratch(L, hd)
    n_groups = lp // GROUP

    def body(q_ref, fl_ref, v_ref, og_ref, dyb_ref, lb_ref, ng_ref,
             dq_ref, dfl_ref, dv_ref, dog_ref, dlb_ref, dng_ref,
             pad_ref, qin_ref, kin_ref, kout_ref, vp_ref, dec_ref, o_ref, s_ref,
             do_ref, ds_ref, dqi_ref, dki_ref, dko_ref, dvv_ref, dct_ref):
        @pl.when(pl.program_id(1) == 0)
        def _():
            dlb_ref[...] = jnp.zeros_like(dlb_ref)

        @pl.when((pl.program_id(0) == 0) & (pl.program_id(1) == 0))
        def _():
            dng_ref[...] = jnp.zeros_like(dng_ref)

        _hgrn_forward_core(q_ref, fl_ref, v_ref, lb_ref, pad_ref, qin_ref, kin_ref, kout_ref, vp_ref, dec_ref,
                           o_ref, s_ref, L)

        def out_bwd(r0, n):
            rows = pl.ds(r0, n)
            _, out_vjp = jax.vjp(_hg_out, o_ref[rows, :], og_ref[rows, :].astype(F32), ng_ref[...])
            d_o, d_og, d_ng = out_vjp(dyb_ref[rows, :].astype(F32))
            dog_ref[rows, :] = d_og.astype(ACT)
            dng_ref[...] += d_ng
            do_ref[rows, :] = d_o.astype(MXU)

        _for_row_blocks(L, out_bwd)
        _fill_tail(((do_ref, 0.0),), L)
        mask = _chunk_mask(GROUP)

        def grads_a(g, carry):
            rows = pl.ds(pl.multiple_of(g * GROUP, GROUP), GROUP)
            qi, ki, vv, do = qin_ref[rows, :], kin_ref[rows, :], vp_ref[rows, :], do_ref[rows, :]
            a = jnp.where(mask, _dot_nt(qi, ki), 0.0)
            da = jnp.where(mask, _dot_nt(do, vv), 0.0)
            sstack = s_ref[pl.ds(g * CPG, CPG)].reshape(CPG * hd, hd)
            dqi_ref[rows, :] = _dot(da, ki) + _dot(_expand(do), sstack)
            dki_ref[rows, :] = _dot_tn(da, qi)
            dvv_ref[rows, :] = _dot_tn(a, do)
            x = _dot_tn(do, _expand(qi))
            for j in range(CPG):
                ds_ref[g * CPG + j] = x[:, j * hd:(j + 1) * hd]
            return carry

        lax.fori_loop(0, n_groups, grads_a, 0, unroll=GROUP_UNROLL)

        def rec_bwd(k, dst):
            n = n_chunks - 1 - k
            r0 = pl.multiple_of(n * CHUNK, CHUNK)
            x = ds_ref[n]
            ds_ref[n] = dst
            dec = dec_ref[pl.ds(r0, SUBLANES), :][0:1]
            ddec = dec * jnp.sum(dst * s_ref[n], axis=0, keepdims=True)
            dct_ref[pl.ds(r0, CHUNK), :] = jnp.broadcast_to(ddec, (CHUNK, hd))
            return dst * dec + x

        lax.fori_loop(0, n_chunks, rec_bwd, jnp.zeros((hd, hd), F32))

        def grads_b(g, carry):
            rows = pl.ds(pl.multiple_of(g * GROUP, GROUP), GROUP)
            dscat = jnp.concatenate([ds_ref[g * CPG + j] for j in range(CPG)], axis=1)
            dvv_ref[rows, :] += _dot_nt(_expand(kout_ref[rows, :]), dscat)
            dstack = ds_ref[pl.ds(g * CPG, CPG)].reshape(CPG * hd, hd)
            dko_ref[rows, :] = _dot(_expand(vp_ref[rows, :]), dstack)
            return carry

        lax.fori_loop(0, n_groups, grads_b, 0, unroll=GROUP_UNROLL)

        def finish(r0, n):
            rows = pl.ds(r0, n)
            pp = _hgrn_prep(q_ref, fl_ref, lb_ref, pad_ref, r0, n)
            dqi, dki, dko = dqi_ref[rows, :], dki_ref[rows, :], dko_ref[rows, :]
            dq = dqi * pp["e_in"]
            dk = dki * pp["e_inv"] + dko * pp["e_out"]
            dq_ref[rows, :] = dq.astype(ACT)
            dv_ref[rows, :] = dvv_ref[rows, :].astype(ACT)
            t_out = pp["k"] * pp["e_out"] * dko
            dc = pp["q"] * pp["e_in"] * dqi - pp["k"] * pp["e_inv"] * dki - t_out
            _, dc_later = _chunk_cumsums(dc, pad_ref, n)
            t_incl, t_later = _chunk_cumsums(t_out, pad_ref, n)
            dlogf = dc + dc_later + t_incl + t_later + dct_ref[rows, :]
            df = dlogf / pp["f"] - dk
            sig = pp["sig"]
            dfl_ref[rows, :] = (df * (1.0 - lb_ref[...]) * sig * (1.0 - sig)).astype(ACT)
            dlb_ref[...] += jnp.sum(df * (1.0 - sig), axis=0, keepdims=True)

        _for_row_blocks(L, finish)

    order = lambda h, b: (b, h)
    W = n_heads * hd
    act_out = jax.ShapeDtypeStruct((B, L, W), ACT)
    blk_out = pl.BlockSpec((None, L, hd), lambda h, b: (b, 0, h))
    return pl.pallas_call(
        body, name=name,
        out_shape=(act_out, act_out, act_out, act_out, jax.ShapeDtypeStruct((1, W), F32),
                   jax.ShapeDtypeStruct((1, hd), F32)),
        grid=(n_heads, B),
        in_specs=_hgrn_specs(L, hd, col_q, n_heads, order) + [
            pl.BlockSpec((None, L, hd), lambda h, b: (b, 0, h)),
            pl.BlockSpec((1, hd), lambda h, b: (0, h)), pl.BlockSpec((1, hd), lambda h, b: (0, 0))],
        out_specs=(blk_out, blk_out, blk_out, blk_out, pl.BlockSpec((1, hd), lambda h, b: (0, h)),
                   pl.BlockSpec((1, hd), lambda h, b: (0, 0))),
        scratch_shapes=scratch + [
            pltpu.VMEM((lp, hd), MXU), pltpu.VMEM((lp // CHUNK, hd, hd), F32)] + [pltpu.VMEM((lp, hd), F32)] * 5,
        compiler_params=_cparams())(p3, p3, p3, p3, dyb, lb, ng)


def _merge_fn(a, bm, ga, gb):
    return jax.nn.sigmoid(ga) * a + jax.nn.sigmoid(gb) * bm


def _merge_call(yb, a, p, h0, whp, wout, g2, col_ga, tm, name):
    T, D = h0.shape

    def body(yb_ref, a_ref, ga_ref, gb_ref, h0_ref, whp_ref, wout_ref, g2_ref, h1_ref, mg_ref, bm_ref, z2_ref):
        bm = _dot(yb_ref[...], whp_ref[...])
        mg = _merge_fn(a_ref[...].astype(F32), bm, ga_ref[...].astype(F32), gb_ref[...].astype(F32))
        h1 = h0_ref[...] + _dot(mg, wout_ref[...])
        h1_ref[...] = h1
        mg_ref[...] = mg.astype(ACT)
        bm_ref[...] = bm.astype(ACT)
        z2_ref[...] = _rms(h1, g2_ref[...]).astype(ACT)

    tile = pl.BlockSpec((tm, D), lambda i: (i, 0))
    full = pl.BlockSpec((D, D), lambda i: (0, 0))
    act = jax.ShapeDtypeStruct((T, D), ACT)
    return pl.pallas_call(
        body, name=name, out_shape=(jax.ShapeDtypeStruct((T, D), F32), act, act, act), grid=(T // tm,),
        in_specs=[tile, tile, pl.BlockSpec((tm, D), lambda i: (i, col_ga)),
                  pl.BlockSpec((tm, D), lambda i: (i, col_ga + 1)), tile, full, full,
                  pl.BlockSpec((1, D), lambda i: (0, 0))],
        out_specs=(tile, tile, tile, tile), compiler_params=_cparams())(yb, a, p, p, h0, whp, wout, g2)


def _merge_bwd_call(dmg, a, bm, p, col_ga, tm, name):
    T, D = dmg.shape

    def body(dmg_ref, a_ref, bm_ref, ga_ref, gb_ref, da_ref, dbm_ref, dga_ref, dgb_ref):
        args = [r[...].astype(F32) for r in (a_ref, bm_ref, ga_ref, gb_ref)]
        _, vjp = jax.vjp(_merge_fn, *args)
        for r, o in zip((da_ref, dbm_ref, dga_ref, dgb_ref), vjp(dmg_ref[...].astype(F32))):
            r[...] = o.astype(ACT)

    tile = pl.BlockSpec((tm, D), lambda i: (i, 0))
    act = jax.ShapeDtypeStruct((T, D), ACT)
    return pl.pallas_call(
        body, name=name, out_shape=(act, act, act, act), grid=(T // tm,),
        in_specs=[tile, tile, tile, pl.BlockSpec((tm, D), lambda i: (i, col_ga)),
                  pl.BlockSpec((tm, D), lambda i: (i, col_ga + 1))],
        out_specs=(tile, tile, tile, tile), compiler_params=_cparams())(dmg, a, bm, p, p)


def _conv_taps(x_ref, halo_ref, ext_ref, edge, tm, before):
    halo = jnp.where(edge, 0.0, halo_ref[...].astype(F32))
    x = x_ref[...].astype(F32)
    if before:
        ext_ref[0:PAD, :] = halo
        ext_ref[PAD:PAD + tm, :] = x
        return [ext_ref[PAD - 2 + k:PAD - 2 + k + tm, :] for k in range(3)]
    ext_ref[0:tm, :] = x
    ext_ref[tm:tm + PAD, :] = halo
    return [ext_ref[k:k + tm, :] for k in range(3)]


def _conv(taps, cw, cb):
    return cb + cw[0:1] * taps[0] + cw[1:2] * taps[1] + cw[2:3] * taps[2]


def _ffn_pair_specs(tm, F, T, n_pairs, order, before):
    hb = tm // PAD
    last = T // PAD - 1

    def halo_row(i):
        return jnp.maximum(i * hb - 1, 0) if before else jnp.minimum((i + 1) * hb, last)

    specs = []
    for off in (0, n_pairs):
        specs.append(pl.BlockSpec((None, tm, F), lambda *g, off=off: (order(*g)[1] + off, order(*g)[0], 0)))
        specs.append(pl.BlockSpec((None, PAD, F), lambda *g, off=off: (order(*g)[1] + off, halo_row(order(*g)[0]), 0)))
    return specs


def _ffn_fwd_call(up, cw, cb, wd, h1, tgt, g3, tm, tps, name):
    S, T, F = up.shape
    n_pairs = S // 2
    D = h1.shape[1]

    def body(ua_ref, ha_ref, ub_ref, hb_ref, cwa_ref, cwb_ref, cba_ref, cbb_ref, wd_ref, h1_ref, tgt_ref, g3_ref,
             act_ref, dh2_ref, loss_ref, dg3_ref, acc_ref, ext_ref):
        i, j = pl.program_id(0), pl.program_id(1)
        edge = (i % tps) == 0
        ua = _conv(_conv_taps(ua_ref, ha_ref, ext_ref, edge, tm, True), cwa_ref[...], cba_ref[...])
        ub = _conv(_conv_taps(ub_ref, hb_ref, ext_ref, edge, tm, True), cwb_ref[...], cbb_ref[...])
        act = _silu(ua) * ub
        act_ref[...] = act.astype(ACT)
        contrib = _dot(act, wd_ref[...])

        @pl.when(j == 0)
        def _():
            acc_ref[...] = h1_ref[...] + contrib

        @pl.when(j > 0)
        def _():
            acc_ref[...] += contrib

        @pl.when((i == 0) & (j == 0))
        def _():
            loss_ref[...] = jnp.zeros_like(loss_ref)
            dg3_ref[...] = jnp.zeros_like(dg3_ref)

        @pl.when(j == n_pairs - 1)
        def _():
            row = lax.broadcasted_iota(jnp.int32, (tm, 1), 0) + (i % tps) * tm
            valid = row >= N_META
            tgt = tgt_ref[...]

            def loss_fn(h2, g):
                err = _rms(h2, g) - tgt
                return 0.5 * jnp.sum(jnp.where(valid, err * err, 0.0)) / D

            loss, vjp = jax.vjp(loss_fn, acc_ref[...], g3_ref[...])
            dh2, dg3 = vjp(jnp.ones((), F32))
            dh2_ref[...] = dh2
            loss_ref[...] += loss
            dg3_ref[...] += dg3

    order = lambda i, j: (i, j)
    tile = pl.BlockSpec((tm, D), lambda i, j: (i, 0))
    vec = pl.BlockSpec((1, D), lambda i, j: (0, 0))
    return pl.pallas_call(
        body, name=name,
        out_shape=(jax.ShapeDtypeStruct((n_pairs, T, F), ACT), jax.ShapeDtypeStruct((T, D), F32),
                   jax.ShapeDtypeStruct((1, LANES), F32), jax.ShapeDtypeStruct((1, D), F32)),
        grid=(T // tm, n_pairs),
        in_specs=_ffn_pair_specs(tm, F, T, n_pairs, order, True) + [
            pl.BlockSpec((None, 3, F), lambda i, j: (j, 0, 0)), pl.BlockSpec((None, 3, F), lambda i, j: (j + n_pairs, 0, 0)),
            pl.BlockSpec((None, 1, F), lambda i, j: (j, 0, 0)), pl.BlockSpec((None, 1, F), lambda i, j: (j + n_pairs, 0, 0)),
            pl.BlockSpec((None, F, D), lambda i, j: (j, 0, 0)), tile, tile, vec],
        out_specs=(pl.BlockSpec((None, tm, F), lambda i, j: (j, i, 0)), tile,
                   pl.BlockSpec((1, LANES), lambda i, j: (0, 0)), vec),
        scratch_shapes=[pltpu.VMEM((tm, D), F32), pltpu.VMEM((tm + PAD, F), F32)],
        compiler_params=_cparams())(up, up, up, up, cw, cw, cb, cb, wd, h1, tgt, g3)


def _ffn_bwd_a_call(dh2, up, act, cw, cb, wd, tm, tps, name):
    S, T, F = up.shape
    n_pairs = S // 2
    D = dh2.shape[1]

    def body(dh2_ref, ua_ref, ha_ref, ub_ref, hb_ref, act_ref, cwa_ref, cwb_ref, cba_ref, cbb_ref, wd_ref,
             dua_ref, dub_ref, dwd_ref, dcwa_ref, dcwb_ref, dcba_ref, dcbb_ref, ext_ref):
        i = pl.program_id(1)
        edge = (i % tps) == 0

        @pl.when(i == 0)
        def _():
            for r in (dwd_ref, dcwa_ref, dcwb_ref, dcba_ref, dcbb_ref):
                r[...] = jnp.zeros_like(r)

        dh2 = dh2_ref[...]
        dact = _dot_nt(dh2, wd_ref[...])
        dwd_ref[...] += _dot_tn(act_ref[...], dh2)
        taps_a = _conv_taps(ua_ref, ha_ref, ext_ref, edge, tm, True)
        ua = _conv(taps_a, cwa_ref[...], cba_ref[...])
        sa = jax.nn.sigmoid(ua)
        dub = dact * ua * sa
        dcbb_ref[...] += jnp.sum(dub, axis=0, keepdims=True)
        taps_b = _conv_taps(ub_ref, hb_ref, ext_ref, edge, tm, True)
        dcwb_ref[...] += jnp.concatenate([jnp.sum(dub * t, axis=0, keepdims=True) for t in taps_b], axis=0)
        ub = _conv(taps_b, cwb_ref[...], cbb_ref[...])
        dua = dact * ub * sa * (1.0 + ua * (1.0 - sa))
        dcba_ref[...] += jnp.sum(dua, axis=0, keepdims=True)
        taps_a = _conv_taps(ua_ref, ha_ref, ext_ref, edge, tm, True)
        dcwa_ref[...] += jnp.concatenate([jnp.sum(dua * t, axis=0, keepdims=True) for t in taps_a], axis=0)
        dua_ref[...] = dua.astype(ACT)
        dub_ref[...] = dub.astype(ACT)

    order = lambda j, i: (i, j)
    sh = lambda rows: jax.ShapeDtypeStruct((n_pairs, rows, F), F32)
    par = lambda rows: pl.BlockSpec((None, rows, F), lambda j, i: (j, 0, 0))
    return pl.pallas_call(
        body, name=name,
        out_shape=(jax.ShapeDtypeStruct((n_pairs, T, F), ACT), jax.ShapeDtypeStruct((n_pairs, T, F), ACT),
                   jax.ShapeDtypeStruct((n_pairs, F, D), F32), sh(3), sh(3), sh(1), sh(1)),
        grid=(n_pairs, T // tm),
        in_specs=[pl.BlockSpec((tm, D), lambda j, i: (i, 0))] + _ffn_pair_specs(tm, F, T, n_pairs, order, True) + [
            pl.BlockSpec((None, tm, F), lambda j, i: (j, i, 0)),
            pl.BlockSpec((None, 3, F), lambda j, i: (j, 0, 0)), pl.BlockSpec((None, 3, F), lambda j, i: (j + n_pairs, 0, 0)),
            pl.BlockSpec((None, 1, F), lambda j, i: (j, 0, 0)), pl.BlockSpec((None, 1, F), lambda j, i: (j + n_pairs, 0, 0)),
            pl.BlockSpec((None, F, D), lambda j, i: (j, 0, 0))],
        out_specs=(pl.BlockSpec((None, tm, F), lambda j, i: (j, i, 0)), pl.BlockSpec((None, tm, F), lambda j, i: (j, i, 0)),
                   pl.BlockSpec((None, F, D), lambda j, i: (j, 0, 0)), par(3), par(3), par(1), par(1)),
        scratch_shapes=[pltpu.VMEM((tm + PAD, F), F32)],
        compiler_params=_cparams())(dh2, up, up, up, up, act, cw, cw, cb, cb, wd)


def _ffn_bwd_b_call(dua, dub, cw, wup, h1, g2, dh2, tm, tps, name):
    n_pairs, T, F = dua.shape
    D = h1.shape[1]
    hb = tm // PAD
    last = T // PAD - 1

    def body(da_ref, na_ref, db_ref, nb_ref, cwa_ref, cwb_ref, wa_ref, wb_ref, h1_ref, g2_ref, dh2_ref,
             dupa_ref, dupb_ref, dh1_ref, dg2_ref, acc_ref, ext_ref):
        i, j = pl.program_id(0), pl.program_id(1)
        edge = (i % tps) == tps - 1
        outs = []
        for d_ref, n_ref, cw_ref, o_ref in ((da_ref, na_ref, cwa_ref, dupa_ref), (db_ref, nb_ref, cwb_ref, dupb_ref)):
            t = _conv_taps(d_ref, n_ref, ext_ref, edge, tm, False)
            cwv = cw_ref[...]
            dup = cwv[2:3] * t[0] + cwv[1:2] * t[1] + cwv[0:1] * t[2]
            o_ref[...] = dup.astype(ACT)
            outs.append(dup)
        contrib = _dot_nt(outs[0], wa_ref[...]) + _dot_nt(outs[1], wb_ref[...])

        @pl.when(j == 0)
        def _():
            acc_ref[...] = contrib

        @pl.when(j > 0)
        def _():
            acc_ref[...] += contrib

        @pl.when((i == 0) & (j == 0))
        def _():
            dg2_ref[...] = jnp.zeros_like(dg2_ref)

        @pl.when(j == n_pairs - 1)
        def _():
            _, vjp = jax.vjp(_rms, h1_ref[...], g2_ref[...])
            dh, dg = vjp(acc_ref[...])
            dh1_ref[...] = dh2_ref[...] + dh
            dg2_ref[...] += dg

    tile = pl.BlockSpec((tm, D), lambda i, j: (i, 0))
    vec = pl.BlockSpec((1, D), lambda i, j: (0, 0))
    pair = lambda: [pl.BlockSpec((None, tm, F), lambda i, j: (j, i, 0)),
                    pl.BlockSpec((None, PAD, F), lambda i, j: (j, jnp.minimum((i + 1) * hb, last), 0))]
    act = jax.ShapeDtypeStruct((n_pairs, T, F), ACT)
    return pl.pallas_call(
        body, name=name,
        out_shape=(act, act, jax.ShapeDtypeStruct((T, D), F32), jax.ShapeDtypeStruct((1, D), F32)),
        grid=(T // tm, n_pairs),
        in_specs=pair() + pair() + [
            pl.BlockSpec((None, 3, F), lambda i, j: (j, 0, 0)), pl.BlockSpec((None, 3, F), lambda i, j: (j + n_pairs, 0, 0)),
            pl.BlockSpec((None, D, F), lambda i, j: (j, 0, 0)), pl.BlockSpec((None, D, F), lambda i, j: (j + n_pairs, 0, 0)),
            tile, vec, tile],
        out_specs=(pl.BlockSpec((None, tm, F), lambda i, j: (j, i, 0)), pl.BlockSpec((None, tm, F), lambda i, j: (j, i, 0)),
                   tile, vec),
        scratch_shapes=[pltpu.VMEM((tm, D), F32), pltpu.VMEM((tm + PAD, F), F32)],
        compiler_params=_cparams())(dua, dua, dub, dub, cw, cw, wup, wup, h1, g2, dh2)


def _in_bwd_call(dp, w_in, h0, g1, dh1, tm, name):
    T, D = h0.shape
    S, _, N = w_in.shape

    def body(dp_ref, w_ref, h0_ref, g1_ref, dh1_ref, dh0_ref, dg1_ref, acc_ref):
        i, j = pl.program_id(0), pl.program_id(1)
        contrib = _dot_nt(dp_ref[...], w_ref[...])

        @pl.when(j == 0)
        def _():
            acc_ref[...] = contrib

        @pl.when(j > 0)
        def _():
            acc_ref[...] += contrib

        @pl.when((i == 0) & (j == 0))
        def _():
            dg1_ref[...] = jnp.zeros_like(dg1_ref)

        @pl.when(j == S - 1)
        def _():
            _, vjp = jax.vjp(_rms, h0_ref[...], g1_ref[...])
            dh, dg = vjp(acc_ref[...])
            dh0_ref[...] = dh1_ref[...] + dh
            dg1_ref[...] += dg

    tile = pl.BlockSpec((tm, D), lambda i, j: (i, 0))
    vec = pl.BlockSpec((1, D), lambda i, j: (0, 0))
    return pl.pallas_call(
        body, name=name, out_shape=(jax.ShapeDtypeStruct((T, D), F32), jax.ShapeDtypeStruct((1, D), F32)),
        grid=(T // tm, S),
        in_specs=[pl.BlockSpec((tm, N), lambda i, j: (i, j)), pl.BlockSpec((None, D, N), lambda i, j: (j, 0, 0)),
                  tile, vec, tile],
        out_specs=(tile, vec), scratch_shapes=[pltpu.VMEM((tm, D), F32)],
        compiler_params=_cparams())(dp, w_in, h0, g1, dh1)


def _meta_grad_call(dh0_3, name):
    B, L, D = dh0_3.shape

    def body(d_ref, o_ref):
        o_ref[...] = jnp.sum(d_ref[...], axis=0)

    return pl.pallas_call(
        body, name=name, out_shape=jax.ShapeDtypeStruct((N_META, D), F32), grid=(1,),
        in_specs=[pl.BlockSpec((B, N_META, D), lambda i: (0, 0, 0))],
        out_specs=pl.BlockSpec((N_META, D), lambda i: (0, 0)), compiler_params=_cparams())(dh0_3)


_RELS = [(dx, dy, dc) for dx in (0, 1) for dy in (0, 1) for dc in (0, 1)][1:]


def _exchange_call(arrs, scatter, name):
    n = len(arrs)
    n_rel = len(_RELS)

    def body(*refs):
        ins, outs = refs[:n], refs[n:2 * n]
        send_sems, recv_sems, loc_sems = refs[2 * n:]
        x, y, c = lax.axis_index("x"), lax.axis_index("y"), lax.axis_index("c")
        me = 4 * x + 2 * y + c
        started = []
        for k in range(n):
            src_me = ins[k].at[me] if scatter else ins[k]
            loc = pltpu.make_async_copy(src_me, outs[k].at[me], loc_sems.at[k])
            loc.start()
            started.append(loc)
        waits = []
        for r, (dx, dy, dc) in enumerate(_RELS):
            px, py, pc = (x + dx) % 2, (y + dy) % 2, (c + dc) % 2
            pid = 4 * px + 2 * py + pc
            for k in range(n):
                s = k * n_rel + r
                src = ins[k].at[pid] if scatter else ins[k]
                cp = pltpu.make_async_remote_copy(
                    src_ref=src, dst_ref=outs[k].at[me], send_sem=send_sems.at[s], recv_sem=recv_sems.at[s],
                    device_id=(px, py, pc), device_id_type=pl.DeviceIdType.MESH)
                cp.start()
                waits.append(pltpu.make_async_remote_copy(
                    src_ref=src, dst_ref=outs[k].at[pid], send_sem=send_sems.at[s], recv_sem=recv_sems.at[s],
                    device_id=(px, py, pc), device_id_type=pl.DeviceIdType.MESH))
        for w in waits:
            w.wait_send()
            w.wait_recv()
        for loc in started:
            loc.wait()

    out_shape = tuple(jax.ShapeDtypeStruct(a.shape if scatter else (N_DEV,) + a.shape, a.dtype) for a in arrs)
    hbm = pl.BlockSpec(memory_space=pl.ANY)
    return pl.pallas_call(
        body, name=name, out_shape=out_shape, in_specs=[hbm] * n, out_specs=tuple([hbm] * n),
        scratch_shapes=[pltpu.SemaphoreType.DMA((n * n_rel,)), pltpu.SemaphoreType.DMA((n * n_rel,)),
                        pltpu.SemaphoreType.DMA((n,))],
        compiler_params=pltpu.CompilerParams(has_side_effects=True))(*arrs)


_HBM = pl.BlockSpec(memory_space=pltpu.HBM)
_SEM = pl.BlockSpec(memory_space=pltpu.SEMAPHORE)
_DATAFLOW = pltpu.SideEffectType.DATAFLOW_SIDE_EFFECTING


def _peer_copies(ins, lands, send_sems, recv_sems, scatter):
    n = len(ins)
    x, y, c = lax.axis_index("x"), lax.axis_index("y"), lax.axis_index("c")
    me = 4 * x + 2 * y + c
    sends, arrivals = [], []
    for r, (dx, dy, dc) in enumerate(_RELS):
        px, py, pc = (x + dx) % 2, (y + dy) % 2, (c + dc) % 2
        pid = 4 * px + 2 * py + pc
        for k in range(n):
            s = k * len(_RELS) + r
            src = ins[k].at[pid] if scatter else ins[k]
            for dst, out in ((lands[k].at[me], sends), (lands[k].at[pid], arrivals)):
                out.append(pltpu.make_async_remote_copy(
                    src_ref=src, dst_ref=dst, send_sem=send_sems.at[s], recv_sem=recv_sems.at[s],
                    device_id=(px, py, pc), device_id_type=pl.DeviceIdType.MESH))
    return sends, arrivals


def _exchange_start(arrs, scatter, name):
    n = len(arrs)
    n_sem = n * len(_RELS)

    def body(*refs):
        ins, lands = refs[:n], refs[n:2 * n]
        send_sems, recv_sems = refs[2 * n], refs[2 * n + 1]
        token = refs[-1]
        sends, _ = _peer_copies(ins, lands, send_sems, recv_sems, scatter)
        for cp in sends:
            cp.start()
        token[...] = jnp.zeros_like(token)

    land_shapes = [a.shape if scatter else (N_DEV,) + a.shape for a in arrs]
    ops = [pltpu.with_memory_space_constraint(a, pltpu.HBM) for a in arrs]
    ops += [pltpu.with_memory_space_constraint(lax.empty(s, a.dtype), pltpu.HBM) for s, a in zip(land_shapes, arrs)]
    out = pl.pallas_call(
        body, name=name,
        out_shape=(pltpu.SemaphoreType.DMA((n_sem,)), pltpu.SemaphoreType.DMA((n_sem,)),
                   *[pltpu.HBM(a.shape, a.dtype) for a in arrs],
                   *[pltpu.HBM(s, a.dtype) for s, a in zip(land_shapes, arrs)],
                   jax.ShapeDtypeStruct((SUBLANES, LANES), F32)),
        in_specs=[_HBM] * (2 * n),
        out_specs=(_SEM, _SEM, *[_HBM] * (2 * n), pl.BlockSpec(memory_space=pltpu.VMEM)),
        input_output_aliases={i: 2 + i for i in range(2 * n)},
        compiler_params=pltpu.CompilerParams(has_side_effects=_DATAFLOW))(*ops)
    return out[0], out[1], list(out[2:2 + n]), list(out[2 + n:2 + 2 * n]), out[-1]


def _exchange_wait(started, after, scatter, name):
    send_sems, recv_sems, srcs, lands, _ = started
    n = len(srcs)

    def body(*refs):
        ins, lands_ = refs[:n], refs[n:2 * n]
        _, arrivals = _peer_copies(ins, lands_, refs[2 * n], refs[2 * n + 1], scatter)
        for cp in arrivals:
            cp.wait_send()
            cp.wait_recv()

    out = pl.pallas_call(
        body, name=name,
        out_shape=(*[pltpu.HBM(a.shape, a.dtype) for a in srcs], *[pltpu.HBM(a.shape, a.dtype) for a in lands]),
        in_specs=[_HBM] * (2 * n) + [_SEM, _SEM, pl.BlockSpec(memory_space=pl.ANY)],
        out_specs=tuple([_HBM] * (2 * n)), input_output_aliases={i: i for i in range(2 * n)},
        compiler_params=pltpu.CompilerParams(has_side_effects=_DATAFLOW))(*srcs, *lands, send_sems, recv_sems, after)
    return list(out[:n]), list(out[n:])


def _place_own_call(srcs, lands, scatter, me, name):
    outs = []
    for k, (src, land) in enumerate(zip(srcs, lands)):
        R, C = land.shape[1:]
        tr = R
        while tr % 32 == 0 and tr * C * land.dtype.itemsize > 2 * 1024 * 1024:
            tr //= 2

        def body(me_ref, s_ref, l_ref, o_ref):
            o_ref[...] = s_ref[...]

        src_spec = (pl.BlockSpec((None, tr, C), lambda i, me_ref: (me_ref[0], i, 0)) if scatter
                    else pl.BlockSpec((tr, C), lambda i, me_ref: (i, 0)))
        outs.append(pl.pallas_call(
            body, name=f"{name}_{k}", out_shape=jax.ShapeDtypeStruct(land.shape, land.dtype),
            grid_spec=pltpu.PrefetchScalarGridSpec(
                num_scalar_prefetch=1, grid=(R // tr,),
                in_specs=[src_spec, pl.BlockSpec(memory_space=pl.ANY)],
                out_specs=pl.BlockSpec((None, tr, C), lambda i, me_ref: (me_ref[0], i, 0))),
            input_output_aliases={2: 0}, compiler_params=_cparams())(me, src, land))
    return outs


def _adamw_shard_call(w, parts, m, v, name):
    R, C = w.shape
    tr = _tile(R, 128) if R % 16 == 0 else R

    def body(w_ref, p_ref, m_ref, v_ref, g_ref, d_ref, nm_ref, nv_ref):
        g = p_ref[0].astype(F32)
        for s in range(1, N_DEV):
            g = g + p_ref[s].astype(F32)
        d, nm, nv = _adamw(w_ref[...], g, m_ref[...], v_ref[...])
        g_ref[...] = g
        d_ref[...] = d
        nm_ref[...] = nm
        nv_ref[...] = nv

    tile = pl.BlockSpec((tr, C), lambda i: (i, 0))
    sh = jax.ShapeDtypeStruct((R, C), F32)
    return pl.pallas_call(
        body, name=name, out_shape=(sh, sh, sh, sh), grid=(R // tr,),
        in_specs=[tile, pl.BlockSpec((N_DEV, tr, C), lambda i: (0, i, 0)), tile, tile],
        out_specs=(tile, tile, tile, tile), compiler_params=_cparams())(w, parts, m, v)


def _pack(arrs, rows_mult=SUBLANES):
    flat = jnp.concatenate([a.reshape(-1).astype(F32) for a in arrs])
    n = flat.shape[0]
    per = rows_mult * LANES
    total = -(-n // per) * per
    return jnp.pad(flat, (0, total - n)).reshape(total // LANES, LANES)


def _unpack(pack, shapes):
    flat = pack.reshape(-1)
    out, off = [], 0
    for s in shapes:
        n = 1
        for d in s:
            n *= d
        out.append(flat[off:off + n].reshape(s))
        off += n
    return out


def kernel(x, meta_tokens, mix_norm_g, w_in, ssm_lambda_re, ssm_lambda_im, ssm_log_dt, ssm_b_re, ssm_b_im, ssm_c_re, ssm_c_im, ssm_d, ssm_w_glu, w_ssm_proj, hgrn_lb_logits, hgrn_norm_g, w_hgrn_proj, w_out, ffn_norm_g, w_up, conv_w, conv_b, w_down, final_norm_g, loss_target, m_meta_tokens, m_mix_norm_g, m_w_in, m_ssm_lambda_re, m_ssm_lambda_im, m_ssm_log_dt, m_ssm_b_re, m_ssm_b_im, m_ssm_c_re, m_ssm_c_im, m_ssm_d, m_ssm_w_glu, m_w_ssm_proj, m_hgrn_lb_logits, m_hgrn_norm_g, m_w_hgrn_proj, m_w_out, m_ffn_norm_g, m_w_up, m_conv_w, m_conv_b, m_w_down, m_final_norm_g, v_meta_tokens, v_mix_norm_g, v_w_in, v_ssm_lambda_re, v_ssm_lambda_im, v_ssm_log_dt, v_ssm_b_re, v_ssm_b_im, v_ssm_c_re, v_ssm_c_im, v_ssm_d, v_ssm_w_glu, v_w_ssm_proj, v_hgrn_lb_logits, v_hgrn_norm_g, v_w_hgrn_proj, v_w_out, v_ffn_norm_g, v_w_up, v_conv_w, v_conv_b, v_w_down, v_final_norm_g):
    args = dict(locals())
    B, S_len, D = x.shape
    L = S_len + N_META
    T = B * L
    tm = _tile(L, ROW_TILE_CAP)
    tps = L // tm
    G, P = ssm_lambda_re.shape[1:]
    H = ssm_b_re.shape[-1]
    W = G * H
    n_cb = W // LANES
    gpb = G // n_cb
    hd = hgrn_norm_g.shape[1]
    n_heads = D // hd
    n_in = w_in.shape[2]
    F = w_up.shape[2]
    assert W == D and n_in % LANES == 0

    me = (4 * lax.axis_index("x") + 2 * lax.axis_index("y") + lax.axis_index("c")).astype(jnp.int32).reshape(1)
    meta_g, cw_g = _exchange_call([meta_tokens, conv_w[0]], False, "gather_small_params")
    ga = _exchange_start([w_in[0].astype(MXU)], False, "gather_a_start")
    gb = _exchange_start(
        [w_up[0].astype(MXU), ssm_w_glu[0].astype(MXU), w_ssm_proj[0].astype(MXU), w_hgrn_proj[0].astype(MXU),
         w_out[0].astype(MXU), w_down[0].astype(MXU)], False, "gather_b_start")
    started_tok = (ga[4] + gb[4])[0:1, 0:1]
    meta_full = meta_g.transpose(1, 0, 2).reshape(N_META, D)
    cb_g = conv_b.reshape(N_DEV, 1, F)

    h0 = jnp.concatenate([jnp.broadcast_to(meta_full[None], (B, N_META, D)), x], axis=1).reshape(T, D)
    tgt = jnp.concatenate([jnp.zeros((B, N_META, D), F32), loss_target], axis=1).reshape(T, D)

    lr, li = ssm_lambda_re[0], ssm_lambda_im[0]
    ldt = ssm_log_dt[0].reshape(G, 1)
    bt_re = ssm_b_re[0].transpose(2, 0, 1).reshape(H, G * P)
    bt_im = ssm_b_im[0].transpose(2, 0, 1).reshape(H, G * P)
    disc = _small_call(_disc_a_powers, [lr, li, ldt], [((G, P), F32)] * (2 * SUBLANES + 2), "s5_discretise")
    pw_re, pw_im = jnp.stack(disc[:SUBLANES]), jnp.stack(disc[SUBLANES:2 * SUBLANES])
    coef_re, coef_im = disc[2 * SUBLANES:]
    bbt_re, bbt_im = _small_call(
        _disc_b, [coef_re.reshape(1, G * P), coef_im.reshape(1, G * P), bt_re, bt_im],
        [((H, G * P), F32)] * 2, "s5_input_matrix")
    eye = jnp.eye(gpb, dtype=F32)
    hw = gpb * P

    def expand_b(bbt):
        t = bbt.reshape(H, n_cb, gpb, P).transpose(1, 0, 2, 3)[:, None]
        return (t * eye[None, :, None, :, None]).reshape(n_cb, gpb * H, hw)

    def expand_c(cm):
        t = cm.reshape(n_cb, gpb, H, P).transpose(0, 1, 3, 2)[:, :, :, None]
        return (t * eye[None, :, None, :, None]).reshape(n_cb, hw, gpb * H)

    wb = jnp.concatenate([expand_b(bbt_re), expand_b(bbt_im)], axis=2).astype(MXU)
    wc = jnp.concatenate([expand_c(ssm_c_re[0]), -expand_c(ssm_c_im[0])], axis=1).astype(MXU)
    pwr = pw_re.reshape(SUBLANES, n_cb, hw).transpose(1, 0, 2)
    pwi = pw_im.reshape(SUBLANES, n_cb, hw).transpose(1, 0, 2)
    rows = jnp.arange(SUBLANES)[None, :, None]

    def table(sign, reverse):
        tabs = []
        for d in (1, 2, 4):
            keep = (rows + d < SUBLANES) if reverse else (rows >= d)
            tabs.append(jnp.concatenate([jnp.where(keep, pwr[:, d - 1:d], 0.0),
                                         jnp.where(keep, sign * pwi[:, d - 1:d], 0.0)], axis=2))
        cr, ci = (pwr[:, ::-1], pwi[:, ::-1]) if reverse else (pwr, pwi)
        tabs.append(jnp.concatenate([cr, sign * ci], axis=2))
        return jnp.stack(tabs, axis=1)

    tab_f, tab_r = table(1.0, False), table(-1.0, True)
    dsk = ssm_d.reshape(n_cb, 1, LANES)
    lb = _small_call(_lb_fn, [hgrn_lb_logits], [((1, D), F32)], "hgrn_lower_bound")[0]

    z1 = _norm_call(h0, mix_norm_g + started_tok, tm, "mix_norm")
    ga_src, ga_land = _exchange_wait(ga, z1, False, "gather_a_wait")
    win_g = _place_own_call(ga_src, ga_land, False, me, "gather_a_own")[0]
    p = _mm_shard(z1, win_g, tm, "in_proj", False)
    p3 = p.reshape(B, L, p.shape[1])
    ya, s_all = _s5_fwd_call(p3, wb, wc, tab_f, dsk, "s5_fwd")
    ya = ya.reshape(T, W)
    gb_src, gb_land = _exchange_wait(gb, ya, False, "gather_b_wait")
    gathered = _place_own_call(gb_src, gb_land, False, me, "gather_b_own")
    wup_g = gathered[0]
    wglu_g, wsp_g, whp_g, wout_g = [g.reshape(D, D) for g in gathered[1:5]]
    wdn_g = gathered[5].reshape(N_DEV // 2, 2 * w_down.shape[1], D)
    yo, a_br = _glu_proj_call(ya, wglu_g, wsp_g, tm, "s5_glu_proj")
    yb = _hgrn_fwd_call(p3, lb, hgrn_norm_g, n_heads, n_cb, "hgrn_fwd").reshape(T, D)
    col_ga = 5
    h1, mg, bm, z2 = _merge_call(yb, a_br, p, h0, whp_g, wout_g, ffn_norm_g, col_ga, tm, "merge")
    up = _mm_shard(z2, wup_g, tm, "up_proj", True)
    act, dh2, loss_part, dg3 = _ffn_fwd_call(up, cw_g, cb_g, wdn_g, h1, tgt, final_norm_g.reshape(1, D),
                                             tm, tps, "ffn_out_loss")

    dua, dub, dwd, dcwa, dcwb, dcba, dcbb = _ffn_bwd_a_call(dh2, up, act, cw_g, cb_g, wdn_g, tm, tps, "ffn_bwd_gate")
    dupa, dupb, dh1, dg2 = _ffn_bwd_b_call(dua, dub, cw_g, wup_g, h1, ffn_norm_g, dh2, tm, tps, "ffn_bwd_up")
    dwup = jnp.concatenate([_mm_tn(z2, dupa, N_DEV // 2, tm, "dw_up_a", True),
                            _mm_tn(z2, dupb, N_DEV // 2, tm, "dw_up_b", True)], axis=0)
    sh_rows = D // N_DEV
    sa = _exchange_start([dwup.astype(WIRE), dwd.reshape(N_DEV, w_down.shape[1], D).astype(WIRE)], True,
                         "scatter_a_start")
    dmg, dwout = _lin_bwd(mg, dh1, wout_g + sa[4][0:1, 0:1].astype(MXU), tm, "out_proj_bwd")
    da_br, dbm, dga, dgb = _merge_bwd_call(dmg, a_br, bm, p, col_ga, tm, "merge_bwd")
    dyo, dwsp = _lin_bwd(yo, da_br, wsp_g, tm, "ssm_proj_bwd")
    dyb, dwhp = _lin_bwd(yb, dbm, whp_g, tm, "hgrn_proj_bwd")
    dya, dwglu = _glu_bwd_call(ya, dyo, wglu_g, tm, "s5_glu_bwd")
    sb = _exchange_start([t.reshape(N_DEV, sh_rows, D).astype(WIRE) for t in (dwglu, dwsp, dwhp, dwout)], True,
                         "scatter_b_start")
    tok_b = sb[4][0:1, :]
    du, dwb, dwc, dab, ddsk = _s5_bwd_call(p3, s_all, dya.reshape(B, L, W), wb, wc, tab_r, dsk + tok_b[None],
                                           "s5_bwd")

    def diag_b(dw):
        t = (dw.reshape(n_cb, gpb, H, gpb, P) * eye[None, :, None, :, None]).sum(axis=1)
        return t.transpose(1, 0, 2, 3).reshape(H, G * P)

    def diag_c(dw):
        t = (dw.reshape(n_cb, gpb, P, gpb, H) * eye[None, :, None, :, None]).sum(axis=3)
        return t.transpose(0, 1, 3, 2).reshape(G, H, P)

    early_parts = [dab[:, 0, :hw].reshape(G, P), dab[:, 0, hw:].reshape(G, P),
                   diag_b(dwb[:, :, :hw]), diag_b(dwb[:, :, hw:]),
                   diag_c(dwc[:, :hw]), -diag_c(dwc[:, hw:]), ddsk.reshape(1, D)]
    early_pack = _pack(early_parts)
    se = _exchange_start([early_pack], False, "gather_s5_grads_start")
    dq, dfl, di, dog, dlb, dng = _hgrn_bwd_call(p3, dyb.reshape(B, L, D), lb, hgrn_norm_g + tok_b + se[4][0:1, :],
                                                n_heads, n_cb, "hgrn_bwd")
    dp = jnp.concatenate([du.reshape(T, W), dq.reshape(T, D), dfl.reshape(T, D), di.reshape(T, D),
                          dog.reshape(T, D), dga, dgb], axis=1)
    dwin = _mm_tn(z1, dp, N_DEV, tm, "dw_in", False)
    sc = _exchange_start([dwin.astype(WIRE)], True, "scatter_c_start")
    dh0, dg1 = _in_bwd_call(dp, win_g, h0, mix_norm_g + sc[4][0:1, 0:1], dh1, tm, "in_proj_bwd")
    dh0_3 = dh0.reshape(B, L, D)
    grad_x = dh0_3[:, N_META:]
    dmeta = _meta_grad_call(dh0_3, "meta_grad")

    late_parts = [dg1, dlb, dng, dg2, jnp.concatenate([dcba, dcbb], axis=0).reshape(1, N_DEV * F), dg3, loss_part]
    late_pack = _pack(late_parts)

    dcw = jnp.concatenate([dcwa, dcwb], axis=0)
    dmeta_s = dmeta.reshape(N_META, N_DEV, D // N_DEV).transpose(1, 0, 2)
    parts_d = _exchange_call([dmeta_s, dcw], True, "scatter_small_grads")
    late_all = _exchange_call([late_pack], False, "gather_small_grads")[0]
    early_all = _place_own_call(*_exchange_wait(se, late_all, False, "gather_s5_grads_wait"), False, me,
                                "gather_s5_grads_own")[0]
    parts_a = _place_own_call(*_exchange_wait(sa, late_all, True, "scatter_a_wait"), True, me, "scatter_a_own")
    parts_b = _place_own_call(*_exchange_wait(sb, late_all, True, "scatter_b_wait"), True, me, "scatter_b_own")
    parts_c = _place_own_call(*_exchange_wait(sc, late_all, True, "scatter_c_wait"), True, me, "scatter_c_own")
    parts = [parts_c[0], parts_a[0], *parts_b, parts_a[1], parts_d[0], parts_d[1]]

    def sum8(a, b):
        ta, tb = a[0], b[0]
        for s in range(1, N_DEV):
            ta, tb = ta + a[s], tb + b[s]
        return ta, tb

    early_sum, late_sum = _small_call(sum8, [early_all, late_all], [(early_pack.shape, F32), (late_pack.shape, F32)],
                                      "sum_small_grads")
    t_abr, t_abi, t_bbr, t_bbi, g_cre, g_cim, g_dsk = _unpack(early_sum, [a.shape for a in early_parts])
    g_g1, t_lb, g_ng, g_g2, g_cb, g_g3, loss_v = _unpack(late_sum, [a.shape for a in late_parts])

    def disc_b_bwd(cr, ci, br, bi, dbr, dbi):
        _, vjp = jax.vjp(_disc_b, cr, ci, br, bi)
        return vjp((dbr, dbi))

    t_cr, t_ci, g_btr, g_bti = _small_call(
        disc_b_bwd, [coef_re.reshape(1, G * P), coef_im.reshape(1, G * P), bt_re, bt_im, t_bbr, t_bbi],
        [((1, G * P), F32)] * 2 + [((H, G * P), F32)] * 2, "s5_input_matrix_bwd")

    def disc_a_bwd(lr_, li_, ldt_, dar, dai, dcr, dci):
        _, vjp = jax.vjp(_disc_a, lr_, li_, ldt_)
        return vjp((dar, dai, dcr, dci))

    g_lr, g_li, g_ldt = _small_call(
        disc_a_bwd, [lr, li, ldt, t_abr, t_abi, t_cr.reshape(G, P), t_ci.reshape(G, P)],
        [((G, P), F32)] * 2 + [((G, 1), F32)], "s5_discretise_bwd")

    def lb_bwd(logits, d):
        _, vjp = jax.vjp(_lb_fn, logits)
        return vjp(d)

    g_lbl = _small_call(lb_bwd, [hgrn_lb_logits, t_lb], [(hgrn_lb_logits.shape, F32)], "hgrn_lower_bound_bwd")[0]

    grads = dict(
        mix_norm_g=g_g1, ssm_lambda_re=g_lr[None], ssm_lambda_im=g_li[None], ssm_log_dt=g_ldt.reshape(1, G),
        ssm_b_re=g_btr.reshape(H, G, P).transpose(1, 2, 0)[None], ssm_b_im=g_bti.reshape(H, G, P).transpose(1, 2, 0)[None],
        ssm_c_re=g_cre[None], ssm_c_im=g_cim[None], ssm_d=g_dsk, hgrn_lb_logits=g_lbl, hgrn_norm_g=g_ng,
        ffn_norm_g=g_g2, conv_b=g_cb.reshape(1, N_DEV * F), final_norm_g=g_g3.reshape(D))
    loss = loss_v[0, 0]

    delta, new_m, new_v = {}, {}, {}
    sharded = [("w_in", parts[0], (D, n_in)), ("w_up", parts[1], (D, F)), ("ssm_w_glu", parts[2], (sh_rows, D)),
               ("w_ssm_proj", parts[3], (sh_rows, D)), ("w_hgrn_proj", parts[4], (sh_rows, D)),
               ("w_out", parts[5], (sh_rows, D)), ("w_down", parts[6], (w_down.shape[1], D)),
               ("meta_tokens", parts[7], (N_META, D // N_DEV)), ("conv_w", parts[8], (3, F))]
    for name, part, shp in sharded:
        full = args[name].shape
        g, d_, nm, nv = _adamw_shard_call(args[name].reshape(shp), part, args["m_" + name].reshape(shp),
                                          args["v_" + name].reshape(shp), "adamw_" + name)
        grads[name], delta[name], new_m[name], new_v[name] = [t.reshape(full) for t in (g, d_, nm, nv)]

    rep = ["mix_norm_g", "ssm_lambda_re", "ssm_lambda_im", "ssm_log_dt", "ssm_b_re", "ssm_b_im", "ssm_c_re",
           "ssm_c_im", "ssm_d", "hgrn_lb_logits", "hgrn_norm_g", "ffn_norm_g", "conv_b", "final_norm_g"]
    rep_shapes = [args[n].shape for n in rep]
    packs = [_pack([args[pre + n] for n in rep]) for pre in ("", "m_", "v_")]
    g_pack = _pack([grads[n] for n in rep])
    outs = _small_call(lambda w, g, m, v: _adamw(w, g, m, v), [packs[0], g_pack, packs[1], packs[2]],
                       [(g_pack.shape, F32)] * 3, "adamw_replicated")
    for n, d_, nm, nv in zip(rep, *[_unpack(o, rep_shapes) for o in outs]):
        delta[n], new_m[n], new_v[n] = d_, nm, nv

    names = ["meta_tokens", "mix_norm_g", "w_in", "ssm_lambda_re", "ssm_lambda_im", "ssm_log_dt", "ssm_b_re",
             "ssm_b_im", "ssm_c_re", "ssm_c_im", "ssm_d", "ssm_w_glu", "w_ssm_proj", "hgrn_lb_logits", "hgrn_norm_g",
             "w_hgrn_proj", "w_out", "ffn_norm_g", "w_up", "conv_w", "conv_b", "w_down", "final_norm_g"]
    return (loss, grad_x, *[grads[n] for n in names], *[delta[n] for n in names],
            *[new_m[n] for n in names], *[new_v[n] for n in names])
```

```python
import functools

import jax
import jax.numpy as jnp
from jax import lax
from jax.experimental import pallas as pl
from jax.experimental.pallas import tpu as pltpu

F32 = jnp.float32
MXU = jnp.bfloat16
ACT = jnp.bfloat16
WIRE = jnp.bfloat16
N_DEV = 8
N_META = 16
CHUNK = 16
EPS = 1e-6
ADAM_LR, ADAM_B1, ADAM_B2, ADAM_EPS, ADAM_WD, ADAM_STEP = 0.001, 0.9, 0.999, 1e-08, 0.01, 10
SUBLANES = 8
LANES = 128
ROW_TILE_CAP = 700
VMEM_LIMIT = 60 * 1024 * 1024


def _cparams(**kw):
    return pltpu.CompilerParams(vmem_limit_bytes=VMEM_LIMIT, **kw)


def _tile(n, cap):
    best = None
    for t in range(16, min(n, cap) + 1, 16):
        if n % t == 0:
            best = t
    assert best is not None, (n, cap)
    return best


def _dot(a, b):
    return lax.dot_general(a.astype(MXU), b.astype(MXU), (((1,), (0,)), ((), ())), preferred_element_type=F32)


def _dot_nt(a, b):
    return lax.dot_general(a.astype(MXU), b.astype(MXU), (((1,), (1,)), ((), ())), preferred_element_type=F32)


def _dot_tn(a, b):
    return lax.dot_general(a.astype(MXU), b.astype(MXU), (((0,), (0,)), ((), ())), preferred_element_type=F32)


def _rms(x, g):
    return x * lax.rsqrt(jnp.mean(x * x, axis=-1, keepdims=True) + EPS) * g


def _silu(x):
    return x * jax.nn.sigmoid(x)


def _small_call(fn, ins, out_shapes, name):
    n_in = len(ins)

    def body(*refs):
        outs = fn(*[r[...] for r in refs[:n_in]])
        outs = outs if isinstance(outs, (tuple, list)) else (outs,)
        for r, o in zip(refs[n_in:], outs):
            r[...] = o.astype(r.dtype)

    vm = pl.BlockSpec(memory_space=pltpu.VMEM)
    return pl.pallas_call(
        body, name=name, out_shape=tuple(jax.ShapeDtypeStruct(s, d) for s, d in out_shapes),
        in_specs=[vm] * n_in, out_specs=tuple([vm] * len(out_shapes)), compiler_params=_cparams())(*ins)


def _disc_a(lr, li, ldt):
    dt = jnp.exp(ldt)
    mag = jnp.exp(lr * dt)
    ab_re = mag * jnp.cos(li * dt)
    ab_im = mag * jnp.sin(li * dt)
    den = lr * lr + li * li
    nr = ab_re - 1.0
    coef_re = (nr * lr + ab_im * li) / den
    coef_im = (ab_im * lr - nr * li) / den
    return ab_re, ab_im, coef_re, coef_im


def _disc_a_power(n):
    def fn(lr, li, ldt):
        ab_re, ab_im, coef_re, coef_im = _disc_a(lr, li, ldt)
        pr, pi, sr, si, m = None, None, ab_re, ab_im, n
        while m:
            if m & 1:
                pr, pi = (sr, si) if pr is None else (pr * sr - pi * si, pr * si + pi * sr)
            m >>= 1
            if m:
                sr, si = sr * sr - si * si, 2.0 * sr * si
        return ab_re, ab_im, pr, pi, coef_re, coef_im
    return fn


def _disc_b(coef_re, coef_im, bt_re, bt_im):
    return coef_re * bt_re - coef_im * bt_im, coef_re * bt_im + coef_im * bt_re


def _lb_fn(logits):
    return jax.nn.softmax(logits, axis=0)[0:1]


def _adamw(w, g, m, v):
    m = ADAM_B1 * m + (1.0 - ADAM_B1) * g
    v = ADAM_B2 * v + (1.0 - ADAM_B2) * jnp.square(g)
    m_hat = m / (1.0 - ADAM_B1 ** ADAM_STEP)
    v_hat = v / (1.0 - ADAM_B2 ** ADAM_STEP)
    delta = -ADAM_LR * (m_hat / (jnp.sqrt(v_hat) + ADAM_EPS) + ADAM_WD * w)
    return delta, m, v


def _norm_call(h, g, tm, name):
    T, D = h.shape

    def body(h_ref, g_ref, z_ref):
        z_ref[...] = _rms(h_ref[...], g_ref[...]).astype(ACT)

    return pl.pallas_call(
        body, name=name, out_shape=jax.ShapeDtypeStruct((T, D), ACT), grid=(T // tm,),
        in_specs=[pl.BlockSpec((tm, D), lambda i: (i, 0)), pl.BlockSpec((1, D), lambda i: (0, 0))],
        out_specs=pl.BlockSpec((tm, D), lambda i: (i, 0)), compiler_params=_cparams())(h, g)


def _mm_shard(x, w, tm, name, major):
    T, K = x.shape
    S, _, N = w.shape

    def body(x_ref, w_ref, o_ref):
        o_ref[...] = _dot(x_ref[...], w_ref[...]).astype(o_ref.dtype)

    if major:
        out_shape = jax.ShapeDtypeStruct((S, T, N), ACT)
        out_spec = pl.BlockSpec((None, tm, N), lambda j, i: (j, i, 0))
    else:
        out_shape = jax.ShapeDtypeStruct((T, S * N), ACT)
        out_spec = pl.BlockSpec((tm, N), lambda j, i: (i, j))
    return pl.pallas_call(
        body, name=name, out_shape=out_shape, grid=(S, T // tm),
        in_specs=[pl.BlockSpec((tm, K), lambda j, i: (i, 0)), pl.BlockSpec((None, K, N), lambda j, i: (j, 0, 0))],
        out_specs=out_spec, compiler_params=_cparams())(x, w)


def _mm_tn(x, y, n_shards, tm, name, major):
    T, K = x.shape
    S = n_shards
    N = y.shape[-1] if major else y.shape[-1] // S

    def body(x_ref, y_ref, o_ref):
        @pl.when(pl.program_id(1) == 0)
        def _():
            o_ref[...] = jnp.zeros_like(o_ref)
        o_ref[...] += _dot_tn(x_ref[...], y_ref[...])

    y_spec = (pl.BlockSpec((None, tm, N), lambda j, i: (j, i, 0)) if major
              else pl.BlockSpec((tm, N), lambda j, i: (i, j)))
    return pl.pallas_call(
        body, name=name, out_shape=jax.ShapeDtypeStruct((S, K, N), F32), grid=(S, T // tm),
        in_specs=[pl.BlockSpec((tm, K), lambda j, i: (i, 0)), y_spec],
        out_specs=pl.BlockSpec((None, K, N), lambda j, i: (j, 0, 0)), compiler_params=_cparams())(x, y)


def _lin_bwd(x, dy, w, tm, name):
    T, K = x.shape
    N = dy.shape[1]

    def body(x_ref, dy_ref, w_ref, dx_ref, dw_ref):
        @pl.when(pl.program_id(0) == 0)
        def _():
            dw_ref[...] = jnp.zeros_like(dw_ref)
        dy = dy_ref[...]
        dx_ref[...] = _dot_nt(dy, w_ref[...]).astype(dx_ref.dtype)
        dw_ref[...] += _dot_tn(x_ref[...], dy)

    return pl.pallas_call(
        body, name=name,
        out_shape=(jax.ShapeDtypeStruct((T, K), ACT), jax.ShapeDtypeStruct((K, N), F32)), grid=(T // tm,),
        in_specs=[pl.BlockSpec((tm, K), lambda i: (i, 0)), pl.BlockSpec((tm, N), lambda i: (i, 0)),
                  pl.BlockSpec((K, N), lambda i: (0, 0))],
        out_specs=(pl.BlockSpec((tm, K), lambda i: (i, 0)), pl.BlockSpec((K, N), lambda i: (0, 0))),
        compiler_params=_cparams())(x, dy, w)


N_SEG = SUBLANES


def _seg_len(L):
    return -(-L // (N_SEG * 16)) * 16


def _to_segments(a3, seg):
    b, length, c = a3.shape
    a = jnp.pad(a3, ((0, 0), (0, N_SEG * seg - length), (0, 0)))
    return a.reshape(b, N_SEG, seg, c).transpose(0, 2, 1, 3).reshape(b, N_SEG * seg, c)


def _from_segments(a3, seg, length):
    b, _, c = a3.shape
    return a3.reshape(b, seg, N_SEG, c).transpose(0, 2, 1, 3).reshape(b, N_SEG * seg, c)[:, :length]


def _seg_scan(x_ref, tab_ref, n_slabs, reverse):
    hw = x_ref.shape[1] // 2
    sign = -1.0 if reverse else 1.0
    ar, ai = tab_ref[0][:, :hw], sign * tab_ref[0][:, hw:]
    br, bi = tab_ref[1][:, :hw], sign * tab_ref[1][:, hw:]

    def slab(k):
        kk = (n_slabs - 1 - k) if reverse else k
        return pl.ds(pl.multiple_of(kk * SUBLANES, SUBLANES), SUBLANES)

    def horner(k, carry):
        cr, ci = carry
        x = x_ref[slab(k), :]
        return ar * cr - ai * ci + x[:, :hw], ar * ci + ai * cr + x[:, hw:]

    z = jnp.zeros((SUBLANES, hw), F32)
    fr, fi = lax.fori_loop(0, n_slabs, horner, (z, z))

    row = lax.broadcasted_iota(jnp.int32, (SUBLANES, hw), 0)
    edge = (row == SUBLANES - 1) if reverse else (row == 0)
    shift = SUBLANES - 1 if reverse else 1
    sr, si = z, z
    for _ in range(N_SEG - 1):
        er, ei = fr + br * sr - bi * si, fi + br * si + bi * sr
        sr = jnp.where(edge, 0.0, pltpu.roll(er, shift, 0))
        si = jnp.where(edge, 0.0, pltpu.roll(ei, shift, 0))

    def scan(k, carry):
        cr, ci = carry
        rows = slab(k)
        x = x_ref[rows, :]
        nr, ni = ar * cr - ai * ci + x[:, :hw], ar * ci + ai * cr + x[:, hw:]
        x_ref[rows, 0:hw] = nr
        x_ref[rows, hw:2 * hw] = ni
        return nr, ni

    lax.fori_loop(0, n_slabs, scan, (sr, si))


def _s5_fwd_call(p3, wb, wc, tab_f, dsk, name):
    B, L, _ = p3.shape
    n_cb, cw, sw = wb.shape

    def body(u_ref, wb_ref, wc_ref, tab_ref, d_ref, ya_ref, so_ref, s_ref):
        u = u_ref[...]
        s_ref[...] = _dot(u, wb_ref[...])
        _seg_scan(s_ref, tab_ref, L // SUBLANES, False)
        s = s_ref[...].astype(MXU)
        so_ref[...] = s
        y = _dot(s, wc_ref[...]) + d_ref[...] * u.astype(F32)
        ya_ref[...] = jax.nn.gelu(y).astype(ACT)

    return pl.pallas_call(
        body, name=name,
        out_shape=(jax.ShapeDtypeStruct((B, L, n_cb * cw), ACT), jax.ShapeDtypeStruct((B, n_cb, L, sw), MXU)),
        grid=(B, n_cb),
        in_specs=[pl.BlockSpec((None, L, cw), lambda b, c: (b, 0, c)),
                  pl.BlockSpec((None, cw, sw), lambda b, c: (c, 0, 0)),
                  pl.BlockSpec((None, sw, cw), lambda b, c: (c, 0, 0)),
                  pl.BlockSpec((None, 2, SUBLANES, sw), lambda b, c: (c, 0, 0, 0)),
                  pl.BlockSpec((None, 1, cw), lambda b, c: (c, 0, 0))],
        out_specs=(pl.BlockSpec((None, L, cw), lambda b, c: (b, 0, c)),
                   pl.BlockSpec((None, None, L, sw), lambda b, c: (b, c, 0, 0))),
        scratch_shapes=[pltpu.VMEM((L, sw), F32)], compiler_params=_cparams())(p3, wb, wc, tab_f, dsk)


def _s5_bwd_call(p3, s_all, dya, wb, wc, tab_r, dsk, name):
    B, L, _ = p3.shape
    n_cb, cw, sw = wb.shape
    hw = sw // 2
    n_slabs = L // SUBLANES

    def body(u_ref, si_ref, dya_ref, wb_ref, wc_ref, tr_ref, d_ref,
             du_ref, dwb_ref, dwc_ref, da_ref, dd_ref, s_ref, l_ref):
        @pl.when(pl.program_id(1) == 0)
        def _():
            dwb_ref[...] = jnp.zeros_like(dwb_ref)
            dwc_ref[...] = jnp.zeros_like(dwc_ref)
            da_ref[...] = jnp.zeros_like(da_ref)
            dd_ref[...] = jnp.zeros_like(dd_ref)

        u = u_ref[...]
        uf = u.astype(F32)
        s_in = si_ref[...]
        s_ref[...] = s_in.astype(F32)
        y = _dot(s_in, wc_ref[...]) + d_ref[...] * uf
        _, gelu_vjp = jax.vjp(jax.nn.gelu, y)
        dy = gelu_vjp(dya_ref[...].astype(F32))[0]
        dd_ref[...] += jnp.sum(dy * uf, axis=0, keepdims=True)
        l_ref[...] = _dot_nt(dy, wc_ref[...])
        _seg_scan(l_ref, tr_ref, n_slabs, True)
        du_ref[...] = (_dot_nt(l_ref[...], wb_ref[...]) + d_ref[...] * dy).astype(ACT)
        dwb_ref[...] += _dot_tn(u, l_ref[...])
        dwc_ref[...] += _dot_tn(s_in, dy)

        row = lax.broadcasted_iota(jnp.int32, (SUBLANES, hw), 0)
        last = s_ref[pl.ds((n_slabs - 1) * SUBLANES, SUBLANES), :]
        p0r = jnp.where(row == 0, 0.0, pltpu.roll(last[:, :hw], 1, 0))
        p0i = jnp.where(row == 0, 0.0, pltpu.roll(last[:, hw:], 1, 0))

        def step(k, carry):
            qr, qi, accr, acci = carry
            r0 = pl.multiple_of(k * SUBLANES, SUBLANES)
            s = s_ref[pl.ds(r0, SUBLANES), :]
            lam = l_ref[pl.ds(r0, SUBLANES), :]
            lr, li = lam[:, :hw], lam[:, hw:]
            accr = accr + lr * qr + li * qi
            acci = acci + li * qr - lr * qi
            return s[:, :hw], s[:, hw:], accr, acci

        z8 = jnp.zeros((SUBLANES, hw), F32)
        _, _, accr, acci = lax.fori_loop(0, n_slabs, step, (p0r, p0i, z8, z8))
        da_ref[...] += jnp.concatenate([jnp.sum(accr, axis=0, keepdims=True),
                                        jnp.sum(acci, axis=0, keepdims=True)], axis=1)

    W = n_cb * cw
    return pl.pallas_call(
        body, name=name,
        out_shape=(jax.ShapeDtypeStruct((B, L, W), ACT), jax.ShapeDtypeStruct((n_cb, cw, sw), F32),
                   jax.ShapeDtypeStruct((n_cb, sw, cw), F32), jax.ShapeDtypeStruct((n_cb, 1, sw), F32),
                   jax.ShapeDtypeStruct((n_cb, 1, cw), F32)),
        grid=(n_cb, B),
        in_specs=[pl.BlockSpec((None, L, cw), lambda c, b: (b, 0, c)),
                  pl.BlockSpec((None, None, L, sw), lambda c, b: (b, c, 0, 0)),
                  pl.BlockSpec((None, L, cw), lambda c, b: (b, 0, c)),
                  pl.BlockSpec((None, cw, sw), lambda c, b: (c, 0, 0)),
                  pl.BlockSpec((None, sw, cw), lambda c, b: (c, 0, 0)),
                  pl.BlockSpec((None, 2, SUBLANES, sw), lambda c, b: (c, 0, 0, 0)),
                  pl.BlockSpec((None, 1, cw), lambda c, b: (c, 0, 0))],
        out_specs=(pl.BlockSpec((None, L, cw), lambda c, b: (b, 0, c)),
                   pl.BlockSpec((None, cw, sw), lambda c, b: (c, 0, 0)),
                   pl.BlockSpec((None, sw, cw), lambda c, b: (c, 0, 0)),
                   pl.BlockSpec((None, 1, sw), lambda c, b: (c, 0, 0)),
                   pl.BlockSpec((None, 1, cw), lambda c, b: (c, 0, 0))),
        scratch_shapes=[pltpu.VMEM((L, sw), F32), pltpu.VMEM((L, sw), F32)],
        compiler_params=_cparams())(p3, s_all, dya, wb, wc, tab_r, dsk)


def _glu_proj_call(ya, wglu, wproj, tm, name):
    T, W = ya.shape
    D = wproj.shape[1]

    def body(ya_ref, wg_ref, wp_ref, yo_ref, a_ref):
        ya = ya_ref[...]
        yo = ya.astype(F32) * jax.nn.sigmoid(_dot(ya, wg_ref[...]))
        yo_ref[...] = yo.astype(ACT)
        a_ref[...] = _dot(yo, wp_ref[...]).astype(ACT)

    return pl.pallas_call(
        body, name=name, out_shape=(jax.ShapeDtypeStruct((T, W), ACT), jax.ShapeDtypeStruct((T, D), ACT)),
        grid=(T // tm,),
        in_specs=[pl.BlockSpec((tm, W), lambda i: (i, 0)), pl.BlockSpec((W, W), lambda i: (0, 0)),
                  pl.BlockSpec((W, D), lambda i: (0, 0))],
        out_specs=(pl.BlockSpec((tm, W), lambda i: (i, 0)), pl.BlockSpec((tm, D), lambda i: (i, 0))),
        compiler_params=_cparams())(ya, wglu, wproj)


def _glu_bwd_call(ya, dyo, wglu, tm, name):
    T, W = ya.shape

    def body(ya_ref, dyo_ref, wg_ref, dya_ref, dwg_ref):
        @pl.when(pl.program_id(0) == 0)
        def _():
            dwg_ref[...] = jnp.zeros_like(dwg_ref)
        ya = ya_ref[...]
        yaf = ya.astype(F32)
        dyo = dyo_ref[...].astype(F32)
        sg = jax.nn.sigmoid(_dot(ya, wg_ref[...]))
        dt = dyo * yaf * sg * (1.0 - sg)
        dya_ref[...] = (dyo * sg + _dot_nt(dt, wg_ref[...])).astype(ACT)
        dwg_ref[...] += _dot_tn(ya, dt)

    return pl.pallas_call(
        body, name=name, out_shape=(jax.ShapeDtypeStruct((T, W), ACT), jax.ShapeDtypeStruct((W, W), F32)),
        grid=(T // tm,),
        in_specs=[pl.BlockSpec((tm, W), lambda i: (i, 0)), pl.BlockSpec((tm, W), lambda i: (i, 0)),
                  pl.BlockSpec((W, W), lambda i: (0, 0))],
        out_specs=(pl.BlockSpec((tm, W), lambda i: (i, 0)), pl.BlockSpec((W, W), lambda i: (0, 0))),
        compiler_params=_cparams())(ya, dyo, wglu)


PAD = 16


def _chunk_cumsums(x, pad_ref, L):
    row = lax.broadcasted_iota(jnp.int32, x.shape, 0) % CHUNK
    zeros = jnp.zeros((PAD, x.shape[1]), F32)
    pad_ref[0:PAD, :] = zeros
    pad_ref[PAD + L:2 * PAD + L, :] = zeros
    c = x
    r = x
    d = 1
    while d < CHUNK:
        pad_ref[PAD:PAD + L, :] = c
        c = c + jnp.where(row >= d, pad_ref[PAD - d:PAD - d + L, :], 0.0)
        pad_ref[PAD:PAD + L, :] = r
        r = r + jnp.where(row + d < CHUNK, pad_ref[PAD + d:PAD + d + L, :], 0.0)
        d *= 2
    return c, r - x


def _hgrn_prep(q_ref, fl_ref, lb_ref, pad_ref, r0, n):
    rows = pl.ds(r0, n)
    lb = lb_ref[...]
    sig = jax.nn.sigmoid(fl_ref[rows, :].astype(F32))
    f = lb + (1.0 - lb) * sig
    k = 1.0 - f
    c, rc = _chunk_cumsums(jnp.log(f), pad_ref, n)
    e_in, e_inv, e_out = jnp.exp(c), jnp.exp(-c), jnp.exp(rc)
    q = q_ref[rows, :].astype(F32)
    return dict(sig=sig, f=f, k=k, q=q, e_in=e_in, e_inv=e_inv, e_out=e_out, dec=jnp.exp(c + rc))


def _for_row_blocks(L, fn):
    full = L // GROUP
    if full:
        def step(g, carry):
            fn(pl.multiple_of(g * GROUP, GROUP), GROUP)
            return carry
        lax.fori_loop(0, full, step, 0)
    if L % GROUP:
        fn(full * GROUP, L % GROUP)


def _chunk_mask(rb):
    r = lax.broadcasted_iota(jnp.int32, (rb, rb), 0)
    c = lax.broadcasted_iota(jnp.int32, (rb, rb), 1)
    return (r // CHUNK == c // CHUNK) & (c <= r)


def _hg_out(o, og, g):
    on = o * lax.rsqrt(jnp.mean(o * o, axis=-1, keepdims=True) + EPS) * g
    return on * _silu(og)


def _hgrn_specs(L, hd, col_q, n_heads, order):
    def spec(sec):
        return pl.BlockSpec((None, L, hd), lambda *g: (order(*g)[0], 0, col_q + sec * n_heads + order(*g)[1]))
    return [spec(0), spec(1), spec(2), spec(3)]


GROUP = 128
CPG = GROUP // CHUNK


def _expand(x):
    xf = x.astype(F32)
    chunk = lax.broadcasted_iota(jnp.int32, xf.shape, 0) // CHUNK
    return jnp.concatenate([jnp.where(chunk == j, xf, 0.0) for j in range(CPG)], axis=1)


def _fill_tail(refs_fills, L):
    for ref, fill in refs_fills:
        if ref.shape[0] > L:
            ref[L:ref.shape[0], :] = jnp.full((ref.shape[0] - L, ref.shape[1]), fill, ref.dtype)


GROUP_UNROLL = 4


def _hgrn_forward_core(q_ref, fl_ref, v_ref, lb_ref, pad_ref, qin_ref, kin_ref, kout_ref, vp_ref, dec_ref, o_ref,
                       s_ref, L):
    hd = qin_ref.shape[1]
    n_groups = qin_ref.shape[0] // GROUP

    def prep(r0, n):
        pp = _hgrn_prep(q_ref, fl_ref, lb_ref, pad_ref, r0, n)
        rows = pl.ds(r0, n)
        qin_ref[rows, :] = (pp["q"] * pp["e_in"]).astype(MXU)
        kin_ref[rows, :] = (pp["k"] * pp["e_inv"]).astype(MXU)
        kout_ref[rows, :] = (pp["k"] * pp["e_out"]).astype(MXU)
        vp_ref[rows, :] = v_ref[rows, :].astype(MXU)
        dec_ref[rows, :] = pp["dec"]

    _for_row_blocks(L, prep)
    _fill_tail(((qin_ref, 0.0), (kin_ref, 0.0), (kout_ref, 0.0), (vp_ref, 0.0), (dec_ref, 1.0)), L)
    mask = _chunk_mask(GROUP)

    def intra(g, carry):
        rows = pl.ds(pl.multiple_of(g * GROUP, GROUP), GROUP)
        a = jnp.where(mask, _dot_nt(qin_ref[rows, :], kin_ref[rows, :]), 0.0)
        o_ref[rows, :] = _dot(a, vp_ref[rows, :])
        kv = _dot_tn(vp_ref[rows, :], _expand(kout_ref[rows, :]))
        for j in range(CPG):
            s_ref[g * CPG + j] = kv[:, j * hd:(j + 1) * hd]
        return carry

    lax.fori_loop(0, n_groups, intra, 0, unroll=GROUP_UNROLL)

    def rec(n, st):
        kv = s_ref[n]
        s_ref[n] = st
        dec = dec_ref[pl.ds(pl.multiple_of(n * CHUNK, CHUNK), SUBLANES), :][0:1]
        return st * dec + kv

    lax.fori_loop(0, L // CHUNK, rec, jnp.zeros((hd, hd), F32))

    def inter(g, carry):
        rows = pl.ds(pl.multiple_of(g * GROUP, GROUP), GROUP)
        scat = jnp.concatenate([s_ref[g * CPG + j] for j in range(CPG)], axis=1)
        o_ref[rows, :] += _dot_nt(_expand(qin_ref[rows, :]), scat)
        return carry

    lax.fori_loop(0, n_groups, inter, 0, unroll=GROUP_UNROLL)


def _hgrn_scratch(L, hd):
    lp = -(-L // GROUP) * GROUP
    return lp, [pltpu.VMEM((GROUP + 2 * PAD, hd), F32), pltpu.VMEM((lp, hd), MXU), pltpu.VMEM((lp, hd), MXU),
                pltpu.VMEM((lp, hd), MXU), pltpu.VMEM((lp, hd), MXU), pltpu.VMEM((lp, hd), F32),
                pltpu.VMEM((lp, hd), F32), pltpu.VMEM((lp // CHUNK, hd, hd), F32)]


def _hgrn_fwd_call(p3, lb, ng, n_heads, col_q, name):
    B, L, _ = p3.shape
    hd = ng.shape[1]
    _, scratch = _hgrn_scratch(L, hd)

    def body(q_ref, fl_ref, v_ref, og_ref, lb_ref, ng_ref, yb_ref,
             pad_ref, qin_ref, kin_ref, kout_ref, vp_ref, dec_ref, o_ref, s_ref):
        _hgrn_forward_core(q_ref, fl_ref, v_ref, lb_ref, pad_ref, qin_ref, kin_ref, kout_ref, vp_ref, dec_ref,
                           o_ref, s_ref, L)

        def out(r0, n):
            rows = pl.ds(r0, n)
            yb_ref[rows, :] = _hg_out(o_ref[rows, :], og_ref[rows, :].astype(F32), ng_ref[...]).astype(ACT)

        _for_row_blocks(L, out)

    order = lambda b, h: (b, h)
    return pl.pallas_call(
        body, name=name, out_shape=jax.ShapeDtypeStruct((B, L, n_heads * hd), ACT), grid=(B, n_heads),
        in_specs=_hgrn_specs(L, hd, col_q, n_heads, order) + [
            pl.BlockSpec((1, hd), lambda b, h: (0, h)), pl.BlockSpec((1, hd), lambda b, h: (0, 0))],
        out_specs=pl.BlockSpec((None, L, hd), lambda b, h: (b, 0, h)),
        scratch_shapes=scratch, compiler_params=_cparams())(p3, p3, p3, p3, lb, ng)


def _hgrn_bwd_call(p3, dyb, lb, ng, n_heads, col_q, name):
    B, L, _ = p3.shape
    hd = ng.shape[1]
    n_chunks = L // CHUNK
    lp, scratch = _hgrn_scratch(L, hd)
    n_groups = lp // GROUP

    def body(q_ref, fl_ref, v_ref, og_ref, dyb_ref, lb_ref, ng_ref,
             dq_ref, dfl_ref, dv_ref, dog_ref, dlb_ref, dng_ref,
             pad_ref, qin_ref, kin_ref, kout_ref, vp_ref, dec_ref, o_ref, s_ref,
             do_ref, ds_ref, dqi_ref, dki_ref, dko_ref, dvv_ref, dct_ref):
        @pl.when(pl.program_id(1) == 0)
        def _():
            dlb_ref[...] = jnp.zeros_like(dlb_ref)

        @pl.when((pl.program_id(0) == 0) & (pl.program_id(1) == 0))
        def _():
            dng_ref[...] = jnp.zeros_like(dng_ref)

        _hgrn_forward_core(q_ref, fl_ref, v_ref, lb_ref, pad_ref, qin_ref, kin_ref, kout_ref, vp_ref, dec_ref,
                           o_ref, s_ref, L)

        def out_bwd(r0, n):
            rows = pl.ds(r0, n)
            _, out_vjp = jax.vjp(_hg_out, o_ref[rows, :], og_ref[rows, :].astype(F32), ng_ref[...])
            d_o, d_og, d_ng = out_vjp(dyb_ref[rows, :].astype(F32))
            dog_ref[rows, :] = d_og.astype(ACT)
            dng_ref[...] += d_ng
            do_ref[rows, :] = d_o.astype(MXU)

        _for_row_blocks(L, out_bwd)
        _fill_tail(((do_ref, 0.0),), L)
        mask = _chunk_mask(GROUP)

        def grads_a(g, carry):
            rows = pl.ds(pl.multiple_of(g * GROUP, GROUP), GROUP)
            qi, ki, vv, do = qin_ref[rows, :], kin_ref[rows, :], vp_ref[rows, :], do_ref[rows, :]
            a = jnp.where(mask, _dot_nt(qi, ki), 0.0)
            da = jnp.where(mask, _dot_nt(do, vv), 0.0)
            sstack = s_ref[pl.ds(g * CPG, CPG)].reshape(CPG * hd, hd)
            dqi_ref[rows, :] = _dot(da, ki) + _dot(_expand(do), sstack)
            dki_ref[rows, :] = _dot_tn(da, qi)
            dvv_ref[rows, :] = _dot_tn(a, do)
            x = _dot_tn(do, _expand(qi))
            for j in range(CPG):
                ds_ref[g * CPG + j] = x[:, j * hd:(j + 1) * hd]
            return carry

        lax.fori_loop(0, n_groups, grads_a, 0, unroll=GROUP_UNROLL)

        def rec_bwd(k, dst):
            n = n_chunks - 1 - k
            r0 = pl.multiple_of(n * CHUNK, CHUNK)
            x = ds_ref[n]
            ds_ref[n] = dst
            dec = dec_ref[pl.ds(r0, SUBLANES), :][0:1]
            ddec = dec * jnp.sum(dst * s_ref[n], axis=0, keepdims=True)
            dct_ref[pl.ds(r0, CHUNK), :] = jnp.broadcast_to(ddec, (CHUNK, hd))
            return dst * dec + x

        lax.fori_loop(0, n_chunks, rec_bwd, jnp.zeros((hd, hd), F32))

        def grads_b(g, carry):
            rows = pl.ds(pl.multiple_of(g * GROUP, GROUP), GROUP)
            dscat = jnp.concatenate([ds_ref[g * CPG + j] for j in range(CPG)], axis=1)
            dvv_ref[rows, :] += _dot_nt(_expand(kout_ref[rows, :]), dscat)
            dstack = ds_ref[pl.ds(g * CPG, CPG)].reshape(CPG * hd, hd)
            dko_ref[rows, :] = _dot(_expand(vp_ref[rows, :]), dstack)
            return carry

        lax.fori_loop(0, n_groups, grads_b, 0, unroll=GROUP_UNROLL)

        def finish(r0, n):
            rows = pl.ds(r0, n)
            pp = _hgrn_prep(q_ref, fl_ref, lb_ref, pad_ref, r0, n)
            dqi, dki, dko = dqi_ref[rows, :], dki_ref[rows, :], dko_ref[rows, :]
            dq = dqi * pp["e_in"]
            dk = dki * pp["e_inv"] + dko * pp["e_out"]
            dq_ref[rows, :] = dq.astype(ACT)
            dv_ref[rows, :] = dvv_ref[rows, :].astype(ACT)
            t_out = pp["k"] * pp["e_out"] * dko
            dc = pp["q"] * pp["e_in"] * dqi - pp["k"] * pp["e_inv"] * dki - t_out
            _, dc_later = _chunk_cumsums(dc, pad_ref, n)
            t_incl, t_later = _chunk_cumsums(t_out, pad_ref, n)
            dlogf = dc + dc_later + t_incl + t_later + dct_ref[rows, :]
            df = dlogf / pp["f"] - dk
            sig = pp["sig"]
            dfl_ref[rows, :] = (df * (1.0 - lb_ref[...]) * sig * (1.0 - sig)).astype(ACT)
            dlb_ref[...] += jnp.sum(df * (1.0 - sig), axis=0, keepdims=True)

        _for_row_blocks(L, finish)

    order = lambda h, b: (b, h)
    W = n_heads * hd
    act_out = jax.ShapeDtypeStruct((B, L, W), ACT)
    blk_out = pl.BlockSpec((None, L, hd), lambda h, b: (b, 0, h))
    return pl.pallas_call(
        body, name=name,
        out_shape=(act_out, act_out, act_out, act_out, jax.ShapeDtypeStruct((1, W), F32),
                   jax.ShapeDtypeStruct((1, hd), F32)),
        grid=(n_heads, B),
        in_specs=_hgrn_specs(L, hd, col_q, n_heads, order) + [
            pl.BlockSpec((None, L, hd), lambda h, b: (b, 0, h)),
            pl.BlockSpec((1, hd), lambda h, b: (0, h)), pl.BlockSpec((1, hd), lambda h, b: (0, 0))],
        out_specs=(blk_out, blk_out, blk_out, blk_out, pl.BlockSpec((1, hd), lambda h, b: (0, h)),
                   pl.BlockSpec((1, hd), lambda h, b: (0, 0))),
        scratch_shapes=scratch + [
            pltpu.VMEM((lp, hd), MXU), pltpu.VMEM((lp // CHUNK, hd, hd), F32)] + [pltpu.VMEM((lp, hd), F32)] * 5,
        compiler_params=_cparams())(p3, p3, p3, p3, dyb, lb, ng)


def _merge_fn(a, bm, ga, gb):
    return jax.nn.sigmoid(ga) * a + jax.nn.sigmoid(gb) * bm


def _merge_call(yb, a, p, h0, whp, wout, g2, col_ga, tm, name):
    T, D = h0.shape

    def body(yb_ref, a_ref, ga_ref, gb_ref, h0_ref, whp_ref, wout_ref, g2_ref, h1_ref, mg_ref, bm_ref, z2_ref):
        bm = _dot(yb_ref[...], whp_ref[...])
        mg = _merge_fn(a_ref[...].astype(F32), bm, ga_ref[...].astype(F32), gb_ref[...].astype(F32))
        h1 = h0_ref[...] + _dot(mg, wout_ref[...])
        h1_ref[...] = h1
        mg_ref[...] = mg.astype(ACT)
        bm_ref[...] = bm.astype(ACT)
        z2_ref[...] = _rms(h1, g2_ref[...]).astype(ACT)

    tile = pl.BlockSpec((tm, D), lambda i: (i, 0))
    full = pl.BlockSpec((D, D), lambda i: (0, 0))
    act = jax.ShapeDtypeStruct((T, D), ACT)
    return pl.pallas_call(
        body, name=name, out_shape=(jax.ShapeDtypeStruct((T, D), F32), act, act, act), grid=(T // tm,),
        in_specs=[tile, tile, pl.BlockSpec((tm, D), lambda i: (i, col_ga)),
                  pl.BlockSpec((tm, D), lambda i: (i, col_ga + 1)), tile, full, full,
                  pl.BlockSpec((1, D), lambda i: (0, 0))],
        out_specs=(tile, tile, tile, tile), compiler_params=_cparams())(yb, a, p, p, h0, whp, wout, g2)


def _merge_bwd_call(dmg, a, bm, p, col_ga, tm, name):
    T, D = dmg.shape

    def body(dmg_ref, a_ref, bm_ref, ga_ref, gb_ref, da_ref, dbm_ref, dga_ref, dgb_ref):
        args = [r[...].astype(F32) for r in (a_ref, bm_ref, ga_ref, gb_ref)]
        _, vjp = jax.vjp(_merge_fn, *args)
        for r, o in zip((da_ref, dbm_ref, dga_ref, dgb_ref), vjp(dmg_ref[...].astype(F32))):
            r[...] = o.astype(ACT)

    tile = pl.BlockSpec((tm, D), lambda i: (i, 0))
    act = jax.ShapeDtypeStruct((T, D), ACT)
    return pl.pallas_call(
        body, name=name, out_shape=(act, act, act, act), grid=(T // tm,),
        in_specs=[tile, tile, tile, pl.BlockSpec((tm, D), lambda i: (i, col_ga)),
                  pl.BlockSpec((tm, D), lambda i: (i, col_ga + 1))],
        out_specs=(tile, tile, tile, tile), compiler_params=_cparams())(dmg, a, bm, p, p)


def _conv_taps(x_ref, halo_ref, ext_ref, edge, tm, before):
    halo = jnp.where(edge, 0.0, halo_ref[...].astype(F32))
    x = x_ref[...].astype(F32)
    if before:
        ext_ref[0:PAD, :] = halo
        ext_ref[PAD:PAD + tm, :] = x
        return [ext_ref[PAD - 2 + k:PAD - 2 + k + tm, :] for k in range(3)]
    ext_ref[0:tm, :] = x
    ext_ref[tm:tm + PAD, :] = halo
    return [ext_ref[k:k + tm, :] for k in range(3)]


def _conv(taps, cw, cb):
    return cb + cw[0:1] * taps[0] + cw[1:2] * taps[1] + cw[2:3] * taps[2]


def _ffn_pair_specs(tm, F, T, n_pairs, order, before):
    hb = tm // PAD
    last = T // PAD - 1

    def halo_row(i):
        return jnp.maximum(i * hb - 1, 0) if before else jnp.minimum((i + 1) * hb, last)

    specs = []
    for off in (0, n_pairs):
        specs.append(pl.BlockSpec((None, tm, F), lambda *g, off=off: (order(*g)[1] + off, order(*g)[0], 0)))
        specs.append(pl.BlockSpec((None, PAD, F), lambda *g, off=off: (order(*g)[1] + off, halo_row(order(*g)[0]), 0)))
    return specs


def _ffn_fwd_call(up, cw, cb, wd, h1, tgt, g3, tm, tps, name):
    S, T, F = up.shape
    n_pairs = S // 2
    D = h1.shape[1]

    def body(ua_ref, ha_ref, ub_ref, hb_ref, cwa_ref, cwb_ref, cba_ref, cbb_ref, wd_ref, h1_ref, tgt_ref, g3_ref,
             act_ref, dh2_ref, loss_ref, dg3_ref, acc_ref, ext_ref):
        i, j = pl.program_id(0), pl.program_id(1)
        edge = (i % tps) == 0
        ua = _conv(_conv_taps(ua_ref, ha_ref, ext_ref, edge, tm, True), cwa_ref[...], cba_ref[...])
        ub = _conv(_conv_taps(ub_ref, hb_ref, ext_ref, edge, tm, True), cwb_ref[...], cbb_ref[...])
        act = _silu(ua) * ub
        act_ref[...] = act.astype(ACT)
        contrib = _dot(act, wd_ref[...])

        @pl.when(j == 0)
        def _():
            acc_ref[...] = h1_ref[...] + contrib

        @pl.when(j > 0)
        def _():
            acc_ref[...] += contrib

        @pl.when((i == 0) & (j == 0))
        def _():
            loss_ref[...] = jnp.zeros_like(loss_ref)
            dg3_ref[...] = jnp.zeros_like(dg3_ref)

        @pl.when(j == n_pairs - 1)
        def _():
            row = lax.broadcasted_iota(jnp.int32, (tm, 1), 0) + (i % tps) * tm
            valid = row >= N_META
            tgt = tgt_ref[...]

            def loss_fn(h2, g):
                err = _rms(h2, g) - tgt
                return 0.5 * jnp.sum(jnp.where(valid, err * err, 0.0)) / D

            loss, vjp = jax.vjp(loss_fn, acc_ref[...], g3_ref[...])
            dh2, dg3 = vjp(jnp.ones((), F32))
            dh2_ref[...] = dh2
            loss_ref[...] += loss
            dg3_ref[...] += dg3

    order = lambda i, j: (i, j)
    tile = pl.BlockSpec((tm, D), lambda i, j: (i, 0))
    vec = pl.BlockSpec((1, D), lambda i, j: (0, 0))
    return pl.pallas_call(
        body, name=name,
        out_shape=(jax.ShapeDtypeStruct((n_pairs, T, F), ACT), jax.ShapeDtypeStruct((T, D), F32),
                   jax.ShapeDtypeStruct((1, LANES), F32), jax.ShapeDtypeStruct((1, D), F32)),
        grid=(T // tm, n_pairs),
        in_specs=_ffn_pair_specs(tm, F, T, n_pairs, order, True) + [
            pl.BlockSpec((None, 3, F), lambda i, j: (j, 0, 0)), pl.BlockSpec((None, 3, F), lambda i, j: (j + n_pairs, 0, 0)),
            pl.BlockSpec((None, 1, F), lambda i, j: (j, 0, 0)), pl.BlockSpec((None, 1, F), lambda i, j: (j + n_pairs, 0, 0)),
            pl.BlockSpec((None, F, D), lambda i, j: (j, 0, 0)), tile, tile, vec],
        out_specs=(pl.BlockSpec((None, tm, F), lambda i, j: (j, i, 0)), tile,
                   pl.BlockSpec((1, LANES), lambda i, j: (0, 0)), vec),
        scratch_shapes=[pltpu.VMEM((tm, D), F32), pltpu.VMEM((tm + PAD, F), F32)],
        compiler_params=_cparams())(up, up, up, up, cw, cw, cb, cb, wd, h1, tgt, g3)


def _ffn_bwd_a_call(dh2, up, act, cw, cb, wd, tm, tps, name):
    S, T, F = up.shape
    n_pairs = S // 2
    D = dh2.shape[1]

    def body(dh2_ref, ua_ref, ha_ref, ub_ref, hb_ref, act_ref, cwa_ref, cwb_ref, cba_ref, cbb_ref, wd_ref,
             dua_ref, dub_ref, dwd_ref, dcwa_ref, dcwb_ref, dcba_ref, dcbb_ref, ext_ref):
        i = pl.program_id(1)
        edge = (i % tps) == 0

        @pl.when(i == 0)
        def _():
            for r in (dwd_ref, dcwa_ref, dcwb_ref, dcba_ref, dcbb_ref):
                r[...] = jnp.zeros_like(r)

        dh2 = dh2_ref[...]
        dact = _dot_nt(dh2, wd_ref[...])
        dwd_ref[...] += _dot_tn(act_ref[...], dh2)
        taps_a = _conv_taps(ua_ref, ha_ref, ext_ref, edge, tm, True)
        ua = _conv(taps_a, cwa_ref[...], cba_ref[...])
        sa = jax.nn.sigmoid(ua)
        dub = dact * ua * sa
        dcbb_ref[...] += jnp.sum(dub, axis=0, keepdims=True)
        taps_b = _conv_taps(ub_ref, hb_ref, ext_ref, edge, tm, True)
        dcwb_ref[...] += jnp.concatenate([jnp.sum(dub * t, axis=0, keepdims=True) for t in taps_b], axis=0)
        ub = _conv(taps_b, cwb_ref[...], cbb_ref[...])
        dua = dact * ub * sa * (1.0 + ua * (1.0 - sa))
        dcba_ref[...] += jnp.sum(dua, axis=0, keepdims=True)
        taps_a = _conv_taps(ua_ref, ha_ref, ext_ref, edge, tm, True)
        dcwa_ref[...] += jnp.concatenate([jnp.sum(dua * t, axis=0, keepdims=True) for t in taps_a], axis=0)
        dua_ref[...] = dua.astype(ACT)
        dub_ref[...] = dub.astype(ACT)

    order = lambda j, i: (i, j)
    sh = lambda rows: jax.ShapeDtypeStruct((n_pairs, rows, F), F32)
    par = lambda rows: pl.BlockSpec((None, rows, F), lambda j, i: (j, 0, 0))
    return pl.pallas_call(
        body, name=name,
        out_shape=(jax.ShapeDtypeStruct((n_pairs, T, F), ACT), jax.ShapeDtypeStruct((n_pairs, T, F), ACT),
                   jax.ShapeDtypeStruct((n_pairs, F, D), F32), sh(3), sh(3), sh(1), sh(1)),
        grid=(n_pairs, T // tm),
        in_specs=[pl.BlockSpec((tm, D), lambda j, i: (i, 0))] + _ffn_pair_specs(tm, F, T, n_pairs, order, True) + [
            pl.BlockSpec((None, tm, F), lambda j, i: (j, i, 0)),
            pl.BlockSpec((None, 3, F), lambda j, i: (j, 0, 0)), pl.BlockSpec((None, 3, F), lambda j, i: (j + n_pairs, 0, 0)),
            pl.BlockSpec((None, 1, F), lambda j, i: (j, 0, 0)), pl.BlockSpec((None, 1, F), lambda j, i: (j + n_pairs, 0, 0)),
            pl.BlockSpec((None, F, D), lambda j, i: (j, 0, 0))],
        out_specs=(pl.BlockSpec((None, tm, F), lambda j, i: (j, i, 0)), pl.BlockSpec((None, tm, F), lambda j, i: (j, i, 0)),
                   pl.BlockSpec((None, F, D), lambda j, i: (j, 0, 0)), par(3), par(3), par(1), par(1)),
        scratch_shapes=[pltpu.VMEM((tm + PAD, F), F32)],
        compiler_params=_cparams())(dh2, up, up, up, up, act, cw, cw, cb, cb, wd)


def _ffn_bwd_b_call(dua, dub, cw, wup, h1, g2, dh2, tm, tps, name):
    n_pairs, T, F = dua.shape
    D = h1.shape[1]
    hb = tm // PAD
    last = T // PAD - 1

    def body(da_ref, na_ref, db_ref, nb_ref, cwa_ref, cwb_ref, wa_ref, wb_ref, h1_ref, g2_ref, dh2_ref,
             dupa_ref, dupb_ref, dh1_ref, dg2_ref, acc_ref, ext_ref):
        i, j = pl.program_id(0), pl.program_id(1)
        edge = (i % tps) == tps - 1
        outs = []
        for d_ref, n_ref, cw_ref, o_ref in ((da_ref, na_ref, cwa_ref, dupa_ref), (db_ref, nb_ref, cwb_ref, dupb_ref)):
            t = _conv_taps(d_ref, n_ref, ext_ref, edge, tm, False)
            cwv = cw_ref[...]
            dup = cwv[2:3] * t[0] + cwv[1:2] * t[1] + cwv[0:1] * t[2]
            o_ref[...] = dup.astype(ACT)
            outs.append(dup)
        contrib = _dot_nt(outs[0], wa_ref[...]) + _dot_nt(outs[1], wb_ref[...])

        @pl.when(j == 0)
        def _():
            acc_ref[...] = contrib

        @pl.when(j > 0)
        def _():
            acc_ref[...] += contrib

        @pl.when((i == 0) & (j == 0))
        def _():
            dg2_ref[...] = jnp.zeros_like(dg2_ref)

        @pl.when(j == n_pairs - 1)
        def _():
            _, vjp = jax.vjp(_rms, h1_ref[...], g2_ref[...])
            dh, dg = vjp(acc_ref[...])
            dh1_ref[...] = dh2_ref[...] + dh
            dg2_ref[...] += dg

    tile = pl.BlockSpec((tm, D), lambda i, j: (i, 0))
    vec = pl.BlockSpec((1, D), lambda i, j: (0, 0))
    pair = lambda: [pl.BlockSpec((None, tm, F), lambda i, j: (j, i, 0)),
                    pl.BlockSpec((None, PAD, F), lambda i, j: (j, jnp.minimum((i + 1) * hb, last), 0))]
    act = jax.ShapeDtypeStruct((n_pairs, T, F), ACT)
    return pl.pallas_call(
        body, name=name,
        out_shape=(act, act, jax.ShapeDtypeStruct((T, D), F32), jax.ShapeDtypeStruct((1, D), F32)),
        grid=(T // tm, n_pairs),
        in_specs=pair() + pair() + [
            pl.BlockSpec((None, 3, F), lambda i, j: (j, 0, 0)), pl.BlockSpec((None, 3, F), lambda i, j: (j + n_pairs, 0, 0)),
            pl.BlockSpec((None, D, F), lambda i, j: (j, 0, 0)), pl.BlockSpec((None, D, F), lambda i, j: (j + n_pairs, 0, 0)),
            tile, vec, tile],
        out_specs=(pl.BlockSpec((None, tm, F), lambda i, j: (j, i, 0)), pl.BlockSpec((None, tm, F), lambda i, j: (j, i, 0)),
                   tile, vec),
        scratch_shapes=[pltpu.VMEM((tm, D), F32), pltpu.VMEM((tm + PAD, F), F32)],
        compiler_params=_cparams())(dua, dua, dub, dub, cw, cw, wup, wup, h1, g2, dh2)


def _in_bwd_call(dp, w_in, h0, g1, dh1, tm, name):
    T, D = h0.shape
    S, _, N = w_in.shape

    def body(dp_ref, w_ref, h0_ref, g1_ref, dh1_ref, dh0_ref, dg1_ref, acc_ref):
        i, j = pl.program_id(0), pl.program_id(1)
        contrib = _dot_nt(dp_ref[...], w_ref[...])

        @pl.when(j == 0)
        def _():
            acc_ref[...] = contrib

        @pl.when(j > 0)
        def _():
            acc_ref[...] += contrib

        @pl.when((i == 0) & (j == 0))
        def _():
            dg1_ref[...] = jnp.zeros_like(dg1_ref)

        @pl.when(j == S - 1)
        def _():
            _, vjp = jax.vjp(_rms, h0_ref[...], g1_ref[...])
            dh, dg = vjp(acc_ref[...])
            dh0_ref[...] = dh1_ref[...] + dh
            dg1_ref[...] += dg

    tile = pl.BlockSpec((tm, D), lambda i, j: (i, 0))
    vec = pl.BlockSpec((1, D), lambda i, j: (0, 0))
    return pl.pallas_call(
        body, name=name, out_shape=(jax.ShapeDtypeStruct((T, D), F32), jax.ShapeDtypeStruct((1, D), F32)),
        grid=(T // tm, S),
        in_specs=[pl.BlockSpec((tm, N), lambda i, j: (i, j)), pl.BlockSpec((None, D, N), lambda i, j: (j, 0, 0)),
                  tile, vec, tile],
        out_specs=(tile, vec), scratch_shapes=[pltpu.VMEM((tm, D), F32)],
        compiler_params=_cparams())(dp, w_in, h0, g1, dh1)


def _meta_grad_call(dh0_3, name):
    B, L, D = dh0_3.shape

    def body(d_ref, o_ref):
        o_ref[...] = jnp.sum(d_ref[...], axis=0)

    return pl.pallas_call(
        body, name=name, out_shape=jax.ShapeDtypeStruct((N_META, D), F32), grid=(1,),
        in_specs=[pl.BlockSpec((B, N_META, D), lambda i: (0, 0, 0))],
        out_specs=pl.BlockSpec((N_META, D), lambda i: (0, 0)), compiler_params=_cparams())(dh0_3)


_RELS = [(dx, dy, dc) for dx in (0, 1) for dy in (0, 1) for dc in (0, 1)][1:]


def _exchange_call(arrs, scatter, name):
    n = len(arrs)
    n_rel = len(_RELS)

    def body(*refs):
        ins, outs = refs[:n], refs[n:2 * n]
        send_sems, recv_sems, loc_sems = refs[2 * n:]
        x, y, c = lax.axis_index("x"), lax.axis_index("y"), lax.axis_index("c")
        me = 4 * x + 2 * y + c
        started = []
        for k in range(n):
            src_me = ins[k].at[me] if scatter else ins[k]
            loc = pltpu.make_async_copy(src_me, outs[k].at[me], loc_sems.at[k])
            loc.start()
            started.append(loc)
        waits = []
        for r, (dx, dy, dc) in enumerate(_RELS):
            px, py, pc = (x + dx) % 2, (y + dy) % 2, (c + dc) % 2
            pid = 4 * px + 2 * py + pc
            for k in range(n):
                s = k * n_rel + r
                src = ins[k].at[pid] if scatter else ins[k]
                cp = pltpu.make_async_remote_copy(
                    src_ref=src, dst_ref=outs[k].at[me], send_sem=send_sems.at[s], recv_sem=recv_sems.at[s],
                    device_id=(px, py, pc), device_id_type=pl.DeviceIdType.MESH)
                cp.start()
                waits.append(pltpu.make_async_remote_copy(
                    src_ref=src, dst_ref=outs[k].at[pid], send_sem=send_sems.at[s], recv_sem=recv_sems.at[s],
                    device_id=(px, py, pc), device_id_type=pl.DeviceIdType.MESH))
        for w in waits:
            w.wait_send()
            w.wait_recv()
        for loc in started:
            loc.wait()

    out_shape = tuple(jax.ShapeDtypeStruct(a.shape if scatter else (N_DEV,) + a.shape, a.dtype) for a in arrs)
    hbm = pl.BlockSpec(memory_space=pl.ANY)
    return pl.pallas_call(
        body, name=name, out_shape=out_shape, in_specs=[hbm] * n, out_specs=tuple([hbm] * n),
        scratch_shapes=[pltpu.SemaphoreType.DMA((n * n_rel,)), pltpu.SemaphoreType.DMA((n * n_rel,)),
                        pltpu.SemaphoreType.DMA((n,))],
        compiler_params=pltpu.CompilerParams(has_side_effects=True))(*arrs)


_HBM = pl.BlockSpec(memory_space=pltpu.HBM)
_SEM = pl.BlockSpec(memory_space=pltpu.SEMAPHORE)
_DATAFLOW = pltpu.SideEffectType.DATAFLOW_SIDE_EFFECTING


def _peer_copies(ins, lands, send_sems, recv_sems, scatter):
    n = len(ins)
    x, y, c = lax.axis_index("x"), lax.axis_index("y"), lax.axis_index("c")
    me = 4 * x + 2 * y + c
    sends, arrivals = [], []
    for r, (dx, dy, dc) in enumerate(_RELS):
        px, py, pc = (x + dx) % 2, (y + dy) % 2, (c + dc) % 2
        pid = 4 * px + 2 * py + pc
        for k in range(n):
            s = k * len(_RELS) + r
            src = ins[k].at[pid] if scatter else ins[k]
            for dst, out in ((lands[k].at[me], sends), (lands[k].at[pid], arrivals)):
                out.append(pltpu.make_async_remote_copy(
                    src_ref=src, dst_ref=dst, send_sem=send_sems.at[s], recv_sem=recv_sems.at[s],
                    device_id=(px, py, pc), device_id_type=pl.DeviceIdType.MESH))
    return sends, arrivals


def _exchange_start(arrs, scatter, name):
    n = len(arrs)
    n_sem = n * len(_RELS)

    def body(*refs):
        ins, lands = refs[:n], refs[n:2 * n]
        send_sems, recv_sems = refs[2 * n], refs[2 * n + 1]
        token = refs[-1]
        sends, _ = _peer_copies(ins, lands, send_sems, recv_sems, scatter)
        for cp in sends:
            cp.start()
        token[...] = jnp.zeros_like(token)

    land_shapes = [a.shape if scatter else (N_DEV,) + a.shape for a in arrs]
    ops = [pltpu.with_memory_space_constraint(a, pltpu.HBM) for a in arrs]
    ops += [pltpu.with_memory_space_constraint(lax.empty(s, a.dtype), pltpu.HBM) for s, a in zip(land_shapes, arrs)]
    out = pl.pallas_call(
        body, name=name,
        out_shape=(pltpu.SemaphoreType.DMA((n_sem,)), pltpu.SemaphoreType.DMA((n_sem,)),
                   *[pltpu.HBM(a.shape, a.dtype) for a in arrs],
                   *[pltpu.HBM(s, a.dtype) for s, a in zip(land_shapes, arrs)],
                   jax.ShapeDtypeStruct((SUBLANES, LANES), F32)),
        in_specs=[_HBM] * (2 * n),
        out_specs=(_SEM, _SEM, *[_HBM] * (2 * n), pl.BlockSpec(memory_space=pltpu.VMEM)),
        input_output_aliases={i: 2 + i for i in range(2 * n)},
        compiler_params=pltpu.CompilerParams(has_side_effects=_DATAFLOW))(*ops)
    return out[0], out[1], list(out[2:2 + n]), list(out[2 + n:2 + 2 * n]), out[-1]


def _exchange_wait(started, after, scatter, name):
    send_sems, recv_sems, srcs, lands, _ = started
    n = len(srcs)

    def body(*refs):
        ins, lands_ = refs[:n], refs[n:2 * n]
        _, arrivals = _peer_copies(ins, lands_, refs[2 * n], refs[2 * n + 1], scatter)
        for cp in arrivals:
            cp.wait_send()
            cp.wait_recv()

    out = pl.pallas_call(
        body, name=name,
        out_shape=(*[pltpu.HBM(a.shape, a.dtype) for a in srcs], *[pltpu.HBM(a.shape, a.dtype) for a in lands]),
        in_specs=[_HBM] * (2 * n) + [_SEM, _SEM, pl.BlockSpec(memory_space=pl.ANY)],
        out_specs=tuple([_HBM] * (2 * n)), input_output_aliases={i: i for i in range(2 * n)},
        compiler_params=pltpu.CompilerParams(has_side_effects=_DATAFLOW))(*srcs, *lands, send_sems, recv_sems, after)
    return list(out[:n]), list(out[n:])


def _place_own_call(srcs, lands, scatter, me, name):
    outs = []
    for k, (src, land) in enumerate(zip(srcs, lands)):
        R, C = land.shape[1:]
        tr = R
        while tr % 32 == 0 and tr * C * land.dtype.itemsize > 2 * 1024 * 1024:
            tr //= 2

        def body(me_ref, s_ref, l_ref, o_ref):
            o_ref[...] = s_ref[...]

        src_spec = (pl.BlockSpec((None, tr, C), lambda i, me_ref: (me_ref[0], i, 0)) if scatter
                    else pl.BlockSpec((tr, C), lambda i, me_ref: (i, 0)))
        outs.append(pl.pallas_call(
            body, name=f"{name}_{k}", out_shape=jax.ShapeDtypeStruct(land.shape, land.dtype),
            grid_spec=pltpu.PrefetchScalarGridSpec(
                num_scalar_prefetch=1, grid=(R // tr,),
                in_specs=[src_spec, pl.BlockSpec(memory_space=pl.ANY)],
                out_specs=pl.BlockSpec((None, tr, C), lambda i, me_ref: (me_ref[0], i, 0))),
            input_output_aliases={2: 0}, compiler_params=_cparams())(me, src, land))
    return outs


def _adamw_shard_call(w, parts, m, v, name):
    R, C = w.shape
    tr = _tile(R, 128) if R % 16 == 0 else R

    def body(w_ref, p_ref, m_ref, v_ref, g_ref, d_ref, nm_ref, nv_ref):
        g = p_ref[0].astype(F32)
        for s in range(1, N_DEV):
            g = g + p_ref[s].astype(F32)
        d, nm, nv = _adamw(w_ref[...], g, m_ref[...], v_ref[...])
        g_ref[...] = g
        d_ref[...] = d
        nm_ref[...] = nm
        nv_ref[...] = nv

    tile = pl.BlockSpec((tr, C), lambda i: (i, 0))
    sh = jax.ShapeDtypeStruct((R, C), F32)
    return pl.pallas_call(
        body, name=name, out_shape=(sh, sh, sh, sh), grid=(R // tr,),
        in_specs=[tile, pl.BlockSpec((N_DEV, tr, C), lambda i: (0, i, 0)), tile, tile],
        out_specs=(tile, tile, tile, tile), compiler_params=_cparams())(w, parts, m, v)


def _pack(arrs, rows_mult=SUBLANES):
    flat = jnp.concatenate([a.reshape(-1).astype(F32) for a in arrs])
    n = flat.shape[0]
    per = rows_mult * LANES
    total = -(-n // per) * per
    return jnp.pad(flat, (0, total - n)).reshape(total // LANES, LANES)


def _unpack(pack, shapes):
    flat = pack.reshape(-1)
    out, off = [], 0
    for s in shapes:
        n = 1
        for d in s:
            n *= d
        out.append(flat[off:off + n].reshape(s))
        off += n
    return out


def kernel(x, meta_tokens, mix_norm_g, w_in, ssm_lambda_re, ssm_lambda_im, ssm_log_dt, ssm_b_re, ssm_b_im, ssm_c_re, ssm_c_im, ssm_d, ssm_w_glu, w_ssm_proj, hgrn_lb_logits, hgrn_norm_g, w_hgrn_proj, w_out, ffn_norm_g, w_up, conv_w, conv_b, w_down, final_norm_g, loss_target, m_meta_tokens, m_mix_norm_g, m_w_in, m_ssm_lambda_re, m_ssm_lambda_im, m_ssm_log_dt, m_ssm_b_re, m_ssm_b_im, m_ssm_c_re, m_ssm_c_im, m_ssm_d, m_ssm_w_glu, m_w_ssm_proj, m_hgrn_lb_logits, m_hgrn_norm_g, m_w_hgrn_proj, m_w_out, m_ffn_norm_g, m_w_up, m_conv_w, m_conv_b, m_w_down, m_final_norm_g, v_meta_tokens, v_mix_norm_g, v_w_in, v_ssm_lambda_re, v_ssm_lambda_im, v_ssm_log_dt, v_ssm_b_re, v_ssm_b_im, v_ssm_c_re, v_ssm_c_im, v_ssm_d, v_ssm_w_glu, v_w_ssm_proj, v_hgrn_lb_logits, v_hgrn_norm_g, v_w_hgrn_proj, v_w_out, v_ffn_norm_g, v_w_up, v_conv_w, v_conv_b, v_w_down, v_final_norm_g):
    args = dict(locals())
    B, S_len, D = x.shape
    L = S_len + N_META
    T = B * L
    tm = _tile(L, ROW_TILE_CAP)
    tps = L // tm
    G, P = ssm_lambda_re.shape[1:]
    H = ssm_b_re.shape[-1]
    W = G * H
    n_cb = W // LANES
    gpb = G // n_cb
    hd = hgrn_norm_g.shape[1]
    n_heads = D // hd
    n_in = w_in.shape[2]
    F = w_up.shape[2]
    assert W == D and n_in % LANES == 0

    me = (4 * lax.axis_index("x") + 2 * lax.axis_index("y") + lax.axis_index("c")).astype(jnp.int32).reshape(1)
    meta_g, cw_g = _exchange_call([meta_tokens, conv_w[0]], False, "gather_small_params")
    ga = _exchange_start([w_in[0].astype(MXU)], False, "gather_a_start")
    gb = _exchange_start(
        [w_up[0].astype(MXU), ssm_w_glu[0].astype(MXU), w_ssm_proj[0].astype(MXU), w_hgrn_proj[0].astype(MXU),
         w_out[0].astype(MXU), w_down[0].astype(MXU)], False, "gather_b_start")
    started_tok = (ga[4] + gb[4])[0:1, 0:1]
    meta_full = meta_g.transpose(1, 0, 2).reshape(N_META, D)
    cb_g = conv_b.reshape(N_DEV, 1, F)

    h0 = jnp.concatenate([jnp.broadcast_to(meta_full[None], (B, N_META, D)), x], axis=1).reshape(T, D)
    tgt = jnp.concatenate([jnp.zeros((B, N_META, D), F32), loss_target], axis=1).reshape(T, D)

    lr, li = ssm_lambda_re[0], ssm_lambda_im[0]
    ldt = ssm_log_dt[0].reshape(G, 1)
    bt_re = ssm_b_re[0].transpose(2, 0, 1).reshape(H, G * P)
    bt_im = ssm_b_im[0].transpose(2, 0, 1).reshape(H, G * P)
    seg = _seg_len(L)
    a_re, a_im, as_re, as_im, coef_re, coef_im = _small_call(
        _disc_a_power(seg), [lr, li, ldt], [((G, P), F32)] * 6, "s5_discretise")
    bbt_re, bbt_im = _small_call(
        _disc_b, [coef_re.reshape(1, G * P), coef_im.reshape(1, G * P), bt_re, bt_im],
        [((H, G * P), F32)] * 2, "s5_input_matrix")
    eye = jnp.eye(gpb, dtype=F32)
    hw = gpb * P

    def expand_b(bbt):
        t = bbt.reshape(H, n_cb, gpb, P).transpose(1, 0, 2, 3)[:, None]
        return (t * eye[None, :, None, :, None]).reshape(n_cb, gpb * H, hw)

    def expand_c(cm):
        t = cm.reshape(n_cb, gpb, H, P).transpose(0, 1, 3, 2)[:, :, :, None]
        return (t * eye[None, :, None, :, None]).reshape(n_cb, hw, gpb * H)

    wb = jnp.concatenate([expand_b(bbt_re), expand_b(bbt_im)], axis=2).astype(MXU)
    wc = jnp.concatenate([expand_c(ssm_c_re[0]), -expand_c(ssm_c_im[0])], axis=1).astype(MXU)
    tab = jnp.stack([jnp.concatenate([a_re.reshape(n_cb, hw), a_im.reshape(n_cb, hw)], axis=1),
                     jnp.concatenate([as_re.reshape(n_cb, hw), as_im.reshape(n_cb, hw)], axis=1)], axis=1)
    tab = jnp.broadcast_to(tab[:, :, None, :], (n_cb, 2, SUBLANES, 2 * hw))
    dsk = ssm_d.reshape(n_cb, 1, LANES)
    lb = _small_call(_lb_fn, [hgrn_lb_logits], [((1, D), F32)], "hgrn_lower_bound")[0]

    z1 = _norm_call(h0, mix_norm_g + started_tok, tm, "mix_norm")
    ga_src, ga_land = _exchange_wait(ga, z1, False, "gather_a_wait")
    win_g = _place_own_call(ga_src, ga_land, False, me, "gather_a_own")[0]
    p = _mm_shard(z1, win_g, tm, "in_proj", False)
    p3 = p.reshape(B, L, p.shape[1])
    u_seg = _to_segments(p3[:, :, :W], seg)
    ya_seg, s_all = _s5_fwd_call(u_seg, wb, wc, tab, dsk, "s5_fwd")
    ya = _from_segments(ya_seg, seg, L).reshape(T, W)
    gb_src, gb_land = _exchange_wait(gb, ya, False, "gather_b_wait")
    gathered = _place_own_call(gb_src, gb_land, False, me, "gather_b_own")
    wup_g = gathered[0]
    wglu_g, wsp_g, whp_g, wout_g = [g.reshape(D, D) for g in gathered[1:5]]
    wdn_g = gathered[5].reshape(N_DEV // 2, 2 * w_down.shape[1], D)
    yo, a_br = _glu_proj_call(ya, wglu_g, wsp_g, tm, "s5_glu_proj")
    yb = _hgrn_fwd_call(p3, lb, hgrn_norm_g, n_heads, n_cb, "hgrn_fwd").reshape(T, D)
    col_ga = 5
    h1, mg, bm, z2 = _merge_call(yb, a_br, p, h0, whp_g, wout_g, ffn_norm_g, col_ga, tm, "merge")
    up = _mm_shard(z2, wup_g, tm, "up_proj", True)
    act, dh2, loss_part, dg3 = _ffn_fwd_call(up, cw_g, cb_g, wdn_g, h1, tgt, final_norm_g.reshape(1, D),
                                             tm, tps, "ffn_out_loss")

    dua, dub, dwd, dcwa, dcwb, dcba, dcbb = _ffn_bwd_a_call(dh2, up, act, cw_g, cb_g, wdn_g, tm, tps, "ffn_bwd_gate")
    dupa, dupb, dh1, dg2 = _ffn_bwd_b_call(dua, dub, cw_g, wup_g, h1, ffn_norm_g, dh2, tm, tps, "ffn_bwd_up")
    dwup = jnp.concatenate([_mm_tn(z2, dupa, N_DEV // 2, tm, "dw_up_a", True),
                            _mm_tn(z2, dupb, N_DEV // 2, tm, "dw_up_b", True)], axis=0)
    sh_rows = D // N_DEV
    sa = _exchange_start([dwup.astype(WIRE), dwd.reshape(N_DEV, w_down.shape[1], D).astype(WIRE)], True,
                         "scatter_a_start")
    dmg, dwout = _lin_bwd(mg, dh1, wout_g + sa[4][0:1, 0:1].astype(MXU), tm, "out_proj_bwd")
    da_br, dbm, dga, dgb = _merge_bwd_call(dmg, a_br, bm, p, col_ga, tm, "merge_bwd")
    dyo, dwsp = _lin_bwd(yo, da_br, wsp_g, tm, "ssm_proj_bwd")
    dyb, dwhp = _lin_bwd(yb, dbm, whp_g, tm, "hgrn_proj_bwd")
    dya, dwglu = _glu_bwd_call(ya, dyo, wglu_g, tm, "s5_glu_bwd")
    sb = _exchange_start([t.reshape(N_DEV, sh_rows, D).astype(WIRE) for t in (dwglu, dwsp, dwhp, dwout)], True,
                         "scatter_b_start")
    tok_b = sb[4][0:1, :]
    du_seg, dwb, dwc, dab, ddsk = _s5_bwd_call(u_seg, s_all, _to_segments(dya.reshape(B, L, W), seg), wb, wc, tab,
                                               dsk + tok_b[None], "s5_bwd")
    du = _from_segments(du_seg, seg, L)

    def diag_b(dw):
        t = (dw.reshape(n_cb, gpb, H, gpb, P) * eye[None, :, None, :, None]).sum(axis=1)
        return t.transpose(1, 0, 2, 3).reshape(H, G * P)

    def diag_c(dw):
        t = (dw.reshape(n_cb, gpb, P, gpb, H) * eye[None, :, None, :, None]).sum(axis=3)
        return t.transpose(0, 1, 3, 2).reshape(G, H, P)

    early_parts = [dab[:, 0, :hw].reshape(G, P), dab[:, 0, hw:].reshape(G, P),
                   diag_b(dwb[:, :, :hw]), diag_b(dwb[:, :, hw:]),
                   diag_c(dwc[:, :hw]), -diag_c(dwc[:, hw:]), ddsk.reshape(1, D)]
    early_pack = _pack(early_parts)
    se = _exchange_start([early_pack], False, "gather_s5_grads_start")
    dq, dfl, di, dog, dlb, dng = _hgrn_bwd_call(p3, dyb.reshape(B, L, D), lb, hgrn_norm_g + tok_b + se[4][0:1, :],
                                                n_heads, n_cb, "hgrn_bwd")
    dp = jnp.concatenate([du.reshape(T, W), dq.reshape(T, D), dfl.reshape(T, D), di.reshape(T, D),
                          dog.reshape(T, D), dga, dgb], axis=1)
    dwin = _mm_tn(z1, dp, N_DEV, tm, "dw_in", False)
    sc = _exchange_start([dwin.astype(WIRE)], True, "scatter_c_start")
    dh0, dg1 = _in_bwd_call(dp, win_g, h0, mix_norm_g + sc[4][0:1, 0:1], dh1, tm, "in_proj_bwd")
    dh0_3 = dh0.reshape(B, L, D)
    grad_x = dh0_3[:, N_META:]
    dmeta = _meta_grad_call(dh0_3, "meta_grad")

    late_parts = [dg1, dlb, dng, dg2, jnp.concatenate([dcba, dcbb], axis=0).reshape(1, N_DEV * F), dg3, loss_part]
    late_pack = _pack(late_parts)

    dcw = jnp.concatenate([dcwa, dcwb], axis=0)
    dmeta_s = dmeta.reshape(N_META, N_DEV, D // N_DEV).transpose(1, 0, 2)
    parts_d = _exchange_call([dmeta_s, dcw], True, "scatter_small_grads")
    late_all = _exchange_call([late_pack], False, "gather_small_grads")[0]
    early_all = _place_own_call(*_exchange_wait(se, late_all, False, "gather_s5_grads_wait"), False, me,
                                "gather_s5_grads_own")[0]
    parts_a = _place_own_call(*_exchange_wait(sa, late_all, True, "scatter_a_wait"), True, me, "scatter_a_own")
    parts_b = _place_own_call(*_exchange_wait(sb, late_all, True, "scatter_b_wait"), True, me, "scatter_b_own")
    parts_c = _place_own_call(*_exchange_wait(sc, late_all, True, "scatter_c_wait"), True, me, "scatter_c_own")
    parts = [parts_c[0], parts_a[0], *parts_b, parts_a[1], parts_d[0], parts_d[1]]

    def sum8(a, b):
        ta, tb = a[0], b[0]
        for s in range(1, N_DEV):
            ta, tb = ta + a[s], tb + b[s]
        return ta, tb

    early_sum, late_sum = _small_call(sum8, [early_all, late_all], [(early_pack.shape, F32), (late_pack.shape, F32)],
                                      "sum_small_grads")
    t_abr, t_abi, t_bbr, t_bbi, g_cre, g_cim, g_dsk = _unpack(early_sum, [a.shape for a in early_parts])
    g_g1, t_lb, g_ng, g_g2, g_cb, g_g3, loss_v = _unpack(late_sum, [a.shape for a in late_parts])

    def disc_b_bwd(cr, ci, br, bi, dbr, dbi):
        _, vjp = jax.vjp(_disc_b, cr, ci, br, bi)
        return vjp((dbr, dbi))

    t_cr, t_ci, g_btr, g_bti = _small_call(
        disc_b_bwd, [coef_re.reshape(1, G * P), coef_im.reshape(1, G * P), bt_re, bt_im, t_bbr, t_bbi],
        [((1, G * P), F32)] * 2 + [((H, G * P), F32)] * 2, "s5_input_matrix_bwd")

    def disc_a_bwd(lr_, li_, ldt_, dar, dai, dcr, dci):
        _, vjp = jax.vjp(_disc_a, lr_, li_, ldt_)
        return vjp((dar, dai, dcr, dci))

    g_lr, g_li, g_ldt = _small_call(
        disc_a_bwd, [lr, li, ldt, t_abr, t_abi, t_cr.reshape(G, P), t_ci.reshape(G, P)],
        [((G, P), F32)] * 2 + [((G, 1), F32)], "s5_discretise_bwd")

    def lb_bwd(logits, d):
        _, vjp = jax.vjp(_lb_fn, logits)
        return vjp(d)

    g_lbl = _small_call(lb_bwd, [hgrn_lb_logits, t_lb], [(hgrn_lb_logits.shape, F32)], "hgrn_lower_bound_bwd")[0]

    grads = dict(
        mix_norm_g=g_g1, ssm_lambda_re=g_lr[None], ssm_lambda_im=g_li[None], ssm_log_dt=g_ldt.reshape(1, G),
        ssm_b_re=g_btr.reshape(H, G, P).transpose(1, 2, 0)[None], ssm_b_im=g_bti.reshape(H, G, P).transpose(1, 2, 0)[None],
        ssm_c_re=g_cre[None], ssm_c_im=g_cim[None], ssm_d=g_dsk, hgrn_lb_logits=g_lbl, hgrn_norm_g=g_ng,
        ffn_norm_g=g_g2, conv_b=g_cb.reshape(1, N_DEV * F), final_norm_g=g_g3.reshape(D))
    loss = loss_v[0, 0]

    delta, new_m, new_v = {}, {}, {}
    sharded = [("w_in", parts[0], (D, n_in)), ("w_up", parts[1], (D, F)), ("ssm_w_glu", parts[2], (sh_rows, D)),
               ("w_ssm_proj", parts[3], (sh_rows, D)), ("w_hgrn_proj", parts[4], (sh_rows, D)),
               ("w_out", parts[5], (sh_rows, D)), ("w_down", parts[6], (w_down.shape[1], D)),
               ("meta_tokens", parts[7], (N_META, D // N_DEV)), ("conv_w", parts[8], (3, F))]
    for name, part, shp in sharded:
        full = args[name].shape
        g, d_, nm, nv = _adamw_shard_call(args[name].reshape(shp), part, args["m_" + name].reshape(shp),
                                          args["v_" + name].reshape(shp), "adamw_" + name)
        grads[name], delta[name], new_m[name], new_v[name] = [t.reshape(full) for t in (g, d_, nm, nv)]

    rep = ["mix_norm_g", "ssm_lambda_re", "ssm_lambda_im", "ssm_log_dt", "ssm_b_re", "ssm_b_im", "ssm_c_re",
           "ssm_c_im", "ssm_d", "hgrn_lb_logits", "hgrn_norm_g", "ffn_norm_g", "conv_b", "final_norm_g"]
    rep_shapes = [args[n].shape for n in rep]
    packs = [_pack([args[pre + n] for n in rep]) for pre in ("", "m_", "v_")]
    g_pack = _pack([grads[n] for n in rep])
    outs = _small_call(lambda w, g, m, v: _adamw(w, g, m, v), [packs[0], g_pack, packs[1], packs[2]],
                       [(g_pack.shape, F32)] * 3, "adamw_replicated")
    for n, d_, nm, nv in zip(rep, *[_unpack(o, rep_shapes) for o in outs]):
        delta[n], new_m[n], new_v[n] = d_, nm, nv

    names = ["meta_tokens", "mix_norm_g", "w_in", "ssm_lambda_re", "ssm_lambda_im", "ssm_log_dt", "ssm_b_re",
             "ssm_b_im", "ssm_c_re", "ssm_c_im", "ssm_d", "ssm_w_glu", "w_ssm_proj", "hgrn_lb_logits", "hgrn_norm_g",
             "w_hgrn_proj", "w_out", "ffn_norm_g", "w_up", "conv_w", "conv_b", "w_down", "final_norm_g"]
    return (loss, grad_x, *[grads[n] for n in names], *[delta[n] for n in names],
            *[new_m[n] for n in names], *[new_v[n] for n in names])
```

```python
import functools

import jax
import jax.numpy as jnp
from jax import lax
from jax.experimental import pallas as pl
from jax.experimental.pallas import tpu as pltpu

F32 = jnp.float32
MXU = jnp.bfloat16
ACT = jnp.bfloat16
WIRE = jnp.bfloat16
N_DEV = 8
N_META = 16
CHUNK = 16
EPS = 1e-6
ADAM_LR, ADAM_B1, ADAM_B2, ADAM_EPS, ADAM_WD, ADAM_STEP = 0.001, 0.9, 0.999, 1e-08, 0.01, 10
SUBLANES = 8
LANES = 128
ROW_TILE_CAP = 700
VMEM_LIMIT = 60 * 1024 * 1024


def _cparams(**kw):
    return pltpu.CompilerParams(vmem_limit_bytes=VMEM_LIMIT, **kw)


def _tile(n, cap):
    best = None
    for t in range(16, min(n, cap) + 1, 16):
        if n % t == 0:
            best = t
    assert best is not None, (n, cap)
    return best


def _dot(a, b):
    return lax.dot_general(a.astype(MXU), b.astype(MXU), (((1,), (0,)), ((), ())), preferred_element_type=F32)


def _dot_nt(a, b):
    return lax.dot_general(a.astype(MXU), b.astype(MXU), (((1,), (1,)), ((), ())), preferred_element_type=F32)


def _dot_tn(a, b):
    return lax.dot_general(a.astype(MXU), b.astype(MXU), (((0,), (0,)), ((), ())), preferred_element_type=F32)


def _rms(x, g):
    return x * lax.rsqrt(jnp.mean(x * x, axis=-1, keepdims=True) + EPS) * g


def _silu(x):
    return x * jax.nn.sigmoid(x)


def _small_call(fn, ins, out_shapes, name):
    n_in = len(ins)

    def body(*refs):
        outs = fn(*[r[...] for r in refs[:n_in]])
        outs = outs if isinstance(outs, (tuple, list)) else (outs,)
        for r, o in zip(refs[n_in:], outs):
            r[...] = o.astype(r.dtype)

    vm = pl.BlockSpec(memory_space=pltpu.VMEM)
    return pl.pallas_call(
        body, name=name, out_shape=tuple(jax.ShapeDtypeStruct(s, d) for s, d in out_shapes),
        in_specs=[vm] * n_in, out_specs=tuple([vm] * len(out_shapes)), compiler_params=_cparams())(*ins)


def _disc_a(lr, li, ldt):
    dt = jnp.exp(ldt)
    mag = jnp.exp(lr * dt)
    ab_re = mag * jnp.cos(li * dt)
    ab_im = mag * jnp.sin(li * dt)
    den = lr * lr + li * li
    nr = ab_re - 1.0
    coef_re = (nr * lr + ab_im * li) / den
    coef_im = (ab_im * lr - nr * li) / den
    return ab_re, ab_im, coef_re, coef_im


def _disc_a_power(n):
    def fn(lr, li, ldt):
        ab_re, ab_im, coef_re, coef_im = _disc_a(lr, li, ldt)
        pr, pi, sr, si, m = None, None, ab_re, ab_im, n
        while m:
            if m & 1:
                pr, pi = (sr, si) if pr is None else (pr * sr - pi * si, pr * si + pi * sr)
            m >>= 1
            if m:
                sr, si = sr * sr - si * si, 2.0 * sr * si
        return ab_re, ab_im, pr, pi, coef_re, coef_im
    return fn


def _disc_b(coef_re, coef_im, bt_re, bt_im):
    return coef_re * bt_re - coef_im * bt_im, coef_re * bt_im + coef_im * bt_re


def _lb_fn(logits):
    return jax.nn.softmax(logits, axis=0)[0:1]


def _adamw(w, g, m, v):
    m = ADAM_B1 * m + (1.0 - ADAM_B1) * g
    v = ADAM_B2 * v + (1.0 - ADAM_B2) * jnp.square(g)
    m_hat = m / (1.0 - ADAM_B1 ** ADAM_STEP)
    v_hat = v / (1.0 - ADAM_B2 ** ADAM_STEP)
    delta = -ADAM_LR * (m_hat / (jnp.sqrt(v_hat) + ADAM_EPS) + ADAM_WD * w)
    return delta, m, v


def _norm_call(h, g, tm, name):
    T, D = h.shape

    def body(h_ref, g_ref, z_ref):
        z_ref[...] = _rms(h_ref[...], g_ref[...]).astype(ACT)

    return pl.pallas_call(
        body, name=name, out_shape=jax.ShapeDtypeStruct((T, D), ACT), grid=(T // tm,),
        in_specs=[pl.BlockSpec((tm, D), lambda i: (i, 0)), pl.BlockSpec((1, D), lambda i: (0, 0))],
        out_specs=pl.BlockSpec((tm, D), lambda i: (i, 0)), compiler_params=_cparams())(h, g)


def _mm_shard(x, w, tm, name, major):
    T, K = x.shape
    S, _, N = w.shape

    def body(x_ref, w_ref, o_ref):
        o_ref[...] = _dot(x_ref[...], w_ref[...]).astype(o_ref.dtype)

    if major:
        out_shape = jax.ShapeDtypeStruct((S, T, N), ACT)
        out_spec = pl.BlockSpec((None, tm, N), lambda j, i: (j, i, 0))
    else:
        out_shape = jax.ShapeDtypeStruct((T, S * N), ACT)
        out_spec = pl.BlockSpec((tm, N), lambda j, i: (i, j))
    return pl.pallas_call(
        body, name=name, out_shape=out_shape, grid=(S, T // tm),
        in_specs=[pl.BlockSpec((tm, K), lambda j, i: (i, 0)), pl.BlockSpec((None, K, N), lambda j, i: (j, 0, 0))],
        out_specs=out_spec, compiler_params=_cparams())(x, w)


def _mm_tn(x, y, n_shards, tm, name, major):
    T, K = x.shape
    S = n_shards
    N = y.shape[-1] if major else y.shape[-1] // S
    n_t = T // tm

    def body(x_ref, y_ref, o_ref, acc_ref):
        i = pl.program_id(1)
        part = _dot_tn(x_ref[...], y_ref[...])

        @pl.when(i == 0)
        def _():
            acc_ref[...] = part

        @pl.when(i > 0)
        def _():
            acc_ref[...] += part

        @pl.when(i == n_t - 1)
        def _():
            o_ref[...] = acc_ref[...].astype(WIRE)

    y_spec = (pl.BlockSpec((None, tm, N), lambda j, i: (j, i, 0)) if major
              else pl.BlockSpec((tm, N), lambda j, i: (i, j)))
    return pl.pallas_call(
        body, name=name, out_shape=jax.ShapeDtypeStruct((S, K, N), WIRE), grid=(S, n_t),
        in_specs=[pl.BlockSpec((tm, K), lambda j, i: (i, 0)), y_spec],
        out_specs=pl.BlockSpec((None, K, N), lambda j, i: (j, 0, 0)),
        scratch_shapes=[pltpu.VMEM((K, N), F32)], compiler_params=_cparams())(x, y)


def _lin_bwd(x, dy, w, tm, name):
    T, K = x.shape
    N = dy.shape[1]

    def body(x_ref, dy_ref, w_ref, dx_ref, dw_ref):
        @pl.when(pl.program_id(0) == 0)
        def _():
            dw_ref[...] = jnp.zeros_like(dw_ref)
        dy = dy_ref[...]
        dx_ref[...] = _dot_nt(dy, w_ref[...]).astype(dx_ref.dtype)
        dw_ref[...] += _dot_tn(x_ref[...], dy)

    return pl.pallas_call(
        body, name=name,
        out_shape=(jax.ShapeDtypeStruct((T, K), ACT), jax.ShapeDtypeStruct((K, N), F32)), grid=(T // tm,),
        in_specs=[pl.BlockSpec((tm, K), lambda i: (i, 0)), pl.BlockSpec((tm, N), lambda i: (i, 0)),
                  pl.BlockSpec((K, N), lambda i: (0, 0))],
        out_specs=(pl.BlockSpec((tm, K), lambda i: (i, 0)), pl.BlockSpec((K, N), lambda i: (0, 0))),
        compiler_params=_cparams())(x, dy, w)


N_SEG = SUBLANES


def _seg_len(L):
    assert L % N_SEG == 0
    return L // N_SEG


def _to_segments(a3, seg):
    b, length, c = a3.shape
    a = jnp.pad(a3, ((0, 0), (0, N_SEG * seg - length), (0, 0)))
    return a.reshape(b, N_SEG, seg, c).transpose(0, 2, 1, 3).reshape(b, N_SEG * seg, c)


def _from_segments(a3, seg, length):
    b, _, c = a3.shape
    return a3.reshape(b, seg, N_SEG, c).transpose(0, 2, 1, 3).reshape(b, N_SEG * seg, c)[:, :length]


def _seg_scan(x_ref, tab_ref, n_slabs, reverse):
    hw = x_ref.shape[1] // 2
    sign = -1.0 if reverse else 1.0
    ar, ai = tab_ref[0][:, :hw], sign * tab_ref[0][:, hw:]
    br, bi = tab_ref[1][:, :hw], sign * tab_ref[1][:, hw:]

    def slab(k):
        kk = (n_slabs - 1 - k) if reverse else k
        return pl.ds(pl.multiple_of(kk * SUBLANES, SUBLANES), SUBLANES)

    def horner(k, carry):
        cr, ci = carry
        x = x_ref[slab(k), :]
        return ar * cr - ai * ci + x[:, :hw], ar * ci + ai * cr + x[:, hw:]

    z = jnp.zeros((SUBLANES, hw), F32)
    fr, fi = lax.fori_loop(0, n_slabs, horner, (z, z))

    row = lax.broadcasted_iota(jnp.int32, (SUBLANES, hw), 0)
    edge = (row == SUBLANES - 1) if reverse else (row == 0)
    shift = SUBLANES - 1 if reverse else 1
    sr, si = z, z
    for _ in range(N_SEG - 1):
        er, ei = fr + br * sr - bi * si, fi + br * si + bi * sr
        sr = jnp.where(edge, 0.0, pltpu.roll(er, shift, 0))
        si = jnp.where(edge, 0.0, pltpu.roll(ei, shift, 0))

    def scan(k, carry):
        cr, ci = carry
        rows = slab(k)
        x = x_ref[rows, :]
        nr, ni = ar * cr - ai * ci + x[:, :hw], ar * ci + ai * cr + x[:, hw:]
        x_ref[rows, 0:hw] = nr
        x_ref[rows, hw:2 * hw] = ni
        return nr, ni

    lax.fori_loop(0, n_slabs, scan, (sr, si))


def _s5_fwd_call(p3, wb, wc, tab_f, dsk, name):
    B, L, _ = p3.shape
    n_cb, cw, sw = wb.shape

    def body(u_ref, wb_ref, wc_ref, tab_ref, d_ref, ya_ref, so_ref, s_ref):
        u = u_ref[...]
        s_ref[...] = _dot(u, wb_ref[...])
        _seg_scan(s_ref, tab_ref, L // SUBLANES, False)
        s = s_ref[...].astype(MXU)
        so_ref[...] = s
        y = _dot(s, wc_ref[...]) + d_ref[...] * u.astype(F32)
        ya_ref[...] = jax.nn.gelu(y).astype(ACT)

    return pl.pallas_call(
        body, name=name,
        out_shape=(jax.ShapeDtypeStruct((B, L, n_cb * cw), ACT), jax.ShapeDtypeStruct((B, n_cb, L, sw), MXU)),
        grid=(B, n_cb),
        in_specs=[pl.BlockSpec((None, L, cw), lambda b, c: (b, 0, c)),
                  pl.BlockSpec((None, cw, sw), lambda b, c: (c, 0, 0)),
                  pl.BlockSpec((None, sw, cw), lambda b, c: (c, 0, 0)),
                  pl.BlockSpec((None, 2, SUBLANES, sw), lambda b, c: (c, 0, 0, 0)),
                  pl.BlockSpec((None, 1, cw), lambda b, c: (c, 0, 0))],
        out_specs=(pl.BlockSpec((None, L, cw), lambda b, c: (b, 0, c)),
                   pl.BlockSpec((None, None, L, sw), lambda b, c: (b, c, 0, 0))),
        scratch_shapes=[pltpu.VMEM((L, sw), F32)], compiler_params=_cparams())(p3, wb, wc, tab_f, dsk)


def _s5_bwd_call(p3, s_all, dya, wb, wc, tab_r, dsk, name):
    B, L, _ = p3.shape
    n_cb, cw, sw = wb.shape
    hw = sw // 2
    n_slabs = L // SUBLANES

    def body(u_ref, si_ref, dya_ref, wb_ref, wc_ref, tr_ref, d_ref,
             du_ref, dwb_ref, dwc_ref, da_ref, dd_ref, s_ref, l_ref):
        @pl.when(pl.program_id(1) == 0)
        def _():
            dwb_ref[...] = jnp.zeros_like(dwb_ref)
            dwc_ref[...] = jnp.zeros_like(dwc_ref)
            da_ref[...] = jnp.zeros_like(da_ref)
            dd_ref[...] = jnp.zeros_like(dd_ref)

        u = u_ref[...]
        uf = u.astype(F32)
        s_in = si_ref[...]
        s_ref[...] = s_in.astype(F32)
        y = _dot(s_in, wc_ref[...]) + d_ref[...] * uf
        _, gelu_vjp = jax.vjp(jax.nn.gelu, y)
        dy = gelu_vjp(dya_ref[...].astype(F32))[0]
        dd_ref[...] += jnp.sum(dy * uf, axis=0, keepdims=True)
        l_ref[...] = _dot_nt(dy, wc_ref[...])
        _seg_scan(l_ref, tr_ref, n_slabs, True)
        du_ref[...] = (_dot_nt(l_ref[...], wb_ref[...]) + d_ref[...] * dy).astype(ACT)
        dwb_ref[...] += _dot_tn(u, l_ref[...])
        dwc_ref[...] += _dot_tn(s_in, dy)

        row = lax.broadcasted_iota(jnp.int32, (SUBLANES, hw), 0)
        last = s_ref[pl.ds((n_slabs - 1) * SUBLANES, SUBLANES), :]
        p0r = jnp.where(row == 0, 0.0, pltpu.roll(last[:, :hw], 1, 0))
        p0i = jnp.where(row == 0, 0.0, pltpu.roll(last[:, hw:], 1, 0))

        def step(k, carry):
            qr, qi, accr, acci = carry
            r0 = pl.multiple_of(k * SUBLANES, SUBLANES)
            s = s_ref[pl.ds(r0, SUBLANES), :]
            lam = l_ref[pl.ds(r0, SUBLANES), :]
            lr, li = lam[:, :hw], lam[:, hw:]
            accr = accr + lr * qr + li * qi
            acci = acci + li * qr - lr * qi
            return s[:, :hw], s[:, hw:], accr, acci

        z8 = jnp.zeros((SUBLANES, hw), F32)
        _, _, accr, acci = lax.fori_loop(0, n_slabs, step, (p0r, p0i, z8, z8))
        da_ref[...] += jnp.concatenate([jnp.sum(accr, axis=0, keepdims=True),
                                        jnp.sum(acci, axis=0, keepdims=True)], axis=1)

    W = n_cb * cw
    return pl.pallas_call(
        body, name=name,
        out_shape=(jax.ShapeDtypeStruct((B, L, W), ACT), jax.ShapeDtypeStruct((n_cb, cw, sw), F32),
                   jax.ShapeDtypeStruct((n_cb, sw, cw), F32), jax.ShapeDtypeStruct((n_cb, 1, sw), F32),
                   jax.ShapeDtypeStruct((n_cb, 1, cw), F32)),
        grid=(n_cb, B),
        in_specs=[pl.BlockSpec((None, L, cw), lambda c, b: (b, 0, c)),
                  pl.BlockSpec((None, None, L, sw), lambda c, b: (b, c, 0, 0)),
                  pl.BlockSpec((None, L, cw), lambda c, b: (b, 0, c)),
                  pl.BlockSpec((None, cw, sw), lambda c, b: (c, 0, 0)),
                  pl.BlockSpec((None, sw, cw), lambda c, b: (c, 0, 0)),
                  pl.BlockSpec((None, 2, SUBLANES, sw), lambda c, b: (c, 0, 0, 0)),
                  pl.BlockSpec((None, 1, cw), lambda c, b: (c, 0, 0))],
        out_specs=(pl.BlockSpec((None, L, cw), lambda c, b: (b, 0, c)),
                   pl.BlockSpec((None, cw, sw), lambda c, b: (c, 0, 0)),
                   pl.BlockSpec((None, sw, cw), lambda c, b: (c, 0, 0)),
                   pl.BlockSpec((None, 1, sw), lambda c, b: (c, 0, 0)),
                   pl.BlockSpec((None, 1, cw), lambda c, b: (c, 0, 0))),
        scratch_shapes=[pltpu.VMEM((L, sw), F32), pltpu.VMEM((L, sw), F32)],
        compiler_params=_cparams())(p3, s_all, dya, wb, wc, tab_r, dsk)


def _glu_proj_call(ya, wglu, wproj, tm, name):
    T, W = ya.shape
    D = wproj.shape[1]

    def body(ya_ref, wg_ref, wp_ref, yo_ref, a_ref):
        ya = ya_ref[...]
        yo = ya.astype(F32) * jax.nn.sigmoid(_dot(ya, wg_ref[...]))
        yo_ref[...] = yo.astype(ACT)
        a_ref[...] = _dot(yo, wp_ref[...]).astype(ACT)

    return pl.pallas_call(
        body, name=name, out_shape=(jax.ShapeDtypeStruct((T, W), ACT), jax.ShapeDtypeStruct((T, D), ACT)),
        grid=(T // tm,),
        in_specs=[pl.BlockSpec((tm, W), lambda i: (i, 0)), pl.BlockSpec((W, W), lambda i: (0, 0)),
                  pl.BlockSpec((W, D), lambda i: (0, 0))],
        out_specs=(pl.BlockSpec((tm, W), lambda i: (i, 0)), pl.BlockSpec((tm, D), lambda i: (i, 0))),
        compiler_params=_cparams())(ya, wglu, wproj)


def _glu_bwd_call(ya, dyo, wglu, tm, name):
    T, W = ya.shape

    def body(ya_ref, dyo_ref, wg_ref, dya_ref, dwg_ref):
        @pl.when(pl.program_id(0) == 0)
        def _():
            dwg_ref[...] = jnp.zeros_like(dwg_ref)
        ya = ya_ref[...]
        yaf = ya.astype(F32)
        dyo = dyo_ref[...].astype(F32)
        sg = jax.nn.sigmoid(_dot(ya, wg_ref[...]))
        dt = dyo * yaf * sg * (1.0 - sg)
        dya_ref[...] = (dyo * sg + _dot_nt(dt, wg_ref[...])).astype(ACT)
        dwg_ref[...] += _dot_tn(ya, dt)

    return pl.pallas_call(
        body, name=name, out_shape=(jax.ShapeDtypeStruct((T, W), ACT), jax.ShapeDtypeStruct((W, W), F32)),
        grid=(T // tm,),
        in_specs=[pl.BlockSpec((tm, W), lambda i: (i, 0)), pl.BlockSpec((tm, W), lambda i: (i, 0)),
                  pl.BlockSpec((W, W), lambda i: (0, 0))],
        out_specs=(pl.BlockSpec((tm, W), lambda i: (i, 0)), pl.BlockSpec((W, W), lambda i: (0, 0))),
        compiler_params=_cparams())(ya, dyo, wglu)


PAD = 16


def _chunk_cumsums(x, pad_ref, L):
    row = lax.broadcasted_iota(jnp.int32, x.shape, 0) % CHUNK
    zeros = jnp.zeros((PAD, x.shape[1]), F32)
    pad_ref[0:PAD, :] = zeros
    pad_ref[PAD + L:2 * PAD + L, :] = zeros
    c = x
    r = x
    d = 1
    while d < CHUNK:
        pad_ref[PAD:PAD + L, :] = c
        c = c + jnp.where(row >= d, pad_ref[PAD - d:PAD - d + L, :], 0.0)
        pad_ref[PAD:PAD + L, :] = r
        r = r + jnp.where(row + d < CHUNK, pad_ref[PAD + d:PAD + d + L, :], 0.0)
        d *= 2
    return c, r - x


def _hgrn_prep(q_ref, fl_ref, lb_ref, pad_ref, r0, n):
    rows = pl.ds(r0, n)
    lb = lb_ref[...]
    sig = jax.nn.sigmoid(fl_ref[rows, :].astype(F32))
    f = lb + (1.0 - lb) * sig
    k = 1.0 - f
    c, rc = _chunk_cumsums(jnp.log(f), pad_ref, n)
    e_in, e_inv, e_out = jnp.exp(c), jnp.exp(-c), jnp.exp(rc)
    q = q_ref[rows, :].astype(F32)
    return dict(sig=sig, f=f, k=k, q=q, e_in=e_in, e_inv=e_inv, e_out=e_out, dec=jnp.exp(c + rc))


def _for_row_blocks(L, fn):
    full = L // GROUP
    if full:
        def step(g, carry):
            fn(pl.multiple_of(g * GROUP, GROUP), GROUP)
            return carry
        lax.fori_loop(0, full, step, 0)
    if L % GROUP:
        fn(full * GROUP, L % GROUP)


def _chunk_mask(rb):
    r = lax.broadcasted_iota(jnp.int32, (rb, rb), 0)
    c = lax.broadcasted_iota(jnp.int32, (rb, rb), 1)
    return (r // CHUNK == c // CHUNK) & (c <= r)


def _hg_out(o, og, g):
    on = o * lax.rsqrt(jnp.mean(o * o, axis=-1, keepdims=True) + EPS) * g
    return on * _silu(og)


def _hgrn_specs(L, hd, col_q, n_heads, order):
    def spec(sec):
        return pl.BlockSpec((None, L, hd), lambda *g: (order(*g)[0], 0, col_q + sec * n_heads + order(*g)[1]))
    return [spec(0), spec(1), spec(2), spec(3)]


GROUP = 128
CPG = GROUP // CHUNK


def _expand(x):
    xf = x.astype(F32)
    chunk = lax.broadcasted_iota(jnp.int32, xf.shape, 0) // CHUNK
    return jnp.concatenate([jnp.where(chunk == j, xf, 0.0) for j in range(CPG)], axis=1)


def _fill_tail(refs_fills, L):
    for ref, fill in refs_fills:
        if ref.shape[0] > L:
            ref[L:ref.shape[0], :] = jnp.full((ref.shape[0] - L, ref.shape[1]), fill, ref.dtype)


GROUP_UNROLL = 4


def _hgrn_forward_core(q_ref, fl_ref, v_ref, lb_ref, pad_ref, qin_ref, kin_ref, kout_ref, vp_ref, dec_ref, o_ref,
                       s_ref, L):
    hd = qin_ref.shape[1]
    n_groups = qin_ref.shape[0] // GROUP

    def prep(r0, n):
        pp = _hgrn_prep(q_ref, fl_ref, lb_ref, pad_ref, r0, n)
        rows = pl.ds(r0, n)
        qin_ref[rows, :] = (pp["q"] * pp["e_in"]).astype(MXU)
        kin_ref[rows, :] = (pp["k"] * pp["e_inv"]).astype(MXU)
        kout_ref[rows, :] = (pp["k"] * pp["e_out"]).astype(MXU)
        vp_ref[rows, :] = v_ref[rows, :].astype(MXU)
        dec_ref[rows, :] = pp["dec"]

    _for_row_blocks(L, prep)
    _fill_tail(((qin_ref, 0.0), (kin_ref, 0.0), (kout_ref, 0.0), (vp_ref, 0.0), (dec_ref, 1.0)), L)
    mask = _chunk_mask(GROUP)

    def intra(g, carry):
        rows = pl.ds(pl.multiple_of(g * GROUP, GROUP), GROUP)
        a = jnp.where(mask, _dot_nt(qin_ref[rows, :], kin_ref[rows, :]), 0.0)
        o_ref[rows, :] = _dot(a, vp_ref[rows, :])
        kv = _dot_tn(vp_ref[rows, :], _expand(kout_ref[rows, :]))
        for j in range(CPG):
            s_ref[g * CPG + j] = kv[:, j * hd:(j + 1) * hd]
        return carry

    lax.fori_loop(0, n_groups, intra, 0, unroll=GROUP_UNROLL)

    def rec(n, st):
        kv = s_ref[n]
        s_ref[n] = st
        dec = dec_ref[pl.ds(pl.multiple_of(n * CHUNK, CHUNK), SUBLANES), :][0:1]
        return st * dec + kv

    lax.fori_loop(0, L // CHUNK, rec, jnp.zeros((hd, hd), F32))

    def inter(g, carry):
        rows = pl.ds(pl.multiple_of(g * GROUP, GROUP), GROUP)
        scat = jnp.concatenate([s_ref[g * CPG + j] for j in range(CPG)], axis=1)
        o_ref[rows, :] += _dot_nt(_expand(qin_ref[rows, :]), scat)
        return carry

    lax.fori_loop(0, n_groups, inter, 0, unroll=GROUP_UNROLL)


def _hgrn_scratch(L, hd):
    lp = -(-L // GROUP) * GROUP
    return lp, [pltpu.VMEM((GROUP + 2 * PAD, hd), F32), pltpu.VMEM((lp, hd), MXU), pltpu.VMEM((lp, hd), MXU),
                pltpu.VMEM((lp, hd), MXU), pltpu.VMEM((lp, hd), MXU), pltpu.VMEM((lp, hd), F32),
                pltpu.VMEM((lp, hd), F32), pltpu.VMEM((lp // CHUNK, hd, hd), F32)]


def _hgrn_fwd_call(p3, lb, ng, n_heads, col_q, name):
    B, L, _ = p3.shape
    hd = ng.shape[1]
    _, scratch = _hgrn_scratch(L, hd)

    def body(q_ref, fl_ref, v_ref, og_ref, lb_ref, ng_ref, yb_ref,
             pad_ref, qin_ref, kin_ref, kout_ref, vp_ref, dec_ref, o_ref, s_ref):
        _hgrn_forward_core(q_ref, fl_ref, v_ref, lb_ref, pad_ref, qin_ref, kin_ref, kout_ref, vp_ref, dec_ref,
                           o_ref, s_ref, L)

        def out(r0, n):
            rows = pl.ds(r0, n)
            yb_ref[rows, :] = _hg_out(o_ref[rows, :], og_ref[rows, :].astype(F32), ng_ref[...]).astype(ACT)

        _for_row_blocks(L, out)

    order = lambda b, h: (b, h)
    return pl.pallas_call(
        body, name=name, out_shape=jax.ShapeDtypeStruct((B, L, n_heads * hd), ACT), grid=(B, n_heads),
        in_specs=_hgrn_specs(L, hd, col_q, n_heads, order) + [
            pl.BlockSpec((1, hd), lambda b, h: (0, h)), pl.BlockSpec((1, hd), lambda b, h: (0, 0))],
        out_specs=pl.BlockSpec((None, L, hd), lambda b, h: (b, 0, h)),
        scratch_shapes=scratch, compiler_params=_cparams())(p3, p3, p3, p3, lb, ng)


def _hgrn_bwd_call(p3, dyb, lb, ng, n_heads, col_q, name):
    B, L, _ = p3.shape
    hd = ng.shape[1]
    n_chunks = L // CHUNK
    lp, scratch = _hgrn_scratch(L, hd)
    n_groups = lp // GROUP

    def body(q_ref, fl_ref, v_ref, og_ref, dyb_ref, lb_ref, ng_ref,
             dq_ref, dfl_ref, dv_ref, dog_ref, dlb_ref, dng_ref,
             pad_ref, qin_ref, kin_ref, kout_ref, vp_ref, dec_ref, o_ref, s_ref,
             do_ref, ds_ref, dqi_ref, dki_ref, dko_ref, dvv_ref, dct_ref):
        @pl.when(pl.program_id(1) == 0)
        def _():
            dlb_ref[...] = jnp.zeros_like(dlb_ref)

        @pl.when((pl.program_id(0) == 0) & (pl.program_id(1) == 0))
        def _():
            dng_ref[...] = jnp.zeros_like(dng_ref)

        _hgrn_forward_core(q_ref, fl_ref, v_ref, lb_ref, pad_ref, qin_ref, kin_ref, kout_ref, vp_ref, dec_ref,
                           o_ref, s_ref, L)

        def out_bwd(r0, n):
            rows = pl.ds(r0, n)
            _, out_vjp = jax.vjp(_hg_out, o_ref[rows, :], og_ref[rows, :].astype(F32), ng_ref[...])
            d_o, d_og, d_ng = out_vjp(dyb_ref[rows, :].astype(F32))
            dog_ref[rows, :] = d_og.astype(ACT)
            dng_ref[...] += d_ng
            do_ref[rows, :] = d_o.astype(MXU)

        _for_row_blocks(L, out_bwd)
        _fill_tail(((do_ref, 0.0),), L)
        mask = _chunk_mask(GROUP)

        def grads_a(g, carry):
            rows = pl.ds(pl.multiple_of(g * GROUP, GROUP), GROUP)
            qi, ki, vv, do = qin_ref[rows, :], kin_ref[rows, :], vp_ref[rows, :], do_ref[rows, :]
            a = jnp.where(mask, _dot_nt(qi, ki), 0.0)
            da = jnp.where(mask, _dot_nt(do, vv), 0.0)
            sstack = s_ref[pl.ds(g * CPG, CPG)].reshape(CPG * hd, hd)
            dqi_ref[rows, :] = _dot(da, ki) + _dot(_expand(do), sstack)
            dki_ref[rows, :] = _dot_tn(da, qi)
            dvv_ref[rows, :] = _dot_tn(a, do)
            x = _dot_tn(do, _expand(qi))
            for j in range(CPG):
                ds_ref[g * CPG + j] = x[:, j * hd:(j + 1) * hd]
            return carry

        lax.fori_loop(0, n_groups, grads_a, 0, unroll=GROUP_UNROLL)

        def rec_bwd(k, dst):
            n = n_chunks - 1 - k
            r0 = pl.multiple_of(n * CHUNK, CHUNK)
            x = ds_ref[n]
            ds_ref[n] = dst
            dec = dec_ref[pl.ds(r0, SUBLANES), :][0:1]
            return dst * dec + x

        lax.fori_loop(0, n_chunks, rec_bwd, jnp.zeros((hd, hd), F32))

        def grads_b(g, carry):
            r0 = pl.multiple_of(g * GROUP, GROUP)
            rows = pl.ds(r0, GROUP)
            ds = [ds_ref[g * CPG + j] for j in range(CPG)]
            dscat = jnp.concatenate(ds, axis=1)
            dvv_ref[rows, :] += _dot_nt(_expand(kout_ref[rows, :]), dscat)
            dstack = ds_ref[pl.ds(g * CPG, CPG)].reshape(CPG * hd, hd)
            dko_ref[rows, :] = _dot(_expand(vp_ref[rows, :]), dstack)
            for j in range(CPG):
                dec = dec_ref[pl.ds(r0 + j * CHUNK, SUBLANES), :][0:1]
                ddec = dec * jnp.sum(ds[j] * s_ref[g * CPG + j], axis=0, keepdims=True)
                dct_ref[pl.ds(r0 + j * CHUNK, CHUNK), :] = jnp.broadcast_to(ddec, (CHUNK, hd))
            return carry

        lax.fori_loop(0, n_groups, grads_b, 0, unroll=GROUP_UNROLL)

        def finish(r0, n):
            rows = pl.ds(r0, n)
            pp = _hgrn_prep(q_ref, fl_ref, lb_ref, pad_ref, r0, n)
            dqi, dki, dko = dqi_ref[rows, :], dki_ref[rows, :], dko_ref[rows, :]
            dq = dqi * pp["e_in"]
            dk = dki * pp["e_inv"] + dko * pp["e_out"]
            dq_ref[rows, :] = dq.astype(ACT)
            dv_ref[rows, :] = dvv_ref[rows, :].astype(ACT)
            t_out = pp["k"] * pp["e_out"] * dko
            dc = pp["q"] * pp["e_in"] * dqi - pp["k"] * pp["e_inv"] * dki - t_out
            _, dc_later = _chunk_cumsums(dc, pad_ref, n)
            t_incl, t_later = _chunk_cumsums(t_out, pad_ref, n)
            dlogf = dc + dc_later + t_incl + t_later + dct_ref[rows, :]
            df = dlogf / pp["f"] - dk
            sig = pp["sig"]
            dfl_ref[rows, :] = (df * (1.0 - lb_ref[...]) * sig * (1.0 - sig)).astype(ACT)
            dlb_ref[...] += jnp.sum(df * (1.0 - sig), axis=0, keepdims=True)

        _for_row_blocks(L, finish)

    order = lambda h, b: (b, h)
    W = n_heads * hd
    act_out = jax.ShapeDtypeStruct((B, L, W), ACT)
    blk_out = pl.BlockSpec((None, L, hd), lambda h, b: (b, 0, h))
    return pl.pallas_call(
        body, name=name,
        out_shape=(act_out, act_out, act_out, act_out, jax.ShapeDtypeStruct((1, W), F32),
                   jax.ShapeDtypeStruct((1, hd), F32)),
        grid=(n_heads, B),
        in_specs=_hgrn_specs(L, hd, col_q, n_heads, order) + [
            pl.BlockSpec((None, L, hd), lambda h, b: (b, 0, h)),
            pl.BlockSpec((1, hd), lambda h, b: (0, h)), pl.BlockSpec((1, hd), lambda h, b: (0, 0))],
        out_specs=(blk_out, blk_out, blk_out, blk_out, pl.BlockSpec((1, hd), lambda h, b: (0, h)),
                   pl.BlockSpec((1, hd), lambda h, b: (0, 0))),
        scratch_shapes=scratch + [
            pltpu.VMEM((lp, hd), MXU), pltpu.VMEM((lp // CHUNK, hd, hd), F32)] + [pltpu.VMEM((lp, hd), F32)] * 5,
        compiler_params=_cparams())(p3, p3, p3, p3, dyb, lb, ng)


def _merge_fn(a, bm, ga, gb):
    return jax.nn.sigmoid(ga) * a + jax.nn.sigmoid(gb) * bm


def _merge_call(yb, a, p, h0, whp, wout, g2, col_ga, tm, name):
    T, D = h0.shape

    def body(yb_ref, a_ref, ga_ref, gb_ref, h0_ref, whp_ref, wout_ref, g2_ref, h1_ref, mg_ref, bm_ref, z2_ref):
        bm = _dot(yb_ref[...], whp_ref[...])
        mg = _merge_fn(a_ref[...].astype(F32), bm, ga_ref[...].astype(F32), gb_ref[...].astype(F32))
        h1 = h0_ref[...] + _dot(mg, wout_ref[...])
        h1_ref[...] = h1
        mg_ref[...] = mg.astype(ACT)
        bm_ref[...] = bm.astype(ACT)
        z2_ref[...] = _rms(h1, g2_ref[...]).astype(ACT)

    tile = pl.BlockSpec((tm, D), lambda i: (i, 0))
    full = pl.BlockSpec((D, D), lambda i: (0, 0))
    act = jax.ShapeDtypeStruct((T, D), ACT)
    return pl.pallas_call(
        body, name=name, out_shape=(jax.ShapeDtypeStruct((T, D), F32), act, act, act), grid=(T // tm,),
        in_specs=[tile, tile, pl.BlockSpec((tm, D), lambda i: (i, col_ga)),
                  pl.BlockSpec((tm, D), lambda i: (i, col_ga + 1)), tile, full, full,
                  pl.BlockSpec((1, D), lambda i: (0, 0))],
        out_specs=(tile, tile, tile, tile), compiler_params=_cparams())(yb, a, p, p, h0, whp, wout, g2)


def _merge_bwd_call(dmg, a, bm, p, col_ga, tm, name):
    T, D = dmg.shape

    def body(dmg_ref, a_ref, bm_ref, ga_ref, gb_ref, da_ref, dbm_ref, dga_ref, dgb_ref):
        args = [r[...].astype(F32) for r in (a_ref, bm_ref, ga_ref, gb_ref)]
        _, vjp = jax.vjp(_merge_fn, *args)
        for r, o in zip((da_ref, dbm_ref, dga_ref, dgb_ref), vjp(dmg_ref[...].astype(F32))):
            r[...] = o.astype(ACT)

    tile = pl.BlockSpec((tm, D), lambda i: (i, 0))
    act = jax.ShapeDtypeStruct((T, D), ACT)
    return pl.pallas_call(
        body, name=name, out_shape=(act, act, act, act), grid=(T // tm,),
        in_specs=[tile, tile, tile, pl.BlockSpec((tm, D), lambda i: (i, col_ga)),
                  pl.BlockSpec((tm, D), lambda i: (i, col_ga + 1))],
        out_specs=(tile, tile, tile, tile), compiler_params=_cparams())(dmg, a, bm, p, p)


def _conv_taps(x_ref, halo_ref, ext_ref, edge, tm, before):
    halo = jnp.where(edge, 0.0, halo_ref[...].astype(F32))
    x = x_ref[...].astype(F32)
    if before:
        ext_ref[0:PAD, :] = halo
        ext_ref[PAD:PAD + tm, :] = x
        return [ext_ref[PAD - 2 + k:PAD - 2 + k + tm, :] for k in range(3)]
    ext_ref[0:tm, :] = x
    ext_ref[tm:tm + PAD, :] = halo
    return [ext_ref[k:k + tm, :] for k in range(3)]


def _conv(taps, cw, cb):
    return cb + cw[0:1] * taps[0] + cw[1:2] * taps[1] + cw[2:3] * taps[2]


def _ffn_pair_specs(tm, F, T, n_pairs, order, before):
    hb = tm // PAD
    last = T // PAD - 1

    def halo_row(i):
        return jnp.maximum(i * hb - 1, 0) if before else jnp.minimum((i + 1) * hb, last)

    specs = []
    for off in (0, n_pairs):
        specs.append(pl.BlockSpec((None, tm, F), lambda *g, off=off: (order(*g)[1] + off, order(*g)[0], 0)))
        specs.append(pl.BlockSpec((None, PAD, F), lambda *g, off=off: (order(*g)[1] + off, halo_row(order(*g)[0]), 0)))
    return specs


def _ffn_fwd_call(up, cw, cb, wd, h1, tgt, g3, tm, tps, name):
    S, T, F = up.shape
    n_pairs = S // 2
    D = h1.shape[1]

    def body(ua_ref, ha_ref, ub_ref, hb_ref, cwa_ref, cwb_ref, cba_ref, cbb_ref, wd_ref, h1_ref, tgt_ref, g3_ref,
             act_ref, dh2_ref, loss_ref, dg3_ref, acc_ref, ext_ref):
        i, j = pl.program_id(0), pl.program_id(1)
        edge = (i % tps) == 0
        ua = _conv(_conv_taps(ua_ref, ha_ref, ext_ref, edge, tm, True), cwa_ref[...], cba_ref[...])
        ub = _conv(_conv_taps(ub_ref, hb_ref, ext_ref, edge, tm, True), cwb_ref[...], cbb_ref[...])
        act = _silu(ua) * ub
        act_ref[...] = act.astype(ACT)
        contrib = _dot(act, wd_ref[...])

        @pl.when(j == 0)
        def _():
            acc_ref[...] = h1_ref[...] + contrib

        @pl.when(j > 0)
        def _():
            acc_ref[...] += contrib

        @pl.when((i == 0) & (j == 0))
        def _():
            loss_ref[...] = jnp.zeros_like(loss_ref)
            dg3_ref[...] = jnp.zeros_like(dg3_ref)

        @pl.when(j == n_pairs - 1)
        def _():
            row = lax.broadcasted_iota(jnp.int32, (tm, 1), 0) + (i % tps) * tm
            valid = row >= N_META
            tgt = tgt_ref[...]

            def loss_fn(h2, g):
                err = _rms(h2, g) - tgt
                return 0.5 * jnp.sum(jnp.where(valid, err * err, 0.0)) / D

            loss, vjp = jax.vjp(loss_fn, acc_ref[...], g3_ref[...])
            dh2, dg3 = vjp(jnp.ones((), F32))
            dh2_ref[...] = dh2
            loss_ref[...] += loss
            dg3_ref[...] += dg3

    order = lambda i, j: (i, j)
    tile = pl.BlockSpec((tm, D), lambda i, j: (i, 0))
    vec = pl.BlockSpec((1, D), lambda i, j: (0, 0))
    return pl.pallas_call(
        body, name=name,
        out_shape=(jax.ShapeDtypeStruct((n_pairs, T, F), ACT), jax.ShapeDtypeStruct((T, D), F32),
                   jax.ShapeDtypeStruct((1, LANES), F32), jax.ShapeDtypeStruct((1, D), F32)),
        grid=(T // tm, n_pairs),
        in_specs=_ffn_pair_specs(tm, F, T, n_pairs, order, True) + [
            pl.BlockSpec((None, 3, F), lambda i, j: (j, 0, 0)), pl.BlockSpec((None, 3, F), lambda i, j: (j + n_pairs, 0, 0)),
            pl.BlockSpec((None, 1, F), lambda i, j: (j, 0, 0)), pl.BlockSpec((None, 1, F), lambda i, j: (j + n_pairs, 0, 0)),
            pl.BlockSpec((None, F, D), lambda i, j: (j, 0, 0)), tile, tile, vec],
        out_specs=(pl.BlockSpec((None, tm, F), lambda i, j: (j, i, 0)), tile,
                   pl.BlockSpec((1, LANES), lambda i, j: (0, 0)), vec),
        scratch_shapes=[pltpu.VMEM((tm, D), F32), pltpu.VMEM((tm + PAD, F), F32)],
        compiler_params=_cparams())(up, up, up, up, cw, cw, cb, cb, wd, h1, tgt, g3)


def _ffn_bwd_a_call(dh2, up, act, cw, cb, wd, tm, tps, name):
    S, T, F = up.shape
    n_pairs = S // 2
    D = dh2.shape[1]

    def body(dh2_ref, ua_ref, ha_ref, ub_ref, hb_ref, act_ref, cwa_ref, cwb_ref, cba_ref, cbb_ref, wd_ref,
             dua_ref, dub_ref, dwd_ref, dcwa_ref, dcwb_ref, dcba_ref, dcbb_ref, ext_ref):
        i = pl.program_id(1)
        edge = (i % tps) == 0

        @pl.when(i == 0)
        def _():
            for r in (dwd_ref, dcwa_ref, dcwb_ref, dcba_ref, dcbb_ref):
                r[...] = jnp.zeros_like(r)

        dh2 = dh2_ref[...]
        dact = _dot_nt(dh2, wd_ref[...])
        dwd_ref[...] += _dot_tn(act_ref[...], dh2)
        taps_a = _conv_taps(ua_ref, ha_ref, ext_ref, edge, tm, True)
        ua = _conv(taps_a, cwa_ref[...], cba_ref[...])
        sa = jax.nn.sigmoid(ua)
        dub = dact * ua * sa
        dcbb_ref[...] += jnp.sum(dub, axis=0, keepdims=True)
        taps_b = _conv_taps(ub_ref, hb_ref, ext_ref, edge, tm, True)
        dcwb_ref[...] += jnp.concatenate([jnp.sum(dub * t, axis=0, keepdims=True) for t in taps_b], axis=0)
        ub = _conv(taps_b, cwb_ref[...], cbb_ref[...])
        dua = dact * ub * sa * (1.0 + ua * (1.0 - sa))
        dcba_ref[...] += jnp.sum(dua, axis=0, keepdims=True)
        taps_a = _conv_taps(ua_ref, ha_ref, ext_ref, edge, tm, True)
        dcwa_ref[...] += jnp.concatenate([jnp.sum(dua * t, axis=0, keepdims=True) for t in taps_a], axis=0)
        dua_ref[...] = dua.astype(ACT)
        dub_ref[...] = dub.astype(ACT)

    order = lambda j, i: (i, j)
    sh = lambda rows: jax.ShapeDtypeStruct((n_pairs, rows, F), F32)
    par = lambda rows: pl.BlockSpec((None, rows, F), lambda j, i: (j, 0, 0))
    return pl.pallas_call(
        body, name=name,
        out_shape=(jax.ShapeDtypeStruct((n_pairs, T, F), ACT), jax.ShapeDtypeStruct((n_pairs, T, F), ACT),
                   jax.ShapeDtypeStruct((n_pairs, F, D), F32), sh(3), sh(3), sh(1), sh(1)),
        grid=(n_pairs, T // tm),
        in_specs=[pl.BlockSpec((tm, D), lambda j, i: (i, 0))] + _ffn_pair_specs(tm, F, T, n_pairs, order, True) + [
            pl.BlockSpec((None, tm, F), lambda j, i: (j, i, 0)),
            pl.BlockSpec((None, 3, F), lambda j, i: (j, 0, 0)), pl.BlockSpec((None, 3, F), lambda j, i: (j + n_pairs, 0, 0)),
            pl.BlockSpec((None, 1, F), lambda j, i: (j, 0, 0)), pl.BlockSpec((None, 1, F), lambda j, i: (j + n_pairs, 0, 0)),
            pl.BlockSpec((None, F, D), lambda j, i: (j, 0, 0))],
        out_specs=(pl.BlockSpec((None, tm, F), lambda j, i: (j, i, 0)), pl.BlockSpec((None, tm, F), lambda j, i: (j, i, 0)),
                   pl.BlockSpec((None, F, D), lambda j, i: (j, 0, 0)), par(3), par(3), par(1), par(1)),
        scratch_shapes=[pltpu.VMEM((tm + PAD, F), F32)],
        compiler_params=_cparams())(dh2, up, up, up, up, act, cw, cw, cb, cb, wd)


def _ffn_bwd_b_call(dua, dub, cw, wup, h1, g2, dh2, tm, tps, name):
    n_pairs, T, F = dua.shape
    D = h1.shape[1]
    hb = tm // PAD
    last = T // PAD - 1

    def body(da_ref, na_ref, db_ref, nb_ref, cwa_ref, cwb_ref, wa_ref, wb_ref, h1_ref, g2_ref, dh2_ref,
             dupa_ref, dupb_ref, dh1_ref, dg2_ref, acc_ref, ext_ref):
        i, j = pl.program_id(0), pl.program_id(1)
        edge = (i % tps) == tps - 1
        outs = []
        for d_ref, n_ref, cw_ref, o_ref in ((da_ref, na_ref, cwa_ref, dupa_ref), (db_ref, nb_ref, cwb_ref, dupb_ref)):
            t = _conv_taps(d_ref, n_ref, ext_ref, edge, tm, False)
            cwv = cw_ref[...]
            dup = cwv[2:3] * t[0] + cwv[1:2] * t[1] + cwv[0:1] * t[2]
            o_ref[...] = dup.astype(ACT)
            outs.append(dup)
        contrib = _dot_nt(outs[0], wa_ref[...]) + _dot_nt(outs[1], wb_ref[...])

        @pl.when(j == 0)
        def _():
            acc_ref[...] = contrib

        @pl.when(j > 0)
        def _():
            acc_ref[...] += contrib

        @pl.when((i == 0) & (j == 0))
        def _():
            dg2_ref[...] = jnp.zeros_like(dg2_ref)

        @pl.when(j == n_pairs - 1)
        def _():
            _, vjp = jax.vjp(_rms, h1_ref[...], g2_ref[...])
            dh, dg = vjp(acc_ref[...])
            dh1_ref[...] = dh2_ref[...] + dh
            dg2_ref[...] += dg

    tile = pl.BlockSpec((tm, D), lambda i, j: (i, 0))
    vec = pl.BlockSpec((1, D), lambda i, j: (0, 0))
    pair = lambda: [pl.BlockSpec((None, tm, F), lambda i, j: (j, i, 0)),
                    pl.BlockSpec((None, PAD, F), lambda i, j: (j, jnp.minimum((i + 1) * hb, last), 0))]
    act = jax.ShapeDtypeStruct((n_pairs, T, F), ACT)
    return pl.pallas_call(
        body, name=name,
        out_shape=(act, act, jax.ShapeDtypeStruct((T, D), F32), jax.ShapeDtypeStruct((1, D), F32)),
        grid=(T // tm, n_pairs),
        in_specs=pair() + pair() + [
            pl.BlockSpec((None, 3, F), lambda i, j: (j, 0, 0)), pl.BlockSpec((None, 3, F), lambda i, j: (j + n_pairs, 0, 0)),
            pl.BlockSpec((None, D, F), lambda i, j: (j, 0, 0)), pl.BlockSpec((None, D, F), lambda i, j: (j + n_pairs, 0, 0)),
            tile, vec, tile],
        out_specs=(pl.BlockSpec((None, tm, F), lambda i, j: (j, i, 0)), pl.BlockSpec((None, tm, F), lambda i, j: (j, i, 0)),
                   tile, vec),
        scratch_shapes=[pltpu.VMEM((tm, D), F32), pltpu.VMEM((tm + PAD, F), F32)],
        compiler_params=_cparams())(dua, dua, dub, dub, cw, cw, wup, wup, h1, g2, dh2)


def _in_bwd_call(dp, w_in, h0, g1, dh1, tm, name):
    T, D = h0.shape
    S, _, N = w_in.shape

    def body(dp_ref, w_ref, h0_ref, g1_ref, dh1_ref, dh0_ref, dg1_ref, acc_ref):
        i, j = pl.program_id(0), pl.program_id(1)
        contrib = _dot_nt(dp_ref[...], w_ref[...])

        @pl.when(j == 0)
        def _():
            acc_ref[...] = contrib

        @pl.when(j > 0)
        def _():
            acc_ref[...] += contrib

        @pl.when((i == 0) & (j == 0))
        def _():
            dg1_ref[...] = jnp.zeros_like(dg1_ref)

        @pl.when(j == S - 1)
        def _():
            _, vjp = jax.vjp(_rms, h0_ref[...], g1_ref[...])
            dh, dg = vjp(acc_ref[...])
            dh0_ref[...] = dh1_ref[...] + dh
            dg1_ref[...] += dg

    tile = pl.BlockSpec((tm, D), lambda i, j: (i, 0))
    vec = pl.BlockSpec((1, D), lambda i, j: (0, 0))
    return pl.pallas_call(
        body, name=name, out_shape=(jax.ShapeDtypeStruct((T, D), F32), jax.ShapeDtypeStruct((1, D), F32)),
        grid=(T // tm, S),
        in_specs=[pl.BlockSpec((tm, N), lambda i, j: (i, j)), pl.BlockSpec((None, D, N), lambda i, j: (j, 0, 0)),
                  tile, vec, tile],
        out_specs=(tile, vec), scratch_shapes=[pltpu.VMEM((tm, D), F32)],
        compiler_params=_cparams())(dp, w_in, h0, g1, dh1)


def _meta_grad_call(dh0_3, name):
    B, L, D = dh0_3.shape

    def body(d_ref, o_ref):
        o_ref[...] = jnp.sum(d_ref[...], axis=0)

    return pl.pallas_call(
        body, name=name, out_shape=jax.ShapeDtypeStruct((N_META, D), F32), grid=(1,),
        in_specs=[pl.BlockSpec((B, N_META, D), lambda i: (0, 0, 0))],
        out_specs=pl.BlockSpec((N_META, D), lambda i: (0, 0)), compiler_params=_cparams())(dh0_3)


_RELS = [(dx, dy, dc) for dx in (0, 1) for dy in (0, 1) for dc in (0, 1)][1:]


def _exchange_call(arrs, scatter, name):
    n = len(arrs)
    n_rel = len(_RELS)

    def body(*refs):
        ins, outs = refs[:n], refs[n:2 * n]
        send_sems, recv_sems, loc_sems = refs[2 * n:]
        x, y, c = lax.axis_index("x"), lax.axis_index("y"), lax.axis_index("c")
        me = 4 * x + 2 * y + c
        started = []
        for k in range(n):
            src_me = ins[k].at[me] if scatter else ins[k]
            loc = pltpu.make_async_copy(src_me, outs[k].at[me], loc_sems.at[k])
            loc.start()
            started.append(loc)
        waits = []
        for r, (dx, dy, dc) in enumerate(_RELS):
            px, py, pc = (x + dx) % 2, (y + dy) % 2, (c + dc) % 2
            pid = 4 * px + 2 * py + pc
            for k in range(n):
                s = k * n_rel + r
                src = ins[k].at[pid] if scatter else ins[k]
                cp = pltpu.make_async_remote_copy(
                    src_ref=src, dst_ref=outs[k].at[me], send_sem=send_sems.at[s], recv_sem=recv_sems.at[s],
                    device_id=(px, py, pc), device_id_type=pl.DeviceIdType.MESH)
                cp.start()
                waits.append(pltpu.make_async_remote_copy(
                    src_ref=src, dst_ref=outs[k].at[pid], send_sem=send_sems.at[s], recv_sem=recv_sems.at[s],
                    device_id=(px, py, pc), device_id_type=pl.DeviceIdType.MESH))
        for w in waits:
            w.wait_send()
            w.wait_recv()
        for loc in started:
            loc.wait()

    out_shape = tuple(jax.ShapeDtypeStruct(a.shape if scatter else (N_DEV,) + a.shape, a.dtype) for a in arrs)
    hbm = pl.BlockSpec(memory_space=pl.ANY)
    return pl.pallas_call(
        body, name=name, out_shape=out_shape, in_specs=[hbm] * n, out_specs=tuple([hbm] * n),
        scratch_shapes=[pltpu.SemaphoreType.DMA((n * n_rel,)), pltpu.SemaphoreType.DMA((n * n_rel,)),
                        pltpu.SemaphoreType.DMA((n,))],
        compiler_params=pltpu.CompilerParams(has_side_effects=True))(*arrs)


_HBM = pl.BlockSpec(memory_space=pltpu.HBM)
_SEM = pl.BlockSpec(memory_space=pltpu.SEMAPHORE)
_DATAFLOW = pltpu.SideEffectType.DATAFLOW_SIDE_EFFECTING


def _peer_copies(ins, lands, send_sems, recv_sems, scatter):
    n = len(ins)
    x, y, c = lax.axis_index("x"), lax.axis_index("y"), lax.axis_index("c")
    me = 4 * x + 2 * y + c
    sends, arrivals = [], []
    for r, (dx, dy, dc) in enumerate(_RELS):
        px, py, pc = (x + dx) % 2, (y + dy) % 2, (c + dc) % 2
        pid = 4 * px + 2 * py + pc
        for k in range(n):
            s = k * len(_RELS) + r
            src = ins[k].at[pid] if scatter else ins[k]
            for dst, out in ((lands[k].at[me], sends), (lands[k].at[pid], arrivals)):
                out.append(pltpu.make_async_remote_copy(
                    src_ref=src, dst_ref=dst, send_sem=send_sems.at[s], recv_sem=recv_sems.at[s],
                    device_id=(px, py, pc), device_id_type=pl.DeviceIdType.MESH))
    return sends, arrivals


def _exchange_start(arrs, scatter, name):
    n = len(arrs)
    n_sem = n * len(_RELS)

    def body(*refs):
        ins, lands = refs[:n], refs[n:2 * n]
        send_sems, recv_sems = refs[2 * n], refs[2 * n + 1]
        token = refs[-1]
        sends, _ = _peer_copies(ins, lands, send_sems, recv_sems, scatter)
        for cp in sends:
            cp.start()
        token[...] = jnp.zeros_like(token)

    land_shapes = [a.shape if scatter else (N_DEV,) + a.shape for a in arrs]
    ops = [pltpu.with_memory_space_constraint(a, pltpu.HBM) for a in arrs]
    ops += [pltpu.with_memory_space_constraint(lax.empty(s, a.dtype), pltpu.HBM) for s, a in zip(land_shapes, arrs)]
    out = pl.pallas_call(
        body, name=name,
        out_shape=(pltpu.SemaphoreType.DMA((n_sem,)), pltpu.SemaphoreType.DMA((n_sem,)),
                   *[pltpu.HBM(a.shape, a.dtype) for a in arrs],
                   *[pltpu.HBM(s, a.dtype) for s, a in zip(land_shapes, arrs)],
                   jax.ShapeDtypeStruct((SUBLANES, LANES), F32)),
        in_specs=[_HBM] * (2 * n),
        out_specs=(_SEM, _SEM, *[_HBM] * (2 * n), pl.BlockSpec(memory_space=pltpu.VMEM)),
        input_output_aliases={i: 2 + i for i in range(2 * n)},
        compiler_params=pltpu.CompilerParams(has_side_effects=_DATAFLOW))(*ops)
    return out[0], out[1], list(out[2:2 + n]), list(out[2 + n:2 + 2 * n]), out[-1]


def _exchange_wait(started, after, scatter, name):
    send_sems, recv_sems, srcs, lands, _ = started
    n = len(srcs)

    def body(*refs):
        ins, lands_ = refs[:n], refs[n:2 * n]
        _, arrivals = _peer_copies(ins, lands_, refs[2 * n], refs[2 * n + 1], scatter)
        for cp in arrivals:
            cp.wait_send()
            cp.wait_recv()

    out = pl.pallas_call(
        body, name=name,
        out_shape=(*[pltpu.HBM(a.shape, a.dtype) for a in srcs], *[pltpu.HBM(a.shape, a.dtype) for a in lands]),
        in_specs=[_HBM] * (2 * n) + [_SEM, _SEM, pl.BlockSpec(memory_space=pl.ANY)],
        out_specs=tuple([_HBM] * (2 * n)), input_output_aliases={i: i for i in range(2 * n)},
        compiler_params=pltpu.CompilerParams(has_side_effects=_DATAFLOW))(*srcs, *lands, send_sems, recv_sems, after)
    return list(out[:n]), list(out[n:])


def _place_own_call(srcs, lands, scatter, me, name):
    outs = []
    for k, (src, land) in enumerate(zip(srcs, lands)):
        R, C = land.shape[1:]
        tr = R
        while tr % 32 == 0 and tr * C * land.dtype.itemsize > 2 * 1024 * 1024:
            tr //= 2

        def body(me_ref, s_ref, l_ref, o_ref):
            o_ref[...] = s_ref[...]

        src_spec = (pl.BlockSpec((None, tr, C), lambda i, me_ref: (me_ref[0], i, 0)) if scatter
                    else pl.BlockSpec((tr, C), lambda i, me_ref: (i, 0)))
        outs.append(pl.pallas_call(
            body, name=f"{name}_{k}", out_shape=jax.ShapeDtypeStruct(land.shape, land.dtype),
            grid_spec=pltpu.PrefetchScalarGridSpec(
                num_scalar_prefetch=1, grid=(R // tr,),
                in_specs=[src_spec, pl.BlockSpec(memory_space=pl.ANY)],
                out_specs=pl.BlockSpec((None, tr, C), lambda i, me_ref: (me_ref[0], i, 0))),
            input_output_aliases={2: 0}, compiler_params=_cparams())(me, src, land))
    return outs


def _adamw_shard_call(w, parts, m, v, name):
    R, C = w.shape
    tr = _tile(R, 128) if R % 16 == 0 else R

    def body(w_ref, p_ref, m_ref, v_ref, g_ref, d_ref, nm_ref, nv_ref):
        g = p_ref[0].astype(F32)
        for s in range(1, N_DEV):
            g = g + p_ref[s].astype(F32)
        d, nm, nv = _adamw(w_ref[...], g, m_ref[...], v_ref[...])
        g_ref[...] = g
        d_ref[...] = d
        nm_ref[...] = nm
        nv_ref[...] = nv

    tile = pl.BlockSpec((tr, C), lambda i: (i, 0))
    sh = jax.ShapeDtypeStruct((R, C), F32)
    return pl.pallas_call(
        body, name=name, out_shape=(sh, sh, sh, sh), grid=(R // tr,),
        in_specs=[tile, pl.BlockSpec((N_DEV, tr, C), lambda i: (0, i, 0)), tile, tile],
        out_specs=(tile, tile, tile, tile), compiler_params=_cparams())(w, parts, m, v)


def _pack(arrs, rows_mult=SUBLANES):
    flat = jnp.concatenate([a.reshape(-1).astype(F32) for a in arrs])
    n = flat.shape[0]
    per = rows_mult * LANES
    total = -(-n // per) * per
    return jnp.pad(flat, (0, total - n)).reshape(total // LANES, LANES)


def _unpack(pack, shapes):
    flat = pack.reshape(-1)
    out, off = [], 0
    for s in shapes:
        n = 1
        for d in s:
            n *= d
        out.append(flat[off:off + n].reshape(s))
        off += n
    return out


def kernel(x, meta_tokens, mix_norm_g, w_in, ssm_lambda_re, ssm_lambda_im, ssm_log_dt, ssm_b_re, ssm_b_im, ssm_c_re, ssm_c_im, ssm_d, ssm_w_glu, w_ssm_proj, hgrn_lb_logits, hgrn_norm_g, w_hgrn_proj, w_out, ffn_norm_g, w_up, conv_w, conv_b, w_down, final_norm_g, loss_target, m_meta_tokens, m_mix_norm_g, m_w_in, m_ssm_lambda_re, m_ssm_lambda_im, m_ssm_log_dt, m_ssm_b_re, m_ssm_b_im, m_ssm_c_re, m_ssm_c_im, m_ssm_d, m_ssm_w_glu, m_w_ssm_proj, m_hgrn_lb_logits, m_hgrn_norm_g, m_w_hgrn_proj, m_w_out, m_ffn_norm_g, m_w_up, m_conv_w, m_conv_b, m_w_down, m_final_norm_g, v_meta_tokens, v_mix_norm_g, v_w_in, v_ssm_lambda_re, v_ssm_lambda_im, v_ssm_log_dt, v_ssm_b_re, v_ssm_b_im, v_ssm_c_re, v_ssm_c_im, v_ssm_d, v_ssm_w_glu, v_w_ssm_proj, v_hgrn_lb_logits, v_hgrn_norm_g, v_w_hgrn_proj, v_w_out, v_ffn_norm_g, v_w_up, v_conv_w, v_conv_b, v_w_down, v_final_norm_g):
    args = dict(locals())
    B, S_len, D = x.shape
    L = S_len + N_META
    T = B * L
    tm = _tile(L, ROW_TILE_CAP)
    tps = L // tm
    G, P = ssm_lambda_re.shape[1:]
    H = ssm_b_re.shape[-1]
    W = G * H
    n_cb = W // LANES
    gpb = G // n_cb
    hd = hgrn_norm_g.shape[1]
    n_heads = D // hd
    n_in = w_in.shape[2]
    F = w_up.shape[2]
    assert W == D and n_in % LANES == 0

    me = (4 * lax.axis_index("x") + 2 * lax.axis_index("y") + lax.axis_index("c")).astype(jnp.int32).reshape(1)
    meta_g, cw_g = _exchange_call([meta_tokens, conv_w[0]], False, "gather_small_params")
    ga = _exchange_start([w_in[0].astype(MXU)], False, "gather_a_start")
    gb = _exchange_start(
        [w_up[0].astype(MXU), ssm_w_glu[0].astype(MXU), w_ssm_proj[0].astype(MXU), w_hgrn_proj[0].astype(MXU),
         w_out[0].astype(MXU), w_down[0].astype(MXU)], False, "gather_b_start")
    started_tok = (ga[4] + gb[4])[0:1, 0:1]
    meta_full = meta_g.transpose(1, 0, 2).reshape(N_META, D)
    cb_g = conv_b.reshape(N_DEV, 1, F)

    h0 = jnp.concatenate([jnp.broadcast_to(meta_full[None], (B, N_META, D)), x], axis=1).reshape(T, D)
    tgt = jnp.concatenate([jnp.zeros((B, N_META, D), F32), loss_target], axis=1).reshape(T, D)

    lr, li = ssm_lambda_re[0], ssm_lambda_im[0]
    ldt = ssm_log_dt[0].reshape(G, 1)
    bt_re = ssm_b_re[0].transpose(2, 0, 1).reshape(H, G * P)
    bt_im = ssm_b_im[0].transpose(2, 0, 1).reshape(H, G * P)
    seg = _seg_len(L)
    a_re, a_im, as_re, as_im, coef_re, coef_im = _small_call(
        _disc_a_power(seg), [lr, li, ldt], [((G, P), F32)] * 6, "s5_discretise")
    bbt_re, bbt_im = _small_call(
        _disc_b, [coef_re.reshape(1, G * P), coef_im.reshape(1, G * P), bt_re, bt_im],
        [((H, G * P), F32)] * 2, "s5_input_matrix")
    eye = jnp.eye(gpb, dtype=F32)
    hw = gpb * P

    def expand_b(bbt):
        t = bbt.reshape(H, n_cb, gpb, P).transpose(1, 0, 2, 3)[:, None]
        return (t * eye[None, :, None, :, None]).reshape(n_cb, gpb * H, hw)

    def expand_c(cm):
        t = cm.reshape(n_cb, gpb, H, P).transpose(0, 1, 3, 2)[:, :, :, None]
        return (t * eye[None, :, None, :, None]).reshape(n_cb, hw, gpb * H)

    wb = jnp.concatenate([expand_b(bbt_re), expand_b(bbt_im)], axis=2).astype(MXU)
    wc = jnp.concatenate([expand_c(ssm_c_re[0]), -expand_c(ssm_c_im[0])], axis=1).astype(MXU)
    tab = jnp.stack([jnp.concatenate([a_re.reshape(n_cb, hw), a_im.reshape(n_cb, hw)], axis=1),
                     jnp.concatenate([as_re.reshape(n_cb, hw), as_im.reshape(n_cb, hw)], axis=1)], axis=1)
    tab = jnp.broadcast_to(tab[:, :, None, :], (n_cb, 2, SUBLANES, 2 * hw))
    dsk = ssm_d.reshape(n_cb, 1, LANES)
    lb = _small_call(_lb_fn, [hgrn_lb_logits], [((1, D), F32)], "hgrn_lower_bound")[0]

    z1 = _norm_call(h0, mix_norm_g + started_tok, tm, "mix_norm")
    ga_src, ga_land = _exchange_wait(ga, z1, False, "gather_a_wait")
    win_g = _place_own_call(ga_src, ga_land, False, me, "gather_a_own")[0]
    p = _mm_shard(z1, win_g, tm, "in_proj", False)
    p3 = p.reshape(B, L, p.shape[1])
    u_seg = _to_segments(p3[:, :, :W], seg)
    ya_seg, s_all = _s5_fwd_call(u_seg, wb, wc, tab, dsk, "s5_fwd")
    ya = _from_segments(ya_seg, seg, L).reshape(T, W)
    gb_src, gb_land = _exchange_wait(gb, ya, False, "gather_b_wait")
    gathered = _place_own_call(gb_src, gb_land, False, me, "gather_b_own")
    wup_g = gathered[0]
    wglu_g, wsp_g, whp_g, wout_g = [g.reshape(D, D) for g in gathered[1:5]]
    wdn_g = gathered[5].reshape(N_DEV // 2, 2 * w_down.shape[1], D)
    yo, a_br = _glu_proj_call(ya, wglu_g, wsp_g, tm, "s5_glu_proj")
    yb = _hgrn_fwd_call(p3, lb, hgrn_norm_g, n_heads, n_cb, "hgrn_fwd").reshape(T, D)
    col_ga = 5
    h1, mg, bm, z2 = _merge_call(yb, a_br, p, h0, whp_g, wout_g, ffn_norm_g, col_ga, tm, "merge")
    up = _mm_shard(z2, wup_g, tm, "up_proj", True)
    act, dh2, loss_part, dg3 = _ffn_fwd_call(up, cw_g, cb_g, wdn_g, h1, tgt, final_norm_g.reshape(1, D),
                                             tm, tps, "ffn_out_loss")

    dua, dub, dwd, dcwa, dcwb, dcba, dcbb = _ffn_bwd_a_call(dh2, up, act, cw_g, cb_g, wdn_g, tm, tps, "ffn_bwd_gate")
    dupa, dupb, dh1, dg2 = _ffn_bwd_b_call(dua, dub, cw_g, wup_g, h1, ffn_norm_g, dh2, tm, tps, "ffn_bwd_up")
    dwup = jnp.concatenate([_mm_tn(z2, dupa, N_DEV // 2, tm, "dw_up_a", True),
                            _mm_tn(z2, dupb, N_DEV // 2, tm, "dw_up_b", True)], axis=0)
    sh_rows = D // N_DEV
    sa = _exchange_start([dwup.astype(WIRE), dwd.reshape(N_DEV, w_down.shape[1], D).astype(WIRE)], True,
                         "scatter_a_start")
    dmg, dwout = _lin_bwd(mg, dh1, wout_g + sa[4][0:1, 0:1].astype(MXU), tm, "out_proj_bwd")
    da_br, dbm, dga, dgb = _merge_bwd_call(dmg, a_br, bm, p, col_ga, tm, "merge_bwd")
    dyo, dwsp = _lin_bwd(yo, da_br, wsp_g, tm, "ssm_proj_bwd")
    dyb, dwhp = _lin_bwd(yb, dbm, whp_g, tm, "hgrn_proj_bwd")
    dya, dwglu = _glu_bwd_call(ya, dyo, wglu_g, tm, "s5_glu_bwd")
    sb = _exchange_start([t.reshape(N_DEV, sh_rows, D).astype(WIRE) for t in (dwglu, dwsp, dwhp, dwout)], True,
                         "scatter_b_start")
    tok_b = sb[4][0:1, :]
    du_seg, dwb, dwc, dab, ddsk = _s5_bwd_call(u_seg, s_all, _to_segments(dya.reshape(B, L, W), seg), wb, wc, tab,
                                               dsk + tok_b[None], "s5_bwd")
    du = _from_segments(du_seg, seg, L)

    def diag_b(dw):
        t = (dw.reshape(n_cb, gpb, H, gpb, P) * eye[None, :, None, :, None]).sum(axis=1)
        return t.transpose(1, 0, 2, 3).reshape(H, G * P)

    def diag_c(dw):
        t = (dw.reshape(n_cb, gpb, P, gpb, H) * eye[None, :, None, :, None]).sum(axis=3)
        return t.transpose(0, 1, 3, 2).reshape(G, H, P)

    early_parts = [dab[:, 0, :hw].reshape(G, P), dab[:, 0, hw:].reshape(G, P),
                   diag_b(dwb[:, :, :hw]), diag_b(dwb[:, :, hw:]),
                   diag_c(dwc[:, :hw]), -diag_c(dwc[:, hw:]), ddsk.reshape(1, D)]
    early_pack = _pack(early_parts)
    se = _exchange_start([early_pack], False, "gather_s5_grads_start")
    dq, dfl, di, dog, dlb, dng = _hgrn_bwd_call(p3, dyb.reshape(B, L, D), lb, hgrn_norm_g + tok_b + se[4][0:1, :],
                                                n_heads, n_cb, "hgrn_bwd")
    dp = jnp.concatenate([du.reshape(T, W), dq.reshape(T, D), dfl.reshape(T, D), di.reshape(T, D),
                          dog.reshape(T, D), dga, dgb], axis=1)
    dwin = _mm_tn(z1, dp, N_DEV, tm, "dw_in", False)
    sc = _exchange_start([dwin.astype(WIRE)], True, "scatter_c_start")
    dh0, dg1 = _in_bwd_call(dp, win_g, h0, mix_norm_g + sc[4][0:1, 0:1], dh1, tm, "in_proj_bwd")
    dh0_3 = dh0.reshape(B, L, D)
    grad_x = dh0_3[:, N_META:]
    dmeta = _meta_grad_call(dh0_3, "meta_grad")

    late_parts = [dg1, dlb, dng, dg2, jnp.concatenate([dcba, dcbb], axis=0).reshape(1, N_DEV * F), dg3, loss_part]
    late_pack = _pack(late_parts)

    dcw = jnp.concatenate([dcwa, dcwb], axis=0)
    dmeta_s = dmeta.reshape(N_META, N_DEV, D // N_DEV).transpose(1, 0, 2)
    parts_d = _exchange_call([dmeta_s, dcw], True, "scatter_small_grads")
    late_all = _exchange_call([late_pack], False, "gather_small_grads")[0]
    early_all = _place_own_call(*_exchange_wait(se, late_all, False, "gather_s5_grads_wait"), False, me,
                                "gather_s5_grads_own")[0]
    parts_a = _place_own_call(*_exchange_wait(sa, late_all, True, "scatter_a_wait"), True, me, "scatter_a_own")
    parts_b = _place_own_call(*_exchange_wait(sb, late_all, True, "scatter_b_wait"), True, me, "scatter_b_own")
    parts_c = _place_own_call(*_exchange_wait(sc, late_all, True, "scatter_c_wait"), True, me, "scatter_c_own")
    parts = [parts_c[0], parts_a[0], *parts_b, parts_a[1], parts_d[0], parts_d[1]]

    def sum8(a, b):
        ta, tb = a[0], b[0]
        for s in range(1, N_DEV):
            ta, tb = ta + a[s], tb + b[s]
        return ta, tb

    early_sum, late_sum = _small_call(sum8, [early_all, late_all], [(early_pack.shape, F32), (late_pack.shape, F32)],
                                      "sum_small_grads")
    t_abr, t_abi, t_bbr, t_bbi, g_cre, g_cim, g_dsk = _unpack(early_sum, [a.shape for a in early_parts])
    g_g1, t_lb, g_ng, g_g2, g_cb, g_g3, loss_v = _unpack(late_sum, [a.shape for a in late_parts])

    def disc_b_bwd(cr, ci, br, bi, dbr, dbi):
        _, vjp = jax.vjp(_disc_b, cr, ci, br, bi)
        return vjp((dbr, dbi))

    t_cr, t_ci, g_btr, g_bti = _small_call(
        disc_b_bwd, [coef_re.reshape(1, G * P), coef_im.reshape(1, G * P), bt_re, bt_im, t_bbr, t_bbi],
        [((1, G * P), F32)] * 2 + [((H, G * P), F32)] * 2, "s5_input_matrix_bwd")

    def disc_a_bwd(lr_, li_, ldt_, dar, dai, dcr, dci):
        _, vjp = jax.vjp(_disc_a, lr_, li_, ldt_)
        return vjp((dar, dai, dcr, dci))

    g_lr, g_li, g_ldt = _small_call(
        disc_a_bwd, [lr, li, ldt, t_abr, t_abi, t_cr.reshape(G, P), t_ci.reshape(G, P)],
        [((G, P), F32)] * 2 + [((G, 1), F32)], "s5_discretise_bwd")

    def lb_bwd(logits, d):
        _, vjp = jax.vjp(_lb_fn, logits)
        return vjp(d)

    g_lbl = _small_call(lb_bwd, [hgrn_lb_logits, t_lb], [(hgrn_lb_logits.shape, F32)], "hgrn_lower_bound_bwd")[0]

    grads = dict(
        mix_norm_g=g_g1, ssm_lambda_re=g_lr[None], ssm_lambda_im=g_li[None], ssm_log_dt=g_ldt.reshape(1, G),
        ssm_b_re=g_btr.reshape(H, G, P).transpose(1, 2, 0)[None], ssm_b_im=g_bti.reshape(H, G, P).transpose(1, 2, 0)[None],
        ssm_c_re=g_cre[None], ssm_c_im=g_cim[None], ssm_d=g_dsk, hgrn_lb_logits=g_lbl, hgrn_norm_g=g_ng,
        ffn_norm_g=g_g2, conv_b=g_cb.reshape(1, N_DEV * F), final_norm_g=g_g3.reshape(D))
    loss = loss_v[0, 0]

    delta, new_m, new_v = {}, {}, {}
    sharded = [("w_in", parts[0], (D, n_in)), ("w_up", parts[1], (D, F)), ("ssm_w_glu", parts[2], (sh_rows, D)),
               ("w_ssm_proj", parts[3], (sh_rows, D)), ("w_hgrn_proj", parts[4], (sh_rows, D)),
               ("w_out", parts[5], (sh_rows, D)), ("w_down", parts[6], (w_down.shape[1], D)),
               ("meta_tokens", parts[7], (N_META, D // N_DEV)), ("conv_w", parts[8], (3, F))]
    for name, part, shp in sharded:
        full = args[name].shape
        g, d_, nm, nv = _adamw_shard_call(args[name].reshape(shp), part, args["m_" + name].reshape(shp),
                                          args["v_" + name].reshape(shp), "adamw_" + name)
        grads[name], delta[name], new_m[name], new_v[name] = [t.reshape(full) for t in (g, d_, nm, nv)]

    rep = ["mix_norm_g", "ssm_lambda_re", "ssm_lambda_im", "ssm_log_dt", "ssm_b_re", "ssm_b_im", "ssm_c_re",
           "ssm_c_im", "ssm_d", "hgrn_lb_logits", "hgrn_norm_g", "ffn_norm_g", "conv_b", "final_norm_g"]
    rep_shapes = [args[n].shape for n in rep]
    packs = [_pack([args[pre + n] for n in rep]) for pre in ("", "m_", "v_")]
    g_pack = _pack([grads[n] for n in rep])
    outs = _small_call(lambda w, g, m, v: _adamw(w, g, m, v), [packs[0], g_pack, packs[1], packs[2]],
                       [(g_pack.shape, F32)] * 3, "adamw_replicated")
    for n, d_, nm, nv in zip(rep, *[_unpack(o, rep_shapes) for o in outs]):
        delta[n], new_m[n], new_v[n] = d_, nm, nv

    names = ["meta_tokens", "mix_norm_g", "w_in", "ssm_lambda_re", "ssm_lambda_im", "ssm_log_dt", "ssm_b_re",
             "ssm_b_im", "ssm_c_re", "ssm_c_im", "ssm_d", "ssm_w_glu", "w_ssm_proj", "hgrn_lb_logits", "hgrn_norm_g",
             "w_hgrn_proj", "w_out", "ffn_norm_g", "w_up", "conv_w", "conv_b", "w_down", "final_norm_g"]
    return (loss, grad_x, *[grads[n] for n in names], *[delta[n] for n in names],
            *[new_m[n] for n in names], *[new_v[n] for n in names])
```

```python
import functools

import jax
import jax.numpy as jnp
from jax import lax
from jax.experimental import pallas as pl
from jax.experimental.pallas import tpu as pltpu

F32 = jnp.float32
MXU = jnp.bfloat16
ACT = jnp.bfloat16
WIRE = jnp.bfloat16
N_DEV = 8
N_META = 16
CHUNK = 16
EPS = 1e-6
ADAM_LR, ADAM_B1, ADAM_B2, ADAM_EPS, ADAM_WD, ADAM_STEP = 0.001, 0.9, 0.999, 1e-08, 0.01, 10
SUBLANES = 8
LANES = 128
ROW_TILE_CAP = 700
VMEM_LIMIT = 60 * 1024 * 1024


def _cparams(**kw):
    return pltpu.CompilerParams(vmem_limit_bytes=VMEM_LIMIT, **kw)


def _tile(n, cap):
    best = None
    for t in range(16, min(n, cap) + 1, 16):
        if n % t == 0:
            best = t
    assert best is not None, (n, cap)
    return best


def _dot(a, b):
    return lax.dot_general(a.astype(MXU), b.astype(MXU), (((1,), (0,)), ((), ())), preferred_element_type=F32)


def _dot_nt(a, b):
    return lax.dot_general(a.astype(MXU), b.astype(MXU), (((1,), (1,)), ((), ())), preferred_element_type=F32)


def _dot_tn(a, b):
    return lax.dot_general(a.astype(MXU), b.astype(MXU), (((0,), (0,)), ((), ())), preferred_element_type=F32)


def _rms(x, g):
    return x * lax.rsqrt(jnp.mean(x * x, axis=-1, keepdims=True) + EPS) * g


def _silu(x):
    return x * jax.nn.sigmoid(x)


def _small_call(fn, ins, out_shapes, name):
    n_in = len(ins)

    def body(*refs):
        outs = fn(*[r[...] for r in refs[:n_in]])
        outs = outs if isinstance(outs, (tuple, list)) else (outs,)
        for r, o in zip(refs[n_in:], outs):
            r[...] = o.astype(r.dtype)

    vm = pl.BlockSpec(memory_space=pltpu.VMEM)
    return pl.pallas_call(
        body, name=name, out_shape=tuple(jax.ShapeDtypeStruct(s, d) for s, d in out_shapes),
        in_specs=[vm] * n_in, out_specs=tuple([vm] * len(out_shapes)), compiler_params=_cparams())(*ins)


def _disc_a(lr, li, ldt):
    dt = jnp.exp(ldt)
    mag = jnp.exp(lr * dt)
    ab_re = mag * jnp.cos(li * dt)
    ab_im = mag * jnp.sin(li * dt)
    den = lr * lr + li * li
    nr = ab_re - 1.0
    coef_re = (nr * lr + ab_im * li) / den
    coef_im = (ab_im * lr - nr * li) / den
    return ab_re, ab_im, coef_re, coef_im


def _disc_a_power(n):
    def fn(lr, li, ldt):
        ab_re, ab_im, coef_re, coef_im = _disc_a(lr, li, ldt)
        pr, pi, sr, si, m = None, None, ab_re, ab_im, n
        while m:
            if m & 1:
                pr, pi = (sr, si) if pr is None else (pr * sr - pi * si, pr * si + pi * sr)
            m >>= 1
            if m:
                sr, si = sr * sr - si * si, 2.0 * sr * si
        return ab_re, ab_im, pr, pi, coef_re, coef_im
    return fn


def _disc_b(coef_re, coef_im, bt_re, bt_im):
    return coef_re * bt_re - coef_im * bt_im, coef_re * bt_im + coef_im * bt_re


def _lb_fn(logits):
    return jax.nn.softmax(logits, axis=0)[0:1]


def _adamw(w, g, m, v):
    m = ADAM_B1 * m + (1.0 - ADAM_B1) * g
    v = ADAM_B2 * v + (1.0 - ADAM_B2) * jnp.square(g)
    m_hat = m / (1.0 - ADAM_B1 ** ADAM_STEP)
    v_hat = v / (1.0 - ADAM_B2 ** ADAM_STEP)
    delta = -ADAM_LR * (m_hat / (jnp.sqrt(v_hat) + ADAM_EPS) + ADAM_WD * w)
    return delta, m, v


def _norm_call(h, g, tm, name):
    T, D = h.shape

    def body(h_ref, g_ref, z_ref):
        z_ref[...] = _rms(h_ref[...], g_ref[...]).astype(ACT)

    return pl.pallas_call(
        body, name=name, out_shape=jax.ShapeDtypeStruct((T, D), ACT), grid=(T // tm,),
        in_specs=[pl.BlockSpec((tm, D), lambda i: (i, 0)), pl.BlockSpec((1, D), lambda i: (0, 0))],
        out_specs=pl.BlockSpec((tm, D), lambda i: (i, 0)), compiler_params=_cparams())(h, g)


def _mm_shard(x, w, tm, name, major):
    T, K = x.shape
    S, _, N = w.shape

    def body(x_ref, w_ref, o_ref):
        o_ref[...] = _dot(x_ref[...], w_ref[...]).astype(o_ref.dtype)

    if major:
        out_shape = jax.ShapeDtypeStruct((S, T, N), ACT)
        out_spec = pl.BlockSpec((None, tm, N), lambda j, i: (j, i, 0))
    else:
        out_shape = jax.ShapeDtypeStruct((T, S * N), ACT)
        out_spec = pl.BlockSpec((tm, N), lambda j, i: (i, j))
    return pl.pallas_call(
        body, name=name, out_shape=out_shape, grid=(S, T // tm),
        in_specs=[pl.BlockSpec((tm, K), lambda j, i: (i, 0)), pl.BlockSpec((None, K, N), lambda j, i: (j, 0, 0))],
        out_specs=out_spec, compiler_params=_cparams())(x, w)


def _mm_tn(x, y, n_shards, tm, name, major):
    T, K = x.shape
    S = n_shards
    N = y.shape[-1] if major else y.shape[-1] // S

    def body(x_ref, y_ref, o_ref):
        @pl.when(pl.program_id(1) == 0)
        def _():
            o_ref[...] = jnp.zeros_like(o_ref)
        o_ref[...] += _dot_tn(x_ref[...], y_ref[...])

    y_spec = (pl.BlockSpec((None, tm, N), lambda j, i: (j, i, 0)) if major
              else pl.BlockSpec((tm, N), lambda j, i: (i, j)))
    return pl.pallas_call(
        body, name=name, out_shape=jax.ShapeDtypeStruct((S, K, N), F32), grid=(S, T // tm),
        in_specs=[pl.BlockSpec((tm, K), lambda j, i: (i, 0)), y_spec],
        out_specs=pl.BlockSpec((None, K, N), lambda j, i: (j, 0, 0)), compiler_params=_cparams())(x, y)


def _lin_bwd(x, dy, w, tm, name):
    T, K = x.shape
    N = dy.shape[1]

    def body(x_ref, dy_ref, w_ref, dx_ref, dw_ref):
        @pl.when(pl.program_id(0) == 0)
        def _():
            dw_ref[...] = jnp.zeros_like(dw_ref)
        dy = dy_ref[...]
        dx_ref[...] = _dot_nt(dy, w_ref[...]).astype(dx_ref.dtype)
        dw_ref[...] += _dot_tn(x_ref[...], dy)

    return pl.pallas_call(
        body, name=name,
        out_shape=(jax.ShapeDtypeStruct((T, K), ACT), jax.ShapeDtypeStruct((K, N), F32)), grid=(T // tm,),
        in_specs=[pl.BlockSpec((tm, K), lambda i: (i, 0)), pl.BlockSpec((tm, N), lambda i: (i, 0)),
                  pl.BlockSpec((K, N), lambda i: (0, 0))],
        out_specs=(pl.BlockSpec((tm, K), lambda i: (i, 0)), pl.BlockSpec((K, N), lambda i: (0, 0))),
        compiler_params=_cparams())(x, dy, w)


N_SEG = SUBLANES


def _seg_len(L):
    return -(-L // (N_SEG * SUBLANES)) * SUBLANES


def _to_segments(a3, seg):
    b, length, c = a3.shape
    a = jnp.pad(a3, ((0, 0), (0, N_SEG * seg - length), (0, 0)))
    return a.reshape(b, N_SEG, seg, c).transpose(0, 2, 1, 3).reshape(b, N_SEG * seg, c)


def _from_segments(a3, seg, length):
    b, _, c = a3.shape
    return a3.reshape(b, seg, N_SEG, c).transpose(0, 2, 1, 3).reshape(b, N_SEG * seg, c)[:, :length]


def _seg_scan(x_ref, tab_ref, n_slabs, reverse):
    hw = x_ref.shape[1] // 2
    sign = -1.0 if reverse else 1.0
    ar, ai = tab_ref[0][:, :hw], sign * tab_ref[0][:, hw:]
    br, bi = tab_ref[1][:, :hw], sign * tab_ref[1][:, hw:]

    def slab(k):
        kk = (n_slabs - 1 - k) if reverse else k
        return pl.ds(pl.multiple_of(kk * SUBLANES, SUBLANES), SUBLANES)

    def horner(k, carry):
        cr, ci = carry
        x = x_ref[slab(k), :]
        return ar * cr - ai * ci + x[:, :hw], ar * ci + ai * cr + x[:, hw:]

    z = jnp.zeros((SUBLANES, hw), F32)
    fr, fi = lax.fori_loop(0, n_slabs, horner, (z, z))

    row = lax.broadcasted_iota(jnp.int32, (SUBLANES, hw), 0)
    edge = (row == SUBLANES - 1) if reverse else (row == 0)
    shift = SUBLANES - 1 if reverse else 1
    sr, si = z, z
    for _ in range(N_SEG - 1):
        er, ei = fr + br * sr - bi * si, fi + br * si + bi * sr
        sr = jnp.where(edge, 0.0, pltpu.roll(er, shift, 0))
        si = jnp.where(edge, 0.0, pltpu.roll(ei, shift, 0))

    def scan(k, carry):
        cr, ci = carry
        rows = slab(k)
        x = x_ref[rows, :]
        nr, ni = ar * cr - ai * ci + x[:, :hw], ar * ci + ai * cr + x[:, hw:]
        x_ref[rows, 0:hw] = nr
        x_ref[rows, hw:2 * hw] = ni
        return nr, ni

    lax.fori_loop(0, n_slabs, scan, (sr, si))


def _s5_fwd_call(p3, wb, wc, tab_f, dsk, name):
    B, L, _ = p3.shape
    n_cb, cw, sw = wb.shape

    def body(u_ref, wb_ref, wc_ref, tab_ref, d_ref, ya_ref, so_ref, s_ref):
        u = u_ref[...]
        s_ref[...] = _dot(u, wb_ref[...])
        _seg_scan(s_ref, tab_ref, L // SUBLANES, False)
        s = s_ref[...].astype(MXU)
        so_ref[...] = s
        y = _dot(s, wc_ref[...]) + d_ref[...] * u.astype(F32)
        ya_ref[...] = jax.nn.gelu(y).astype(ACT)

    return pl.pallas_call(
        body, name=name,
        out_shape=(jax.ShapeDtypeStruct((B, L, n_cb * cw), ACT), jax.ShapeDtypeStruct((B, n_cb, L, sw), MXU)),
        grid=(B, n_cb),
        in_specs=[pl.BlockSpec((None, L, cw), lambda b, c: (b, 0, c)),
                  pl.BlockSpec((None, cw, sw), lambda b, c: (c, 0, 0)),
                  pl.BlockSpec((None, sw, cw), lambda b, c: (c, 0, 0)),
                  pl.BlockSpec((None, 2, SUBLANES, sw), lambda b, c: (c, 0, 0, 0)),
                  pl.BlockSpec((None, 1, cw), lambda b, c: (c, 0, 0))],
        out_specs=(pl.BlockSpec((None, L, cw), lambda b, c: (b, 0, c)),
                   pl.BlockSpec((None, None, L, sw), lambda b, c: (b, c, 0, 0))),
        scratch_shapes=[pltpu.VMEM((L, sw), F32)], compiler_params=_cparams())(p3, wb, wc, tab_f, dsk)


def _s5_bwd_call(p3, s_all, dya, wb, wc, tab_r, dsk, name):
    B, L, _ = p3.shape
    n_cb, cw, sw = wb.shape
    hw = sw // 2
    n_slabs = L // SUBLANES

    def body(u_ref, si_ref, dya_ref, wb_ref, wc_ref, tr_ref, d_ref,
             du_ref, dwb_ref, dwc_ref, da_ref, dd_ref, s_ref, l_ref):
        @pl.when(pl.program_id(1) == 0)
        def _():
            dwb_ref[...] = jnp.zeros_like(dwb_ref)
            dwc_ref[...] = jnp.zeros_like(dwc_ref)
            da_ref[...] = jnp.zeros_like(da_ref)
            dd_ref[...] = jnp.zeros_like(dd_ref)

        u = u_ref[...]
        uf = u.astype(F32)
        s_in = si_ref[...]
        s_ref[...] = s_in.astype(F32)
        y = _dot(s_in, wc_ref[...]) + d_ref[...] * uf
        _, gelu_vjp = jax.vjp(jax.nn.gelu, y)
        dy = gelu_vjp(dya_ref[...].astype(F32))[0]
        dd_ref[...] += jnp.sum(dy * uf, axis=0, keepdims=True)
        l_ref[...] = _dot_nt(dy, wc_ref[...])
        _seg_scan(l_ref, tr_ref, n_slabs, True)
        du_ref[...] = (_dot_nt(l_ref[...], wb_ref[...]) + d_ref[...] * dy).astype(ACT)
        dwb_ref[...] += _dot_tn(u, l_ref[...])
        dwc_ref[...] += _dot_tn(s_in, dy)

        row = lax.broadcasted_iota(jnp.int32, (SUBLANES, hw), 0)
        last = s_ref[pl.ds((n_slabs - 1) * SUBLANES, SUBLANES), :]
        p0r = jnp.where(row == 0, 0.0, pltpu.roll(last[:, :hw], 1, 0))
        p0i = jnp.where(row == 0, 0.0, pltpu.roll(last[:, hw:], 1, 0))

        def step(k, carry):
            qr, qi, accr, acci = carry
            r0 = pl.multiple_of(k * SUBLANES, SUBLANES)
            s = s_ref[pl.ds(r0, SUBLANES), :]
            lam = l_ref[pl.ds(r0, SUBLANES), :]
            lr, li = lam[:, :hw], lam[:, hw:]
            accr = accr + lr * qr + li * qi
            acci = acci + li * qr - lr * qi
            return s[:, :hw], s[:, hw:], accr, acci

        z8 = jnp.zeros((SUBLANES, hw), F32)
        _, _, accr, acci = lax.fori_loop(0, n_slabs, step, (p0r, p0i, z8, z8))
        da_ref[...] += jnp.concatenate([jnp.sum(accr, axis=0, keepdims=True),
                                        jnp.sum(acci, axis=0, keepdims=True)], axis=1)

    W = n_cb * cw
    return pl.pallas_call(
        body, name=name,
        out_shape=(jax.ShapeDtypeStruct((B, L, W), ACT), jax.ShapeDtypeStruct((n_cb, cw, sw), F32),
                   jax.ShapeDtypeStruct((n_cb, sw, cw), F32), jax.ShapeDtypeStruct((n_cb, 1, sw), F32),
                   jax.ShapeDtypeStruct((n_cb, 1, cw), F32)),
        grid=(n_cb, B),
        in_specs=[pl.BlockSpec((None, L, cw), lambda c, b: (b, 0, c)),
                  pl.BlockSpec((None, None, L, sw), lambda c, b: (b, c, 0, 0)),
                  pl.BlockSpec((None, L, cw), lambda c, b: (b, 0, c)),
                  pl.BlockSpec((None, cw, sw), lambda c, b: (c, 0, 0)),
                  pl.BlockSpec((None, sw, cw), lambda c, b: (c, 0, 0)),
                  pl.BlockSpec((None, 2, SUBLANES, sw), lambda c, b: (c, 0, 0, 0)),
                  pl.BlockSpec((None, 1, cw), lambda c, b: (c, 0, 0))],
        out_specs=(pl.BlockSpec((None, L, cw), lambda c, b: (b, 0, c)),
                   pl.BlockSpec((None, cw, sw), lambda c, b: (c, 0, 0)),
                   pl.BlockSpec((None, sw, cw), lambda c, b: (c, 0, 0)),
                   pl.BlockSpec((None, 1, sw), lambda c, b: (c, 0, 0)),
                   pl.BlockSpec((None, 1, cw), lambda c, b: (c, 0, 0))),
        scratch_shapes=[pltpu.VMEM((L, sw), F32), pltpu.VMEM((L, sw), F32)],
        compiler_params=_cparams())(p3, s_all, dya, wb, wc, tab_r, dsk)


def _glu_proj_call(ya, wglu, wproj, tm, name):
    T, W = ya.shape
    D = wproj.shape[1]

    def body(ya_ref, wg_ref, wp_ref, yo_ref, a_ref):
        ya = ya_ref[...]
        yo = ya.astype(F32) * jax.nn.sigmoid(_dot(ya, wg_ref[...]))
        yo_ref[...] = yo.astype(ACT)
        a_ref[...] = _dot(yo, wp_ref[...]).astype(ACT)

    return pl.pallas_call(
        body, name=name, out_shape=(jax.ShapeDtypeStruct((T, W), ACT), jax.ShapeDtypeStruct((T, D), ACT)),
        grid=(T // tm,),
        in_specs=[pl.BlockSpec((tm, W), lambda i: (i, 0)), pl.BlockSpec((W, W), lambda i: (0, 0)),
                  pl.BlockSpec((W, D), lambda i: (0, 0))],
        out_specs=(pl.BlockSpec((tm, W), lambda i: (i, 0)), pl.BlockSpec((tm, D), lambda i: (i, 0))),
        compiler_params=_cparams())(ya, wglu, wproj)


def _glu_bwd_call(ya, dyo, wglu, tm, name):
    T, W = ya.shape

    def body(ya_ref, dyo_ref, wg_ref, dya_ref, dwg_ref):
        @pl.when(pl.program_id(0) == 0)
        def _():
            dwg_ref[...] = jnp.zeros_like(dwg_ref)
        ya = ya_ref[...]
        yaf = ya.astype(F32)
        dyo = dyo_ref[...].astype(F32)
        sg = jax.nn.sigmoid(_dot(ya, wg_ref[...]))
        dt = dyo * yaf * sg * (1.0 - sg)
        dya_ref[...] = (dyo * sg + _dot_nt(dt, wg_ref[...])).astype(ACT)
        dwg_ref[...] += _dot_tn(ya, dt)

    return pl.pallas_call(
        body, name=name, out_shape=(jax.ShapeDtypeStruct((T, W), ACT), jax.ShapeDtypeStruct((W, W), F32)),
        grid=(T // tm,),
        in_specs=[pl.BlockSpec((tm, W), lambda i: (i, 0)), pl.BlockSpec((tm, W), lambda i: (i, 0)),
                  pl.BlockSpec((W, W), lambda i: (0, 0))],
        out_specs=(pl.BlockSpec((tm, W), lambda i: (i, 0)), pl.BlockSpec((W, W), lambda i: (0, 0))),
        compiler_params=_cparams())(ya, dyo, wglu)


PAD = 16


def _chunk_cumsums(x, pad_ref, L):
    row = lax.broadcasted_iota(jnp.int32, x.shape, 0) % CHUNK
    zeros = jnp.zeros((PAD, x.shape[1]), F32)
    pad_ref[0:PAD, :] = zeros
    pad_ref[PAD + L:2 * PAD + L, :] = zeros
    c = x
    r = x
    d = 1
    while d < CHUNK:
        pad_ref[PAD:PAD + L, :] = c
        c = c + jnp.where(row >= d, pad_ref[PAD - d:PAD - d + L, :], 0.0)
        pad_ref[PAD:PAD + L, :] = r
        r = r + jnp.where(row + d < CHUNK, pad_ref[PAD + d:PAD + d + L, :], 0.0)
        d *= 2
    return c, r - x


def _hgrn_prep(q_ref, fl_ref, lb_ref, pad_ref, r0, n):
    rows = pl.ds(r0, n)
    lb = lb_ref[...]
    sig = jax.nn.sigmoid(fl_ref[rows, :].astype(F32))
    f = lb + (1.0 - lb) * sig
    k = 1.0 - f
    c, rc = _chunk_cumsums(jnp.log(f), pad_ref, n)
    e_in, e_inv, e_out = jnp.exp(c), jnp.exp(-c), jnp.exp(rc)
    q = q_ref[rows, :].astype(F32)
    return dict(sig=sig, f=f, k=k, q=q, e_in=e_in, e_inv=e_inv, e_out=e_out, dec=jnp.exp(c + rc))


def _for_row_blocks(L, fn):
    full = L // GROUP
    if full:
        def step(g, carry):
            fn(pl.multiple_of(g * GROUP, GROUP), GROUP)
            return carry
        lax.fori_loop(0, full, step, 0)
    if L % GROUP:
        fn(full * GROUP, L % GROUP)


def _chunk_mask(rb):
    r = lax.broadcasted_iota(jnp.int32, (rb, rb), 0)
    c = lax.broadcasted_iota(jnp.int32, (rb, rb), 1)
    return (r // CHUNK == c // CHUNK) & (c <= r)


def _hg_out(o, og, g):
    on = o * lax.rsqrt(jnp.mean(o * o, axis=-1, keepdims=True) + EPS) * g
    return on * _silu(og)


def _hgrn_specs(L, hd, col_q, n_heads, order):
    def spec(sec):
        return pl.BlockSpec((None, L, hd), lambda *g: (order(*g)[0], 0, col_q + sec * n_heads + order(*g)[1]))
    return [spec(0), spec(1), spec(2), spec(3)]


GROUP = 128
CPG = GROUP // CHUNK


def _expand(x):
    xf = x.astype(F32)
    chunk = lax.broadcasted_iota(jnp.int32, xf.shape, 0) // CHUNK
    return jnp.concatenate([jnp.where(chunk == j, xf, 0.0) for j in range(CPG)], axis=1)


def _fill_tail(refs_fills, L):
    for ref, fill in refs_fills:
        if ref.shape[0] > L:
            ref[L:ref.shape[0], :] = jnp.full((ref.shape[0] - L, ref.shape[1]), fill, ref.dtype)


GROUP_UNROLL = 4


def _hgrn_forward_core(q_ref, fl_ref, v_ref, lb_ref, pad_ref, qin_ref, kin_ref, kout_ref, vp_ref, dec_ref, o_ref,
                       s_ref, L, keep=()):
    hd = qin_ref.shape[1]
    n_groups = qin_ref.shape[0] // GROUP

    def prep(r0, n):
        pp = _hgrn_prep(q_ref, fl_ref, lb_ref, pad_ref, r0, n)
        rows = pl.ds(r0, n)
        for key, ref in keep:
            ref[rows, :] = pp[key]
        qin_ref[rows, :] = (pp["q"] * pp["e_in"]).astype(MXU)
        kin_ref[rows, :] = (pp["k"] * pp["e_inv"]).astype(MXU)
        kout_ref[rows, :] = (pp["k"] * pp["e_out"]).astype(MXU)
        vp_ref[rows, :] = v_ref[rows, :].astype(MXU)
        dec_ref[rows, :] = pp["dec"]

    _for_row_blocks(L, prep)
    _fill_tail(((qin_ref, 0.0), (kin_ref, 0.0), (kout_ref, 0.0), (vp_ref, 0.0), (dec_ref, 1.0)), L)
    mask = _chunk_mask(GROUP)

    def intra(g, carry):
        rows = pl.ds(pl.multiple_of(g * GROUP, GROUP), GROUP)
        a = jnp.where(mask, _dot_nt(qin_ref[rows, :], kin_ref[rows, :]), 0.0)
        o_ref[rows, :] = _dot(a, vp_ref[rows, :])
        kv = _dot_tn(vp_ref[rows, :], _expand(kout_ref[rows, :]))
        for j in range(CPG):
            s_ref[g * CPG + j] = kv[:, j * hd:(j + 1) * hd]
        return carry

    lax.fori_loop(0, n_groups, intra, 0, unroll=GROUP_UNROLL)

    def rec(n, st):
        kv = s_ref[n]
        s_ref[n] = st
        dec = dec_ref[pl.ds(pl.multiple_of(n * CHUNK, CHUNK), SUBLANES), :][0:1]
        return st * dec + kv

    lax.fori_loop(0, L // CHUNK, rec, jnp.zeros((hd, hd), F32))

    def inter(g, carry):
        rows = pl.ds(pl.multiple_of(g * GROUP, GROUP), GROUP)
        scat = jnp.concatenate([s_ref[g * CPG + j] for j in range(CPG)], axis=1)
        o_ref[rows, :] += _dot_nt(_expand(qin_ref[rows, :]), scat)
        return carry

    lax.fori_loop(0, n_groups, inter, 0, unroll=GROUP_UNROLL)


def _hgrn_scratch(L, hd):
    lp = -(-L // GROUP) * GROUP
    return lp, [pltpu.VMEM((GROUP + 2 * PAD, hd), F32), pltpu.VMEM((lp, hd), MXU), pltpu.VMEM((lp, hd), MXU),
                pltpu.VMEM((lp, hd), MXU), pltpu.VMEM((lp, hd), MXU), pltpu.VMEM((lp, hd), F32),
                pltpu.VMEM((lp, hd), F32), pltpu.VMEM((lp // CHUNK, hd, hd), F32)]


def _hgrn_fwd_call(p3, lb, ng, n_heads, col_q, name):
    B, L, _ = p3.shape
    hd = ng.shape[1]
    _, scratch = _hgrn_scratch(L, hd)

    def body(q_ref, fl_ref, v_ref, og_ref, lb_ref, ng_ref, yb_ref,
             pad_ref, qin_ref, kin_ref, kout_ref, vp_ref, dec_ref, o_ref, s_ref):
        _hgrn_forward_core(q_ref, fl_ref, v_ref, lb_ref, pad_ref, qin_ref, kin_ref, kout_ref, vp_ref, dec_ref,
                           o_ref, s_ref, L)

        def out(r0, n):
            rows = pl.ds(r0, n)
            yb_ref[rows, :] = _hg_out(o_ref[rows, :], og_ref[rows, :].astype(F32), ng_ref[...]).astype(ACT)

        _for_row_blocks(L, out)

    order = lambda b, h: (b, h)
    return pl.pallas_call(
        body, name=name, out_shape=jax.ShapeDtypeStruct((B, L, n_heads * hd), ACT), grid=(B, n_heads),
        in_specs=_hgrn_specs(L, hd, col_q, n_heads, order) + [
            pl.BlockSpec((1, hd), lambda b, h: (0, h)), pl.BlockSpec((1, hd), lambda b, h: (0, 0))],
        out_specs=pl.BlockSpec((None, L, hd), lambda b, h: (b, 0, h)),
        scratch_shapes=scratch, compiler_params=_cparams())(p3, p3, p3, p3, lb, ng)


def _hgrn_bwd_call(p3, dyb, lb, ng, n_heads, col_q, name):
    B, L, _ = p3.shape
    hd = ng.shape[1]
    n_chunks = L // CHUNK
    lp, scratch = _hgrn_scratch(L, hd)
    n_groups = lp // GROUP

    def body(q_ref, fl_ref, v_ref, og_ref, dyb_ref, lb_ref, ng_ref,
             dq_ref, dfl_ref, dv_ref, dog_ref, dlb_ref, dng_ref,
             pad_ref, qin_ref, kin_ref, kout_ref, vp_ref, dec_ref, o_ref, s_ref,
             do_ref, ds_ref, dqi_ref, dki_ref, dko_ref, dvv_ref, dct_ref,
             sig_ref, f_ref, ein_ref, einv_ref, eout_ref):
        @pl.when(pl.program_id(1) == 0)
        def _():
            dlb_ref[...] = jnp.zeros_like(dlb_ref)

        @pl.when((pl.program_id(0) == 0) & (pl.program_id(1) == 0))
        def _():
            dng_ref[...] = jnp.zeros_like(dng_ref)

        _hgrn_forward_core(q_ref, fl_ref, v_ref, lb_ref, pad_ref, qin_ref, kin_ref, kout_ref, vp_ref, dec_ref,
                           o_ref, s_ref, L, keep=(("sig", sig_ref), ("f", f_ref), ("e_in", ein_ref),
                                                  ("e_inv", einv_ref), ("e_out", eout_ref)))

        def out_bwd(r0, n):
            rows = pl.ds(r0, n)
            _, out_vjp = jax.vjp(_hg_out, o_ref[rows, :], og_ref[rows, :].astype(F32), ng_ref[...])
            d_o, d_og, d_ng = out_vjp(dyb_ref[rows, :].astype(F32))
            dog_ref[rows, :] = d_og.astype(ACT)
            dng_ref[...] += d_ng
            do_ref[rows, :] = d_o.astype(MXU)

        _for_row_blocks(L, out_bwd)
        _fill_tail(((do_ref, 0.0),), L)
        mask = _chunk_mask(GROUP)

        def grads_a(g, carry):
            rows = pl.ds(pl.multiple_of(g * GROUP, GROUP), GROUP)
            qi, ki, vv, do = qin_ref[rows, :], kin_ref[rows, :], vp_ref[rows, :], do_ref[rows, :]
            a = jnp.where(mask, _dot_nt(qi, ki), 0.0)
            da = jnp.where(mask, _dot_nt(do, vv), 0.0)
            sstack = s_ref[pl.ds(g * CPG, CPG)].reshape(CPG * hd, hd)
            dqi_ref[rows, :] = _dot(da, ki) + _dot(_expand(do), sstack)
            dki_ref[rows, :] = _dot_tn(da, qi)
            dvv_ref[rows, :] = _dot_tn(a, do)
            x = _dot_tn(do, _expand(qi))
            for j in range(CPG):
                ds_ref[g * CPG + j] = x[:, j * hd:(j + 1) * hd]
            return carry

        lax.fori_loop(0, n_groups, grads_a, 0, unroll=GROUP_UNROLL)

        def rec_bwd(k, dst):
            n = n_chunks - 1 - k
            r0 = pl.multiple_of(n * CHUNK, CHUNK)
            x = ds_ref[n]
            ds_ref[n] = dst
            dec = dec_ref[pl.ds(r0, SUBLANES), :][0:1]
            return dst * dec + x

        lax.fori_loop(0, n_chunks, rec_bwd, jnp.zeros((hd, hd), F32))

        def grads_b(g, carry):
            r0 = pl.multiple_of(g * GROUP, GROUP)
            rows = pl.ds(r0, GROUP)
            ds = [ds_ref[g * CPG + j] for j in range(CPG)]
            dscat = jnp.concatenate(ds, axis=1)
            dvv_ref[rows, :] += _dot_nt(_expand(kout_ref[rows, :]), dscat)
            dstack = ds_ref[pl.ds(g * CPG, CPG)].reshape(CPG * hd, hd)
            dko_ref[rows, :] = _dot(_expand(vp_ref[rows, :]), dstack)
            for j in range(CPG):
                dec = dec_ref[pl.ds(r0 + j * CHUNK, SUBLANES), :][0:1]
                ddec = dec * jnp.sum(ds[j] * s_ref[g * CPG + j], axis=0, keepdims=True)
                dct_ref[pl.ds(r0 + j * CHUNK, CHUNK), :] = jnp.broadcast_to(ddec, (CHUNK, hd))
            return carry

        lax.fori_loop(0, n_groups, grads_b, 0, unroll=GROUP_UNROLL)

        def finish(r0, n):
            rows = pl.ds(r0, n)
            sig, f, e_in, e_inv, e_out = [r[rows, :] for r in (sig_ref, f_ref, ein_ref, einv_ref, eout_ref)]
            q, k = q_ref[rows, :].astype(F32), 1.0 - f
            dqi, dki, dko = dqi_ref[rows, :], dki_ref[rows, :], dko_ref[rows, :]
            dq = dqi * e_in
            dk = dki * e_inv + dko * e_out
            dq_ref[rows, :] = dq.astype(ACT)
            dv_ref[rows, :] = dvv_ref[rows, :].astype(ACT)
            t_out = k * e_out * dko
            dc = q * dq - k * e_inv * dki - t_out
            _, dc_later = _chunk_cumsums(dc, pad_ref, n)
            t_incl, t_later = _chunk_cumsums(t_out, pad_ref, n)
            dlogf = dc + dc_later + t_incl + t_later + dct_ref[rows, :]
            df = dlogf / f - dk
            dfl_ref[rows, :] = (df * (1.0 - lb_ref[...]) * sig * (1.0 - sig)).astype(ACT)
            dlb_ref[...] += jnp.sum(df * (1.0 - sig), axis=0, keepdims=True)

        _for_row_blocks(L, finish)

    order = lambda h, b: (b, h)
    W = n_heads * hd
    act_out = jax.ShapeDtypeStruct((B, L, W), ACT)
    blk_out = pl.BlockSpec((None, L, hd), lambda h, b: (b, 0, h))
    return pl.pallas_call(
        body, name=name,
        out_shape=(act_out, act_out, act_out, act_out, jax.ShapeDtypeStruct((1, W), F32),
                   jax.ShapeDtypeStruct((1, hd), F32)),
        grid=(n_heads, B),
        in_specs=_hgrn_specs(L, hd, col_q, n_heads, order) + [
            pl.BlockSpec((None, L, hd), lambda h, b: (b, 0, h)),
            pl.BlockSpec((1, hd), lambda h, b: (0, h)), pl.BlockSpec((1, hd), lambda h, b: (0, 0))],
        out_specs=(blk_out, blk_out, blk_out, blk_out, pl.BlockSpec((1, hd), lambda h, b: (0, h)),
                   pl.BlockSpec((1, hd), lambda h, b: (0, 0))),
        scratch_shapes=scratch + [
            pltpu.VMEM((lp, hd), MXU), pltpu.VMEM((lp // CHUNK, hd, hd), F32)] + [pltpu.VMEM((lp, hd), F32)] * 10,
        compiler_params=_cparams())(p3, p3, p3, p3, dyb, lb, ng)


def _merge_fn(a, bm, ga, gb):
    return jax.nn.sigmoid(ga) * a + jax.nn.sigmoid(gb) * bm


def _merge_call(yb, a, p, h0, whp, wout, g2, col_ga, tm, name):
    T, D = h0.shape

    def body(yb_ref, a_ref, ga_ref, gb_ref, h0_ref, whp_ref, wout_ref, g2_ref, h1_ref, mg_ref, bm_ref, z2_ref):
        bm = _dot(yb_ref[...], whp_ref[...])
        mg = _merge_fn(a_ref[...].astype(F32), bm, ga_ref[...].astype(F32), gb_ref[...].astype(F32))
        h1 = h0_ref[...] + _dot(mg, wout_ref[...])
        h1_ref[...] = h1
        mg_ref[...] = mg.astype(ACT)
        bm_ref[...] = bm.astype(ACT)
        z2_ref[...] = _rms(h1, g2_ref[...]).astype(ACT)

    tile = pl.BlockSpec((tm, D), lambda i: (i, 0))
    full = pl.BlockSpec((D, D), lambda i: (0, 0))
    act = jax.ShapeDtypeStruct((T, D), ACT)
    return pl.pallas_call(
        body, name=name, out_shape=(jax.ShapeDtypeStruct((T, D), F32), act, act, act), grid=(T // tm,),
        in_specs=[tile, tile, pl.BlockSpec((tm, D), lambda i: (i, col_ga)),
                  pl.BlockSpec((tm, D), lambda i: (i, col_ga + 1)), tile, full, full,
                  pl.BlockSpec((1, D), lambda i: (0, 0))],
        out_specs=(tile, tile, tile, tile), compiler_params=_cparams())(yb, a, p, p, h0, whp, wout, g2)


def _merge_bwd_call(dmg, a, bm, p, col_ga, tm, name):
    T, D = dmg.shape

    def body(dmg_ref, a_ref, bm_ref, ga_ref, gb_ref, da_ref, dbm_ref, dga_ref, dgb_ref):
        args = [r[...].astype(F32) for r in (a_ref, bm_ref, ga_ref, gb_ref)]
        _, vjp = jax.vjp(_merge_fn, *args)
        for r, o in zip((da_ref, dbm_ref, dga_ref, dgb_ref), vjp(dmg_ref[...].astype(F32))):
            r[...] = o.astype(ACT)

    tile = pl.BlockSpec((tm, D), lambda i: (i, 0))
    act = jax.ShapeDtypeStruct((T, D), ACT)
    return pl.pallas_call(
        body, name=name, out_shape=(act, act, act, act), grid=(T // tm,),
        in_specs=[tile, tile, tile, pl.BlockSpec((tm, D), lambda i: (i, col_ga)),
                  pl.BlockSpec((tm, D), lambda i: (i, col_ga + 1))],
        out_specs=(tile, tile, tile, tile), compiler_params=_cparams())(dmg, a, bm, p, p)


def _conv_taps(x_ref, halo_ref, ext_ref, edge, tm, before):
    halo = jnp.where(edge, 0.0, halo_ref[...].astype(F32))
    x = x_ref[...].astype(F32)
    if before:
        ext_ref[0:PAD, :] = halo
        ext_ref[PAD:PAD + tm, :] = x
        return [ext_ref[PAD - 2 + k:PAD - 2 + k + tm, :] for k in range(3)]
    ext_ref[0:tm, :] = x
    ext_ref[tm:tm + PAD, :] = halo
    return [ext_ref[k:k + tm, :] for k in range(3)]


def _conv(taps, cw, cb):
    return cb + cw[0:1] * taps[0] + cw[1:2] * taps[1] + cw[2:3] * taps[2]


def _ffn_pair_specs(tm, F, T, n_pairs, order, before):
    hb = tm // PAD
    last = T // PAD - 1

    def halo_row(i):
        return jnp.maximum(i * hb - 1, 0) if before else jnp.minimum((i + 1) * hb, last)

    specs = []
    for off in (0, n_pairs):
        specs.append(pl.BlockSpec((None, tm, F), lambda *g, off=off: (order(*g)[1] + off, order(*g)[0], 0)))
        specs.append(pl.BlockSpec((None, PAD, F), lambda *g, off=off: (order(*g)[1] + off, halo_row(order(*g)[0]), 0)))
    return specs


def _ffn_fwd_call(up, cw, cb, wd, h1, tgt, g3, tm, tps, name):
    S, T, F = up.shape
    n_pairs = S // 2
    D = h1.shape[1]

    def body(ua_ref, ha_ref, ub_ref, hb_ref, cwa_ref, cwb_ref, cba_ref, cbb_ref, wd_ref, h1_ref, tgt_ref, g3_ref,
             act_ref, dh2_ref, loss_ref, dg3_ref, acc_ref, ext_ref):
        i, j = pl.program_id(0), pl.program_id(1)
        edge = (i % tps) == 0
        ua = _conv(_conv_taps(ua_ref, ha_ref, ext_ref, edge, tm, True), cwa_ref[...], cba_ref[...])
        ub = _conv(_conv_taps(ub_ref, hb_ref, ext_ref, edge, tm, True), cwb_ref[...], cbb_ref[...])
        act = _silu(ua) * ub
        act_ref[...] = act.astype(ACT)
        contrib = _dot(act, wd_ref[...])

        @pl.when(j == 0)
        def _():
            acc_ref[...] = h1_ref[...] + contrib

        @pl.when(j > 0)
        def _():
            acc_ref[...] += contrib

        @pl.when((i == 0) & (j == 0))
        def _():
            loss_ref[...] = jnp.zeros_like(loss_ref)
            dg3_ref[...] = jnp.zeros_like(dg3_ref)

        @pl.when(j == n_pairs - 1)
        def _():
            row = lax.broadcasted_iota(jnp.int32, (tm, 1), 0) + (i % tps) * tm
            valid = row >= N_META
            tgt = tgt_ref[...]

            def loss_fn(h2, g):
                err = _rms(h2, g) - tgt
                return 0.5 * jnp.sum(jnp.where(valid, err * err, 0.0)) / D

            loss, vjp = jax.vjp(loss_fn, acc_ref[...], g3_ref[...])
            dh2, dg3 = vjp(jnp.ones((), F32))
            dh2_ref[...] = dh2
            loss_ref[...] += loss
            dg3_ref[...] += dg3

    order = lambda i, j: (i, j)
    tile = pl.BlockSpec((tm, D), lambda i, j: (i, 0))
    vec = pl.BlockSpec((1, D), lambda i, j: (0, 0))
    return pl.pallas_call(
        body, name=name,
        out_shape=(jax.ShapeDtypeStruct((n_pairs, T, F), ACT), jax.ShapeDtypeStruct((T, D), F32),
                   jax.ShapeDtypeStruct((1, LANES), F32), jax.ShapeDtypeStruct((1, D), F32)),
        grid=(T // tm, n_pairs),
        in_specs=_ffn_pair_specs(tm, F, T, n_pairs, order, True) + [
            pl.BlockSpec((None, 3, F), lambda i, j: (j, 0, 0)), pl.BlockSpec((None, 3, F), lambda i, j: (j + n_pairs, 0, 0)),
            pl.BlockSpec((None, 1, F), lambda i, j: (j, 0, 0)), pl.BlockSpec((None, 1, F), lambda i, j: (j + n_pairs, 0, 0)),
            pl.BlockSpec((None, F, D), lambda i, j: (j, 0, 0)), tile, tile, vec],
        out_specs=(pl.BlockSpec((None, tm, F), lambda i, j: (j, i, 0)), tile,
                   pl.BlockSpec((1, LANES), lambda i, j: (0, 0)), vec),
        scratch_shapes=[pltpu.VMEM((tm, D), F32), pltpu.VMEM((tm + PAD, F), F32)],
        compiler_params=_cparams())(up, up, up, up, cw, cw, cb, cb, wd, h1, tgt, g3)


def _ffn_bwd_a_call(dh2, up, act, cw, cb, wd, tm, tps, name):
    S, T, F = up.shape
    n_pairs = S // 2
    D = dh2.shape[1]

    def body(dh2_ref, ua_ref, ha_ref, ub_ref, hb_ref, act_ref, cwa_ref, cwb_ref, cba_ref, cbb_ref, wd_ref,
             dua_ref, dub_ref, dwd_ref, dcwa_ref, dcwb_ref, dcba_ref, dcbb_ref, ext_ref):
        i = pl.program_id(1)
        edge = (i % tps) == 0

        @pl.when(i == 0)
        def _():
            for r in (dwd_ref, dcwa_ref, dcwb_ref, dcba_ref, dcbb_ref):
                r[...] = jnp.zeros_like(r)

        dh2 = dh2_ref[...]
        dact = _dot_nt(dh2, wd_ref[...])
        dwd_ref[...] += _dot_tn(act_ref[...], dh2)
        taps_a = _conv_taps(ua_ref, ha_ref, ext_ref, edge, tm, True)
        ua = _conv(taps_a, cwa_ref[...], cba_ref[...])
        sa = jax.nn.sigmoid(ua)
        dub = dact * ua * sa
        dcbb_ref[...] += jnp.sum(dub, axis=0, keepdims=True)
        taps_b = _conv_taps(ub_ref, hb_ref, ext_ref, edge, tm, True)
        dcwb_ref[...] += jnp.concatenate([jnp.sum(dub * t, axis=0, keepdims=True) for t in taps_b], axis=0)
        ub = _conv(taps_b, cwb_ref[...], cbb_ref[...])
        dua = dact * ub * sa * (1.0 + ua * (1.0 - sa))
        dcba_ref[...] += jnp.sum(dua, axis=0, keepdims=True)
        taps_a = _conv_taps(ua_ref, ha_ref, ext_ref, edge, tm, True)
        dcwa_ref[...] += jnp.concatenate([jnp.sum(dua * t, axis=0, keepdims=True) for t in taps_a], axis=0)
        dua_ref[...] = dua.astype(ACT)
        dub_ref[...] = dub.astype(ACT)

    order = lambda j, i: (i, j)
    sh = lambda rows: jax.ShapeDtypeStruct((n_pairs, rows, F), F32)
    par = lambda rows: pl.BlockSpec((None, rows, F), lambda j, i: (j, 0, 0))
    return pl.pallas_call(
        body, name=name,
        out_shape=(jax.ShapeDtypeStruct((n_pairs, T, F), ACT), jax.ShapeDtypeStruct((n_pairs, T, F), ACT),
                   jax.ShapeDtypeStruct((n_pairs, F, D), F32), sh(3), sh(3), sh(1), sh(1)),
        grid=(n_pairs, T // tm),
        in_specs=[pl.BlockSpec((tm, D), lambda j, i: (i, 0))] + _ffn_pair_specs(tm, F, T, n_pairs, order, True) + [
            pl.BlockSpec((None, tm, F), lambda j, i: (j, i, 0)),
            pl.BlockSpec((None, 3, F), lambda j, i: (j, 0, 0)), pl.BlockSpec((None, 3, F), lambda j, i: (j + n_pairs, 0, 0)),
            pl.BlockSpec((None, 1, F), lambda j, i: (j, 0, 0)), pl.BlockSpec((None, 1, F), lambda j, i: (j + n_pairs, 0, 0)),
            pl.BlockSpec((None, F, D), lambda j, i: (j, 0, 0))],
        out_specs=(pl.BlockSpec((None, tm, F), lambda j, i: (j, i, 0)), pl.BlockSpec((None, tm, F), lambda j, i: (j, i, 0)),
                   pl.BlockSpec((None, F, D), lambda j, i: (j, 0, 0)), par(3), par(3), par(1), par(1)),
        scratch_shapes=[pltpu.VMEM((tm + PAD, F), F32)],
        compiler_params=_cparams())(dh2, up, up, up, up, act, cw, cw, cb, cb, wd)


def _ffn_bwd_b_call(dua, dub, cw, wup, h1, g2, dh2, tm, tps, name):
    n_pairs, T, F = dua.shape
    D = h1.shape[1]
    hb = tm // PAD
    last = T // PAD - 1

    def body(da_ref, na_ref, db_ref, nb_ref, cwa_ref, cwb_ref, wa_ref, wb_ref, h1_ref, g2_ref, dh2_ref,
             dupa_ref, dupb_ref, dh1_ref, dg2_ref, acc_ref, ext_ref):
        i, j = pl.program_id(0), pl.program_id(1)
        edge = (i % tps) == tps - 1
        outs = []
        for d_ref, n_ref, cw_ref, o_ref in ((da_ref, na_ref, cwa_ref, dupa_ref), (db_ref, nb_ref, cwb_ref, dupb_ref)):
            t = _conv_taps(d_ref, n_ref, ext_ref, edge, tm, False)
            cwv = cw_ref[...]
            dup = cwv[2:3] * t[0] + cwv[1:2] * t[1] + cwv[0:1] * t[2]
            o_ref[...] = dup.astype(ACT)
            outs.append(dup)
        contrib = _dot_nt(outs[0], wa_ref[...]) + _dot_nt(outs[1], wb_ref[...])

        @pl.when(j == 0)
        def _():
            acc_ref[...] = contrib

        @pl.when(j > 0)
        def _():
            acc_ref[...] += contrib

        @pl.when((i == 0) & (j == 0))
        def _():
            dg2_ref[...] = jnp.zeros_like(dg2_ref)

        @pl.when(j == n_pairs - 1)
        def _():
            _, vjp = jax.vjp(_rms, h1_ref[...], g2_ref[...])
            dh, dg = vjp(acc_ref[...])
            dh1_ref[...] = dh2_ref[...] + dh
            dg2_ref[...] += dg

    tile = pl.BlockSpec((tm, D), lambda i, j: (i, 0))
    vec = pl.BlockSpec((1, D), lambda i, j: (0, 0))
    pair = lambda: [pl.BlockSpec((None, tm, F), lambda i, j: (j, i, 0)),
                    pl.BlockSpec((None, PAD, F), lambda i, j: (j, jnp.minimum((i + 1) * hb, last), 0))]
    act = jax.ShapeDtypeStruct((n_pairs, T, F), ACT)
    return pl.pallas_call(
        body, name=name,
        out_shape=(act, act, jax.ShapeDtypeStruct((T, D), F32), jax.ShapeDtypeStruct((1, D), F32)),
        grid=(T // tm, n_pairs),
        in_specs=pair() + pair() + [
            pl.BlockSpec((None, 3, F), lambda i, j: (j, 0, 0)), pl.BlockSpec((None, 3, F), lambda i, j: (j + n_pairs, 0, 0)),
            pl.BlockSpec((None, D, F), lambda i, j: (j, 0, 0)), pl.BlockSpec((None, D, F), lambda i, j: (j + n_pairs, 0, 0)),
            tile, vec, tile],
        out_specs=(pl.BlockSpec((None, tm, F), lambda i, j: (j, i, 0)), pl.BlockSpec((None, tm, F), lambda i, j: (j, i, 0)),
                   tile, vec),
        scratch_shapes=[pltpu.VMEM((tm, D), F32), pltpu.VMEM((tm + PAD, F), F32)],
        compiler_params=_cparams())(dua, dua, dub, dub, cw, cw, wup, wup, h1, g2, dh2)


def _in_bwd_call(dp, w_in, h0, g1, dh1, tm, name):
    T, D = h0.shape
    S, _, N = w_in.shape

    def body(dp_ref, w_ref, h0_ref, g1_ref, dh1_ref, dh0_ref, dg1_ref, acc_ref):
        i, j = pl.program_id(0), pl.program_id(1)
        contrib = _dot_nt(dp_ref[...], w_ref[...])

        @pl.when(j == 0)
        def _():
            acc_ref[...] = contrib

        @pl.when(j > 0)
        def _():
            acc_ref[...] += contrib

        @pl.when((i == 0) & (j == 0))
        def _():
            dg1_ref[...] = jnp.zeros_like(dg1_ref)

        @pl.when(j == S - 1)
        def _():
            _, vjp = jax.vjp(_rms, h0_ref[...], g1_ref[...])
            dh, dg = vjp(acc_ref[...])
            dh0_ref[...] = dh1_ref[...] + dh
            dg1_ref[...] += dg

    tile = pl.BlockSpec((tm, D), lambda i, j: (i, 0))
    vec = pl.BlockSpec((1, D), lambda i, j: (0, 0))
    return pl.pallas_call(
        body, name=name, out_shape=(jax.ShapeDtypeStruct((T, D), F32), jax.ShapeDtypeStruct((1, D), F32)),
        grid=(T // tm, S),
        in_specs=[pl.BlockSpec((tm, N), lambda i, j: (i, j)), pl.BlockSpec((None, D, N), lambda i, j: (j, 0, 0)),
                  tile, vec, tile],
        out_specs=(tile, vec), scratch_shapes=[pltpu.VMEM((tm, D), F32)],
        compiler_params=_cparams())(dp, w_in, h0, g1, dh1)


def _meta_grad_call(dh0_3, name):
    B, L, D = dh0_3.shape

    def body(d_ref, o_ref):
        o_ref[...] = jnp.sum(d_ref[...], axis=0)

    return pl.pallas_call(
        body, name=name, out_shape=jax.ShapeDtypeStruct((N_META, D), F32), grid=(1,),
        in_specs=[pl.BlockSpec((B, N_META, D), lambda i: (0, 0, 0))],
        out_specs=pl.BlockSpec((N_META, D), lambda i: (0, 0)), compiler_params=_cparams())(dh0_3)


_RELS = [(dx, dy, dc) for dx in (0, 1) for dy in (0, 1) for dc in (0, 1)][1:]


def _exchange_call(arrs, scatter, name):
    n = len(arrs)
    n_rel = len(_RELS)

    def body(*refs):
        ins, outs = refs[:n], refs[n:2 * n]
        send_sems, recv_sems, loc_sems = refs[2 * n:]
        x, y, c = lax.axis_index("x"), lax.axis_index("y"), lax.axis_index("c")
        me = 4 * x + 2 * y + c
        started = []
        for k in range(n):
            src_me = ins[k].at[me] if scatter else ins[k]
            loc = pltpu.make_async_copy(src_me, outs[k].at[me], loc_sems.at[k])
            loc.start()
            started.append(loc)
        waits = []
        for r, (dx, dy, dc) in enumerate(_RELS):
            px, py, pc = (x + dx) % 2, (y + dy) % 2, (c + dc) % 2
            pid = 4 * px + 2 * py + pc
            for k in range(n):
                s = k * n_rel + r
                src = ins[k].at[pid] if scatter else ins[k]
                cp = pltpu.make_async_remote_copy(
                    src_ref=src, dst_ref=outs[k].at[me], send_sem=send_sems.at[s], recv_sem=recv_sems.at[s],
                    device_id=(px, py, pc), device_id_type=pl.DeviceIdType.MESH)
                cp.start()
                waits.append(pltpu.make_async_remote_copy(
                    src_ref=src, dst_ref=outs[k].at[pid], send_sem=send_sems.at[s], recv_sem=recv_sems.at[s],
                    device_id=(px, py, pc), device_id_type=pl.DeviceIdType.MESH))
        for w in waits:
            w.wait_send()
            w.wait_recv()
        for loc in started:
            loc.wait()

    out_shape = tuple(jax.ShapeDtypeStruct(a.shape if scatter else (N_DEV,) + a.shape, a.dtype) for a in arrs)
    hbm = pl.BlockSpec(memory_space=pl.ANY)
    return pl.pallas_call(
        body, name=name, out_shape=out_shape, in_specs=[hbm] * n, out_specs=tuple([hbm] * n),
        scratch_shapes=[pltpu.SemaphoreType.DMA((n * n_rel,)), pltpu.SemaphoreType.DMA((n * n_rel,)),
                        pltpu.SemaphoreType.DMA((n,))],
        compiler_params=pltpu.CompilerParams(has_side_effects=True))(*arrs)


_HBM = pl.BlockSpec(memory_space=pltpu.HBM)
_SEM = pl.BlockSpec(memory_space=pltpu.SEMAPHORE)
_DATAFLOW = pltpu.SideEffectType.DATAFLOW_SIDE_EFFECTING


def _peer_copies(ins, lands, send_sems, recv_sems, scatter):
    n = len(ins)
    x, y, c = lax.axis_index("x"), lax.axis_index("y"), lax.axis_index("c")
    me = 4 * x + 2 * y + c
    sends, arrivals = [], []
    for r, (dx, dy, dc) in enumerate(_RELS):
        px, py, pc = (x + dx) % 2, (y + dy) % 2, (c + dc) % 2
        pid = 4 * px + 2 * py + pc
        for k in range(n):
            s = k * len(_RELS) + r
            src = ins[k].at[pid] if scatter else ins[k]
            for dst, out in ((lands[k].at[me], sends), (lands[k].at[pid], arrivals)):
                out.append(pltpu.make_async_remote_copy(
                    src_ref=src, dst_ref=dst, send_sem=send_sems.at[s], recv_sem=recv_sems.at[s],
                    device_id=(px, py, pc), device_id_type=pl.DeviceIdType.MESH))
    return sends, arrivals


def _exchange_start(arrs, scatter, name):
    n = len(arrs)
    n_sem = n * len(_RELS)

    def body(*refs):
        ins, lands = refs[:n], refs[n:2 * n]
        send_sems, recv_sems = refs[2 * n], refs[2 * n + 1]
        token = refs[-1]
        sends, _ = _peer_copies(ins, lands, send_sems, recv_sems, scatter)
        for cp in sends:
            cp.start()
        token[...] = jnp.zeros_like(token)

    land_shapes = [a.shape if scatter else (N_DEV,) + a.shape for a in arrs]
    ops = [pltpu.with_memory_space_constraint(a, pltpu.HBM) for a in arrs]
    ops += [pltpu.with_memory_space_constraint(lax.empty(s, a.dtype), pltpu.HBM) for s, a in zip(land_shapes, arrs)]
    out = pl.pallas_call(
        body, name=name,
        out_shape=(pltpu.SemaphoreType.DMA((n_sem,)), pltpu.SemaphoreType.DMA((n_sem,)),
                   *[pltpu.HBM(a.shape, a.dtype) for a in arrs],
                   *[pltpu.HBM(s, a.dtype) for s, a in zip(land_shapes, arrs)],
                   jax.ShapeDtypeStruct((SUBLANES, LANES), F32)),
        in_specs=[_HBM] * (2 * n),
        out_specs=(_SEM, _SEM, *[_HBM] * (2 * n), pl.BlockSpec(memory_space=pltpu.VMEM)),
        input_output_aliases={i: 2 + i for i in range(2 * n)},
        compiler_params=pltpu.CompilerParams(has_side_effects=_DATAFLOW))(*ops)
    return out[0], out[1], list(out[2:2 + n]), list(out[2 + n:2 + 2 * n]), out[-1]


def _exchange_wait(started, after, scatter, name):
    send_sems, recv_sems, srcs, lands, _ = started
    n = len(srcs)

    def body(*refs):
        ins, lands_ = refs[:n], refs[n:2 * n]
        _, arrivals = _peer_copies(ins, lands_, refs[2 * n], refs[2 * n + 1], scatter)
        for cp in arrivals:
            cp.wait_send()
            cp.wait_recv()

    out = pl.pallas_call(
        body, name=name,
        out_shape=(*[pltpu.HBM(a.shape, a.dtype) for a in srcs], *[pltpu.HBM(a.shape, a.dtype) for a in lands]),
        in_specs=[_HBM] * (2 * n) + [_SEM, _SEM, pl.BlockSpec(memory_space=pl.ANY)],
        out_specs=tuple([_HBM] * (2 * n)), input_output_aliases={i: i for i in range(2 * n)},
        compiler_params=pltpu.CompilerParams(has_side_effects=_DATAFLOW))(*srcs, *lands, send_sems, recv_sems, after)
    return list(out[:n]), list(out[n:])


def _place_own_call(srcs, lands, scatter, me, name):
    outs = []
    for k, (src, land) in enumerate(zip(srcs, lands)):
        R, C = land.shape[1:]
        tr = R
        while tr % 32 == 0 and tr * C * land.dtype.itemsize > 2 * 1024 * 1024:
            tr //= 2

        def body(me_ref, s_ref, l_ref, o_ref):
            o_ref[...] = s_ref[...]

        src_spec = (pl.BlockSpec((None, tr, C), lambda i, me_ref: (me_ref[0], i, 0)) if scatter
                    else pl.BlockSpec((tr, C), lambda i, me_ref: (i, 0)))
        outs.append(pl.pallas_call(
            body, name=f"{name}_{k}", out_shape=jax.ShapeDtypeStruct(land.shape, land.dtype),
            grid_spec=pltpu.PrefetchScalarGridSpec(
                num_scalar_prefetch=1, grid=(R // tr,),
                in_specs=[src_spec, pl.BlockSpec(memory_space=pl.ANY)],
                out_specs=pl.BlockSpec((None, tr, C), lambda i, me_ref: (me_ref[0], i, 0))),
            input_output_aliases={2: 0}, compiler_params=_cparams())(me, src, land))
    return outs


def _adamw_shard_call(w, parts, m, v, name):
    R, C = w.shape
    tr = _tile(R, 128) if R % 16 == 0 else R

    def body(w_ref, p_ref, m_ref, v_ref, g_ref, d_ref, nm_ref, nv_ref):
        g = p_ref[0].astype(F32)
        for s in range(1, N_DEV):
            g = g + p_ref[s].astype(F32)
        d, nm, nv = _adamw(w_ref[...], g, m_ref[...], v_ref[...])
        g_ref[...] = g
        d_ref[...] = d
        nm_ref[...] = nm
        nv_ref[...] = nv

    tile = pl.BlockSpec((tr, C), lambda i: (i, 0))
    sh = jax.ShapeDtypeStruct((R, C), F32)
    return pl.pallas_call(
        body, name=name, out_shape=(sh, sh, sh, sh), grid=(R // tr,),
        in_specs=[tile, pl.BlockSpec((N_DEV, tr, C), lambda i: (0, i, 0)), tile, tile],
        out_specs=(tile, tile, tile, tile), compiler_params=_cparams())(w, parts, m, v)


def _pack(arrs, rows_mult=SUBLANES):
    flat = jnp.concatenate([a.reshape(-1).astype(F32) for a in arrs])
    n = flat.shape[0]
    per = rows_mult * LANES
    total = -(-n // per) * per
    return jnp.pad(flat, (0, total - n)).reshape(total // LANES, LANES)


def _unpack(pack, shapes):
    flat = pack.reshape(-1)
    out, off = [], 0
    for s in shapes:
        n = 1
        for d in s:
            n *= d
        out.append(flat[off:off + n].reshape(s))
        off += n
    return out


def kernel(x, meta_tokens, mix_norm_g, w_in, ssm_lambda_re, ssm_lambda_im, ssm_log_dt, ssm_b_re, ssm_b_im, ssm_c_re, ssm_c_im, ssm_d, ssm_w_glu, w_ssm_proj, hgrn_lb_logits, hgrn_norm_g, w_hgrn_proj, w_out, ffn_norm_g, w_up, conv_w, conv_b, w_down, final_norm_g, loss_target, m_meta_tokens, m_mix_norm_g, m_w_in, m_ssm_lambda_re, m_ssm_lambda_im, m_ssm_log_dt, m_ssm_b_re, m_ssm_b_im, m_ssm_c_re, m_ssm_c_im, m_ssm_d, m_ssm_w_glu, m_w_ssm_proj, m_hgrn_lb_logits, m_hgrn_norm_g, m_w_hgrn_proj, m_w_out, m_ffn_norm_g, m_w_up, m_conv_w, m_conv_b, m_w_down, m_final_norm_g, v_meta_tokens, v_mix_norm_g, v_w_in, v_ssm_lambda_re, v_ssm_lambda_im, v_ssm_log_dt, v_ssm_b_re, v_ssm_b_im, v_ssm_c_re, v_ssm_c_im, v_ssm_d, v_ssm_w_glu, v_w_ssm_proj, v_hgrn_lb_logits, v_hgrn_norm_g, v_w_hgrn_proj, v_w_out, v_ffn_norm_g, v_w_up, v_conv_w, v_conv_b, v_w_down, v_final_norm_g):
    args = dict(locals())
    B, S_len, D = x.shape
    L = S_len + N_META
    T = B * L
    tm = _tile(L, ROW_TILE_CAP)
    tps = L // tm
    G, P = ssm_lambda_re.shape[1:]
    H = ssm_b_re.shape[-1]
    W = G * H
    n_cb = W // LANES
    gpb = G // n_cb
    hd = hgrn_norm_g.shape[1]
    n_heads = D // hd
    n_in = w_in.shape[2]
    F = w_up.shape[2]
    assert W == D and n_in % LANES == 0

    me = (4 * lax.axis_index("x") + 2 * lax.axis_index("y") + lax.axis_index("c")).astype(jnp.int32).reshape(1)
    meta_g, cw_g = _exchange_call([meta_tokens, conv_w[0]], False, "gather_small_params")
    ga = _exchange_start([w_in[0].astype(MXU)], False, "gather_a_start")
    gb = _exchange_start(
        [w_up[0].astype(MXU), ssm_w_glu[0].astype(MXU), w_ssm_proj[0].astype(MXU), w_hgrn_proj[0].astype(MXU),
         w_out[0].astype(MXU), w_down[0].astype(MXU)], False, "gather_b_start")
    started_tok = (ga[4] + gb[4])[0:1, 0:1]
    meta_full = meta_g.transpose(1, 0, 2).reshape(N_META, D)
    cb_g = conv_b.reshape(N_DEV, 1, F)

    h0 = jnp.concatenate([jnp.broadcast_to(meta_full[None], (B, N_META, D)), x], axis=1).reshape(T, D)
    tgt = jnp.concatenate([jnp.zeros((B, N_META, D), F32), loss_target], axis=1).reshape(T, D)

    lr, li = ssm_lambda_re[0], ssm_lambda_im[0]
    ldt = ssm_log_dt[0].reshape(G, 1)
    bt_re = ssm_b_re[0].transpose(2, 0, 1).reshape(H, G * P)
    bt_im = ssm_b_im[0].transpose(2, 0, 1).reshape(H, G * P)
    seg = _seg_len(L)
    a_re, a_im, as_re, as_im, coef_re, coef_im = _small_call(
        _disc_a_power(seg), [lr, li, ldt], [((G, P), F32)] * 6, "s5_discretise")
    bbt_re, bbt_im = _small_call(
        _disc_b, [coef_re.reshape(1, G * P), coef_im.reshape(1, G * P), bt_re, bt_im],
        [((H, G * P), F32)] * 2, "s5_input_matrix")
    eye = jnp.eye(gpb, dtype=F32)
    hw = gpb * P

    def expand_b(bbt):
        t = bbt.reshape(H, n_cb, gpb, P).transpose(1, 0, 2, 3)[:, None]
        return (t * eye[None, :, None, :, None]).reshape(n_cb, gpb * H, hw)

    def expand_c(cm):
        t = cm.reshape(n_cb, gpb, H, P).transpose(0, 1, 3, 2)[:, :, :, None]
        return (t * eye[None, :, None, :, None]).reshape(n_cb, hw, gpb * H)

    wb = jnp.concatenate([expand_b(bbt_re), expand_b(bbt_im)], axis=2).astype(MXU)
    wc = jnp.concatenate([expand_c(ssm_c_re[0]), -expand_c(ssm_c_im[0])], axis=1).astype(MXU)
    tab = jnp.stack([jnp.concatenate([a_re.reshape(n_cb, hw), a_im.reshape(n_cb, hw)], axis=1),
                     jnp.concatenate([as_re.reshape(n_cb, hw), as_im.reshape(n_cb, hw)], axis=1)], axis=1)
    tab = jnp.broadcast_to(tab[:, :, None, :], (n_cb, 2, SUBLANES, 2 * hw))
    dsk = ssm_d.reshape(n_cb, 1, LANES)
    lb = _small_call(_lb_fn, [hgrn_lb_logits], [((1, D), F32)], "hgrn_lower_bound")[0]

    z1 = _norm_call(h0, mix_norm_g + started_tok, tm, "mix_norm")
    ga_src, ga_land = _exchange_wait(ga, z1, False, "gather_a_wait")
    win_g = _place_own_call(ga_src, ga_land, False, me, "gather_a_own")[0]
    p = _mm_shard(z1, win_g, tm, "in_proj", False)
    p3 = p.reshape(B, L, p.shape[1])
    u_seg = _to_segments(p3[:, :, :W], seg)
    ya_seg, s_all = _s5_fwd_call(u_seg, wb, wc, tab, dsk, "s5_fwd")
    ya = _from_segments(ya_seg, seg, L).reshape(T, W)
    gb_src, gb_land = _exchange_wait(gb, ya, False, "gather_b_wait")
    gathered = _place_own_call(gb_src, gb_land, False, me, "gather_b_own")
    wup_g = gathered[0]
    wglu_g, wsp_g, whp_g, wout_g = [g.reshape(D, D) for g in gathered[1:5]]
    wdn_g = gathered[5].reshape(N_DEV // 2, 2 * w_down.shape[1], D)
    yo, a_br = _glu_proj_call(ya, wglu_g, wsp_g, tm, "s5_glu_proj")
    yb = _hgrn_fwd_call(p3, lb, hgrn_norm_g, n_heads, n_cb, "hgrn_fwd").reshape(T, D)
    col_ga = 5
    h1, mg, bm, z2 = _merge_call(yb, a_br, p, h0, whp_g, wout_g, ffn_norm_g, col_ga, tm, "merge")
    up = _mm_shard(z2, wup_g, tm, "up_proj", True)
    act, dh2, loss_part, dg3 = _ffn_fwd_call(up, cw_g, cb_g, wdn_g, h1, tgt, final_norm_g.reshape(1, D),
                                             tm, tps, "ffn_out_loss")

    dua, dub, dwd, dcwa, dcwb, dcba, dcbb = _ffn_bwd_a_call(dh2, up, act, cw_g, cb_g, wdn_g, tm, tps, "ffn_bwd_gate")
    dupa, dupb, dh1, dg2 = _ffn_bwd_b_call(dua, dub, cw_g, wup_g, h1, ffn_norm_g, dh2, tm, tps, "ffn_bwd_up")
    dwup = jnp.concatenate([_mm_tn(z2, dupa, N_DEV // 2, tm, "dw_up_a", True),
                            _mm_tn(z2, dupb, N_DEV // 2, tm, "dw_up_b", True)], axis=0)
    sh_rows = D // N_DEV
    sa = _exchange_start([dwup.astype(WIRE), dwd.reshape(N_DEV, w_down.shape[1], D).astype(WIRE)], True,
                         "scatter_a_start")
    dmg, dwout = _lin_bwd(mg, dh1, wout_g + sa[4][0:1, 0:1].astype(MXU), tm, "out_proj_bwd")
    da_br, dbm, dga, dgb = _merge_bwd_call(dmg, a_br, bm, p, col_ga, tm, "merge_bwd")
    dyo, dwsp = _lin_bwd(yo, da_br, wsp_g, tm, "ssm_proj_bwd")
    dyb, dwhp = _lin_bwd(yb, dbm, whp_g, tm, "hgrn_proj_bwd")
    dya, dwglu = _glu_bwd_call(ya, dyo, wglu_g, tm, "s5_glu_bwd")
    sb = _exchange_start([t.reshape(N_DEV, sh_rows, D).astype(WIRE) for t in (dwglu, dwsp, dwhp, dwout)], True,
                         "scatter_b_start")
    tok_b = sb[4][0:1, :]
    du_seg, dwb, dwc, dab, ddsk = _s5_bwd_call(u_seg, s_all, _to_segments(dya.reshape(B, L, W), seg), wb, wc, tab,
                                               dsk + tok_b[None], "s5_bwd")
    du = _from_segments(du_seg, seg, L)

    def diag_b(dw):
        t = (dw.reshape(n_cb, gpb, H, gpb, P) * eye[None, :, None, :, None]).sum(axis=1)
        return t.transpose(1, 0, 2, 3).reshape(H, G * P)

    def diag_c(dw):
        t = (dw.reshape(n_cb, gpb, P, gpb, H) * eye[None, :, None, :, None]).sum(axis=3)
        return t.transpose(0, 1, 3, 2).reshape(G, H, P)

    early_parts = [dab[:, 0, :hw].reshape(G, P), dab[:, 0, hw:].reshape(G, P),
                   diag_b(dwb[:, :, :hw]), diag_b(dwb[:, :, hw:]),
                   diag_c(dwc[:, :hw]), -diag_c(dwc[:, hw:]), ddsk.reshape(1, D)]
    early_pack = _pack(early_parts)
    se = _exchange_start([early_pack], False, "gather_s5_grads_start")
    dq, dfl, di, dog, dlb, dng = _hgrn_bwd_call(p3, dyb.reshape(B, L, D), lb, hgrn_norm_g + tok_b + se[4][0:1, :],
                                                n_heads, n_cb, "hgrn_bwd")
    dp = jnp.concatenate([du.reshape(T, W), dq.reshape(T, D), dfl.reshape(T, D), di.reshape(T, D),
                          dog.reshape(T, D), dga, dgb], axis=1)
    dwin = _mm_tn(z1, dp, N_DEV, tm, "dw_in", False)
    sc = _exchange_start([dwin.astype(WIRE)], True, "scatter_c_start")
    dh0, dg1 = _in_bwd_call(dp, win_g, h0, mix_norm_g + sc[4][0:1, 0:1], dh1, tm, "in_proj_bwd")
    dh0_3 = dh0.reshape(B, L, D)
    grad_x = dh0_3[:, N_META:]
    dmeta = _meta_grad_call(dh0_3, "meta_grad")

    late_parts = [dg1, dlb, dng, dg2, jnp.concatenate([dcba, dcbb], axis=0).reshape(1, N_DEV * F), dg3, loss_part]
    late_pack = _pack(late_parts)

    dcw = jnp.concatenate([dcwa, dcwb], axis=0)
    dmeta_s = dmeta.reshape(N_META, N_DEV, D // N_DEV).transpose(1, 0, 2)
    parts_d = _exchange_call([dmeta_s, dcw], True, "scatter_small_grads")
    late_all = _exchange_call([late_pack], False, "gather_small_grads")[0]
    early_all = _place_own_call(*_exchange_wait(se, late_all, False, "gather_s5_grads_wait"), False, me,
                                "gather_s5_grads_own")[0]
    parts_a = _place_own_call(*_exchange_wait(sa, late_all, True, "scatter_a_wait"), True, me, "scatter_a_own")
    parts_b = _place_own_call(*_exchange_wait(sb, late_all, True, "scatter_b_wait"), True, me, "scatter_b_own")
    parts_c = _place_own_call(*_exchange_wait(sc, late_all, True, "scatter_c_wait"), True, me, "scatter_c_own")
    parts = [parts_c[0], parts_a[0], *parts_b, parts_a[1], parts_d[0], parts_d[1]]

    def sum8(a, b):
        ta, tb = a[0], b[0]
        for s in range(1, N_DEV):
            ta, tb = ta + a[s], tb + b[s]
        return ta, tb

    early_sum, late_sum = _small_call(sum8, [early_all, late_all], [(early_pack.shape, F32), (late_pack.shape, F32)],
                                      "sum_small_grads")
    t_abr, t_abi, t_bbr, t_bbi, g_cre, g_cim, g_dsk = _unpack(early_sum, [a.shape for a in early_parts])
    g_g1, t_lb, g_ng, g_g2, g_cb, g_g3, loss_v = _unpack(late_sum, [a.shape for a in late_parts])

    def disc_b_bwd(cr, ci, br, bi, dbr, dbi):
        _, vjp = jax.vjp(_disc_b, cr, ci, br, bi)
        return vjp((dbr, dbi))

    t_cr, t_ci, g_btr, g_bti = _small_call(
        disc_b_bwd, [coef_re.reshape(1, G * P), coef_im.reshape(1, G * P), bt_re, bt_im, t_bbr, t_bbi],
        [((1, G * P), F32)] * 2 + [((H, G * P), F32)] * 2, "s5_input_matrix_bwd")

    def disc_a_bwd(lr_, li_, ldt_, dar, dai, dcr, dci):
        _, vjp = jax.vjp(_disc_a, lr_, li_, ldt_)
        return vjp((dar, dai, dcr, dci))

    g_lr, g_li, g_ldt = _small_call(
        disc_a_bwd, [lr, li, ldt, t_abr, t_abi, t_cr.reshape(G, P), t_ci.reshape(G, P)],
        [((G, P), F32)] * 2 + [((G, 1), F32)], "s5_discretise_bwd")

    def lb_bwd(logits, d):
        _, vjp = jax.vjp(_lb_fn, logits)
        return vjp(d)

    g_lbl = _small_call(lb_bwd, [hgrn_lb_logits, t_lb], [(hgrn_lb_logits.shape, F32)], "hgrn_lower_bound_bwd")[0]

    grads = dict(
        mix_norm_g=g_g1, ssm_lambda_re=g_lr[None], ssm_lambda_im=g_li[None], ssm_log_dt=g_ldt.reshape(1, G),
        ssm_b_re=g_btr.reshape(H, G, P).transpose(1, 2, 0)[None], ssm_b_im=g_bti.reshape(H, G, P).transpose(1, 2, 0)[None],
        ssm_c_re=g_cre[None], ssm_c_im=g_cim[None], ssm_d=g_dsk, hgrn_lb_logits=g_lbl, hgrn_norm_g=g_ng,
        ffn_norm_g=g_g2, conv_b=g_cb.reshape(1, N_DEV * F), final_norm_g=g_g3.reshape(D))
    loss = loss_v[0, 0]

    delta, new_m, new_v = {}, {}, {}
    sharded = [("w_in", parts[0], (D, n_in)), ("w_up", parts[1], (D, F)), ("ssm_w_glu", parts[2], (sh_rows, D)),
               ("w_ssm_proj", parts[3], (sh_rows, D)), ("w_hgrn_proj", parts[4], (sh_rows, D)),
               ("w_out", parts[5], (sh_rows, D)), ("w_down", parts[6], (w_down.shape[1], D)),
               ("meta_tokens", parts[7], (N_META, D // N_DEV)), ("conv_w", parts[8], (3, F))]
    for name, part, shp in sharded:
        full = args[name].shape
        g, d_, nm, nv = _adamw_shard_call(args[name].reshape(shp), part, args["m_" + name].reshape(shp),
                                          args["v_" + name].reshape(shp), "adamw_" + name)
        grads[name], delta[name], new_m[name], new_v[name] = [t.reshape(full) for t in (g, d_, nm, nv)]

    rep = ["mix_norm_g", "ssm_lambda_re", "ssm_lambda_im", "ssm_log_dt", "ssm_b_re", "ssm_b_im", "ssm_c_re",
           "ssm_c_im", "ssm_d", "hgrn_lb_logits", "hgrn_norm_g", "ffn_norm_g", "conv_b", "final_norm_g"]
    rep_shapes = [args[n].shape for n in rep]
    packs = [_pack([args[pre + n] for n in rep]) for pre in ("", "m_", "v_")]
    g_pack = _pack([grads[n] for n in rep])
    outs = _small_call(lambda w, g, m, v: _adamw(w, g, m, v), [packs[0], g_pack, packs[1], packs[2]],
                       [(g_pack.shape, F32)] * 3, "adamw_replicated")
    for n, d_, nm, nv in zip(rep, *[_unpack(o, rep_shapes) for o in outs]):
        delta[n], new_m[n], new_v[n] = d_, nm, nv

    names = ["meta_tokens", "mix_norm_g", "w_in", "ssm_lambda_re", "ssm_lambda_im", "ssm_log_dt", "ssm_b_re",
             "ssm_b_im", "ssm_c_re", "ssm_c_im", "ssm_d", "ssm_w_glu", "w_ssm_proj", "hgrn_lb_logits", "hgrn_norm_g",
             "w_hgrn_proj", "w_out", "ffn_norm_g", "w_up", "conv_w", "conv_b", "w_down", "final_norm_g"]
    return (loss, grad_x, *[grads[n] for n in names], *[delta[n] for n in names],
            *[new_m[n] for n in names], *[new_v[n] for n in names])
```

```python
import functools

import jax
import jax.numpy as jnp
from jax import lax
from jax.experimental import pallas as pl
from jax.experimental.pallas import tpu as pltpu

F32 = jnp.float32
MXU = jnp.bfloat16
ACT = jnp.bfloat16
WIRE = jnp.bfloat16
N_DEV = 8
N_META = 16
CHUNK = 16
EPS = 1e-6
ADAM_LR, ADAM_B1, ADAM_B2, ADAM_EPS, ADAM_WD, ADAM_STEP = 0.001, 0.9, 0.999, 1e-08, 0.01, 10
SUBLANES = 8
LANES = 128
ROW_TILE_CAP = 700
VMEM_LIMIT = 60 * 1024 * 1024


def _cparams(**kw):
    return pltpu.CompilerParams(vmem_limit_bytes=VMEM_LIMIT, **kw)


def _tile(n, cap):
    best = None
    for t in range(16, min(n, cap) + 1, 16):
        if n % t == 0:
            best = t
    assert best is not None, (n, cap)
    return best


def _dot(a, b):
    return lax.dot_general(a.astype(MXU), b.astype(MXU), (((1,), (0,)), ((), ())), preferred_element_type=F32)


def _dot_nt(a, b):
    return lax.dot_general(a.astype(MXU), b.astype(MXU), (((1,), (1,)), ((), ())), preferred_element_type=F32)


def _dot_tn(a, b):
    return lax.dot_general(a.astype(MXU), b.astype(MXU), (((0,), (0,)), ((), ())), preferred_element_type=F32)


def _rms(x, g):
    return x * lax.rsqrt(jnp.mean(x * x, axis=-1, keepdims=True) + EPS) * g


def _silu(x):
    return x * jax.nn.sigmoid(x)


def _small_call(fn, ins, out_shapes, name):
    n_in = len(ins)

    def body(*refs):
        outs = fn(*[r[...] for r in refs[:n_in]])
        outs = outs if isinstance(outs, (tuple, list)) else (outs,)
        for r, o in zip(refs[n_in:], outs):
            r[...] = o.astype(r.dtype)

    vm = pl.BlockSpec(memory_space=pltpu.VMEM)
    return pl.pallas_call(
        body, name=name, out_shape=tuple(jax.ShapeDtypeStruct(s, d) for s, d in out_shapes),
        in_specs=[vm] * n_in, out_specs=tuple([vm] * len(out_shapes)), compiler_params=_cparams())(*ins)


def _disc_a(lr, li, ldt):
    dt = jnp.exp(ldt)
    mag = jnp.exp(lr * dt)
    ab_re = mag * jnp.cos(li * dt)
    ab_im = mag * jnp.sin(li * dt)
    den = lr * lr + li * li
    nr = ab_re - 1.0
    coef_re = (nr * lr + ab_im * li) / den
    coef_im = (ab_im * lr - nr * li) / den
    return ab_re, ab_im, coef_re, coef_im


def _disc_a_power(n):
    def fn(lr, li, ldt):
        ab_re, ab_im, coef_re, coef_im = _disc_a(lr, li, ldt)
        pr, pi, sr, si, m = None, None, ab_re, ab_im, n
        while m:
            if m & 1:
                pr, pi = (sr, si) if pr is None else (pr * sr - pi * si, pr * si + pi * sr)
            m >>= 1
            if m:
                sr, si = sr * sr - si * si, 2.0 * sr * si
        return ab_re, ab_im, pr, pi, coef_re, coef_im
    return fn


def _disc_b(coef_re, coef_im, bt_re, bt_im):
    return coef_re * bt_re - coef_im * bt_im, coef_re * bt_im + coef_im * bt_re


def _lb_fn(logits):
    return jax.nn.softmax(logits, axis=0)[0:1]


def _adamw(w, g, m, v):
    m = ADAM_B1 * m + (1.0 - ADAM_B1) * g
    v = ADAM_B2 * v + (1.0 - ADAM_B2) * jnp.square(g)
    m_hat = m / (1.0 - ADAM_B1 ** ADAM_STEP)
    v_hat = v / (1.0 - ADAM_B2 ** ADAM_STEP)
    delta = -ADAM_LR * (m_hat / (jnp.sqrt(v_hat) + ADAM_EPS) + ADAM_WD * w)
    return delta, m, v


def _norm_call(h, g, tm, name):
    T, D = h.shape

    def body(h_ref, g_ref, z_ref):
        z_ref[...] = _rms(h_ref[...], g_ref[...]).astype(ACT)

    return pl.pallas_call(
        body, name=name, out_shape=jax.ShapeDtypeStruct((T, D), ACT), grid=(T // tm,),
        in_specs=[pl.BlockSpec((tm, D), lambda i: (i, 0)), pl.BlockSpec((1, D), lambda i: (0, 0))],
        out_specs=pl.BlockSpec((tm, D), lambda i: (i, 0)), compiler_params=_cparams())(h, g)


def _mm_shard(x, w, tm, name, major):
    T, K = x.shape
    S, _, N = w.shape

    def body(x_ref, w_ref, o_ref):
        o_ref[...] = _dot(x_ref[...], w_ref[...]).astype(o_ref.dtype)

    if major:
        out_shape = jax.ShapeDtypeStruct((S, T, N), ACT)
        out_spec = pl.BlockSpec((None, tm, N), lambda j, i: (j, i, 0))
    else:
        out_shape = jax.ShapeDtypeStruct((T, S * N), ACT)
        out_spec = pl.BlockSpec((tm, N), lambda j, i: (i, j))
    return pl.pallas_call(
        body, name=name, out_shape=out_shape, grid=(S, T // tm),
        in_specs=[pl.BlockSpec((tm, K), lambda j, i: (i, 0)), pl.BlockSpec((None, K, N), lambda j, i: (j, 0, 0))],
        out_specs=out_spec, compiler_params=_cparams())(x, w)


def _mm_tn(x, y, n_shards, tm, name, major):
    T, K = x.shape
    S = n_shards
    N = y.shape[-1] if major else y.shape[-1] // S

    def body(x_ref, y_ref, o_ref):
        @pl.when(pl.program_id(1) == 0)
        def _():
            o_ref[...] = jnp.zeros_like(o_ref)
        o_ref[...] += _dot_tn(x_ref[...], y_ref[...])

    y_spec = (pl.BlockSpec((None, tm, N), lambda j, i: (j, i, 0)) if major
              else pl.BlockSpec((tm, N), lambda j, i: (i, j)))
    return pl.pallas_call(
        body, name=name, out_shape=jax.ShapeDtypeStruct((S, K, N), F32), grid=(S, T // tm),
        in_specs=[pl.BlockSpec((tm, K), lambda j, i: (i, 0)), y_spec],
        out_specs=pl.BlockSpec((None, K, N), lambda j, i: (j, 0, 0)), compiler_params=_cparams())(x, y)


def _lin_bwd(x, dy, w, tm, name):
    T, K = x.shape
    N = dy.shape[1]

    def body(x_ref, dy_ref, w_ref, dx_ref, dw_ref):
        @pl.when(pl.program_id(0) == 0)
        def _():
            dw_ref[...] = jnp.zeros_like(dw_ref)
        dy = dy_ref[...]
        dx_ref[...] = _dot_nt(dy, w_ref[...]).astype(dx_ref.dtype)
        dw_ref[...] += _dot_tn(x_ref[...], dy)

    return pl.pallas_call(
        body, name=name,
        out_shape=(jax.ShapeDtypeStruct((T, K), ACT), jax.ShapeDtypeStruct((K, N), F32)), grid=(T // tm,),
        in_specs=[pl.BlockSpec((tm, K), lambda i: (i, 0)), pl.BlockSpec((tm, N), lambda i: (i, 0)),
                  pl.BlockSpec((K, N), lambda i: (0, 0))],
        out_specs=(pl.BlockSpec((tm, K), lambda i: (i, 0)), pl.BlockSpec((K, N), lambda i: (0, 0))),
        compiler_params=_cparams())(x, dy, w)


N_SEG = SUBLANES


def _seg_len(L):
    return -(-L // (N_SEG * SUBLANES)) * SUBLANES


def _to_segments(a3, seg):
    b, length, c = a3.shape
    a = jnp.pad(a3, ((0, 0), (0, N_SEG * seg - length), (0, 0)))
    return a.reshape(b, N_SEG, seg, c).transpose(0, 2, 1, 3).reshape(b, N_SEG * seg, c)


def _from_segments(a3, seg, length):
    b, _, c = a3.shape
    return a3.reshape(b, seg, N_SEG, c).transpose(0, 2, 1, 3).reshape(b, N_SEG * seg, c)[:, :length]


def _seg_scan(x_ref, tab_ref, n_slabs, reverse):
    hw = x_ref.shape[1] // 2
    sign = -1.0 if reverse else 1.0
    ar, ai = tab_ref[0][:, :hw], sign * tab_ref[0][:, hw:]
    br, bi = tab_ref[1][:, :hw], sign * tab_ref[1][:, hw:]

    def slab(k):
        kk = (n_slabs - 1 - k) if reverse else k
        return pl.ds(pl.multiple_of(kk * SUBLANES, SUBLANES), SUBLANES)

    def horner(k, carry):
        cr, ci = carry
        x = x_ref[slab(k), :]
        return ar * cr - ai * ci + x[:, :hw], ar * ci + ai * cr + x[:, hw:]

    z = jnp.zeros((SUBLANES, hw), F32)
    fr, fi = lax.fori_loop(0, n_slabs, horner, (z, z))

    row = lax.broadcasted_iota(jnp.int32, (SUBLANES, hw), 0)
    edge = (row == SUBLANES - 1) if reverse else (row == 0)
    shift = SUBLANES - 1 if reverse else 1
    sr, si = z, z
    for _ in range(N_SEG - 1):
        er, ei = fr + br * sr - bi * si, fi + br * si + bi * sr
        sr = jnp.where(edge, 0.0, pltpu.roll(er, shift, 0))
        si = jnp.where(edge, 0.0, pltpu.roll(ei, shift, 0))

    def scan(k, carry):
        cr, ci = carry
        rows = slab(k)
        x = x_ref[rows, :]
        nr, ni = ar * cr - ai * ci + x[:, :hw], ar * ci + ai * cr + x[:, hw:]
        x_ref[rows, 0:hw] = nr
        x_ref[rows, hw:2 * hw] = ni
        return nr, ni

    lax.fori_loop(0, n_slabs, scan, (sr, si))


def _s5_fwd_call(p3, wb, wc, tab_f, dsk, name):
    B, L, _ = p3.shape
    n_cb, cw, sw = wb.shape

    def body(u_ref, wb_ref, wc_ref, tab_ref, d_ref, ya_ref, so_ref, s_ref):
        u = u_ref[...]
        s_ref[...] = _dot(u, wb_ref[...])
        _seg_scan(s_ref, tab_ref, L // SUBLANES, False)
        s = s_ref[...].astype(MXU)
        so_ref[...] = s
        y = _dot(s, wc_ref[...]) + d_ref[...] * u.astype(F32)
        ya_ref[...] = jax.nn.gelu(y).astype(ACT)

    return pl.pallas_call(
        body, name=name,
        out_shape=(jax.ShapeDtypeStruct((B, L, n_cb * cw), ACT), jax.ShapeDtypeStruct((B, n_cb, L, sw), MXU)),
        grid=(B, n_cb),
        in_specs=[pl.BlockSpec((None, L, cw), lambda b, c: (b, 0, c)),
                  pl.BlockSpec((None, cw, sw), lambda b, c: (c, 0, 0)),
                  pl.BlockSpec((None, sw, cw), lambda b, c: (c, 0, 0)),
                  pl.BlockSpec((None, 2, SUBLANES, sw), lambda b, c: (c, 0, 0, 0)),
                  pl.BlockSpec((None, 1, cw), lambda b, c: (c, 0, 0))],
        out_specs=(pl.BlockSpec((None, L, cw), lambda b, c: (b, 0, c)),
                   pl.BlockSpec((None, None, L, sw), lambda b, c: (b, c, 0, 0))),
        scratch_shapes=[pltpu.VMEM((L, sw), F32)], compiler_params=_cparams())(p3, wb, wc, tab_f, dsk)


def _s5_bwd_call(p3, s_all, dya, wb, wc, tab_r, dsk, name):
    B, L, _ = p3.shape
    n_cb, cw, sw = wb.shape
    hw = sw // 2
    n_slabs = L // SUBLANES

    def body(u_ref, si_ref, dya_ref, wb_ref, wc_ref, tr_ref, d_ref,
             du_ref, dwb_ref, dwc_ref, da_ref, dd_ref, s_ref, l_ref):
        @pl.when(pl.program_id(1) == 0)
        def _():
            dwb_ref[...] = jnp.zeros_like(dwb_ref)
            dwc_ref[...] = jnp.zeros_like(dwc_ref)
            da_ref[...] = jnp.zeros_like(da_ref)
            dd_ref[...] = jnp.zeros_like(dd_ref)

        u = u_ref[...]
        uf = u.astype(F32)
        s_in = si_ref[...]
        s_ref[...] = s_in.astype(F32)
        y = _dot(s_in, wc_ref[...]) + d_ref[...] * uf
        _, gelu_vjp = jax.vjp(jax.nn.gelu, y)
        dy = gelu_vjp(dya_ref[...].astype(F32))[0]
        dd_ref[...] += jnp.sum(dy * uf, axis=0, keepdims=True)
        l_ref[...] = _dot_nt(dy, wc_ref[...])
        _seg_scan(l_ref, tr_ref, n_slabs, True)
        du_ref[...] = (_dot_nt(l_ref[...], wb_ref[...]) + d_ref[...] * dy).astype(ACT)
        dwb_ref[...] += _dot_tn(u, l_ref[...])
        dwc_ref[...] += _dot_tn(s_in, dy)

        row = lax.broadcasted_iota(jnp.int32, (SUBLANES, hw), 0)
        last = s_ref[pl.ds((n_slabs - 1) * SUBLANES, SUBLANES), :]
        p0r = jnp.where(row == 0, 0.0, pltpu.roll(last[:, :hw], 1, 0))
        p0i = jnp.where(row == 0, 0.0, pltpu.roll(last[:, hw:], 1, 0))

        def step(k, carry):
            qr, qi, accr, acci = carry
            r0 = pl.multiple_of(k * SUBLANES, SUBLANES)
            s = s_ref[pl.ds(r0, SUBLANES), :]
            lam = l_ref[pl.ds(r0, SUBLANES), :]
            lr, li = lam[:, :hw], lam[:, hw:]
            accr = accr + lr * qr + li * qi
            acci = acci + li * qr - lr * qi
            return s[:, :hw], s[:, hw:], accr, acci

        z8 = jnp.zeros((SUBLANES, hw), F32)
        _, _, accr, acci = lax.fori_loop(0, n_slabs, step, (p0r, p0i, z8, z8))
        da_ref[...] += jnp.concatenate([jnp.sum(accr, axis=0, keepdims=True),
                                        jnp.sum(acci, axis=0, keepdims=True)], axis=1)

    W = n_cb * cw
    return pl.pallas_call(
        body, name=name,
        out_shape=(jax.ShapeDtypeStruct((B, L, W), ACT), jax.ShapeDtypeStruct((n_cb, cw, sw), F32),
                   jax.ShapeDtypeStruct((n_cb, sw, cw), F32), jax.ShapeDtypeStruct((n_cb, 1, sw), F32),
                   jax.ShapeDtypeStruct((n_cb, 1, cw), F32)),
        grid=(n_cb, B),
        in_specs=[pl.BlockSpec((None, L, cw), lambda c, b: (b, 0, c)),
                  pl.BlockSpec((None, None, L, sw), lambda c, b: (b, c, 0, 0)),
                  pl.BlockSpec((None, L, cw), lambda c, b: (b, 0, c)),
                  pl.BlockSpec((None, cw, sw), lambda c, b: (c, 0, 0)),
                  pl.BlockSpec((None, sw, cw), lambda c, b: (c, 0, 0)),
                  pl.BlockSpec((None, 2, SUBLANES, sw), lambda c, b: (c, 0, 0, 0)),
                  pl.BlockSpec((None, 1, cw), lambda c, b: (c, 0, 0))],
        out_specs=(pl.BlockSpec((None, L, cw), lambda c, b: (b, 0, c)),
                   pl.BlockSpec((None, cw, sw), lambda c, b: (c, 0, 0)),
                   pl.BlockSpec((None, sw, cw), lambda c, b: (c, 0, 0)),
                   pl.BlockSpec((None, 1, sw), lambda c, b: (c, 0, 0)),
                   pl.BlockSpec((None, 1, cw), lambda c, b: (c, 0, 0))),
        scratch_shapes=[pltpu.VMEM((L, sw), F32), pltpu.VMEM((L, sw), F32)],
        compiler_params=_cparams())(p3, s_all, dya, wb, wc, tab_r, dsk)


def _glu_proj_call(ya, wglu, wproj, tm, name):
    T, W = ya.shape
    D = wproj.shape[1]

    def body(ya_ref, wg_ref, wp_ref, yo_ref, a_ref):
        ya = ya_ref[...]
        yo = ya.astype(F32) * jax.nn.sigmoid(_dot(ya, wg_ref[...]))
        yo_ref[...] = yo.astype(ACT)
        a_ref[...] = _dot(yo, wp_ref[...]).astype(ACT)

    return pl.pallas_call(
        body, name=name, out_shape=(jax.ShapeDtypeStruct((T, W), ACT), jax.ShapeDtypeStruct((T, D), ACT)),
        grid=(T // tm,),
        in_specs=[pl.BlockSpec((tm, W), lambda i: (i, 0)), pl.BlockSpec((W, W), lambda i: (0, 0)),
                  pl.BlockSpec((W, D), lambda i: (0, 0))],
        out_specs=(pl.BlockSpec((tm, W), lambda i: (i, 0)), pl.BlockSpec((tm, D), lambda i: (i, 0))),
        compiler_params=_cparams())(ya, wglu, wproj)


def _glu_bwd_call(ya, dyo, wglu, tm, name):
    T, W = ya.shape

    def body(ya_ref, dyo_ref, wg_ref, dya_ref, dwg_ref):
        @pl.when(pl.program_id(0) == 0)
        def _():
            dwg_ref[...] = jnp.zeros_like(dwg_ref)
        ya = ya_ref[...]
        yaf = ya.astype(F32)
        dyo = dyo_ref[...].astype(F32)
        sg = jax.nn.sigmoid(_dot(ya, wg_ref[...]))
        dt = dyo * yaf * sg * (1.0 - sg)
        dya_ref[...] = (dyo * sg + _dot_nt(dt, wg_ref[...])).astype(ACT)
        dwg_ref[...] += _dot_tn(ya, dt)

    return pl.pallas_call(
        body, name=name, out_shape=(jax.ShapeDtypeStruct((T, W), ACT), jax.ShapeDtypeStruct((W, W), F32)),
        grid=(T // tm,),
        in_specs=[pl.BlockSpec((tm, W), lambda i: (i, 0)), pl.BlockSpec((tm, W), lambda i: (i, 0)),
                  pl.BlockSpec((W, W), lambda i: (0, 0))],
        out_specs=(pl.BlockSpec((tm, W), lambda i: (i, 0)), pl.BlockSpec((W, W), lambda i: (0, 0))),
        compiler_params=_cparams())(ya, dyo, wglu)


PAD = 16


def _chunk_cumsums(x, pad_ref, L):
    row = lax.broadcasted_iota(jnp.int32, x.shape, 0) % CHUNK
    zeros = jnp.zeros((PAD, x.shape[1]), F32)
    pad_ref[0:PAD, :] = zeros
    pad_ref[PAD + L:2 * PAD + L, :] = zeros
    c = x
    r = x
    d = 1
    while d < CHUNK:
        pad_ref[PAD:PAD + L, :] = c
        c = c + jnp.where(row >= d, pad_ref[PAD - d:PAD - d + L, :], 0.0)
        pad_ref[PAD:PAD + L, :] = r
        r = r + jnp.where(row + d < CHUNK, pad_ref[PAD + d:PAD + d + L, :], 0.0)
        d *= 2
    return c, r - x


def _hgrn_prep(q_ref, fl_ref, lb_ref, pad_ref, r0, n):
    rows = pl.ds(r0, n)
    lb = lb_ref[...]
    sig = jax.nn.sigmoid(fl_ref[rows, :].astype(F32))
    f = lb + (1.0 - lb) * sig
    k = 1.0 - f
    c, rc = _chunk_cumsums(jnp.log(f), pad_ref, n)
    e_in, e_inv, e_out = jnp.exp(c), jnp.exp(-c), jnp.exp(rc)
    q = q_ref[rows, :].astype(F32)
    return dict(sig=sig, f=f, k=k, q=q, e_in=e_in, e_inv=e_inv, e_out=e_out, dec=jnp.exp(c + rc))


def _for_row_blocks(L, fn):
    full = L // GROUP
    if full:
        def step(g, carry):
            fn(pl.multiple_of(g * GROUP, GROUP), GROUP)
            return carry
        lax.fori_loop(0, full, step, 0)
    if L % GROUP:
        fn(full * GROUP, L % GROUP)


def _chunk_mask(rb):
    r = lax.broadcasted_iota(jnp.int32, (rb, rb), 0)
    c = lax.broadcasted_iota(jnp.int32, (rb, rb), 1)
    return (r // CHUNK == c // CHUNK) & (c <= r)


def _hg_out(o, og, g):
    on = o * lax.rsqrt(jnp.mean(o * o, axis=-1, keepdims=True) + EPS) * g
    return on * _silu(og)


def _hgrn_specs(L, hd, col_q, n_heads, order):
    def spec(sec):
        return pl.BlockSpec((None, L, hd), lambda *g: (order(*g)[0], 0, col_q + sec * n_heads + order(*g)[1]))
    return [spec(0), spec(1), spec(2), spec(3)]


GROUP = 128
CPG = GROUP // CHUNK


def _expand(x):
    xf = x.astype(F32)
    chunk = lax.broadcasted_iota(jnp.int32, xf.shape, 0) // CHUNK
    return jnp.concatenate([jnp.where(chunk == j, xf, 0.0) for j in range(CPG)], axis=1)


def _fill_tail(refs_fills, L):
    for ref, fill in refs_fills:
        if ref.shape[0] > L:
            ref[L:ref.shape[0], :] = jnp.full((ref.shape[0] - L, ref.shape[1]), fill, ref.dtype)


GROUP_UNROLL = 4


def _hgrn_forward_core(q_ref, fl_ref, v_ref, lb_ref, pad_ref, qin_ref, kin_ref, kout_ref, vp_ref, dec_ref, o_ref,
                       s_ref, a_ref, L, keep=()):
    hd = qin_ref.shape[1]
    n_groups = qin_ref.shape[0] // GROUP

    def prep(r0, n):
        pp = _hgrn_prep(q_ref, fl_ref, lb_ref, pad_ref, r0, n)
        rows = pl.ds(r0, n)
        for key, ref in keep:
            ref[rows, :] = pp[key]
        qin_ref[rows, :] = (pp["q"] * pp["e_in"]).astype(MXU)
        kin_ref[rows, :] = (pp["k"] * pp["e_inv"]).astype(MXU)
        kout_ref[rows, :] = (pp["k"] * pp["e_out"]).astype(MXU)
        vp_ref[rows, :] = v_ref[rows, :].astype(MXU)
        dec_ref[rows, :] = pp["dec"]

    _for_row_blocks(L, prep)
    _fill_tail(((qin_ref, 0.0), (kin_ref, 0.0), (kout_ref, 0.0), (vp_ref, 0.0), (dec_ref, 1.0)), L)
    mask = _chunk_mask(GROUP)

    def scores(g, carry):
        rows = pl.ds(pl.multiple_of(g * GROUP, GROUP), GROUP)
        a_ref[rows, :] = jnp.where(mask, _dot_nt(qin_ref[rows, :], kin_ref[rows, :]), 0.0).astype(MXU)
        return carry

    lax.fori_loop(0, n_groups, scores, 0, unroll=GROUP_UNROLL)

    def intra(g, carry):
        rows = pl.ds(pl.multiple_of(g * GROUP, GROUP), GROUP)
        o_ref[rows, :] = _dot(a_ref[rows, :], vp_ref[rows, :])
        kv = _dot_tn(vp_ref[rows, :], _expand(kout_ref[rows, :]))
        for j in range(CPG):
            s_ref[g * CPG + j] = kv[:, j * hd:(j + 1) * hd]
        return carry

    lax.fori_loop(0, n_groups, intra, 0, unroll=GROUP_UNROLL)

    def rec(n, st):
        kv = s_ref[n]
        s_ref[n] = st
        dec = dec_ref[pl.ds(pl.multiple_of(n * CHUNK, CHUNK), SUBLANES), :][0:1]
        return st * dec + kv

    lax.fori_loop(0, L // CHUNK, rec, jnp.zeros((hd, hd), F32))

    def inter(g, carry):
        rows = pl.ds(pl.multiple_of(g * GROUP, GROUP), GROUP)
        scat = jnp.concatenate([s_ref[g * CPG + j] for j in range(CPG)], axis=1)
        o_ref[rows, :] += _dot_nt(_expand(qin_ref[rows, :]), scat)
        return carry

    lax.fori_loop(0, n_groups, inter, 0, unroll=GROUP_UNROLL)


def _hgrn_scratch(L, hd):
    lp = -(-L // GROUP) * GROUP
    return lp, [pltpu.VMEM((GROUP + 2 * PAD, hd), F32), pltpu.VMEM((lp, hd), MXU), pltpu.VMEM((lp, hd), MXU),
                pltpu.VMEM((lp, hd), MXU), pltpu.VMEM((lp, hd), MXU), pltpu.VMEM((lp, hd), F32),
                pltpu.VMEM((lp, hd), F32), pltpu.VMEM((lp // CHUNK, hd, hd), F32), pltpu.VMEM((lp, GROUP), MXU)]


def _hgrn_fwd_call(p3, lb, ng, n_heads, col_q, name):
    B, L, _ = p3.shape
    hd = ng.shape[1]
    _, scratch = _hgrn_scratch(L, hd)

    def body(q_ref, fl_ref, v_ref, og_ref, lb_ref, ng_ref, yb_ref,
             pad_ref, qin_ref, kin_ref, kout_ref, vp_ref, dec_ref, o_ref, s_ref, a_ref):
        _hgrn_forward_core(q_ref, fl_ref, v_ref, lb_ref, pad_ref, qin_ref, kin_ref, kout_ref, vp_ref, dec_ref,
                           o_ref, s_ref, a_ref, L)

        def out(r0, n):
            rows = pl.ds(r0, n)
            yb_ref[rows, :] = _hg_out(o_ref[rows, :], og_ref[rows, :].astype(F32), ng_ref[...]).astype(ACT)

        _for_row_blocks(L, out)

    order = lambda b, h: (b, h)
    return pl.pallas_call(
        body, name=name, out_shape=jax.ShapeDtypeStruct((B, L, n_heads * hd), ACT), grid=(B, n_heads),
        in_specs=_hgrn_specs(L, hd, col_q, n_heads, order) + [
            pl.BlockSpec((1, hd), lambda b, h: (0, h)), pl.BlockSpec((1, hd), lambda b, h: (0, 0))],
        out_specs=pl.BlockSpec((None, L, hd), lambda b, h: (b, 0, h)),
        scratch_shapes=scratch, compiler_params=_cparams())(p3, p3, p3, p3, lb, ng)


def _hgrn_bwd_call(p3, dyb, lb, ng, n_heads, col_q, name):
    B, L, _ = p3.shape
    hd = ng.shape[1]
    n_chunks = L // CHUNK
    lp, scratch = _hgrn_scratch(L, hd)
    n_groups = lp // GROUP

    def body(q_ref, fl_ref, v_ref, og_ref, dyb_ref, lb_ref, ng_ref,
             dq_ref, dfl_ref, dv_ref, dog_ref, dlb_ref, dng_ref,
             pad_ref, qin_ref, kin_ref, kout_ref, vp_ref, dec_ref, o_ref, s_ref, a_ref,
             do_ref, ds_ref, dqi_ref, dki_ref, dko_ref, dvv_ref, dct_ref,
             sig_ref, f_ref, ein_ref, einv_ref, eout_ref, da_ref):
        @pl.when(pl.program_id(1) == 0)
        def _():
            dlb_ref[...] = jnp.zeros_like(dlb_ref)

        @pl.when((pl.program_id(0) == 0) & (pl.program_id(1) == 0))
        def _():
            dng_ref[...] = jnp.zeros_like(dng_ref)

        _hgrn_forward_core(q_ref, fl_ref, v_ref, lb_ref, pad_ref, qin_ref, kin_ref, kout_ref, vp_ref, dec_ref,
                           o_ref, s_ref, a_ref, L, keep=(("sig", sig_ref), ("f", f_ref), ("e_in", ein_ref),
                                                  ("e_inv", einv_ref), ("e_out", eout_ref)))

        def out_bwd(r0, n):
            rows = pl.ds(r0, n)
            _, out_vjp = jax.vjp(_hg_out, o_ref[rows, :], og_ref[rows, :].astype(F32), ng_ref[...])
            d_o, d_og, d_ng = out_vjp(dyb_ref[rows, :].astype(F32))
            dog_ref[rows, :] = d_og.astype(ACT)
            dng_ref[...] += d_ng
            do_ref[rows, :] = d_o.astype(MXU)

        _for_row_blocks(L, out_bwd)
        _fill_tail(((do_ref, 0.0),), L)
        mask = _chunk_mask(GROUP)

        def score_grads(g, carry):
            rows = pl.ds(pl.multiple_of(g * GROUP, GROUP), GROUP)
            da_ref[rows, :] = jnp.where(mask, _dot_nt(do_ref[rows, :], vp_ref[rows, :]), 0.0).astype(MXU)
            return carry

        lax.fori_loop(0, n_groups, score_grads, 0, unroll=GROUP_UNROLL)

        def grads_a(g, carry):
            rows = pl.ds(pl.multiple_of(g * GROUP, GROUP), GROUP)
            qi, ki, do, da = qin_ref[rows, :], kin_ref[rows, :], do_ref[rows, :], da_ref[rows, :]
            sstack = s_ref[pl.ds(g * CPG, CPG)].reshape(CPG * hd, hd)
            dqi_ref[rows, :] = _dot(da, ki) + _dot(_expand(do), sstack)
            dki_ref[rows, :] = _dot_tn(da, qi)
            dvv_ref[rows, :] = _dot_tn(a_ref[rows, :], do)
            x = _dot_tn(do, _expand(qi))
            for j in range(CPG):
                ds_ref[g * CPG + j] = x[:, j * hd:(j + 1) * hd]
            return carry

        lax.fori_loop(0, n_groups, grads_a, 0, unroll=GROUP_UNROLL)

        def rec_bwd(k, dst):
            n = n_chunks - 1 - k
            r0 = pl.multiple_of(n * CHUNK, CHUNK)
            x = ds_ref[n]
            ds_ref[n] = dst
            dec = dec_ref[pl.ds(r0, SUBLANES), :][0:1]
            return dst * dec + x

        lax.fori_loop(0, n_chunks, rec_bwd, jnp.zeros((hd, hd), F32))

        def grads_b(g, carry):
            r0 = pl.multiple_of(g * GROUP, GROUP)
            rows = pl.ds(r0, GROUP)
            ds = [ds_ref[g * CPG + j] for j in range(CPG)]
            dscat = jnp.concatenate(ds, axis=1)
            dvv_ref[rows, :] += _dot_nt(_expand(kout_ref[rows, :]), dscat)
            dstack = ds_ref[pl.ds(g * CPG, CPG)].reshape(CPG * hd, hd)
            dko_ref[rows, :] = _dot(_expand(vp_ref[rows, :]), dstack)
            for j in range(CPG):
                dec = dec_ref[pl.ds(r0 + j * CHUNK, SUBLANES), :][0:1]
                ddec = dec * jnp.sum(ds[j] * s_ref[g * CPG + j], axis=0, keepdims=True)
                dct_ref[pl.ds(r0 + j * CHUNK, CHUNK), :] = jnp.broadcast_to(ddec, (CHUNK, hd))
            return carry

        lax.fori_loop(0, n_groups, grads_b, 0, unroll=GROUP_UNROLL)

        def finish(r0, n):
            rows = pl.ds(r0, n)
            sig, f, e_in, e_inv, e_out = [r[rows, :] for r in (sig_ref, f_ref, ein_ref, einv_ref, eout_ref)]
            q, k = q_ref[rows, :].astype(F32), 1.0 - f
            dqi, dki, dko = dqi_ref[rows, :], dki_ref[rows, :], dko_ref[rows, :]
            dq = dqi * e_in
            dk = dki * e_inv + dko * e_out
            dq_ref[rows, :] = dq.astype(ACT)
            dv_ref[rows, :] = dvv_ref[rows, :].astype(ACT)
            t_out = k * e_out * dko
            dc = q * dq - k * e_inv * dki - t_out
            _, dc_later = _chunk_cumsums(dc, pad_ref, n)
            t_incl, t_later = _chunk_cumsums(t_out, pad_ref, n)
            dlogf = dc + dc_later + t_incl + t_later + dct_ref[rows, :]
            df = dlogf / f - dk
            dfl_ref[rows, :] = (df * (1.0 - lb_ref[...]) * sig * (1.0 - sig)).astype(ACT)
            dlb_ref[...] += jnp.sum(df * (1.0 - sig), axis=0, keepdims=True)

        _for_row_blocks(L, finish)

    order = lambda h, b: (b, h)
    W = n_heads * hd
    act_out = jax.ShapeDtypeStruct((B, L, W), ACT)
    blk_out = pl.BlockSpec((None, L, hd), lambda h, b: (b, 0, h))
    return pl.pallas_call(
        body, name=name,
        out_shape=(act_out, act_out, act_out, act_out, jax.ShapeDtypeStruct((1, W), F32),
                   jax.ShapeDtypeStruct((1, hd), F32)),
        grid=(n_heads, B),
        in_specs=_hgrn_specs(L, hd, col_q, n_heads, order) + [
            pl.BlockSpec((None, L, hd), lambda h, b: (b, 0, h)),
            pl.BlockSpec((1, hd), lambda h, b: (0, h)), pl.BlockSpec((1, hd), lambda h, b: (0, 0))],
        out_specs=(blk_out, blk_out, blk_out, blk_out, pl.BlockSpec((1, hd), lambda h, b: (0, h)),
                   pl.BlockSpec((1, hd), lambda h, b: (0, 0))),
        scratch_shapes=scratch + [
            pltpu.VMEM((lp, hd), MXU), pltpu.VMEM((lp // CHUNK, hd, hd), F32)] + [pltpu.VMEM((lp, hd), F32)] * 10 + [
            pltpu.VMEM((lp, GROUP), MXU)],
        compiler_params=_cparams())(p3, p3, p3, p3, dyb, lb, ng)


def _merge_fn(a, bm, ga, gb):
    return jax.nn.sigmoid(ga) * a + jax.nn.sigmoid(gb) * bm


def _merge_call(yb, a, p, h0, whp, wout, g2, col_ga, tm, name):
    T, D = h0.shape

    def body(yb_ref, a_ref, ga_ref, gb_ref, h0_ref, whp_ref, wout_ref, g2_ref, h1_ref, mg_ref, bm_ref, z2_ref):
        bm = _dot(yb_ref[...], whp_ref[...])
        mg = _merge_fn(a_ref[...].astype(F32), bm, ga_ref[...].astype(F32), gb_ref[...].astype(F32))
        h1 = h0_ref[...] + _dot(mg, wout_ref[...])
        h1_ref[...] = h1
        mg_ref[...] = mg.astype(ACT)
        bm_ref[...] = bm.astype(ACT)
        z2_ref[...] = _rms(h1, g2_ref[...]).astype(ACT)

    tile = pl.BlockSpec((tm, D), lambda i: (i, 0))
    full = pl.BlockSpec((D, D), lambda i: (0, 0))
    act = jax.ShapeDtypeStruct((T, D), ACT)
    return pl.pallas_call(
        body, name=name, out_shape=(jax.ShapeDtypeStruct((T, D), F32), act, act, act), grid=(T // tm,),
        in_specs=[tile, tile, pl.BlockSpec((tm, D), lambda i: (i, col_ga)),
                  pl.BlockSpec((tm, D), lambda i: (i, col_ga + 1)), tile, full, full,
                  pl.BlockSpec((1, D), lambda i: (0, 0))],
        out_specs=(tile, tile, tile, tile), compiler_params=_cparams())(yb, a, p, p, h0, whp, wout, g2)


def _merge_bwd_call(dmg, a, bm, p, col_ga, tm, name):
    T, D = dmg.shape

    def body(dmg_ref, a_ref, bm_ref, ga_ref, gb_ref, da_ref, dbm_ref, dga_ref, dgb_ref):
        args = [r[...].astype(F32) for r in (a_ref, bm_ref, ga_ref, gb_ref)]
        _, vjp = jax.vjp(_merge_fn, *args)
        for r, o in zip((da_ref, dbm_ref, dga_ref, dgb_ref), vjp(dmg_ref[...].astype(F32))):
            r[...] = o.astype(ACT)

    tile = pl.BlockSpec((tm, D), lambda i: (i, 0))
    act = jax.ShapeDtypeStruct((T, D), ACT)
    return pl.pallas_call(
        body, name=name, out_shape=(act, act, act, act), grid=(T // tm,),
        in_specs=[tile, tile, tile, pl.BlockSpec((tm, D), lambda i: (i, col_ga)),
                  pl.BlockSpec((tm, D), lambda i: (i, col_ga + 1))],
        out_specs=(tile, tile, tile, tile), compiler_params=_cparams())(dmg, a, bm, p, p)


def _conv_taps(x_ref, halo_ref, ext_ref, edge, tm, before):
    halo = jnp.where(edge, 0.0, halo_ref[...].astype(F32))
    x = x_ref[...].astype(F32)
    if before:
        ext_ref[0:PAD, :] = halo
        ext_ref[PAD:PAD + tm, :] = x
        return [ext_ref[PAD - 2 + k:PAD - 2 + k + tm, :] for k in range(3)]
    ext_ref[0:tm, :] = x
    ext_ref[tm:tm + PAD, :] = halo
    return [ext_ref[k:k + tm, :] for k in range(3)]


def _conv(taps, cw, cb):
    return cb + cw[0:1] * taps[0] + cw[1:2] * taps[1] + cw[2:3] * taps[2]


def _ffn_pair_specs(tm, F, T, n_pairs, order, before):
    hb = tm // PAD
    last = T // PAD - 1

    def halo_row(i):
        return jnp.maximum(i * hb - 1, 0) if before else jnp.minimum((i + 1) * hb, last)

    specs = []
    for off in (0, n_pairs):
        specs.append(pl.BlockSpec((None, tm, F), lambda *g, off=off: (order(*g)[1] + off, order(*g)[0], 0)))
        specs.append(pl.BlockSpec((None, PAD, F), lambda *g, off=off: (order(*g)[1] + off, halo_row(order(*g)[0]), 0)))
    return specs


def _ffn_fwd_call(up, cw, cb, wd, h1, tgt, g3, tm, tps, name):
    S, T, F = up.shape
    n_pairs = S // 2
    D = h1.shape[1]

    def body(ua_ref, ha_ref, ub_ref, hb_ref, cwa_ref, cwb_ref, cba_ref, cbb_ref, wd_ref, h1_ref, tgt_ref, g3_ref,
             act_ref, dh2_ref, loss_ref, dg3_ref, acc_ref, ext_ref):
        i, j = pl.program_id(0), pl.program_id(1)
        edge = (i % tps) == 0
        ua = _conv(_conv_taps(ua_ref, ha_ref, ext_ref, edge, tm, True), cwa_ref[...], cba_ref[...])
        ub = _conv(_conv_taps(ub_ref, hb_ref, ext_ref, edge, tm, True), cwb_ref[...], cbb_ref[...])
        act = _silu(ua) * ub
        act_ref[...] = act.astype(ACT)
        contrib = _dot(act, wd_ref[...])

        @pl.when(j == 0)
        def _():
            acc_ref[...] = h1_ref[...] + contrib

        @pl.when(j > 0)
        def _():
            acc_ref[...] += contrib

        @pl.when((i == 0) & (j == 0))
        def _():
            loss_ref[...] = jnp.zeros_like(loss_ref)
            dg3_ref[...] = jnp.zeros_like(dg3_ref)

        @pl.when(j == n_pairs - 1)
        def _():
            row = lax.broadcasted_iota(jnp.int32, (tm, 1), 0) + (i % tps) * tm
            valid = row >= N_META
            tgt = tgt_ref[...]

            def loss_fn(h2, g):
                err = _rms(h2, g) - tgt
                return 0.5 * jnp.sum(jnp.where(valid, err * err, 0.0)) / D

            loss, vjp = jax.vjp(loss_fn, acc_ref[...], g3_ref[...])
            dh2, dg3 = vjp(jnp.ones((), F32))
            dh2_ref[...] = dh2
            loss_ref[...] += loss
            dg3_ref[...] += dg3

    order = lambda i, j: (i, j)
    tile = pl.BlockSpec((tm, D), lambda i, j: (i, 0))
    vec = pl.BlockSpec((1, D), lambda i, j: (0, 0))
    return pl.pallas_call(
        body, name=name,
        out_shape=(jax.ShapeDtypeStruct((n_pairs, T, F), ACT), jax.ShapeDtypeStruct((T, D), F32),
                   jax.ShapeDtypeStruct((1, LANES), F32), jax.ShapeDtypeStruct((1, D), F32)),
        grid=(T // tm, n_pairs),
        in_specs=_ffn_pair_specs(tm, F, T, n_pairs, order, True) + [
            pl.BlockSpec((None, 3, F), lambda i, j: (j, 0, 0)), pl.BlockSpec((None, 3, F), lambda i, j: (j + n_pairs, 0, 0)),
            pl.BlockSpec((None, 1, F), lambda i, j: (j, 0, 0)), pl.BlockSpec((None, 1, F), lambda i, j: (j + n_pairs, 0, 0)),
            pl.BlockSpec((None, F, D), lambda i, j: (j, 0, 0)), tile, tile, vec],
        out_specs=(pl.BlockSpec((None, tm, F), lambda i, j: (j, i, 0)), tile,
                   pl.BlockSpec((1, LANES), lambda i, j: (0, 0)), vec),
        scratch_shapes=[pltpu.VMEM((tm, D), F32), pltpu.VMEM((tm + PAD, F), F32)],
        compiler_params=_cparams())(up, up, up, up, cw, cw, cb, cb, wd, h1, tgt, g3)


def _ffn_bwd_a_call(dh2, up, act, cw, cb, wd, tm, tps, name):
    S, T, F = up.shape
    n_pairs = S // 2
    D = dh2.shape[1]

    def body(dh2_ref, ua_ref, ha_ref, ub_ref, hb_ref, act_ref, cwa_ref, cwb_ref, cba_ref, cbb_ref, wd_ref,
             dua_ref, dub_ref, dwd_ref, dcwa_ref, dcwb_ref, dcba_ref, dcbb_ref, ext_ref):
        i = pl.program_id(1)
        edge = (i % tps) == 0

        @pl.when(i == 0)
        def _():
            for r in (dwd_ref, dcwa_ref, dcwb_ref, dcba_ref, dcbb_ref):
                r[...] = jnp.zeros_like(r)

        dh2 = dh2_ref[...]
        dact = _dot_nt(dh2, wd_ref[...])
        dwd_ref[...] += _dot_tn(act_ref[...], dh2)
        taps_a = _conv_taps(ua_ref, ha_ref, ext_ref, edge, tm, True)
        ua = _conv(taps_a, cwa_ref[...], cba_ref[...])
        sa = jax.nn.sigmoid(ua)
        dub = dact * ua * sa
        dcbb_ref[...] += jnp.sum(dub, axis=0, keepdims=True)
        taps_b = _conv_taps(ub_ref, hb_ref, ext_ref, edge, tm, True)
        dcwb_ref[...] += jnp.concatenate([jnp.sum(dub * t, axis=0, keepdims=True) for t in taps_b], axis=0)
        ub = _conv(taps_b, cwb_ref[...], cbb_ref[...])
        dua = dact * ub * sa * (1.0 + ua * (1.0 - sa))
        dcba_ref[...] += jnp.sum(dua, axis=0, keepdims=True)
        taps_a = _conv_taps(ua_ref, ha_ref, ext_ref, edge, tm, True)
        dcwa_ref[...] += jnp.concatenate([jnp.sum(dua * t, axis=0, keepdims=True) for t in taps_a], axis=0)
        dua_ref[...] = dua.astype(ACT)
        dub_ref[...] = dub.astype(ACT)

    order = lambda j, i: (i, j)
    sh = lambda rows: jax.ShapeDtypeStruct((n_pairs, rows, F), F32)
    par = lambda rows: pl.BlockSpec((None, rows, F), lambda j, i: (j, 0, 0))
    return pl.pallas_call(
        body, name=name,
        out_shape=(jax.ShapeDtypeStruct((n_pairs, T, F), ACT), jax.ShapeDtypeStruct((n_pairs, T, F), ACT),
                   jax.ShapeDtypeStruct((n_pairs, F, D), F32), sh(3), sh(3), sh(1), sh(1)),
        grid=(n_pairs, T // tm),
        in_specs=[pl.BlockSpec((tm, D), lambda j, i: (i, 0))] + _ffn_pair_specs(tm, F, T, n_pairs, order, True) + [
            pl.BlockSpec((None, tm, F), lambda j, i: (j, i, 0)),
            pl.BlockSpec((None, 3, F), lambda j, i: (j, 0, 0)), pl.BlockSpec((None, 3, F), lambda j, i: (j + n_pairs, 0, 0)),
            pl.BlockSpec((None, 1, F), lambda j, i: (j, 0, 0)), pl.BlockSpec((None, 1, F), lambda j, i: (j + n_pairs, 0, 0)),
            pl.BlockSpec((None, F, D), lambda j, i: (j, 0, 0))],
        out_specs=(pl.BlockSpec((None, tm, F), lambda j, i: (j, i, 0)), pl.BlockSpec((None, tm, F), lambda j, i: (j, i, 0)),
                   pl.BlockSpec((None, F, D), lambda j, i: (j, 0, 0)), par(3), par(3), par(1), par(1)),
        scratch_shapes=[pltpu.VMEM((tm + PAD, F), F32)],
        compiler_params=_cparams())(dh2, up, up, up, up, act, cw, cw, cb, cb, wd)


def _ffn_bwd_b_call(dua, dub, cw, wup, h1, g2, dh2, tm, tps, name):
    n_pairs, T, F = dua.shape
    D = h1.shape[1]
    hb = tm // PAD
    last = T // PAD - 1

    def body(da_ref, na_ref, db_ref, nb_ref, cwa_ref, cwb_ref, wa_ref, wb_ref, h1_ref, g2_ref, dh2_ref,
             dupa_ref, dupb_ref, dh1_ref, dg2_ref, acc_ref, ext_ref):
        i, j = pl.program_id(0), pl.program_id(1)
        edge = (i % tps) == tps - 1
        outs = []
        for d_ref, n_ref, cw_ref, o_ref in ((da_ref, na_ref, cwa_ref, dupa_ref), (db_ref, nb_ref, cwb_ref, dupb_ref)):
            t = _conv_taps(d_ref, n_ref, ext_ref, edge, tm, False)
            cwv = cw_ref[...]
            dup = cwv[2:3] * t[0] + cwv[1:2] * t[1] + cwv[0:1] * t[2]
            o_ref[...] = dup.astype(ACT)
            outs.append(dup)
        contrib = _dot_nt(outs[0], wa_ref[...]) + _dot_nt(outs[1], wb_ref[...])

        @pl.when(j == 0)
        def _():
            acc_ref[...] = contrib

        @pl.when(j > 0)
        def _():
            acc_ref[...] += contrib

        @pl.when((i == 0) & (j == 0))
        def _():
            dg2_ref[...] = jnp.zeros_like(dg2_ref)

        @pl.when(j == n_pairs - 1)
        def _():
            _, vjp = jax.vjp(_rms, h1_ref[...], g2_ref[...])
            dh, dg = vjp(acc_ref[...])
            dh1_ref[...] = dh2_ref[...] + dh
            dg2_ref[...] += dg

    tile = pl.BlockSpec((tm, D), lambda i, j: (i, 0))
    vec = pl.BlockSpec((1, D), lambda i, j: (0, 0))
    pair = lambda: [pl.BlockSpec((None, tm, F), lambda i, j: (j, i, 0)),
                    pl.BlockSpec((None, PAD, F), lambda i, j: (j, jnp.minimum((i + 1) * hb, last), 0))]
    act = jax.ShapeDtypeStruct((n_pairs, T, F), ACT)
    return pl.pallas_call(
        body, name=name,
        out_shape=(act, act, jax.ShapeDtypeStruct((T, D), F32), jax.ShapeDtypeStruct((1, D), F32)),
        grid=(T // tm, n_pairs),
        in_specs=pair() + pair() + [
            pl.BlockSpec((None, 3, F), lambda i, j: (j, 0, 0)), pl.BlockSpec((None, 3, F), lambda i, j: (j + n_pairs, 0, 0)),
            pl.BlockSpec((None, D, F), lambda i, j: (j, 0, 0)), pl.BlockSpec((None, D, F), lambda i, j: (j + n_pairs, 0, 0)),
            tile, vec, tile],
        out_specs=(pl.BlockSpec((None, tm, F), lambda i, j: (j, i, 0)), pl.BlockSpec((None, tm, F), lambda i, j: (j, i, 0)),
                   tile, vec),
        scratch_shapes=[pltpu.VMEM((tm, D), F32), pltpu.VMEM((tm + PAD, F), F32)],
        compiler_params=_cparams())(dua, dua, dub, dub, cw, cw, wup, wup, h1, g2, dh2)


def _in_bwd_call(dp, w_in, h0, g1, dh1, tm, name):
    T, D = h0.shape
    S, _, N = w_in.shape

    def body(dp_ref, w_ref, h0_ref, g1_ref, dh1_ref, dh0_ref, dg1_ref, acc_ref):
        i, j = pl.program_id(0), pl.program_id(1)
        contrib = _dot_nt(dp_ref[...], w_ref[...])

        @pl.when(j == 0)
        def _():
            acc_ref[...] = contrib

        @pl.when(j > 0)
        def _():
            acc_ref[...] += contrib

        @pl.when((i == 0) & (j == 0))
        def _():
            dg1_ref[...] = jnp.zeros_like(dg1_ref)

        @pl.when(j == S - 1)
        def _():
            _, vjp = jax.vjp(_rms, h0_ref[...], g1_ref[...])
            dh, dg = vjp(acc_ref[...])
            dh0_ref[...] = dh1_ref[...] + dh
            dg1_ref[...] += dg

    tile = pl.BlockSpec((tm, D), lambda i, j: (i, 0))
    vec = pl.BlockSpec((1, D), lambda i, j: (0, 0))
    return pl.pallas_call(
        body, name=name, out_shape=(jax.ShapeDtypeStruct((T, D), F32), jax.ShapeDtypeStruct((1, D), F32)),
        grid=(T // tm, S),
        in_specs=[pl.BlockSpec((tm, N), lambda i, j: (i, j)), pl.BlockSpec((None, D, N), lambda i, j: (j, 0, 0)),
                  tile, vec, tile],
        out_specs=(tile, vec), scratch_shapes=[pltpu.VMEM((tm, D), F32)],
        compiler_params=_cparams())(dp, w_in, h0, g1, dh1)


def _meta_grad_call(dh0_3, name):
    B, L, D = dh0_3.shape

    def body(d_ref, o_ref):
        o_ref[...] = jnp.sum(d_ref[...], axis=0)

    return pl.pallas_call(
        body, name=name, out_shape=jax.ShapeDtypeStruct((N_META, D), F32), grid=(1,),
        in_specs=[pl.BlockSpec((B, N_META, D), lambda i: (0, 0, 0))],
        out_specs=pl.BlockSpec((N_META, D), lambda i: (0, 0)), compiler_params=_cparams())(dh0_3)


_RELS = [(dx, dy, dc) for dx in (0, 1) for dy in (0, 1) for dc in (0, 1)][1:]


def _exchange_call(arrs, scatter, name):
    n = len(arrs)
    n_rel = len(_RELS)

    def body(*refs):
        ins, outs = refs[:n], refs[n:2 * n]
        send_sems, recv_sems, loc_sems = refs[2 * n:]
        x, y, c = lax.axis_index("x"), lax.axis_index("y"), lax.axis_index("c")
        me = 4 * x + 2 * y + c
        started = []
        for k in range(n):
            src_me = ins[k].at[me] if scatter else ins[k]
            loc = pltpu.make_async_copy(src_me, outs[k].at[me], loc_sems.at[k])
            loc.start()
            started.append(loc)
        waits = []
        for r, (dx, dy, dc) in enumerate(_RELS):
            px, py, pc = (x + dx) % 2, (y + dy) % 2, (c + dc) % 2
            pid = 4 * px + 2 * py + pc
            for k in range(n):
                s = k * n_rel + r
                src = ins[k].at[pid] if scatter else ins[k]
                cp = pltpu.make_async_remote_copy(
                    src_ref=src, dst_ref=outs[k].at[me], send_sem=send_sems.at[s], recv_sem=recv_sems.at[s],
                    device_id=(px, py, pc), device_id_type=pl.DeviceIdType.MESH)
                cp.start()
                waits.append(pltpu.make_async_remote_copy(
                    src_ref=src, dst_ref=outs[k].at[pid], send_sem=send_sems.at[s], recv_sem=recv_sems.at[s],
                    device_id=(px, py, pc), device_id_type=pl.DeviceIdType.MESH))
        for w in waits:
            w.wait_send()
            w.wait_recv()
        for loc in started:
            loc.wait()

    out_shape = tuple(jax.ShapeDtypeStruct(a.shape if scatter else (N_DEV,) + a.shape, a.dtype) for a in arrs)
    hbm = pl.BlockSpec(memory_space=pl.ANY)
    return pl.pallas_call(
        body, name=name, out_shape=out_shape, in_specs=[hbm] * n, out_specs=tuple([hbm] * n),
        scratch_shapes=[pltpu.SemaphoreType.DMA((n * n_rel,)), pltpu.SemaphoreType.DMA((n * n_rel,)),
                        pltpu.SemaphoreType.DMA((n,))],
        compiler_params=pltpu.CompilerParams(has_side_effects=True))(*arrs)


_HBM = pl.BlockSpec(memory_space=pltpu.HBM)
_SEM = pl.BlockSpec(memory_space=pltpu.SEMAPHORE)
_DATAFLOW = pltpu.SideEffectType.DATAFLOW_SIDE_EFFECTING


def _peer_copies(ins, lands, send_sems, recv_sems, scatter):
    n = len(ins)
    x, y, c = lax.axis_index("x"), lax.axis_index("y"), lax.axis_index("c")
    me = 4 * x + 2 * y + c
    sends, arrivals = [], []
    for r, (dx, dy, dc) in enumerate(_RELS):
        px, py, pc = (x + dx) % 2, (y + dy) % 2, (c + dc) % 2
        pid = 4 * px + 2 * py + pc
        for k in range(n):
            s = k * len(_RELS) + r
            src = ins[k].at[pid] if scatter else ins[k]
            for dst, out in ((lands[k].at[me], sends), (lands[k].at[pid], arrivals)):
                out.append(pltpu.make_async_remote_copy(
                    src_ref=src, dst_ref=dst, send_sem=send_sems.at[s], recv_sem=recv_sems.at[s],
                    device_id=(px, py, pc), device_id_type=pl.DeviceIdType.MESH))
    return sends, arrivals


def _exchange_start(arrs, scatter, name):
    n = len(arrs)
    n_sem = n * len(_RELS)

    def body(*refs):
        ins, lands = refs[:n], refs[n:2 * n]
        send_sems, recv_sems = refs[2 * n], refs[2 * n + 1]
        token = refs[-1]
        sends, _ = _peer_copies(ins, lands, send_sems, recv_sems, scatter)
        for cp in sends:
            cp.start()
        token[...] = jnp.zeros_like(token)

    land_shapes = [a.shape if scatter else (N_DEV,) + a.shape for a in arrs]
    ops = [pltpu.with_memory_space_constraint(a, pltpu.HBM) for a in arrs]
    ops += [pltpu.with_memory_space_constraint(lax.empty(s, a.dtype), pltpu.HBM) for s, a in zip(land_shapes, arrs)]
    out = pl.pallas_call(
        body, name=name,
        out_shape=(pltpu.SemaphoreType.DMA((n_sem,)), pltpu.SemaphoreType.DMA((n_sem,)),
                   *[pltpu.HBM(a.shape, a.dtype) for a in arrs],
                   *[pltpu.HBM(s, a.dtype) for s, a in zip(land_shapes, arrs)],
                   jax.ShapeDtypeStruct((SUBLANES, LANES), F32)),
        in_specs=[_HBM] * (2 * n),
        out_specs=(_SEM, _SEM, *[_HBM] * (2 * n), pl.BlockSpec(memory_space=pltpu.VMEM)),
        input_output_aliases={i: 2 + i for i in range(2 * n)},
        compiler_params=pltpu.CompilerParams(has_side_effects=_DATAFLOW))(*ops)
    return out[0], out[1], list(out[2:2 + n]), list(out[2 + n:2 + 2 * n]), out[-1]


def _exchange_wait(started, after, scatter, name):
    send_sems, recv_sems, srcs, lands, _ = started
    n = len(srcs)

    def body(*refs):
        ins, lands_ = refs[:n], refs[n:2 * n]
        _, arrivals = _peer_copies(ins, lands_, refs[2 * n], refs[2 * n + 1], scatter)
        for cp in arrivals:
            cp.wait_send()
            cp.wait_recv()

    out = pl.pallas_call(
        body, name=name,
        out_shape=(*[pltpu.HBM(a.shape, a.dtype) for a in srcs], *[pltpu.HBM(a.shape, a.dtype) for a in lands]),
        in_specs=[_HBM] * (2 * n) + [_SEM, _SEM, pl.BlockSpec(memory_space=pl.ANY)],
        out_specs=tuple([_HBM] * (2 * n)), input_output_aliases={i: i for i in range(2 * n)},
        compiler_params=pltpu.CompilerParams(has_side_effects=_DATAFLOW))(*srcs, *lands, send_sems, recv_sems, after)
    return list(out[:n]), list(out[n:])


def _place_own_call(srcs, lands, scatter, me, name):
    outs = []
    for k, (src, land) in enumerate(zip(srcs, lands)):
        R, C = land.shape[1:]
        tr = R
        while tr % 32 == 0 and tr * C * land.dtype.itemsize > 2 * 1024 * 1024:
            tr //= 2

        def body(me_ref, s_ref, l_ref, o_ref):
            o_ref[...] = s_ref[...]

        src_spec = (pl.BlockSpec((None, tr, C), lambda i, me_ref: (me_ref[0], i, 0)) if scatter
                    else pl.BlockSpec((tr, C), lambda i, me_ref: (i, 0)))
        outs.append(pl.pallas_call(
            body, name=f"{name}_{k}", out_shape=jax.ShapeDtypeStruct(land.shape, land.dtype),
            grid_spec=pltpu.PrefetchScalarGridSpec(
                num_scalar_prefetch=1, grid=(R // tr,),
                in_specs=[src_spec, pl.BlockSpec(memory_space=pl.ANY)],
                out_specs=pl.BlockSpec((None, tr, C), lambda i, me_ref: (me_ref[0], i, 0))),
            input_output_aliases={2: 0}, compiler_params=_cparams())(me, src, land))
    return outs


def _adamw_shard_call(w, parts, m, v, name):
    R, C = w.shape
    tr = _tile(R, 128) if R % 16 == 0 else R

    def body(w_ref, p_ref, m_ref, v_ref, g_ref, d_ref, nm_ref, nv_ref):
        g = p_ref[0].astype(F32)
        for s in range(1, N_DEV):
            g = g + p_ref[s].astype(F32)
        d, nm, nv = _adamw(w_ref[...], g, m_ref[...], v_ref[...])
        g_ref[...] = g
        d_ref[...] = d
        nm_ref[...] = nm
        nv_ref[...] = nv

    tile = pl.BlockSpec((tr, C), lambda i: (i, 0))
    sh = jax.ShapeDtypeStruct((R, C), F32)
    return pl.pallas_call(
        body, name=name, out_shape=(sh, sh, sh, sh), grid=(R // tr,),
        in_specs=[tile, pl.BlockSpec((N_DEV, tr, C), lambda i: (0, i, 0)), tile, tile],
        out_specs=(tile, tile, tile, tile), compiler_params=_cparams())(w, parts, m, v)


def _pack(arrs, rows_mult=SUBLANES):
    flat = jnp.concatenate([a.reshape(-1).astype(F32) for a in arrs])
    n = flat.shape[0]
    per = rows_mult * LANES
    total = -(-n // per) * per
    return jnp.pad(flat, (0, total - n)).reshape(total // LANES, LANES)


def _unpack(pack, shapes):
    flat = pack.reshape(-1)
    out, off = [], 0
    for s in shapes:
        n = 1
        for d in s:
            n *= d
        out.append(flat[off:off + n].reshape(s))
        off += n
    return out


def kernel(x, meta_tokens, mix_norm_g, w_in, ssm_lambda_re, ssm_lambda_im, ssm_log_dt, ssm_b_re, ssm_b_im, ssm_c_re, ssm_c_im, ssm_d, ssm_w_glu, w_ssm_proj, hgrn_lb_logits, hgrn_norm_g, w_hgrn_proj, w_out, ffn_norm_g, w_up, conv_w, conv_b, w_down, final_norm_g, loss_target, m_meta_tokens, m_mix_norm_g, m_w_in, m_ssm_lambda_re, m_ssm_lambda_im, m_ssm_log_dt, m_ssm_b_re, m_ssm_b_im, m_ssm_c_re, m_ssm_c_im, m_ssm_d, m_ssm_w_glu, m_w_ssm_proj, m_hgrn_lb_logits, m_hgrn_norm_g, m_w_hgrn_proj, m_w_out, m_ffn_norm_g, m_w_up, m_conv_w, m_conv_b, m_w_down, m_final_norm_g, v_meta_tokens, v_mix_norm_g, v_w_in, v_ssm_lambda_re, v_ssm_lambda_im, v_ssm_log_dt, v_ssm_b_re, v_ssm_b_im, v_ssm_c_re, v_ssm_c_im, v_ssm_d, v_ssm_w_glu, v_w_ssm_proj, v_hgrn_lb_logits, v_hgrn_norm_g, v_w_hgrn_proj, v_w_out, v_ffn_norm_g, v_w_up, v_conv_w, v_conv_b, v_w_down, v_final_norm_g):
    args = dict(locals())
    B, S_len, D = x.shape
    L = S_len + N_META
    T = B * L
    tm = _tile(L, ROW_TILE_CAP)
    tps = L // tm
    G, P = ssm_lambda_re.shape[1:]
    H = ssm_b_re.shape[-1]
    W = G * H
    n_cb = W // LANES
    gpb = G // n_cb
    hd = hgrn_norm_g.shape[1]
    n_heads = D // hd
    n_in = w_in.shape[2]
    F = w_up.shape[2]
    assert W == D and n_in % LANES == 0

    me = (4 * lax.axis_index("x") + 2 * lax.axis_index("y") + lax.axis_index("c")).astype(jnp.int32).reshape(1)
    meta_g, cw_g = _exchange_call([meta_tokens, conv_w[0]], False, "gather_small_params")
    ga = _exchange_start([w_in[0].astype(MXU)], False, "gather_a_start")
    gb = _exchange_start(
        [w_up[0].astype(MXU), ssm_w_glu[0].astype(MXU), w_ssm_proj[0].astype(MXU), w_hgrn_proj[0].astype(MXU),
         w_out[0].astype(MXU), w_down[0].astype(MXU)], False, "gather_b_start")
    started_tok = (ga[4] + gb[4])[0:1, 0:1]
    meta_full = meta_g.transpose(1, 0, 2).reshape(N_META, D)
    cb_g = conv_b.reshape(N_DEV, 1, F)

    h0 = jnp.concatenate([jnp.broadcast_to(meta_full[None], (B, N_META, D)), x], axis=1).reshape(T, D)
    tgt = jnp.concatenate([jnp.zeros((B, N_META, D), F32), loss_target], axis=1).reshape(T, D)

    lr, li = ssm_lambda_re[0], ssm_lambda_im[0]
    ldt = ssm_log_dt[0].reshape(G, 1)
    bt_re = ssm_b_re[0].transpose(2, 0, 1).reshape(H, G * P)
    bt_im = ssm_b_im[0].transpose(2, 0, 1).reshape(H, G * P)
    seg = _seg_len(L)
    a_re, a_im, as_re, as_im, coef_re, coef_im = _small_call(
        _disc_a_power(seg), [lr, li, ldt], [((G, P), F32)] * 6, "s5_discretise")
    bbt_re, bbt_im = _small_call(
        _disc_b, [coef_re.reshape(1, G * P), coef_im.reshape(1, G * P), bt_re, bt_im],
        [((H, G * P), F32)] * 2, "s5_input_matrix")
    eye = jnp.eye(gpb, dtype=F32)
    hw = gpb * P

    def expand_b(bbt):
        t = bbt.reshape(H, n_cb, gpb, P).transpose(1, 0, 2, 3)[:, None]
        return (t * eye[None, :, None, :, None]).reshape(n_cb, gpb * H, hw)

    def expand_c(cm):
        t = cm.reshape(n_cb, gpb, H, P).transpose(0, 1, 3, 2)[:, :, :, None]
        return (t * eye[None, :, None, :, None]).reshape(n_cb, hw, gpb * H)

    wb = jnp.concatenate([expand_b(bbt_re), expand_b(bbt_im)], axis=2).astype(MXU)
    wc = jnp.concatenate([expand_c(ssm_c_re[0]), -expand_c(ssm_c_im[0])], axis=1).astype(MXU)
    tab = jnp.stack([jnp.concatenate([a_re.reshape(n_cb, hw), a_im.reshape(n_cb, hw)], axis=1),
                     jnp.concatenate([as_re.reshape(n_cb, hw), as_im.reshape(n_cb, hw)], axis=1)], axis=1)
    tab = jnp.broadcast_to(tab[:, :, None, :], (n_cb, 2, SUBLANES, 2 * hw))
    dsk = ssm_d.reshape(n_cb, 1, LANES)
    lb = _small_call(_lb_fn, [hgrn_lb_logits], [((1, D), F32)], "hgrn_lower_bound")[0]

    z1 = _norm_call(h0, mix_norm_g + started_tok, tm, "mix_norm")
    ga_src, ga_land = _exchange_wait(ga, z1, False, "gather_a_wait")
    win_g = _place_own_call(ga_src, ga_land, False, me, "gather_a_own")[0]
    p = _mm_shard(z1, win_g, tm, "in_proj", False)
    p3 = p.reshape(B, L, p.shape[1])
    u_seg = _to_segments(p3[:, :, :W], seg)
    ya_seg, s_all = _s5_fwd_call(u_seg, wb, wc, tab, dsk, "s5_fwd")
    ya = _from_segments(ya_seg, seg, L).reshape(T, W)
    gb_src, gb_land = _exchange_wait(gb, ya, False, "gather_b_wait")
    gathered = _place_own_call(gb_src, gb_land, False, me, "gather_b_own")
    wup_g = gathered[0]
    wglu_g, wsp_g, whp_g, wout_g = [g.reshape(D, D) for g in gathered[1:5]]
    wdn_g = gathered[5].reshape(N_DEV // 2, 2 * w_down.shape[1], D)
    yo, a_br = _glu_proj_call(ya, wglu_g, wsp_g, tm, "s5_glu_proj")
    yb = _hgrn_fwd_call(p3, lb, hgrn_norm_g, n_heads, n_cb, "hgrn_fwd").reshape(T, D)
    col_ga = 5
    h1, mg, bm, z2 = _merge_call(yb, a_br, p, h0, whp_g, wout_g, ffn_norm_g, col_ga, tm, "merge")
    up = _mm_shard(z2, wup_g, tm, "up_proj", True)
    act, dh2, loss_part, dg3 = _ffn_fwd_call(up, cw_g, cb_g, wdn_g, h1, tgt, final_norm_g.reshape(1, D),
                                             tm, tps, "ffn_out_loss")

    dua, dub, dwd, dcwa, dcwb, dcba, dcbb = _ffn_bwd_a_call(dh2, up, act, cw_g, cb_g, wdn_g, tm, tps, "ffn_bwd_gate")
    dupa, dupb, dh1, dg2 = _ffn_bwd_b_call(dua, dub, cw_g, wup_g, h1, ffn_norm_g, dh2, tm, tps, "ffn_bwd_up")
    dwup = jnp.concatenate([_mm_tn(z2, dupa, N_DEV // 2, tm, "dw_up_a", True),
                            _mm_tn(z2, dupb, N_DEV // 2, tm, "dw_up_b", True)], axis=0)
    sh_rows = D // N_DEV
    sa = _exchange_start([dwup.astype(WIRE), dwd.reshape(N_DEV, w_down.shape[1], D).astype(WIRE)], True,
                         "scatter_a_start")
    dmg, dwout = _lin_bwd(mg, dh1, wout_g + sa[4][0:1, 0:1].astype(MXU), tm, "out_proj_bwd")
    da_br, dbm, dga, dgb = _merge_bwd_call(dmg, a_br, bm, p, col_ga, tm, "merge_bwd")
    dyo, dwsp = _lin_bwd(yo, da_br, wsp_g, tm, "ssm_proj_bwd")
    dyb, dwhp = _lin_bwd(yb, dbm, whp_g, tm, "hgrn_proj_bwd")
    dya, dwglu = _glu_bwd_call(ya, dyo, wglu_g, tm, "s5_glu_bwd")
    sb = _exchange_start([t.reshape(N_DEV, sh_rows, D).astype(WIRE) for t in (dwglu, dwsp, dwhp, dwout)], True,
                         "scatter_b_start")
    tok_b = sb[4][0:1, :]
    du_seg, dwb, dwc, dab, ddsk = _s5_bwd_call(u_seg, s_all, _to_segments(dya.reshape(B, L, W), seg), wb, wc, tab,
                                               dsk + tok_b[None], "s5_bwd")
    du = _from_segments(du_seg, seg, L)

    def diag_b(dw):
        t = (dw.reshape(n_cb, gpb, H, gpb, P) * eye[None, :, None, :, None]).sum(axis=1)
        return t.transpose(1, 0, 2, 3).reshape(H, G * P)

    def diag_c(dw):
        t = (dw.reshape(n_cb, gpb, P, gpb, H) * eye[None, :, None, :, None]).sum(axis=3)
        return t.transpose(0, 1, 3, 2).reshape(G, H, P)

    early_parts = [dab[:, 0, :hw].reshape(G, P), dab[:, 0, hw:].reshape(G, P),
                   diag_b(dwb[:, :, :hw]), diag_b(dwb[:, :, hw:]),
                   diag_c(dwc[:, :hw]), -diag_c(dwc[:, hw:]), ddsk.reshape(1, D)]
    early_pack = _pack(early_parts)
    se = _exchange_start([early_pack], False, "gather_s5_grads_start")
    dq, dfl, di, dog, dlb, dng = _hgrn_bwd_call(p3, dyb.reshape(B, L, D), lb, hgrn_norm_g + tok_b + se[4][0:1, :],
                                                n_heads, n_cb, "hgrn_bwd")
    dp = jnp.concatenate([du.reshape(T, W), dq.reshape(T, D), dfl.reshape(T, D), di.reshape(T, D),
                          dog.reshape(T, D), dga, dgb], axis=1)
    dwin = _mm_tn(z1, dp, N_DEV, tm, "dw_in", False)
    sc = _exchange_start([dwin.astype(WIRE)], True, "scatter_c_start")
    dh0, dg1 = _in_bwd_call(dp, win_g, h0, mix_norm_g + sc[4][0:1, 0:1], dh1, tm, "in_proj_bwd")
    dh0_3 = dh0.reshape(B, L, D)
    grad_x = dh0_3[:, N_META:]
    dmeta = _meta_grad_call(dh0_3, "meta_grad")

    late_parts = [dg1, dlb, dng, dg2, jnp.concatenate([dcba, dcbb], axis=0).reshape(1, N_DEV * F), dg3, loss_part]
    late_pack = _pack(late_parts)

    dcw = jnp.concatenate([dcwa, dcwb], axis=0)
    dmeta_s = dmeta.reshape(N_META, N_DEV, D // N_DEV).transpose(1, 0, 2)
    parts_d = _exchange_call([dmeta_s, dcw], True, "scatter_small_grads")
    late_all = _exchange_call([late_pack], False, "gather_small_grads")[0]
    early_all = _place_own_call(*_exchange_wait(se, late_all, False, "gather_s5_grads_wait"), False, me,
                                "gather_s5_grads_own")[0]
    parts_a = _place_own_call(*_exchange_wait(sa, late_all, True, "scatter_a_wait"), True, me, "scatter_a_own")
    parts_b = _place_own_call(*_exchange_wait(sb, late_all, True, "scatter_b_wait"), True, me, "scatter_b_own")
    parts_c = _place_own_call(*_exchange_wait(sc, late_all, True, "scatter_c_wait"), True, me, "scatter_c_own")
    parts = [parts_c[0], parts_a[0], *parts_b, parts_a[1], parts_d[0], parts_d[1]]

    def sum8(a, b):
        ta, tb = a[0], b[0]
        for s in range(1, N_DEV):
            ta, tb = ta + a[s], tb + b[s]
        return ta, tb

    early_sum, late_sum = _small_call(sum8, [early_all, late_all], [(early_pack.shape, F32), (late_pack.shape, F32)],
                                      "sum_small_grads")
    t_abr, t_abi, t_bbr, t_bbi, g_cre, g_cim, g_dsk = _unpack(early_sum, [a.shape for a in early_parts])
    g_g1, t_lb, g_ng, g_g2, g_cb, g_g3, loss_v = _unpack(late_sum, [a.shape for a in late_parts])

    def disc_b_bwd(cr, ci, br, bi, dbr, dbi):
        _, vjp = jax.vjp(_disc_b, cr, ci, br, bi)
        return vjp((dbr, dbi))

    t_cr, t_ci, g_btr, g_bti = _small_call(
        disc_b_bwd, [coef_re.reshape(1, G * P), coef_im.reshape(1, G * P), bt_re, bt_im, t_bbr, t_bbi],
        [((1, G * P), F32)] * 2 + [((H, G * P), F32)] * 2, "s5_input_matrix_bwd")

    def disc_a_bwd(lr_, li_, ldt_, dar, dai, dcr, dci):
        _, vjp = jax.vjp(_disc_a, lr_, li_, ldt_)
        return vjp((dar, dai, dcr, dci))

    g_lr, g_li, g_ldt = _small_call(
        disc_a_bwd, [lr, li, ldt, t_abr, t_abi, t_cr.reshape(G, P), t_ci.reshape(G, P)],
        [((G, P), F32)] * 2 + [((G, 1), F32)], "s5_discretise_bwd")

    def lb_bwd(logits, d):
        _, vjp = jax.vjp(_lb_fn, logits)
        return vjp(d)

    g_lbl = _small_call(lb_bwd, [hgrn_lb_logits, t_lb], [(hgrn_lb_logits.shape, F32)], "hgrn_lower_bound_bwd")[0]

    grads = dict(
        mix_norm_g=g_g1, ssm_lambda_re=g_lr[None], ssm_lambda_im=g_li[None], ssm_log_dt=g_ldt.reshape(1, G),
        ssm_b_re=g_btr.reshape(H, G, P).transpose(1, 2, 0)[None], ssm_b_im=g_bti.reshape(H, G, P).transpose(1, 2, 0)[None],
        ssm_c_re=g_cre[None], ssm_c_im=g_cim[None], ssm_d=g_dsk, hgrn_lb_logits=g_lbl, hgrn_norm_g=g_ng,
        ffn_norm_g=g_g2, conv_b=g_cb.reshape(1, N_DEV * F), final_norm_g=g_g3.reshape(D))
    loss = loss_v[0, 0]

    delta, new_m, new_v = {}, {}, {}
    sharded = [("w_in", parts[0], (D, n_in)), ("w_up", parts[1], (D, F)), ("ssm_w_glu", parts[2], (sh_rows, D)),
               ("w_ssm_proj", parts[3], (sh_rows, D)), ("w_hgrn_proj", parts[4], (sh_rows, D)),
               ("w_out", parts[5], (sh_rows, D)), ("w_down", parts[6], (w_down.shape[1], D)),
               ("meta_tokens", parts[7], (N_META, D // N_DEV)), ("conv_w", parts[8], (3, F))]
    for name, part, shp in sharded:
        full = args[name].shape
        g, d_, nm, nv = _adamw_shard_call(args[name].reshape(shp), part, args["m_" + name].reshape(shp),
                                          args["v_" + name].reshape(shp), "adamw_" + name)
        grads[name], delta[name], new_m[name], new_v[name] = [t.reshape(full) for t in (g, d_, nm, nv)]

    rep = ["mix_norm_g", "ssm_lambda_re", "ssm_lambda_im", "ssm_log_dt", "ssm_b_re", "ssm_b_im", "ssm_c_re",
           "ssm_c_im", "ssm_d", "hgrn_lb_logits", "hgrn_norm_g", "ffn_norm_g", "conv_b", "final_norm_g"]
    rep_shapes = [args[n].shape for n in rep]
    packs = [_pack([args[pre + n] for n in rep]) for pre in ("", "m_", "v_")]
    g_pack = _pack([grads[n] for n in rep])
    outs = _small_call(lambda w, g, m, v: _adamw(w, g, m, v), [packs[0], g_pack, packs[1], packs[2]],
                       [(g_pack.shape, F32)] * 3, "adamw_replicated")
    for n, d_, nm, nv in zip(rep, *[_unpack(o, rep_shapes) for o in outs]):
        delta[n], new_m[n], new_v[n] = d_, nm, nv

    names = ["meta_tokens", "mix_norm_g", "w_in", "ssm_lambda_re", "ssm_lambda_im", "ssm_log_dt", "ssm_b_re",
             "ssm_b_im", "ssm_c_re", "ssm_c_im", "ssm_d", "ssm_w_glu", "w_ssm_proj", "hgrn_lb_logits", "hgrn_norm_g",
             "w_hgrn_proj", "w_out", "ffn_norm_g", "w_up", "conv_w", "conv_b", "w_down", "final_norm_g"]
    return (loss, grad_x, *[grads[n] for n in names], *[delta[n] for n in names],
            *[new_m[n] for n in names], *[new_v[n] for n in names])
```

```python
import functools

import jax
import jax.numpy as jnp
from jax import lax
from jax.experimental import pallas as pl
from jax.experimental.pallas import tpu as pltpu

F32 = jnp.float32
MXU = jnp.bfloat16
ACT = jnp.bfloat16
WIRE = jnp.bfloat16
N_DEV = 8
N_META = 16
CHUNK = 16
EPS = 1e-6
ADAM_LR, ADAM_B1, ADAM_B2, ADAM_EPS, ADAM_WD, ADAM_STEP = 0.001, 0.9, 0.999, 1e-08, 0.01, 10
SUBLANES = 8
LANES = 128
ROW_TILE_CAP = 700
VMEM_LIMIT = 60 * 1024 * 1024


def _cparams(**kw):
    return pltpu.CompilerParams(vmem_limit_bytes=VMEM_LIMIT, **kw)


def _tile(n, cap):
    best = None
    for t in range(16, min(n, cap) + 1, 16):
        if n % t == 0:
            best = t
    assert best is not None, (n, cap)
    return best


def _dot(a, b):
    return lax.dot_general(a.astype(MXU), b.astype(MXU), (((1,), (0,)), ((), ())), preferred_element_type=F32)


def _dot_nt(a, b):
    return lax.dot_general(a.astype(MXU), b.astype(MXU), (((1,), (1,)), ((), ())), preferred_element_type=F32)


def _dot_tn(a, b):
    return lax.dot_general(a.astype(MXU), b.astype(MXU), (((0,), (0,)), ((), ())), preferred_element_type=F32)


def _rms(x, g):
    return x * lax.rsqrt(jnp.mean(x * x, axis=-1, keepdims=True) + EPS) * g


def _silu(x):
    return x * jax.nn.sigmoid(x)


def _small_call(fn, ins, out_shapes, name):
    n_in = len(ins)

    def body(*refs):
        outs = fn(*[r[...] for r in refs[:n_in]])
        outs = outs if isinstance(outs, (tuple, list)) else (outs,)
        for r, o in zip(refs[n_in:], outs):
            r[...] = o.astype(r.dtype)

    vm = pl.BlockSpec(memory_space=pltpu.VMEM)
    return pl.pallas_call(
        body, name=name, out_shape=tuple(jax.ShapeDtypeStruct(s, d) for s, d in out_shapes),
        in_specs=[vm] * n_in, out_specs=tuple([vm] * len(out_shapes)), compiler_params=_cparams())(*ins)


def _disc_a(lr, li, ldt):
    dt = jnp.exp(ldt)
    mag = jnp.exp(lr * dt)
    ab_re = mag * jnp.cos(li * dt)
    ab_im = mag * jnp.sin(li * dt)
    den = lr * lr + li * li
    nr = ab_re - 1.0
    coef_re = (nr * lr + ab_im * li) / den
    coef_im = (ab_im * lr - nr * li) / den
    return ab_re, ab_im, coef_re, coef_im


def _disc_a_power(n):
    def fn(lr, li, ldt):
        ab_re, ab_im, coef_re, coef_im = _disc_a(lr, li, ldt)
        pr, pi, sr, si, m = None, None, ab_re, ab_im, n
        while m:
            if m & 1:
                pr, pi = (sr, si) if pr is None else (pr * sr - pi * si, pr * si + pi * sr)
            m >>= 1
            if m:
                sr, si = sr * sr - si * si, 2.0 * sr * si
        return ab_re, ab_im, pr, pi, coef_re, coef_im
    return fn


def _disc_b(coef_re, coef_im, bt_re, bt_im):
    return coef_re * bt_re - coef_im * bt_im, coef_re * bt_im + coef_im * bt_re


def _lb_fn(logits):
    return jax.nn.softmax(logits, axis=0)[0:1]


def _adamw(w, g, m, v):
    m = ADAM_B1 * m + (1.0 - ADAM_B1) * g
    v = ADAM_B2 * v + (1.0 - ADAM_B2) * jnp.square(g)
    m_hat = m / (1.0 - ADAM_B1 ** ADAM_STEP)
    v_hat = v / (1.0 - ADAM_B2 ** ADAM_STEP)
    delta = -ADAM_LR * (m_hat / (jnp.sqrt(v_hat) + ADAM_EPS) + ADAM_WD * w)
    return delta, m, v


def _norm_call(h, g, tm, name):
    T, D = h.shape

    def body(h_ref, g_ref, z_ref):
        z_ref[...] = _rms(h_ref[...], g_ref[...]).astype(ACT)

    return pl.pallas_call(
        body, name=name, out_shape=jax.ShapeDtypeStruct((T, D), ACT), grid=(T // tm,),
        in_specs=[pl.BlockSpec((tm, D), lambda i: (i, 0)), pl.BlockSpec((1, D), lambda i: (0, 0))],
        out_specs=pl.BlockSpec((tm, D), lambda i: (i, 0)), compiler_params=_cparams())(h, g)


def _mm_shard(x, w, tm, name, major):
    T, K = x.shape
    S, _, N = w.shape

    def body(x_ref, w_ref, o_ref):
        o_ref[...] = _dot(x_ref[...], w_ref[...]).astype(o_ref.dtype)

    if major:
        out_shape = jax.ShapeDtypeStruct((S, T, N), ACT)
        out_spec = pl.BlockSpec((None, tm, N), lambda j, i: (j, i, 0))
    else:
        out_shape = jax.ShapeDtypeStruct((T, S * N), ACT)
        out_spec = pl.BlockSpec((tm, N), lambda j, i: (i, j))
    return pl.pallas_call(
        body, name=name, out_shape=out_shape, grid=(S, T // tm),
        in_specs=[pl.BlockSpec((tm, K), lambda j, i: (i, 0)), pl.BlockSpec((None, K, N), lambda j, i: (j, 0, 0))],
        out_specs=out_spec, compiler_params=_cparams())(x, w)


def _mm_tn(x, y, n_shards, tm, name, major):
    T, K = x.shape
    S = n_shards
    N = y.shape[-1] if major else y.shape[-1] // S

    def body(x_ref, y_ref, o_ref):
        @pl.when(pl.program_id(1) == 0)
        def _():
            o_ref[...] = jnp.zeros_like(o_ref)
        o_ref[...] += _dot_tn(x_ref[...], y_ref[...])

    y_spec = (pl.BlockSpec((None, tm, N), lambda j, i: (j, i, 0)) if major
              else pl.BlockSpec((tm, N), lambda j, i: (i, j)))
    return pl.pallas_call(
        body, name=name, out_shape=jax.ShapeDtypeStruct((S, K, N), F32), grid=(S, T // tm),
        in_specs=[pl.BlockSpec((tm, K), lambda j, i: (i, 0)), y_spec],
        out_specs=pl.BlockSpec((None, K, N), lambda j, i: (j, 0, 0)), compiler_params=_cparams())(x, y)


def _lin_bwd(x, dy, w, tm, name):
    T, K = x.shape
    N = dy.shape[1]

    def body(x_ref, dy_ref, w_ref, dx_ref, dw_ref):
        @pl.when(pl.program_id(0) == 0)
        def _():
            dw_ref[...] = jnp.zeros_like(dw_ref)
        dy = dy_ref[...]
        dx_ref[...] = _dot_nt(dy, w_ref[...]).astype(dx_ref.dtype)
        dw_ref[...] += _dot_tn(x_ref[...], dy)

    return pl.pallas_call(
        body, name=name,
        out_shape=(jax.ShapeDtypeStruct((T, K), ACT), jax.ShapeDtypeStruct((K, N), F32)), grid=(T // tm,),
        in_specs=[pl.BlockSpec((tm, K), lambda i: (i, 0)), pl.BlockSpec((tm, N), lambda i: (i, 0)),
                  pl.BlockSpec((K, N), lambda i: (0, 0))],
        out_specs=(pl.BlockSpec((tm, K), lambda i: (i, 0)), pl.BlockSpec((K, N), lambda i: (0, 0))),
        compiler_params=_cparams())(x, dy, w)


N_SEG = SUBLANES


def _seg_len(L):
    return -(-L // (N_SEG * SUBLANES)) * SUBLANES


def _to_segments(a3, seg):
    b, length, c = a3.shape
    a = jnp.pad(a3, ((0, 0), (0, N_SEG * seg - length), (0, 0)))
    return a.reshape(b, N_SEG, seg, c).transpose(0, 2, 1, 3).reshape(b, N_SEG * seg, c)


def _from_segments(a3, seg, length):
    b, _, c = a3.shape
    return a3.reshape(b, seg, N_SEG, c).transpose(0, 2, 1, 3).reshape(b, N_SEG * seg, c)[:, :length]


def _seg_scan(x_ref, tab_ref, n_slabs, reverse):
    hw = x_ref.shape[1] // 2
    sign = -1.0 if reverse else 1.0
    ar, ai = tab_ref[0][:, :hw], sign * tab_ref[0][:, hw:]
    br, bi = tab_ref[1][:, :hw], sign * tab_ref[1][:, hw:]

    def slab(k):
        kk = (n_slabs - 1 - k) if reverse else k
        return pl.ds(pl.multiple_of(kk * SUBLANES, SUBLANES), SUBLANES)

    def horner(k, carry):
        cr, ci = carry
        x = x_ref[slab(k), :]
        return ar * cr - ai * ci + x[:, :hw], ar * ci + ai * cr + x[:, hw:]

    z = jnp.zeros((SUBLANES, hw), F32)
    fr, fi = lax.fori_loop(0, n_slabs, horner, (z, z))

    row = lax.broadcasted_iota(jnp.int32, (SUBLANES, hw), 0)
    edge = (row == SUBLANES - 1) if reverse else (row == 0)
    shift = SUBLANES - 1 if reverse else 1
    sr, si = z, z
    for _ in range(N_SEG - 1):
        er, ei = fr + br * sr - bi * si, fi + br * si + bi * sr
        sr = jnp.where(edge, 0.0, pltpu.roll(er, shift, 0))
        si = jnp.where(edge, 0.0, pltpu.roll(ei, shift, 0))

    def scan(k, carry):
        cr, ci = carry
        rows = slab(k)
        x = x_ref[rows, :]
        nr, ni = ar * cr - ai * ci + x[:, :hw], ar * ci + ai * cr + x[:, hw:]
        x_ref[rows, 0:hw] = nr
        x_ref[rows, hw:2 * hw] = ni
        return nr, ni

    lax.fori_loop(0, n_slabs, scan, (sr, si))


def _s5_fwd_call(p3, wb, wc, tab_f, dsk, name):
    B, L, _ = p3.shape
    n_cb, cw, sw = wb.shape

    def body(u_ref, wb_ref, wc_ref, tab_ref, d_ref, ya_ref, so_ref, s_ref):
        u = u_ref[...]
        s_ref[...] = _dot(u, wb_ref[...])
        _seg_scan(s_ref, tab_ref, L // SUBLANES, False)
        s = s_ref[...].astype(MXU)
        so_ref[...] = s
        y = _dot(s, wc_ref[...]) + d_ref[...] * u.astype(F32)
        ya_ref[...] = jax.nn.gelu(y).astype(ACT)

    return pl.pallas_call(
        body, name=name,
        out_shape=(jax.ShapeDtypeStruct((B, L, n_cb * cw), ACT), jax.ShapeDtypeStruct((B, n_cb, L, sw), MXU)),
        grid=(B, n_cb),
        in_specs=[pl.BlockSpec((None, L, cw), lambda b, c: (b, 0, c)),
                  pl.BlockSpec((None, cw, sw), lambda b, c: (c, 0, 0)),
                  pl.BlockSpec((None, sw, cw), lambda b, c: (c, 0, 0)),
                  pl.BlockSpec((None, 2, SUBLANES, sw), lambda b, c: (c, 0, 0, 0)),
                  pl.BlockSpec((None, 1, cw), lambda b, c: (c, 0, 0))],
        out_specs=(pl.BlockSpec((None, L, cw), lambda b, c: (b, 0, c)),
                   pl.BlockSpec((None, None, L, sw), lambda b, c: (b, c, 0, 0))),
        scratch_shapes=[pltpu.VMEM((L, sw), F32)], compiler_params=_cparams())(p3, wb, wc, tab_f, dsk)


def _s5_bwd_call(p3, s_all, dya, wb, wc, tab_r, dsk, name):
    B, L, _ = p3.shape
    n_cb, cw, sw = wb.shape
    hw = sw // 2
    n_slabs = L // SUBLANES

    def body(u_ref, si_ref, dya_ref, wb_ref, wc_ref, tr_ref, d_ref,
             du_ref, dwb_ref, dwc_ref, da_ref, dd_ref, s_ref, l_ref):
        @pl.when(pl.program_id(1) == 0)
        def _():
            dwb_ref[...] = jnp.zeros_like(dwb_ref)
            dwc_ref[...] = jnp.zeros_like(dwc_ref)
            da_ref[...] = jnp.zeros_like(da_ref)
            dd_ref[...] = jnp.zeros_like(dd_ref)

        u = u_ref[...]
        uf = u.astype(F32)
        s_in = si_ref[...]
        s_ref[...] = s_in.astype(F32)
        y = _dot(s_in, wc_ref[...]) + d_ref[...] * uf
        _, gelu_vjp = jax.vjp(jax.nn.gelu, y)
        dy = gelu_vjp(dya_ref[...].astype(F32))[0]
        dd_ref[...] += jnp.sum(dy * uf, axis=0, keepdims=True)
        l_ref[...] = _dot_nt(dy, wc_ref[...])
        _seg_scan(l_ref, tr_ref, n_slabs, True)
        du_ref[...] = (_dot_nt(l_ref[...], wb_ref[...]) + d_ref[...] * dy).astype(ACT)
        dwb_ref[...] += _dot_tn(u, l_ref[...])
        dwc_ref[...] += _dot_tn(s_in, dy)

        row = lax.broadcasted_iota(jnp.int32, (SUBLANES, hw), 0)
        last = s_ref[pl.ds((n_slabs - 1) * SUBLANES, SUBLANES), :]
        p0r = jnp.where(row == 0, 0.0, pltpu.roll(last[:, :hw], 1, 0))
        p0i = jnp.where(row == 0, 0.0, pltpu.roll(last[:, hw:], 1, 0))

        def step(k, carry):
            qr, qi, accr, acci = carry
            r0 = pl.multiple_of(k * SUBLANES, SUBLANES)
            s = s_ref[pl.ds(r0, SUBLANES), :]
            lam = l_ref[pl.ds(r0, SUBLANES), :]
            lr, li = lam[:, :hw], lam[:, hw:]
            accr = accr + lr * qr + li * qi
            acci = acci + li * qr - lr * qi
            return s[:, :hw], s[:, hw:], accr, acci

        z8 = jnp.zeros((SUBLANES, hw), F32)
        _, _, accr, acci = lax.fori_loop(0, n_slabs, step, (p0r, p0i, z8, z8))
        da_ref[...] += jnp.concatenate([jnp.sum(accr, axis=0, keepdims=True),
                                        jnp.sum(acci, axis=0, keepdims=True)], axis=1)

    W = n_cb * cw
    return pl.pallas_call(
        body, name=name,
        out_shape=(jax.ShapeDtypeStruct((B, L, W), ACT), jax.ShapeDtypeStruct((n_cb, cw, sw), F32),
                   jax.ShapeDtypeStruct((n_cb, sw, cw), F32), jax.ShapeDtypeStruct((n_cb, 1, sw), F32),
                   jax.ShapeDtypeStruct((n_cb, 1, cw), F32)),
        grid=(n_cb, B),
        in_specs=[pl.BlockSpec((None, L, cw), lambda c, b: (b, 0, c)),
                  pl.BlockSpec((None, None, L, sw), lambda c, b: (b, c, 0, 0)),
                  pl.BlockSpec((None, L, cw), lambda c, b: (b, 0, c)),
                  pl.BlockSpec((None, cw, sw), lambda c, b: (c, 0, 0)),
                  pl.BlockSpec((None, sw, cw), lambda c, b: (c, 0, 0)),
                  pl.BlockSpec((None, 2, SUBLANES, sw), lambda c, b: (c, 0, 0, 0)),
                  pl.BlockSpec((None, 1, cw), lambda c, b: (c, 0, 0))],
        out_specs=(pl.BlockSpec((None, L, cw), lambda c, b: (b, 0, c)),
                   pl.BlockSpec((None, cw, sw), lambda c, b: (c, 0, 0)),
                   pl.BlockSpec((None, sw, cw), lambda c, b: (c, 0, 0)),
                   pl.BlockSpec((None, 1, sw), lambda c, b: (c, 0, 0)),
                   pl.BlockSpec((None, 1, cw), lambda c, b: (c, 0, 0))),
        scratch_shapes=[pltpu.VMEM((L, sw), F32), pltpu.VMEM((L, sw), F32)],
        compiler_params=_cparams())(p3, s_all, dya, wb, wc, tab_r, dsk)


def _glu_proj_call(ya, wglu, wproj, tm, name):
    T, W = ya.shape
    D = wproj.shape[1]

    def body(ya_ref, wg_ref, wp_ref, yo_ref, a_ref):
        ya = ya_ref[...]
        yo = ya.astype(F32) * jax.nn.sigmoid(_dot(ya, wg_ref[...]))
        yo_ref[...] = yo.astype(ACT)
        a_ref[...] = _dot(yo, wp_ref[...]).astype(ACT)

    return pl.pallas_call(
        body, name=name, out_shape=(jax.ShapeDtypeStruct((T, W), ACT), jax.ShapeDtypeStruct((T, D), ACT)),
        grid=(T // tm,),
        in_specs=[pl.BlockSpec((tm, W), lambda i: (i, 0)), pl.BlockSpec((W, W), lambda i: (0, 0)),
                  pl.BlockSpec((W, D), lambda i: (0, 0))],
        out_specs=(pl.BlockSpec((tm, W), lambda i: (i, 0)), pl.BlockSpec((tm, D), lambda i: (i, 0))),
        compiler_params=_cparams())(ya, wglu, wproj)


def _glu_bwd_call(ya, dyo, wglu, tm, name):
    T, W = ya.shape

    def body(ya_ref, dyo_ref, wg_ref, dya_ref, dwg_ref):
        @pl.when(pl.program_id(0) == 0)
        def _():
            dwg_ref[...] = jnp.zeros_like(dwg_ref)
        ya = ya_ref[...]
        yaf = ya.astype(F32)
        dyo = dyo_ref[...].astype(F32)
        sg = jax.nn.sigmoid(_dot(ya, wg_ref[...]))
        dt = dyo * yaf * sg * (1.0 - sg)
        dya_ref[...] = (dyo * sg + _dot_nt(dt, wg_ref[...])).astype(ACT)
        dwg_ref[...] += _dot_tn(ya, dt)

    return pl.pallas_call(
        body, name=name, out_shape=(jax.ShapeDtypeStruct((T, W), ACT), jax.ShapeDtypeStruct((W, W), F32)),
        grid=(T // tm,),
        in_specs=[pl.BlockSpec((tm, W), lambda i: (i, 0)), pl.BlockSpec((tm, W), lambda i: (i, 0)),
                  pl.BlockSpec((W, W), lambda i: (0, 0))],
        out_specs=(pl.BlockSpec((tm, W), lambda i: (i, 0)), pl.BlockSpec((W, W), lambda i: (0, 0))),
        compiler_params=_cparams())(ya, dyo, wglu)


PAD = 16


def _chunk_cumsums(x, pad_ref, L):
    row = lax.broadcasted_iota(jnp.int32, x.shape, 0) % CHUNK
    zeros = jnp.zeros((PAD, x.shape[1]), F32)
    pad_ref[0:PAD, :] = zeros
    pad_ref[PAD + L:2 * PAD + L, :] = zeros
    c = x
    r = x
    d = 1
    while d < CHUNK:
        pad_ref[PAD:PAD + L, :] = c
        c = c + jnp.where(row >= d, pad_ref[PAD - d:PAD - d + L, :], 0.0)
        pad_ref[PAD:PAD + L, :] = r
        r = r + jnp.where(row + d < CHUNK, pad_ref[PAD + d:PAD + d + L, :], 0.0)
        d *= 2
    return c, r - x


def _hgrn_prep(q_ref, fl_ref, lb_ref, pad_ref, r0, n):
    rows = pl.ds(r0, n)
    lb = lb_ref[...]
    sig = jax.nn.sigmoid(fl_ref[rows, :].astype(F32))
    f = lb + (1.0 - lb) * sig
    k = 1.0 - f
    c, rc = _chunk_cumsums(jnp.log(f), pad_ref, n)
    e_in, e_inv, e_out = jnp.exp(c), jnp.exp(-c), jnp.exp(rc)
    q = q_ref[rows, :].astype(F32)
    return dict(sig=sig, f=f, k=k, q=q, e_in=e_in, e_inv=e_inv, e_out=e_out, dec=jnp.exp(c + rc))


def _for_row_blocks(L, fn):
    full = L // GROUP
    if full:
        def step(g, carry):
            fn(pl.multiple_of(g * GROUP, GROUP), GROUP)
            return carry
        lax.fori_loop(0, full, step, 0)
    if L % GROUP:
        fn(full * GROUP, L % GROUP)


def _chunk_mask(rb):
    r = lax.broadcasted_iota(jnp.int32, (rb, rb), 0)
    c = lax.broadcasted_iota(jnp.int32, (rb, rb), 1)
    return (r // CHUNK == c // CHUNK) & (c <= r)


def _hg_out(o, og, g):
    on = o * lax.rsqrt(jnp.mean(o * o, axis=-1, keepdims=True) + EPS) * g
    return on * _silu(og)


def _hgrn_specs(L, hd, col_q, n_heads, order):
    def spec(sec):
        return pl.BlockSpec((None, L, hd), lambda *g: (order(*g)[0], 0, col_q + sec * n_heads + order(*g)[1]))
    return [spec(0), spec(1), spec(2), spec(3)]


GROUP = 128
CPG = GROUP // CHUNK


def _expand(x):
    xf = x.astype(F32)
    chunk = lax.broadcasted_iota(jnp.int32, xf.shape, 0) // CHUNK
    return jnp.concatenate([jnp.where(chunk == j, xf, 0.0) for j in range(CPG)], axis=1)


def _fill_tail(refs_fills, L):
    for ref, fill in refs_fills:
        if ref.shape[0] > L:
            ref[L:ref.shape[0], :] = jnp.full((ref.shape[0] - L, ref.shape[1]), fill, ref.dtype)


GROUP_UNROLL = 4


def _hgrn_forward_core(q_ref, fl_ref, v_ref, lb_ref, pad_ref, qin_ref, kin_ref, kout_ref, vp_ref, dec_ref, o_ref,
                       s_ref, a_ref, L, keep=()):
    hd = qin_ref.shape[1]
    n_groups = qin_ref.shape[0] // GROUP

    def prep(r0, n):
        pp = _hgrn_prep(q_ref, fl_ref, lb_ref, pad_ref, r0, n)
        rows = pl.ds(r0, n)
        for key, ref in keep:
            ref[rows, :] = pp[key]
        qin_ref[rows, :] = (pp["q"] * pp["e_in"]).astype(MXU)
        kin_ref[rows, :] = (pp["k"] * pp["e_inv"]).astype(MXU)
        kout_ref[rows, :] = (pp["k"] * pp["e_out"]).astype(MXU)
        vp_ref[rows, :] = v_ref[rows, :].astype(MXU)
        dec_ref[rows, :] = pp["dec"]

    _for_row_blocks(L, prep)
    _fill_tail(((qin_ref, 0.0), (kin_ref, 0.0), (kout_ref, 0.0), (vp_ref, 0.0), (dec_ref, 1.0)), L)
    mask = _chunk_mask(GROUP)

    def scores(g, carry):
        rows = pl.ds(pl.multiple_of(g * GROUP, GROUP), GROUP)
        a_ref[rows, :] = jnp.where(mask, _dot_nt(qin_ref[rows, :], kin_ref[rows, :]), 0.0).astype(MXU)
        return carry

    lax.fori_loop(0, n_groups, scores, 0, unroll=GROUP_UNROLL)

    def intra(g, carry):
        rows = pl.ds(pl.multiple_of(g * GROUP, GROUP), GROUP)
        o_ref[rows, :] = _dot(a_ref[rows, :], vp_ref[rows, :])
        kv = _dot_tn(vp_ref[rows, :], _expand(kout_ref[rows, :]))
        for j in range(CPG):
            s_ref[g * CPG + j] = kv[:, j * hd:(j + 1) * hd]
        return carry

    lax.fori_loop(0, n_groups, intra, 0, unroll=GROUP_UNROLL)

    def rec(n, st):
        kv = s_ref[n]
        s_ref[n] = st
        dec = dec_ref[pl.ds(pl.multiple_of(n * CHUNK, CHUNK), SUBLANES), :][0:1]
        return st * dec + kv

    lax.fori_loop(0, L // CHUNK, rec, jnp.zeros((hd, hd), F32))

    def inter(g, carry):
        rows = pl.ds(pl.multiple_of(g * GROUP, GROUP), GROUP)
        scat = jnp.concatenate([s_ref[g * CPG + j] for j in range(CPG)], axis=1)
        o_ref[rows, :] += _dot_nt(_expand(qin_ref[rows, :]), scat)
        return carry

    lax.fori_loop(0, n_groups, inter, 0, unroll=GROUP_UNROLL)


def _hgrn_scratch(L, hd):
    lp = -(-L // GROUP) * GROUP
    return lp, [pltpu.VMEM((GROUP + 2 * PAD, hd), F32), pltpu.VMEM((lp, hd), MXU), pltpu.VMEM((lp, hd), MXU),
                pltpu.VMEM((lp, hd), MXU), pltpu.VMEM((lp, hd), MXU), pltpu.VMEM((lp, hd), F32),
                pltpu.VMEM((lp, hd), F32), pltpu.VMEM((lp // CHUNK, hd, hd), F32), pltpu.VMEM((lp, GROUP), MXU)]


def _hgrn_fwd_call(p3, lb, ng, n_heads, col_q, name):
    B, L, _ = p3.shape
    hd = ng.shape[1]
    _, scratch = _hgrn_scratch(L, hd)

    def body(q_ref, fl_ref, v_ref, og_ref, lb_ref, ng_ref, yb_ref,
             pad_ref, qin_ref, kin_ref, kout_ref, vp_ref, dec_ref, o_ref, s_ref, a_ref):
        _hgrn_forward_core(q_ref, fl_ref, v_ref, lb_ref, pad_ref, qin_ref, kin_ref, kout_ref, vp_ref, dec_ref,
                           o_ref, s_ref, a_ref, L)

        def out(r0, n):
            rows = pl.ds(r0, n)
            yb_ref[rows, :] = _hg_out(o_ref[rows, :], og_ref[rows, :].astype(F32), ng_ref[...]).astype(ACT)

        _for_row_blocks(L, out)

    order = lambda b, h: (b, h)
    return pl.pallas_call(
        body, name=name, out_shape=jax.ShapeDtypeStruct((B, L, n_heads * hd), ACT), grid=(B, n_heads),
        in_specs=_hgrn_specs(L, hd, col_q, n_heads, order) + [
            pl.BlockSpec((1, hd), lambda b, h: (0, h)), pl.BlockSpec((1, hd), lambda b, h: (0, 0))],
        out_specs=pl.BlockSpec((None, L, hd), lambda b, h: (b, 0, h)),
        scratch_shapes=scratch, compiler_params=_cparams())(p3, p3, p3, p3, lb, ng)


def _hgrn_bwd_call(p3, dyb, lb, ng, n_heads, col_q, name):
    B, L, _ = p3.shape
    hd = ng.shape[1]
    n_chunks = L // CHUNK
    lp, scratch = _hgrn_scratch(L, hd)
    n_groups = lp // GROUP

    def body(q_ref, fl_ref, v_ref, og_ref, dyb_ref, lb_ref, ng_ref,
             dq_ref, dfl_ref, dv_ref, dog_ref, dlb_ref, dng_ref,
             pad_ref, qin_ref, kin_ref, kout_ref, vp_ref, dec_ref, o_ref, s_ref, a_ref,
             do_ref, ds_ref, dqi_ref, dki_ref, dko_ref, dvv_ref, dct_ref,
             sig_ref, f_ref, ein_ref, einv_ref, eout_ref, da_ref):
        @pl.when(pl.program_id(1) == 0)
        def _():
            dlb_ref[...] = jnp.zeros_like(dlb_ref)

        @pl.when((pl.program_id(0) == 0) & (pl.program_id(1) == 0))
        def _():
            dng_ref[...] = jnp.zeros_like(dng_ref)

        _hgrn_forward_core(q_ref, fl_ref, v_ref, lb_ref, pad_ref, qin_ref, kin_ref, kout_ref, vp_ref, dec_ref,
                           o_ref, s_ref, a_ref, L, keep=(("sig", sig_ref), ("f", f_ref), ("e_in", ein_ref),
                                                  ("e_inv", einv_ref), ("e_out", eout_ref)))

        def out_bwd(r0, n):
            rows = pl.ds(r0, n)
            _, out_vjp = jax.vjp(_hg_out, o_ref[rows, :], og_ref[rows, :].astype(F32), ng_ref[...])
            d_o, d_og, d_ng = out_vjp(dyb_ref[rows, :].astype(F32))
            dog_ref[rows, :] = d_og.astype(ACT)
            dng_ref[...] += d_ng
            do_ref[rows, :] = d_o.astype(MXU)

        _for_row_blocks(L, out_bwd)
        _fill_tail(((do_ref, 0.0),), L)
        mask = _chunk_mask(GROUP)

        def score_grads(g, carry):
            rows = pl.ds(pl.multiple_of(g * GROUP, GROUP), GROUP)
            da_ref[rows, :] = jnp.where(mask, _dot_nt(do_ref[rows, :], vp_ref[rows, :]), 0.0).astype(MXU)
            return carry

        lax.fori_loop(0, n_groups, score_grads, 0, unroll=GROUP_UNROLL)

        def grads_a(g, carry):
            rows = pl.ds(pl.multiple_of(g * GROUP, GROUP), GROUP)
            qi, ki, do, da = qin_ref[rows, :], kin_ref[rows, :], do_ref[rows, :], da_ref[rows, :]
            sstack = s_ref[pl.ds(g * CPG, CPG)].reshape(CPG * hd, hd)
            dqi_ref[rows, :] = _dot(da, ki) + _dot(_expand(do), sstack)
            dki_ref[rows, :] = _dot_tn(da, qi)
            dvv_ref[rows, :] = _dot_tn(a_ref[rows, :], do)
            x = _dot_tn(do, _expand(qi))
            for j in range(CPG):
                ds_ref[g * CPG + j] = x[:, j * hd:(j + 1) * hd]
            return carry

        lax.fori_loop(0, n_groups, grads_a, 0, unroll=GROUP_UNROLL)

        def rec_bwd(k, dst):
            n = n_chunks - 1 - k
            r0 = pl.multiple_of(n * CHUNK, CHUNK)
            x = ds_ref[n]
            ds_ref[n] = dst
            dec = dec_ref[pl.ds(r0, SUBLANES), :][0:1]
            return dst * dec + x

        lax.fori_loop(0, n_chunks, rec_bwd, jnp.zeros((hd, hd), F32))

        def grads_b(g, carry):
            r0 = pl.multiple_of(g * GROUP, GROUP)
            rows = pl.ds(r0, GROUP)
            ds = [ds_ref[g * CPG + j] for j in range(CPG)]
            dscat = jnp.concatenate(ds, axis=1)
            dvv_ref[rows, :] += _dot_nt(_expand(kout_ref[rows, :]), dscat)
            dstack = ds_ref[pl.ds(g * CPG, CPG)].reshape(CPG * hd, hd)
            dko_ref[rows, :] = _dot(_expand(vp_ref[rows, :]), dstack)
            for j in range(CPG):
                dec = dec_ref[pl.ds(r0 + j * CHUNK, SUBLANES), :][0:1]
                ddec = dec * jnp.sum(ds[j] * s_ref[g * CPG + j], axis=0, keepdims=True)
                dct_ref[pl.ds(r0 + j * CHUNK, CHUNK), :] = jnp.broadcast_to(ddec, (CHUNK, hd))
            return carry

        lax.fori_loop(0, n_groups, grads_b, 0, unroll=GROUP_UNROLL)

        def finish(r0, n):
            rows = pl.ds(r0, n)
            sig, f, e_in, e_inv, e_out = [r[rows, :] for r in (sig_ref, f_ref, ein_ref, einv_ref, eout_ref)]
            q, k = q_ref[rows, :].astype(F32), 1.0 - f
            dqi, dki, dko = dqi_ref[rows, :], dki_ref[rows, :], dko_ref[rows, :]
            dq = dqi * e_in
            dk = dki * e_inv + dko * e_out
            dq_ref[rows, :] = dq.astype(ACT)
            dv_ref[rows, :] = dvv_ref[rows, :].astype(ACT)
            t_out = k * e_out * dko
            dc = q * dq - k * e_inv * dki - t_out
            _, dc_later = _chunk_cumsums(dc, pad_ref, n)
            t_incl, t_later = _chunk_cumsums(t_out, pad_ref, n)
            dlogf = dc + dc_later + t_incl + t_later + dct_ref[rows, :]
            df = dlogf / f - dk
            dfl_ref[rows, :] = (df * (1.0 - lb_ref[...]) * sig * (1.0 - sig)).astype(ACT)
            dlb_ref[...] += jnp.sum(df * (1.0 - sig), axis=0, keepdims=True)

        _for_row_blocks(L, finish)

    order = lambda h, b: (b, h)
    W = n_heads * hd
    act_out = jax.ShapeDtypeStruct((B, L, W), ACT)
    blk_out = pl.BlockSpec((None, L, hd), lambda h, b: (b, 0, h))
    return pl.pallas_call(
        body, name=name,
        out_shape=(act_out, act_out, act_out, act_out, jax.ShapeDtypeStruct((1, W), F32),
                   jax.ShapeDtypeStruct((1, hd), F32)),
        grid=(n_heads, B),
        in_specs=_hgrn_specs(L, hd, col_q, n_heads, order) + [
            pl.BlockSpec((None, L, hd), lambda h, b: (b, 0, h)),
            pl.BlockSpec((1, hd), lambda h, b: (0, h)), pl.BlockSpec((1, hd), lambda h, b: (0, 0))],
        out_specs=(blk_out, blk_out, blk_out, blk_out, pl.BlockSpec((1, hd), lambda h, b: (0, h)),
                   pl.BlockSpec((1, hd), lambda h, b: (0, 0))),
        scratch_shapes=scratch + [
            pltpu.VMEM((lp, hd), MXU), pltpu.VMEM((lp // CHUNK, hd, hd), F32)] + [pltpu.VMEM((lp, hd), F32)] * 10 + [
            pltpu.VMEM((lp, GROUP), MXU)],
        compiler_params=_cparams())(p3, p3, p3, p3, dyb, lb, ng)


def _merge_fn(a, bm, ga, gb):
    return jax.nn.sigmoid(ga) * a + jax.nn.sigmoid(gb) * bm


def _merge_call(yb, a, p, h0, whp, wout, g2, col_ga, tm, name):
    T, D = h0.shape

    def body(yb_ref, a_ref, ga_ref, gb_ref, h0_ref, whp_ref, wout_ref, g2_ref, h1_ref, mg_ref, bm_ref, z2_ref):
        bm = _dot(yb_ref[...], whp_ref[...])
        mg = _merge_fn(a_ref[...].astype(F32), bm, ga_ref[...].astype(F32), gb_ref[...].astype(F32))
        h1 = h0_ref[...] + _dot(mg, wout_ref[...])
        h1_ref[...] = h1
        mg_ref[...] = mg.astype(ACT)
        bm_ref[...] = bm.astype(ACT)
        z2_ref[...] = _rms(h1, g2_ref[...]).astype(ACT)

    tile = pl.BlockSpec((tm, D), lambda i: (i, 0))
    full = pl.BlockSpec((D, D), lambda i: (0, 0))
    act = jax.ShapeDtypeStruct((T, D), ACT)
    return pl.pallas_call(
        body, name=name, out_shape=(jax.ShapeDtypeStruct((T, D), F32), act, act, act), grid=(T // tm,),
        in_specs=[tile, tile, pl.BlockSpec((tm, D), lambda i: (i, col_ga)),
                  pl.BlockSpec((tm, D), lambda i: (i, col_ga + 1)), tile, full, full,
                  pl.BlockSpec((1, D), lambda i: (0, 0))],
        out_specs=(tile, tile, tile, tile), compiler_params=_cparams())(yb, a, p, p, h0, whp, wout, g2)


def _merge_bwd_call(dmg, a, bm, p, col_ga, tm, name):
    T, D = dmg.shape

    def body(dmg_ref, a_ref, bm_ref, ga_ref, gb_ref, da_ref, dbm_ref, dga_ref, dgb_ref):
        args = [r[...].astype(F32) for r in (a_ref, bm_ref, ga_ref, gb_ref)]
        _, vjp = jax.vjp(_merge_fn, *args)
        for r, o in zip((da_ref, dbm_ref, dga_ref, dgb_ref), vjp(dmg_ref[...].astype(F32))):
            r[...] = o.astype(ACT)

    tile = pl.BlockSpec((tm, D), lambda i: (i, 0))
    act = jax.ShapeDtypeStruct((T, D), ACT)
    return pl.pallas_call(
        body, name=name, out_shape=(act, act, act, act), grid=(T // tm,),
        in_specs=[tile, tile, tile, pl.BlockSpec((tm, D), lambda i: (i, col_ga)),
                  pl.BlockSpec((tm, D), lambda i: (i, col_ga + 1))],
        out_specs=(tile, tile, tile, tile), compiler_params=_cparams())(dmg, a, bm, p, p)


def _conv_taps(x_ref, halo_ref, ext_ref, edge, tm, before):
    halo = jnp.where(edge, 0.0, halo_ref[...].astype(F32))
    x = x_ref[...].astype(F32)
    if before:
        ext_ref[0:PAD, :] = halo
        ext_ref[PAD:PAD + tm, :] = x
        return [ext_ref[PAD - 2 + k:PAD - 2 + k + tm, :] for k in range(3)]
    ext_ref[0:tm, :] = x
    ext_ref[tm:tm + PAD, :] = halo
    return [ext_ref[k:k + tm, :] for k in range(3)]


def _conv(taps, cw, cb):
    return cb + cw[0:1] * taps[0] + cw[1:2] * taps[1] + cw[2:3] * taps[2]


def _ffn_pair_specs(tm, F, T, n_pairs, order, before):
    hb = tm // PAD
    last = T // PAD - 1

    def halo_row(i):
        return jnp.maximum(i * hb - 1, 0) if before else jnp.minimum((i + 1) * hb, last)

    specs = []
    for off in (0, n_pairs):
        specs.append(pl.BlockSpec((None, tm, F), lambda *g, off=off: (order(*g)[1] + off, order(*g)[0], 0)))
        specs.append(pl.BlockSpec((None, PAD, F), lambda *g, off=off: (order(*g)[1] + off, halo_row(order(*g)[0]), 0)))
    return specs


def _ffn_fwd_call(up, cw, cb, wd, h1, tgt, g3, tm, tps, name):
    S, T, F = up.shape
    n_pairs = S // 2
    D = h1.shape[1]

    def body(ua_ref, ha_ref, ub_ref, hb_ref, cwa_ref, cwb_ref, cba_ref, cbb_ref, wd_ref, h1_ref, tgt_ref, g3_ref,
             ca_ref, cb_ref, dh2_ref, loss_ref, dg3_ref, acc_ref, ext_ref):
        i, j = pl.program_id(0), pl.program_id(1)
        edge = (i % tps) == 0
        ua = _conv(_conv_taps(ua_ref, ha_ref, ext_ref, edge, tm, True), cwa_ref[...], cba_ref[...])
        ub = _conv(_conv_taps(ub_ref, hb_ref, ext_ref, edge, tm, True), cwb_ref[...], cbb_ref[...])
        ca_ref[...] = ua.astype(ACT)
        cb_ref[...] = ub.astype(ACT)
        contrib = _dot(_silu(ua) * ub, wd_ref[...])

        @pl.when(j == 0)
        def _():
            acc_ref[...] = h1_ref[...] + contrib

        @pl.when(j > 0)
        def _():
            acc_ref[...] += contrib

        @pl.when((i == 0) & (j == 0))
        def _():
            loss_ref[...] = jnp.zeros_like(loss_ref)
            dg3_ref[...] = jnp.zeros_like(dg3_ref)

        @pl.when(j == n_pairs - 1)
        def _():
            row = lax.broadcasted_iota(jnp.int32, (tm, 1), 0) + (i % tps) * tm
            valid = row >= N_META
            tgt = tgt_ref[...]

            def loss_fn(h2, g):
                err = _rms(h2, g) - tgt
                return 0.5 * jnp.sum(jnp.where(valid, err * err, 0.0)) / D

            loss, vjp = jax.vjp(loss_fn, acc_ref[...], g3_ref[...])
            dh2, dg3 = vjp(jnp.ones((), F32))
            dh2_ref[...] = dh2
            loss_ref[...] += loss
            dg3_ref[...] += dg3

    order = lambda i, j: (i, j)
    tile = pl.BlockSpec((tm, D), lambda i, j: (i, 0))
    vec = pl.BlockSpec((1, D), lambda i, j: (0, 0))
    return pl.pallas_call(
        body, name=name,
        out_shape=(jax.ShapeDtypeStruct((n_pairs, T, F), ACT), jax.ShapeDtypeStruct((n_pairs, T, F), ACT),
                   jax.ShapeDtypeStruct((T, D), F32), jax.ShapeDtypeStruct((1, LANES), F32),
                   jax.ShapeDtypeStruct((1, D), F32)),
        grid=(T // tm, n_pairs),
        in_specs=_ffn_pair_specs(tm, F, T, n_pairs, order, True) + [
            pl.BlockSpec((None, 3, F), lambda i, j: (j, 0, 0)), pl.BlockSpec((None, 3, F), lambda i, j: (j + n_pairs, 0, 0)),
            pl.BlockSpec((None, 1, F), lambda i, j: (j, 0, 0)), pl.BlockSpec((None, 1, F), lambda i, j: (j + n_pairs, 0, 0)),
            pl.BlockSpec((None, F, D), lambda i, j: (j, 0, 0)), tile, tile, vec],
        out_specs=(pl.BlockSpec((None, tm, F), lambda i, j: (j, i, 0)), pl.BlockSpec((None, tm, F), lambda i, j: (j, i, 0)),
                   tile, pl.BlockSpec((1, LANES), lambda i, j: (0, 0)), vec),
        scratch_shapes=[pltpu.VMEM((tm, D), F32), pltpu.VMEM((tm + PAD, F), F32)],
        compiler_params=_cparams())(up, up, up, up, cw, cw, cb, cb, wd, h1, tgt, g3)


def _ffn_bwd_a_call(dh2, ca, cb, wd, tm, name):
    n_pairs, T, F = ca.shape
    D = dh2.shape[1]

    def body(dh2_ref, ca_ref, cb_ref, wd_ref, dua_ref, dub_ref, dwd_ref, dcba_ref, dcbb_ref):
        @pl.when(pl.program_id(1) == 0)
        def _():
            for r in (dwd_ref, dcba_ref, dcbb_ref):
                r[...] = jnp.zeros_like(r)

        dh2 = dh2_ref[...]
        ua, ub = ca_ref[...].astype(F32), cb_ref[...].astype(F32)
        sa = jax.nn.sigmoid(ua)
        gate = ua * sa
        dact = _dot_nt(dh2, wd_ref[...])
        dwd_ref[...] += _dot_tn(gate * ub, dh2)
        dub = dact * gate
        dua = dact * ub * sa * (1.0 + ua * (1.0 - sa))
        dcba_ref[...] += jnp.sum(dua, axis=0, keepdims=True)
        dcbb_ref[...] += jnp.sum(dub, axis=0, keepdims=True)
        dua_ref[...] = dua.astype(ACT)
        dub_ref[...] = dub.astype(ACT)

    blk = pl.BlockSpec((None, tm, F), lambda j, i: (j, i, 0))
    vec = pl.BlockSpec((None, 1, F), lambda j, i: (j, 0, 0))
    return pl.pallas_call(
        body, name=name,
        out_shape=(jax.ShapeDtypeStruct((n_pairs, T, F), ACT), jax.ShapeDtypeStruct((n_pairs, T, F), ACT),
                   jax.ShapeDtypeStruct((n_pairs, F, D), F32), jax.ShapeDtypeStruct((n_pairs, 1, F), F32),
                   jax.ShapeDtypeStruct((n_pairs, 1, F), F32)),
        grid=(n_pairs, T // tm),
        in_specs=[pl.BlockSpec((tm, D), lambda j, i: (i, 0)), blk, blk, pl.BlockSpec((None, F, D), lambda j, i: (j, 0, 0))],
        out_specs=(blk, blk, pl.BlockSpec((None, F, D), lambda j, i: (j, 0, 0)), vec, vec),
        compiler_params=_cparams())(dh2, ca, cb, wd)


def _ffn_bwd_b_call(dua, dub, up, cw, wup, h1, g2, dh2, tm, tps, name):
    n_pairs, T, F = dua.shape
    D = h1.shape[1]
    hb = tm // PAD
    last = T // PAD - 1

    def body(da_ref, na_ref, db_ref, nb_ref, ua_ref, ub_ref, cwa_ref, cwb_ref, wa_ref, wb_ref, h1_ref, g2_ref, dh2_ref,
             dupa_ref, dupb_ref, dh1_ref, dg2_ref, dcwa_ref, dcwb_ref, acc_ref, ext_ref):
        i, j = pl.program_id(0), pl.program_id(1)
        edge = (i % tps) == tps - 1

        @pl.when((i == 0) & (j == 0))
        def _():
            dcwa_ref[...] = jnp.zeros_like(dcwa_ref)
            dcwb_ref[...] = jnp.zeros_like(dcwb_ref)

        outs = []
        for d_ref, n_ref, u_ref, cw_ref, o_ref, dcw_ref in (
                (da_ref, na_ref, ua_ref, cwa_ref, dupa_ref, dcwa_ref),
                (db_ref, nb_ref, ub_ref, cwb_ref, dupb_ref, dcwb_ref)):
            t = _conv_taps(d_ref, n_ref, ext_ref, edge, tm, False)
            cwv = cw_ref[...]
            dup = cwv[2:3] * t[0] + cwv[1:2] * t[1] + cwv[0:1] * t[2]
            o_ref[...] = dup.astype(ACT)
            outs.append(dup)
            u = u_ref[...].astype(F32)
            dcw_ref[j] += jnp.concatenate([jnp.sum(u * t[2 - k], axis=0, keepdims=True) for k in range(3)], axis=0)
        contrib = _dot_nt(outs[0], wa_ref[...]) + _dot_nt(outs[1], wb_ref[...])

        @pl.when(j == 0)
        def _():
            acc_ref[...] = contrib

        @pl.when(j > 0)
        def _():
            acc_ref[...] += contrib

        @pl.when((i == 0) & (j == 0))
        def _():
            dg2_ref[...] = jnp.zeros_like(dg2_ref)

        @pl.when(j == n_pairs - 1)
        def _():
            _, vjp = jax.vjp(_rms, h1_ref[...], g2_ref[...])
            dh, dg = vjp(acc_ref[...])
            dh1_ref[...] = dh2_ref[...] + dh
            dg2_ref[...] += dg

    tile = pl.BlockSpec((tm, D), lambda i, j: (i, 0))
    vec = pl.BlockSpec((1, D), lambda i, j: (0, 0))
    pair = lambda: [pl.BlockSpec((None, tm, F), lambda i, j: (j, i, 0)),
                    pl.BlockSpec((None, PAD, F), lambda i, j: (j, jnp.minimum((i + 1) * hb, last), 0))]
    act = jax.ShapeDtypeStruct((n_pairs, T, F), ACT)
    dcw = jax.ShapeDtypeStruct((n_pairs, 3, F), F32)
    dcw_spec = pl.BlockSpec((n_pairs, 3, F), lambda i, j: (0, 0, 0))
    return pl.pallas_call(
        body, name=name,
        out_shape=(act, act, jax.ShapeDtypeStruct((T, D), F32), jax.ShapeDtypeStruct((1, D), F32), dcw, dcw),
        grid=(T // tm, n_pairs),
        in_specs=pair() + pair() + [
            pl.BlockSpec((None, tm, F), lambda i, j: (j, i, 0)), pl.BlockSpec((None, tm, F), lambda i, j: (j + n_pairs, i, 0)),
            pl.BlockSpec((None, 3, F), lambda i, j: (j, 0, 0)), pl.BlockSpec((None, 3, F), lambda i, j: (j + n_pairs, 0, 0)),
            pl.BlockSpec((None, D, F), lambda i, j: (j, 0, 0)), pl.BlockSpec((None, D, F), lambda i, j: (j + n_pairs, 0, 0)),
            tile, vec, tile],
        out_specs=(pl.BlockSpec((None, tm, F), lambda i, j: (j, i, 0)), pl.BlockSpec((None, tm, F), lambda i, j: (j, i, 0)),
                   tile, vec, dcw_spec, dcw_spec),
        scratch_shapes=[pltpu.VMEM((tm, D), F32), pltpu.VMEM((tm + PAD, F), F32)],
        compiler_params=_cparams())(dua, dua, dub, dub, up, up, cw, cw, wup, wup, h1, g2, dh2)


def _in_bwd_call(dp, w_in, h0, g1, dh1, tm, name):
    T, D = h0.shape
    S, _, N = w_in.shape

    def body(dp_ref, w_ref, h0_ref, g1_ref, dh1_ref, dh0_ref, dg1_ref, acc_ref):
        i, j = pl.program_id(0), pl.program_id(1)
        contrib = _dot_nt(dp_ref[...], w_ref[...])

        @pl.when(j == 0)
        def _():
            acc_ref[...] = contrib

        @pl.when(j > 0)
        def _():
            acc_ref[...] += contrib

        @pl.when((i == 0) & (j == 0))
        def _():
            dg1_ref[...] = jnp.zeros_like(dg1_ref)

        @pl.when(j == S - 1)
        def _():
            _, vjp = jax.vjp(_rms, h0_ref[...], g1_ref[...])
            dh, dg = vjp(acc_ref[...])
            dh0_ref[...] = dh1_ref[...] + dh
            dg1_ref[...] += dg

    tile = pl.BlockSpec((tm, D), lambda i, j: (i, 0))
    vec = pl.BlockSpec((1, D), lambda i, j: (0, 0))
    return pl.pallas_call(
        body, name=name, out_shape=(jax.ShapeDtypeStruct((T, D), F32), jax.ShapeDtypeStruct((1, D), F32)),
        grid=(T // tm, S),
        in_specs=[pl.BlockSpec((tm, N), lambda i, j: (i, j)), pl.BlockSpec((None, D, N), lambda i, j: (j, 0, 0)),
                  tile, vec, tile],
        out_specs=(tile, vec), scratch_shapes=[pltpu.VMEM((tm, D), F32)],
        compiler_params=_cparams())(dp, w_in, h0, g1, dh1)


def _meta_grad_call(dh0_3, name):
    B, L, D = dh0_3.shape

    def body(d_ref, o_ref):
        o_ref[...] = jnp.sum(d_ref[...], axis=0)

    return pl.pallas_call(
        body, name=name, out_shape=jax.ShapeDtypeStruct((N_META, D), F32), grid=(1,),
        in_specs=[pl.BlockSpec((B, N_META, D), lambda i: (0, 0, 0))],
        out_specs=pl.BlockSpec((N_META, D), lambda i: (0, 0)), compiler_params=_cparams())(dh0_3)


_RELS = [(dx, dy, dc) for dx in (0, 1) for dy in (0, 1) for dc in (0, 1)][1:]


def _exchange_call(arrs, scatter, name):
    n = len(arrs)
    n_rel = len(_RELS)

    def body(*refs):
        ins, outs = refs[:n], refs[n:2 * n]
        send_sems, recv_sems, loc_sems = refs[2 * n:]
        x, y, c = lax.axis_index("x"), lax.axis_index("y"), lax.axis_index("c")
        me = 4 * x + 2 * y + c
        started = []
        for k in range(n):
            src_me = ins[k].at[me] if scatter else ins[k]
            loc = pltpu.make_async_copy(src_me, outs[k].at[me], loc_sems.at[k])
            loc.start()
            started.append(loc)
        waits = []
        for r, (dx, dy, dc) in enumerate(_RELS):
            px, py, pc = (x + dx) % 2, (y + dy) % 2, (c + dc) % 2
            pid = 4 * px + 2 * py + pc
            for k in range(n):
                s = k * n_rel + r
                src = ins[k].at[pid] if scatter else ins[k]
                cp = pltpu.make_async_remote_copy(
                    src_ref=src, dst_ref=outs[k].at[me], send_sem=send_sems.at[s], recv_sem=recv_sems.at[s],
                    device_id=(px, py, pc), device_id_type=pl.DeviceIdType.MESH)
                cp.start()
                waits.append(pltpu.make_async_remote_copy(
                    src_ref=src, dst_ref=outs[k].at[pid], send_sem=send_sems.at[s], recv_sem=recv_sems.at[s],
                    device_id=(px, py, pc), device_id_type=pl.DeviceIdType.MESH))
        for w in waits:
            w.wait_send()
            w.wait_recv()
        for loc in started:
            loc.wait()

    out_shape = tuple(jax.ShapeDtypeStruct(a.shape if scatter else (N_DEV,) + a.shape, a.dtype) for a in arrs)
    hbm = pl.BlockSpec(memory_space=pl.ANY)
    return pl.pallas_call(
        body, name=name, out_shape=out_shape, in_specs=[hbm] * n, out_specs=tuple([hbm] * n),
        scratch_shapes=[pltpu.SemaphoreType.DMA((n * n_rel,)), pltpu.SemaphoreType.DMA((n * n_rel,)),
                        pltpu.SemaphoreType.DMA((n,))],
        compiler_params=pltpu.CompilerParams(has_side_effects=True))(*arrs)


_HBM = pl.BlockSpec(memory_space=pltpu.HBM)
_SEM = pl.BlockSpec(memory_space=pltpu.SEMAPHORE)
_DATAFLOW = pltpu.SideEffectType.DATAFLOW_SIDE_EFFECTING


def _peer_copies(ins, lands, send_sems, recv_sems, scatter):
    n = len(ins)
    x, y, c = lax.axis_index("x"), lax.axis_index("y"), lax.axis_index("c")
    me = 4 * x + 2 * y + c
    sends, arrivals = [], []
    for r, (dx, dy, dc) in enumerate(_RELS):
        px, py, pc = (x + dx) % 2, (y + dy) % 2, (c + dc) % 2
        pid = 4 * px + 2 * py + pc
        for k in range(n):
            s = k * len(_RELS) + r
            src = ins[k].at[pid] if scatter else ins[k]
            for dst, out in ((lands[k].at[me], sends), (lands[k].at[pid], arrivals)):
                out.append(pltpu.make_async_remote_copy(
                    src_ref=src, dst_ref=dst, send_sem=send_sems.at[s], recv_sem=recv_sems.at[s],
                    device_id=(px, py, pc), device_id_type=pl.DeviceIdType.MESH))
    return sends, arrivals


def _exchange_start(arrs, scatter, name):
    n = len(arrs)
    n_sem = n * len(_RELS)

    def body(*refs):
        ins, lands = refs[:n], refs[n:2 * n]
        send_sems, recv_sems = refs[2 * n], refs[2 * n + 1]
        token = refs[-1]
        sends, _ = _peer_copies(ins, lands, send_sems, recv_sems, scatter)
        for cp in sends:
            cp.start()
        token[...] = jnp.zeros_like(token)

    land_shapes = [a.shape if scatter else (N_DEV,) + a.shape for a in arrs]
    ops = [pltpu.with_memory_space_constraint(a, pltpu.HBM) for a in arrs]
    ops += [pltpu.with_memory_space_constraint(lax.empty(s, a.dtype), pltpu.HBM) for s, a in zip(land_shapes, arrs)]
    out = pl.pallas_call(
        body, name=name,
        out_shape=(pltpu.SemaphoreType.DMA((n_sem,)), pltpu.SemaphoreType.DMA((n_sem,)),
                   *[pltpu.HBM(a.shape, a.dtype) for a in arrs],
                   *[pltpu.HBM(s, a.dtype) for s, a in zip(land_shapes, arrs)],
                   jax.ShapeDtypeStruct((SUBLANES, LANES), F32)),
        in_specs=[_HBM] * (2 * n),
        out_specs=(_SEM, _SEM, *[_HBM] * (2 * n), pl.BlockSpec(memory_space=pltpu.VMEM)),
        input_output_aliases={i: 2 + i for i in range(2 * n)},
        compiler_params=pltpu.CompilerParams(has_side_effects=_DATAFLOW))(*ops)
    return out[0], out[1], list(out[2:2 + n]), list(out[2 + n:2 + 2 * n]), out[-1]


def _exchange_wait(started, after, scatter, name):
    send_sems, recv_sems, srcs, lands, _ = started
    n = len(srcs)

    def body(*refs):
        ins, lands_ = refs[:n], refs[n:2 * n]
        _, arrivals = _peer_copies(ins, lands_, refs[2 * n], refs[2 * n + 1], scatter)
        for cp in arrivals:
            cp.wait_send()
            cp.wait_recv()

    out = pl.pallas_call(
        body, name=name,
        out_shape=(*[pltpu.HBM(a.shape, a.dtype) for a in srcs], *[pltpu.HBM(a.shape, a.dtype) for a in lands]),
        in_specs=[_HBM] * (2 * n) + [_SEM, _SEM, pl.BlockSpec(memory_space=pl.ANY)],
        out_specs=tuple([_HBM] * (2 * n)), input_output_aliases={i: i for i in range(2 * n)},
        compiler_params=pltpu.CompilerParams(has_side_effects=_DATAFLOW))(*srcs, *lands, send_sems, recv_sems, after)
    return list(out[:n]), list(out[n:])


def _place_own_call(srcs, lands, scatter, me, name):
    outs = []
    for k, (src, land) in enumerate(zip(srcs, lands)):
        R, C = land.shape[1:]
        tr = R
        while tr % 32 == 0 and tr * C * land.dtype.itemsize > 2 * 1024 * 1024:
            tr //= 2

        def body(me_ref, s_ref, l_ref, o_ref):
            o_ref[...] = s_ref[...]

        src_spec = (pl.BlockSpec((None, tr, C), lambda i, me_ref: (me_ref[0], i, 0)) if scatter
                    else pl.BlockSpec((tr, C), lambda i, me_ref: (i, 0)))
        outs.append(pl.pallas_call(
            body, name=f"{name}_{k}", out_shape=jax.ShapeDtypeStruct(land.shape, land.dtype),
            grid_spec=pltpu.PrefetchScalarGridSpec(
                num_scalar_prefetch=1, grid=(R // tr,),
                in_specs=[src_spec, pl.BlockSpec(memory_space=pl.ANY)],
                out_specs=pl.BlockSpec((None, tr, C), lambda i, me_ref: (me_ref[0], i, 0))),
            input_output_aliases={2: 0}, compiler_params=_cparams())(me, src, land))
    return outs


def _adamw_shard_call(w, parts, m, v, name):
    R, C = w.shape
    tr = _tile(R, 128) if R % 16 == 0 else R

    def body(w_ref, p_ref, m_ref, v_ref, g_ref, d_ref, nm_ref, nv_ref):
        g = p_ref[0].astype(F32)
        for s in range(1, N_DEV):
            g = g + p_ref[s].astype(F32)
        d, nm, nv = _adamw(w_ref[...], g, m_ref[...], v_ref[...])
        g_ref[...] = g
        d_ref[...] = d
        nm_ref[...] = nm
        nv_ref[...] = nv

    tile = pl.BlockSpec((tr, C), lambda i: (i, 0))
    sh = jax.ShapeDtypeStruct((R, C), F32)
    return pl.pallas_call(
        body, name=name, out_shape=(sh, sh, sh, sh), grid=(R // tr,),
        in_specs=[tile, pl.BlockSpec((N_DEV, tr, C), lambda i: (0, i, 0)), tile, tile],
        out_specs=(tile, tile, tile, tile), compiler_params=_cparams())(w, parts, m, v)


def _pack(arrs, rows_mult=SUBLANES):
    flat = jnp.concatenate([a.reshape(-1).astype(F32) for a in arrs])
    n = flat.shape[0]
    per = rows_mult * LANES
    total = -(-n // per) * per
    return jnp.pad(flat, (0, total - n)).reshape(total // LANES, LANES)


def _unpack(pack, shapes):
    flat = pack.reshape(-1)
    out, off = [], 0
    for s in shapes:
        n = 1
        for d in s:
            n *= d
        out.append(flat[off:off + n].reshape(s))
        off += n
    return out


def kernel(x, meta_tokens, mix_norm_g, w_in, ssm_lambda_re, ssm_lambda_im, ssm_log_dt, ssm_b_re, ssm_b_im, ssm_c_re, ssm_c_im, ssm_d, ssm_w_glu, w_ssm_proj, hgrn_lb_logits, hgrn_norm_g, w_hgrn_proj, w_out, ffn_norm_g, w_up, conv_w, conv_b, w_down, final_norm_g, loss_target, m_meta_tokens, m_mix_norm_g, m_w_in, m_ssm_lambda_re, m_ssm_lambda_im, m_ssm_log_dt, m_ssm_b_re, m_ssm_b_im, m_ssm_c_re, m_ssm_c_im, m_ssm_d, m_ssm_w_glu, m_w_ssm_proj, m_hgrn_lb_logits, m_hgrn_norm_g, m_w_hgrn_proj, m_w_out, m_ffn_norm_g, m_w_up, m_conv_w, m_conv_b, m_w_down, m_final_norm_g, v_meta_tokens, v_mix_norm_g, v_w_in, v_ssm_lambda_re, v_ssm_lambda_im, v_ssm_log_dt, v_ssm_b_re, v_ssm_b_im, v_ssm_c_re, v_ssm_c_im, v_ssm_d, v_ssm_w_glu, v_w_ssm_proj, v_hgrn_lb_logits, v_hgrn_norm_g, v_w_hgrn_proj, v_w_out, v_ffn_norm_g, v_w_up, v_conv_w, v_conv_b, v_w_down, v_final_norm_g):
    args = dict(locals())
    B, S_len, D = x.shape
    L = S_len + N_META
    T = B * L
    tm = _tile(L, ROW_TILE_CAP)
    tps = L // tm
    G, P = ssm_lambda_re.shape[1:]
    H = ssm_b_re.shape[-1]
    W = G * H
    n_cb = W // LANES
    gpb = G // n_cb
    hd = hgrn_norm_g.shape[1]
    n_heads = D // hd
    n_in = w_in.shape[2]
    F = w_up.shape[2]
    assert W == D and n_in % LANES == 0

    me = (4 * lax.axis_index("x") + 2 * lax.axis_index("y") + lax.axis_index("c")).astype(jnp.int32).reshape(1)
    meta_g, cw_g = _exchange_call([meta_tokens, conv_w[0]], False, "gather_small_params")
    ga = _exchange_start([w_in[0].astype(MXU)], False, "gather_a_start")
    gb = _exchange_start(
        [w_up[0].astype(MXU), ssm_w_glu[0].astype(MXU), w_ssm_proj[0].astype(MXU), w_hgrn_proj[0].astype(MXU),
         w_out[0].astype(MXU), w_down[0].astype(MXU)], False, "gather_b_start")
    started_tok = (ga[4] + gb[4])[0:1, 0:1]
    meta_full = meta_g.transpose(1, 0, 2).reshape(N_META, D)
    cb_g = conv_b.reshape(N_DEV, 1, F)

    h0 = jnp.concatenate([jnp.broadcast_to(meta_full[None], (B, N_META, D)), x], axis=1).reshape(T, D)
    tgt = jnp.concatenate([jnp.zeros((B, N_META, D), F32), loss_target], axis=1).reshape(T, D)

    lr, li = ssm_lambda_re[0], ssm_lambda_im[0]
    ldt = ssm_log_dt[0].reshape(G, 1)
    bt_re = ssm_b_re[0].transpose(2, 0, 1).reshape(H, G * P)
    bt_im = ssm_b_im[0].transpose(2, 0, 1).reshape(H, G * P)
    seg = _seg_len(L)
    a_re, a_im, as_re, as_im, coef_re, coef_im = _small_call(
        _disc_a_power(seg), [lr, li, ldt], [((G, P), F32)] * 6, "s5_discretise")
    bbt_re, bbt_im = _small_call(
        _disc_b, [coef_re.reshape(1, G * P), coef_im.reshape(1, G * P), bt_re, bt_im],
        [((H, G * P), F32)] * 2, "s5_input_matrix")
    eye = jnp.eye(gpb, dtype=F32)
    hw = gpb * P

    def expand_b(bbt):
        t = bbt.reshape(H, n_cb, gpb, P).transpose(1, 0, 2, 3)[:, None]
        return (t * eye[None, :, None, :, None]).reshape(n_cb, gpb * H, hw)

    def expand_c(cm):
        t = cm.reshape(n_cb, gpb, H, P).transpose(0, 1, 3, 2)[:, :, :, None]
        return (t * eye[None, :, None, :, None]).reshape(n_cb, hw, gpb * H)

    wb = jnp.concatenate([expand_b(bbt_re), expand_b(bbt_im)], axis=2).astype(MXU)
    wc = jnp.concatenate([expand_c(ssm_c_re[0]), -expand_c(ssm_c_im[0])], axis=1).astype(MXU)
    tab = jnp.stack([jnp.concatenate([a_re.reshape(n_cb, hw), a_im.reshape(n_cb, hw)], axis=1),
                     jnp.concatenate([as_re.reshape(n_cb, hw), as_im.reshape(n_cb, hw)], axis=1)], axis=1)
    tab = jnp.broadcast_to(tab[:, :, None, :], (n_cb, 2, SUBLANES, 2 * hw))
    dsk = ssm_d.reshape(n_cb, 1, LANES)
    lb = _small_call(_lb_fn, [hgrn_lb_logits], [((1, D), F32)], "hgrn_lower_bound")[0]

    z1 = _norm_call(h0, mix_norm_g + started_tok, tm, "mix_norm")
    ga_src, ga_land = _exchange_wait(ga, z1, False, "gather_a_wait")
    win_g = _place_own_call(ga_src, ga_land, False, me, "gather_a_own")[0]
    p = _mm_shard(z1, win_g, tm, "in_proj", False)
    p3 = p.reshape(B, L, p.shape[1])
    u_seg = _to_segments(p3[:, :, :W], seg)
    ya_seg, s_all = _s5_fwd_call(u_seg, wb, wc, tab, dsk, "s5_fwd")
    ya = _from_segments(ya_seg, seg, L).reshape(T, W)
    gb_src, gb_land = _exchange_wait(gb, ya, False, "gather_b_wait")
    gathered = _place_own_call(gb_src, gb_land, False, me, "gather_b_own")
    wup_g = gathered[0]
    wglu_g, wsp_g, whp_g, wout_g = [g.reshape(D, D) for g in gathered[1:5]]
    wdn_g = gathered[5].reshape(N_DEV // 2, 2 * w_down.shape[1], D)
    yo, a_br = _glu_proj_call(ya, wglu_g, wsp_g, tm, "s5_glu_proj")
    yb = _hgrn_fwd_call(p3, lb, hgrn_norm_g, n_heads, n_cb, "hgrn_fwd").reshape(T, D)
    col_ga = 5
    h1, mg, bm, z2 = _merge_call(yb, a_br, p, h0, whp_g, wout_g, ffn_norm_g, col_ga, tm, "merge")
    up = _mm_shard(z2, wup_g, tm, "up_proj", True)
    conv_a, conv_b_out, dh2, loss_part, dg3 = _ffn_fwd_call(up, cw_g, cb_g, wdn_g, h1, tgt, final_norm_g.reshape(1, D),
                                                            tm, tps, "ffn_out_loss")

    dua, dub, dwd, dcba, dcbb = _ffn_bwd_a_call(dh2, conv_a, conv_b_out, wdn_g, tm, "ffn_bwd_gate")
    dupa, dupb, dh1, dg2, dcwa, dcwb = _ffn_bwd_b_call(dua, dub, up, cw_g, wup_g, h1, ffn_norm_g, dh2, tm, tps,
                                                       "ffn_bwd_up")
    dwup = jnp.concatenate([_mm_tn(z2, dupa, N_DEV // 2, tm, "dw_up_a", True),
                            _mm_tn(z2, dupb, N_DEV // 2, tm, "dw_up_b", True)], axis=0)
    sh_rows = D // N_DEV
    sa = _exchange_start([dwup.astype(WIRE), dwd.reshape(N_DEV, w_down.shape[1], D).astype(WIRE)], True,
                         "scatter_a_start")
    dmg, dwout = _lin_bwd(mg, dh1, wout_g + sa[4][0:1, 0:1].astype(MXU), tm, "out_proj_bwd")
    da_br, dbm, dga, dgb = _merge_bwd_call(dmg, a_br, bm, p, col_ga, tm, "merge_bwd")
    dyo, dwsp = _lin_bwd(yo, da_br, wsp_g, tm, "ssm_proj_bwd")
    dyb, dwhp = _lin_bwd(yb, dbm, whp_g, tm, "hgrn_proj_bwd")
    dya, dwglu = _glu_bwd_call(ya, dyo, wglu_g, tm, "s5_glu_bwd")
    sb = _exchange_start([t.reshape(N_DEV, sh_rows, D).astype(WIRE) for t in (dwglu, dwsp, dwhp, dwout)], True,
                         "scatter_b_start")
    tok_b = sb[4][0:1, :]
    du_seg, dwb, dwc, dab, ddsk = _s5_bwd_call(u_seg, s_all, _to_segments(dya.reshape(B, L, W), seg), wb, wc, tab,
                                               dsk + tok_b[None], "s5_bwd")
    du = _from_segments(du_seg, seg, L)

    def diag_b(dw):
        t = (dw.reshape(n_cb, gpb, H, gpb, P) * eye[None, :, None, :, None]).sum(axis=1)
        return t.transpose(1, 0, 2, 3).reshape(H, G * P)

    def diag_c(dw):
        t = (dw.reshape(n_cb, gpb, P, gpb, H) * eye[None, :, None, :, None]).sum(axis=3)
        return t.transpose(0, 1, 3, 2).reshape(G, H, P)

    early_parts = [dab[:, 0, :hw].reshape(G, P), dab[:, 0, hw:].reshape(G, P),
                   diag_b(dwb[:, :, :hw]), diag_b(dwb[:, :, hw:]),
                   diag_c(dwc[:, :hw]), -diag_c(dwc[:, hw:]), ddsk.reshape(1, D)]
    early_pack = _pack(early_parts)
    se = _exchange_start([early_pack], False, "gather_s5_grads_start")
    dq, dfl, di, dog, dlb, dng = _hgrn_bwd_call(p3, dyb.reshape(B, L, D), lb, hgrn_norm_g + tok_b + se[4][0:1, :],
                                                n_heads, n_cb, "hgrn_bwd")
    dp = jnp.concatenate([du.reshape(T, W), dq.reshape(T, D), dfl.reshape(T, D), di.reshape(T, D),
                          dog.reshape(T, D), dga, dgb], axis=1)
    dwin = _mm_tn(z1, dp, N_DEV, tm, "dw_in", False)
    sc = _exchange_start([dwin.astype(WIRE)], True, "scatter_c_start")
    dh0, dg1 = _in_bwd_call(dp, win_g, h0, mix_norm_g + sc[4][0:1, 0:1], dh1, tm, "in_proj_bwd")
    dh0_3 = dh0.reshape(B, L, D)
    grad_x = dh0_3[:, N_META:]
    dmeta = _meta_grad_call(dh0_3, "meta_grad")

    late_parts = [dg1, dlb, dng, dg2, jnp.concatenate([dcba, dcbb], axis=0).reshape(1, N_DEV * F), dg3, loss_part]
    late_pack = _pack(late_parts)

    dcw = jnp.concatenate([dcwa, dcwb], axis=0)
    dmeta_s = dmeta.reshape(N_META, N_DEV, D // N_DEV).transpose(1, 0, 2)
    parts_d = _exchange_call([dmeta_s, dcw], True, "scatter_small_grads")
    late_all = _exchange_call([late_pack], False, "gather_small_grads")[0]
    early_all = _place_own_call(*_exchange_wait(se, late_all, False, "gather_s5_grads_wait"), False, me,
                                "gather_s5_grads_own")[0]
    parts_a = _place_own_call(*_exchange_wait(sa, late_all, True, "scatter_a_wait"), True, me, "scatter_a_own")
    parts_b = _place_own_call(*_exchange_wait(sb, late_all, True, "scatter_b_wait"), True, me, "scatter_b_own")
    parts_c = _place_own_call(*_exchange_wait(sc, late_all, True, "scatter_c_wait"), True, me, "scatter_c_own")
    parts = [parts_c[0], parts_a[0], *parts_b, parts_a[1], parts_d[0], parts_d[1]]

    def sum8(a, b):
        ta, tb = a[0], b[0]
        for s in range(1, N_DEV):
            ta, tb = ta + a[s], tb + b[s]
        return ta, tb

    early_sum, late_sum = _small_call(sum8, [early_all, late_all], [(early_pack.shape, F32), (late_pack.shape, F32)],
                                      "sum_small_grads")
    t_abr, t_abi, t_bbr, t_bbi, g_cre, g_cim, g_dsk = _unpack(early_sum, [a.shape for a in early_parts])
    g_g1, t_lb, g_ng, g_g2, g_cb, g_g3, loss_v = _unpack(late_sum, [a.shape for a in late_parts])

    def disc_b_bwd(cr, ci, br, bi, dbr, dbi):
        _, vjp = jax.vjp(_disc_b, cr, ci, br, bi)
        return vjp((dbr, dbi))

    t_cr, t_ci, g_btr, g_bti = _small_call(
        disc_b_bwd, [coef_re.reshape(1, G * P), coef_im.reshape(1, G * P), bt_re, bt_im, t_bbr, t_bbi],
        [((1, G * P), F32)] * 2 + [((H, G * P), F32)] * 2, "s5_input_matrix_bwd")

    def disc_a_bwd(lr_, li_, ldt_, dar, dai, dcr, dci):
        _, vjp = jax.vjp(_disc_a, lr_, li_, ldt_)
        return vjp((dar, dai, dcr, dci))

    g_lr, g_li, g_ldt = _small_call(
        disc_a_bwd, [lr, li, ldt, t_abr, t_abi, t_cr.reshape(G, P), t_ci.reshape(G, P)],
        [((G, P), F32)] * 2 + [((G, 1), F32)], "s5_discretise_bwd")

    def lb_bwd(logits, d):
        _, vjp = jax.vjp(_lb_fn, logits)
        return vjp(d)

    g_lbl = _small_call(lb_bwd, [hgrn_lb_logits, t_lb], [(hgrn_lb_logits.shape, F32)], "hgrn_lower_bound_bwd")[0]

    grads = dict(
        mix_norm_g=g_g1, ssm_lambda_re=g_lr[None], ssm_lambda_im=g_li[None], ssm_log_dt=g_ldt.reshape(1, G),
        ssm_b_re=g_btr.reshape(H, G, P).transpose(1, 2, 0)[None], ssm_b_im=g_bti.reshape(H, G, P).transpose(1, 2, 0)[None],
        ssm_c_re=g_cre[None], ssm_c_im=g_cim[None], ssm_d=g_dsk, hgrn_lb_logits=g_lbl, hgrn_norm_g=g_ng,
        ffn_norm_g=g_g2, conv_b=g_cb.reshape(1, N_DEV * F), final_norm_g=g_g3.reshape(D))
    loss = loss_v[0, 0]

    delta, new_m, new_v = {}, {}, {}
    sharded = [("w_in", parts[0], (D, n_in)), ("w_up", parts[1], (D, F)), ("ssm_w_glu", parts[2], (sh_rows, D)),
               ("w_ssm_proj", parts[3], (sh_rows, D)), ("w_hgrn_proj", parts[4], (sh_rows, D)),
               ("w_out", parts[5], (sh_rows, D)), ("w_down", parts[6], (w_down.shape[1], D)),
               ("meta_tokens", parts[7], (N_META, D // N_DEV)), ("conv_w", parts[8], (3, F))]
    for name, part, shp in sharded:
        full = args[name].shape
        g, d_, nm, nv = _adamw_shard_call(args[name].reshape(shp), part, args["m_" + name].reshape(shp),
                                          args["v_" + name].reshape(shp), "adamw_" + name)
        grads[name], delta[name], new_m[name], new_v[name] = [t.reshape(full) for t in (g, d_, nm, nv)]

    rep = ["mix_norm_g", "ssm_lambda_re", "ssm_lambda_im", "ssm_log_dt", "ssm_b_re", "ssm_b_im", "ssm_c_re",
           "ssm_c_im", "ssm_d", "hgrn_lb_logits", "hgrn_norm_g", "ffn_norm_g", "conv_b", "final_norm_g"]
    rep_shapes = [args[n].shape for n in rep]
    packs = [_pack([args[pre + n] for n in rep]) for pre in ("", "m_", "v_")]
    g_pack = _pack([grads[n] for n in rep])
    outs = _small_call(lambda w, g, m, v: _adamw(w, g, m, v), [packs[0], g_pack, packs[1], packs[2]],
                       [(g_pack.shape, F32)] * 3, "adamw_replicated")
    for n, d_, nm, nv in zip(rep, *[_unpack(o, rep_shapes) for o in outs]):
        delta[n], new_m[n], new_v[n] = d_, nm, nv

    names = ["meta_tokens", "mix_norm_g", "w_in", "ssm_lambda_re", "ssm_lambda_im", "ssm_log_dt", "ssm_b_re",
             "ssm_b_im", "ssm_c_re", "ssm_c_im", "ssm_d", "ssm_w_glu", "w_ssm_proj", "hgrn_lb_logits", "hgrn_norm_g",
             "w_hgrn_proj", "w_out", "ffn_norm_g", "w_up", "conv_w", "conv_b", "w_down", "final_norm_g"]
    return (loss, grad_x, *[grads[n] for n in names], *[delta[n] for n in names],
            *[new_m[n] for n in names], *[new_v[n] for n in names])
```

```python
import functools

import jax
import jax.numpy as jnp
from jax import lax
from jax.experimental import pallas as pl
from jax.experimental.pallas import tpu as pltpu

F32 = jnp.float32
MXU = jnp.bfloat16
ACT = jnp.bfloat16
WIRE = jnp.bfloat16
N_DEV = 8
N_META = 16
CHUNK = 16
EPS = 1e-6
ADAM_LR, ADAM_B1, ADAM_B2, ADAM_EPS, ADAM_WD, ADAM_STEP = 0.001, 0.9, 0.999, 1e-08, 0.01, 10
SUBLANES = 8
LANES = 128
ROW_TILE_CAP = 700
VMEM_LIMIT = 60 * 1024 * 1024


def _cparams(**kw):
    return pltpu.CompilerParams(vmem_limit_bytes=VMEM_LIMIT, **kw)


def _tile(n, cap):
    best = None
    for t in range(16, min(n, cap) + 1, 16):
        if n % t == 0:
            best = t
    assert best is not None, (n, cap)
    return best


def _dot(a, b):
    return lax.dot_general(a.astype(MXU), b.astype(MXU), (((1,), (0,)), ((), ())), preferred_element_type=F32)


def _dot_nt(a, b):
    return lax.dot_general(a.astype(MXU), b.astype(MXU), (((1,), (1,)), ((), ())), preferred_element_type=F32)


def _dot_tn(a, b):
    return lax.dot_general(a.astype(MXU), b.astype(MXU), (((0,), (0,)), ((), ())), preferred_element_type=F32)


def _rms(x, g):
    return x * lax.rsqrt(jnp.mean(x * x, axis=-1, keepdims=True) + EPS) * g


def _silu(x):
    return x * jax.nn.sigmoid(x)


def _small_call(fn, ins, out_shapes, name):
    n_in = len(ins)

    def body(*refs):
        outs = fn(*[r[...] for r in refs[:n_in]])
        outs = outs if isinstance(outs, (tuple, list)) else (outs,)
        for r, o in zip(refs[n_in:], outs):
            r[...] = o.astype(r.dtype)

    vm = pl.BlockSpec(memory_space=pltpu.VMEM)
    return pl.pallas_call(
        body, name=name, out_shape=tuple(jax.ShapeDtypeStruct(s, d) for s, d in out_shapes),
        in_specs=[vm] * n_in, out_specs=tuple([vm] * len(out_shapes)), compiler_params=_cparams())(*ins)


def _disc_a(lr, li, ldt):
    dt = jnp.exp(ldt)
    mag = jnp.exp(lr * dt)
    ab_re = mag * jnp.cos(li * dt)
    ab_im = mag * jnp.sin(li * dt)
    den = lr * lr + li * li
    nr = ab_re - 1.0
    coef_re = (nr * lr + ab_im * li) / den
    coef_im = (ab_im * lr - nr * li) / den
    return ab_re, ab_im, coef_re, coef_im


def _disc_a_power(n):
    def fn(lr, li, ldt):
        ab_re, ab_im, coef_re, coef_im = _disc_a(lr, li, ldt)
        pr, pi, sr, si, m = None, None, ab_re, ab_im, n
        while m:
            if m & 1:
                pr, pi = (sr, si) if pr is None else (pr * sr - pi * si, pr * si + pi * sr)
            m >>= 1
            if m:
                sr, si = sr * sr - si * si, 2.0 * sr * si
        return ab_re, ab_im, pr, pi, coef_re, coef_im
    return fn


def _disc_b(coef_re, coef_im, bt_re, bt_im):
    return coef_re * bt_re - coef_im * bt_im, coef_re * bt_im + coef_im * bt_re


def _lb_fn(logits):
    return jax.nn.softmax(logits, axis=0)[0:1]


def _adamw(w, g, m, v):
    m = ADAM_B1 * m + (1.0 - ADAM_B1) * g
    v = ADAM_B2 * v + (1.0 - ADAM_B2) * jnp.square(g)
    m_hat = m / (1.0 - ADAM_B1 ** ADAM_STEP)
    v_hat = v / (1.0 - ADAM_B2 ** ADAM_STEP)
    delta = -ADAM_LR * (m_hat / (jnp.sqrt(v_hat) + ADAM_EPS) + ADAM_WD * w)
    return delta, m, v


def _norm_call(h, g, tm, name):
    T, D = h.shape

    def body(h_ref, g_ref, z_ref):
        z_ref[...] = _rms(h_ref[...], g_ref[...]).astype(ACT)

    return pl.pallas_call(
        body, name=name, out_shape=jax.ShapeDtypeStruct((T, D), ACT), grid=(T // tm,),
        in_specs=[pl.BlockSpec((tm, D), lambda i: (i, 0)), pl.BlockSpec((1, D), lambda i: (0, 0))],
        out_specs=pl.BlockSpec((tm, D), lambda i: (i, 0)), compiler_params=_cparams())(h, g)


def _mm_shard(x, w, tm, name, major):
    T, K = x.shape
    S, _, N = w.shape

    def body(x_ref, w_ref, o_ref):
        o_ref[...] = _dot(x_ref[...], w_ref[...]).astype(o_ref.dtype)

    if major:
        out_shape = jax.ShapeDtypeStruct((S, T, N), ACT)
        out_spec = pl.BlockSpec((None, tm, N), lambda j, i: (j, i, 0))
    else:
        out_shape = jax.ShapeDtypeStruct((T, S * N), ACT)
        out_spec = pl.BlockSpec((tm, N), lambda j, i: (i, j))
    return pl.pallas_call(
        body, name=name, out_shape=out_shape, grid=(S, T // tm),
        in_specs=[pl.BlockSpec((tm, K), lambda j, i: (i, 0)), pl.BlockSpec((None, K, N), lambda j, i: (j, 0, 0))],
        out_specs=out_spec, compiler_params=_cparams())(x, w)


def _mm_tn(x, y, n_shards, tm, name, major):
    T, K = x.shape
    S = n_shards
    N = y.shape[-1] if major else y.shape[-1] // S

    def body(x_ref, y_ref, o_ref):
        @pl.when(pl.program_id(1) == 0)
        def _():
            o_ref[...] = jnp.zeros_like(o_ref)
        o_ref[...] += _dot_tn(x_ref[...], y_ref[...])

    y_spec = (pl.BlockSpec((None, tm, N), lambda j, i: (j, i, 0)) if major
              else pl.BlockSpec((tm, N), lambda j, i: (i, j)))
    return pl.pallas_call(
        body, name=name, out_shape=jax.ShapeDtypeStruct((S, K, N), F32), grid=(S, T // tm),
        in_specs=[pl.BlockSpec((tm, K), lambda j, i: (i, 0)), y_spec],
        out_specs=pl.BlockSpec((None, K, N), lambda j, i: (j, 0, 0)), compiler_params=_cparams())(x, y)


def _lin_bwd(x, dy, w, tm, name):
    T, K = x.shape
    N = dy.shape[1]

    def body(x_ref, dy_ref, w_ref, dx_ref, dw_ref):
        @pl.when(pl.program_id(0) == 0)
        def _():
            dw_ref[...] = jnp.zeros_like(dw_ref)
        dy = dy_ref[...]
        dx_ref[...] = _dot_nt(dy, w_ref[...]).astype(dx_ref.dtype)
        dw_ref[...] += _dot_tn(x_ref[...], dy)

    return pl.pallas_call(
        body, name=name,
        out_shape=(jax.ShapeDtypeStruct((T, K), ACT), jax.ShapeDtypeStruct((K, N), F32)), grid=(T // tm,),
        in_specs=[pl.BlockSpec((tm, K), lambda i: (i, 0)), pl.BlockSpec((tm, N), lambda i: (i, 0)),
                  pl.BlockSpec((K, N), lambda i: (0, 0))],
        out_specs=(pl.BlockSpec((tm, K), lambda i: (i, 0)), pl.BlockSpec((K, N), lambda i: (0, 0))),
        compiler_params=_cparams())(x, dy, w)


N_SEG = SUBLANES


def _seg_len(L):
    return -(-L // (N_SEG * SUBLANES)) * SUBLANES


def _to_segments(a3, seg):
    b, length, c = a3.shape
    a = jnp.pad(a3, ((0, 0), (0, N_SEG * seg - length), (0, 0)))
    return a.reshape(b, N_SEG, seg, c).transpose(0, 2, 1, 3).reshape(b, N_SEG * seg, c)


def _from_segments(a3, seg, length):
    b, _, c = a3.shape
    return a3.reshape(b, seg, N_SEG, c).transpose(0, 2, 1, 3).reshape(b, N_SEG * seg, c)[:, :length]


def _seg_scan(x_ref, tab_ref, n_slabs, reverse):
    hw = x_ref.shape[1] // 2
    sign = -1.0 if reverse else 1.0
    ar, ai = tab_ref[0][:, :hw], sign * tab_ref[0][:, hw:]
    br, bi = tab_ref[1][:, :hw], sign * tab_ref[1][:, hw:]

    def slab(k):
        kk = (n_slabs - 1 - k) if reverse else k
        return pl.ds(pl.multiple_of(kk * SUBLANES, SUBLANES), SUBLANES)

    def horner(k, carry):
        cr, ci = carry
        x = x_ref[slab(k), :]
        return ar * cr - ai * ci + x[:, :hw], ar * ci + ai * cr + x[:, hw:]

    z = jnp.zeros((SUBLANES, hw), F32)
    fr, fi = lax.fori_loop(0, n_slabs, horner, (z, z))

    row = lax.broadcasted_iota(jnp.int32, (SUBLANES, hw), 0)
    edge = (row == SUBLANES - 1) if reverse else (row == 0)
    shift = SUBLANES - 1 if reverse else 1
    sr, si = z, z
    for _ in range(N_SEG - 1):
        er, ei = fr + br * sr - bi * si, fi + br * si + bi * sr
        sr = jnp.where(edge, 0.0, pltpu.roll(er, shift, 0))
        si = jnp.where(edge, 0.0, pltpu.roll(ei, shift, 0))

    def scan(k, carry):
        cr, ci = carry
        rows = slab(k)
        x = x_ref[rows, :]
        nr, ni = ar * cr - ai * ci + x[:, :hw], ar * ci + ai * cr + x[:, hw:]
        x_ref[rows, 0:hw] = nr
        x_ref[rows, hw:2 * hw] = ni
        return nr, ni

    lax.fori_loop(0, n_slabs, scan, (sr, si))


def _s5_fwd_call(p3, wb, wc, tab_f, dsk, name):
    B, L, _ = p3.shape
    n_cb, cw, sw = wb.shape

    def body(u_ref, wb_ref, wc_ref, tab_ref, d_ref, ya_ref, so_ref, s_ref):
        u = u_ref[...]
        s_ref[...] = _dot(u, wb_ref[...])
        _seg_scan(s_ref, tab_ref, L // SUBLANES, False)
        s = s_ref[...].astype(MXU)
        so_ref[...] = s
        y = _dot(s, wc_ref[...]) + d_ref[...] * u.astype(F32)
        ya_ref[...] = jax.nn.gelu(y).astype(ACT)

    return pl.pallas_call(
        body, name=name,
        out_shape=(jax.ShapeDtypeStruct((B, L, n_cb * cw), ACT), jax.ShapeDtypeStruct((B, n_cb, L, sw), MXU)),
        grid=(B, n_cb),
        in_specs=[pl.BlockSpec((None, L, cw), lambda b, c: (b, 0, c)),
                  pl.BlockSpec((None, cw, sw), lambda b, c: (c, 0, 0)),
                  pl.BlockSpec((None, sw, cw), lambda b, c: (c, 0, 0)),
                  pl.BlockSpec((None, 2, SUBLANES, sw), lambda b, c: (c, 0, 0, 0)),
                  pl.BlockSpec((None, 1, cw), lambda b, c: (c, 0, 0))],
        out_specs=(pl.BlockSpec((None, L, cw), lambda b, c: (b, 0, c)),
                   pl.BlockSpec((None, None, L, sw), lambda b, c: (b, c, 0, 0))),
        scratch_shapes=[pltpu.VMEM((L, sw), F32)], compiler_params=_cparams())(p3, wb, wc, tab_f, dsk)


def _s5_bwd_call(p3, s_all, dya, wb, wc, tab_r, dsk, name):
    B, L, _ = p3.shape
    n_cb, cw, sw = wb.shape
    hw = sw // 2
    n_slabs = L // SUBLANES

    def body(u_ref, si_ref, dya_ref, wb_ref, wc_ref, tr_ref, d_ref,
             du_ref, dwb_ref, dwc_ref, da_ref, dd_ref, s_ref, l_ref):
        @pl.when(pl.program_id(1) == 0)
        def _():
            dwb_ref[...] = jnp.zeros_like(dwb_ref)
            dwc_ref[...] = jnp.zeros_like(dwc_ref)
            da_ref[...] = jnp.zeros_like(da_ref)
            dd_ref[...] = jnp.zeros_like(dd_ref)

        u = u_ref[...]
        uf = u.astype(F32)
        s_in = si_ref[...]
        s_ref[...] = s_in.astype(F32)
        y = _dot(s_in, wc_ref[...]) + d_ref[...] * uf
        _, gelu_vjp = jax.vjp(jax.nn.gelu, y)
        dy = gelu_vjp(dya_ref[...].astype(F32))[0]
        dd_ref[...] += jnp.sum(dy * uf, axis=0, keepdims=True)
        l_ref[...] = _dot_nt(dy, wc_ref[...])
        _seg_scan(l_ref, tr_ref, n_slabs, True)
        du_ref[...] = (_dot_nt(l_ref[...], wb_ref[...]) + d_ref[...] * dy).astype(ACT)
        dwb_ref[...] += _dot_tn(u, l_ref[...])
        dwc_ref[...] += _dot_tn(s_in, dy)

        row = lax.broadcasted_iota(jnp.int32, (SUBLANES, hw), 0)
        last = s_ref[pl.ds((n_slabs - 1) * SUBLANES, SUBLANES), :]
        p0r = jnp.where(row == 0, 0.0, pltpu.roll(last[:, :hw], 1, 0))
        p0i = jnp.where(row == 0, 0.0, pltpu.roll(last[:, hw:], 1, 0))

        def step(k, carry):
            qr, qi, accr, acci = carry
            r0 = pl.multiple_of(k * SUBLANES, SUBLANES)
            s = s_ref[pl.ds(r0, SUBLANES), :]
            lam = l_ref[pl.ds(r0, SUBLANES), :]
            lr, li = lam[:, :hw], lam[:, hw:]
            accr = accr + lr * qr + li * qi
            acci = acci + li * qr - lr * qi
            return s[:, :hw], s[:, hw:], accr, acci

        z8 = jnp.zeros((SUBLANES, hw), F32)
        _, _, accr, acci = lax.fori_loop(0, n_slabs, step, (p0r, p0i, z8, z8))
        da_ref[...] += jnp.concatenate([jnp.sum(accr, axis=0, keepdims=True),
                                        jnp.sum(acci, axis=0, keepdims=True)], axis=1)

    W = n_cb * cw
    return pl.pallas_call(
        body, name=name,
        out_shape=(jax.ShapeDtypeStruct((B, L, W), ACT), jax.ShapeDtypeStruct((n_cb, cw, sw), F32),
                   jax.ShapeDtypeStruct((n_cb, sw, cw), F32), jax.ShapeDtypeStruct((n_cb, 1, sw), F32),
                   jax.ShapeDtypeStruct((n_cb, 1, cw), F32)),
        grid=(n_cb, B),
        in_specs=[pl.BlockSpec((None, L, cw), lambda c, b: (b, 0, c)),
                  pl.BlockSpec((None, None, L, sw), lambda c, b: (b, c, 0, 0)),
                  pl.BlockSpec((None, L, cw), lambda c, b: (b, 0, c)),
                  pl.BlockSpec((None, cw, sw), lambda c, b: (c, 0, 0)),
                  pl.BlockSpec((None, sw, cw), lambda c, b: (c, 0, 0)),
                  pl.BlockSpec((None, 2, SUBLANES, sw), lambda c, b: (c, 0, 0, 0)),
                  pl.BlockSpec((None, 1, cw), lambda c, b: (c, 0, 0))],
        out_specs=(pl.BlockSpec((None, L, cw), lambda c, b: (b, 0, c)),
                   pl.BlockSpec((None, cw, sw), lambda c, b: (c, 0, 0)),
                   pl.BlockSpec((None, sw, cw), lambda c, b: (c, 0, 0)),
                   pl.BlockSpec((None, 1, sw), lambda c, b: (c, 0, 0)),
                   pl.BlockSpec((None, 1, cw), lambda c, b: (c, 0, 0))),
        scratch_shapes=[pltpu.VMEM((L, sw), F32), pltpu.VMEM((L, sw), F32)],
        compiler_params=_cparams())(p3, s_all, dya, wb, wc, tab_r, dsk)


def _glu_proj_call(ya, wglu, wproj, tm, name):
    T, W = ya.shape
    D = wproj.shape[1]

    def body(ya_ref, wg_ref, wp_ref, yo_ref, a_ref):
        ya = ya_ref[...]
        yo = ya.astype(F32) * jax.nn.sigmoid(_dot(ya, wg_ref[...]))
        yo_ref[...] = yo.astype(ACT)
        a_ref[...] = _dot(yo, wp_ref[...]).astype(ACT)

    return pl.pallas_call(
        body, name=name, out_shape=(jax.ShapeDtypeStruct((T, W), ACT), jax.ShapeDtypeStruct((T, D), ACT)),
        grid=(T // tm,),
        in_specs=[pl.BlockSpec((tm, W), lambda i: (i, 0)), pl.BlockSpec((W, W), lambda i: (0, 0)),
                  pl.BlockSpec((W, D), lambda i: (0, 0))],
        out_specs=(pl.BlockSpec((tm, W), lambda i: (i, 0)), pl.BlockSpec((tm, D), lambda i: (i, 0))),
        compiler_params=_cparams())(ya, wglu, wproj)


def _glu_bwd_call(ya, dyo, wglu, tm, name):
    T, W = ya.shape

    def body(ya_ref, dyo_ref, wg_ref, dya_ref, dwg_ref):
        @pl.when(pl.program_id(0) == 0)
        def _():
            dwg_ref[...] = jnp.zeros_like(dwg_ref)
        ya = ya_ref[...]
        yaf = ya.astype(F32)
        dyo = dyo_ref[...].astype(F32)
        sg = jax.nn.sigmoid(_dot(ya, wg_ref[...]))
        dt = dyo * yaf * sg * (1.0 - sg)
        dya_ref[...] = (dyo * sg + _dot_nt(dt, wg_ref[...])).astype(ACT)
        dwg_ref[...] += _dot_tn(ya, dt)

    return pl.pallas_call(
        body, name=name, out_shape=(jax.ShapeDtypeStruct((T, W), ACT), jax.ShapeDtypeStruct((W, W), F32)),
        grid=(T // tm,),
        in_specs=[pl.BlockSpec((tm, W), lambda i: (i, 0)), pl.BlockSpec((tm, W), lambda i: (i, 0)),
                  pl.BlockSpec((W, W), lambda i: (0, 0))],
        out_specs=(pl.BlockSpec((tm, W), lambda i: (i, 0)), pl.BlockSpec((W, W), lambda i: (0, 0))),
        compiler_params=_cparams())(ya, dyo, wglu)


PAD = 16


def _chunk_cumsums(x, pad_ref, L):
    row = lax.broadcasted_iota(jnp.int32, x.shape, 0) % CHUNK
    zeros = jnp.zeros((PAD, x.shape[1]), F32)
    pad_ref[0:PAD, :] = zeros
    pad_ref[PAD + L:2 * PAD + L, :] = zeros
    c = x
    r = x
    d = 1
    while d < CHUNK:
        pad_ref[PAD:PAD + L, :] = c
        c = c + jnp.where(row >= d, pad_ref[PAD - d:PAD - d + L, :], 0.0)
        pad_ref[PAD:PAD + L, :] = r
        r = r + jnp.where(row + d < CHUNK, pad_ref[PAD + d:PAD + d + L, :], 0.0)
        d *= 2
    return c, r - x


def _hgrn_prep(q_ref, fl_ref, lb_ref, pad_ref, r0, n):
    rows = pl.ds(r0, n)
    lb = lb_ref[...]
    sig = jax.nn.sigmoid(fl_ref[rows, :].astype(F32))
    f = lb + (1.0 - lb) * sig
    k = 1.0 - f
    c, rc = _chunk_cumsums(jnp.log(f), pad_ref, n)
    e_in, e_inv, e_out = jnp.exp(c), jnp.exp(-c), jnp.exp(rc)
    q = q_ref[rows, :].astype(F32)
    return dict(sig=sig, f=f, k=k, q=q, e_in=e_in, e_inv=e_inv, e_out=e_out, dec=jnp.exp(c + rc))


def _for_row_blocks(L, fn):
    full = L // GROUP
    if full:
        def step(g, carry):
            fn(pl.multiple_of(g * GROUP, GROUP), GROUP)
            return carry
        lax.fori_loop(0, full, step, 0)
    if L % GROUP:
        fn(full * GROUP, L % GROUP)


def _chunk_mask(rb):
    r = lax.broadcasted_iota(jnp.int32, (rb, rb), 0)
    c = lax.broadcasted_iota(jnp.int32, (rb, rb), 1)
    return (r // CHUNK == c // CHUNK) & (c <= r)


def _hg_out(o, og, g):
    on = o * lax.rsqrt(jnp.mean(o * o, axis=-1, keepdims=True) + EPS) * g
    return on * _silu(og)


def _hgrn_specs(L, hd, col_q, n_heads, order):
    def spec(sec):
        return pl.BlockSpec((None, L, hd), lambda *g: (order(*g)[0], 0, col_q + sec * n_heads + order(*g)[1]))
    return [spec(0), spec(1), spec(2), spec(3)]


GROUP = 128
CPG = GROUP // CHUNK


def _expand(x):
    xf = x.astype(F32)
    chunk = lax.broadcasted_iota(jnp.int32, xf.shape, 0) // CHUNK
    return jnp.concatenate([jnp.where(chunk == j, xf, 0.0) for j in range(CPG)], axis=1)


def _fill_tail(refs_fills, L):
    for ref, fill in refs_fills:
        if ref.shape[0] > L:
            ref[L:ref.shape[0], :] = jnp.full((ref.shape[0] - L, ref.shape[1]), fill, ref.dtype)


GROUP_UNROLL = 4


def _hgrn_forward_core(q_ref, fl_ref, v_ref, lb_ref, pad_ref, qin_ref, kin_ref, kout_ref, vp_ref, dec_ref, o_ref,
                       s_ref, a_ref, L, keep=()):
    hd = qin_ref.shape[1]
    n_groups = qin_ref.shape[0] // GROUP

    def prep(r0, n):
        pp = _hgrn_prep(q_ref, fl_ref, lb_ref, pad_ref, r0, n)
        rows = pl.ds(r0, n)
        for key, ref in keep:
            ref[rows, :] = pp[key]
        qin_ref[rows, :] = (pp["q"] * pp["e_in"]).astype(MXU)
        kin_ref[rows, :] = (pp["k"] * pp["e_inv"]).astype(MXU)
        kout_ref[rows, :] = (pp["k"] * pp["e_out"]).astype(MXU)
        vp_ref[rows, :] = v_ref[rows, :].astype(MXU)
        dec_ref[rows, :] = pp["dec"]

    _for_row_blocks(L, prep)
    _fill_tail(((qin_ref, 0.0), (kin_ref, 0.0), (kout_ref, 0.0), (vp_ref, 0.0), (dec_ref, 1.0)), L)
    mask = _chunk_mask(GROUP)

    def scores(g, carry):
        rows = pl.ds(pl.multiple_of(g * GROUP, GROUP), GROUP)
        a_ref[rows, :] = jnp.where(mask, _dot_nt(qin_ref[rows, :], kin_ref[rows, :]), 0.0).astype(MXU)
        return carry

    lax.fori_loop(0, n_groups, scores, 0, unroll=GROUP_UNROLL)

    def intra(g, carry):
        rows = pl.ds(pl.multiple_of(g * GROUP, GROUP), GROUP)
        o_ref[rows, :] = _dot(a_ref[rows, :], vp_ref[rows, :])
        kv = _dot_tn(vp_ref[rows, :], _expand(kout_ref[rows, :]))
        for j in range(CPG):
            s_ref[g * CPG + j] = kv[:, j * hd:(j + 1) * hd]
        return carry

    lax.fori_loop(0, n_groups, intra, 0, unroll=GROUP_UNROLL)

    def rec(n, st):
        kv = s_ref[n]
        s_ref[n] = st
        dec = dec_ref[pl.ds(pl.multiple_of(n * CHUNK, CHUNK), SUBLANES), :][0:1]
        return st * dec + kv

    lax.fori_loop(0, L // CHUNK, rec, jnp.zeros((hd, hd), F32))

    def inter(g, carry):
        rows = pl.ds(pl.multiple_of(g * GROUP, GROUP), GROUP)
        scat = jnp.concatenate([s_ref[g * CPG + j] for j in range(CPG)], axis=1)
        o_ref[rows, :] += _dot_nt(_expand(qin_ref[rows, :]), scat)
        return carry

    lax.fori_loop(0, n_groups, inter, 0, unroll=GROUP_UNROLL)


def _hgrn_scratch(L, hd):
    lp = -(-L // GROUP) * GROUP
    return lp, [pltpu.VMEM((GROUP + 2 * PAD, hd), F32), pltpu.VMEM((lp, hd), MXU), pltpu.VMEM((lp, hd), MXU),
                pltpu.VMEM((lp, hd), MXU), pltpu.VMEM((lp, hd), MXU), pltpu.VMEM((lp, hd), F32),
                pltpu.VMEM((lp, hd), F32), pltpu.VMEM((lp // CHUNK, hd, hd), F32), pltpu.VMEM((lp, GROUP), MXU)]


def _hgrn_fwd_call(p3, lb, ng, n_heads, col_q, name):
    B, L, _ = p3.shape
    hd = ng.shape[1]
    _, scratch = _hgrn_scratch(L, hd)

    def body(q_ref, fl_ref, v_ref, og_ref, lb_ref, ng_ref, yb_ref,
             pad_ref, qin_ref, kin_ref, kout_ref, vp_ref, dec_ref, o_ref, s_ref, a_ref):
        _hgrn_forward_core(q_ref, fl_ref, v_ref, lb_ref, pad_ref, qin_ref, kin_ref, kout_ref, vp_ref, dec_ref,
                           o_ref, s_ref, a_ref, L)

        def out(r0, n):
            rows = pl.ds(r0, n)
            yb_ref[rows, :] = _hg_out(o_ref[rows, :], og_ref[rows, :].astype(F32), ng_ref[...]).astype(ACT)

        _for_row_blocks(L, out)

    order = lambda b, h: (b, h)
    return pl.pallas_call(
        body, name=name, out_shape=jax.ShapeDtypeStruct((B, L, n_heads * hd), ACT), grid=(B, n_heads),
        in_specs=_hgrn_specs(L, hd, col_q, n_heads, order) + [
            pl.BlockSpec((1, hd), lambda b, h: (0, h)), pl.BlockSpec((1, hd), lambda b, h: (0, 0))],
        out_specs=pl.BlockSpec((None, L, hd), lambda b, h: (b, 0, h)),
        scratch_shapes=scratch, compiler_params=_cparams())(p3, p3, p3, p3, lb, ng)


def _hgrn_bwd_call(p3, dyb, lb, ng, n_heads, col_q, name):
    B, L, _ = p3.shape
    hd = ng.shape[1]
    n_chunks = L // CHUNK
    lp, scratch = _hgrn_scratch(L, hd)
    n_groups = lp // GROUP

    def body(q_ref, fl_ref, v_ref, og_ref, dyb_ref, lb_ref, ng_ref,
             dq_ref, dfl_ref, dv_ref, dog_ref, dlb_ref, dng_ref,
             pad_ref, qin_ref, kin_ref, kout_ref, vp_ref, dec_ref, o_ref, s_ref, a_ref,
             do_ref, ds_ref, dqi_ref, dki_ref, dko_ref, dvv_ref, dct_ref,
             sig_ref, f_ref, ein_ref, einv_ref, eout_ref, da_ref):
        @pl.when(pl.program_id(1) == 0)
        def _():
            dlb_ref[...] = jnp.zeros_like(dlb_ref)

        @pl.when((pl.program_id(0) == 0) & (pl.program_id(1) == 0))
        def _():
            dng_ref[...] = jnp.zeros_like(dng_ref)

        _hgrn_forward_core(q_ref, fl_ref, v_ref, lb_ref, pad_ref, qin_ref, kin_ref, kout_ref, vp_ref, dec_ref,
                           o_ref, s_ref, a_ref, L, keep=(("sig", sig_ref), ("f", f_ref), ("e_in", ein_ref),
                                                  ("e_inv", einv_ref), ("e_out", eout_ref)))

        def out_bwd(r0, n):
            rows = pl.ds(r0, n)
            _, out_vjp = jax.vjp(_hg_out, o_ref[rows, :], og_ref[rows, :].astype(F32), ng_ref[...])
            d_o, d_og, d_ng = out_vjp(dyb_ref[rows, :].astype(F32))
            dog_ref[rows, :] = d_og.astype(ACT)
            dng_ref[...] += d_ng
            do_ref[rows, :] = d_o.astype(MXU)

        _for_row_blocks(L, out_bwd)
        _fill_tail(((do_ref, 0.0),), L)
        mask = _chunk_mask(GROUP)

        def score_grads(g, carry):
            rows = pl.ds(pl.multiple_of(g * GROUP, GROUP), GROUP)
            da_ref[rows, :] = jnp.where(mask, _dot_nt(do_ref[rows, :], vp_ref[rows, :]), 0.0).astype(MXU)
            return carry

        lax.fori_loop(0, n_groups, score_grads, 0, unroll=GROUP_UNROLL)

        def grads_a(g, carry):
            rows = pl.ds(pl.multiple_of(g * GROUP, GROUP), GROUP)
            qi, ki, do, da = qin_ref[rows, :], kin_ref[rows, :], do_ref[rows, :], da_ref[rows, :]
            sstack = s_ref[pl.ds(g * CPG, CPG)].reshape(CPG * hd, hd)
            dqi_ref[rows, :] = _dot(da, ki) + _dot(_expand(do), sstack)
            dki_ref[rows, :] = _dot_tn(da, qi)
            dvv_ref[rows, :] = _dot_tn(a_ref[rows, :], do)
            x = _dot_tn(do, _expand(qi))
            for j in range(CPG):
                ds_ref[g * CPG + j] = x[:, j * hd:(j + 1) * hd]
            return carry

        lax.fori_loop(0, n_groups, grads_a, 0, unroll=GROUP_UNROLL)

        def rec_bwd(k, dst):
            n = n_chunks - 1 - k
            r0 = pl.multiple_of(n * CHUNK, CHUNK)
            x = ds_ref[n]
            ds_ref[n] = dst
            dec = dec_ref[pl.ds(r0, SUBLANES), :][0:1]
            return dst * dec + x

        lax.fori_loop(0, n_chunks, rec_bwd, jnp.zeros((hd, hd), F32))

        def grads_b(g, carry):
            r0 = pl.multiple_of(g * GROUP, GROUP)
            rows = pl.ds(r0, GROUP)
            ds = [ds_ref[g * CPG + j] for j in range(CPG)]
            dscat = jnp.concatenate(ds, axis=1)
            dvv_ref[rows, :] += _dot_nt(_expand(kout_ref[rows, :]), dscat)
            dstack = ds_ref[pl.ds(g * CPG, CPG)].reshape(CPG * hd, hd)
            dko_ref[rows, :] = _dot(_expand(vp_ref[rows, :]), dstack)
            for j in range(CPG):
                dec = dec_ref[pl.ds(r0 + j * CHUNK, SUBLANES), :][0:1]
                ddec = dec * jnp.sum(ds[j] * s_ref[g * CPG + j], axis=0, keepdims=True)
                dct_ref[pl.ds(r0 + j * CHUNK, CHUNK), :] = jnp.broadcast_to(ddec, (CHUNK, hd))
            return carry

        lax.fori_loop(0, n_groups, grads_b, 0, unroll=GROUP_UNROLL)

        def finish(r0, n):
            rows = pl.ds(r0, n)
            sig, f, e_in, e_inv, e_out = [r[rows, :] for r in (sig_ref, f_ref, ein_ref, einv_ref, eout_ref)]
            q, k = q_ref[rows, :].astype(F32), 1.0 - f
            dqi, dki, dko = dqi_ref[rows, :], dki_ref[rows, :], dko_ref[rows, :]
            dq = dqi * e_in
            dk = dki * e_inv + dko * e_out
            dq_ref[rows, :] = dq.astype(ACT)
            dv_ref[rows, :] = dvv_ref[rows, :].astype(ACT)
            t_out = k * e_out * dko
            dc = q * dq - k * e_inv * dki - t_out
            _, dc_later = _chunk_cumsums(dc, pad_ref, n)
            t_incl, t_later = _chunk_cumsums(t_out, pad_ref, n)
            dlogf = dc + dc_later + t_incl + t_later + dct_ref[rows, :]
            df = dlogf / f - dk
            dfl_ref[rows, :] = (df * (1.0 - lb_ref[...]) * sig * (1.0 - sig)).astype(ACT)
            dlb_ref[...] += jnp.sum(df * (1.0 - sig), axis=0, keepdims=True)

        _for_row_blocks(L, finish)

    order = lambda h, b: (b, h)
    W = n_heads * hd
    act_out = jax.ShapeDtypeStruct((B, L, W), ACT)
    blk_out = pl.BlockSpec((None, L, hd), lambda h, b: (b, 0, h))
    return pl.pallas_call(
        body, name=name,
        out_shape=(act_out, act_out, act_out, act_out, jax.ShapeDtypeStruct((1, W), F32),
                   jax.ShapeDtypeStruct((1, hd), F32)),
        grid=(n_heads, B),
        in_specs=_hgrn_specs(L, hd, col_q, n_heads, order) + [
            pl.BlockSpec((None, L, hd), lambda h, b: (b, 0, h)),
            pl.BlockSpec((1, hd), lambda h, b: (0, h)), pl.BlockSpec((1, hd), lambda h, b: (0, 0))],
        out_specs=(blk_out, blk_out, blk_out, blk_out, pl.BlockSpec((1, hd), lambda h, b: (0, h)),
                   pl.BlockSpec((1, hd), lambda h, b: (0, 0))),
        scratch_shapes=scratch + [
            pltpu.VMEM((lp, hd), MXU), pltpu.VMEM((lp // CHUNK, hd, hd), F32)] + [pltpu.VMEM((lp, hd), F32)] * 10 + [
            pltpu.VMEM((lp, GROUP), MXU)],
        compiler_params=_cparams())(p3, p3, p3, p3, dyb, lb, ng)


def _merge_fn(a, bm, ga, gb):
    return jax.nn.sigmoid(ga) * a + jax.nn.sigmoid(gb) * bm


def _merge_call(yb, a, p, h0, whp, wout, g2, col_ga, tm, name):
    T, D = h0.shape

    def body(yb_ref, a_ref, ga_ref, gb_ref, h0_ref, whp_ref, wout_ref, g2_ref, h1_ref, mg_ref, bm_ref, z2_ref):
        bm = _dot(yb_ref[...], whp_ref[...])
        mg = _merge_fn(a_ref[...].astype(F32), bm, ga_ref[...].astype(F32), gb_ref[...].astype(F32))
        h1 = h0_ref[...] + _dot(mg, wout_ref[...])
        h1_ref[...] = h1
        mg_ref[...] = mg.astype(ACT)
        bm_ref[...] = bm.astype(ACT)
        z2_ref[...] = _rms(h1, g2_ref[...]).astype(ACT)

    tile = pl.BlockSpec((tm, D), lambda i: (i, 0))
    full = pl.BlockSpec((D, D), lambda i: (0, 0))
    act = jax.ShapeDtypeStruct((T, D), ACT)
    return pl.pallas_call(
        body, name=name, out_shape=(jax.ShapeDtypeStruct((T, D), F32), act, act, act), grid=(T // tm,),
        in_specs=[tile, tile, pl.BlockSpec((tm, D), lambda i: (i, col_ga)),
                  pl.BlockSpec((tm, D), lambda i: (i, col_ga + 1)), tile, full, full,
                  pl.BlockSpec((1, D), lambda i: (0, 0))],
        out_specs=(tile, tile, tile, tile), compiler_params=_cparams())(yb, a, p, p, h0, whp, wout, g2)


def _merge_bwd_call(dmg, a, bm, p, col_ga, tm, name):
    T, D = dmg.shape

    def body(dmg_ref, a_ref, bm_ref, ga_ref, gb_ref, da_ref, dbm_ref, dga_ref, dgb_ref):
        args = [r[...].astype(F32) for r in (a_ref, bm_ref, ga_ref, gb_ref)]
        _, vjp = jax.vjp(_merge_fn, *args)
        for r, o in zip((da_ref, dbm_ref, dga_ref, dgb_ref), vjp(dmg_ref[...].astype(F32))):
            r[...] = o.astype(ACT)

    tile = pl.BlockSpec((tm, D), lambda i: (i, 0))
    act = jax.ShapeDtypeStruct((T, D), ACT)
    return pl.pallas_call(
        body, name=name, out_shape=(act, act, act, act), grid=(T // tm,),
        in_specs=[tile, tile, tile, pl.BlockSpec((tm, D), lambda i: (i, col_ga)),
                  pl.BlockSpec((tm, D), lambda i: (i, col_ga + 1))],
        out_specs=(tile, tile, tile, tile), compiler_params=_cparams())(dmg, a, bm, p, p)


def _conv_taps(x_ref, halo_ref, ext_ref, edge, tm, before):
    halo = jnp.where(edge, 0.0, halo_ref[...].astype(F32))
    x = x_ref[...].astype(F32)
    if before:
        ext_ref[0:PAD, :] = halo
        ext_ref[PAD:PAD + tm, :] = x
        return [ext_ref[PAD - 2 + k:PAD - 2 + k + tm, :] for k in range(3)]
    ext_ref[0:tm, :] = x
    ext_ref[tm:tm + PAD, :] = halo
    return [ext_ref[k:k + tm, :] for k in range(3)]


def _conv(taps, cw, cb):
    return cb + cw[0:1] * taps[0] + cw[1:2] * taps[1] + cw[2:3] * taps[2]


def _ffn_pair_specs(tm, F, T, n_pairs, order, before):
    hb = tm // PAD
    last = T // PAD - 1

    def halo_row(i):
        return jnp.maximum(i * hb - 1, 0) if before else jnp.minimum((i + 1) * hb, last)

    specs = []
    for off in (0, n_pairs):
        specs.append(pl.BlockSpec((None, tm, F), lambda *g, off=off: (order(*g)[1] + off, order(*g)[0], 0)))
        specs.append(pl.BlockSpec((None, PAD, F), lambda *g, off=off: (order(*g)[1] + off, halo_row(order(*g)[0]), 0)))
    return specs


def _ffn_fwd_call(up, cw, cb, wd, h1, tgt, g3, tm, tps, name):
    S, T, F = up.shape
    n_pairs = S // 2
    D = h1.shape[1]

    def body(ua_ref, ha_ref, ub_ref, hb_ref, cwa_ref, cwb_ref, cba_ref, cbb_ref, wd_ref, h1_ref, tgt_ref, g3_ref,
             ca_ref, cb_ref, dh2_ref, loss_ref, dg3_ref, acc_ref, ext_ref):
        i, j = pl.program_id(0), pl.program_id(1)
        edge = (i % tps) == 0
        ua = _conv(_conv_taps(ua_ref, ha_ref, ext_ref, edge, tm, True), cwa_ref[...], cba_ref[...])
        ub = _conv(_conv_taps(ub_ref, hb_ref, ext_ref, edge, tm, True), cwb_ref[...], cbb_ref[...])
        ca_ref[...] = ua.astype(ACT)
        cb_ref[...] = ub.astype(ACT)
        contrib = _dot(_silu(ua) * ub, wd_ref[...])

        @pl.when(j == 0)
        def _():
            acc_ref[...] = h1_ref[...] + contrib

        @pl.when(j > 0)
        def _():
            acc_ref[...] += contrib

        @pl.when((i == 0) & (j == 0))
        def _():
            loss_ref[...] = jnp.zeros_like(loss_ref)
            dg3_ref[...] = jnp.zeros_like(dg3_ref)

        @pl.when(j == n_pairs - 1)
        def _():
            row = lax.broadcasted_iota(jnp.int32, (tm, 1), 0) + (i % tps) * tm
            valid = row >= N_META
            tgt = tgt_ref[...]

            def loss_fn(h2, g):
                err = _rms(h2, g) - tgt
                return 0.5 * jnp.sum(jnp.where(valid, err * err, 0.0)) / D

            loss, vjp = jax.vjp(loss_fn, acc_ref[...], g3_ref[...])
            dh2, dg3 = vjp(jnp.ones((), F32))
            dh2_ref[...] = dh2
            loss_ref[...] += loss
            dg3_ref[...] += dg3

    order = lambda i, j: (i, j)
    tile = pl.BlockSpec((tm, D), lambda i, j: (i, 0))
    vec = pl.BlockSpec((1, D), lambda i, j: (0, 0))
    return pl.pallas_call(
        body, name=name,
        out_shape=(jax.ShapeDtypeStruct((n_pairs, T, F), ACT), jax.ShapeDtypeStruct((n_pairs, T, F), ACT),
                   jax.ShapeDtypeStruct((T, D), F32), jax.ShapeDtypeStruct((1, LANES), F32),
                   jax.ShapeDtypeStruct((1, D), F32)),
        grid=(T // tm, n_pairs),
        in_specs=_ffn_pair_specs(tm, F, T, n_pairs, order, True) + [
            pl.BlockSpec((None, 3, F), lambda i, j: (j, 0, 0)), pl.BlockSpec((None, 3, F), lambda i, j: (j + n_pairs, 0, 0)),
            pl.BlockSpec((None, 1, F), lambda i, j: (j, 0, 0)), pl.BlockSpec((None, 1, F), lambda i, j: (j + n_pairs, 0, 0)),
            pl.BlockSpec((None, F, D), lambda i, j: (j, 0, 0)), tile, tile, vec],
        out_specs=(pl.BlockSpec((None, tm, F), lambda i, j: (j, i, 0)), pl.BlockSpec((None, tm, F), lambda i, j: (j, i, 0)),
                   tile, pl.BlockSpec((1, LANES), lambda i, j: (0, 0)), vec),
        scratch_shapes=[pltpu.VMEM((tm, D), F32), pltpu.VMEM((tm + PAD, F), F32)],
        compiler_params=_cparams())(up, up, up, up, cw, cw, cb, cb, wd, h1, tgt, g3)


def _ffn_bwd_a_call(dh2, ca, cb, wd, tm, name):
    n_pairs, T, F = ca.shape
    D = dh2.shape[1]

    def body(dh2_ref, ca_ref, cb_ref, wd_ref, dua_ref, dub_ref, dwd_ref, dcba_ref, dcbb_ref):
        @pl.when(pl.program_id(1) == 0)
        def _():
            for r in (dwd_ref, dcba_ref, dcbb_ref):
                r[...] = jnp.zeros_like(r)

        dh2 = dh2_ref[...]
        ua, ub = ca_ref[...].astype(F32), cb_ref[...].astype(F32)
        sa = jax.nn.sigmoid(ua)
        gate = ua * sa
        dact = _dot_nt(dh2, wd_ref[...])
        dwd_ref[...] += _dot_tn(gate * ub, dh2)
        dub = dact * gate
        dua = dact * ub * sa * (1.0 + ua * (1.0 - sa))
        dcba_ref[...] += jnp.sum(dua, axis=0, keepdims=True)
        dcbb_ref[...] += jnp.sum(dub, axis=0, keepdims=True)
        dua_ref[...] = dua.astype(ACT)
        dub_ref[...] = dub.astype(ACT)

    blk = pl.BlockSpec((None, tm, F), lambda j, i: (j, i, 0))
    vec = pl.BlockSpec((None, 1, F), lambda j, i: (j, 0, 0))
    return pl.pallas_call(
        body, name=name,
        out_shape=(jax.ShapeDtypeStruct((n_pairs, T, F), ACT), jax.ShapeDtypeStruct((n_pairs, T, F), ACT),
                   jax.ShapeDtypeStruct((n_pairs, F, D), F32), jax.ShapeDtypeStruct((n_pairs, 1, F), F32),
                   jax.ShapeDtypeStruct((n_pairs, 1, F), F32)),
        grid=(n_pairs, T // tm),
        in_specs=[pl.BlockSpec((tm, D), lambda j, i: (i, 0)), blk, blk, pl.BlockSpec((None, F, D), lambda j, i: (j, 0, 0))],
        out_specs=(blk, blk, pl.BlockSpec((None, F, D), lambda j, i: (j, 0, 0)), vec, vec),
        compiler_params=_cparams())(dh2, ca, cb, wd)


def _ffn_bwd_b_call(dua, dub, up, cw, wup, h1, g2, dh2, tm, tps, name):
    n_pairs, T, F = dua.shape
    D = h1.shape[1]
    hb = tm // PAD
    last = T // PAD - 1

    def body(da_ref, na_ref, db_ref, nb_ref, ua_ref, ub_ref, cwa_ref, cwb_ref, wa_ref, wb_ref, h1_ref, g2_ref, dh2_ref,
             dupa_ref, dupb_ref, dh1_ref, dg2_ref, dcwa_ref, dcwb_ref, acc_ref, ext_ref):
        i, j = pl.program_id(0), pl.program_id(1)
        edge = (i % tps) == tps - 1

        @pl.when((i == 0) & (j == 0))
        def _():
            dcwa_ref[...] = jnp.zeros_like(dcwa_ref)
            dcwb_ref[...] = jnp.zeros_like(dcwb_ref)

        outs = []
        for d_ref, n_ref, u_ref, cw_ref, o_ref, dcw_ref in (
                (da_ref, na_ref, ua_ref, cwa_ref, dupa_ref, dcwa_ref),
                (db_ref, nb_ref, ub_ref, cwb_ref, dupb_ref, dcwb_ref)):
            t = _conv_taps(d_ref, n_ref, ext_ref, edge, tm, False)
            cwv = cw_ref[...]
            dup = cwv[2:3] * t[0] + cwv[1:2] * t[1] + cwv[0:1] * t[2]
            o_ref[...] = dup.astype(ACT)
            outs.append(dup)
            u = u_ref[...].astype(F32)
            dcw_ref[j] += jnp.concatenate([jnp.sum(u * t[2 - k], axis=0, keepdims=True) for k in range(3)], axis=0)
        contrib = _dot_nt(outs[0], wa_ref[...]) + _dot_nt(outs[1], wb_ref[...])

        @pl.when(j == 0)
        def _():
            acc_ref[...] = contrib

        @pl.when(j > 0)
        def _():
            acc_ref[...] += contrib

        @pl.when((i == 0) & (j == 0))
        def _():
            dg2_ref[...] = jnp.zeros_like(dg2_ref)

        @pl.when(j == n_pairs - 1)
        def _():
            _, vjp = jax.vjp(_rms, h1_ref[...], g2_ref[...])
            dh, dg = vjp(acc_ref[...])
            dh1_ref[...] = dh2_ref[...] + dh
            dg2_ref[...] += dg

    tile = pl.BlockSpec((tm, D), lambda i, j: (i, 0))
    vec = pl.BlockSpec((1, D), lambda i, j: (0, 0))
    pair = lambda: [pl.BlockSpec((None, tm, F), lambda i, j: (j, i, 0)),
                    pl.BlockSpec((None, PAD, F), lambda i, j: (j, jnp.minimum((i + 1) * hb, last), 0))]
    act = jax.ShapeDtypeStruct((n_pairs, T, F), ACT)
    dcw = jax.ShapeDtypeStruct((n_pairs, 3, F), F32)
    dcw_spec = pl.BlockSpec((n_pairs, 3, F), lambda i, j: (0, 0, 0))
    return pl.pallas_call(
        body, name=name,
        out_shape=(act, act, jax.ShapeDtypeStruct((T, D), F32), jax.ShapeDtypeStruct((1, D), F32), dcw, dcw),
        grid=(T // tm, n_pairs),
        in_specs=pair() + pair() + [
            pl.BlockSpec((None, tm, F), lambda i, j: (j, i, 0)), pl.BlockSpec((None, tm, F), lambda i, j: (j + n_pairs, i, 0)),
            pl.BlockSpec((None, 3, F), lambda i, j: (j, 0, 0)), pl.BlockSpec((None, 3, F), lambda i, j: (j + n_pairs, 0, 0)),
            pl.BlockSpec((None, D, F), lambda i, j: (j, 0, 0)), pl.BlockSpec((None, D, F), lambda i, j: (j + n_pairs, 0, 0)),
            tile, vec, tile],
        out_specs=(pl.BlockSpec((None, tm, F), lambda i, j: (j, i, 0)), pl.BlockSpec((None, tm, F), lambda i, j: (j, i, 0)),
                   tile, vec, dcw_spec, dcw_spec),
        scratch_shapes=[pltpu.VMEM((tm, D), F32), pltpu.VMEM((tm + PAD, F), F32)],
        compiler_params=_cparams())(dua, dua, dub, dub, up, up, cw, cw, wup, wup, h1, g2, dh2)


def _in_bwd_call(dp, w_in, h0, g1, dh1, tm, name):
    T, D = h0.shape
    S, _, N = w_in.shape

    def body(dp_ref, w_ref, h0_ref, g1_ref, dh1_ref, dh0_ref, dg1_ref, acc_ref):
        i, j = pl.program_id(0), pl.program_id(1)
        contrib = _dot_nt(dp_ref[...], w_ref[...])

        @pl.when(j == 0)
        def _():
            acc_ref[...] = contrib

        @pl.when(j > 0)
        def _():
            acc_ref[...] += contrib

        @pl.when((i == 0) & (j == 0))
        def _():
            dg1_ref[...] = jnp.zeros_like(dg1_ref)

        @pl.when(j == S - 1)
        def _():
            _, vjp = jax.vjp(_rms, h0_ref[...], g1_ref[...])
            dh, dg = vjp(acc_ref[...])
            dh0_ref[...] = dh1_ref[...] + dh
            dg1_ref[...] += dg

    tile = pl.BlockSpec((tm, D), lambda i, j: (i, 0))
    vec = pl.BlockSpec((1, D), lambda i, j: (0, 0))
    return pl.pallas_call(
        body, name=name, out_shape=(jax.ShapeDtypeStruct((T, D), F32), jax.ShapeDtypeStruct((1, D), F32)),
        grid=(T // tm, S),
        in_specs=[pl.BlockSpec((tm, N), lambda i, j: (i, j)), pl.BlockSpec((None, D, N), lambda i, j: (j, 0, 0)),
                  tile, vec, tile],
        out_specs=(tile, vec), scratch_shapes=[pltpu.VMEM((tm, D), F32)],
        compiler_params=_cparams())(dp, w_in, h0, g1, dh1)


def _meta_grad_call(dh0_3, name):
    B, L, D = dh0_3.shape

    def body(d_ref, o_ref):
        o_ref[...] = jnp.sum(d_ref[...], axis=0)

    return pl.pallas_call(
        body, name=name, out_shape=jax.ShapeDtypeStruct((N_META, D), F32), grid=(1,),
        in_specs=[pl.BlockSpec((B, N_META, D), lambda i: (0, 0, 0))],
        out_specs=pl.BlockSpec((N_META, D), lambda i: (0, 0)), compiler_params=_cparams())(dh0_3)


_RELS = [(dx, dy, dc) for dx in (0, 1) for dy in (0, 1) for dc in (0, 1)][1:]


def _exchange_call(arrs, scatter, name):
    n = len(arrs)
    n_rel = len(_RELS)

    def body(*refs):
        ins, outs = refs[:n], refs[n:2 * n]
        send_sems, recv_sems, loc_sems = refs[2 * n:]
        x, y, c = lax.axis_index("x"), lax.axis_index("y"), lax.axis_index("c")
        me = 4 * x + 2 * y + c
        started = []
        for k in range(n):
            src_me = ins[k].at[me] if scatter else ins[k]
            loc = pltpu.make_async_copy(src_me, outs[k].at[me], loc_sems.at[k])
            loc.start()
            started.append(loc)
        waits = []
        for r, (dx, dy, dc) in enumerate(_RELS):
            px, py, pc = (x + dx) % 2, (y + dy) % 2, (c + dc) % 2
            pid = 4 * px + 2 * py + pc
            for k in range(n):
                s = k * n_rel + r
                src = ins[k].at[pid] if scatter else ins[k]
                cp = pltpu.make_async_remote_copy(
                    src_ref=src, dst_ref=outs[k].at[me], send_sem=send_sems.at[s], recv_sem=recv_sems.at[s],
                    device_id=(px, py, pc), device_id_type=pl.DeviceIdType.MESH)
                cp.start()
                waits.append(pltpu.make_async_remote_copy(
                    src_ref=src, dst_ref=outs[k].at[pid], send_sem=send_sems.at[s], recv_sem=recv_sems.at[s],
                    device_id=(px, py, pc), device_id_type=pl.DeviceIdType.MESH))
        for w in waits:
            w.wait_send()
            w.wait_recv()
        for loc in started:
            loc.wait()

    out_shape = tuple(jax.ShapeDtypeStruct(a.shape if scatter else (N_DEV,) + a.shape, a.dtype) for a in arrs)
    hbm = pl.BlockSpec(memory_space=pl.ANY)
    return pl.pallas_call(
        body, name=name, out_shape=out_shape, in_specs=[hbm] * n, out_specs=tuple([hbm] * n),
        scratch_shapes=[pltpu.SemaphoreType.DMA((n * n_rel,)), pltpu.SemaphoreType.DMA((n * n_rel,)),
                        pltpu.SemaphoreType.DMA((n,))],
        compiler_params=pltpu.CompilerParams(has_side_effects=True))(*arrs)


_HBM = pl.BlockSpec(memory_space=pltpu.HBM)
_SEM = pl.BlockSpec(memory_space=pltpu.SEMAPHORE)
_DATAFLOW = pltpu.SideEffectType.DATAFLOW_SIDE_EFFECTING


def _peer_copies(ins, lands, send_sems, recv_sems, scatter):
    n = len(ins)
    x, y, c = lax.axis_index("x"), lax.axis_index("y"), lax.axis_index("c")
    me = 4 * x + 2 * y + c
    sends, arrivals = [], []
    for r, (dx, dy, dc) in enumerate(_RELS):
        px, py, pc = (x + dx) % 2, (y + dy) % 2, (c + dc) % 2
        pid = 4 * px + 2 * py + pc
        for k in range(n):
            s = k * len(_RELS) + r
            src = ins[k].at[pid] if scatter else ins[k]
            for dst, out in ((lands[k].at[me], sends), (lands[k].at[pid], arrivals)):
                out.append(pltpu.make_async_remote_copy(
                    src_ref=src, dst_ref=dst, send_sem=send_sems.at[s], recv_sem=recv_sems.at[s],
                    device_id=(px, py, pc), device_id_type=pl.DeviceIdType.MESH))
    return sends, arrivals


def _exchange_start(arrs, scatter, name):
    n = len(arrs)
    n_sem = n * len(_RELS)

    def body(*refs):
        ins, lands = refs[:n], refs[n:2 * n]
        send_sems, recv_sems = refs[2 * n], refs[2 * n + 1]
        token = refs[-1]
        sends, _ = _peer_copies(ins, lands, send_sems, recv_sems, scatter)
        for cp in sends:
            cp.start()
        token[...] = jnp.zeros_like(token)

    land_shapes = [a.shape if scatter else (N_DEV,) + a.shape for a in arrs]
    ops = [pltpu.with_memory_space_constraint(a, pltpu.HBM) for a in arrs]
    ops += [pltpu.with_memory_space_constraint(lax.empty(s, a.dtype), pltpu.HBM) for s, a in zip(land_shapes, arrs)]
    out = pl.pallas_call(
        body, name=name,
        out_shape=(pltpu.SemaphoreType.DMA((n_sem,)), pltpu.SemaphoreType.DMA((n_sem,)),
                   *[pltpu.HBM(a.shape, a.dtype) for a in arrs],
                   *[pltpu.HBM(s, a.dtype) for s, a in zip(land_shapes, arrs)],
                   jax.ShapeDtypeStruct((SUBLANES, LANES), F32)),
        in_specs=[_HBM] * (2 * n),
        out_specs=(_SEM, _SEM, *[_HBM] * (2 * n), pl.BlockSpec(memory_space=pltpu.VMEM)),
        input_output_aliases={i: 2 + i for i in range(2 * n)},
        compiler_params=pltpu.CompilerParams(has_side_effects=_DATAFLOW))(*ops)
    return out[0], out[1], list(out[2:2 + n]), list(out[2 + n:2 + 2 * n]), out[-1]


def _exchange_wait(started, after, scatter, name):
    send_sems, recv_sems, srcs, lands, _ = started
    n = len(srcs)

    def body(*refs):
        ins, lands_ = refs[:n], refs[n:2 * n]
        _, arrivals = _peer_copies(ins, lands_, refs[2 * n], refs[2 * n + 1], scatter)
        for cp in arrivals:
            cp.wait_send()
            cp.wait_recv()

    out = pl.pallas_call(
        body, name=name,
        out_shape=(*[pltpu.HBM(a.shape, a.dtype) for a in srcs], *[pltpu.HBM(a.shape, a.dtype) for a in lands]),
        in_specs=[_HBM] * (2 * n) + [_SEM, _SEM, pl.BlockSpec(memory_space=pl.ANY)],
        out_specs=tuple([_HBM] * (2 * n)), input_output_aliases={i: i for i in range(2 * n)},
        compiler_params=pltpu.CompilerParams(has_side_effects=_DATAFLOW))(*srcs, *lands, send_sems, recv_sems, after)
    return list(out[:n]), list(out[n:])


def _place_own_call(srcs, lands, scatter, me, name):
    outs = []
    for k, (src, land) in enumerate(zip(srcs, lands)):
        R, C = land.shape[1:]
        tr = R
        while tr % 32 == 0 and tr * C * land.dtype.itemsize > 2 * 1024 * 1024:
            tr //= 2

        def body(me_ref, s_ref, l_ref, o_ref):
            o_ref[...] = s_ref[...]

        src_spec = (pl.BlockSpec((None, tr, C), lambda i, me_ref: (me_ref[0], i, 0)) if scatter
                    else pl.BlockSpec((tr, C), lambda i, me_ref: (i, 0)))
        outs.append(pl.pallas_call(
            body, name=f"{name}_{k}", out_shape=jax.ShapeDtypeStruct(land.shape, land.dtype),
            grid_spec=pltpu.PrefetchScalarGridSpec(
                num_scalar_prefetch=1, grid=(R // tr,),
                in_specs=[src_spec, pl.BlockSpec(memory_space=pl.ANY)],
                out_specs=pl.BlockSpec((None, tr, C), lambda i, me_ref: (me_ref[0], i, 0))),
            input_output_aliases={2: 0}, compiler_params=_cparams())(me, src, land))
    return outs


def _adamw_shard_call(w, parts, m, v, name):
    R, C = w.shape
    tr = _tile(R, 128) if R % 16 == 0 else R

    def body(w_ref, p_ref, m_ref, v_ref, g_ref, d_ref, nm_ref, nv_ref):
        g = p_ref[0].astype(F32)
        for s in range(1, N_DEV):
            g = g + p_ref[s].astype(F32)
        d, nm, nv = _adamw(w_ref[...], g, m_ref[...], v_ref[...])
        g_ref[...] = g
        d_ref[...] = d
        nm_ref[...] = nm
        nv_ref[...] = nv

    tile = pl.BlockSpec((tr, C), lambda i: (i, 0))
    sh = jax.ShapeDtypeStruct((R, C), F32)
    return pl.pallas_call(
        body, name=name, out_shape=(sh, sh, sh, sh), grid=(R // tr,),
        in_specs=[tile, pl.BlockSpec((N_DEV, tr, C), lambda i: (0, i, 0)), tile, tile],
        out_specs=(tile, tile, tile, tile), compiler_params=_cparams())(w, parts, m, v)


def _pack(arrs, rows_mult=SUBLANES):
    flat = jnp.concatenate([a.reshape(-1).astype(F32) for a in arrs])
    n = flat.shape[0]
    per = rows_mult * LANES
    total = -(-n // per) * per
    return jnp.pad(flat, (0, total - n)).reshape(total // LANES, LANES)


def _unpack(pack, shapes):
    flat = pack.reshape(-1)
    out, off = [], 0
    for s in shapes:
        n = 1
        for d in s:
            n *= d
        out.append(flat[off:off + n].reshape(s))
        off += n
    return out


def kernel(x, meta_tokens, mix_norm_g, w_in, ssm_lambda_re, ssm_lambda_im, ssm_log_dt, ssm_b_re, ssm_b_im, ssm_c_re, ssm_c_im, ssm_d, ssm_w_glu, w_ssm_proj, hgrn_lb_logits, hgrn_norm_g, w_hgrn_proj, w_out, ffn_norm_g, w_up, conv_w, conv_b, w_down, final_norm_g, loss_target, m_meta_tokens, m_mix_norm_g, m_w_in, m_ssm_lambda_re, m_ssm_lambda_im, m_ssm_log_dt, m_ssm_b_re, m_ssm_b_im, m_ssm_c_re, m_ssm_c_im, m_ssm_d, m_ssm_w_glu, m_w_ssm_proj, m_hgrn_lb_logits, m_hgrn_norm_g, m_w_hgrn_proj, m_w_out, m_ffn_norm_g, m_w_up, m_conv_w, m_conv_b, m_w_down, m_final_norm_g, v_meta_tokens, v_mix_norm_g, v_w_in, v_ssm_lambda_re, v_ssm_lambda_im, v_ssm_log_dt, v_ssm_b_re, v_ssm_b_im, v_ssm_c_re, v_ssm_c_im, v_ssm_d, v_ssm_w_glu, v_w_ssm_proj, v_hgrn_lb_logits, v_hgrn_norm_g, v_w_hgrn_proj, v_w_out, v_ffn_norm_g, v_w_up, v_conv_w, v_conv_b, v_w_down, v_final_norm_g):
    args = dict(locals())
    B, S_len, D = x.shape
    L = S_len + N_META
    T = B * L
    tm = _tile(L, ROW_TILE_CAP)
    tps = L // tm
    G, P = ssm_lambda_re.shape[1:]
    H = ssm_b_re.shape[-1]
    W = G * H
    n_cb = W // LANES
    gpb = G // n_cb
    hd = hgrn_norm_g.shape[1]
    n_heads = D // hd
    n_in = w_in.shape[2]
    F = w_up.shape[2]
    assert W == D and n_in % LANES == 0

    me = (4 * lax.axis_index("x") + 2 * lax.axis_index("y") + lax.axis_index("c")).astype(jnp.int32).reshape(1)
    meta_g, cw_g = _exchange_call([meta_tokens, conv_w[0]], False, "gather_small_params")
    ga = _exchange_start([w_in[0].astype(MXU)], False, "gather_a_start")
    gb = _exchange_start(
        [w_up[0].astype(MXU), ssm_w_glu[0].astype(MXU), w_ssm_proj[0].astype(MXU), w_hgrn_proj[0].astype(MXU),
         w_out[0].astype(MXU), w_down[0].astype(MXU)], False, "gather_b_start")
    started_tok = (ga[4] + gb[4])[0:1, 0:1]
    meta_full = meta_g.transpose(1, 0, 2).reshape(N_META, D)
    cb_g = conv_b.reshape(N_DEV, 1, F)

    h0 = jnp.concatenate([jnp.broadcast_to(meta_full[None], (B, N_META, D)), x], axis=1).reshape(T, D)
    tgt = jnp.concatenate([jnp.zeros((B, N_META, D), F32), loss_target], axis=1).reshape(T, D)

    lr, li = ssm_lambda_re[0], ssm_lambda_im[0]
    ldt = ssm_log_dt[0].reshape(G, 1)
    bt_re = ssm_b_re[0].transpose(2, 0, 1).reshape(H, G * P)
    bt_im = ssm_b_im[0].transpose(2, 0, 1).reshape(H, G * P)
    seg = _seg_len(L)
    a_re, a_im, as_re, as_im, coef_re, coef_im = _small_call(
        _disc_a_power(seg), [lr, li, ldt], [((G, P), F32)] * 6, "s5_discretise")
    bbt_re, bbt_im = _small_call(
        _disc_b, [coef_re.reshape(1, G * P), coef_im.reshape(1, G * P), bt_re, bt_im],
        [((H, G * P), F32)] * 2, "s5_input_matrix")
    eye = jnp.eye(gpb, dtype=F32)
    hw = gpb * P

    def expand_b(bbt):
        t = bbt.reshape(H, n_cb, gpb, P).transpose(1, 0, 2, 3)[:, None]
        return (t * eye[None, :, None, :, None]).reshape(n_cb, gpb * H, hw)

    def expand_c(cm):
        t = cm.reshape(n_cb, gpb, H, P).transpose(0, 1, 3, 2)[:, :, :, None]
        return (t * eye[None, :, None, :, None]).reshape(n_cb, hw, gpb * H)

    wb = jnp.concatenate([expand_b(bbt_re), expand_b(bbt_im)], axis=2).astype(MXU)
    wc = jnp.concatenate([expand_c(ssm_c_re[0]), -expand_c(ssm_c_im[0])], axis=1).astype(MXU)
    tab = jnp.stack([jnp.concatenate([a_re.reshape(n_cb, hw), a_im.reshape(n_cb, hw)], axis=1),
                     jnp.concatenate([as_re.reshape(n_cb, hw), as_im.reshape(n_cb, hw)], axis=1)], axis=1)
    tab = jnp.broadcast_to(tab[:, :, None, :], (n_cb, 2, SUBLANES, 2 * hw))
    dsk = ssm_d.reshape(n_cb, 1, LANES)
    lb = _small_call(_lb_fn, [hgrn_lb_logits], [((1, D), F32)], "hgrn_lower_bound")[0]

    z1 = _norm_call(h0, mix_norm_g + started_tok, tm, "mix_norm")
    ready = jnp.concatenate([t[(0,) * (t.ndim - 1)][0:1].astype(F32) for t in (z1, wb, wc, tab, lb, tgt)])
    ga_src, ga_land = _exchange_wait(ga, ready, False, "gather_a_wait")
    win_g = _place_own_call(ga_src, ga_land, False, me, "gather_a_own")[0]
    p = _mm_shard(z1, win_g, tm, "in_proj", False)
    p3 = p.reshape(B, L, p.shape[1])
    u_seg = _to_segments(p3[:, :, :W], seg)
    ya_seg, s_all = _s5_fwd_call(u_seg, wb, wc, tab, dsk, "s5_fwd")
    ya = _from_segments(ya_seg, seg, L).reshape(T, W)
    gb_src, gb_land = _exchange_wait(gb, ya, False, "gather_b_wait")
    gathered = _place_own_call(gb_src, gb_land, False, me, "gather_b_own")
    wup_g = gathered[0]
    wglu_g, wsp_g, whp_g, wout_g = [g.reshape(D, D) for g in gathered[1:5]]
    wdn_g = gathered[5].reshape(N_DEV // 2, 2 * w_down.shape[1], D)
    yo, a_br = _glu_proj_call(ya, wglu_g, wsp_g, tm, "s5_glu_proj")
    yb = _hgrn_fwd_call(p3, lb, hgrn_norm_g, n_heads, n_cb, "hgrn_fwd").reshape(T, D)
    col_ga = 5
    h1, mg, bm, z2 = _merge_call(yb, a_br, p, h0, whp_g, wout_g, ffn_norm_g, col_ga, tm, "merge")
    up = _mm_shard(z2, wup_g, tm, "up_proj", True)
    conv_a, conv_b_out, dh2, loss_part, dg3 = _ffn_fwd_call(up, cw_g, cb_g, wdn_g, h1, tgt, final_norm_g.reshape(1, D),
                                                            tm, tps, "ffn_out_loss")

    dua, dub, dwd, dcba, dcbb = _ffn_bwd_a_call(dh2, conv_a, conv_b_out, wdn_g, tm, "ffn_bwd_gate")
    dupa, dupb, dh1, dg2, dcwa, dcwb = _ffn_bwd_b_call(dua, dub, up, cw_g, wup_g, h1, ffn_norm_g, dh2, tm, tps,
                                                       "ffn_bwd_up")
    dwup = jnp.concatenate([_mm_tn(z2, dupa, N_DEV // 2, tm, "dw_up_a", True),
                            _mm_tn(z2, dupb, N_DEV // 2, tm, "dw_up_b", True)], axis=0)
    sh_rows = D // N_DEV
    sa = _exchange_start([dwup.astype(WIRE), dwd.reshape(N_DEV, w_down.shape[1], D).astype(WIRE)], True,
                         "scatter_a_start")
    dmg, dwout = _lin_bwd(mg, dh1, wout_g + sa[4][0:1, 0:1].astype(MXU), tm, "out_proj_bwd")
    da_br, dbm, dga, dgb = _merge_bwd_call(dmg, a_br, bm, p, col_ga, tm, "merge_bwd")
    dyo, dwsp = _lin_bwd(yo, da_br, wsp_g, tm, "ssm_proj_bwd")
    dyb, dwhp = _lin_bwd(yb, dbm, whp_g, tm, "hgrn_proj_bwd")
    dya, dwglu = _glu_bwd_call(ya, dyo, wglu_g, tm, "s5_glu_bwd")
    sb = _exchange_start([t.reshape(N_DEV, sh_rows, D).astype(WIRE) for t in (dwglu, dwsp, dwhp, dwout)], True,
                         "scatter_b_start")
    tok_b = sb[4][0:1, :]
    du_seg, dwb, dwc, dab, ddsk = _s5_bwd_call(u_seg, s_all, _to_segments(dya.reshape(B, L, W), seg), wb, wc, tab,
                                               dsk + tok_b[None], "s5_bwd")
    du = _from_segments(du_seg, seg, L)

    def diag_b(dw):
        t = (dw.reshape(n_cb, gpb, H, gpb, P) * eye[None, :, None, :, None]).sum(axis=1)
        return t.transpose(1, 0, 2, 3).reshape(H, G * P)

    def diag_c(dw):
        t = (dw.reshape(n_cb, gpb, P, gpb, H) * eye[None, :, None, :, None]).sum(axis=3)
        return t.transpose(0, 1, 3, 2).reshape(G, H, P)

    early_parts = [dab[:, 0, :hw].reshape(G, P), dab[:, 0, hw:].reshape(G, P),
                   diag_b(dwb[:, :, :hw]), diag_b(dwb[:, :, hw:]),
                   diag_c(dwc[:, :hw]), -diag_c(dwc[:, hw:]), ddsk.reshape(1, D)]
    early_pack = _pack(early_parts)
    se = _exchange_start([early_pack], False, "gather_s5_grads_start")
    dq, dfl, di, dog, dlb, dng = _hgrn_bwd_call(p3, dyb.reshape(B, L, D), lb, hgrn_norm_g + tok_b + se[4][0:1, :],
                                                n_heads, n_cb, "hgrn_bwd")
    dp = jnp.concatenate([du.reshape(T, W), dq.reshape(T, D), dfl.reshape(T, D), di.reshape(T, D),
                          dog.reshape(T, D), dga, dgb], axis=1)
    dwin = _mm_tn(z1, dp, N_DEV, tm, "dw_in", False)
    sc = _exchange_start([dwin.astype(WIRE)], True, "scatter_c_start")
    dh0, dg1 = _in_bwd_call(dp, win_g, h0, mix_norm_g + sc[4][0:1, 0:1], dh1, tm, "in_proj_bwd")
    dh0_3 = dh0.reshape(B, L, D)
    grad_x = dh0_3[:, N_META:]
    dmeta = _meta_grad_call(dh0_3, "meta_grad")

    late_parts = [dg1, dlb, dng, dg2, jnp.concatenate([dcba, dcbb], axis=0).reshape(1, N_DEV * F), dg3, loss_part]
    late_pack = _pack(late_parts)

    dcw = jnp.concatenate([dcwa, dcwb], axis=0)
    dmeta_s = dmeta.reshape(N_META, N_DEV, D // N_DEV).transpose(1, 0, 2)
    parts_d = _exchange_call([dmeta_s, dcw], True, "scatter_small_grads")
    late_all = _exchange_call([late_pack], False, "gather_small_grads")[0]
    early_all = _place_own_call(*_exchange_wait(se, late_all, False, "gather_s5_grads_wait"), False, me,
                                "gather_s5_grads_own")[0]
    parts_a = _place_own_call(*_exchange_wait(sa, late_all, True, "scatter_a_wait"), True, me, "scatter_a_own")
    parts_b = _place_own_call(*_exchange_wait(sb, late_all, True, "scatter_b_wait"), True, me, "scatter_b_own")
    parts_c = _place_own_call(*_exchange_wait(sc, late_all, True, "scatter_c_wait"), True, me, "scatter_c_own")
    parts = [parts_c[0], parts_a[0], *parts_b, parts_a[1], parts_d[0], parts_d[1]]

    def sum8(a, b):
        ta, tb = a[0], b[0]
        for s in range(1, N_DEV):
            ta, tb = ta + a[s], tb + b[s]
        return ta, tb

    early_sum, late_sum = _small_call(sum8, [early_all, late_all], [(early_pack.shape, F32), (late_pack.shape, F32)],
                                      "sum_small_grads")
    t_abr, t_abi, t_bbr, t_bbi, g_cre, g_cim, g_dsk = _unpack(early_sum, [a.shape for a in early_parts])
    g_g1, t_lb, g_ng, g_g2, g_cb, g_g3, loss_v = _unpack(late_sum, [a.shape for a in late_parts])

    def disc_b_bwd(cr, ci, br, bi, dbr, dbi):
        _, vjp = jax.vjp(_disc_b, cr, ci, br, bi)
        return vjp((dbr, dbi))

    t_cr, t_ci, g_btr, g_bti = _small_call(
        disc_b_bwd, [coef_re.reshape(1, G * P), coef_im.reshape(1, G * P), bt_re, bt_im, t_bbr, t_bbi],
        [((1, G * P), F32)] * 2 + [((H, G * P), F32)] * 2, "s5_input_matrix_bwd")

    def disc_a_bwd(lr_, li_, ldt_, dar, dai, dcr, dci):
        _, vjp = jax.vjp(_disc_a, lr_, li_, ldt_)
        return vjp((dar, dai, dcr, dci))

    g_lr, g_li, g_ldt = _small_call(
        disc_a_bwd, [lr, li, ldt, t_abr, t_abi, t_cr.reshape(G, P), t_ci.reshape(G, P)],
        [((G, P), F32)] * 2 + [((G, 1), F32)], "s5_discretise_bwd")

    def lb_bwd(logits, d):
        _, vjp = jax.vjp(_lb_fn, logits)
        return vjp(d)

    g_lbl = _small_call(lb_bwd, [hgrn_lb_logits, t_lb], [(hgrn_lb_logits.shape, F32)], "hgrn_lower_bound_bwd")[0]

    grads = dict(
        mix_norm_g=g_g1, ssm_lambda_re=g_lr[None], ssm_lambda_im=g_li[None], ssm_log_dt=g_ldt.reshape(1, G),
        ssm_b_re=g_btr.reshape(H, G, P).transpose(1, 2, 0)[None], ssm_b_im=g_bti.reshape(H, G, P).transpose(1, 2, 0)[None],
        ssm_c_re=g_cre[None], ssm_c_im=g_cim[None], ssm_d=g_dsk, hgrn_lb_logits=g_lbl, hgrn_norm_g=g_ng,
        ffn_norm_g=g_g2, conv_b=g_cb.reshape(1, N_DEV * F), final_norm_g=g_g3.reshape(D))
    loss = loss_v[0, 0]

    delta, new_m, new_v = {}, {}, {}
    sharded = [("w_in", parts[0], (D, n_in)), ("w_up", parts[1], (D, F)), ("ssm_w_glu", parts[2], (sh_rows, D)),
               ("w_ssm_proj", parts[3], (sh_rows, D)), ("w_hgrn_proj", parts[4], (sh_rows, D)),
               ("w_out", parts[5], (sh_rows, D)), ("w_down", parts[6], (w_down.shape[1], D)),
               ("meta_tokens", parts[7], (N_META, D // N_DEV)), ("conv_w", parts[8], (3, F))]
    for name, part, shp in sharded:
        full = args[name].shape
        g, d_, nm, nv = _adamw_shard_call(args[name].reshape(shp), part, args["m_" + name].reshape(shp),
                                          args["v_" + name].reshape(shp), "adamw_" + name)
        grads[name], delta[name], new_m[name], new_v[name] = [t.reshape(full) for t in (g, d_, nm, nv)]

    rep = ["mix_norm_g", "ssm_lambda_re", "ssm_lambda_im", "ssm_log_dt", "ssm_b_re", "ssm_b_im", "ssm_c_re",
           "ssm_c_im", "ssm_d", "hgrn_lb_logits", "hgrn_norm_g", "ffn_norm_g", "conv_b", "final_norm_g"]
    rep_shapes = [args[n].shape for n in rep]
    packs = [_pack([args[pre + n] for n in rep]) for pre in ("", "m_", "v_")]
    g_pack = _pack([grads[n] for n in rep])
    outs = _small_call(lambda w, g, m, v: _adamw(w, g, m, v), [packs[0], g_pack, packs[1], packs[2]],
                       [(g_pack.shape, F32)] * 3, "adamw_replicated")
    for n, d_, nm, nv in zip(rep, *[_unpack(o, rep_shapes) for o in outs]):
        delta[n], new_m[n], new_v[n] = d_, nm, nv

    names = ["meta_tokens", "mix_norm_g", "w_in", "ssm_lambda_re", "ssm_lambda_im", "ssm_log_dt", "ssm_b_re",
             "ssm_b_im", "ssm_c_re", "ssm_c_im", "ssm_d", "ssm_w_glu", "w_ssm_proj", "hgrn_lb_logits", "hgrn_norm_g",
             "w_hgrn_proj", "w_out", "ffn_norm_g", "w_up", "conv_w", "conv_b", "w_down", "final_norm_g"]
    return (loss, grad_x, *[grads[n] for n in names], *[delta[n] for n in names],
            *[new_m[n] for n in names], *[new_v[n] for n in names])
```

```python
import functools

import jax
import jax.numpy as jnp
from jax import lax
from jax.experimental import pallas as pl
from jax.experimental.pallas import tpu as pltpu

F32 = jnp.float32
MXU = jnp.bfloat16
ACT = jnp.bfloat16
WIRE = jnp.bfloat16
N_DEV = 8
N_META = 16
CHUNK = 16
EPS = 1e-6
ADAM_LR, ADAM_B1, ADAM_B2, ADAM_EPS, ADAM_WD, ADAM_STEP = 0.001, 0.9, 0.999, 1e-08, 0.01, 10
SUBLANES = 8
LANES = 128
ROW_TILE_CAP = 700
VMEM_LIMIT = 60 * 1024 * 1024


def _cparams(**kw):
    return pltpu.CompilerParams(vmem_limit_bytes=VMEM_LIMIT, **kw)


def _tile(n, cap):
    best = None
    for t in range(16, min(n, cap) + 1, 16):
        if n % t == 0:
            best = t
    assert best is not None, (n, cap)
    return best


def _dot(a, b):
    return lax.dot_general(a.astype(MXU), b.astype(MXU), (((1,), (0,)), ((), ())), preferred_element_type=F32)


def _dot_nt(a, b):
    return lax.dot_general(a.astype(MXU), b.astype(MXU), (((1,), (1,)), ((), ())), preferred_element_type=F32)


def _dot_tn(a, b):
    return lax.dot_general(a.astype(MXU), b.astype(MXU), (((0,), (0,)), ((), ())), preferred_element_type=F32)


def _rms(x, g):
    return x * lax.rsqrt(jnp.mean(x * x, axis=-1, keepdims=True) + EPS) * g


def _silu(x):
    return x * jax.nn.sigmoid(x)


def _small_call(fn, ins, out_shapes, name):
    n_in = len(ins)

    def body(*refs):
        outs = fn(*[r[...] for r in refs[:n_in]])
        outs = outs if isinstance(outs, (tuple, list)) else (outs,)
        for r, o in zip(refs[n_in:], outs):
            r[...] = o.astype(r.dtype)

    vm = pl.BlockSpec(memory_space=pltpu.VMEM)
    return pl.pallas_call(
        body, name=name, out_shape=tuple(jax.ShapeDtypeStruct(s, d) for s, d in out_shapes),
        in_specs=[vm] * n_in, out_specs=tuple([vm] * len(out_shapes)), compiler_params=_cparams())(*ins)


def _disc_a(lr, li, ldt):
    dt = jnp.exp(ldt)
    mag = jnp.exp(lr * dt)
    ab_re = mag * jnp.cos(li * dt)
    ab_im = mag * jnp.sin(li * dt)
    den = lr * lr + li * li
    nr = ab_re - 1.0
    coef_re = (nr * lr + ab_im * li) / den
    coef_im = (ab_im * lr - nr * li) / den
    return ab_re, ab_im, coef_re, coef_im


def _disc_a_power(n):
    def fn(lr, li, ldt):
        ab_re, ab_im, coef_re, coef_im = _disc_a(lr, li, ldt)
        pr, pi, sr, si, m = None, None, ab_re, ab_im, n
        while m:
            if m & 1:
                pr, pi = (sr, si) if pr is None else (pr * sr - pi * si, pr * si + pi * sr)
            m >>= 1
            if m:
                sr, si = sr * sr - si * si, 2.0 * sr * si
        return ab_re, ab_im, pr, pi, coef_re, coef_im
    return fn


def _disc_b(coef_re, coef_im, bt_re, bt_im):
    return coef_re * bt_re - coef_im * bt_im, coef_re * bt_im + coef_im * bt_re


def _lb_fn(logits):
    return jax.nn.softmax(logits, axis=0)[0:1]


def _adamw(w, g, m, v):
    m = ADAM_B1 * m + (1.0 - ADAM_B1) * g
    v = ADAM_B2 * v + (1.0 - ADAM_B2) * jnp.square(g)
    m_hat = m / (1.0 - ADAM_B1 ** ADAM_STEP)
    v_hat = v / (1.0 - ADAM_B2 ** ADAM_STEP)
    delta = -ADAM_LR * (m_hat / (jnp.sqrt(v_hat) + ADAM_EPS) + ADAM_WD * w)
    return delta, m, v


def _norm_call(h, g, tm, name):
    T, D = h.shape

    def body(h_ref, g_ref, z_ref):
        z_ref[...] = _rms(h_ref[...], g_ref[...]).astype(ACT)

    return pl.pallas_call(
        body, name=name, out_shape=jax.ShapeDtypeStruct((T, D), ACT), grid=(T // tm,),
        in_specs=[pl.BlockSpec((tm, D), lambda i: (i, 0)), pl.BlockSpec((1, D), lambda i: (0, 0))],
        out_specs=pl.BlockSpec((tm, D), lambda i: (i, 0)), compiler_params=_cparams())(h, g)


def _mm_shard(x, w, tm, name, major):
    T, K = x.shape
    S, _, N = w.shape

    def body(x_ref, w_ref, o_ref):
        o_ref[...] = _dot(x_ref[...], w_ref[...]).astype(o_ref.dtype)

    if major:
        out_shape = jax.ShapeDtypeStruct((S, T, N), ACT)
        out_spec = pl.BlockSpec((None, tm, N), lambda j, i: (j, i, 0))
    else:
        out_shape = jax.ShapeDtypeStruct((T, S * N), ACT)
        out_spec = pl.BlockSpec((tm, N), lambda j, i: (i, j))
    return pl.pallas_call(
        body, name=name, out_shape=out_shape, grid=(S, T // tm),
        in_specs=[pl.BlockSpec((tm, K), lambda j, i: (i, 0)), pl.BlockSpec((None, K, N), lambda j, i: (j, 0, 0))],
        out_specs=out_spec, compiler_params=_cparams())(x, w)


def _mm_tn(x, y, n_shards, tm, name, major):
    T, K = x.shape
    S = n_shards
    N = y.shape[-1] if major else y.shape[-1] // S

    def body(x_ref, y_ref, o_ref):
        @pl.when(pl.program_id(1) == 0)
        def _():
            o_ref[...] = jnp.zeros_like(o_ref)
        o_ref[...] += _dot_tn(x_ref[...], y_ref[...])

    y_spec = (pl.BlockSpec((None, tm, N), lambda j, i: (j, i, 0)) if major
              else pl.BlockSpec((tm, N), lambda j, i: (i, j)))
    return pl.pallas_call(
        body, name=name, out_shape=jax.ShapeDtypeStruct((S, K, N), F32), grid=(S, T // tm),
        in_specs=[pl.BlockSpec((tm, K), lambda j, i: (i, 0)), y_spec],
        out_specs=pl.BlockSpec((None, K, N), lambda j, i: (j, 0, 0)), compiler_params=_cparams())(x, y)


def _lin_bwd(x, dy, w, tm, name):
    T, K = x.shape
    N = dy.shape[1]

    def body(x_ref, dy_ref, w_ref, dx_ref, dw_ref):
        @pl.when(pl.program_id(0) == 0)
        def _():
            dw_ref[...] = jnp.zeros_like(dw_ref)
        dy = dy_ref[...]
        dx_ref[...] = _dot_nt(dy, w_ref[...]).astype(dx_ref.dtype)
        dw_ref[...] += _dot_tn(x_ref[...], dy)

    return pl.pallas_call(
        body, name=name,
        out_shape=(jax.ShapeDtypeStruct((T, K), ACT), jax.ShapeDtypeStruct((K, N), F32)), grid=(T // tm,),
        in_specs=[pl.BlockSpec((tm, K), lambda i: (i, 0)), pl.BlockSpec((tm, N), lambda i: (i, 0)),
                  pl.BlockSpec((K, N), lambda i: (0, 0))],
        out_specs=(pl.BlockSpec((tm, K), lambda i: (i, 0)), pl.BlockSpec((K, N), lambda i: (0, 0))),
        compiler_params=_cparams())(x, dy, w)


N_SEG = SUBLANES


def _seg_len(L):
    return -(-L // (N_SEG * SUBLANES)) * SUBLANES


def _to_segments(a3, seg):
    b, length, c = a3.shape
    a = jnp.pad(a3, ((0, 0), (0, N_SEG * seg - length), (0, 0)))
    return a.reshape(b, N_SEG, seg, c).transpose(0, 2, 1, 3).reshape(b, N_SEG * seg, c)


def _from_segments(a3, seg, length):
    b, _, c = a3.shape
    return a3.reshape(b, seg, N_SEG, c).transpose(0, 2, 1, 3).reshape(b, N_SEG * seg, c)[:, :length]


def _seg_scan(x_ref, tab_ref, n_slabs, reverse):
    hw = x_ref.shape[1] // 2
    sign = -1.0 if reverse else 1.0
    ar, ai = tab_ref[0][:, :hw], sign * tab_ref[0][:, hw:]
    br, bi = tab_ref[1][:, :hw], sign * tab_ref[1][:, hw:]

    def slab(k):
        kk = (n_slabs - 1 - k) if reverse else k
        return pl.ds(pl.multiple_of(kk * SUBLANES, SUBLANES), SUBLANES)

    def horner(k, carry):
        cr, ci = carry
        x = x_ref[slab(k), :]
        return ar * cr - ai * ci + x[:, :hw], ar * ci + ai * cr + x[:, hw:]

    z = jnp.zeros((SUBLANES, hw), F32)
    fr, fi = lax.fori_loop(0, n_slabs, horner, (z, z))

    row = lax.broadcasted_iota(jnp.int32, (SUBLANES, hw), 0)
    edge = (row == SUBLANES - 1) if reverse else (row == 0)
    shift = SUBLANES - 1 if reverse else 1
    sr, si = z, z
    for _ in range(N_SEG - 1):
        er, ei = fr + br * sr - bi * si, fi + br * si + bi * sr
        sr = jnp.where(edge, 0.0, pltpu.roll(er, shift, 0))
        si = jnp.where(edge, 0.0, pltpu.roll(ei, shift, 0))

    def scan(k, carry):
        cr, ci = carry
        rows = slab(k)
        x = x_ref[rows, :]
        nr, ni = ar * cr - ai * ci + x[:, :hw], ar * ci + ai * cr + x[:, hw:]
        x_ref[rows, 0:hw] = nr
        x_ref[rows, hw:2 * hw] = ni
        return nr, ni

    lax.fori_loop(0, n_slabs, scan, (sr, si))


def _s5_fwd_call(p3, wb, wc, tab_f, dsk, name):
    B, L, _ = p3.shape
    n_cb, cw, sw = wb.shape

    def body(u_ref, wb_ref, wc_ref, tab_ref, d_ref, ya_ref, so_ref, s_ref):
        u = u_ref[...]
        s_ref[...] = _dot(u, wb_ref[...])
        _seg_scan(s_ref, tab_ref, L // SUBLANES, False)
        s = s_ref[...].astype(MXU)
        so_ref[...] = s
        y = _dot(s, wc_ref[...]) + d_ref[...] * u.astype(F32)
        ya_ref[...] = jax.nn.gelu(y).astype(ACT)

    return pl.pallas_call(
        body, name=name,
        out_shape=(jax.ShapeDtypeStruct((B, L, n_cb * cw), ACT), jax.ShapeDtypeStruct((B, n_cb, L, sw), MXU)),
        grid=(B, n_cb),
        in_specs=[pl.BlockSpec((None, L, cw), lambda b, c: (b, 0, c)),
                  pl.BlockSpec((None, cw, sw), lambda b, c: (c, 0, 0)),
                  pl.BlockSpec((None, sw, cw), lambda b, c: (c, 0, 0)),
                  pl.BlockSpec((None, 2, SUBLANES, sw), lambda b, c: (c, 0, 0, 0)),
                  pl.BlockSpec((None, 1, cw), lambda b, c: (c, 0, 0))],
        out_specs=(pl.BlockSpec((None, L, cw), lambda b, c: (b, 0, c)),
                   pl.BlockSpec((None, None, L, sw), lambda b, c: (b, c, 0, 0))),
        scratch_shapes=[pltpu.VMEM((L, sw), F32)], compiler_params=_cparams())(p3, wb, wc, tab_f, dsk)


def _s5_bwd_call(p3, s_all, dya, wb, wc, tab_r, dsk, name):
    B, L, _ = p3.shape
    n_cb, cw, sw = wb.shape
    hw = sw // 2
    n_slabs = L // SUBLANES

    def body(u_ref, si_ref, dya_ref, wb_ref, wc_ref, tr_ref, d_ref,
             du_ref, dwb_ref, dwc_ref, da_ref, dd_ref, s_ref, l_ref):
        @pl.when(pl.program_id(1) == 0)
        def _():
            dwb_ref[...] = jnp.zeros_like(dwb_ref)
            dwc_ref[...] = jnp.zeros_like(dwc_ref)
            da_ref[...] = jnp.zeros_like(da_ref)
            dd_ref[...] = jnp.zeros_like(dd_ref)

        u = u_ref[...]
        uf = u.astype(F32)
        s_in = si_ref[...]
        s_ref[...] = s_in.astype(F32)
        y = _dot(s_in, wc_ref[...]) + d_ref[...] * uf
        _, gelu_vjp = jax.vjp(jax.nn.gelu, y)
        dy = gelu_vjp(dya_ref[...].astype(F32))[0]
        dd_ref[...] += jnp.sum(dy * uf, axis=0, keepdims=True)
        l_ref[...] = _dot_nt(dy, wc_ref[...])
        _seg_scan(l_ref, tr_ref, n_slabs, True)
        du_ref[...] = (_dot_nt(l_ref[...], wb_ref[...]) + d_ref[...] * dy).astype(ACT)
        dwb_ref[...] += _dot_tn(u, l_ref[...])
        dwc_ref[...] += _dot_tn(s_in, dy)

        row = lax.broadcasted_iota(jnp.int32, (SUBLANES, hw), 0)
        last = s_ref[pl.ds((n_slabs - 1) * SUBLANES, SUBLANES), :]
        p0r = jnp.where(row == 0, 0.0, pltpu.roll(last[:, :hw], 1, 0))
        p0i = jnp.where(row == 0, 0.0, pltpu.roll(last[:, hw:], 1, 0))

        def step(k, carry):
            qr, qi, accr, acci = carry
            r0 = pl.multiple_of(k * SUBLANES, SUBLANES)
            s = s_ref[pl.ds(r0, SUBLANES), :]
            lam = l_ref[pl.ds(r0, SUBLANES), :]
            lr, li = lam[:, :hw], lam[:, hw:]
            accr = accr + lr * qr + li * qi
            acci = acci + li * qr - lr * qi
            return s[:, :hw], s[:, hw:], accr, acci

        z8 = jnp.zeros((SUBLANES, hw), F32)
        _, _, accr, acci = lax.fori_loop(0, n_slabs, step, (p0r, p0i, z8, z8))
        da_ref[...] += jnp.concatenate([jnp.sum(accr, axis=0, keepdims=True),
                                        jnp.sum(acci, axis=0, keepdims=True)], axis=1)

    W = n_cb * cw
    return pl.pallas_call(
        body, name=name,
        out_shape=(jax.ShapeDtypeStruct((B, L, W), ACT), jax.ShapeDtypeStruct((n_cb, cw, sw), F32),
                   jax.ShapeDtypeStruct((n_cb, sw, cw), F32), jax.ShapeDtypeStruct((n_cb, 1, sw), F32),
                   jax.ShapeDtypeStruct((n_cb, 1, cw), F32)),
        grid=(n_cb, B),
        in_specs=[pl.BlockSpec((None, L, cw), lambda c, b: (b, 0, c)),
                  pl.BlockSpec((None, None, L, sw), lambda c, b: (b, c, 0, 0)),
                  pl.BlockSpec((None, L, cw), lambda c, b: (b, 0, c)),
                  pl.BlockSpec((None, cw, sw), lambda c, b: (c, 0, 0)),
                  pl.BlockSpec((None, sw, cw), lambda c, b: (c, 0, 0)),
                  pl.BlockSpec((None, 2, SUBLANES, sw), lambda c, b: (c, 0, 0, 0)),
                  pl.BlockSpec((None, 1, cw), lambda c, b: (c, 0, 0))],
        out_specs=(pl.BlockSpec((None, L, cw), lambda c, b: (b, 0, c)),
                   pl.BlockSpec((None, cw, sw), lambda c, b: (c, 0, 0)),
                   pl.BlockSpec((None, sw, cw), lambda c, b: (c, 0, 0)),
                   pl.BlockSpec((None, 1, sw), lambda c, b: (c, 0, 0)),
                   pl.BlockSpec((None, 1, cw), lambda c, b: (c, 0, 0))),
        scratch_shapes=[pltpu.VMEM((L, sw), F32), pltpu.VMEM((L, sw), F32)],
        compiler_params=_cparams())(p3, s_all, dya, wb, wc, tab_r, dsk)


def _glu_proj_call(ya, wglu, wproj, tm, name):
    T, W = ya.shape
    D = wproj.shape[1]

    def body(ya_ref, wg_ref, wp_ref, yo_ref, a_ref):
        ya = ya_ref[...]
        yo = ya.astype(F32) * jax.nn.sigmoid(_dot(ya, wg_ref[...]))
        yo_ref[...] = yo.astype(ACT)
        a_ref[...] = _dot(yo, wp_ref[...]).astype(ACT)

    return pl.pallas_call(
        body, name=name, out_shape=(jax.ShapeDtypeStruct((T, W), ACT), jax.ShapeDtypeStruct((T, D), ACT)),
        grid=(T // tm,),
        in_specs=[pl.BlockSpec((tm, W), lambda i: (i, 0)), pl.BlockSpec((W, W), lambda i: (0, 0)),
                  pl.BlockSpec((W, D), lambda i: (0, 0))],
        out_specs=(pl.BlockSpec((tm, W), lambda i: (i, 0)), pl.BlockSpec((tm, D), lambda i: (i, 0))),
        compiler_params=_cparams())(ya, wglu, wproj)


def _glu_bwd_call(ya, dyo, wglu, tm, name):
    T, W = ya.shape

    def body(ya_ref, dyo_ref, wg_ref, dya_ref, dwg_ref):
        @pl.when(pl.program_id(0) == 0)
        def _():
            dwg_ref[...] = jnp.zeros_like(dwg_ref)
        ya = ya_ref[...]
        yaf = ya.astype(F32)
        dyo = dyo_ref[...].astype(F32)
        sg = jax.nn.sigmoid(_dot(ya, wg_ref[...]))
        dt = dyo * yaf * sg * (1.0 - sg)
        dya_ref[...] = (dyo * sg + _dot_nt(dt, wg_ref[...])).astype(ACT)
        dwg_ref[...] += _dot_tn(ya, dt)

    return pl.pallas_call(
        body, name=name, out_shape=(jax.ShapeDtypeStruct((T, W), ACT), jax.ShapeDtypeStruct((W, W), F32)),
        grid=(T // tm,),
        in_specs=[pl.BlockSpec((tm, W), lambda i: (i, 0)), pl.BlockSpec((tm, W), lambda i: (i, 0)),
                  pl.BlockSpec((W, W), lambda i: (0, 0))],
        out_specs=(pl.BlockSpec((tm, W), lambda i: (i, 0)), pl.BlockSpec((W, W), lambda i: (0, 0))),
        compiler_params=_cparams())(ya, dyo, wglu)


PAD = 16


def _chunk_cumsums(x, pad_ref, L):
    row = lax.broadcasted_iota(jnp.int32, x.shape, 0) % CHUNK
    zeros = jnp.zeros((PAD, x.shape[1]), F32)
    pad_ref[0:PAD, :] = zeros
    pad_ref[PAD + L:2 * PAD + L, :] = zeros
    c = x
    r = x
    d = 1
    while d < CHUNK:
        pad_ref[PAD:PAD + L, :] = c
        c = c + jnp.where(row >= d, pad_ref[PAD - d:PAD - d + L, :], 0.0)
        pad_ref[PAD:PAD + L, :] = r
        r = r + jnp.where(row + d < CHUNK, pad_ref[PAD + d:PAD + d + L, :], 0.0)
        d *= 2
    return c, r - x


def _hgrn_prep(q_ref, fl_ref, lb_ref, pad_ref, r0, n):
    rows = pl.ds(r0, n)
    lb = lb_ref[...]
    sig = jax.nn.sigmoid(fl_ref[rows, :].astype(F32))
    f = lb + (1.0 - lb) * sig
    k = 1.0 - f
    c, rc = _chunk_cumsums(jnp.log(f), pad_ref, n)
    e_in, e_inv, e_out = jnp.exp(c), jnp.exp(-c), jnp.exp(rc)
    q = q_ref[rows, :].astype(F32)
    return dict(sig=sig, f=f, k=k, q=q, e_in=e_in, e_inv=e_inv, e_out=e_out, dec=jnp.exp(c + rc))


def _for_row_blocks(L, fn):
    full = L // GROUP
    if full:
        def step(g, carry):
            fn(pl.multiple_of(g * GROUP, GROUP), GROUP)
            return carry
        lax.fori_loop(0, full, step, 0)
    if L % GROUP:
        fn(full * GROUP, L % GROUP)


def _chunk_mask(rb):
    r = lax.broadcasted_iota(jnp.int32, (rb, rb), 0)
    c = lax.broadcasted_iota(jnp.int32, (rb, rb), 1)
    return (r // CHUNK == c // CHUNK) & (c <= r)


def _hg_out(o, og, g):
    on = o * lax.rsqrt(jnp.mean(o * o, axis=-1, keepdims=True) + EPS) * g
    return on * _silu(og)


def _hgrn_specs(L, hd, col_q, n_heads, order):
    def spec(sec):
        return pl.BlockSpec((None, L, hd), lambda *g: (order(*g)[0], 0, col_q + sec * n_heads + order(*g)[1]))
    return [spec(0), spec(1), spec(2), spec(3)]


GROUP = 128
CPG = GROUP // CHUNK


def _expand(x):
    xf = x.astype(F32)
    chunk = lax.broadcasted_iota(jnp.int32, xf.shape, 0) // CHUNK
    return jnp.concatenate([jnp.where(chunk == j, xf, 0.0) for j in range(CPG)], axis=1)


def _fill_tail(refs_fills, L):
    for ref, fill in refs_fills:
        if ref.shape[0] > L:
            ref[L:ref.shape[0], :] = jnp.full((ref.shape[0] - L, ref.shape[1]), fill, ref.dtype)


GROUP_UNROLL = 4


def _hgrn_forward_core(q_ref, fl_ref, v_ref, lb_ref, pad_ref, qin_ref, kin_ref, kout_ref, vp_ref, dec_ref, o_ref,
                       s_ref, a_ref, L, keep=()):
    hd = qin_ref.shape[1]
    n_groups = qin_ref.shape[0] // GROUP

    def prep(r0, n):
        pp = _hgrn_prep(q_ref, fl_ref, lb_ref, pad_ref, r0, n)
        rows = pl.ds(r0, n)
        for key, ref in keep:
            ref[rows, :] = pp[key]
        qin_ref[rows, :] = (pp["q"] * pp["e_in"]).astype(MXU)
        kin_ref[rows, :] = (pp["k"] * pp["e_inv"]).astype(MXU)
        kout_ref[rows, :] = (pp["k"] * pp["e_out"]).astype(MXU)
        vp_ref[rows, :] = v_ref[rows, :].astype(MXU)
        dec_ref[rows, :] = pp["dec"]

    _for_row_blocks(L, prep)
    _fill_tail(((qin_ref, 0.0), (kin_ref, 0.0), (kout_ref, 0.0), (vp_ref, 0.0), (dec_ref, 1.0)), L)
    mask = _chunk_mask(GROUP)

    def scores(g, carry):
        rows = pl.ds(pl.multiple_of(g * GROUP, GROUP), GROUP)
        a_ref[rows, :] = jnp.where(mask, _dot_nt(qin_ref[rows, :], kin_ref[rows, :]), 0.0).astype(MXU)
        return carry

    lax.fori_loop(0, n_groups, scores, 0, unroll=GROUP_UNROLL)

    def intra(g, carry):
        rows = pl.ds(pl.multiple_of(g * GROUP, GROUP), GROUP)
        o_ref[rows, :] = _dot(a_ref[rows, :], vp_ref[rows, :])
        kv = _dot_tn(vp_ref[rows, :], _expand(kout_ref[rows, :]))
        for j in range(CPG):
            s_ref[g * CPG + j] = kv[:, j * hd:(j + 1) * hd]
        return carry

    lax.fori_loop(0, n_groups, intra, 0, unroll=GROUP_UNROLL)

    def rec(n, st):
        kv = s_ref[n]
        s_ref[n] = st
        dec = dec_ref[pl.ds(pl.multiple_of(n * CHUNK, CHUNK), SUBLANES), :][0:1]
        return st * dec + kv

    lax.fori_loop(0, L // CHUNK, rec, jnp.zeros((hd, hd), F32))

    def inter(g, carry):
        rows = pl.ds(pl.multiple_of(g * GROUP, GROUP), GROUP)
        scat = jnp.concatenate([s_ref[g * CPG + j] for j in range(CPG)], axis=1)
        o_ref[rows, :] += _dot_nt(_expand(qin_ref[rows, :]), scat)
        return carry

    lax.fori_loop(0, n_groups, inter, 0, unroll=GROUP_UNROLL)


def _hgrn_scratch(L, hd):
    lp = -(-L // GROUP) * GROUP
    return lp, [pltpu.VMEM((GROUP + 2 * PAD, hd), F32), pltpu.VMEM((lp, hd), MXU), pltpu.VMEM((lp, hd), MXU),
                pltpu.VMEM((lp, hd), MXU), pltpu.VMEM((lp, hd), MXU), pltpu.VMEM((lp, hd), F32),
                pltpu.VMEM((lp, hd), F32), pltpu.VMEM((lp // CHUNK, hd, hd), F32), pltpu.VMEM((lp, GROUP), MXU)]


def _hgrn_fwd_call(p3, lb, ng, n_heads, col_q, name):
    B, L, _ = p3.shape
    hd = ng.shape[1]
    _, scratch = _hgrn_scratch(L, hd)

    def body(q_ref, fl_ref, v_ref, og_ref, lb_ref, ng_ref, yb_ref,
             pad_ref, qin_ref, kin_ref, kout_ref, vp_ref, dec_ref, o_ref, s_ref, a_ref):
        _hgrn_forward_core(q_ref, fl_ref, v_ref, lb_ref, pad_ref, qin_ref, kin_ref, kout_ref, vp_ref, dec_ref,
                           o_ref, s_ref, a_ref, L)

        def out(r0, n):
            rows = pl.ds(r0, n)
            yb_ref[rows, :] = _hg_out(o_ref[rows, :], og_ref[rows, :].astype(F32), ng_ref[...]).astype(ACT)

        _for_row_blocks(L, out)

    order = lambda b, h: (b, h)
    return pl.pallas_call(
        body, name=name, out_shape=jax.ShapeDtypeStruct((B, L, n_heads * hd), ACT), grid=(B, n_heads),
        in_specs=_hgrn_specs(L, hd, col_q, n_heads, order) + [
            pl.BlockSpec((1, hd), lambda b, h: (0, h)), pl.BlockSpec((1, hd), lambda b, h: (0, 0))],
        out_specs=pl.BlockSpec((None, L, hd), lambda b, h: (b, 0, h)),
        scratch_shapes=scratch, compiler_params=_cparams())(p3, p3, p3, p3, lb, ng)


def _hgrn_bwd_call(p3, dyb, lb, ng, n_heads, col_q, name):
    B, L, _ = p3.shape
    hd = ng.shape[1]
    n_chunks = L // CHUNK
    lp, scratch = _hgrn_scratch(L, hd)
    n_groups = lp // GROUP

    def body(q_ref, fl_ref, v_ref, og_ref, dyb_ref, lb_ref, ng_ref,
             dq_ref, dfl_ref, dv_ref, dog_ref, dlb_ref, dng_ref,
             pad_ref, qin_ref, kin_ref, kout_ref, vp_ref, dec_ref, o_ref, s_ref, a_ref,
             do_ref, ds_ref, dqi_ref, dki_ref, dko_ref, dvv_ref, dct_ref,
             sig_ref, f_ref, ein_ref, einv_ref, eout_ref, da_ref):
        @pl.when(pl.program_id(1) == 0)
        def _():
            dlb_ref[...] = jnp.zeros_like(dlb_ref)

        @pl.when((pl.program_id(0) == 0) & (pl.program_id(1) == 0))
        def _():
            dng_ref[...] = jnp.zeros_like(dng_ref)

        _hgrn_forward_core(q_ref, fl_ref, v_ref, lb_ref, pad_ref, qin_ref, kin_ref, kout_ref, vp_ref, dec_ref,
                           o_ref, s_ref, a_ref, L, keep=(("sig", sig_ref), ("f", f_ref), ("e_in", ein_ref),
                                                  ("e_inv", einv_ref), ("e_out", eout_ref)))

        def out_bwd(r0, n):
            rows = pl.ds(r0, n)
            _, out_vjp = jax.vjp(_hg_out, o_ref[rows, :], og_ref[rows, :].astype(F32), ng_ref[...])
            d_o, d_og, d_ng = out_vjp(dyb_ref[rows, :].astype(F32))
            dog_ref[rows, :] = d_og.astype(ACT)
            dng_ref[...] += d_ng
            do_ref[rows, :] = d_o.astype(MXU)

        _for_row_blocks(L, out_bwd)
        _fill_tail(((do_ref, 0.0),), L)
        mask = _chunk_mask(GROUP)

        def score_grads(g, carry):
            rows = pl.ds(pl.multiple_of(g * GROUP, GROUP), GROUP)
            da_ref[rows, :] = jnp.where(mask, _dot_nt(do_ref[rows, :], vp_ref[rows, :]), 0.0).astype(MXU)
            return carry

        lax.fori_loop(0, n_groups, score_grads, 0, unroll=GROUP_UNROLL)

        def grads_a(g, carry):
            rows = pl.ds(pl.multiple_of(g * GROUP, GROUP), GROUP)
            qi, ki, do, da = qin_ref[rows, :], kin_ref[rows, :], do_ref[rows, :], da_ref[rows, :]
            sstack = s_ref[pl.ds(g * CPG, CPG)].reshape(CPG * hd, hd)
            dqi_ref[rows, :] = _dot(da, ki) + _dot(_expand(do), sstack)
            dki_ref[rows, :] = _dot_tn(da, qi)
            dvv_ref[rows, :] = _dot_tn(a_ref[rows, :], do)
            x = _dot_tn(do, _expand(qi))
            for j in range(CPG):
                ds_ref[g * CPG + j] = x[:, j * hd:(j + 1) * hd]
            return carry

        lax.fori_loop(0, n_groups, grads_a, 0, unroll=GROUP_UNROLL)

        def rec_bwd(k, dst):
            n = n_chunks - 1 - k
            r0 = pl.multiple_of(n * CHUNK, CHUNK)
            x = ds_ref[n]
            ds_ref[n] = dst
            dec = dec_ref[pl.ds(r0, SUBLANES), :][0:1]
            return dst * dec + x

        lax.fori_loop(0, n_chunks, rec_bwd, jnp.zeros((hd, hd), F32))

        def grads_b(g, carry):
            r0 = pl.multiple_of(g * GROUP, GROUP)
            rows = pl.ds(r0, GROUP)
            ds = [ds_ref[g * CPG + j] for j in range(CPG)]
            dscat = jnp.concatenate(ds, axis=1)
            dvv_ref[rows, :] += _dot_nt(_expand(kout_ref[rows, :]), dscat)
            dstack = ds_ref[pl.ds(g * CPG, CPG)].reshape(CPG * hd, hd)
            dko_ref[rows, :] = _dot(_expand(vp_ref[rows, :]), dstack)
            for j in range(CPG):
                dec = dec_ref[pl.ds(r0 + j * CHUNK, SUBLANES), :][0:1]
                ddec = dec * jnp.sum(ds[j] * s_ref[g * CPG + j], axis=0, keepdims=True)
                dct_ref[pl.ds(r0 + j * CHUNK, CHUNK), :] = jnp.broadcast_to(ddec, (CHUNK, hd))
            return carry

        lax.fori_loop(0, n_groups, grads_b, 0, unroll=GROUP_UNROLL)

        def finish(r0, n):
            rows = pl.ds(r0, n)
            sig, f, e_in, e_inv, e_out = [r[rows, :] for r in (sig_ref, f_ref, ein_ref, einv_ref, eout_ref)]
            q, k = q_ref[rows, :].astype(F32), 1.0 - f
            dqi, dki, dko = dqi_ref[rows, :], dki_ref[rows, :], dko_ref[rows, :]
            dq = dqi * e_in
            dk = dki * e_inv + dko * e_out
            dq_ref[rows, :] = dq.astype(ACT)
            dv_ref[rows, :] = dvv_ref[rows, :].astype(ACT)
            t_out = k * e_out * dko
            dc = q * dq - k * e_inv * dki - t_out
            _, dc_later = _chunk_cumsums(dc, pad_ref, n)
            t_incl, t_later = _chunk_cumsums(t_out, pad_ref, n)
            dlogf = dc + dc_later + t_incl + t_later + dct_ref[rows, :]
            df = dlogf / f - dk
            dfl_ref[rows, :] = (df * (1.0 - lb_ref[...]) * sig * (1.0 - sig)).astype(ACT)
            dlb_ref[...] += jnp.sum(df * (1.0 - sig), axis=0, keepdims=True)

        _for_row_blocks(L, finish)

    order = lambda h, b: (b, h)
    W = n_heads * hd
    act_out = jax.ShapeDtypeStruct((B, L, W), ACT)
    blk_out = pl.BlockSpec((None, L, hd), lambda h, b: (b, 0, h))
    return pl.pallas_call(
        body, name=name,
        out_shape=(act_out, act_out, act_out, act_out, jax.ShapeDtypeStruct((1, W), F32),
                   jax.ShapeDtypeStruct((1, hd), F32)),
        grid=(n_heads, B),
        in_specs=_hgrn_specs(L, hd, col_q, n_heads, order) + [
            pl.BlockSpec((None, L, hd), lambda h, b: (b, 0, h)),
            pl.BlockSpec((1, hd), lambda h, b: (0, h)), pl.BlockSpec((1, hd), lambda h, b: (0, 0))],
        out_specs=(blk_out, blk_out, blk_out, blk_out, pl.BlockSpec((1, hd), lambda h, b: (0, h)),
                   pl.BlockSpec((1, hd), lambda h, b: (0, 0))),
        scratch_shapes=scratch + [
            pltpu.VMEM((lp, hd), MXU), pltpu.VMEM((lp // CHUNK, hd, hd), F32)] + [pltpu.VMEM((lp, hd), F32)] * 10 + [
            pltpu.VMEM((lp, GROUP), MXU)],
        compiler_params=_cparams())(p3, p3, p3, p3, dyb, lb, ng)


def _merge_fn(a, bm, ga, gb):
    return jax.nn.sigmoid(ga) * a + jax.nn.sigmoid(gb) * bm


def _merge_call(yb, a, p, h0, whp, wout, g2, col_ga, tm, name):
    T, D = h0.shape

    def body(yb_ref, a_ref, ga_ref, gb_ref, h0_ref, whp_ref, wout_ref, g2_ref, h1_ref, mg_ref, bm_ref, z2_ref):
        bm = _dot(yb_ref[...], whp_ref[...])
        mg = _merge_fn(a_ref[...].astype(F32), bm, ga_ref[...].astype(F32), gb_ref[...].astype(F32))
        h1 = h0_ref[...] + _dot(mg, wout_ref[...])
        h1_ref[...] = h1
        mg_ref[...] = mg.astype(ACT)
        bm_ref[...] = bm.astype(ACT)
        z2_ref[...] = _rms(h1, g2_ref[...]).astype(ACT)

    tile = pl.BlockSpec((tm, D), lambda i: (i, 0))
    full = pl.BlockSpec((D, D), lambda i: (0, 0))
    act = jax.ShapeDtypeStruct((T, D), ACT)
    return pl.pallas_call(
        body, name=name, out_shape=(jax.ShapeDtypeStruct((T, D), F32), act, act, act), grid=(T // tm,),
        in_specs=[tile, tile, pl.BlockSpec((tm, D), lambda i: (i, col_ga)),
                  pl.BlockSpec((tm, D), lambda i: (i, col_ga + 1)), tile, full, full,
                  pl.BlockSpec((1, D), lambda i: (0, 0))],
        out_specs=(tile, tile, tile, tile), compiler_params=_cparams())(yb, a, p, p, h0, whp, wout, g2)


def _merge_bwd_call(dmg, a, bm, p, col_ga, tm, name):
    T, D = dmg.shape

    def body(dmg_ref, a_ref, bm_ref, ga_ref, gb_ref, da_ref, dbm_ref, dga_ref, dgb_ref):
        args = [r[...].astype(F32) for r in (a_ref, bm_ref, ga_ref, gb_ref)]
        _, vjp = jax.vjp(_merge_fn, *args)
        for r, o in zip((da_ref, dbm_ref, dga_ref, dgb_ref), vjp(dmg_ref[...].astype(F32))):
            r[...] = o.astype(ACT)

    tile = pl.BlockSpec((tm, D), lambda i: (i, 0))
    act = jax.ShapeDtypeStruct((T, D), ACT)
    return pl.pallas_call(
        body, name=name, out_shape=(act, act, act, act), grid=(T // tm,),
        in_specs=[tile, tile, tile, pl.BlockSpec((tm, D), lambda i: (i, col_ga)),
                  pl.BlockSpec((tm, D), lambda i: (i, col_ga + 1))],
        out_specs=(tile, tile, tile, tile), compiler_params=_cparams())(dmg, a, bm, p, p)


def _conv_taps(x_ref, halo_ref, ext_ref, edge, tm, before):
    halo = jnp.where(edge, 0.0, halo_ref[...].astype(F32))
    x = x_ref[...].astype(F32)
    if before:
        ext_ref[0:PAD, :] = halo
        ext_ref[PAD:PAD + tm, :] = x
        return [ext_ref[PAD - 2 + k:PAD - 2 + k + tm, :] for k in range(3)]
    ext_ref[0:tm, :] = x
    ext_ref[tm:tm + PAD, :] = halo
    return [ext_ref[k:k + tm, :] for k in range(3)]


def _conv(taps, cw, cb):
    return cb + cw[0:1] * taps[0] + cw[1:2] * taps[1] + cw[2:3] * taps[2]


def _ffn_pair_specs(tm, F, T, n_pairs, order, before):
    hb = tm // PAD
    last = T // PAD - 1

    def halo_row(i):
        return jnp.maximum(i * hb - 1, 0) if before else jnp.minimum((i + 1) * hb, last)

    specs = []
    for off in (0, n_pairs):
        specs.append(pl.BlockSpec((None, tm, F), lambda *g, off=off: (order(*g)[1] + off, order(*g)[0], 0)))
        specs.append(pl.BlockSpec((None, PAD, F), lambda *g, off=off: (order(*g)[1] + off, halo_row(order(*g)[0]), 0)))
    return specs


def _ffn_fwd_call(up, cw, cb, wd, h1, tgt, g3, tm, tps, name):
    S, T, F = up.shape
    n_pairs = S // 2
    D = h1.shape[1]

    def body(ua_ref, ha_ref, ub_ref, hb_ref, cwa_ref, cwb_ref, cba_ref, cbb_ref, wd_ref, h1_ref, tgt_ref, g3_ref,
             ca_ref, cb_ref, dh2_ref, loss_ref, dg3_ref, acc_ref, ext_ref):
        i, j = pl.program_id(0), pl.program_id(1)
        edge = (i % tps) == 0
        ua = _conv(_conv_taps(ua_ref, ha_ref, ext_ref, edge, tm, True), cwa_ref[...], cba_ref[...])
        ub = _conv(_conv_taps(ub_ref, hb_ref, ext_ref, edge, tm, True), cwb_ref[...], cbb_ref[...])
        ca_ref[...] = ua.astype(ACT)
        cb_ref[...] = ub.astype(ACT)
        contrib = _dot(_silu(ua) * ub, wd_ref[...])

        @pl.when(j == 0)
        def _():
            acc_ref[...] = h1_ref[...] + contrib

        @pl.when(j > 0)
        def _():
            acc_ref[...] += contrib

        @pl.when((i == 0) & (j == 0))
        def _():
            loss_ref[...] = jnp.zeros_like(loss_ref)
            dg3_ref[...] = jnp.zeros_like(dg3_ref)

        @pl.when(j == n_pairs - 1)
        def _():
            row = lax.broadcasted_iota(jnp.int32, (tm, 1), 0) + (i % tps) * tm
            valid = row >= N_META
            tgt = tgt_ref[...]

            def loss_fn(h2, g):
                err = _rms(h2, g) - tgt
                return 0.5 * jnp.sum(jnp.where(valid, err * err, 0.0)) / D

            loss, vjp = jax.vjp(loss_fn, acc_ref[...], g3_ref[...])
            dh2, dg3 = vjp(jnp.ones((), F32))
            dh2_ref[...] = dh2
            loss_ref[...] += loss
            dg3_ref[...] += dg3

    order = lambda i, j: (i, j)
    tile = pl.BlockSpec((tm, D), lambda i, j: (i, 0))
    vec = pl.BlockSpec((1, D), lambda i, j: (0, 0))
    return pl.pallas_call(
        body, name=name,
        out_shape=(jax.ShapeDtypeStruct((n_pairs, T, F), ACT), jax.ShapeDtypeStruct((n_pairs, T, F), ACT),
                   jax.ShapeDtypeStruct((T, D), F32), jax.ShapeDtypeStruct((1, LANES), F32),
                   jax.ShapeDtypeStruct((1, D), F32)),
        grid=(T // tm, n_pairs),
        in_specs=_ffn_pair_specs(tm, F, T, n_pairs, order, True) + [
            pl.BlockSpec((None, 3, F), lambda i, j: (j, 0, 0)), pl.BlockSpec((None, 3, F), lambda i, j: (j + n_pairs, 0, 0)),
            pl.BlockSpec((None, 1, F), lambda i, j: (j, 0, 0)), pl.BlockSpec((None, 1, F), lambda i, j: (j + n_pairs, 0, 0)),
            pl.BlockSpec((None, F, D), lambda i, j: (j, 0, 0)), tile, tile, vec],
        out_specs=(pl.BlockSpec((None, tm, F), lambda i, j: (j, i, 0)), pl.BlockSpec((None, tm, F), lambda i, j: (j, i, 0)),
                   tile, pl.BlockSpec((1, LANES), lambda i, j: (0, 0)), vec),
        scratch_shapes=[pltpu.VMEM((tm, D), F32), pltpu.VMEM((tm + PAD, F), F32)],
        compiler_params=_cparams())(up, up, up, up, cw, cw, cb, cb, wd, h1, tgt, g3)


def _ffn_bwd_a_call(dh2, ca, cb, wd, tm, name):
    n_pairs, T, F = ca.shape
    D = dh2.shape[1]

    def body(dh2_ref, ca_ref, cb_ref, wd_ref, dua_ref, dub_ref, dwd_ref, dcba_ref, dcbb_ref):
        @pl.when(pl.program_id(1) == 0)
        def _():
            for r in (dwd_ref, dcba_ref, dcbb_ref):
                r[...] = jnp.zeros_like(r)

        dh2 = dh2_ref[...]
        ua, ub = ca_ref[...].astype(F32), cb_ref[...].astype(F32)
        sa = jax.nn.sigmoid(ua)
        gate = ua * sa
        dact = _dot_nt(dh2, wd_ref[...])
        dwd_ref[...] += _dot_tn(gate * ub, dh2)
        dub = dact * gate
        dua = dact * ub * sa * (1.0 + ua * (1.0 - sa))
        dcba_ref[...] += jnp.sum(dua, axis=0, keepdims=True)
        dcbb_ref[...] += jnp.sum(dub, axis=0, keepdims=True)
        dua_ref[...] = dua.astype(ACT)
        dub_ref[...] = dub.astype(ACT)

    blk = pl.BlockSpec((None, tm, F), lambda j, i: (j, i, 0))
    vec = pl.BlockSpec((None, 1, F), lambda j, i: (j, 0, 0))
    return pl.pallas_call(
        body, name=name,
        out_shape=(jax.ShapeDtypeStruct((n_pairs, T, F), ACT), jax.ShapeDtypeStruct((n_pairs, T, F), ACT),
                   jax.ShapeDtypeStruct((n_pairs, F, D), F32), jax.ShapeDtypeStruct((n_pairs, 1, F), F32),
                   jax.ShapeDtypeStruct((n_pairs, 1, F), F32)),
        grid=(n_pairs, T // tm),
        in_specs=[pl.BlockSpec((tm, D), lambda j, i: (i, 0)), blk, blk, pl.BlockSpec((None, F, D), lambda j, i: (j, 0, 0))],
        out_specs=(blk, blk, pl.BlockSpec((None, F, D), lambda j, i: (j, 0, 0)), vec, vec),
        compiler_params=_cparams())(dh2, ca, cb, wd)


def _ffn_bwd_b_call(dua, dub, up, cw, wup, h1, g2, dh2, tm, tps, name):
    n_pairs, T, F = dua.shape
    D = h1.shape[1]
    hb = tm // PAD
    last = T // PAD - 1

    def body(da_ref, na_ref, db_ref, nb_ref, ua_ref, ub_ref, cwa_ref, cwb_ref, wa_ref, wb_ref, h1_ref, g2_ref, dh2_ref,
             dupa_ref, dupb_ref, dh1_ref, dg2_ref, dcwa_ref, dcwb_ref, acc_ref, ext_ref):
        i, j = pl.program_id(0), pl.program_id(1)
        edge = (i % tps) == tps - 1

        @pl.when((i == 0) & (j == 0))
        def _():
            dcwa_ref[...] = jnp.zeros_like(dcwa_ref)
            dcwb_ref[...] = jnp.zeros_like(dcwb_ref)

        outs = []
        for d_ref, n_ref, u_ref, cw_ref, o_ref, dcw_ref in (
                (da_ref, na_ref, ua_ref, cwa_ref, dupa_ref, dcwa_ref),
                (db_ref, nb_ref, ub_ref, cwb_ref, dupb_ref, dcwb_ref)):
            t = _conv_taps(d_ref, n_ref, ext_ref, edge, tm, False)
            cwv = cw_ref[...]
            dup = cwv[2:3] * t[0] + cwv[1:2] * t[1] + cwv[0:1] * t[2]
            o_ref[...] = dup.astype(ACT)
            outs.append(dup)
            u = u_ref[...].astype(F32)
            dcw_ref[j] += jnp.concatenate([jnp.sum(u * t[2 - k], axis=0, keepdims=True) for k in range(3)], axis=0)
        contrib = _dot_nt(outs[0], wa_ref[...]) + _dot_nt(outs[1], wb_ref[...])

        @pl.when(j == 0)
        def _():
            acc_ref[...] = contrib

        @pl.when(j > 0)
        def _():
            acc_ref[...] += contrib

        @pl.when((i == 0) & (j == 0))
        def _():
            dg2_ref[...] = jnp.zeros_like(dg2_ref)

        @pl.when(j == n_pairs - 1)
        def _():
            _, vjp = jax.vjp(_rms, h1_ref[...], g2_ref[...])
            dh, dg = vjp(acc_ref[...])
            dh1_ref[...] = dh2_ref[...] + dh
            dg2_ref[...] += dg

    tile = pl.BlockSpec((tm, D), lambda i, j: (i, 0))
    vec = pl.BlockSpec((1, D), lambda i, j: (0, 0))
    pair = lambda: [pl.BlockSpec((None, tm, F), lambda i, j: (j, i, 0)),
                    pl.BlockSpec((None, PAD, F), lambda i, j: (j, jnp.minimum((i + 1) * hb, last), 0))]
    act = jax.ShapeDtypeStruct((n_pairs, T, F), ACT)
    dcw = jax.ShapeDtypeStruct((n_pairs, 3, F), F32)
    dcw_spec = pl.BlockSpec((n_pairs, 3, F), lambda i, j: (0, 0, 0))
    return pl.pallas_call(
        body, name=name,
        out_shape=(act, act, jax.ShapeDtypeStruct((T, D), F32), jax.ShapeDtypeStruct((1, D), F32), dcw, dcw),
        grid=(T // tm, n_pairs),
        in_specs=pair() + pair() + [
            pl.BlockSpec((None, tm, F), lambda i, j: (j, i, 0)), pl.BlockSpec((None, tm, F), lambda i, j: (j + n_pairs, i, 0)),
            pl.BlockSpec((None, 3, F), lambda i, j: (j, 0, 0)), pl.BlockSpec((None, 3, F), lambda i, j: (j + n_pairs, 0, 0)),
            pl.BlockSpec((None, D, F), lambda i, j: (j, 0, 0)), pl.BlockSpec((None, D, F), lambda i, j: (j + n_pairs, 0, 0)),
            tile, vec, tile],
        out_specs=(pl.BlockSpec((None, tm, F), lambda i, j: (j, i, 0)), pl.BlockSpec((None, tm, F), lambda i, j: (j, i, 0)),
                   tile, vec, dcw_spec, dcw_spec),
        scratch_shapes=[pltpu.VMEM((tm, D), F32), pltpu.VMEM((tm + PAD, F), F32)],
        compiler_params=_cparams())(dua, dua, dub, dub, up, up, cw, cw, wup, wup, h1, g2, dh2)


def _in_bwd_call(dp, w_in, h0, g1, dh1, tm, name):
    T, D = h0.shape
    S, _, N = w_in.shape

    def body(dp_ref, w_ref, h0_ref, g1_ref, dh1_ref, dh0_ref, dg1_ref, acc_ref):
        i, j = pl.program_id(0), pl.program_id(1)
        contrib = _dot_nt(dp_ref[...], w_ref[...])

        @pl.when(j == 0)
        def _():
            acc_ref[...] = contrib

        @pl.when(j > 0)
        def _():
            acc_ref[...] += contrib

        @pl.when((i == 0) & (j == 0))
        def _():
            dg1_ref[...] = jnp.zeros_like(dg1_ref)

        @pl.when(j == S - 1)
        def _():
            _, vjp = jax.vjp(_rms, h0_ref[...], g1_ref[...])
            dh, dg = vjp(acc_ref[...])
            dh0_ref[...] = dh1_ref[...] + dh
            dg1_ref[...] += dg

    tile = pl.BlockSpec((tm, D), lambda i, j: (i, 0))
    vec = pl.BlockSpec((1, D), lambda i, j: (0, 0))
    return pl.pallas_call(
        body, name=name, out_shape=(jax.ShapeDtypeStruct((T, D), F32), jax.ShapeDtypeStruct((1, D), F32)),
        grid=(T // tm, S),
        in_specs=[pl.BlockSpec((tm, N), lambda i, j: (i, j)), pl.BlockSpec((None, D, N), lambda i, j: (j, 0, 0)),
                  tile, vec, tile],
        out_specs=(tile, vec), scratch_shapes=[pltpu.VMEM((tm, D), F32)],
        compiler_params=_cparams())(dp, w_in, h0, g1, dh1)


def _meta_grad_call(dh0_3, name):
    B, L, D = dh0_3.shape

    def body(d_ref, o_ref):
        o_ref[...] = jnp.sum(d_ref[...], axis=0)

    return pl.pallas_call(
        body, name=name, out_shape=jax.ShapeDtypeStruct((N_META, D), F32), grid=(1,),
        in_specs=[pl.BlockSpec((B, N_META, D), lambda i: (0, 0, 0))],
        out_specs=pl.BlockSpec((N_META, D), lambda i: (0, 0)), compiler_params=_cparams())(dh0_3)


_RELS = [(dx, dy, dc) for dx in (0, 1) for dy in (0, 1) for dc in (0, 1)][1:]


def _exchange_call(arrs, scatter, name):
    n = len(arrs)
    n_rel = len(_RELS)

    def body(*refs):
        ins, outs = refs[:n], refs[n:2 * n]
        send_sems, recv_sems, loc_sems = refs[2 * n:]
        x, y, c = lax.axis_index("x"), lax.axis_index("y"), lax.axis_index("c")
        me = 4 * x + 2 * y + c
        started = []
        for k in range(n):
            src_me = ins[k].at[me] if scatter else ins[k]
            loc = pltpu.make_async_copy(src_me, outs[k].at[me], loc_sems.at[k])
            loc.start()
            started.append(loc)
        waits = []
        for r, (dx, dy, dc) in enumerate(_RELS):
            px, py, pc = (x + dx) % 2, (y + dy) % 2, (c + dc) % 2
            pid = 4 * px + 2 * py + pc
            for k in range(n):
                s = k * n_rel + r
                src = ins[k].at[pid] if scatter else ins[k]
                cp = pltpu.make_async_remote_copy(
                    src_ref=src, dst_ref=outs[k].at[me], send_sem=send_sems.at[s], recv_sem=recv_sems.at[s],
                    device_id=(px, py, pc), device_id_type=pl.DeviceIdType.MESH)
                cp.start()
                waits.append(pltpu.make_async_remote_copy(
                    src_ref=src, dst_ref=outs[k].at[pid], send_sem=send_sems.at[s], recv_sem=recv_sems.at[s],
                    device_id=(px, py, pc), device_id_type=pl.DeviceIdType.MESH))
        for w in waits:
            w.wait_send()
            w.wait_recv()
        for loc in started:
            loc.wait()

    out_shape = tuple(jax.ShapeDtypeStruct(a.shape if scatter else (N_DEV,) + a.shape, a.dtype) for a in arrs)
    hbm = pl.BlockSpec(memory_space=pl.ANY)
    return pl.pallas_call(
        body, name=name, out_shape=out_shape, in_specs=[hbm] * n, out_specs=tuple([hbm] * n),
        scratch_shapes=[pltpu.SemaphoreType.DMA((n * n_rel,)), pltpu.SemaphoreType.DMA((n * n_rel,)),
                        pltpu.SemaphoreType.DMA((n,))],
        compiler_params=pltpu.CompilerParams(has_side_effects=True))(*arrs)


_HBM = pl.BlockSpec(memory_space=pltpu.HBM)
_SEM = pl.BlockSpec(memory_space=pltpu.SEMAPHORE)
_DATAFLOW = pltpu.SideEffectType.DATAFLOW_SIDE_EFFECTING


def _peer_copies(ins, lands, send_sems, recv_sems, scatter):
    n = len(ins)
    x, y, c = lax.axis_index("x"), lax.axis_index("y"), lax.axis_index("c")
    me = 4 * x + 2 * y + c
    sends, arrivals = [], []
    for r, (dx, dy, dc) in enumerate(_RELS):
        px, py, pc = (x + dx) % 2, (y + dy) % 2, (c + dc) % 2
        pid = 4 * px + 2 * py + pc
        for k in range(n):
            s = k * len(_RELS) + r
            src = ins[k].at[pid] if scatter else ins[k]
            for dst, out in ((lands[k].at[me], sends), (lands[k].at[pid], arrivals)):
                out.append(pltpu.make_async_remote_copy(
                    src_ref=src, dst_ref=dst, send_sem=send_sems.at[s], recv_sem=recv_sems.at[s],
                    device_id=(px, py, pc), device_id_type=pl.DeviceIdType.MESH))
    return sends, arrivals


def _exchange_start(arrs, scatter, name):
    n = len(arrs)
    n_sem = n * len(_RELS)

    def body(*refs):
        ins, lands = refs[:n], refs[n:2 * n]
        send_sems, recv_sems = refs[2 * n], refs[2 * n + 1]
        token = refs[-1]
        sends, _ = _peer_copies(ins, lands, send_sems, recv_sems, scatter)
        for cp in sends:
            cp.start()
        token[...] = jnp.zeros_like(token)

    land_shapes = [a.shape if scatter else (N_DEV,) + a.shape for a in arrs]
    ops = [pltpu.with_memory_space_constraint(a, pltpu.HBM) for a in arrs]
    ops += [pltpu.with_memory_space_constraint(lax.empty(s, a.dtype), pltpu.HBM) for s, a in zip(land_shapes, arrs)]
    out = pl.pallas_call(
        body, name=name,
        out_shape=(pltpu.SemaphoreType.DMA((n_sem,)), pltpu.SemaphoreType.DMA((n_sem,)),
                   *[pltpu.HBM(a.shape, a.dtype) for a in arrs],
                   *[pltpu.HBM(s, a.dtype) for s, a in zip(land_shapes, arrs)],
                   jax.ShapeDtypeStruct((SUBLANES, LANES), F32)),
        in_specs=[_HBM] * (2 * n),
        out_specs=(_SEM, _SEM, *[_HBM] * (2 * n), pl.BlockSpec(memory_space=pltpu.VMEM)),
        input_output_aliases={i: 2 + i for i in range(2 * n)},
        compiler_params=pltpu.CompilerParams(has_side_effects=_DATAFLOW))(*ops)
    return out[0], out[1], list(out[2:2 + n]), list(out[2 + n:2 + 2 * n]), out[-1]


def _exchange_wait(started, after, scatter, name):
    send_sems, recv_sems, srcs, lands, _ = started
    n = len(srcs)

    def body(*refs):
        ins, lands_ = refs[:n], refs[n:2 * n]
        _, arrivals = _peer_copies(ins, lands_, refs[2 * n], refs[2 * n + 1], scatter)
        for cp in arrivals:
            cp.wait_send()
            cp.wait_recv()

    out = pl.pallas_call(
        body, name=name,
        out_shape=(*[pltpu.HBM(a.shape, a.dtype) for a in srcs], *[pltpu.HBM(a.shape, a.dtype) for a in lands]),
        in_specs=[_HBM] * (2 * n) + [_SEM, _SEM, pl.BlockSpec(memory_space=pl.ANY)],
        out_specs=tuple([_HBM] * (2 * n)), input_output_aliases={i: i for i in range(2 * n)},
        compiler_params=pltpu.CompilerParams(has_side_effects=_DATAFLOW))(*srcs, *lands, send_sems, recv_sems, after)
    return list(out[:n]), list(out[n:])


def _place_own_call(srcs, lands, scatter, me, name):
    outs = []
    for k, (src, land) in enumerate(zip(srcs, lands)):
        R, C = land.shape[1:]
        tr = R
        while tr % 32 == 0 and tr * C * land.dtype.itemsize > 2 * 1024 * 1024:
            tr //= 2

        def body(me_ref, s_ref, l_ref, o_ref):
            o_ref[...] = s_ref[...]

        src_spec = (pl.BlockSpec((None, tr, C), lambda i, me_ref: (me_ref[0], i, 0)) if scatter
                    else pl.BlockSpec((tr, C), lambda i, me_ref: (i, 0)))
        outs.append(pl.pallas_call(
            body, name=f"{name}_{k}", out_shape=jax.ShapeDtypeStruct(land.shape, land.dtype),
            grid_spec=pltpu.PrefetchScalarGridSpec(
                num_scalar_prefetch=1, grid=(R // tr,),
                in_specs=[src_spec, pl.BlockSpec(memory_space=pl.ANY)],
                out_specs=pl.BlockSpec((None, tr, C), lambda i, me_ref: (me_ref[0], i, 0))),
            input_output_aliases={2: 0}, compiler_params=_cparams())(me, src, land))
    return outs


def _adamw_shard_call(w, parts, m, v, name):
    R, C = w.shape
    tr = _tile(R, 128) if R % 16 == 0 else R

    def body(w_ref, p_ref, m_ref, v_ref, g_ref, d_ref, nm_ref, nv_ref):
        g = p_ref[0].astype(F32)
        for s in range(1, N_DEV):
            g = g + p_ref[s].astype(F32)
        d, nm, nv = _adamw(w_ref[...], g, m_ref[...], v_ref[...])
        g_ref[...] = g
        d_ref[...] = d
        nm_ref[...] = nm
        nv_ref[...] = nv

    tile = pl.BlockSpec((tr, C), lambda i: (i, 0))
    sh = jax.ShapeDtypeStruct((R, C), F32)
    return pl.pallas_call(
        body, name=name, out_shape=(sh, sh, sh, sh), grid=(R // tr,),
        in_specs=[tile, pl.BlockSpec((N_DEV, tr, C), lambda i: (0, i, 0)), tile, tile],
        out_specs=(tile, tile, tile, tile), compiler_params=_cparams())(w, parts, m, v)


def _pack(arrs, rows_mult=SUBLANES):
    flat = jnp.concatenate([a.reshape(-1).astype(F32) for a in arrs])
    n = flat.shape[0]
    per = rows_mult * LANES
    total = -(-n // per) * per
    return jnp.pad(flat, (0, total - n)).reshape(total // LANES, LANES)


def _unpack(pack, shapes):
    flat = pack.reshape(-1)
    out, off = [], 0
    for s in shapes:
        n = 1
        for d in s:
            n *= d
        out.append(flat[off:off + n].reshape(s))
        off += n
    return out


def kernel(x, meta_tokens, mix_norm_g, w_in, ssm_lambda_re, ssm_lambda_im, ssm_log_dt, ssm_b_re, ssm_b_im, ssm_c_re, ssm_c_im, ssm_d, ssm_w_glu, w_ssm_proj, hgrn_lb_logits, hgrn_norm_g, w_hgrn_proj, w_out, ffn_norm_g, w_up, conv_w, conv_b, w_down, final_norm_g, loss_target, m_meta_tokens, m_mix_norm_g, m_w_in, m_ssm_lambda_re, m_ssm_lambda_im, m_ssm_log_dt, m_ssm_b_re, m_ssm_b_im, m_ssm_c_re, m_ssm_c_im, m_ssm_d, m_ssm_w_glu, m_w_ssm_proj, m_hgrn_lb_logits, m_hgrn_norm_g, m_w_hgrn_proj, m_w_out, m_ffn_norm_g, m_w_up, m_conv_w, m_conv_b, m_w_down, m_final_norm_g, v_meta_tokens, v_mix_norm_g, v_w_in, v_ssm_lambda_re, v_ssm_lambda_im, v_ssm_log_dt, v_ssm_b_re, v_ssm_b_im, v_ssm_c_re, v_ssm_c_im, v_ssm_d, v_ssm_w_glu, v_w_ssm_proj, v_hgrn_lb_logits, v_hgrn_norm_g, v_w_hgrn_proj, v_w_out, v_ffn_norm_g, v_w_up, v_conv_w, v_conv_b, v_w_down, v_final_norm_g):
    args = dict(locals())
    B, S_len, D = x.shape
    L = S_len + N_META
    T = B * L
    tm = _tile(L, ROW_TILE_CAP)
    tps = L // tm
    G, P = ssm_lambda_re.shape[1:]
    H = ssm_b_re.shape[-1]
    W = G * H
    n_cb = W // LANES
    gpb = G // n_cb
    hd = hgrn_norm_g.shape[1]
    n_heads = D // hd
    n_in = w_in.shape[2]
    F = w_up.shape[2]
    assert W == D and n_in % LANES == 0

    me = (4 * lax.axis_index("x") + 2 * lax.axis_index("y") + lax.axis_index("c")).astype(jnp.int32).reshape(1)
    meta_g, cw_g = _exchange_call([meta_tokens, conv_w[0]], False, "gather_small_params")
    ga = _exchange_start([w_in[0].astype(MXU)], False, "gather_a_start")
    gb = _exchange_start(
        [w_up[0].astype(MXU), ssm_w_glu[0].astype(MXU), w_ssm_proj[0].astype(MXU), w_hgrn_proj[0].astype(MXU),
         w_out[0].astype(MXU), w_down[0].astype(MXU)], False, "gather_b_start")
    started_tok = (ga[4] + gb[4])[0:1, 0:1]
    meta_full = meta_g.transpose(1, 0, 2).reshape(N_META, D)
    cb_g = conv_b.reshape(N_DEV, 1, F)

    h0 = jnp.concatenate([jnp.broadcast_to(meta_full[None], (B, N_META, D)), x], axis=1).reshape(T, D)
    tgt = jnp.concatenate([jnp.zeros((B, N_META, D), F32), loss_target], axis=1).reshape(T, D)

    lr, li = ssm_lambda_re[0], ssm_lambda_im[0]
    ldt = ssm_log_dt[0].reshape(G, 1)
    bt_re = ssm_b_re[0].transpose(2, 0, 1).reshape(H, G * P)
    bt_im = ssm_b_im[0].transpose(2, 0, 1).reshape(H, G * P)
    seg = _seg_len(L)
    a_re, a_im, as_re, as_im, coef_re, coef_im = _small_call(
        _disc_a_power(seg), [lr, li, ldt], [((G, P), F32)] * 6, "s5_discretise")
    bbt_re, bbt_im = _small_call(
        _disc_b, [coef_re.reshape(1, G * P), coef_im.reshape(1, G * P), bt_re, bt_im],
        [((H, G * P), F32)] * 2, "s5_input_matrix")
    eye = jnp.eye(gpb, dtype=F32)
    hw = gpb * P

    def expand_b(bbt):
        t = bbt.reshape(H, n_cb, gpb, P).transpose(1, 0, 2, 3)[:, None]
        return (t * eye[None, :, None, :, None]).reshape(n_cb, gpb * H, hw)

    def expand_c(cm):
        t = cm.reshape(n_cb, gpb, H, P).transpose(0, 1, 3, 2)[:, :, :, None]
        return (t * eye[None, :, None, :, None]).reshape(n_cb, hw, gpb * H)

    wb = jnp.concatenate([expand_b(bbt_re), expand_b(bbt_im)], axis=2).astype(MXU)
    wc = jnp.concatenate([expand_c(ssm_c_re[0]), -expand_c(ssm_c_im[0])], axis=1).astype(MXU)
    tab = jnp.stack([jnp.concatenate([a_re.reshape(n_cb, hw), a_im.reshape(n_cb, hw)], axis=1),
                     jnp.concatenate([as_re.reshape(n_cb, hw), as_im.reshape(n_cb, hw)], axis=1)], axis=1)
    tab = jnp.broadcast_to(tab[:, :, None, :], (n_cb, 2, SUBLANES, 2 * hw))
    dsk = ssm_d.reshape(n_cb, 1, LANES)
    lb = _small_call(_lb_fn, [hgrn_lb_logits], [((1, D), F32)], "hgrn_lower_bound")[0]

    z1 = _norm_call(h0, mix_norm_g + started_tok, tm, "mix_norm")
    ready = jnp.concatenate([t[(0,) * (t.ndim - 1)][0:1].astype(F32) for t in (z1, wb, wc, tab, lb, tgt)])
    ga_src, ga_land = _exchange_wait(ga, ready, False, "gather_a_wait")
    win_g = _place_own_call(ga_src, ga_land, False, me, "gather_a_own")[0]
    p = _mm_shard(z1, win_g, tm, "in_proj", False)
    p3 = p.reshape(B, L, p.shape[1])
    u_seg = _to_segments(p3[:, :, :W], seg)
    ya_seg, s_all = _s5_fwd_call(u_seg, wb, wc, tab, dsk, "s5_fwd")
    ya = _from_segments(ya_seg, seg, L).reshape(T, W)
    gb_src, gb_land = _exchange_wait(gb, ya, False, "gather_b_wait")
    gathered = _place_own_call(gb_src, gb_land, False, me, "gather_b_own")
    wup_g = gathered[0]
    wglu_g, wsp_g, whp_g, wout_g = [g.reshape(D, D) for g in gathered[1:5]]
    wdn_g = gathered[5].reshape(N_DEV // 2, 2 * w_down.shape[1], D)
    yo, a_br = _glu_proj_call(ya, wglu_g, wsp_g, tm, "s5_glu_proj")
    yb = _hgrn_fwd_call(p3, lb, hgrn_norm_g, n_heads, n_cb, "hgrn_fwd").reshape(T, D)
    col_ga = 5
    h1, mg, bm, z2 = _merge_call(yb, a_br, p, h0, whp_g, wout_g, ffn_norm_g, col_ga, tm, "merge")
    up = _mm_shard(z2, wup_g, tm, "up_proj", True)
    conv_a, conv_b_out, dh2, loss_part, dg3 = _ffn_fwd_call(up, cw_g, cb_g, wdn_g, h1, tgt, final_norm_g.reshape(1, D),
                                                            tm, tps, "ffn_out_loss")

    dua, dub, dwd, dcba, dcbb = _ffn_bwd_a_call(dh2, conv_a, conv_b_out, wdn_g, tm, "ffn_bwd_gate")
    dupa, dupb, dh1, dg2, dcwa, dcwb = _ffn_bwd_b_call(dua, dub, up, cw_g, wup_g, h1, ffn_norm_g, dh2, tm, tps,
                                                       "ffn_bwd_up")
    dwup = jnp.concatenate([_mm_tn(z2, dupa, N_DEV // 2, tm, "dw_up_a", True),
                            _mm_tn(z2, dupb, N_DEV // 2, tm, "dw_up_b", True)], axis=0)
    sh_rows = D // N_DEV
    sa = _exchange_start([dwup, dwd.reshape(N_DEV, w_down.shape[1], D)], True, "scatter_a_start")
    dmg, dwout = _lin_bwd(mg, dh1, wout_g + sa[4][0:1, 0:1].astype(MXU), tm, "out_proj_bwd")
    da_br, dbm, dga, dgb = _merge_bwd_call(dmg, a_br, bm, p, col_ga, tm, "merge_bwd")
    dyo, dwsp = _lin_bwd(yo, da_br, wsp_g, tm, "ssm_proj_bwd")
    dyb, dwhp = _lin_bwd(yb, dbm, whp_g, tm, "hgrn_proj_bwd")
    dya, dwglu = _glu_bwd_call(ya, dyo, wglu_g, tm, "s5_glu_bwd")
    sb = _exchange_start([t.reshape(N_DEV, sh_rows, D) for t in (dwglu, dwsp, dwhp, dwout)], True, "scatter_b_start")
    tok_b = sb[4][0:1, :]
    du_seg, dwb, dwc, dab, ddsk = _s5_bwd_call(u_seg, s_all, _to_segments(dya.reshape(B, L, W), seg), wb, wc, tab,
                                               dsk + tok_b[None], "s5_bwd")
    du = _from_segments(du_seg, seg, L)

    def diag_b(dw):
        t = (dw.reshape(n_cb, gpb, H, gpb, P) * eye[None, :, None, :, None]).sum(axis=1)
        return t.transpose(1, 0, 2, 3).reshape(H, G * P)

    def diag_c(dw):
        t = (dw.reshape(n_cb, gpb, P, gpb, H) * eye[None, :, None, :, None]).sum(axis=3)
        return t.transpose(0, 1, 3, 2).reshape(G, H, P)

    early_parts = [dab[:, 0, :hw].reshape(G, P), dab[:, 0, hw:].reshape(G, P), ddsk.reshape(1, D)]
    early = [_pack(early_parts), diag_b(dwb[:, :, :hw]), diag_b(dwb[:, :, hw:]),
             diag_c(dwc[:, :hw]).reshape(G * H, P), -diag_c(dwc[:, hw:]).reshape(G * H, P)]
    se = _exchange_start(early, False, "gather_s5_grads_start")
    dq, dfl, di, dog, dlb, dng = _hgrn_bwd_call(p3, dyb.reshape(B, L, D), lb, hgrn_norm_g + tok_b + se[4][0:1, :],
                                                n_heads, n_cb, "hgrn_bwd")
    dp = jnp.concatenate([du.reshape(T, W), dq.reshape(T, D), dfl.reshape(T, D), di.reshape(T, D),
                          dog.reshape(T, D), dga, dgb], axis=1)
    dwin = _mm_tn(z1, dp, N_DEV, tm, "dw_in", False)
    sc = _exchange_start([dwin.astype(WIRE)], True, "scatter_c_start")
    dh0, dg1 = _in_bwd_call(dp, win_g, h0, mix_norm_g + sc[4][0:1, 0:1], dh1, tm, "in_proj_bwd")
    dh0_3 = dh0.reshape(B, L, D)
    grad_x = dh0_3[:, N_META:]
    dmeta = _meta_grad_call(dh0_3, "meta_grad")

    late_parts = [dg1, dlb, dng, dg2, jnp.concatenate([dcba, dcbb], axis=0).reshape(1, N_DEV * F), dg3, loss_part]
    late_pack = _pack(late_parts)

    dcw = jnp.concatenate([dcwa, dcwb], axis=0)
    dmeta_s = dmeta.reshape(N_META, N_DEV, D // N_DEV).transpose(1, 0, 2)
    parts_d = _exchange_call([dmeta_s, dcw], True, "scatter_small_grads")
    late_all = _exchange_call([late_pack], False, "gather_small_grads")[0]
    early_all = _place_own_call(*_exchange_wait(se, late_all, False, "gather_s5_grads_wait"), False, me,
                                "gather_s5_grads_own")
    parts_a = _place_own_call(*_exchange_wait(sa, late_all, True, "scatter_a_wait"), True, me, "scatter_a_own")
    parts_b = _place_own_call(*_exchange_wait(sb, late_all, True, "scatter_b_wait"), True, me, "scatter_b_own")
    parts_c = _place_own_call(*_exchange_wait(sc, late_all, True, "scatter_c_wait"), True, me, "scatter_c_own")
    parts = [parts_c[0], parts_a[0], *parts_b, parts_a[1], parts_d[0], parts_d[1]]

    def sum8(*gathered):
        out = []
        for a in gathered:
            t = a[0]
            for s in range(1, N_DEV):
                t = t + a[s]
            out.append(t)
        return tuple(out)

    sums = _small_call(sum8, [*early_all, late_all], [(a.shape, F32) for a in (*early, late_pack)], "sum_small_grads")
    t_abr, t_abi, g_dsk = _unpack(sums[0], [a.shape for a in early_parts])
    t_bbr, t_bbi = sums[1], sums[2]
    g_cre, g_cim = sums[3].reshape(G, H, P), sums[4].reshape(G, H, P)
    g_g1, t_lb, g_ng, g_g2, g_cb, g_g3, loss_v = _unpack(sums[5], [a.shape for a in late_parts])

    def disc_b_bwd(cr, ci, br, bi, dbr, dbi):
        _, vjp = jax.vjp(_disc_b, cr, ci, br, bi)
        return vjp((dbr, dbi))

    t_cr, t_ci, g_btr, g_bti = _small_call(
        disc_b_bwd, [coef_re.reshape(1, G * P), coef_im.reshape(1, G * P), bt_re, bt_im, t_bbr, t_bbi],
        [((1, G * P), F32)] * 2 + [((H, G * P), F32)] * 2, "s5_input_matrix_bwd")

    def disc_a_bwd(lr_, li_, ldt_, dar, dai, dcr, dci):
        _, vjp = jax.vjp(_disc_a, lr_, li_, ldt_)
        return vjp((dar, dai, dcr, dci))

    g_lr, g_li, g_ldt = _small_call(
        disc_a_bwd, [lr, li, ldt, t_abr, t_abi, t_cr.reshape(G, P), t_ci.reshape(G, P)],
        [((G, P), F32)] * 2 + [((G, 1), F32)], "s5_discretise_bwd")

    def lb_bwd(logits, d):
        _, vjp = jax.vjp(_lb_fn, logits)
        return vjp(d)

    g_lbl = _small_call(lb_bwd, [hgrn_lb_logits, t_lb], [(hgrn_lb_logits.shape, F32)], "hgrn_lower_bound_bwd")[0]

    grads = dict(
        mix_norm_g=g_g1, ssm_lambda_re=g_lr[None], ssm_lambda_im=g_li[None], ssm_log_dt=g_ldt.reshape(1, G),
        ssm_b_re=g_btr.reshape(H, G, P).transpose(1, 2, 0)[None], ssm_b_im=g_bti.reshape(H, G, P).transpose(1, 2, 0)[None],
        ssm_c_re=g_cre[None], ssm_c_im=g_cim[None], ssm_d=g_dsk, hgrn_lb_logits=g_lbl, hgrn_norm_g=g_ng,
        ffn_norm_g=g_g2, conv_b=g_cb.reshape(1, N_DEV * F), final_norm_g=g_g3.reshape(D))
    loss = loss_v[0, 0]

    delta, new_m, new_v = {}, {}, {}
    sharded = [("w_in", parts[0], (D, n_in)), ("w_up", parts[1], (D, F)), ("ssm_w_glu", parts[2], (sh_rows, D)),
               ("w_ssm_proj", parts[3], (sh_rows, D)), ("w_hgrn_proj", parts[4], (sh_rows, D)),
               ("w_out", parts[5], (sh_rows, D)), ("w_down", parts[6], (w_down.shape[1], D)),
               ("meta_tokens", parts[7], (N_META, D // N_DEV)), ("conv_w", parts[8], (3, F))]
    for name, part, shp in sharded:
        full = args[name].shape
        g, d_, nm, nv = _adamw_shard_call(args[name].reshape(shp), part, args["m_" + name].reshape(shp),
                                          args["v_" + name].reshape(shp), "adamw_" + name)
        grads[name], delta[name], new_m[name], new_v[name] = [t.reshape(full) for t in (g, d_, nm, nv)]

    for n, shp in (("ssm_b_re", (G * P, H)), ("ssm_b_im", (G * P, H)), ("ssm_c_re", (G * H, P)), ("ssm_c_im", (G * H, P))):
        outs = _small_call(_adamw, [t.reshape(shp) for t in (args[n], grads[n], args["m_" + n], args["v_" + n])],
                           [(shp, F32)] * 3, "adamw_" + n)
        delta[n], new_m[n], new_v[n] = [o.reshape(args[n].shape) for o in outs]
    rep = ["mix_norm_g", "ssm_lambda_re", "ssm_lambda_im", "ssm_log_dt", "ssm_d", "hgrn_lb_logits", "hgrn_norm_g",
           "ffn_norm_g", "conv_b", "final_norm_g"]
    rep_shapes = [args[n].shape for n in rep]
    packs = [_pack([args[pre + n] for n in rep]) for pre in ("", "m_", "v_")]
    g_pack = _pack([grads[n] for n in rep])
    outs = _small_call(lambda w, g, m, v: _adamw(w, g, m, v), [packs[0], g_pack, packs[1], packs[2]],
                       [(g_pack.shape, F32)] * 3, "adamw_replicated")
    for n, d_, nm, nv in zip(rep, *[_unpack(o, rep_shapes) for o in outs]):
        delta[n], new_m[n], new_v[n] = d_, nm, nv

    names = ["meta_tokens", "mix_norm_g", "w_in", "ssm_lambda_re", "ssm_lambda_im", "ssm_log_dt", "ssm_b_re",
             "ssm_b_im", "ssm_c_re", "ssm_c_im", "ssm_d", "ssm_w_glu", "w_ssm_proj", "hgrn_lb_logits", "hgrn_norm_g",
             "w_hgrn_proj", "w_out", "ffn_norm_g", "w_up", "conv_w", "conv_b", "w_down", "final_norm_g"]
    return (loss, grad_x, *[grads[n] for n in names], *[delta[n] for n in names],
            *[new_m[n] for n in names], *[new_v[n] for n in names])
```

```python
import jax
import jax.numpy as jnp
from jax import lax
from jax.experimental import pallas as pl
from jax.experimental.pallas import tpu as pltpu

F32 = jnp.float32
MXU = jnp.bfloat16
ACT = jnp.bfloat16
WIRE = jnp.bfloat16
N_DEV = 8
N_META = 16
CHUNK = 16
EPS = 1e-6
ADAM_LR, ADAM_B1, ADAM_B2, ADAM_EPS, ADAM_WD, ADAM_STEP = 0.001, 0.9, 0.999, 1e-08, 0.01, 10
SUBLANES = 8
LANES = 128
ROW_TILE_CAP = 700
VMEM_LIMIT = 60 * 1024 * 1024


def _cparams(**kw):
    return pltpu.CompilerParams(vmem_limit_bytes=VMEM_LIMIT, **kw)


def _tile(n, cap):
    best = None
    for t in range(16, min(n, cap) + 1, 16):
        if n % t == 0:
            best = t
    assert best is not None, (n, cap)
    return best


def _dot(a, b):
    return lax.dot_general(a.astype(MXU), b.astype(MXU), (((1,), (0,)), ((), ())), preferred_element_type=F32)


def _dot_nt(a, b):
    return lax.dot_general(a.astype(MXU), b.astype(MXU), (((1,), (1,)), ((), ())), preferred_element_type=F32)


def _dot_tn(a, b):
    return lax.dot_general(a.astype(MXU), b.astype(MXU), (((0,), (0,)), ((), ())), preferred_element_type=F32)


def _rms(x, g):
    return x * lax.rsqrt(jnp.mean(x * x, axis=-1, keepdims=True) + EPS) * g


def _silu(x):
    return x * jax.nn.sigmoid(x)


def _small_call(fn, ins, out_shapes, name):
    n_in = len(ins)

    def body(*refs):
        outs = fn(*[r[...] for r in refs[:n_in]])
        outs = outs if isinstance(outs, (tuple, list)) else (outs,)
        for r, o in zip(refs[n_in:], outs):
            r[...] = o.astype(r.dtype)

    vm = pl.BlockSpec(memory_space=pltpu.VMEM)
    return pl.pallas_call(
        body, name=name, out_shape=tuple(jax.ShapeDtypeStruct(s, d) for s, d in out_shapes),
        in_specs=[vm] * n_in, out_specs=tuple([vm] * len(out_shapes)), compiler_params=_cparams())(*ins)


def _disc_a(lr, li, ldt):
    dt = jnp.exp(ldt)
    mag = jnp.exp(lr * dt)
    ab_re = mag * jnp.cos(li * dt)
    ab_im = mag * jnp.sin(li * dt)
    den = lr * lr + li * li
    nr = ab_re - 1.0
    coef_re = (nr * lr + ab_im * li) / den
    coef_im = (ab_im * lr - nr * li) / den
    return ab_re, ab_im, coef_re, coef_im


def _disc_a_power(n):
    def fn(lr, li, ldt):
        ab_re, ab_im, coef_re, coef_im = _disc_a(lr, li, ldt)
        pr, pi, sr, si, m = None, None, ab_re, ab_im, n
        while m:
            if m & 1:
                pr, pi = (sr, si) if pr is None else (pr * sr - pi * si, pr * si + pi * sr)
            m >>= 1
            if m:
                sr, si = sr * sr - si * si, 2.0 * sr * si
        return ab_re, ab_im, pr, pi, coef_re, coef_im
    return fn


def _disc_b(coef_re, coef_im, bt_re, bt_im):
    return coef_re * bt_re - coef_im * bt_im, coef_re * bt_im + coef_im * bt_re


def _lb_fn(logits):
    return jax.nn.softmax(logits, axis=0)[0:1]


def _adamw(w, g, m, v):
    m = ADAM_B1 * m + (1.0 - ADAM_B1) * g
    v = ADAM_B2 * v + (1.0 - ADAM_B2) * jnp.square(g)
    m_hat = m / (1.0 - ADAM_B1 ** ADAM_STEP)
    v_hat = v / (1.0 - ADAM_B2 ** ADAM_STEP)
    delta = -ADAM_LR * (m_hat / (jnp.sqrt(v_hat) + ADAM_EPS) + ADAM_WD * w)
    return delta, m, v


def _norm_call(h, g, tm, name):
    T, D = h.shape

    def body(h_ref, g_ref, z_ref):
        z_ref[...] = _rms(h_ref[...], g_ref[...]).astype(ACT)

    return pl.pallas_call(
        body, name=name, out_shape=jax.ShapeDtypeStruct((T, D), ACT), grid=(T // tm,),
        in_specs=[pl.BlockSpec((tm, D), lambda i: (i, 0)), pl.BlockSpec((1, D), lambda i: (0, 0))],
        out_specs=pl.BlockSpec((tm, D), lambda i: (i, 0)), compiler_params=_cparams())(h, g)


def _mm_shard(x, w, tm, name, major):
    T, K = x.shape
    S, _, N = w.shape

    def body(x_ref, w_ref, o_ref):
        o_ref[...] = _dot(x_ref[...], w_ref[...]).astype(o_ref.dtype)

    if major:
        out_shape = jax.ShapeDtypeStruct((S, T, N), ACT)
        out_spec = pl.BlockSpec((None, tm, N), lambda j, i: (j, i, 0))
    else:
        out_shape = jax.ShapeDtypeStruct((T, S * N), ACT)
        out_spec = pl.BlockSpec((tm, N), lambda j, i: (i, j))
    return pl.pallas_call(
        body, name=name, out_shape=out_shape, grid=(S, T // tm),
        in_specs=[pl.BlockSpec((tm, K), lambda j, i: (i, 0)), pl.BlockSpec((None, K, N), lambda j, i: (j, 0, 0))],
        out_specs=out_spec, compiler_params=_cparams())(x, w)


def _mm_tn(x, y, n_shards, tm, name, major):
    T, K = x.shape
    S = n_shards
    N = y.shape[-1] if major else y.shape[-1] // S

    def body(x_ref, y_ref, o_ref):
        @pl.when(pl.program_id(1) == 0)
        def _():
            o_ref[...] = jnp.zeros_like(o_ref)
        o_ref[...] += _dot_tn(x_ref[...], y_ref[...])

    y_spec = (pl.BlockSpec((None, tm, N), lambda j, i: (j, i, 0)) if major
              else pl.BlockSpec((tm, N), lambda j, i: (i, j)))
    return pl.pallas_call(
        body, name=name, out_shape=jax.ShapeDtypeStruct((S, K, N), F32), grid=(S, T // tm),
        in_specs=[pl.BlockSpec((tm, K), lambda j, i: (i, 0)), y_spec],
        out_specs=pl.BlockSpec((None, K, N), lambda j, i: (j, 0, 0)), compiler_params=_cparams())(x, y)


def _lin_bwd(x, dy, w, tm, name):
    T, K = x.shape
    N = dy.shape[1]

    def body(x_ref, dy_ref, w_ref, dx_ref, dw_ref):
        @pl.when(pl.program_id(0) == 0)
        def _():
            dw_ref[...] = jnp.zeros_like(dw_ref)
        dy = dy_ref[...]
        dx_ref[...] = _dot_nt(dy, w_ref[...]).astype(dx_ref.dtype)
        dw_ref[...] += _dot_tn(x_ref[...], dy)

    return pl.pallas_call(
        body, name=name,
        out_shape=(jax.ShapeDtypeStruct((T, K), ACT), jax.ShapeDtypeStruct((K, N), F32)), grid=(T // tm,),
        in_specs=[pl.BlockSpec((tm, K), lambda i: (i, 0)), pl.BlockSpec((tm, N), lambda i: (i, 0)),
                  pl.BlockSpec((K, N), lambda i: (0, 0))],
        out_specs=(pl.BlockSpec((tm, K), lambda i: (i, 0)), pl.BlockSpec((K, N), lambda i: (0, 0))),
        compiler_params=_cparams())(x, dy, w)


N_SEG = SUBLANES


def _seg_len(L):
    return -(-L // (N_SEG * SUBLANES)) * SUBLANES


def _to_segments(a3, seg):
    b, length, c = a3.shape
    a = jnp.pad(a3, ((0, 0), (0, N_SEG * seg - length), (0, 0)))
    return a.reshape(b, N_SEG, seg, c).transpose(0, 2, 1, 3).reshape(b, N_SEG * seg, c)


def _from_segments(a3, seg, length):
    b, _, c = a3.shape
    return a3.reshape(b, seg, N_SEG, c).transpose(0, 2, 1, 3).reshape(b, N_SEG * seg, c)[:, :length]


def _seg_scan(x_ref, tab_ref, n_slabs, reverse):
    hw = x_ref.shape[1] // 2
    sign = -1.0 if reverse else 1.0
    ar, ai = tab_ref[0][:, :hw], sign * tab_ref[0][:, hw:]
    br, bi = tab_ref[1][:, :hw], sign * tab_ref[1][:, hw:]

    def slab(k):
        kk = (n_slabs - 1 - k) if reverse else k
        return pl.ds(pl.multiple_of(kk * SUBLANES, SUBLANES), SUBLANES)

    def horner(k, carry):
        cr, ci = carry
        x = x_ref[slab(k), :]
        return ar * cr - ai * ci + x[:, :hw], ar * ci + ai * cr + x[:, hw:]

    z = jnp.zeros((SUBLANES, hw), F32)
    fr, fi = lax.fori_loop(0, n_slabs, horner, (z, z))

    row = lax.broadcasted_iota(jnp.int32, (SUBLANES, hw), 0)
    edge = (row == SUBLANES - 1) if reverse else (row == 0)
    shift = SUBLANES - 1 if reverse else 1
    sr, si = z, z
    for _ in range(N_SEG - 1):
        er, ei = fr + br * sr - bi * si, fi + br * si + bi * sr
        sr = jnp.where(edge, 0.0, pltpu.roll(er, shift, 0))
        si = jnp.where(edge, 0.0, pltpu.roll(ei, shift, 0))

    def scan(k, carry):
        cr, ci = carry
        rows = slab(k)
        x = x_ref[rows, :]
        nr, ni = ar * cr - ai * ci + x[:, :hw], ar * ci + ai * cr + x[:, hw:]
        x_ref[rows, 0:hw] = nr
        x_ref[rows, hw:2 * hw] = ni
        return nr, ni

    lax.fori_loop(0, n_slabs, scan, (sr, si))


def _s5_fwd_call(p3, wb, wc, tab_f, dsk, name):
    B, L, _ = p3.shape
    n_cb, cw, sw = wb.shape

    def body(u_ref, wb_ref, wc_ref, tab_ref, d_ref, ya_ref, so_ref, s_ref):
        u = u_ref[...]
        s_ref[...] = _dot(u, wb_ref[...])
        _seg_scan(s_ref, tab_ref, L // SUBLANES, False)
        s = s_ref[...].astype(MXU)
        so_ref[...] = s
        y = _dot(s, wc_ref[...]) + d_ref[...] * u.astype(F32)
        ya_ref[...] = jax.nn.gelu(y).astype(ACT)

    return pl.pallas_call(
        body, name=name,
        out_shape=(jax.ShapeDtypeStruct((B, L, n_cb * cw), ACT), jax.ShapeDtypeStruct((B, n_cb, L, sw), MXU)),
        grid=(B, n_cb),
        in_specs=[pl.BlockSpec((None, L, cw), lambda b, c: (b, 0, c)),
                  pl.BlockSpec((None, cw, sw), lambda b, c: (c, 0, 0)),
                  pl.BlockSpec((None, sw, cw), lambda b, c: (c, 0, 0)),
                  pl.BlockSpec((None, 2, SUBLANES, sw), lambda b, c: (c, 0, 0, 0)),
                  pl.BlockSpec((None, 1, cw), lambda b, c: (c, 0, 0))],
        out_specs=(pl.BlockSpec((None, L, cw), lambda b, c: (b, 0, c)),
                   pl.BlockSpec((None, None, L, sw), lambda b, c: (b, c, 0, 0))),
        scratch_shapes=[pltpu.VMEM((L, sw), F32)], compiler_params=_cparams())(p3, wb, wc, tab_f, dsk)


def _s5_bwd_call(p3, s_all, dya, wb, wc, tab_r, dsk, name):
    B, L, _ = p3.shape
    n_cb, cw, sw = wb.shape
    hw = sw // 2
    n_slabs = L // SUBLANES

    def body(u_ref, si_ref, dya_ref, wb_ref, wc_ref, tr_ref, d_ref,
             du_ref, dwb_ref, dwc_ref, da_ref, dd_ref, s_ref, l_ref):
        @pl.when(pl.program_id(1) == 0)
        def _():
            dwb_ref[...] = jnp.zeros_like(dwb_ref)
            dwc_ref[...] = jnp.zeros_like(dwc_ref)
            da_ref[...] = jnp.zeros_like(da_ref)
            dd_ref[...] = jnp.zeros_like(dd_ref)

        u = u_ref[...]
        uf = u.astype(F32)
        s_in = si_ref[...]
        s_ref[...] = s_in.astype(F32)
        y = _dot(s_in, wc_ref[...]) + d_ref[...] * uf
        _, gelu_vjp = jax.vjp(jax.nn.gelu, y)
        dy = gelu_vjp(dya_ref[...].astype(F32))[0]
        dd_ref[...] += jnp.sum(dy * uf, axis=0, keepdims=True)
        l_ref[...] = _dot_nt(dy, wc_ref[...])
        _seg_scan(l_ref, tr_ref, n_slabs, True)
        du_ref[...] = (_dot_nt(l_ref[...], wb_ref[...]) + d_ref[...] * dy).astype(ACT)
        dwb_ref[...] += _dot_tn(u, l_ref[...])
        dwc_ref[...] += _dot_tn(s_in, dy)

        row = lax.broadcasted_iota(jnp.int32, (SUBLANES, hw), 0)
        last = s_ref[pl.ds((n_slabs - 1) * SUBLANES, SUBLANES), :]
        p0r = jnp.where(row == 0, 0.0, pltpu.roll(last[:, :hw], 1, 0))
        p0i = jnp.where(row == 0, 0.0, pltpu.roll(last[:, hw:], 1, 0))

        def step(k, carry):
            qr, qi, accr, acci = carry
            r0 = pl.multiple_of(k * SUBLANES, SUBLANES)
            s = s_ref[pl.ds(r0, SUBLANES), :]
            lam = l_ref[pl.ds(r0, SUBLANES), :]
            lr, li = lam[:, :hw], lam[:, hw:]
            accr = accr + lr * qr + li * qi
            acci = acci + li * qr - lr * qi
            return s[:, :hw], s[:, hw:], accr, acci

        z8 = jnp.zeros((SUBLANES, hw), F32)
        _, _, accr, acci = lax.fori_loop(0, n_slabs, step, (p0r, p0i, z8, z8))
        da_ref[...] += jnp.concatenate([jnp.sum(accr, axis=0, keepdims=True),
                                        jnp.sum(acci, axis=0, keepdims=True)], axis=1)

    W = n_cb * cw
    return pl.pallas_call(
        body, name=name,
        out_shape=(jax.ShapeDtypeStruct((B, L, W), ACT), jax.ShapeDtypeStruct((n_cb, cw, sw), F32),
                   jax.ShapeDtypeStruct((n_cb, sw, cw), F32), jax.ShapeDtypeStruct((n_cb, 1, sw), F32),
                   jax.ShapeDtypeStruct((n_cb, 1, cw), F32)),
        grid=(n_cb, B),
        in_specs=[pl.BlockSpec((None, L, cw), lambda c, b: (b, 0, c)),
                  pl.BlockSpec((None, None, L, sw), lambda c, b: (b, c, 0, 0)),
                  pl.BlockSpec((None, L, cw), lambda c, b: (b, 0, c)),
                  pl.BlockSpec((None, cw, sw), lambda c, b: (c, 0, 0)),
                  pl.BlockSpec((None, sw, cw), lambda c, b: (c, 0, 0)),
                  pl.BlockSpec((None, 2, SUBLANES, sw), lambda c, b: (c, 0, 0, 0)),
                  pl.BlockSpec((None, 1, cw), lambda c, b: (c, 0, 0))],
        out_specs=(pl.BlockSpec((None, L, cw), lambda c, b: (b, 0, c)),
                   pl.BlockSpec((None, cw, sw), lambda c, b: (c, 0, 0)),
                   pl.BlockSpec((None, sw, cw), lambda c, b: (c, 0, 0)),
                   pl.BlockSpec((None, 1, sw), lambda c, b: (c, 0, 0)),
                   pl.BlockSpec((None, 1, cw), lambda c, b: (c, 0, 0))),
        scratch_shapes=[pltpu.VMEM((L, sw), F32), pltpu.VMEM((L, sw), F32)],
        compiler_params=_cparams())(p3, s_all, dya, wb, wc, tab_r, dsk)


def _glu_proj_call(ya, wglu, wproj, tm, name):
    T, W = ya.shape
    D = wproj.shape[1]

    def body(ya_ref, wg_ref, wp_ref, yo_ref, a_ref):
        ya = ya_ref[...]
        yo = ya.astype(F32) * jax.nn.sigmoid(_dot(ya, wg_ref[...]))
        yo_ref[...] = yo.astype(ACT)
        a_ref[...] = _dot(yo, wp_ref[...]).astype(ACT)

    return pl.pallas_call(
        body, name=name, out_shape=(jax.ShapeDtypeStruct((T, W), ACT), jax.ShapeDtypeStruct((T, D), ACT)),
        grid=(T // tm,),
        in_specs=[pl.BlockSpec((tm, W), lambda i: (i, 0)), pl.BlockSpec((W, W), lambda i: (0, 0)),
                  pl.BlockSpec((W, D), lambda i: (0, 0))],
        out_specs=(pl.BlockSpec((tm, W), lambda i: (i, 0)), pl.BlockSpec((tm, D), lambda i: (i, 0))),
        compiler_params=_cparams())(ya, wglu, wproj)


def _glu_bwd_call(ya, dyo, wglu, tm, name):
    T, W = ya.shape

    def body(ya_ref, dyo_ref, wg_ref, dya_ref, dwg_ref):
        @pl.when(pl.program_id(0) == 0)
        def _():
            dwg_ref[...] = jnp.zeros_like(dwg_ref)
        ya = ya_ref[...]
        yaf = ya.astype(F32)
        dyo = dyo_ref[...].astype(F32)
        sg = jax.nn.sigmoid(_dot(ya, wg_ref[...]))
        dt = dyo * yaf * sg * (1.0 - sg)
        dya_ref[...] = (dyo * sg + _dot_nt(dt, wg_ref[...])).astype(ACT)
        dwg_ref[...] += _dot_tn(ya, dt)

    return pl.pallas_call(
        body, name=name, out_shape=(jax.ShapeDtypeStruct((T, W), ACT), jax.ShapeDtypeStruct((W, W), F32)),
        grid=(T // tm,),
        in_specs=[pl.BlockSpec((tm, W), lambda i: (i, 0)), pl.BlockSpec((tm, W), lambda i: (i, 0)),
                  pl.BlockSpec((W, W), lambda i: (0, 0))],
        out_specs=(pl.BlockSpec((tm, W), lambda i: (i, 0)), pl.BlockSpec((W, W), lambda i: (0, 0))),
        compiler_params=_cparams())(ya, dyo, wglu)


PAD = 16


def _chunk_cumsums(x, pad_ref, L):
    row = lax.broadcasted_iota(jnp.int32, x.shape, 0) % CHUNK
    zeros = jnp.zeros((PAD, x.shape[1]), F32)
    pad_ref[0:PAD, :] = zeros
    pad_ref[PAD + L:2 * PAD + L, :] = zeros
    c = x
    r = x
    d = 1
    while d < CHUNK:
        pad_ref[PAD:PAD + L, :] = c
        c = c + jnp.where(row >= d, pad_ref[PAD - d:PAD - d + L, :], 0.0)
        pad_ref[PAD:PAD + L, :] = r
        r = r + jnp.where(row + d < CHUNK, pad_ref[PAD + d:PAD + d + L, :], 0.0)
        d *= 2
    return c, r - x


def _hgrn_prep(q_ref, fl_ref, lb_ref, pad_ref, r0, n):
    rows = pl.ds(r0, n)
    lb = lb_ref[...]
    sig = jax.nn.sigmoid(fl_ref[rows, :].astype(F32))
    f = lb + (1.0 - lb) * sig
    k = 1.0 - f
    c, rc = _chunk_cumsums(jnp.log(f), pad_ref, n)
    e_in, e_inv, e_out = jnp.exp(c), jnp.exp(-c), jnp.exp(rc)
    q = q_ref[rows, :].astype(F32)
    return dict(sig=sig, f=f, k=k, q=q, e_in=e_in, e_inv=e_inv, e_out=e_out, dec=jnp.exp(c + rc))


def _for_row_blocks(L, fn):
    full = L // GROUP
    if full:
        def step(g, carry):
            fn(pl.multiple_of(g * GROUP, GROUP), GROUP)
            return carry
        lax.fori_loop(0, full, step, 0)
    if L % GROUP:
        fn(full * GROUP, L % GROUP)


def _chunk_mask(rb):
    r = lax.broadcasted_iota(jnp.int32, (rb, rb), 0)
    c = lax.broadcasted_iota(jnp.int32, (rb, rb), 1)
    return (r // CHUNK == c // CHUNK) & (c <= r)


def _hg_out(o, og, g):
    on = o * lax.rsqrt(jnp.mean(o * o, axis=-1, keepdims=True) + EPS) * g
    return on * _silu(og)


def _hgrn_specs(L, hd, col_q, n_heads, order):
    def spec(sec):
        return pl.BlockSpec((None, L, hd), lambda *g: (order(*g)[0], 0, col_q + sec * n_heads + order(*g)[1]))
    return [spec(0), spec(1), spec(2), spec(3)]


GROUP = 128
CPG = GROUP // CHUNK


def _expand(x):
    xf = x.astype(F32)
    chunk = lax.broadcasted_iota(jnp.int32, xf.shape, 0) // CHUNK
    return jnp.concatenate([jnp.where(chunk == j, xf, 0.0) for j in range(CPG)], axis=1)


def _fill_tail(refs_fills, L):
    for ref, fill in refs_fills:
        if ref.shape[0] > L:
            ref[L:ref.shape[0], :] = jnp.full((ref.shape[0] - L, ref.shape[1]), fill, ref.dtype)


GROUP_UNROLL = 8


def _hgrn_forward_core(q_ref, fl_ref, v_ref, lb_ref, pad_ref, qin_ref, kin_ref, kout_ref, vp_ref, dec_ref, o_ref,
                       s_ref, a_ref, L, keep=()):
    hd = qin_ref.shape[1]
    n_groups = qin_ref.shape[0] // GROUP

    def prep(r0, n):
        pp = _hgrn_prep(q_ref, fl_ref, lb_ref, pad_ref, r0, n)
        rows = pl.ds(r0, n)
        for key, ref in keep:
            ref[rows, :] = pp[key]
        qin_ref[rows, :] = (pp["q"] * pp["e_in"]).astype(MXU)
        kin_ref[rows, :] = (pp["k"] * pp["e_inv"]).astype(MXU)
        kout_ref[rows, :] = (pp["k"] * pp["e_out"]).astype(MXU)
        vp_ref[rows, :] = v_ref[rows, :].astype(MXU)
        dec_ref[rows, :] = pp["dec"]

    _for_row_blocks(L, prep)
    _fill_tail(((qin_ref, 0.0), (kin_ref, 0.0), (kout_ref, 0.0), (vp_ref, 0.0), (dec_ref, 1.0)), L)
    mask = _chunk_mask(GROUP)

    def scores(g, carry):
        rows = pl.ds(pl.multiple_of(g * GROUP, GROUP), GROUP)
        a_ref[rows, :] = jnp.where(mask, _dot_nt(qin_ref[rows, :], kin_ref[rows, :]), 0.0).astype(MXU)
        return carry

    lax.fori_loop(0, n_groups, scores, 0, unroll=GROUP_UNROLL)

    def intra(g, carry):
        rows = pl.ds(pl.multiple_of(g * GROUP, GROUP), GROUP)
        o_ref[rows, :] = _dot(a_ref[rows, :], vp_ref[rows, :])
        kv = _dot_tn(vp_ref[rows, :], _expand(kout_ref[rows, :]))
        for j in range(CPG):
            s_ref[g * CPG + j] = kv[:, j * hd:(j + 1) * hd]
        return carry

    lax.fori_loop(0, n_groups, intra, 0, unroll=GROUP_UNROLL)

    def rec(n, st):
        kv = s_ref[n]
        s_ref[n] = st
        dec = dec_ref[pl.ds(pl.multiple_of(n * CHUNK, CHUNK), SUBLANES), :][0:1]
        return st * dec + kv

    lax.fori_loop(0, L // CHUNK, rec, jnp.zeros((hd, hd), F32))

    def inter(g, carry):
        rows = pl.ds(pl.multiple_of(g * GROUP, GROUP), GROUP)
        scat = jnp.concatenate([s_ref[g * CPG + j] for j in range(CPG)], axis=1)
        o_ref[rows, :] += _dot_nt(_expand(qin_ref[rows, :]), scat)
        return carry

    lax.fori_loop(0, n_groups, inter, 0, unroll=GROUP_UNROLL)


def _hgrn_scratch(L, hd):
    lp = -(-L // GROUP) * GROUP
    return lp, [pltpu.VMEM((GROUP + 2 * PAD, hd), F32), pltpu.VMEM((lp, hd), MXU), pltpu.VMEM((lp, hd), MXU),
                pltpu.VMEM((lp, hd), MXU), pltpu.VMEM((lp, hd), MXU), pltpu.VMEM((lp, hd), F32),
                pltpu.VMEM((lp, hd), F32), pltpu.VMEM((lp // CHUNK, hd, hd), F32), pltpu.VMEM((lp, GROUP), MXU)]


def _hgrn_fwd_call(p3, lb, ng, n_heads, col_q, name):
    B, L, _ = p3.shape
    hd = ng.shape[1]
    _, scratch = _hgrn_scratch(L, hd)

    def body(q_ref, fl_ref, v_ref, og_ref, lb_ref, ng_ref, yb_ref,
             pad_ref, qin_ref, kin_ref, kout_ref, vp_ref, dec_ref, o_ref, s_ref, a_ref):
        _hgrn_forward_core(q_ref, fl_ref, v_ref, lb_ref, pad_ref, qin_ref, kin_ref, kout_ref, vp_ref, dec_ref,
                           o_ref, s_ref, a_ref, L)

        def out(r0, n):
            rows = pl.ds(r0, n)
            yb_ref[rows, :] = _hg_out(o_ref[rows, :], og_ref[rows, :].astype(F32), ng_ref[...]).astype(ACT)

        _for_row_blocks(L, out)

    order = lambda b, h: (b, h)
    return pl.pallas_call(
        body, name=name, out_shape=jax.ShapeDtypeStruct((B, L, n_heads * hd), ACT), grid=(B, n_heads),
        in_specs=_hgrn_specs(L, hd, col_q, n_heads, order) + [
            pl.BlockSpec((1, hd), lambda b, h: (0, h)), pl.BlockSpec((1, hd), lambda b, h: (0, 0))],
        out_specs=pl.BlockSpec((None, L, hd), lambda b, h: (b, 0, h)),
        scratch_shapes=scratch, compiler_params=_cparams())(p3, p3, p3, p3, lb, ng)


def _hgrn_bwd_call(p3, dyb, lb, ng, n_heads, col_q, name):
    B, L, _ = p3.shape
    hd = ng.shape[1]
    n_chunks = L // CHUNK
    lp, scratch = _hgrn_scratch(L, hd)
    n_groups = lp // GROUP

    def body(q_ref, fl_ref, v_ref, og_ref, dyb_ref, lb_ref, ng_ref,
             dq_ref, dfl_ref, dv_ref, dog_ref, dlb_ref, dng_ref,
             pad_ref, qin_ref, kin_ref, kout_ref, vp_ref, dec_ref, o_ref, s_ref, a_ref,
             do_ref, ds_ref, dqi_ref, dki_ref, dko_ref, dvv_ref, dct_ref,
             sig_ref, f_ref, ein_ref, einv_ref, eout_ref, da_ref):
        @pl.when(pl.program_id(1) == 0)
        def _():
            dlb_ref[...] = jnp.zeros_like(dlb_ref)

        @pl.when((pl.program_id(0) == 0) & (pl.program_id(1) == 0))
        def _():
            dng_ref[...] = jnp.zeros_like(dng_ref)

        _hgrn_forward_core(q_ref, fl_ref, v_ref, lb_ref, pad_ref, qin_ref, kin_ref, kout_ref, vp_ref, dec_ref,
                           o_ref, s_ref, a_ref, L, keep=(("sig", sig_ref), ("f", f_ref), ("e_in", ein_ref),
                                                  ("e_inv", einv_ref), ("e_out", eout_ref)))

        def out_bwd(r0, n):
            rows = pl.ds(r0, n)
            _, out_vjp = jax.vjp(_hg_out, o_ref[rows, :], og_ref[rows, :].astype(F32), ng_ref[...])
            d_o, d_og, d_ng = out_vjp(dyb_ref[rows, :].astype(F32))
            dog_ref[rows, :] = d_og.astype(ACT)
            dng_ref[...] += d_ng
            do_ref[rows, :] = d_o.astype(MXU)

        _for_row_blocks(L, out_bwd)
        _fill_tail(((do_ref, 0.0),), L)
        mask = _chunk_mask(GROUP)

        def score_grads(g, carry):
            rows = pl.ds(pl.multiple_of(g * GROUP, GROUP), GROUP)
            da_ref[rows, :] = jnp.where(mask, _dot_nt(do_ref[rows, :], vp_ref[rows, :]), 0.0).astype(MXU)
            return carry

        lax.fori_loop(0, n_groups, score_grads, 0, unroll=GROUP_UNROLL)

        def grads_a(g, carry):
            rows = pl.ds(pl.multiple_of(g * GROUP, GROUP), GROUP)
            qi, ki, do, da = qin_ref[rows, :], kin_ref[rows, :], do_ref[rows, :], da_ref[rows, :]
            sstack = s_ref[pl.ds(g * CPG, CPG)].reshape(CPG * hd, hd)
            dqi_ref[rows, :] = _dot(da, ki) + _dot(_expand(do), sstack)
            dki_ref[rows, :] = _dot_tn(da, qi)
            dvv_ref[rows, :] = _dot_tn(a_ref[rows, :], do)
            x = _dot_tn(do, _expand(qi))
            for j in range(CPG):
                ds_ref[g * CPG + j] = x[:, j * hd:(j + 1) * hd]
            return carry

        lax.fori_loop(0, n_groups, grads_a, 0, unroll=GROUP_UNROLL)

        def rec_bwd(k, dst):
            n = n_chunks - 1 - k
            r0 = pl.multiple_of(n * CHUNK, CHUNK)
            x = ds_ref[n]
            ds_ref[n] = dst
            dec = dec_ref[pl.ds(r0, SUBLANES), :][0:1]
            return dst * dec + x

        lax.fori_loop(0, n_chunks, rec_bwd, jnp.zeros((hd, hd), F32))

        def grads_b(g, carry):
            r0 = pl.multiple_of(g * GROUP, GROUP)
            rows = pl.ds(r0, GROUP)
            ds = [ds_ref[g * CPG + j] for j in range(CPG)]
            dscat = jnp.concatenate(ds, axis=1)
            dvv_ref[rows, :] += _dot_nt(_expand(kout_ref[rows, :]), dscat)
            dstack = ds_ref[pl.ds(g * CPG, CPG)].reshape(CPG * hd, hd)
            dko_ref[rows, :] = _dot(_expand(vp_ref[rows, :]), dstack)
            for j in range(CPG):
                dec = dec_ref[pl.ds(r0 + j * CHUNK, SUBLANES), :][0:1]
                ddec = dec * jnp.sum(ds[j] * s_ref[g * CPG + j], axis=0, keepdims=True)
                dct_ref[pl.ds(r0 + j * CHUNK, CHUNK), :] = jnp.broadcast_to(ddec, (CHUNK, hd))
            return carry

        lax.fori_loop(0, n_groups, grads_b, 0, unroll=GROUP_UNROLL)

        def finish(r0, n):
            rows = pl.ds(r0, n)
            sig, f, e_in, e_inv, e_out = [r[rows, :] for r in (sig_ref, f_ref, ein_ref, einv_ref, eout_ref)]
            q, k = q_ref[rows, :].astype(F32), 1.0 - f
            dqi, dki, dko = dqi_ref[rows, :], dki_ref[rows, :], dko_ref[rows, :]
            dq = dqi * e_in
            dk = dki * e_inv + dko * e_out
            dq_ref[rows, :] = dq.astype(ACT)
            dv_ref[rows, :] = dvv_ref[rows, :].astype(ACT)
            t_out = k * e_out * dko
            dc = q * dq - k * e_inv * dki - t_out
            _, dc_later = _chunk_cumsums(dc, pad_ref, n)
            t_incl, t_later = _chunk_cumsums(t_out, pad_ref, n)
            dlogf = dc + dc_later + t_incl + t_later + dct_ref[rows, :]
            df = dlogf / f - dk
            dfl_ref[rows, :] = (df * (1.0 - lb_ref[...]) * sig * (1.0 - sig)).astype(ACT)
            dlb_ref[...] += jnp.sum(df * (1.0 - sig), axis=0, keepdims=True)

        _for_row_blocks(L, finish)

    order = lambda h, b: (b, h)
    W = n_heads * hd
    act_out = jax.ShapeDtypeStruct((B, L, W), ACT)
    blk_out = pl.BlockSpec((None, L, hd), lambda h, b: (b, 0, h))
    return pl.pallas_call(
        body, name=name,
        out_shape=(act_out, act_out, act_out, act_out, jax.ShapeDtypeStruct((1, W), F32),
                   jax.ShapeDtypeStruct((1, hd), F32)),
        grid=(n_heads, B),
        in_specs=_hgrn_specs(L, hd, col_q, n_heads, order) + [
            pl.BlockSpec((None, L, hd), lambda h, b: (b, 0, h)),
            pl.BlockSpec((1, hd), lambda h, b: (0, h)), pl.BlockSpec((1, hd), lambda h, b: (0, 0))],
        out_specs=(blk_out, blk_out, blk_out, blk_out, pl.BlockSpec((1, hd), lambda h, b: (0, h)),
                   pl.BlockSpec((1, hd), lambda h, b: (0, 0))),
        scratch_shapes=scratch + [
            pltpu.VMEM((lp, hd), MXU), pltpu.VMEM((lp // CHUNK, hd, hd), F32)] + [pltpu.VMEM((lp, hd), F32)] * 10 + [
            pltpu.VMEM((lp, GROUP), MXU)],
        compiler_params=_cparams())(p3, p3, p3, p3, dyb, lb, ng)


def _merge_fn(a, bm, ga, gb):
    return jax.nn.sigmoid(ga) * a + jax.nn.sigmoid(gb) * bm


def _merge_call(yb, a, p, h0, whp, wout, g2, col_ga, tm, name):
    T, D = h0.shape

    def body(yb_ref, a_ref, ga_ref, gb_ref, h0_ref, whp_ref, wout_ref, g2_ref, h1_ref, mg_ref, bm_ref, z2_ref):
        bm = _dot(yb_ref[...], whp_ref[...])
        mg = _merge_fn(a_ref[...].astype(F32), bm, ga_ref[...].astype(F32), gb_ref[...].astype(F32))
        h1 = h0_ref[...] + _dot(mg, wout_ref[...])
        h1_ref[...] = h1
        mg_ref[...] = mg.astype(ACT)
        bm_ref[...] = bm.astype(ACT)
        z2_ref[...] = _rms(h1, g2_ref[...]).astype(ACT)

    tile = pl.BlockSpec((tm, D), lambda i: (i, 0))
    full = pl.BlockSpec((D, D), lambda i: (0, 0))
    act = jax.ShapeDtypeStruct((T, D), ACT)
    return pl.pallas_call(
        body, name=name, out_shape=(jax.ShapeDtypeStruct((T, D), F32), act, act, act), grid=(T // tm,),
        in_specs=[tile, tile, pl.BlockSpec((tm, D), lambda i: (i, col_ga)),
                  pl.BlockSpec((tm, D), lambda i: (i, col_ga + 1)), tile, full, full,
                  pl.BlockSpec((1, D), lambda i: (0, 0))],
        out_specs=(tile, tile, tile, tile), compiler_params=_cparams())(yb, a, p, p, h0, whp, wout, g2)


def _merge_bwd_call(dmg, a, bm, p, col_ga, tm, name):
    T, D = dmg.shape

    def body(dmg_ref, a_ref, bm_ref, ga_ref, gb_ref, da_ref, dbm_ref, dga_ref, dgb_ref):
        args = [r[...].astype(F32) for r in (a_ref, bm_ref, ga_ref, gb_ref)]
        _, vjp = jax.vjp(_merge_fn, *args)
        for r, o in zip((da_ref, dbm_ref, dga_ref, dgb_ref), vjp(dmg_ref[...].astype(F32))):
            r[...] = o.astype(ACT)

    tile = pl.BlockSpec((tm, D), lambda i: (i, 0))
    act = jax.ShapeDtypeStruct((T, D), ACT)
    return pl.pallas_call(
        body, name=name, out_shape=(act, act, act, act), grid=(T // tm,),
        in_specs=[tile, tile, tile, pl.BlockSpec((tm, D), lambda i: (i, col_ga)),
                  pl.BlockSpec((tm, D), lambda i: (i, col_ga + 1))],
        out_specs=(tile, tile, tile, tile), compiler_params=_cparams())(dmg, a, bm, p, p)


def _conv_taps(x_ref, halo_ref, ext_ref, edge, tm, before):
    halo = jnp.where(edge, 0.0, halo_ref[...].astype(F32))
    x = x_ref[...].astype(F32)
    if before:
        ext_ref[0:PAD, :] = halo
        ext_ref[PAD:PAD + tm, :] = x
        return [ext_ref[PAD - 2 + k:PAD - 2 + k + tm, :] for k in range(3)]
    ext_ref[0:tm, :] = x
    ext_ref[tm:tm + PAD, :] = halo
    return [ext_ref[k:k + tm, :] for k in range(3)]


def _conv(taps, cw, cb):
    return cb + cw[0:1] * taps[0] + cw[1:2] * taps[1] + cw[2:3] * taps[2]


def _ffn_pair_specs(tm, F, T, n_pairs, order, before):
    hb = tm // PAD
    last = T // PAD - 1

    def halo_row(i):
        return jnp.maximum(i * hb - 1, 0) if before else jnp.minimum((i + 1) * hb, last)

    specs = []
    for off in (0, n_pairs):
        specs.append(pl.BlockSpec((None, tm, F), lambda *g, off=off: (order(*g)[1] + off, order(*g)[0], 0)))
        specs.append(pl.BlockSpec((None, PAD, F), lambda *g, off=off: (order(*g)[1] + off, halo_row(order(*g)[0]), 0)))
    return specs


def _ffn_fwd_call(up, cw, cb, wd, h1, tgt, g3, tm, tps, name):
    S, T, F = up.shape
    n_pairs = S // 2
    D = h1.shape[1]

    def body(ua_ref, ha_ref, ub_ref, hb_ref, cwa_ref, cwb_ref, cba_ref, cbb_ref, wd_ref, h1_ref, tgt_ref, g3_ref,
             ca_ref, cb_ref, dh2_ref, loss_ref, dg3_ref, acc_ref, ext_ref):
        i, j = pl.program_id(0), pl.program_id(1)
        edge = (i % tps) == 0
        ua = _conv(_conv_taps(ua_ref, ha_ref, ext_ref, edge, tm, True), cwa_ref[...], cba_ref[...])
        ub = _conv(_conv_taps(ub_ref, hb_ref, ext_ref, edge, tm, True), cwb_ref[...], cbb_ref[...])
        ca_ref[...] = ua.astype(ACT)
        cb_ref[...] = ub.astype(ACT)
        contrib = _dot(_silu(ua) * ub, wd_ref[...])

        @pl.when(j == 0)
        def _():
            acc_ref[...] = h1_ref[...] + contrib

        @pl.when(j > 0)
        def _():
            acc_ref[...] += contrib

        @pl.when((i == 0) & (j == 0))
        def _():
            loss_ref[...] = jnp.zeros_like(loss_ref)
            dg3_ref[...] = jnp.zeros_like(dg3_ref)

        @pl.when(j == n_pairs - 1)
        def _():
            row = lax.broadcasted_iota(jnp.int32, (tm, 1), 0) + (i % tps) * tm
            valid = row >= N_META
            tgt = tgt_ref[...]

            def loss_fn(h2, g):
                err = _rms(h2, g) - tgt
                return 0.5 * jnp.sum(jnp.where(valid, err * err, 0.0)) / D

            loss, vjp = jax.vjp(loss_fn, acc_ref[...], g3_ref[...])
            dh2, dg3 = vjp(jnp.ones((), F32))
            dh2_ref[...] = dh2
            loss_ref[...] += loss
            dg3_ref[...] += dg3

    order = lambda i, j: (i, j)
    tile = pl.BlockSpec((tm, D), lambda i, j: (i, 0))
    vec = pl.BlockSpec((1, D), lambda i, j: (0, 0))
    return pl.pallas_call(
        body, name=name,
        out_shape=(jax.ShapeDtypeStruct((n_pairs, T, F), ACT), jax.ShapeDtypeStruct((n_pairs, T, F), ACT),
                   jax.ShapeDtypeStruct((T, D), F32), jax.ShapeDtypeStruct((1, LANES), F32),
                   jax.ShapeDtypeStruct((1, D), F32)),
        grid=(T // tm, n_pairs),
        in_specs=_ffn_pair_specs(tm, F, T, n_pairs, order, True) + [
            pl.BlockSpec((None, 3, F), lambda i, j: (j, 0, 0)), pl.BlockSpec((None, 3, F), lambda i, j: (j + n_pairs, 0, 0)),
            pl.BlockSpec((None, 1, F), lambda i, j: (j, 0, 0)), pl.BlockSpec((None, 1, F), lambda i, j: (j + n_pairs, 0, 0)),
            pl.BlockSpec((None, F, D), lambda i, j: (j, 0, 0)), tile, tile, vec],
        out_specs=(pl.BlockSpec((None, tm, F), lambda i, j: (j, i, 0)), pl.BlockSpec((None, tm, F), lambda i, j: (j, i, 0)),
                   tile, pl.BlockSpec((1, LANES), lambda i, j: (0, 0)), vec),
        scratch_shapes=[pltpu.VMEM((tm, D), F32), pltpu.VMEM((tm + PAD, F), F32)],
        compiler_params=_cparams())(up, up, up, up, cw, cw, cb, cb, wd, h1, tgt, g3)


def _ffn_bwd_a_call(dh2, ca, cb, wd, tm, name):
    n_pairs, T, F = ca.shape
    D = dh2.shape[1]

    def body(dh2_ref, ca_ref, cb_ref, wd_ref, dua_ref, dub_ref, dwd_ref, dcba_ref, dcbb_ref):
        @pl.when(pl.program_id(1) == 0)
        def _():
            for r in (dwd_ref, dcba_ref, dcbb_ref):
                r[...] = jnp.zeros_like(r)

        dh2 = dh2_ref[...]
        ua, ub = ca_ref[...].astype(F32), cb_ref[...].astype(F32)
        sa = jax.nn.sigmoid(ua)
        gate = ua * sa
        dact = _dot_nt(dh2, wd_ref[...])
        dwd_ref[...] += _dot_tn(gate * ub, dh2)
        dub = dact * gate
        dua = dact * ub * sa * (1.0 + ua * (1.0 - sa))
        dcba_ref[...] += jnp.sum(dua, axis=0, keepdims=True)
        dcbb_ref[...] += jnp.sum(dub, axis=0, keepdims=True)
        dua_ref[...] = dua.astype(ACT)
        dub_ref[...] = dub.astype(ACT)

    blk = pl.BlockSpec((None, tm, F), lambda j, i: (j, i, 0))
    vec = pl.BlockSpec((None, 1, F), lambda j, i: (j, 0, 0))
    return pl.pallas_call(
        body, name=name,
        out_shape=(jax.ShapeDtypeStruct((n_pairs, T, F), ACT), jax.ShapeDtypeStruct((n_pairs, T, F), ACT),
                   jax.ShapeDtypeStruct((n_pairs, F, D), F32), jax.ShapeDtypeStruct((n_pairs, 1, F), F32),
                   jax.ShapeDtypeStruct((n_pairs, 1, F), F32)),
        grid=(n_pairs, T // tm),
        in_specs=[pl.BlockSpec((tm, D), lambda j, i: (i, 0)), blk, blk, pl.BlockSpec((None, F, D), lambda j, i: (j, 0, 0))],
        out_specs=(blk, blk, pl.BlockSpec((None, F, D), lambda j, i: (j, 0, 0)), vec, vec),
        compiler_params=_cparams())(dh2, ca, cb, wd)


def _ffn_bwd_b_call(dua, dub, up, cw, wup, h1, g2, dh2, tm, tps, name):
    n_pairs, T, F = dua.shape
    D = h1.shape[1]
    hb = tm // PAD
    last = T // PAD - 1

    def body(da_ref, na_ref, db_ref, nb_ref, ua_ref, ub_ref, cwa_ref, cwb_ref, wa_ref, wb_ref, h1_ref, g2_ref, dh2_ref,
             dupa_ref, dupb_ref, dh1_ref, dg2_ref, dcwa_ref, dcwb_ref, acc_ref, ext_ref):
        i, j = pl.program_id(0), pl.program_id(1)
        edge = (i % tps) == tps - 1

        @pl.when((i == 0) & (j == 0))
        def _():
            dcwa_ref[...] = jnp.zeros_like(dcwa_ref)
            dcwb_ref[...] = jnp.zeros_like(dcwb_ref)

        outs = []
        for d_ref, n_ref, u_ref, cw_ref, o_ref, dcw_ref in (
                (da_ref, na_ref, ua_ref, cwa_ref, dupa_ref, dcwa_ref),
                (db_ref, nb_ref, ub_ref, cwb_ref, dupb_ref, dcwb_ref)):
            t = _conv_taps(d_ref, n_ref, ext_ref, edge, tm, False)
            cwv = cw_ref[...]
            dup = cwv[2:3] * t[0] + cwv[1:2] * t[1] + cwv[0:1] * t[2]
            o_ref[...] = dup.astype(ACT)
            outs.append(dup)
            u = u_ref[...].astype(F32)
            dcw_ref[j] += jnp.concatenate([jnp.sum(u * t[2 - k], axis=0, keepdims=True) for k in range(3)], axis=0)
        contrib = _dot_nt(outs[0], wa_ref[...]) + _dot_nt(outs[1], wb_ref[...])

        @pl.when(j == 0)
        def _():
            acc_ref[...] = contrib

        @pl.when(j > 0)
        def _():
            acc_ref[...] += contrib

        @pl.when((i == 0) & (j == 0))
        def _():
            dg2_ref[...] = jnp.zeros_like(dg2_ref)

        @pl.when(j == n_pairs - 1)
        def _():
            _, vjp = jax.vjp(_rms, h1_ref[...], g2_ref[...])
            dh, dg = vjp(acc_ref[...])
            dh1_ref[...] = dh2_ref[...] + dh
            dg2_ref[...] += dg

    tile = pl.BlockSpec((tm, D), lambda i, j: (i, 0))
    vec = pl.BlockSpec((1, D), lambda i, j: (0, 0))
    pair = lambda: [pl.BlockSpec((None, tm, F), lambda i, j: (j, i, 0)),
                    pl.BlockSpec((None, PAD, F), lambda i, j: (j, jnp.minimum((i + 1) * hb, last), 0))]
    act = jax.ShapeDtypeStruct((n_pairs, T, F), ACT)
    dcw = jax.ShapeDtypeStruct((n_pairs, 3, F), F32)
    dcw_spec = pl.BlockSpec((n_pairs, 3, F), lambda i, j: (0, 0, 0))
    return pl.pallas_call(
        body, name=name,
        out_shape=(act, act, jax.ShapeDtypeStruct((T, D), F32), jax.ShapeDtypeStruct((1, D), F32), dcw, dcw),
        grid=(T // tm, n_pairs),
        in_specs=pair() + pair() + [
            pl.BlockSpec((None, tm, F), lambda i, j: (j, i, 0)), pl.BlockSpec((None, tm, F), lambda i, j: (j + n_pairs, i, 0)),
            pl.BlockSpec((None, 3, F), lambda i, j: (j, 0, 0)), pl.BlockSpec((None, 3, F), lambda i, j: (j + n_pairs, 0, 0)),
            pl.BlockSpec((None, D, F), lambda i, j: (j, 0, 0)), pl.BlockSpec((None, D, F), lambda i, j: (j + n_pairs, 0, 0)),
            tile, vec, tile],
        out_specs=(pl.BlockSpec((None, tm, F), lambda i, j: (j, i, 0)), pl.BlockSpec((None, tm, F), lambda i, j: (j, i, 0)),
                   tile, vec, dcw_spec, dcw_spec),
        scratch_shapes=[pltpu.VMEM((tm, D), F32), pltpu.VMEM((tm + PAD, F), F32)],
        compiler_params=_cparams())(dua, dua, dub, dub, up, up, cw, cw, wup, wup, h1, g2, dh2)


def _in_bwd_call(dp, w_in, h0, g1, dh1, tm, name):
    T, D = h0.shape
    S, _, N = w_in.shape

    def body(dp_ref, w_ref, h0_ref, g1_ref, dh1_ref, dh0_ref, dg1_ref, acc_ref):
        i, j = pl.program_id(0), pl.program_id(1)
        contrib = _dot_nt(dp_ref[...], w_ref[...])

        @pl.when(j == 0)
        def _():
            acc_ref[...] = contrib

        @pl.when(j > 0)
        def _():
            acc_ref[...] += contrib

        @pl.when((i == 0) & (j == 0))
        def _():
            dg1_ref[...] = jnp.zeros_like(dg1_ref)

        @pl.when(j == S - 1)
        def _():
            _, vjp = jax.vjp(_rms, h0_ref[...], g1_ref[...])
            dh, dg = vjp(acc_ref[...])
            dh0_ref[...] = dh1_ref[...] + dh
            dg1_ref[...] += dg

    tile = pl.BlockSpec((tm, D), lambda i, j: (i, 0))
    vec = pl.BlockSpec((1, D), lambda i, j: (0, 0))
    return pl.pallas_call(
        body, name=name, out_shape=(jax.ShapeDtypeStruct((T, D), F32), jax.ShapeDtypeStruct((1, D), F32)),
        grid=(T // tm, S),
        in_specs=[pl.BlockSpec((tm, N), lambda i, j: (i, j)), pl.BlockSpec((None, D, N), lambda i, j: (j, 0, 0)),
                  tile, vec, tile],
        out_specs=(tile, vec), scratch_shapes=[pltpu.VMEM((tm, D), F32)],
        compiler_params=_cparams())(dp, w_in, h0, g1, dh1)


def _meta_grad_call(dh0_3, name):
    B, L, D = dh0_3.shape

    def body(d_ref, o_ref):
        o_ref[...] = jnp.sum(d_ref[...], axis=0)

    return pl.pallas_call(
        body, name=name, out_shape=jax.ShapeDtypeStruct((N_META, D), F32), grid=(1,),
        in_specs=[pl.BlockSpec((B, N_META, D), lambda i: (0, 0, 0))],
        out_specs=pl.BlockSpec((N_META, D), lambda i: (0, 0)), compiler_params=_cparams())(dh0_3)


_RELS = [(dx, dy, dc) for dx in (0, 1) for dy in (0, 1) for dc in (0, 1)][1:]


def _exchange_call(arrs, scatter, name):
    n = len(arrs)
    n_rel = len(_RELS)

    def body(*refs):
        ins, outs = refs[:n], refs[n:2 * n]
        send_sems, recv_sems, loc_sems = refs[2 * n:]
        x, y, c = lax.axis_index("x"), lax.axis_index("y"), lax.axis_index("c")
        me = 4 * x + 2 * y + c
        started = []
        for k in range(n):
            src_me = ins[k].at[me] if scatter else ins[k]
            loc = pltpu.make_async_copy(src_me, outs[k].at[me], loc_sems.at[k])
            loc.start()
            started.append(loc)
        waits = []
        for r, (dx, dy, dc) in enumerate(_RELS):
            px, py, pc = (x + dx) % 2, (y + dy) % 2, (c + dc) % 2
            pid = 4 * px + 2 * py + pc
            for k in range(n):
                s = k * n_rel + r
                src = ins[k].at[pid] if scatter else ins[k]
                cp = pltpu.make_async_remote_copy(
                    src_ref=src, dst_ref=outs[k].at[me], send_sem=send_sems.at[s], recv_sem=recv_sems.at[s],
                    device_id=(px, py, pc), device_id_type=pl.DeviceIdType.MESH)
                cp.start()
                waits.append(pltpu.make_async_remote_copy(
                    src_ref=src, dst_ref=outs[k].at[pid], send_sem=send_sems.at[s], recv_sem=recv_sems.at[s],
                    device_id=(px, py, pc), device_id_type=pl.DeviceIdType.MESH))
        for w in waits:
            w.wait_send()
            w.wait_recv()
        for loc in started:
            loc.wait()

    out_shape = tuple(jax.ShapeDtypeStruct(a.shape if scatter else (N_DEV,) + a.shape, a.dtype) for a in arrs)
    hbm = pl.BlockSpec(memory_space=pl.ANY)
    return pl.pallas_call(
        body, name=name, out_shape=out_shape, in_specs=[hbm] * n, out_specs=tuple([hbm] * n),
        scratch_shapes=[pltpu.SemaphoreType.DMA((n * n_rel,)), pltpu.SemaphoreType.DMA((n * n_rel,)),
                        pltpu.SemaphoreType.DMA((n,))],
        compiler_params=pltpu.CompilerParams(has_side_effects=True))(*arrs)


_HBM = pl.BlockSpec(memory_space=pltpu.HBM)
_SEM = pl.BlockSpec(memory_space=pltpu.SEMAPHORE)
_DATAFLOW = pltpu.SideEffectType.DATAFLOW_SIDE_EFFECTING


def _peer_copies(ins, lands, send_sems, recv_sems, scatter):
    n = len(ins)
    x, y, c = lax.axis_index("x"), lax.axis_index("y"), lax.axis_index("c")
    me = 4 * x + 2 * y + c
    sends, arrivals = [], []
    for r, (dx, dy, dc) in enumerate(_RELS):
        px, py, pc = (x + dx) % 2, (y + dy) % 2, (c + dc) % 2
        pid = 4 * px + 2 * py + pc
        for k in range(n):
            s = k * len(_RELS) + r
            src = ins[k].at[pid] if scatter else ins[k]
            for dst, out in ((lands[k].at[me], sends), (lands[k].at[pid], arrivals)):
                out.append(pltpu.make_async_remote_copy(
                    src_ref=src, dst_ref=dst, send_sem=send_sems.at[s], recv_sem=recv_sems.at[s],
                    device_id=(px, py, pc), device_id_type=pl.DeviceIdType.MESH))
    return sends, arrivals


def _exchange_start(arrs, scatter, name):
    n = len(arrs)
    n_sem = n * len(_RELS)

    def body(*refs):
        ins, lands = refs[:n], refs[n:2 * n]
        send_sems, recv_sems = refs[2 * n], refs[2 * n + 1]
        token = refs[-1]
        sends, _ = _peer_copies(ins, lands, send_sems, recv_sems, scatter)
        for cp in sends:
            cp.start()
        token[...] = jnp.zeros_like(token)

    land_shapes = [a.shape if scatter else (N_DEV,) + a.shape for a in arrs]
    ops = [pltpu.with_memory_space_constraint(a, pltpu.HBM) for a in arrs]
    ops += [pltpu.with_memory_space_constraint(lax.empty(s, a.dtype), pltpu.HBM) for s, a in zip(land_shapes, arrs)]
    out = pl.pallas_call(
        body, name=name,
        out_shape=(pltpu.SemaphoreType.DMA((n_sem,)), pltpu.SemaphoreType.DMA((n_sem,)),
                   *[pltpu.HBM(a.shape, a.dtype) for a in arrs],
                   *[pltpu.HBM(s, a.dtype) for s, a in zip(land_shapes, arrs)],
                   jax.ShapeDtypeStruct((SUBLANES, LANES), F32)),
        in_specs=[_HBM] * (2 * n),
        out_specs=(_SEM, _SEM, *[_HBM] * (2 * n), pl.BlockSpec(memory_space=pltpu.VMEM)),
        input_output_aliases={i: 2 + i for i in range(2 * n)},
        compiler_params=pltpu.CompilerParams(has_side_effects=_DATAFLOW))(*ops)
    return out[0], out[1], list(out[2:2 + n]), list(out[2 + n:2 + 2 * n]), out[-1]


def _exchange_wait(started, after, scatter, name):
    send_sems, recv_sems, srcs, lands, _ = started
    n = len(srcs)

    def body(*refs):
        ins, lands_ = refs[:n], refs[n:2 * n]
        _, arrivals = _peer_copies(ins, lands_, refs[2 * n], refs[2 * n + 1], scatter)
        for cp in arrivals:
            cp.wait_send()
            cp.wait_recv()

    out = pl.pallas_call(
        body, name=name,
        out_shape=(*[pltpu.HBM(a.shape, a.dtype) for a in srcs], *[pltpu.HBM(a.shape, a.dtype) for a in lands]),
        in_specs=[_HBM] * (2 * n) + [_SEM, _SEM, pl.BlockSpec(memory_space=pl.ANY)],
        out_specs=tuple([_HBM] * (2 * n)), input_output_aliases={i: i for i in range(2 * n)},
        compiler_params=pltpu.CompilerParams(has_side_effects=_DATAFLOW))(*srcs, *lands, send_sems, recv_sems, after)
    return list(out[:n]), list(out[n:])


def _place_own_call(srcs, lands, scatter, me, name):
    outs = []
    for k, (src, land) in enumerate(zip(srcs, lands)):
        R, C = land.shape[1:]
        tr = R
        while tr % 32 == 0 and tr * C * land.dtype.itemsize > 2 * 1024 * 1024:
            tr //= 2

        def body(me_ref, s_ref, l_ref, o_ref):
            o_ref[...] = s_ref[...]

        src_spec = (pl.BlockSpec((None, tr, C), lambda i, me_ref: (me_ref[0], i, 0)) if scatter
                    else pl.BlockSpec((tr, C), lambda i, me_ref: (i, 0)))
        outs.append(pl.pallas_call(
            body, name=f"{name}_{k}", out_shape=jax.ShapeDtypeStruct(land.shape, land.dtype),
            grid_spec=pltpu.PrefetchScalarGridSpec(
                num_scalar_prefetch=1, grid=(R // tr,),
                in_specs=[src_spec, pl.BlockSpec(memory_space=pl.ANY)],
                out_specs=pl.BlockSpec((None, tr, C), lambda i, me_ref: (me_ref[0], i, 0))),
            input_output_aliases={2: 0}, compiler_params=_cparams())(me, src, land))
    return outs


def _adamw_shard_call(w, parts, m, v, name):
    R, C = w.shape
    tr = _tile(R, 128) if R % 16 == 0 else R

    def body(w_ref, p_ref, m_ref, v_ref, g_ref, d_ref, nm_ref, nv_ref):
        g = p_ref[0].astype(F32)
        for s in range(1, N_DEV):
            g = g + p_ref[s].astype(F32)
        d, nm, nv = _adamw(w_ref[...], g, m_ref[...], v_ref[...])
        g_ref[...] = g
        d_ref[...] = d
        nm_ref[...] = nm
        nv_ref[...] = nv

    tile = pl.BlockSpec((tr, C), lambda i: (i, 0))
    sh = jax.ShapeDtypeStruct((R, C), F32)
    return pl.pallas_call(
        body, name=name, out_shape=(sh, sh, sh, sh), grid=(R // tr,),
        in_specs=[tile, pl.BlockSpec((N_DEV, tr, C), lambda i: (0, i, 0)), tile, tile],
        out_specs=(tile, tile, tile, tile), compiler_params=_cparams())(w, parts, m, v)


def _pack(arrs, rows_mult=SUBLANES):
    flat = jnp.concatenate([a.reshape(-1).astype(F32) for a in arrs])
    n = flat.shape[0]
    per = rows_mult * LANES
    total = -(-n // per) * per
    return jnp.pad(flat, (0, total - n)).reshape(total // LANES, LANES)


def _unpack(pack, shapes):
    flat = pack.reshape(-1)
    out, off = [], 0
    for s in shapes:
        n = 1
        for d in s:
            n *= d
        out.append(flat[off:off + n].reshape(s))
        off += n
    return out


def kernel(x, meta_tokens, mix_norm_g, w_in, ssm_lambda_re, ssm_lambda_im, ssm_log_dt, ssm_b_re, ssm_b_im, ssm_c_re, ssm_c_im, ssm_d, ssm_w_glu, w_ssm_proj, hgrn_lb_logits, hgrn_norm_g, w_hgrn_proj, w_out, ffn_norm_g, w_up, conv_w, conv_b, w_down, final_norm_g, loss_target, m_meta_tokens, m_mix_norm_g, m_w_in, m_ssm_lambda_re, m_ssm_lambda_im, m_ssm_log_dt, m_ssm_b_re, m_ssm_b_im, m_ssm_c_re, m_ssm_c_im, m_ssm_d, m_ssm_w_glu, m_w_ssm_proj, m_hgrn_lb_logits, m_hgrn_norm_g, m_w_hgrn_proj, m_w_out, m_ffn_norm_g, m_w_up, m_conv_w, m_conv_b, m_w_down, m_final_norm_g, v_meta_tokens, v_mix_norm_g, v_w_in, v_ssm_lambda_re, v_ssm_lambda_im, v_ssm_log_dt, v_ssm_b_re, v_ssm_b_im, v_ssm_c_re, v_ssm_c_im, v_ssm_d, v_ssm_w_glu, v_w_ssm_proj, v_hgrn_lb_logits, v_hgrn_norm_g, v_w_hgrn_proj, v_w_out, v_ffn_norm_g, v_w_up, v_conv_w, v_conv_b, v_w_down, v_final_norm_g):
    args = dict(locals())
    B, S_len, D = x.shape
    L = S_len + N_META
    T = B * L
    tm = _tile(L, ROW_TILE_CAP)
    tps = L // tm
    G, P = ssm_lambda_re.shape[1:]
    H = ssm_b_re.shape[-1]
    W = G * H
    n_cb = W // LANES
    gpb = G // n_cb
    hd = hgrn_norm_g.shape[1]
    n_heads = D // hd
    n_in = w_in.shape[2]
    F = w_up.shape[2]
    assert W == D and n_in % LANES == 0

    me = (4 * lax.axis_index("x") + 2 * lax.axis_index("y") + lax.axis_index("c")).astype(jnp.int32).reshape(1)
    meta_g, cw_g = _exchange_call([meta_tokens, conv_w[0]], False, "gather_small_params")
    ga = _exchange_start([w_in[0].astype(MXU)], False, "gather_a_start")
    gb = _exchange_start(
        [w_up[0].astype(MXU), ssm_w_glu[0].astype(MXU), w_ssm_proj[0].astype(MXU), w_hgrn_proj[0].astype(MXU),
         w_out[0].astype(MXU), w_down[0].astype(MXU)], False, "gather_b_start")
    started_tok = (ga[4] + gb[4])[0:1, 0:1]
    meta_full = meta_g.transpose(1, 0, 2).reshape(N_META, D)
    cb_g = conv_b.reshape(N_DEV, 1, F)

    h0 = jnp.concatenate([jnp.broadcast_to(meta_full[None], (B, N_META, D)), x], axis=1).reshape(T, D)
    tgt = jnp.concatenate([jnp.zeros((B, N_META, D), F32), loss_target], axis=1).reshape(T, D)

    lr, li = ssm_lambda_re[0], ssm_lambda_im[0]
    ldt = ssm_log_dt[0].reshape(G, 1)
    bt_re = ssm_b_re[0].transpose(2, 0, 1).reshape(H, G * P)
    bt_im = ssm_b_im[0].transpose(2, 0, 1).reshape(H, G * P)
    seg = _seg_len(L)
    a_re, a_im, as_re, as_im, coef_re, coef_im = _small_call(
        _disc_a_power(seg), [lr, li, ldt], [((G, P), F32)] * 6, "s5_discretise")
    bbt_re, bbt_im = _small_call(
        _disc_b, [coef_re.reshape(1, G * P), coef_im.reshape(1, G * P), bt_re, bt_im],
        [((H, G * P), F32)] * 2, "s5_input_matrix")
    eye = jnp.eye(gpb, dtype=F32)
    hw = gpb * P

    def expand_b(bbt):
        t = bbt.reshape(H, n_cb, gpb, P).transpose(1, 0, 2, 3)[:, None]
        return (t * eye[None, :, None, :, None]).reshape(n_cb, gpb * H, hw)

    def expand_c(cm):
        t = cm.reshape(n_cb, gpb, H, P).transpose(0, 1, 3, 2)[:, :, :, None]
        return (t * eye[None, :, None, :, None]).reshape(n_cb, hw, gpb * H)

    wb = jnp.concatenate([expand_b(bbt_re), expand_b(bbt_im)], axis=2).astype(MXU)
    wc = jnp.concatenate([expand_c(ssm_c_re[0]), -expand_c(ssm_c_im[0])], axis=1).astype(MXU)
    tab = jnp.stack([jnp.concatenate([a_re.reshape(n_cb, hw), a_im.reshape(n_cb, hw)], axis=1),
                     jnp.concatenate([as_re.reshape(n_cb, hw), as_im.reshape(n_cb, hw)], axis=1)], axis=1)
    tab = jnp.broadcast_to(tab[:, :, None, :], (n_cb, 2, SUBLANES, 2 * hw))
    dsk = ssm_d.reshape(n_cb, 1, LANES)
    lb = _small_call(_lb_fn, [hgrn_lb_logits], [((1, D), F32)], "hgrn_lower_bound")[0]

    z1 = _norm_call(h0, mix_norm_g + started_tok, tm, "mix_norm")
    ready = jnp.concatenate([t[(0,) * (t.ndim - 1)][0:1].astype(F32) for t in (z1, wb, wc, tab, lb, tgt)])
    ga_src, ga_land = _exchange_wait(ga, ready, False, "gather_a_wait")
    win_g = _place_own_call(ga_src, ga_land, False, me, "gather_a_own")[0]
    p = _mm_shard(z1, win_g, tm, "in_proj", False)
    p3 = p.reshape(B, L, p.shape[1])
    u_seg = _to_segments(p3[:, :, :W], seg)
    ya_seg, s_all = _s5_fwd_call(u_seg, wb, wc, tab, dsk, "s5_fwd")
    ya = _from_segments(ya_seg, seg, L).reshape(T, W)
    gb_src, gb_land = _exchange_wait(gb, ya, False, "gather_b_wait")
    gathered = _place_own_call(gb_src, gb_land, False, me, "gather_b_own")
    wup_g = gathered[0]
    wglu_g, wsp_g, whp_g, wout_g = [g.reshape(D, D) for g in gathered[1:5]]
    wdn_g = gathered[5].reshape(N_DEV // 2, 2 * w_down.shape[1], D)
    yo, a_br = _glu_proj_call(ya, wglu_g, wsp_g, tm, "s5_glu_proj")
    yb = _hgrn_fwd_call(p3, lb, hgrn_norm_g, n_heads, n_cb, "hgrn_fwd").reshape(T, D)
    col_ga = 5
    h1, mg, bm, z2 = _merge_call(yb, a_br, p, h0, whp_g, wout_g, ffn_norm_g, col_ga, tm, "merge")
    up = _mm_shard(z2, wup_g, tm, "up_proj", True)
    conv_a, conv_b_out, dh2, loss_part, dg3 = _ffn_fwd_call(up, cw_g, cb_g, wdn_g, h1, tgt, final_norm_g.reshape(1, D),
                                                            tm, tps, "ffn_out_loss")

    dua, dub, dwd, dcba, dcbb = _ffn_bwd_a_call(dh2, conv_a, conv_b_out, wdn_g, tm, "ffn_bwd_gate")
    dupa, dupb, dh1, dg2, dcwa, dcwb = _ffn_bwd_b_call(dua, dub, up, cw_g, wup_g, h1, ffn_norm_g, dh2, tm, tps,
                                                       "ffn_bwd_up")
    dwup = jnp.concatenate([_mm_tn(z2, dupa, N_DEV // 2, tm, "dw_up_a", True),
                            _mm_tn(z2, dupb, N_DEV // 2, tm, "dw_up_b", True)], axis=0)
    sh_rows = D // N_DEV
    sa = _exchange_start([dwup, dwd.reshape(N_DEV, w_down.shape[1], D)], True, "scatter_a_start")
    dmg, dwout = _lin_bwd(mg, dh1, wout_g + sa[4][0:1, 0:1].astype(MXU), tm, "out_proj_bwd")
    da_br, dbm, dga, dgb = _merge_bwd_call(dmg, a_br, bm, p, col_ga, tm, "merge_bwd")
    dyo, dwsp = _lin_bwd(yo, da_br, wsp_g, tm, "ssm_proj_bwd")
    dyb, dwhp = _lin_bwd(yb, dbm, whp_g, tm, "hgrn_proj_bwd")
    dya, dwglu = _glu_bwd_call(ya, dyo, wglu_g, tm, "s5_glu_bwd")
    sb = _exchange_start([t.reshape(N_DEV, sh_rows, D) for t in (dwglu, dwsp, dwhp, dwout)], True, "scatter_b_start")
    tok_b = sb[4][0:1, :]
    du_seg, dwb, dwc, dab, ddsk = _s5_bwd_call(u_seg, s_all, _to_segments(dya.reshape(B, L, W), seg), wb, wc, tab,
                                               dsk + tok_b[None], "s5_bwd")
    du = _from_segments(du_seg, seg, L)

    def diag_b(dw):
        t = (dw.reshape(n_cb, gpb, H, gpb, P) * eye[None, :, None, :, None]).sum(axis=1)
        return t.transpose(1, 0, 2, 3).reshape(H, G * P)

    def diag_c(dw):
        t = (dw.reshape(n_cb, gpb, P, gpb, H) * eye[None, :, None, :, None]).sum(axis=3)
        return t.transpose(0, 1, 3, 2).reshape(G, H, P)

    early_parts = [dab[:, 0, :hw].reshape(G, P), dab[:, 0, hw:].reshape(G, P), ddsk.reshape(1, D)]
    early = [_pack(early_parts), diag_b(dwb[:, :, :hw]), diag_b(dwb[:, :, hw:]),
             diag_c(dwc[:, :hw]).reshape(G * H, P), -diag_c(dwc[:, hw:]).reshape(G * H, P)]
    se = _exchange_start(early, False, "gather_s5_grads_start")
    dq, dfl, di, dog, dlb, dng = _hgrn_bwd_call(p3, dyb.reshape(B, L, D), lb, hgrn_norm_g + tok_b + se[4][0:1, :],
                                                n_heads, n_cb, "hgrn_bwd")
    dp = jnp.concatenate([du.reshape(T, W), dq.reshape(T, D), dfl.reshape(T, D), di.reshape(T, D),
                          dog.reshape(T, D), dga, dgb], axis=1)
    dwin = _mm_tn(z1, dp, N_DEV, tm, "dw_in", False)
    sc = _exchange_start([dwin.astype(WIRE)], True, "scatter_c_start")
    dh0, dg1 = _in_bwd_call(dp, win_g, h0, mix_norm_g + sc[4][0:1, 0:1], dh1, tm, "in_proj_bwd")
    dh0_3 = dh0.reshape(B, L, D)
    grad_x = dh0_3[:, N_META:]
    dmeta = _meta_grad_call(dh0_3, "meta_grad")

    late_parts = [dg1, dlb, dng, dg2, jnp.concatenate([dcba, dcbb], axis=0).reshape(1, N_DEV * F), dg3, loss_part]
    late_pack = _pack(late_parts)

    dcw = jnp.concatenate([dcwa, dcwb], axis=0)
    dmeta_s = dmeta.reshape(N_META, N_DEV, D // N_DEV).transpose(1, 0, 2)
    parts_d = _exchange_call([dmeta_s, dcw], True, "scatter_small_grads")
    late_all = _exchange_call([late_pack], False, "gather_small_grads")[0]
    early_all = _place_own_call(*_exchange_wait(se, late_all, False, "gather_s5_grads_wait"), False, me,
                                "gather_s5_grads_own")
    parts_a = _place_own_call(*_exchange_wait(sa, late_all, True, "scatter_a_wait"), True, me, "scatter_a_own")
    parts_b = _place_own_call(*_exchange_wait(sb, late_all, True, "scatter_b_wait"), True, me, "scatter_b_own")
    parts_c = _place_own_call(*_exchange_wait(sc, late_all, True, "scatter_c_wait"), True, me, "scatter_c_own")
    parts = [parts_c[0], parts_a[0], *parts_b, parts_a[1], parts_d[0], parts_d[1]]

    def sum8(*gathered):
        out = []
        for a in gathered:
            t = a[0]
            for s in range(1, N_DEV):
                t = t + a[s]
            out.append(t)
        return tuple(out)

    sums = _small_call(sum8, [*early_all, late_all], [(a.shape, F32) for a in (*early, late_pack)], "sum_small_grads")
    t_abr, t_abi, g_dsk = _unpack(sums[0], [a.shape for a in early_parts])
    t_bbr, t_bbi = sums[1], sums[2]
    g_cre, g_cim = sums[3].reshape(G, H, P), sums[4].reshape(G, H, P)
    g_g1, t_lb, g_ng, g_g2, g_cb, g_g3, loss_v = _unpack(sums[5], [a.shape for a in late_parts])

    def disc_b_bwd(cr, ci, br, bi, dbr, dbi):
        _, vjp = jax.vjp(_disc_b, cr, ci, br, bi)
        return vjp((dbr, dbi))

    t_cr, t_ci, g_btr, g_bti = _small_call(
        disc_b_bwd, [coef_re.reshape(1, G * P), coef_im.reshape(1, G * P), bt_re, bt_im, t_bbr, t_bbi],
        [((1, G * P), F32)] * 2 + [((H, G * P), F32)] * 2, "s5_input_matrix_bwd")

    def disc_a_bwd(lr_, li_, ldt_, dar, dai, dcr, dci):
        _, vjp = jax.vjp(_disc_a, lr_, li_, ldt_)
        return vjp((dar, dai, dcr, dci))

    g_lr, g_li, g_ldt = _small_call(
        disc_a_bwd, [lr, li, ldt, t_abr, t_abi, t_cr.reshape(G, P), t_ci.reshape(G, P)],
        [((G, P), F32)] * 2 + [((G, 1), F32)], "s5_discretise_bwd")

    def lb_bwd(logits, d):
        _, vjp = jax.vjp(_lb_fn, logits)
        return vjp(d)

    g_lbl = _small_call(lb_bwd, [hgrn_lb_logits, t_lb], [(hgrn_lb_logits.shape, F32)], "hgrn_lower_bound_bwd")[0]

    grads = dict(
        mix_norm_g=g_g1, ssm_lambda_re=g_lr[None], ssm_lambda_im=g_li[None], ssm_log_dt=g_ldt.reshape(1, G),
        ssm_b_re=g_btr.reshape(H, G, P).transpose(1, 2, 0)[None], ssm_b_im=g_bti.reshape(H, G, P).transpose(1, 2, 0)[None],
        ssm_c_re=g_cre[None], ssm_c_im=g_cim[None], ssm_d=g_dsk, hgrn_lb_logits=g_lbl, hgrn_norm_g=g_ng,
        ffn_norm_g=g_g2, conv_b=g_cb.reshape(1, N_DEV * F), final_norm_g=g_g3.reshape(D))
    loss = loss_v[0, 0]

    delta, new_m, new_v = {}, {}, {}
    sharded = [("w_in", parts[0], (D, n_in)), ("w_up", parts[1], (D, F)), ("ssm_w_glu", parts[2], (sh_rows, D)),
               ("w_ssm_proj", parts[3], (sh_rows, D)), ("w_hgrn_proj", parts[4], (sh_rows, D)),
               ("w_out", parts[5], (sh_rows, D)), ("w_down", parts[6], (w_down.shape[1], D)),
               ("meta_tokens", parts[7], (N_META, D // N_DEV)), ("conv_w", parts[8], (3, F))]
    for name, part, shp in sharded:
        full = args[name].shape
        g, d_, nm, nv = _adamw_shard_call(args[name].reshape(shp), part, args["m_" + name].reshape(shp),
                                          args["v_" + name].reshape(shp), "adamw_" + name)
        grads[name], delta[name], new_m[name], new_v[name] = [t.reshape(full) for t in (g, d_, nm, nv)]

    for n, shp in (("ssm_b_re", (G * P, H)), ("ssm_b_im", (G * P, H)), ("ssm_c_re", (G * H, P)), ("ssm_c_im", (G * H, P))):
        outs = _small_call(_adamw, [t.reshape(shp) for t in (args[n], grads[n], args["m_" + n], args["v_" + n])],
                           [(shp, F32)] * 3, "adamw_" + n)
        delta[n], new_m[n], new_v[n] = [o.reshape(args[n].shape) for o in outs]
    rep = ["mix_norm_g", "ssm_lambda_re", "ssm_lambda_im", "ssm_log_dt", "ssm_d", "hgrn_lb_logits", "hgrn_norm_g",
           "ffn_norm_g", "conv_b", "final_norm_g"]
    rep_shapes = [args[n].shape for n in rep]
    packs = [_pack([args[pre + n] for n in rep]) for pre in ("", "m_", "v_")]
    g_pack = _pack([grads[n] for n in rep])
    outs = _small_call(lambda w, g, m, v: _adamw(w, g, m, v), [packs[0], g_pack, packs[1], packs[2]],
                       [(g_pack.shape, F32)] * 3, "adamw_replicated")
    for n, d_, nm, nv in zip(rep, *[_unpack(o, rep_shapes) for o in outs]):
        delta[n], new_m[n], new_v[n] = d_, nm, nv

    names = ["meta_tokens", "mix_norm_g", "w_in", "ssm_lambda_re", "ssm_lambda_im", "ssm_log_dt", "ssm_b_re",
             "ssm_b_im", "ssm_c_re", "ssm_c_im", "ssm_d", "ssm_w_glu", "w_ssm_proj", "hgrn_lb_logits", "hgrn_norm_g",
             "w_hgrn_proj", "w_out", "ffn_norm_g", "w_up", "conv_w", "conv_b", "w_down", "final_norm_g"]
    return (loss, grad_x, *[grads[n] for n in names], *[delta[n] for n in names],
            *[new_m[n] for n in names], *[new_v[n] for n in names])
```

```python
import jax
import jax.numpy as jnp
from jax import lax
from jax.experimental import pallas as pl
from jax.experimental.pallas import tpu as pltpu

F32 = jnp.float32
MXU = jnp.bfloat16
ACT = jnp.bfloat16
WIRE = jnp.bfloat16
N_DEV = 8
N_META = 16
CHUNK = 16
EPS = 1e-6
ADAM_LR, ADAM_B1, ADAM_B2, ADAM_EPS, ADAM_WD, ADAM_STEP = 0.001, 0.9, 0.999, 1e-08, 0.01, 10
SUBLANES = 8
LANES = 128
ROW_TILE_CAP = 700
VMEM_LIMIT = 60 * 1024 * 1024


def _cparams(**kw):
    return pltpu.CompilerParams(vmem_limit_bytes=VMEM_LIMIT, **kw)


def _tile(n, cap):
    best = None
    for t in range(16, min(n, cap) + 1, 16):
        if n % t == 0:
            best = t
    assert best is not None, (n, cap)
    return best


def _dot(a, b):
    return lax.dot_general(a.astype(MXU), b.astype(MXU), (((1,), (0,)), ((), ())), preferred_element_type=F32)


def _dot_nt(a, b):
    return lax.dot_general(a.astype(MXU), b.astype(MXU), (((1,), (1,)), ((), ())), preferred_element_type=F32)


def _dot_tn(a, b):
    return lax.dot_general(a.astype(MXU), b.astype(MXU), (((0,), (0,)), ((), ())), preferred_element_type=F32)


def _rms(x, g):
    return x * lax.rsqrt(jnp.mean(x * x, axis=-1, keepdims=True) + EPS) * g


def _silu(x):
    return x * jax.nn.sigmoid(x)


def _small_call(fn, ins, out_shapes, name):
    n_in = len(ins)

    def body(*refs):
        outs = fn(*[r[...] for r in refs[:n_in]])
        outs = outs if isinstance(outs, (tuple, list)) else (outs,)
        for r, o in zip(refs[n_in:], outs):
            r[...] = o.astype(r.dtype)

    vm = pl.BlockSpec(memory_space=pltpu.VMEM)
    return pl.pallas_call(
        body, name=name, out_shape=tuple(jax.ShapeDtypeStruct(s, d) for s, d in out_shapes),
        in_specs=[vm] * n_in, out_specs=tuple([vm] * len(out_shapes)), compiler_params=_cparams())(*ins)


def _disc_a(lr, li, ldt):
    dt = jnp.exp(ldt)
    mag = jnp.exp(lr * dt)
    ab_re = mag * jnp.cos(li * dt)
    ab_im = mag * jnp.sin(li * dt)
    den = lr * lr + li * li
    nr = ab_re - 1.0
    coef_re = (nr * lr + ab_im * li) / den
    coef_im = (ab_im * lr - nr * li) / den
    return ab_re, ab_im, coef_re, coef_im


def _disc_a_power(n):
    def fn(lr, li, ldt):
        ab_re, ab_im, coef_re, coef_im = _disc_a(lr, li, ldt)
        pr, pi, sr, si, m = None, None, ab_re, ab_im, n
        while m:
            if m & 1:
                pr, pi = (sr, si) if pr is None else (pr * sr - pi * si, pr * si + pi * sr)
            m >>= 1
            if m:
                sr, si = sr * sr - si * si, 2.0 * sr * si
        return ab_re, ab_im, pr, pi, coef_re, coef_im
    return fn


def _disc_b(coef_re, coef_im, bt_re, bt_im):
    return coef_re * bt_re - coef_im * bt_im, coef_re * bt_im + coef_im * bt_re


def _lb_fn(logits):
    return jax.nn.softmax(logits, axis=0)[0:1]


def _adamw(w, g, m, v):
    m = ADAM_B1 * m + (1.0 - ADAM_B1) * g
    v = ADAM_B2 * v + (1.0 - ADAM_B2) * jnp.square(g)
    m_hat = m / (1.0 - ADAM_B1 ** ADAM_STEP)
    v_hat = v / (1.0 - ADAM_B2 ** ADAM_STEP)
    delta = -ADAM_LR * (m_hat / (jnp.sqrt(v_hat) + ADAM_EPS) + ADAM_WD * w)
    return delta, m, v


def _norm_call(h, g, tm, name):
    T, D = h.shape

    def body(h_ref, g_ref, z_ref):
        z_ref[...] = _rms(h_ref[...], g_ref[...]).astype(ACT)

    return pl.pallas_call(
        body, name=name, out_shape=jax.ShapeDtypeStruct((T, D), ACT), grid=(T // tm,),
        in_specs=[pl.BlockSpec((tm, D), lambda i: (i, 0)), pl.BlockSpec((1, D), lambda i: (0, 0))],
        out_specs=pl.BlockSpec((tm, D), lambda i: (i, 0)), compiler_params=_cparams())(h, g)


def _mm_shard(x, w, tm, name, major):
    T, K = x.shape
    S, _, N = w.shape

    def body(x_ref, w_ref, o_ref):
        o_ref[...] = _dot(x_ref[...], w_ref[...]).astype(o_ref.dtype)

    if major:
        out_shape = jax.ShapeDtypeStruct((S, T, N), ACT)
        out_spec = pl.BlockSpec((None, tm, N), lambda j, i: (j, i, 0))
    else:
        out_shape = jax.ShapeDtypeStruct((T, S * N), ACT)
        out_spec = pl.BlockSpec((tm, N), lambda j, i: (i, j))
    return pl.pallas_call(
        body, name=name, out_shape=out_shape, grid=(S, T // tm),
        in_specs=[pl.BlockSpec((tm, K), lambda j, i: (i, 0)), pl.BlockSpec((None, K, N), lambda j, i: (j, 0, 0))],
        out_specs=out_spec, compiler_params=_cparams())(x, w)


def _mm_tn(x, y, n_shards, tm, name, major):
    T, K = x.shape
    S = n_shards
    N = y.shape[-1] if major else y.shape[-1] // S

    def body(x_ref, y_ref, o_ref):
        @pl.when(pl.program_id(1) == 0)
        def _():
            o_ref[...] = jnp.zeros_like(o_ref)
        o_ref[...] += _dot_tn(x_ref[...], y_ref[...])

    y_spec = (pl.BlockSpec((None, tm, N), lambda j, i: (j, i, 0)) if major
              else pl.BlockSpec((tm, N), lambda j, i: (i, j)))
    return pl.pallas_call(
        body, name=name, out_shape=jax.ShapeDtypeStruct((S, K, N), F32), grid=(S, T // tm),
        in_specs=[pl.BlockSpec((tm, K), lambda j, i: (i, 0)), y_spec],
        out_specs=pl.BlockSpec((None, K, N), lambda j, i: (j, 0, 0)), compiler_params=_cparams())(x, y)


def _lin_bwd(x, dy, w, tm, name):
    T, K = x.shape
    N = dy.shape[1]

    def body(x_ref, dy_ref, w_ref, dx_ref, dw_ref):
        @pl.when(pl.program_id(0) == 0)
        def _():
            dw_ref[...] = jnp.zeros_like(dw_ref)
        dy = dy_ref[...]
        dx_ref[...] = _dot_nt(dy, w_ref[...]).astype(dx_ref.dtype)
        dw_ref[...] += _dot_tn(x_ref[...], dy)

    return pl.pallas_call(
        body, name=name,
        out_shape=(jax.ShapeDtypeStruct((T, K), ACT), jax.ShapeDtypeStruct((K, N), F32)), grid=(T // tm,),
        in_specs=[pl.BlockSpec((tm, K), lambda i: (i, 0)), pl.BlockSpec((tm, N), lambda i: (i, 0)),
                  pl.BlockSpec((K, N), lambda i: (0, 0))],
        out_specs=(pl.BlockSpec((tm, K), lambda i: (i, 0)), pl.BlockSpec((K, N), lambda i: (0, 0))),
        compiler_params=_cparams())(x, dy, w)


N_SEG = SUBLANES


def _chain_loop(n, step, init):
    per = next(u for u in (4, 3, 2, 1) if n % u == 0)

    def trip(t, carry):
        for u in range(per):
            carry = step(t * per + u, carry)
        return carry

    return lax.fori_loop(0, n // per, trip, init)


def _seg_len(L):
    return -(-L // (N_SEG * SUBLANES)) * SUBLANES


def _to_segments(a3, seg):
    b, length, c = a3.shape
    a = jnp.pad(a3, ((0, 0), (0, N_SEG * seg - length), (0, 0)))
    return a.reshape(b, N_SEG, seg, c).transpose(0, 2, 1, 3).reshape(b, N_SEG * seg, c)


def _from_segments(a3, seg, length):
    b, _, c = a3.shape
    return a3.reshape(b, seg, N_SEG, c).transpose(0, 2, 1, 3).reshape(b, N_SEG * seg, c)[:, :length]


def _seg_scan(x_ref, tab_ref, n_slabs, reverse):
    hw = x_ref.shape[1] // 2
    sign = -1.0 if reverse else 1.0
    ar, ai = tab_ref[0][:, :hw], sign * tab_ref[0][:, hw:]
    br, bi = tab_ref[1][:, :hw], sign * tab_ref[1][:, hw:]

    def slab(k):
        kk = (n_slabs - 1 - k) if reverse else k
        return pl.ds(pl.multiple_of(kk * SUBLANES, SUBLANES), SUBLANES)

    def horner(k, carry):
        cr, ci = carry
        x = x_ref[slab(k), :]
        return ar * cr - ai * ci + x[:, :hw], ar * ci + ai * cr + x[:, hw:]

    z = jnp.zeros((SUBLANES, hw), F32)
    fr, fi = _chain_loop(n_slabs, horner, (z, z))

    row = lax.broadcasted_iota(jnp.int32, (SUBLANES, hw), 0)
    edge = (row == SUBLANES - 1) if reverse else (row == 0)
    shift = SUBLANES - 1 if reverse else 1
    sr, si = z, z
    for _ in range(N_SEG - 1):
        er, ei = fr + br * sr - bi * si, fi + br * si + bi * sr
        sr = jnp.where(edge, 0.0, pltpu.roll(er, shift, 0))
        si = jnp.where(edge, 0.0, pltpu.roll(ei, shift, 0))

    def scan(k, carry):
        cr, ci = carry
        rows = slab(k)
        x = x_ref[rows, :]
        nr, ni = ar * cr - ai * ci + x[:, :hw], ar * ci + ai * cr + x[:, hw:]
        x_ref[rows, 0:hw] = nr
        x_ref[rows, hw:2 * hw] = ni
        return nr, ni

    _chain_loop(n_slabs, scan, (sr, si))


def _s5_fwd_call(p3, wb, wc, tab_f, dsk, name):
    B, L, _ = p3.shape
    n_cb, cw, sw = wb.shape

    def body(u_ref, wb_ref, wc_ref, tab_ref, d_ref, ya_ref, so_ref, s_ref):
        u = u_ref[...]
        s_ref[...] = _dot(u, wb_ref[...])
        _seg_scan(s_ref, tab_ref, L // SUBLANES, False)
        s = s_ref[...].astype(MXU)
        so_ref[...] = s
        y = _dot(s, wc_ref[...]) + d_ref[...] * u.astype(F32)
        ya_ref[...] = jax.nn.gelu(y).astype(ACT)

    return pl.pallas_call(
        body, name=name,
        out_shape=(jax.ShapeDtypeStruct((B, L, n_cb * cw), ACT), jax.ShapeDtypeStruct((B, n_cb, L, sw), MXU)),
        grid=(B, n_cb),
        in_specs=[pl.BlockSpec((None, L, cw), lambda b, c: (b, 0, c)),
                  pl.BlockSpec((None, cw, sw), lambda b, c: (c, 0, 0)),
                  pl.BlockSpec((None, sw, cw), lambda b, c: (c, 0, 0)),
                  pl.BlockSpec((None, 2, SUBLANES, sw), lambda b, c: (c, 0, 0, 0)),
                  pl.BlockSpec((None, 1, cw), lambda b, c: (c, 0, 0))],
        out_specs=(pl.BlockSpec((None, L, cw), lambda b, c: (b, 0, c)),
                   pl.BlockSpec((None, None, L, sw), lambda b, c: (b, c, 0, 0))),
        scratch_shapes=[pltpu.VMEM((L, sw), F32)], compiler_params=_cparams())(p3, wb, wc, tab_f, dsk)


def _s5_bwd_call(p3, s_all, dya, wb, wc, tab_r, dsk, name):
    B, L, _ = p3.shape
    n_cb, cw, sw = wb.shape
    hw = sw // 2
    n_slabs = L // SUBLANES

    def body(u_ref, si_ref, dya_ref, wb_ref, wc_ref, tr_ref, d_ref,
             du_ref, dwb_ref, dwc_ref, da_ref, dd_ref, s_ref, l_ref):
        @pl.when(pl.program_id(1) == 0)
        def _():
            dwb_ref[...] = jnp.zeros_like(dwb_ref)
            dwc_ref[...] = jnp.zeros_like(dwc_ref)
            da_ref[...] = jnp.zeros_like(da_ref)
            dd_ref[...] = jnp.zeros_like(dd_ref)

        u = u_ref[...]
        uf = u.astype(F32)
        s_in = si_ref[...]
        s_ref[...] = s_in.astype(F32)
        y = _dot(s_in, wc_ref[...]) + d_ref[...] * uf
        _, gelu_vjp = jax.vjp(jax.nn.gelu, y)
        dy = gelu_vjp(dya_ref[...].astype(F32))[0]
        dd_ref[...] += jnp.sum(dy * uf, axis=0, keepdims=True)
        l_ref[...] = _dot_nt(dy, wc_ref[...])
        _seg_scan(l_ref, tr_ref, n_slabs, True)
        du_ref[...] = (_dot_nt(l_ref[...], wb_ref[...]) + d_ref[...] * dy).astype(ACT)
        dwb_ref[...] += _dot_tn(u, l_ref[...])
        dwc_ref[...] += _dot_tn(s_in, dy)

        row = lax.broadcasted_iota(jnp.int32, (SUBLANES, hw), 0)
        last = s_ref[pl.ds((n_slabs - 1) * SUBLANES, SUBLANES), :]
        p0r = jnp.where(row == 0, 0.0, pltpu.roll(last[:, :hw], 1, 0))
        p0i = jnp.where(row == 0, 0.0, pltpu.roll(last[:, hw:], 1, 0))

        def step(k, carry):
            qr, qi, accr, acci = carry
            r0 = pl.multiple_of(k * SUBLANES, SUBLANES)
            s = s_ref[pl.ds(r0, SUBLANES), :]
            lam = l_ref[pl.ds(r0, SUBLANES), :]
            lr, li = lam[:, :hw], lam[:, hw:]
            accr = accr + lr * qr + li * qi
            acci = acci + li * qr - lr * qi
            return s[:, :hw], s[:, hw:], accr, acci

        z8 = jnp.zeros((SUBLANES, hw), F32)
        _, _, accr, acci = _chain_loop(n_slabs, step, (p0r, p0i, z8, z8))
        da_ref[...] += jnp.concatenate([jnp.sum(accr, axis=0, keepdims=True),
                                        jnp.sum(acci, axis=0, keepdims=True)], axis=1)

    W = n_cb * cw
    return pl.pallas_call(
        body, name=name,
        out_shape=(jax.ShapeDtypeStruct((B, L, W), ACT), jax.ShapeDtypeStruct((n_cb, cw, sw), F32),
                   jax.ShapeDtypeStruct((n_cb, sw, cw), F32), jax.ShapeDtypeStruct((n_cb, 1, sw), F32),
                   jax.ShapeDtypeStruct((n_cb, 1, cw), F32)),
        grid=(n_cb, B),
        in_specs=[pl.BlockSpec((None, L, cw), lambda c, b: (b, 0, c)),
                  pl.BlockSpec((None, None, L, sw), lambda c, b: (b, c, 0, 0)),
                  pl.BlockSpec((None, L, cw), lambda c, b: (b, 0, c)),
                  pl.BlockSpec((None, cw, sw), lambda c, b: (c, 0, 0)),
                  pl.BlockSpec((None, sw, cw), lambda c, b: (c, 0, 0)),
                  pl.BlockSpec((None, 2, SUBLANES, sw), lambda c, b: (c, 0, 0, 0)),
                  pl.BlockSpec((None, 1, cw), lambda c, b: (c, 0, 0))],
        out_specs=(pl.BlockSpec((None, L, cw), lambda c, b: (b, 0, c)),
                   pl.BlockSpec((None, cw, sw), lambda c, b: (c, 0, 0)),
                   pl.BlockSpec((None, sw, cw), lambda c, b: (c, 0, 0)),
                   pl.BlockSpec((None, 1, sw), lambda c, b: (c, 0, 0)),
                   pl.BlockSpec((None, 1, cw), lambda c, b: (c, 0, 0))),
        scratch_shapes=[pltpu.VMEM((L, sw), F32), pltpu.VMEM((L, sw), F32)],
        compiler_params=_cparams())(p3, s_all, dya, wb, wc, tab_r, dsk)


def _glu_proj_call(ya, wglu, wproj, tm, name):
    T, W = ya.shape
    D = wproj.shape[1]

    def body(ya_ref, wg_ref, wp_ref, yo_ref, a_ref):
        ya = ya_ref[...]
        yo = ya.astype(F32) * jax.nn.sigmoid(_dot(ya, wg_ref[...]))
        yo_ref[...] = yo.astype(ACT)
        a_ref[...] = _dot(yo, wp_ref[...]).astype(ACT)

    return pl.pallas_call(
        body, name=name, out_shape=(jax.ShapeDtypeStruct((T, W), ACT), jax.ShapeDtypeStruct((T, D), ACT)),
        grid=(T // tm,),
        in_specs=[pl.BlockSpec((tm, W), lambda i: (i, 0)), pl.BlockSpec((W, W), lambda i: (0, 0)),
                  pl.BlockSpec((W, D), lambda i: (0, 0))],
        out_specs=(pl.BlockSpec((tm, W), lambda i: (i, 0)), pl.BlockSpec((tm, D), lambda i: (i, 0))),
        compiler_params=_cparams())(ya, wglu, wproj)


def _glu_bwd_call(ya, dyo, wglu, tm, name):
    T, W = ya.shape

    def body(ya_ref, dyo_ref, wg_ref, dya_ref, dwg_ref):
        @pl.when(pl.program_id(0) == 0)
        def _():
            dwg_ref[...] = jnp.zeros_like(dwg_ref)
        ya = ya_ref[...]
        yaf = ya.astype(F32)
        dyo = dyo_ref[...].astype(F32)
        sg = jax.nn.sigmoid(_dot(ya, wg_ref[...]))
        dt = dyo * yaf * sg * (1.0 - sg)
        dya_ref[...] = (dyo * sg + _dot_nt(dt, wg_ref[...])).astype(ACT)
        dwg_ref[...] += _dot_tn(ya, dt)

    return pl.pallas_call(
        body, name=name, out_shape=(jax.ShapeDtypeStruct((T, W), ACT), jax.ShapeDtypeStruct((W, W), F32)),
        grid=(T // tm,),
        in_specs=[pl.BlockSpec((tm, W), lambda i: (i, 0)), pl.BlockSpec((tm, W), lambda i: (i, 0)),
                  pl.BlockSpec((W, W), lambda i: (0, 0))],
        out_specs=(pl.BlockSpec((tm, W), lambda i: (i, 0)), pl.BlockSpec((W, W), lambda i: (0, 0))),
        compiler_params=_cparams())(ya, dyo, wglu)


PAD = 16


def _chunk_cumsums(x, pad_ref, L):
    row = lax.broadcasted_iota(jnp.int32, x.shape, 0) % CHUNK
    zeros = jnp.zeros((PAD, x.shape[1]), F32)
    pad_ref[0:PAD, :] = zeros
    pad_ref[PAD + L:2 * PAD + L, :] = zeros
    c = x
    r = x
    d = 1
    while d < CHUNK:
        pad_ref[PAD:PAD + L, :] = c
        c = c + jnp.where(row >= d, pad_ref[PAD - d:PAD - d + L, :], 0.0)
        pad_ref[PAD:PAD + L, :] = r
        r = r + jnp.where(row + d < CHUNK, pad_ref[PAD + d:PAD + d + L, :], 0.0)
        d *= 2
    return c, r - x


def _hgrn_prep(q_ref, fl_ref, lb_ref, pad_ref, r0, n):
    rows = pl.ds(r0, n)
    lb = lb_ref[...]
    sig = jax.nn.sigmoid(fl_ref[rows, :].astype(F32))
    f = lb + (1.0 - lb) * sig
    k = 1.0 - f
    c, rc = _chunk_cumsums(jnp.log(f), pad_ref, n)
    e_in, e_inv, e_out = jnp.exp(c), jnp.exp(-c), jnp.exp(rc)
    q = q_ref[rows, :].astype(F32)
    return dict(sig=sig, f=f, k=k, q=q, e_in=e_in, e_inv=e_inv, e_out=e_out, dec=jnp.exp(c + rc))


def _for_row_blocks(L, fn):
    full = L // GROUP
    if full:
        def step(g, carry):
            fn(pl.multiple_of(g * GROUP, GROUP), GROUP)
            return carry
        lax.fori_loop(0, full, step, 0)
    if L % GROUP:
        fn(full * GROUP, L % GROUP)


def _chunk_mask(rb):
    r = lax.broadcasted_iota(jnp.int32, (rb, rb), 0)
    c = lax.broadcasted_iota(jnp.int32, (rb, rb), 1)
    return (r // CHUNK == c // CHUNK) & (c <= r)


def _hg_out(o, og, g):
    on = o * lax.rsqrt(jnp.mean(o * o, axis=-1, keepdims=True) + EPS) * g
    return on * _silu(og)


def _hgrn_specs(L, hd, col_q, n_heads, order):
    def spec(sec):
        return pl.BlockSpec((None, L, hd), lambda *g: (order(*g)[0], 0, col_q + sec * n_heads + order(*g)[1]))
    return [spec(0), spec(1), spec(2), spec(3)]


GROUP = 128
CPG = GROUP // CHUNK


def _expand(x):
    xf = x.astype(F32)
    chunk = lax.broadcasted_iota(jnp.int32, xf.shape, 0) // CHUNK
    return jnp.concatenate([jnp.where(chunk == j, xf, 0.0) for j in range(CPG)], axis=1)


def _fill_tail(refs_fills, L):
    for ref, fill in refs_fills:
        if ref.shape[0] > L:
            ref[L:ref.shape[0], :] = jnp.full((ref.shape[0] - L, ref.shape[1]), fill, ref.dtype)


GROUP_UNROLL = 8


def _hgrn_forward_core(q_ref, fl_ref, v_ref, lb_ref, pad_ref, qin_ref, kin_ref, kout_ref, vp_ref, dec_ref, o_ref,
                       s_ref, a_ref, L, keep=()):
    hd = qin_ref.shape[1]
    n_groups = qin_ref.shape[0] // GROUP

    def prep(r0, n):
        pp = _hgrn_prep(q_ref, fl_ref, lb_ref, pad_ref, r0, n)
        rows = pl.ds(r0, n)
        for key, ref in keep:
            ref[rows, :] = pp[key]
        qin_ref[rows, :] = (pp["q"] * pp["e_in"]).astype(MXU)
        kin_ref[rows, :] = (pp["k"] * pp["e_inv"]).astype(MXU)
        kout_ref[rows, :] = (pp["k"] * pp["e_out"]).astype(MXU)
        vp_ref[rows, :] = v_ref[rows, :].astype(MXU)
        dec_ref[rows, :] = pp["dec"]

    _for_row_blocks(L, prep)
    _fill_tail(((qin_ref, 0.0), (kin_ref, 0.0), (kout_ref, 0.0), (vp_ref, 0.0), (dec_ref, 1.0)), L)
    mask = _chunk_mask(GROUP)

    def scores(g, carry):
        rows = pl.ds(pl.multiple_of(g * GROUP, GROUP), GROUP)
        a_ref[rows, :] = jnp.where(mask, _dot_nt(qin_ref[rows, :], kin_ref[rows, :]), 0.0).astype(MXU)
        return carry

    lax.fori_loop(0, n_groups, scores, 0, unroll=GROUP_UNROLL)

    def intra(g, carry):
        rows = pl.ds(pl.multiple_of(g * GROUP, GROUP), GROUP)
        o_ref[rows, :] = _dot(a_ref[rows, :], vp_ref[rows, :])
        kv = _dot_tn(vp_ref[rows, :], _expand(kout_ref[rows, :]))
        for j in range(CPG):
            s_ref[g * CPG + j] = kv[:, j * hd:(j + 1) * hd]
        return carry

    lax.fori_loop(0, n_groups, intra, 0, unroll=GROUP_UNROLL)

    def rec(n, st):
        kv = s_ref[n]
        s_ref[n] = st
        dec = dec_ref[pl.ds(pl.multiple_of(n * CHUNK, CHUNK), SUBLANES), :][0:1]
        return st * dec + kv

    _chain_loop(L // CHUNK, rec, jnp.zeros((hd, hd), F32))

    def inter(g, carry):
        rows = pl.ds(pl.multiple_of(g * GROUP, GROUP), GROUP)
        scat = jnp.concatenate([s_ref[g * CPG + j] for j in range(CPG)], axis=1)
        o_ref[rows, :] += _dot_nt(_expand(qin_ref[rows, :]), scat)
        return carry

    lax.fori_loop(0, n_groups, inter, 0, unroll=GROUP_UNROLL)


def _hgrn_scratch(L, hd):
    lp = -(-L // GROUP) * GROUP
    return lp, [pltpu.VMEM((GROUP + 2 * PAD, hd), F32), pltpu.VMEM((lp, hd), MXU), pltpu.VMEM((lp, hd), MXU),
                pltpu.VMEM((lp, hd), MXU), pltpu.VMEM((lp, hd), MXU), pltpu.VMEM((lp, hd), F32),
                pltpu.VMEM((lp, hd), F32), pltpu.VMEM((lp // CHUNK, hd, hd), F32), pltpu.VMEM((lp, GROUP), MXU)]


def _hgrn_fwd_call(p3, lb, ng, n_heads, col_q, name):
    B, L, _ = p3.shape
    hd = ng.shape[1]
    _, scratch = _hgrn_scratch(L, hd)

    def body(q_ref, fl_ref, v_ref, og_ref, lb_ref, ng_ref, yb_ref,
             pad_ref, qin_ref, kin_ref, kout_ref, vp_ref, dec_ref, o_ref, s_ref, a_ref):
        _hgrn_forward_core(q_ref, fl_ref, v_ref, lb_ref, pad_ref, qin_ref, kin_ref, kout_ref, vp_ref, dec_ref,
                           o_ref, s_ref, a_ref, L)

        def out(r0, n):
            rows = pl.ds(r0, n)
            yb_ref[rows, :] = _hg_out(o_ref[rows, :], og_ref[rows, :].astype(F32), ng_ref[...]).astype(ACT)

        _for_row_blocks(L, out)

    order = lambda b, h: (b, h)
    return pl.pallas_call(
        body, name=name, out_shape=jax.ShapeDtypeStruct((B, L, n_heads * hd), ACT), grid=(B, n_heads),
        in_specs=_hgrn_specs(L, hd, col_q, n_heads, order) + [
            pl.BlockSpec((1, hd), lambda b, h: (0, h)), pl.BlockSpec((1, hd), lambda b, h: (0, 0))],
        out_specs=pl.BlockSpec((None, L, hd), lambda b, h: (b, 0, h)),
        scratch_shapes=scratch, compiler_params=_cparams())(p3, p3, p3, p3, lb, ng)


def _hgrn_bwd_call(p3, dyb, lb, ng, n_heads, col_q, name):
    B, L, _ = p3.shape
    hd = ng.shape[1]
    n_chunks = L // CHUNK
    lp, scratch = _hgrn_scratch(L, hd)
    n_groups = lp // GROUP

    def body(q_ref, fl_ref, v_ref, og_ref, dyb_ref, lb_ref, ng_ref,
             dq_ref, dfl_ref, dv_ref, dog_ref, dlb_ref, dng_ref,
             pad_ref, qin_ref, kin_ref, kout_ref, vp_ref, dec_ref, o_ref, s_ref, a_ref,
             do_ref, ds_ref, dqi_ref, dki_ref, dko_ref, dvv_ref, dct_ref,
             sig_ref, f_ref, ein_ref, einv_ref, eout_ref, da_ref):
        @pl.when(pl.program_id(1) == 0)
        def _():
            dlb_ref[...] = jnp.zeros_like(dlb_ref)

        @pl.when((pl.program_id(0) == 0) & (pl.program_id(1) == 0))
        def _():
            dng_ref[...] = jnp.zeros_like(dng_ref)

        _hgrn_forward_core(q_ref, fl_ref, v_ref, lb_ref, pad_ref, qin_ref, kin_ref, kout_ref, vp_ref, dec_ref,
                           o_ref, s_ref, a_ref, L, keep=(("sig", sig_ref), ("f", f_ref), ("e_in", ein_ref),
                                                  ("e_inv", einv_ref), ("e_out", eout_ref)))

        def out_bwd(r0, n):
            rows = pl.ds(r0, n)
            _, out_vjp = jax.vjp(_hg_out, o_ref[rows, :], og_ref[rows, :].astype(F32), ng_ref[...])
            d_o, d_og, d_ng = out_vjp(dyb_ref[rows, :].astype(F32))
            dog_ref[rows, :] = d_og.astype(ACT)
            dng_ref[...] += d_ng
            do_ref[rows, :] = d_o.astype(MXU)

        _for_row_blocks(L, out_bwd)
        _fill_tail(((do_ref, 0.0),), L)
        mask = _chunk_mask(GROUP)

        def score_grads(g, carry):
            rows = pl.ds(pl.multiple_of(g * GROUP, GROUP), GROUP)
            da_ref[rows, :] = jnp.where(mask, _dot_nt(do_ref[rows, :], vp_ref[rows, :]), 0.0).astype(MXU)
            return carry

        lax.fori_loop(0, n_groups, score_grads, 0, unroll=GROUP_UNROLL)

        def grads_a(g, carry):
            rows = pl.ds(pl.multiple_of(g * GROUP, GROUP), GROUP)
            qi, ki, do, da = qin_ref[rows, :], kin_ref[rows, :], do_ref[rows, :], da_ref[rows, :]
            sstack = s_ref[pl.ds(g * CPG, CPG)].reshape(CPG * hd, hd)
            dqi_ref[rows, :] = _dot(da, ki) + _dot(_expand(do), sstack)
            dki_ref[rows, :] = _dot_tn(da, qi)
            dvv_ref[rows, :] = _dot_tn(a_ref[rows, :], do)
            x = _dot_tn(do, _expand(qi))
            for j in range(CPG):
                ds_ref[g * CPG + j] = x[:, j * hd:(j + 1) * hd]
            return carry

        lax.fori_loop(0, n_groups, grads_a, 0, unroll=GROUP_UNROLL)

        def rec_bwd(k, dst):
            n = n_chunks - 1 - k
            r0 = pl.multiple_of(n * CHUNK, CHUNK)
            x = ds_ref[n]
            ds_ref[n] = dst
            dec = dec_ref[pl.ds(r0, SUBLANES), :][0:1]
            return dst * dec + x

        _chain_loop(n_chunks, rec_bwd, jnp.zeros((hd, hd), F32))

        def grads_b(g, carry):
            r0 = pl.multiple_of(g * GROUP, GROUP)
            rows = pl.ds(r0, GROUP)
            ds = [ds_ref[g * CPG + j] for j in range(CPG)]
            dscat = jnp.concatenate(ds, axis=1)
            dvv_ref[rows, :] += _dot_nt(_expand(kout_ref[rows, :]), dscat)
            dstack = ds_ref[pl.ds(g * CPG, CPG)].reshape(CPG * hd, hd)
            dko_ref[rows, :] = _dot(_expand(vp_ref[rows, :]), dstack)
            for j in range(CPG):
                dec = dec_ref[pl.ds(r0 + j * CHUNK, SUBLANES), :][0:1]
                ddec = dec * jnp.sum(ds[j] * s_ref[g * CPG + j], axis=0, keepdims=True)
                dct_ref[pl.ds(r0 + j * CHUNK, CHUNK), :] = jnp.broadcast_to(ddec, (CHUNK, hd))
            return carry

        lax.fori_loop(0, n_groups, grads_b, 0, unroll=GROUP_UNROLL)

        def finish(r0, n):
            rows = pl.ds(r0, n)
            sig, f, e_in, e_inv, e_out = [r[rows, :] for r in (sig_ref, f_ref, ein_ref, einv_ref, eout_ref)]
            q, k = q_ref[rows, :].astype(F32), 1.0 - f
            dqi, dki, dko = dqi_ref[rows, :], dki_ref[rows, :], dko_ref[rows, :]
            dq = dqi * e_in
            dk = dki * e_inv + dko * e_out
            dq_ref[rows, :] = dq.astype(ACT)
            dv_ref[rows, :] = dvv_ref[rows, :].astype(ACT)
            t_out = k * e_out * dko
            dc = q * dq - k * e_inv * dki - t_out
            _, dc_later = _chunk_cumsums(dc, pad_ref, n)
            t_incl, t_later = _chunk_cumsums(t_out, pad_ref, n)
            dlogf = dc + dc_later + t_incl + t_later + dct_ref[rows, :]
            df = dlogf / f - dk
            dfl_ref[rows, :] = (df * (1.0 - lb_ref[...]) * sig * (1.0 - sig)).astype(ACT)
            dlb_ref[...] += jnp.sum(df * (1.0 - sig), axis=0, keepdims=True)

        _for_row_blocks(L, finish)

    order = lambda h, b: (b, h)
    W = n_heads * hd
    act_out = jax.ShapeDtypeStruct((B, L, W), ACT)
    blk_out = pl.BlockSpec((None, L, hd), lambda h, b: (b, 0, h))
    return pl.pallas_call(
        body, name=name,
        out_shape=(act_out, act_out, act_out, act_out, jax.ShapeDtypeStruct((1, W), F32),
                   jax.ShapeDtypeStruct((1, hd), F32)),
        grid=(n_heads, B),
        in_specs=_hgrn_specs(L, hd, col_q, n_heads, order) + [
            pl.BlockSpec((None, L, hd), lambda h, b: (b, 0, h)),
            pl.BlockSpec((1, hd), lambda h, b: (0, h)), pl.BlockSpec((1, hd), lambda h, b: (0, 0))],
        out_specs=(blk_out, blk_out, blk_out, blk_out, pl.BlockSpec((1, hd), lambda h, b: (0, h)),
                   pl.BlockSpec((1, hd), lambda h, b: (0, 0))),
        scratch_shapes=scratch + [
            pltpu.VMEM((lp, hd), MXU), pltpu.VMEM((lp // CHUNK, hd, hd), F32)] + [pltpu.VMEM((lp, hd), F32)] * 10 + [
            pltpu.VMEM((lp, GROUP), MXU)],
        compiler_params=_cparams())(p3, p3, p3, p3, dyb, lb, ng)


def _merge_fn(a, bm, ga, gb):
    return jax.nn.sigmoid(ga) * a + jax.nn.sigmoid(gb) * bm


def _merge_call(yb, a, p, h0, whp, wout, g2, col_ga, tm, name):
    T, D = h0.shape

    def body(yb_ref, a_ref, ga_ref, gb_ref, h0_ref, whp_ref, wout_ref, g2_ref, h1_ref, mg_ref, bm_ref, z2_ref):
        bm = _dot(yb_ref[...], whp_ref[...])
        mg = _merge_fn(a_ref[...].astype(F32), bm, ga_ref[...].astype(F32), gb_ref[...].astype(F32))
        h1 = h0_ref[...] + _dot(mg, wout_ref[...])
        h1_ref[...] = h1
        mg_ref[...] = mg.astype(ACT)
        bm_ref[...] = bm.astype(ACT)
        z2_ref[...] = _rms(h1, g2_ref[...]).astype(ACT)

    tile = pl.BlockSpec((tm, D), lambda i: (i, 0))
    full = pl.BlockSpec((D, D), lambda i: (0, 0))
    act = jax.ShapeDtypeStruct((T, D), ACT)
    return pl.pallas_call(
        body, name=name, out_shape=(jax.ShapeDtypeStruct((T, D), F32), act, act, act), grid=(T // tm,),
        in_specs=[tile, tile, pl.BlockSpec((tm, D), lambda i: (i, col_ga)),
                  pl.BlockSpec((tm, D), lambda i: (i, col_ga + 1)), tile, full, full,
                  pl.BlockSpec((1, D), lambda i: (0, 0))],
        out_specs=(tile, tile, tile, tile), compiler_params=_cparams())(yb, a, p, p, h0, whp, wout, g2)


def _merge_bwd_call(dmg, a, bm, p, col_ga, tm, name):
    T, D = dmg.shape

    def body(dmg_ref, a_ref, bm_ref, ga_ref, gb_ref, da_ref, dbm_ref, dga_ref, dgb_ref):
        args = [r[...].astype(F32) for r in (a_ref, bm_ref, ga_ref, gb_ref)]
        _, vjp = jax.vjp(_merge_fn, *args)
        for r, o in zip((da_ref, dbm_ref, dga_ref, dgb_ref), vjp(dmg_ref[...].astype(F32))):
            r[...] = o.astype(ACT)

    tile = pl.BlockSpec((tm, D), lambda i: (i, 0))
    act = jax.ShapeDtypeStruct((T, D), ACT)
    return pl.pallas_call(
        body, name=name, out_shape=(act, act, act, act), grid=(T // tm,),
        in_specs=[tile, tile, tile, pl.BlockSpec((tm, D), lambda i: (i, col_ga)),
                  pl.BlockSpec((tm, D), lambda i: (i, col_ga + 1))],
        out_specs=(tile, tile, tile, tile), compiler_params=_cparams())(dmg, a, bm, p, p)


def _conv_taps(x_ref, halo_ref, ext_ref, edge, tm, before):
    halo = jnp.where(edge, 0.0, halo_ref[...].astype(F32))
    x = x_ref[...].astype(F32)
    if before:
        ext_ref[0:PAD, :] = halo
        ext_ref[PAD:PAD + tm, :] = x
        return [ext_ref[PAD - 2 + k:PAD - 2 + k + tm, :] for k in range(3)]
    ext_ref[0:tm, :] = x
    ext_ref[tm:tm + PAD, :] = halo
    return [ext_ref[k:k + tm, :] for k in range(3)]


def _conv(taps, cw, cb):
    return cb + cw[0:1] * taps[0] + cw[1:2] * taps[1] + cw[2:3] * taps[2]


def _ffn_pair_specs(tm, F, T, n_pairs, order, before):
    hb = tm // PAD
    last = T // PAD - 1

    def halo_row(i):
        return jnp.maximum(i * hb - 1, 0) if before else jnp.minimum((i + 1) * hb, last)

    specs = []
    for off in (0, n_pairs):
        specs.append(pl.BlockSpec((None, tm, F), lambda *g, off=off: (order(*g)[1] + off, order(*g)[0], 0)))
        specs.append(pl.BlockSpec((None, PAD, F), lambda *g, off=off: (order(*g)[1] + off, halo_row(order(*g)[0]), 0)))
    return specs


def _ffn_fwd_call(up, cw, cb, wd, h1, tgt, g3, tm, tps, name):
    S, T, F = up.shape
    n_pairs = S // 2
    D = h1.shape[1]

    def body(ua_ref, ha_ref, ub_ref, hb_ref, cwa_ref, cwb_ref, cba_ref, cbb_ref, wd_ref, h1_ref, tgt_ref, g3_ref,
             ca_ref, cb_ref, dh2_ref, loss_ref, dg3_ref, acc_ref, ext_ref):
        i, j = pl.program_id(0), pl.program_id(1)
        edge = (i % tps) == 0
        ua = _conv(_conv_taps(ua_ref, ha_ref, ext_ref, edge, tm, True), cwa_ref[...], cba_ref[...])
        ub = _conv(_conv_taps(ub_ref, hb_ref, ext_ref, edge, tm, True), cwb_ref[...], cbb_ref[...])
        ca_ref[...] = ua.astype(ACT)
        cb_ref[...] = ub.astype(ACT)
        contrib = _dot(_silu(ua) * ub, wd_ref[...])

        @pl.when(j == 0)
        def _():
            acc_ref[...] = h1_ref[...] + contrib

        @pl.when(j > 0)
        def _():
            acc_ref[...] += contrib

        @pl.when((i == 0) & (j == 0))
        def _():
            loss_ref[...] = jnp.zeros_like(loss_ref)
            dg3_ref[...] = jnp.zeros_like(dg3_ref)

        @pl.when(j == n_pairs - 1)
        def _():
            row = lax.broadcasted_iota(jnp.int32, (tm, 1), 0) + (i % tps) * tm
            valid = row >= N_META
            tgt = tgt_ref[...]

            def loss_fn(h2, g):
                err = _rms(h2, g) - tgt
                return 0.5 * jnp.sum(jnp.where(valid, err * err, 0.0)) / D

            loss, vjp = jax.vjp(loss_fn, acc_ref[...], g3_ref[...])
            dh2, dg3 = vjp(jnp.ones((), F32))
            dh2_ref[...] = dh2
            loss_ref[...] += loss
            dg3_ref[...] += dg3

    order = lambda i, j: (i, j)
    tile = pl.BlockSpec((tm, D), lambda i, j: (i, 0))
    vec = pl.BlockSpec((1, D), lambda i, j: (0, 0))
    return pl.pallas_call(
        body, name=name,
        out_shape=(jax.ShapeDtypeStruct((n_pairs, T, F), ACT), jax.ShapeDtypeStruct((n_pairs, T, F), ACT),
                   jax.ShapeDtypeStruct((T, D), F32), jax.ShapeDtypeStruct((1, LANES), F32),
                   jax.ShapeDtypeStruct((1, D), F32)),
        grid=(T // tm, n_pairs),
        in_specs=_ffn_pair_specs(tm, F, T, n_pairs, order, True) + [
            pl.BlockSpec((None, 3, F), lambda i, j: (j, 0, 0)), pl.BlockSpec((None, 3, F), lambda i, j: (j + n_pairs, 0, 0)),
            pl.BlockSpec((None, 1, F), lambda i, j: (j, 0, 0)), pl.BlockSpec((None, 1, F), lambda i, j: (j + n_pairs, 0, 0)),
            pl.BlockSpec((None, F, D), lambda i, j: (j, 0, 0)), tile, tile, vec],
        out_specs=(pl.BlockSpec((None, tm, F), lambda i, j: (j, i, 0)), pl.BlockSpec((None, tm, F), lambda i, j: (j, i, 0)),
                   tile, pl.BlockSpec((1, LANES), lambda i, j: (0, 0)), vec),
        scratch_shapes=[pltpu.VMEM((tm, D), F32), pltpu.VMEM((tm + PAD, F), F32)],
        compiler_params=_cparams())(up, up, up, up, cw, cw, cb, cb, wd, h1, tgt, g3)


def _ffn_bwd_a_call(dh2, ca, cb, wd, tm, name):
    n_pairs, T, F = ca.shape
    D = dh2.shape[1]

    def body(dh2_ref, ca_ref, cb_ref, wd_ref, dua_ref, dub_ref, dwd_ref, dcba_ref, dcbb_ref):
        @pl.when(pl.program_id(1) == 0)
        def _():
            for r in (dwd_ref, dcba_ref, dcbb_ref):
                r[...] = jnp.zeros_like(r)

        dh2 = dh2_ref[...]
        ua, ub = ca_ref[...].astype(F32), cb_ref[...].astype(F32)
        sa = jax.nn.sigmoid(ua)
        gate = ua * sa
        dact = _dot_nt(dh2, wd_ref[...])
        dwd_ref[...] += _dot_tn(gate * ub, dh2)
        dub = dact * gate
        dua = dact * ub * sa * (1.0 + ua * (1.0 - sa))
        dcba_ref[...] += jnp.sum(dua, axis=0, keepdims=True)
        dcbb_ref[...] += jnp.sum(dub, axis=0, keepdims=True)
        dua_ref[...] = dua.astype(ACT)
        dub_ref[...] = dub.astype(ACT)

    blk = pl.BlockSpec((None, tm, F), lambda j, i: (j, i, 0))
    vec = pl.BlockSpec((None, 1, F), lambda j, i: (j, 0, 0))
    return pl.pallas_call(
        body, name=name,
        out_shape=(jax.ShapeDtypeStruct((n_pairs, T, F), ACT), jax.ShapeDtypeStruct((n_pairs, T, F), ACT),
                   jax.ShapeDtypeStruct((n_pairs, F, D), F32), jax.ShapeDtypeStruct((n_pairs, 1, F), F32),
                   jax.ShapeDtypeStruct((n_pairs, 1, F), F32)),
        grid=(n_pairs, T // tm),
        in_specs=[pl.BlockSpec((tm, D), lambda j, i: (i, 0)), blk, blk, pl.BlockSpec((None, F, D), lambda j, i: (j, 0, 0))],
        out_specs=(blk, blk, pl.BlockSpec((None, F, D), lambda j, i: (j, 0, 0)), vec, vec),
        compiler_params=_cparams())(dh2, ca, cb, wd)


def _ffn_bwd_b_call(dua, dub, up, cw, wup, h1, g2, dh2, tm, tps, name):
    n_pairs, T, F = dua.shape
    D = h1.shape[1]
    hb = tm // PAD
    last = T // PAD - 1

    def body(da_ref, na_ref, db_ref, nb_ref, ua_ref, ub_ref, cwa_ref, cwb_ref, wa_ref, wb_ref, h1_ref, g2_ref, dh2_ref,
             dupa_ref, dupb_ref, dh1_ref, dg2_ref, dcwa_ref, dcwb_ref, acc_ref, ext_ref):
        i, j = pl.program_id(0), pl.program_id(1)
        edge = (i % tps) == tps - 1

        @pl.when((i == 0) & (j == 0))
        def _():
            dcwa_ref[...] = jnp.zeros_like(dcwa_ref)
            dcwb_ref[...] = jnp.zeros_like(dcwb_ref)

        outs = []
        for d_ref, n_ref, u_ref, cw_ref, o_ref, dcw_ref in (
                (da_ref, na_ref, ua_ref, cwa_ref, dupa_ref, dcwa_ref),
                (db_ref, nb_ref, ub_ref, cwb_ref, dupb_ref, dcwb_ref)):
            t = _conv_taps(d_ref, n_ref, ext_ref, edge, tm, False)
            cwv = cw_ref[...]
            dup = cwv[2:3] * t[0] + cwv[1:2] * t[1] + cwv[0:1] * t[2]
            o_ref[...] = dup.astype(ACT)
            outs.append(dup)
            u = u_ref[...].astype(F32)
            dcw_ref[j] += jnp.concatenate([jnp.sum(u * t[2 - k], axis=0, keepdims=True) for k in range(3)], axis=0)
        contrib = _dot_nt(outs[0], wa_ref[...]) + _dot_nt(outs[1], wb_ref[...])

        @pl.when(j == 0)
        def _():
            acc_ref[...] = contrib

        @pl.when(j > 0)
        def _():
            acc_ref[...] += contrib

        @pl.when((i == 0) & (j == 0))
        def _():
            dg2_ref[...] = jnp.zeros_like(dg2_ref)

        @pl.when(j == n_pairs - 1)
        def _():
            _, vjp = jax.vjp(_rms, h1_ref[...], g2_ref[...])
            dh, dg = vjp(acc_ref[...])
            dh1_ref[...] = dh2_ref[...] + dh
            dg2_ref[...] += dg

    tile = pl.BlockSpec((tm, D), lambda i, j: (i, 0))
    vec = pl.BlockSpec((1, D), lambda i, j: (0, 0))
    pair = lambda: [pl.BlockSpec((None, tm, F), lambda i, j: (j, i, 0)),
                    pl.BlockSpec((None, PAD, F), lambda i, j: (j, jnp.minimum((i + 1) * hb, last), 0))]
    act = jax.ShapeDtypeStruct((n_pairs, T, F), ACT)
    dcw = jax.ShapeDtypeStruct((n_pairs, 3, F), F32)
    dcw_spec = pl.BlockSpec((n_pairs, 3, F), lambda i, j: (0, 0, 0))
    return pl.pallas_call(
        body, name=name,
        out_shape=(act, act, jax.ShapeDtypeStruct((T, D), F32), jax.ShapeDtypeStruct((1, D), F32), dcw, dcw),
        grid=(T // tm, n_pairs),
        in_specs=pair() + pair() + [
            pl.BlockSpec((None, tm, F), lambda i, j: (j, i, 0)), pl.BlockSpec((None, tm, F), lambda i, j: (j + n_pairs, i, 0)),
            pl.BlockSpec((None, 3, F), lambda i, j: (j, 0, 0)), pl.BlockSpec((None, 3, F), lambda i, j: (j + n_pairs, 0, 0)),
            pl.BlockSpec((None, D, F), lambda i, j: (j, 0, 0)), pl.BlockSpec((None, D, F), lambda i, j: (j + n_pairs, 0, 0)),
            tile, vec, tile],
        out_specs=(pl.BlockSpec((None, tm, F), lambda i, j: (j, i, 0)), pl.BlockSpec((None, tm, F), lambda i, j: (j, i, 0)),
                   tile, vec, dcw_spec, dcw_spec),
        scratch_shapes=[pltpu.VMEM((tm, D), F32), pltpu.VMEM((tm + PAD, F), F32)],
        compiler_params=_cparams())(dua, dua, dub, dub, up, up, cw, cw, wup, wup, h1, g2, dh2)


def _in_bwd_call(dp, w_in, h0, g1, dh1, tm, name):
    T, D = h0.shape
    S, _, N = w_in.shape

    def body(dp_ref, w_ref, h0_ref, g1_ref, dh1_ref, dh0_ref, dg1_ref, acc_ref):
        i, j = pl.program_id(0), pl.program_id(1)
        contrib = _dot_nt(dp_ref[...], w_ref[...])

        @pl.when(j == 0)
        def _():
            acc_ref[...] = contrib

        @pl.when(j > 0)
        def _():
            acc_ref[...] += contrib

        @pl.when((i == 0) & (j == 0))
        def _():
            dg1_ref[...] = jnp.zeros_like(dg1_ref)

        @pl.when(j == S - 1)
        def _():
            _, vjp = jax.vjp(_rms, h0_ref[...], g1_ref[...])
            dh, dg = vjp(acc_ref[...])
            dh0_ref[...] = dh1_ref[...] + dh
            dg1_ref[...] += dg

    tile = pl.BlockSpec((tm, D), lambda i, j: (i, 0))
    vec = pl.BlockSpec((1, D), lambda i, j: (0, 0))
    return pl.pallas_call(
        body, name=name, out_shape=(jax.ShapeDtypeStruct((T, D), F32), jax.ShapeDtypeStruct((1, D), F32)),
        grid=(T // tm, S),
        in_specs=[pl.BlockSpec((tm, N), lambda i, j: (i, j)), pl.BlockSpec((None, D, N), lambda i, j: (j, 0, 0)),
                  tile, vec, tile],
        out_specs=(tile, vec), scratch_shapes=[pltpu.VMEM((tm, D), F32)],
        compiler_params=_cparams())(dp, w_in, h0, g1, dh1)


def _meta_grad_call(dh0_3, name):
    B, L, D = dh0_3.shape

    def body(d_ref, o_ref):
        o_ref[...] = jnp.sum(d_ref[...], axis=0)

    return pl.pallas_call(
        body, name=name, out_shape=jax.ShapeDtypeStruct((N_META, D), F32), grid=(1,),
        in_specs=[pl.BlockSpec((B, N_META, D), lambda i: (0, 0, 0))],
        out_specs=pl.BlockSpec((N_META, D), lambda i: (0, 0)), compiler_params=_cparams())(dh0_3)


_RELS = [(dx, dy, dc) for dx in (0, 1) for dy in (0, 1) for dc in (0, 1)][1:]


def _exchange_call(arrs, scatter, name):
    n = len(arrs)
    n_rel = len(_RELS)

    def body(*refs):
        ins, outs = refs[:n], refs[n:2 * n]
        send_sems, recv_sems, loc_sems = refs[2 * n:]
        x, y, c = lax.axis_index("x"), lax.axis_index("y"), lax.axis_index("c")
        me = 4 * x + 2 * y + c
        started = []
        for k in range(n):
            src_me = ins[k].at[me] if scatter else ins[k]
            loc = pltpu.make_async_copy(src_me, outs[k].at[me], loc_sems.at[k])
            loc.start()
            started.append(loc)
        waits = []
        for r, (dx, dy, dc) in enumerate(_RELS):
            px, py, pc = (x + dx) % 2, (y + dy) % 2, (c + dc) % 2
            pid = 4 * px + 2 * py + pc
            for k in range(n):
                s = k * n_rel + r
                src = ins[k].at[pid] if scatter else ins[k]
                cp = pltpu.make_async_remote_copy(
                    src_ref=src, dst_ref=outs[k].at[me], send_sem=send_sems.at[s], recv_sem=recv_sems.at[s],
                    device_id=(px, py, pc), device_id_type=pl.DeviceIdType.MESH)
                cp.start()
                waits.append(pltpu.make_async_remote_copy(
                    src_ref=src, dst_ref=outs[k].at[pid], send_sem=send_sems.at[s], recv_sem=recv_sems.at[s],
                    device_id=(px, py, pc), device_id_type=pl.DeviceIdType.MESH))
        for w in waits:
            w.wait_send()
            w.wait_recv()
        for loc in started:
            loc.wait()

    out_shape = tuple(jax.ShapeDtypeStruct(a.shape if scatter else (N_DEV,) + a.shape, a.dtype) for a in arrs)
    hbm = pl.BlockSpec(memory_space=pl.ANY)
    return pl.pallas_call(
        body, name=name, out_shape=out_shape, in_specs=[hbm] * n, out_specs=tuple([hbm] * n),
        scratch_shapes=[pltpu.SemaphoreType.DMA((n * n_rel,)), pltpu.SemaphoreType.DMA((n * n_rel,)),
                        pltpu.SemaphoreType.DMA((n,))],
        compiler_params=pltpu.CompilerParams(has_side_effects=True))(*arrs)


_HBM = pl.BlockSpec(memory_space=pltpu.HBM)
_SEM = pl.BlockSpec(memory_space=pltpu.SEMAPHORE)
_DATAFLOW = pltpu.SideEffectType.DATAFLOW_SIDE_EFFECTING


def _peer_copies(ins, lands, send_sems, recv_sems, scatter):
    n = len(ins)
    x, y, c = lax.axis_index("x"), lax.axis_index("y"), lax.axis_index("c")
    me = 4 * x + 2 * y + c
    sends, arrivals = [], []
    for r, (dx, dy, dc) in enumerate(_RELS):
        px, py, pc = (x + dx) % 2, (y + dy) % 2, (c + dc) % 2
        pid = 4 * px + 2 * py + pc
        for k in range(n):
            s = k * len(_RELS) + r
            src = ins[k].at[pid] if scatter else ins[k]
            for dst, out in ((lands[k].at[me], sends), (lands[k].at[pid], arrivals)):
                out.append(pltpu.make_async_remote_copy(
                    src_ref=src, dst_ref=dst, send_sem=send_sems.at[s], recv_sem=recv_sems.at[s],
                    device_id=(px, py, pc), device_id_type=pl.DeviceIdType.MESH))
    return sends, arrivals


def _exchange_start(arrs, scatter, name):
    n = len(arrs)
    n_sem = n * len(_RELS)

    def body(*refs):
        ins, lands = refs[:n], refs[n:2 * n]
        send_sems, recv_sems = refs[2 * n], refs[2 * n + 1]
        token = refs[-1]
        sends, _ = _peer_copies(ins, lands, send_sems, recv_sems, scatter)
        for cp in sends:
            cp.start()
        token[...] = jnp.zeros_like(token)

    land_shapes = [a.shape if scatter else (N_DEV,) + a.shape for a in arrs]
    ops = [pltpu.with_memory_space_constraint(a, pltpu.HBM) for a in arrs]
    ops += [pltpu.with_memory_space_constraint(lax.empty(s, a.dtype), pltpu.HBM) for s, a in zip(land_shapes, arrs)]
    out = pl.pallas_call(
        body, name=name,
        out_shape=(pltpu.SemaphoreType.DMA((n_sem,)), pltpu.SemaphoreType.DMA((n_sem,)),
                   *[pltpu.HBM(a.shape, a.dtype) for a in arrs],
                   *[pltpu.HBM(s, a.dtype) for s, a in zip(land_shapes, arrs)],
                   jax.ShapeDtypeStruct((SUBLANES, LANES), F32)),
        in_specs=[_HBM] * (2 * n),
        out_specs=(_SEM, _SEM, *[_HBM] * (2 * n), pl.BlockSpec(memory_space=pltpu.VMEM)),
        input_output_aliases={i: 2 + i for i in range(2 * n)},
        compiler_params=pltpu.CompilerParams(has_side_effects=_DATAFLOW))(*ops)
    return out[0], out[1], list(out[2:2 + n]), list(out[2 + n:2 + 2 * n]), out[-1]


def _exchange_wait(started, after, scatter, name):
    send_sems, recv_sems, srcs, lands, _ = started
    n = len(srcs)

    def body(*refs):
        ins, lands_ = refs[:n], refs[n:2 * n]
        _, arrivals = _peer_copies(ins, lands_, refs[2 * n], refs[2 * n + 1], scatter)
        for cp in arrivals:
            cp.wait_send()
            cp.wait_recv()

    out = pl.pallas_call(
        body, name=name,
        out_shape=(*[pltpu.HBM(a.shape, a.dtype) for a in srcs], *[pltpu.HBM(a.shape, a.dtype) for a in lands]),
        in_specs=[_HBM] * (2 * n) + [_SEM, _SEM, pl.BlockSpec(memory_space=pl.ANY)],
        out_specs=tuple([_HBM] * (2 * n)), input_output_aliases={i: i for i in range(2 * n)},
        compiler_params=pltpu.CompilerParams(has_side_effects=_DATAFLOW))(*srcs, *lands, send_sems, recv_sems, after)
    return list(out[:n]), list(out[n:])


def _place_own_call(srcs, lands, scatter, me, name):
    outs = []
    for k, (src, land) in enumerate(zip(srcs, lands)):
        R, C = land.shape[1:]
        tr = R
        while tr % 32 == 0 and tr * C * land.dtype.itemsize > 2 * 1024 * 1024:
            tr //= 2

        def body(me_ref, s_ref, l_ref, o_ref):
            o_ref[...] = s_ref[...]

        src_spec = (pl.BlockSpec((None, tr, C), lambda i, me_ref: (me_ref[0], i, 0)) if scatter
                    else pl.BlockSpec((tr, C), lambda i, me_ref: (i, 0)))
        outs.append(pl.pallas_call(
            body, name=f"{name}_{k}", out_shape=jax.ShapeDtypeStruct(land.shape, land.dtype),
            grid_spec=pltpu.PrefetchScalarGridSpec(
                num_scalar_prefetch=1, grid=(R // tr,),
                in_specs=[src_spec, pl.BlockSpec(memory_space=pl.ANY)],
                out_specs=pl.BlockSpec((None, tr, C), lambda i, me_ref: (me_ref[0], i, 0))),
            input_output_aliases={2: 0}, compiler_params=_cparams())(me, src, land))
    return outs


def _adamw_shard_call(w, parts, m, v, name):
    R, C = w.shape
    tr = _tile(R, 128) if R % 16 == 0 else R

    def body(w_ref, p_ref, m_ref, v_ref, g_ref, d_ref, nm_ref, nv_ref):
        g = p_ref[0].astype(F32)
        for s in range(1, N_DEV):
            g = g + p_ref[s].astype(F32)
        d, nm, nv = _adamw(w_ref[...], g, m_ref[...], v_ref[...])
        g_ref[...] = g
        d_ref[...] = d
        nm_ref[...] = nm
        nv_ref[...] = nv

    tile = pl.BlockSpec((tr, C), lambda i: (i, 0))
    sh = jax.ShapeDtypeStruct((R, C), F32)
    return pl.pallas_call(
        body, name=name, out_shape=(sh, sh, sh, sh), grid=(R // tr,),
        in_specs=[tile, pl.BlockSpec((N_DEV, tr, C), lambda i: (0, i, 0)), tile, tile],
        out_specs=(tile, tile, tile, tile), compiler_params=_cparams())(w, parts, m, v)


def _pack(arrs, rows_mult=SUBLANES):
    flat = jnp.concatenate([a.reshape(-1).astype(F32) for a in arrs])
    n = flat.shape[0]
    per = rows_mult * LANES
    total = -(-n // per) * per
    return jnp.pad(flat, (0, total - n)).reshape(total // LANES, LANES)


def _unpack(pack, shapes):
    flat = pack.reshape(-1)
    out, off = [], 0
    for s in shapes:
        n = 1
        for d in s:
            n *= d
        out.append(flat[off:off + n].reshape(s))
        off += n
    return out


def kernel(x, meta_tokens, mix_norm_g, w_in, ssm_lambda_re, ssm_lambda_im, ssm_log_dt, ssm_b_re, ssm_b_im, ssm_c_re, ssm_c_im, ssm_d, ssm_w_glu, w_ssm_proj, hgrn_lb_logits, hgrn_norm_g, w_hgrn_proj, w_out, ffn_norm_g, w_up, conv_w, conv_b, w_down, final_norm_g, loss_target, m_meta_tokens, m_mix_norm_g, m_w_in, m_ssm_lambda_re, m_ssm_lambda_im, m_ssm_log_dt, m_ssm_b_re, m_ssm_b_im, m_ssm_c_re, m_ssm_c_im, m_ssm_d, m_ssm_w_glu, m_w_ssm_proj, m_hgrn_lb_logits, m_hgrn_norm_g, m_w_hgrn_proj, m_w_out, m_ffn_norm_g, m_w_up, m_conv_w, m_conv_b, m_w_down, m_final_norm_g, v_meta_tokens, v_mix_norm_g, v_w_in, v_ssm_lambda_re, v_ssm_lambda_im, v_ssm_log_dt, v_ssm_b_re, v_ssm_b_im, v_ssm_c_re, v_ssm_c_im, v_ssm_d, v_ssm_w_glu, v_w_ssm_proj, v_hgrn_lb_logits, v_hgrn_norm_g, v_w_hgrn_proj, v_w_out, v_ffn_norm_g, v_w_up, v_conv_w, v_conv_b, v_w_down, v_final_norm_g):
    args = dict(locals())
    B, S_len, D = x.shape
    L = S_len + N_META
    T = B * L
    tm = _tile(L, ROW_TILE_CAP)
    tps = L // tm
    G, P = ssm_lambda_re.shape[1:]
    H = ssm_b_re.shape[-1]
    W = G * H
    n_cb = W // LANES
    gpb = G // n_cb
    hd = hgrn_norm_g.shape[1]
    n_heads = D // hd
    n_in = w_in.shape[2]
    F = w_up.shape[2]
    assert W == D and n_in % LANES == 0

    me = (4 * lax.axis_index("x") + 2 * lax.axis_index("y") + lax.axis_index("c")).astype(jnp.int32).reshape(1)
    meta_g, cw_g = _exchange_call([meta_tokens, conv_w[0]], False, "gather_small_params")
    ga = _exchange_start([w_in[0].astype(MXU)], False, "gather_a_start")
    gb = _exchange_start(
        [w_up[0].astype(MXU), ssm_w_glu[0].astype(MXU), w_ssm_proj[0].astype(MXU), w_hgrn_proj[0].astype(MXU),
         w_out[0].astype(MXU), w_down[0].astype(MXU)], False, "gather_b_start")
    started_tok = (ga[4] + gb[4])[0:1, 0:1]
    meta_full = meta_g.transpose(1, 0, 2).reshape(N_META, D)
    cb_g = conv_b.reshape(N_DEV, 1, F)

    h0 = jnp.concatenate([jnp.broadcast_to(meta_full[None], (B, N_META, D)), x], axis=1).reshape(T, D)
    tgt = jnp.concatenate([jnp.zeros((B, N_META, D), F32), loss_target], axis=1).reshape(T, D)

    lr, li = ssm_lambda_re[0], ssm_lambda_im[0]
    ldt = ssm_log_dt[0].reshape(G, 1)
    bt_re = ssm_b_re[0].transpose(2, 0, 1).reshape(H, G * P)
    bt_im = ssm_b_im[0].transpose(2, 0, 1).reshape(H, G * P)
    seg = _seg_len(L)
    a_re, a_im, as_re, as_im, coef_re, coef_im = _small_call(
        _disc_a_power(seg), [lr, li, ldt], [((G, P), F32)] * 6, "s5_discretise")
    bbt_re, bbt_im = _small_call(
        _disc_b, [coef_re.reshape(1, G * P), coef_im.reshape(1, G * P), bt_re, bt_im],
        [((H, G * P), F32)] * 2, "s5_input_matrix")
    eye = jnp.eye(gpb, dtype=F32)
    hw = gpb * P

    def expand_b(bbt):
        t = bbt.reshape(H, n_cb, gpb, P).transpose(1, 0, 2, 3)[:, None]
        return (t * eye[None, :, None, :, None]).reshape(n_cb, gpb * H, hw)

    def expand_c(cm):
        t = cm.reshape(n_cb, gpb, H, P).transpose(0, 1, 3, 2)[:, :, :, None]
        return (t * eye[None, :, None, :, None]).reshape(n_cb, hw, gpb * H)

    wb = jnp.concatenate([expand_b(bbt_re), expand_b(bbt_im)], axis=2).astype(MXU)
    wc = jnp.concatenate([expand_c(ssm_c_re[0]), -expand_c(ssm_c_im[0])], axis=1).astype(MXU)
    tab = jnp.stack([jnp.concatenate([a_re.reshape(n_cb, hw), a_im.reshape(n_cb, hw)], axis=1),
                     jnp.concatenate([as_re.reshape(n_cb, hw), as_im.reshape(n_cb, hw)], axis=1)], axis=1)
    tab = jnp.broadcast_to(tab[:, :, None, :], (n_cb, 2, SUBLANES, 2 * hw))
    dsk = ssm_d.reshape(n_cb, 1, LANES)
    lb = _small_call(_lb_fn, [hgrn_lb_logits], [((1, D), F32)], "hgrn_lower_bound")[0]

    z1 = _norm_call(h0, mix_norm_g + started_tok, tm, "mix_norm")
    ready = jnp.concatenate([t[(0,) * (t.ndim - 1)][0:1].astype(F32) for t in (z1, wb, wc, tab, lb, tgt)])
    ga_src, ga_land = _exchange_wait(ga, ready, False, "gather_a_wait")
    win_g = _place_own_call(ga_src, ga_land, False, me, "gather_a_own")[0]
    p = _mm_shard(z1, win_g, tm, "in_proj", False)
    p3 = p.reshape(B, L, p.shape[1])
    u_seg = _to_segments(p3[:, :, :W], seg)
    ya_seg, s_all = _s5_fwd_call(u_seg, wb, wc, tab, dsk, "s5_fwd")
    ya = _from_segments(ya_seg, seg, L).reshape(T, W)
    gb_src, gb_land = _exchange_wait(gb, ya, False, "gather_b_wait")
    gathered = _place_own_call(gb_src, gb_land, False, me, "gather_b_own")
    wup_g = gathered[0]
    wglu_g, wsp_g, whp_g, wout_g = [g.reshape(D, D) for g in gathered[1:5]]
    wdn_g = gathered[5].reshape(N_DEV // 2, 2 * w_down.shape[1], D)
    yo, a_br = _glu_proj_call(ya, wglu_g, wsp_g, tm, "s5_glu_proj")
    yb = _hgrn_fwd_call(p3, lb, hgrn_norm_g, n_heads, n_cb, "hgrn_fwd").reshape(T, D)
    col_ga = 5
    h1, mg, bm, z2 = _merge_call(yb, a_br, p, h0, whp_g, wout_g, ffn_norm_g, col_ga, tm, "merge")
    up = _mm_shard(z2, wup_g, tm, "up_proj", True)
    conv_a, conv_b_out, dh2, loss_part, dg3 = _ffn_fwd_call(up, cw_g, cb_g, wdn_g, h1, tgt, final_norm_g.reshape(1, D),
                                                            tm, tps, "ffn_out_loss")

    dua, dub, dwd, dcba, dcbb = _ffn_bwd_a_call(dh2, conv_a, conv_b_out, wdn_g, tm, "ffn_bwd_gate")
    dupa, dupb, dh1, dg2, dcwa, dcwb = _ffn_bwd_b_call(dua, dub, up, cw_g, wup_g, h1, ffn_norm_g, dh2, tm, tps,
                                                       "ffn_bwd_up")
    dwup = jnp.concatenate([_mm_tn(z2, dupa, N_DEV // 2, tm, "dw_up_a", True),
                            _mm_tn(z2, dupb, N_DEV // 2, tm, "dw_up_b", True)], axis=0)
    sh_rows = D // N_DEV
    sa = _exchange_start([dwup, dwd.reshape(N_DEV, w_down.shape[1], D)], True, "scatter_a_start")
    dmg, dwout = _lin_bwd(mg, dh1, wout_g + sa[4][0:1, 0:1].astype(MXU), tm, "out_proj_bwd")
    da_br, dbm, dga, dgb = _merge_bwd_call(dmg, a_br, bm, p, col_ga, tm, "merge_bwd")
    dyo, dwsp = _lin_bwd(yo, da_br, wsp_g, tm, "ssm_proj_bwd")
    dyb, dwhp = _lin_bwd(yb, dbm, whp_g, tm, "hgrn_proj_bwd")
    dya, dwglu = _glu_bwd_call(ya, dyo, wglu_g, tm, "s5_glu_bwd")
    sb = _exchange_start([t.reshape(N_DEV, sh_rows, D) for t in (dwglu, dwsp, dwhp, dwout)], True, "scatter_b_start")
    tok_b = sb[4][0:1, :]
    du_seg, dwb, dwc, dab, ddsk = _s5_bwd_call(u_seg, s_all, _to_segments(dya.reshape(B, L, W), seg), wb, wc, tab,
                                               dsk + tok_b[None], "s5_bwd")
    du = _from_segments(du_seg, seg, L)

    def diag_b(dw):
        t = (dw.reshape(n_cb, gpb, H, gpb, P) * eye[None, :, None, :, None]).sum(axis=1)
        return t.transpose(1, 0, 2, 3).reshape(H, G * P)

    def diag_c(dw):
        t = (dw.reshape(n_cb, gpb, P, gpb, H) * eye[None, :, None, :, None]).sum(axis=3)
        return t.transpose(0, 1, 3, 2).reshape(G, H, P)

    early_parts = [dab[:, 0, :hw].reshape(G, P), dab[:, 0, hw:].reshape(G, P), ddsk.reshape(1, D)]
    early = [_pack(early_parts), diag_b(dwb[:, :, :hw]), diag_b(dwb[:, :, hw:]),
             diag_c(dwc[:, :hw]).reshape(G * H, P), -diag_c(dwc[:, hw:]).reshape(G * H, P)]
    se = _exchange_start(early, False, "gather_s5_grads_start")
    dq, dfl, di, dog, dlb, dng = _hgrn_bwd_call(p3, dyb.reshape(B, L, D), lb, hgrn_norm_g + tok_b + se[4][0:1, :],
                                                n_heads, n_cb, "hgrn_bwd")
    dp = jnp.concatenate([du.reshape(T, W), dq.reshape(T, D), dfl.reshape(T, D), di.reshape(T, D),
                          dog.reshape(T, D), dga, dgb], axis=1)
    dwin = _mm_tn(z1, dp, N_DEV, tm, "dw_in", False)
    sc = _exchange_start([dwin.astype(WIRE)], True, "scatter_c_start")
    dh0, dg1 = _in_bwd_call(dp, win_g, h0, mix_norm_g + sc[4][0:1, 0:1], dh1, tm, "in_proj_bwd")
    dh0_3 = dh0.reshape(B, L, D)
    grad_x = dh0_3[:, N_META:]
    dmeta = _meta_grad_call(dh0_3, "meta_grad")

    late_parts = [dg1, dlb, dng, dg2, jnp.concatenate([dcba, dcbb], axis=0).reshape(1, N_DEV * F), dg3, loss_part]
    late_pack = _pack(late_parts)

    dcw = jnp.concatenate([dcwa, dcwb], axis=0)
    dmeta_s = dmeta.reshape(N_META, N_DEV, D // N_DEV).transpose(1, 0, 2)
    parts_d = _exchange_call([dmeta_s, dcw], True, "scatter_small_grads")
    late_all = _exchange_call([late_pack], False, "gather_small_grads")[0]
    early_all = _place_own_call(*_exchange_wait(se, late_all, False, "gather_s5_grads_wait"), False, me,
                                "gather_s5_grads_own")
    parts_a = _place_own_call(*_exchange_wait(sa, late_all, True, "scatter_a_wait"), True, me, "scatter_a_own")
    parts_b = _place_own_call(*_exchange_wait(sb, late_all, True, "scatter_b_wait"), True, me, "scatter_b_own")
    parts_c = _place_own_call(*_exchange_wait(sc, late_all, True, "scatter_c_wait"), True, me, "scatter_c_own")
    parts = [parts_c[0], parts_a[0], *parts_b, parts_a[1], parts_d[0], parts_d[1]]

    def sum8(*gathered):
        out = []
        for a in gathered:
            t = a[0]
            for s in range(1, N_DEV):
                t = t + a[s]
            out.append(t)
        return tuple(out)

    sums = _small_call(sum8, [*early_all, late_all], [(a.shape, F32) for a in (*early, late_pack)], "sum_small_grads")
    t_abr, t_abi, g_dsk = _unpack(sums[0], [a.shape for a in early_parts])
    t_bbr, t_bbi = sums[1], sums[2]
    g_cre, g_cim = sums[3].reshape(G, H, P), sums[4].reshape(G, H, P)
    g_g1, t_lb, g_ng, g_g2, g_cb, g_g3, loss_v = _unpack(sums[5], [a.shape for a in late_parts])

    def disc_b_bwd(cr, ci, br, bi, dbr, dbi):
        _, vjp = jax.vjp(_disc_b, cr, ci, br, bi)
        return vjp((dbr, dbi))

    t_cr, t_ci, g_btr, g_bti = _small_call(
        disc_b_bwd, [coef_re.reshape(1, G * P), coef_im.reshape(1, G * P), bt_re, bt_im, t_bbr, t_bbi],
        [((1, G * P), F32)] * 2 + [((H, G * P), F32)] * 2, "s5_input_matrix_bwd")

    def disc_a_bwd(lr_, li_, ldt_, dar, dai, dcr, dci):
        _, vjp = jax.vjp(_disc_a, lr_, li_, ldt_)
        return vjp((dar, dai, dcr, dci))

    g_lr, g_li, g_ldt = _small_call(
        disc_a_bwd, [lr, li, ldt, t_abr, t_abi, t_cr.reshape(G, P), t_ci.reshape(G, P)],
        [((G, P), F32)] * 2 + [((G, 1), F32)], "s5_discretise_bwd")

    def lb_bwd(logits, d):
        _, vjp = jax.vjp(_lb_fn, logits)
        return vjp(d)

    g_lbl = _small_call(lb_bwd, [hgrn_lb_logits, t_lb], [(hgrn_lb_logits.shape, F32)], "hgrn_lower_bound_bwd")[0]

    grads = dict(
        mix_norm_g=g_g1, ssm_lambda_re=g_lr[None], ssm_lambda_im=g_li[None], ssm_log_dt=g_ldt.reshape(1, G),
        ssm_b_re=g_btr.reshape(H, G, P).transpose(1, 2, 0)[None], ssm_b_im=g_bti.reshape(H, G, P).transpose(1, 2, 0)[None],
        ssm_c_re=g_cre[None], ssm_c_im=g_cim[None], ssm_d=g_dsk, hgrn_lb_logits=g_lbl, hgrn_norm_g=g_ng,
        ffn_norm_g=g_g2, conv_b=g_cb.reshape(1, N_DEV * F), final_norm_g=g_g3.reshape(D))
    loss = loss_v[0, 0]

    delta, new_m, new_v = {}, {}, {}
    sharded = [("w_in", parts[0], (D, n_in)), ("w_up", parts[1], (D, F)), ("ssm_w_glu", parts[2], (sh_rows, D)),
               ("w_ssm_proj", parts[3], (sh_rows, D)), ("w_hgrn_proj", parts[4], (sh_rows, D)),
               ("w_out", parts[5], (sh_rows, D)), ("w_down", parts[6], (w_down.shape[1], D)),
               ("meta_tokens", parts[7], (N_META, D // N_DEV)), ("conv_w", parts[8], (3, F))]
    for name, part, shp in sharded:
        full = args[name].shape
        g, d_, nm, nv = _adamw_shard_call(args[name].reshape(shp), part, args["m_" + name].reshape(shp),
                                          args["v_" + name].reshape(shp), "adamw_" + name)
        grads[name], delta[name], new_m[name], new_v[name] = [t.reshape(full) for t in (g, d_, nm, nv)]

    for n, shp in (("ssm_b_re", (G * P, H)), ("ssm_b_im", (G * P, H)), ("ssm_c_re", (G * H, P)), ("ssm_c_im", (G * H, P))):
        outs = _small_call(_adamw, [t.reshape(shp) for t in (args[n], grads[n], args["m_" + n], args["v_" + n])],
                           [(shp, F32)] * 3, "adamw_" + n)
        delta[n], new_m[n], new_v[n] = [o.reshape(args[n].shape) for o in outs]
    rep = ["mix_norm_g", "ssm_lambda_re", "ssm_lambda_im", "ssm_log_dt", "ssm_d", "hgrn_lb_logits", "hgrn_norm_g",
           "ffn_norm_g", "conv_b", "final_norm_g"]
    rep_shapes = [args[n].shape for n in rep]
    packs = [_pack([args[pre + n] for n in rep]) for pre in ("", "m_", "v_")]
    g_pack = _pack([grads[n] for n in rep])
    outs = _small_call(lambda w, g, m, v: _adamw(w, g, m, v), [packs[0], g_pack, packs[1], packs[2]],
                       [(g_pack.shape, F32)] * 3, "adamw_replicated")
    for n, d_, nm, nv in zip(rep, *[_unpack(o, rep_shapes) for o in outs]):
        delta[n], new_m[n], new_v[n] = d_, nm, nv

    names = ["meta_tokens", "mix_norm_g", "w_in", "ssm_lambda_re", "ssm_lambda_im", "ssm_log_dt", "ssm_b_re",
             "ssm_b_im", "ssm_c_re", "ssm_c_im", "ssm_d", "ssm_w_glu", "w_ssm_proj", "hgrn_lb_logits", "hgrn_norm_g",
             "w_hgrn_proj", "w_out", "ffn_norm_g", "w_up", "conv_w", "conv_b", "w_down", "final_norm_g"]
    return (loss, grad_x, *[grads[n] for n in names], *[delta[n] for n in names],
            *[new_m[n] for n in names], *[new_v[n] for n in names])
```

```python
import jax
import jax.numpy as jnp
from jax import lax
from jax.experimental import pallas as pl
from jax.experimental.pallas import tpu as pltpu

F32 = jnp.float32
MXU = jnp.bfloat16
ACT = jnp.bfloat16
WIRE = jnp.bfloat16
N_DEV = 8
N_META = 16
CHUNK = 16
EPS = 1e-6
ADAM_LR, ADAM_B1, ADAM_B2, ADAM_EPS, ADAM_WD, ADAM_STEP = 0.001, 0.9, 0.999, 1e-08, 0.01, 10
SUBLANES = 8
LANES = 128
ROW_TILE_CAP = 700
VMEM_LIMIT = 60 * 1024 * 1024


def _cparams(**kw):
    return pltpu.CompilerParams(vmem_limit_bytes=VMEM_LIMIT, **kw)


def _tile(n, cap):
    best = None
    for t in range(16, min(n, cap) + 1, 16):
        if n % t == 0:
            best = t
    assert best is not None, (n, cap)
    return best


def _dot(a, b):
    return lax.dot_general(a.astype(MXU), b.astype(MXU), (((1,), (0,)), ((), ())), preferred_element_type=F32)


def _dot_nt(a, b):
    return lax.dot_general(a.astype(MXU), b.astype(MXU), (((1,), (1,)), ((), ())), preferred_element_type=F32)


def _dot_tn(a, b):
    return lax.dot_general(a.astype(MXU), b.astype(MXU), (((0,), (0,)), ((), ())), preferred_element_type=F32)


def _rms(x, g):
    return x * lax.rsqrt(jnp.mean(x * x, axis=-1, keepdims=True) + EPS) * g


def _silu(x):
    return x * jax.nn.sigmoid(x)


def _small_call(fn, ins, out_shapes, name):
    n_in = len(ins)

    def body(*refs):
        outs = fn(*[r[...] for r in refs[:n_in]])
        outs = outs if isinstance(outs, (tuple, list)) else (outs,)
        for r, o in zip(refs[n_in:], outs):
            r[...] = o.astype(r.dtype)

    vm = pl.BlockSpec(memory_space=pltpu.VMEM)
    return pl.pallas_call(
        body, name=name, out_shape=tuple(jax.ShapeDtypeStruct(s, d) for s, d in out_shapes),
        in_specs=[vm] * n_in, out_specs=tuple([vm] * len(out_shapes)), compiler_params=_cparams())(*ins)


def _disc_a(lr, li, ldt):
    dt = jnp.exp(ldt)
    mag = jnp.exp(lr * dt)
    ab_re = mag * jnp.cos(li * dt)
    ab_im = mag * jnp.sin(li * dt)
    den = lr * lr + li * li
    nr = ab_re - 1.0
    coef_re = (nr * lr + ab_im * li) / den
    coef_im = (ab_im * lr - nr * li) / den
    return ab_re, ab_im, coef_re, coef_im


def _disc_a_power(n):
    def fn(lr, li, ldt):
        ab_re, ab_im, coef_re, coef_im = _disc_a(lr, li, ldt)
        pr, pi, sr, si, m = None, None, ab_re, ab_im, n
        while m:
            if m & 1:
                pr, pi = (sr, si) if pr is None else (pr * sr - pi * si, pr * si + pi * sr)
            m >>= 1
            if m:
                sr, si = sr * sr - si * si, 2.0 * sr * si
        return ab_re, ab_im, pr, pi, coef_re, coef_im
    return fn


def _disc_b(coef_re, coef_im, bt_re, bt_im):
    return coef_re * bt_re - coef_im * bt_im, coef_re * bt_im + coef_im * bt_re


def _lb_fn(logits):
    return jax.nn.softmax(logits, axis=0)[0:1]


def _adamw(w, g, m, v):
    m = ADAM_B1 * m + (1.0 - ADAM_B1) * g
    v = ADAM_B2 * v + (1.0 - ADAM_B2) * jnp.square(g)
    m_hat = m / (1.0 - ADAM_B1 ** ADAM_STEP)
    v_hat = v / (1.0 - ADAM_B2 ** ADAM_STEP)
    delta = -ADAM_LR * (m_hat / (jnp.sqrt(v_hat) + ADAM_EPS) + ADAM_WD * w)
    return delta, m, v


def _norm_call(h, g, tm, name):
    T, D = h.shape

    def body(h_ref, g_ref, z_ref):
        z_ref[...] = _rms(h_ref[...], g_ref[...]).astype(ACT)

    return pl.pallas_call(
        body, name=name, out_shape=jax.ShapeDtypeStruct((T, D), ACT), grid=(T // tm,),
        in_specs=[pl.BlockSpec((tm, D), lambda i: (i, 0)), pl.BlockSpec((1, D), lambda i: (0, 0))],
        out_specs=pl.BlockSpec((tm, D), lambda i: (i, 0)), compiler_params=_cparams())(h, g)


def _mm_shard(x, w, tm, name, major):
    T, K = x.shape
    S, _, N = w.shape

    def body(x_ref, w_ref, o_ref):
        o_ref[...] = _dot(x_ref[...], w_ref[...]).astype(o_ref.dtype)

    if major:
        out_shape = jax.ShapeDtypeStruct((S, T, N), ACT)
        out_spec = pl.BlockSpec((None, tm, N), lambda j, i: (j, i, 0))
    else:
        out_shape = jax.ShapeDtypeStruct((T, S * N), ACT)
        out_spec = pl.BlockSpec((tm, N), lambda j, i: (i, j))
    return pl.pallas_call(
        body, name=name, out_shape=out_shape, grid=(S, T // tm),
        in_specs=[pl.BlockSpec((tm, K), lambda j, i: (i, 0)), pl.BlockSpec((None, K, N), lambda j, i: (j, 0, 0))],
        out_specs=out_spec, compiler_params=_cparams())(x, w)


def _mm_tn(x, y, n_shards, tm, name, major):
    T, K = x.shape
    S = n_shards
    N = y.shape[-1] if major else y.shape[-1] // S

    def body(x_ref, y_ref, o_ref):
        @pl.when(pl.program_id(1) == 0)
        def _():
            o_ref[...] = jnp.zeros_like(o_ref)
        o_ref[...] += _dot_tn(x_ref[...], y_ref[...])

    y_spec = (pl.BlockSpec((None, tm, N), lambda j, i: (j, i, 0)) if major
              else pl.BlockSpec((tm, N), lambda j, i: (i, j)))
    return pl.pallas_call(
        body, name=name, out_shape=jax.ShapeDtypeStruct((S, K, N), F32), grid=(S, T // tm),
        in_specs=[pl.BlockSpec((tm, K), lambda j, i: (i, 0)), y_spec],
        out_specs=pl.BlockSpec((None, K, N), lambda j, i: (j, 0, 0)), compiler_params=_cparams())(x, y)


def _lin_bwd(x, dy, w, tm, name):
    T, K = x.shape
    N = dy.shape[1]

    def body(x_ref, dy_ref, w_ref, dx_ref, dw_ref):
        @pl.when(pl.program_id(0) == 0)
        def _():
            dw_ref[...] = jnp.zeros_like(dw_ref)
        dy = dy_ref[...]
        dx_ref[...] = _dot_nt(dy, w_ref[...]).astype(dx_ref.dtype)
        dw_ref[...] += _dot_tn(x_ref[...], dy)

    return pl.pallas_call(
        body, name=name,
        out_shape=(jax.ShapeDtypeStruct((T, K), ACT), jax.ShapeDtypeStruct((K, N), F32)), grid=(T // tm,),
        in_specs=[pl.BlockSpec((tm, K), lambda i: (i, 0)), pl.BlockSpec((tm, N), lambda i: (i, 0)),
                  pl.BlockSpec((K, N), lambda i: (0, 0))],
        out_specs=(pl.BlockSpec((tm, K), lambda i: (i, 0)), pl.BlockSpec((K, N), lambda i: (0, 0))),
        compiler_params=_cparams())(x, dy, w)


N_SEG = SUBLANES


def _chain_loop(n, step, init):
    per = next(u for u in (4, 3, 2, 1) if n % u == 0)

    def trip(t, carry):
        for u in range(per):
            carry = step(t * per + u, carry)
        return carry

    return lax.fori_loop(0, n // per, trip, init)


def _seg_len(L):
    return -(-L // (N_SEG * SUBLANES)) * SUBLANES


def _to_segments(a3, seg):
    b, length, c = a3.shape
    a = jnp.pad(a3, ((0, 0), (0, N_SEG * seg - length), (0, 0)))
    return a.reshape(b, N_SEG, seg, c).transpose(0, 2, 1, 3).reshape(b, N_SEG * seg, c)


def _from_segments(a3, seg, length):
    b, _, c = a3.shape
    return a3.reshape(b, seg, N_SEG, c).transpose(0, 2, 1, 3).reshape(b, N_SEG * seg, c)[:, :length]


def _seg_scan(x_ref, tab_ref, n_slabs, reverse):
    hw = x_ref.shape[1] // 2
    sign = -1.0 if reverse else 1.0
    ar, ai = tab_ref[0][:, :hw], sign * tab_ref[0][:, hw:]
    br, bi = tab_ref[1][:, :hw], sign * tab_ref[1][:, hw:]

    def slab(k):
        kk = (n_slabs - 1 - k) if reverse else k
        return pl.ds(pl.multiple_of(kk * SUBLANES, SUBLANES), SUBLANES)

    def horner(k, carry):
        cr, ci = carry
        x = x_ref[slab(k), :]
        return ar * cr - ai * ci + x[:, :hw], ar * ci + ai * cr + x[:, hw:]

    z = jnp.zeros((SUBLANES, hw), F32)
    fr, fi = _chain_loop(n_slabs, horner, (z, z))

    row = lax.broadcasted_iota(jnp.int32, (SUBLANES, hw), 0)
    edge = (row == SUBLANES - 1) if reverse else (row == 0)
    shift = SUBLANES - 1 if reverse else 1
    sr, si = z, z
    for _ in range(N_SEG - 1):
        er, ei = fr + br * sr - bi * si, fi + br * si + bi * sr
        sr = jnp.where(edge, 0.0, pltpu.roll(er, shift, 0))
        si = jnp.where(edge, 0.0, pltpu.roll(ei, shift, 0))

    def scan(k, carry):
        cr, ci = carry
        rows = slab(k)
        x = x_ref[rows, :]
        nr, ni = ar * cr - ai * ci + x[:, :hw], ar * ci + ai * cr + x[:, hw:]
        x_ref[rows, 0:hw] = nr
        x_ref[rows, hw:2 * hw] = ni
        return nr, ni

    _chain_loop(n_slabs, scan, (sr, si))


def _s5_fwd_call(p3, wb, wc, tab_f, dsk, name):
    B, L, _ = p3.shape
    n_cb, cw, sw = wb.shape

    def body(u_ref, wb_ref, wc_ref, tab_ref, d_ref, ya_ref, so_ref, s_ref):
        u = u_ref[...]
        s_ref[...] = _dot(u, wb_ref[...])
        _seg_scan(s_ref, tab_ref, L // SUBLANES, False)
        s = s_ref[...].astype(MXU)
        so_ref[...] = s
        y = _dot(s, wc_ref[...]) + d_ref[...] * u.astype(F32)
        ya_ref[...] = jax.nn.gelu(y).astype(ACT)

    return pl.pallas_call(
        body, name=name,
        out_shape=(jax.ShapeDtypeStruct((B, L, n_cb * cw), ACT), jax.ShapeDtypeStruct((B, n_cb, L, sw), MXU)),
        grid=(B, n_cb),
        in_specs=[pl.BlockSpec((None, L, cw), lambda b, c: (b, 0, c)),
                  pl.BlockSpec((None, cw, sw), lambda b, c: (c, 0, 0)),
                  pl.BlockSpec((None, sw, cw), lambda b, c: (c, 0, 0)),
                  pl.BlockSpec((None, 2, SUBLANES, sw), lambda b, c: (c, 0, 0, 0)),
                  pl.BlockSpec((None, 1, cw), lambda b, c: (c, 0, 0))],
        out_specs=(pl.BlockSpec((None, L, cw), lambda b, c: (b, 0, c)),
                   pl.BlockSpec((None, None, L, sw), lambda b, c: (b, c, 0, 0))),
        scratch_shapes=[pltpu.VMEM((L, sw), F32)], compiler_params=_cparams())(p3, wb, wc, tab_f, dsk)


def _s5_bwd_call(p3, s_all, dya, wb, wc, tab_r, dsk, name):
    B, L, _ = p3.shape
    n_cb, cw, sw = wb.shape
    hw = sw // 2
    n_slabs = L // SUBLANES

    def body(u_ref, si_ref, dya_ref, wb_ref, wc_ref, tr_ref, d_ref,
             du_ref, dwb_ref, dwc_ref, da_ref, dd_ref, s_ref, l_ref):
        @pl.when(pl.program_id(1) == 0)
        def _():
            dwb_ref[...] = jnp.zeros_like(dwb_ref)
            dwc_ref[...] = jnp.zeros_like(dwc_ref)
            da_ref[...] = jnp.zeros_like(da_ref)
            dd_ref[...] = jnp.zeros_like(dd_ref)

        u = u_ref[...]
        uf = u.astype(F32)
        s_in = si_ref[...]
        s_ref[...] = s_in.astype(F32)
        y = _dot(s_in, wc_ref[...]) + d_ref[...] * uf
        _, gelu_vjp = jax.vjp(jax.nn.gelu, y)
        dy = gelu_vjp(dya_ref[...].astype(F32))[0]
        dd_ref[...] += jnp.sum(dy * uf, axis=0, keepdims=True)
        l_ref[...] = _dot_nt(dy, wc_ref[...])
        _seg_scan(l_ref, tr_ref, n_slabs, True)
        du_ref[...] = (_dot_nt(l_ref[...], wb_ref[...]) + d_ref[...] * dy).astype(ACT)
        dwb_ref[...] += _dot_tn(u, l_ref[...])
        dwc_ref[...] += _dot_tn(s_in, dy)

        row = lax.broadcasted_iota(jnp.int32, (SUBLANES, hw), 0)
        last = s_ref[pl.ds((n_slabs - 1) * SUBLANES, SUBLANES), :]
        p0r = jnp.where(row == 0, 0.0, pltpu.roll(last[:, :hw], 1, 0))
        p0i = jnp.where(row == 0, 0.0, pltpu.roll(last[:, hw:], 1, 0))

        def step(k, carry):
            qr, qi, accr, acci = carry
            r0 = pl.multiple_of(k * SUBLANES, SUBLANES)
            s = s_ref[pl.ds(r0, SUBLANES), :]
            lam = l_ref[pl.ds(r0, SUBLANES), :]
            lr, li = lam[:, :hw], lam[:, hw:]
            accr = accr + lr * qr + li * qi
            acci = acci + li * qr - lr * qi
            return s[:, :hw], s[:, hw:], accr, acci

        z8 = jnp.zeros((SUBLANES, hw), F32)
        _, _, accr, acci = _chain_loop(n_slabs, step, (p0r, p0i, z8, z8))
        da_ref[...] += jnp.concatenate([jnp.sum(accr, axis=0, keepdims=True),
                                        jnp.sum(acci, axis=0, keepdims=True)], axis=1)

    W = n_cb * cw
    return pl.pallas_call(
        body, name=name,
        out_shape=(jax.ShapeDtypeStruct((B, L, W), ACT), jax.ShapeDtypeStruct((n_cb, cw, sw), F32),
                   jax.ShapeDtypeStruct((n_cb, sw, cw), F32), jax.ShapeDtypeStruct((n_cb, 1, sw), F32),
                   jax.ShapeDtypeStruct((n_cb, 1, cw), F32)),
        grid=(n_cb, B),
        in_specs=[pl.BlockSpec((None, L, cw), lambda c, b: (b, 0, c)),
                  pl.BlockSpec((None, None, L, sw), lambda c, b: (b, c, 0, 0)),
                  pl.BlockSpec((None, L, cw), lambda c, b: (b, 0, c)),
                  pl.BlockSpec((None, cw, sw), lambda c, b: (c, 0, 0)),
                  pl.BlockSpec((None, sw, cw), lambda c, b: (c, 0, 0)),
                  pl.BlockSpec((None, 2, SUBLANES, sw), lambda c, b: (c, 0, 0, 0)),
                  pl.BlockSpec((None, 1, cw), lambda c, b: (c, 0, 0))],
        out_specs=(pl.BlockSpec((None, L, cw), lambda c, b: (b, 0, c)),
                   pl.BlockSpec((None, cw, sw), lambda c, b: (c, 0, 0)),
                   pl.BlockSpec((None, sw, cw), lambda c, b: (c, 0, 0)),
                   pl.BlockSpec((None, 1, sw), lambda c, b: (c, 0, 0)),
                   pl.BlockSpec((None, 1, cw), lambda c, b: (c, 0, 0))),
        scratch_shapes=[pltpu.VMEM((L, sw), F32), pltpu.VMEM((L, sw), F32)],
        compiler_params=_cparams())(p3, s_all, dya, wb, wc, tab_r, dsk)


def _glu_proj_call(ya, wglu, wproj, tm, name):
    T, W = ya.shape
    D = wproj.shape[1]

    def body(ya_ref, wg_ref, wp_ref, yo_ref, a_ref):
        ya = ya_ref[...]
        yo = ya.astype(F32) * jax.nn.sigmoid(_dot(ya, wg_ref[...]))
        yo_ref[...] = yo.astype(ACT)
        a_ref[...] = _dot(yo, wp_ref[...]).astype(ACT)

    return pl.pallas_call(
        body, name=name, out_shape=(jax.ShapeDtypeStruct((T, W), ACT), jax.ShapeDtypeStruct((T, D), ACT)),
        grid=(T // tm,),
        in_specs=[pl.BlockSpec((tm, W), lambda i: (i, 0)), pl.BlockSpec((W, W), lambda i: (0, 0)),
                  pl.BlockSpec((W, D), lambda i: (0, 0))],
        out_specs=(pl.BlockSpec((tm, W), lambda i: (i, 0)), pl.BlockSpec((tm, D), lambda i: (i, 0))),
        compiler_params=_cparams())(ya, wglu, wproj)


def _glu_bwd_call(ya, dyo, wglu, tm, name):
    T, W = ya.shape

    def body(ya_ref, dyo_ref, wg_ref, dya_ref, dwg_ref):
        @pl.when(pl.program_id(0) == 0)
        def _():
            dwg_ref[...] = jnp.zeros_like(dwg_ref)
        ya = ya_ref[...]
        yaf = ya.astype(F32)
        dyo = dyo_ref[...].astype(F32)
        sg = jax.nn.sigmoid(_dot(ya, wg_ref[...]))
        dt = dyo * yaf * sg * (1.0 - sg)
        dya_ref[...] = (dyo * sg + _dot_nt(dt, wg_ref[...])).astype(ACT)
        dwg_ref[...] += _dot_tn(ya, dt)

    return pl.pallas_call(
        body, name=name, out_shape=(jax.ShapeDtypeStruct((T, W), ACT), jax.ShapeDtypeStruct((W, W), F32)),
        grid=(T // tm,),
        in_specs=[pl.BlockSpec((tm, W), lambda i: (i, 0)), pl.BlockSpec((tm, W), lambda i: (i, 0)),
                  pl.BlockSpec((W, W), lambda i: (0, 0))],
        out_specs=(pl.BlockSpec((tm, W), lambda i: (i, 0)), pl.BlockSpec((W, W), lambda i: (0, 0))),
        compiler_params=_cparams())(ya, dyo, wglu)


PAD = 16


def _chunk_cumsums(x, pad_ref, L):
    row = lax.broadcasted_iota(jnp.int32, x.shape, 0) % CHUNK
    zeros = jnp.zeros((PAD, x.shape[1]), F32)
    pad_ref[0:PAD, :] = zeros
    pad_ref[PAD + L:2 * PAD + L, :] = zeros
    c = x
    r = x
    d = 1
    while d < CHUNK:
        pad_ref[PAD:PAD + L, :] = c
        c = c + jnp.where(row >= d, pad_ref[PAD - d:PAD - d + L, :], 0.0)
        pad_ref[PAD:PAD + L, :] = r
        r = r + jnp.where(row + d < CHUNK, pad_ref[PAD + d:PAD + d + L, :], 0.0)
        d *= 2
    return c, r - x


def _hgrn_prep(q_ref, fl_ref, lb_ref, pad_ref, r0, n):
    rows = pl.ds(r0, n)
    lb = lb_ref[...]
    sig = jax.nn.sigmoid(fl_ref[rows, :].astype(F32))
    f = lb + (1.0 - lb) * sig
    k = 1.0 - f
    c, rc = _chunk_cumsums(jnp.log(f), pad_ref, n)
    e_in, e_inv, e_out = jnp.exp(c), jnp.exp(-c), jnp.exp(rc)
    q = q_ref[rows, :].astype(F32)
    return dict(sig=sig, f=f, k=k, q=q, e_in=e_in, e_inv=e_inv, e_out=e_out, dec=jnp.exp(c + rc))


def _for_row_blocks(L, fn):
    full = L // GROUP
    if full:
        def step(g, carry):
            fn(pl.multiple_of(g * GROUP, GROUP), GROUP)
            return carry
        lax.fori_loop(0, full, step, 0)
    if L % GROUP:
        fn(full * GROUP, L % GROUP)


def _chunk_mask(rb):
    r = lax.broadcasted_iota(jnp.int32, (rb, rb), 0)
    c = lax.broadcasted_iota(jnp.int32, (rb, rb), 1)
    return (r // CHUNK == c // CHUNK) & (c <= r)


def _hg_out(o, og, g):
    on = o * lax.rsqrt(jnp.mean(o * o, axis=-1, keepdims=True) + EPS) * g
    return on * _silu(og)


def _hgrn_specs(L, hd, col_q, n_heads, order):
    def spec(sec):
        return pl.BlockSpec((None, L, hd), lambda *g: (order(*g)[0], 0, col_q + sec * n_heads + order(*g)[1]))
    return [spec(0), spec(1), spec(2), spec(3)]


GROUP = 128
CPG = GROUP // CHUNK


def _expand(x):
    xf = x.astype(F32)
    chunk = lax.broadcasted_iota(jnp.int32, xf.shape, 0) // CHUNK
    return jnp.concatenate([jnp.where(chunk == j, xf, 0.0) for j in range(CPG)], axis=1)


def _fill_tail(refs_fills, L):
    for ref, fill in refs_fills:
        if ref.shape[0] > L:
            ref[L:ref.shape[0], :] = jnp.full((ref.shape[0] - L, ref.shape[1]), fill, ref.dtype)


GROUP_UNROLL = 17


def _hgrn_forward_core(q_ref, fl_ref, v_ref, lb_ref, pad_ref, qin_ref, kin_ref, kout_ref, vp_ref, dec_ref, o_ref,
                       s_ref, a_ref, L, keep=()):
    hd = qin_ref.shape[1]
    n_groups = qin_ref.shape[0] // GROUP

    def prep(r0, n):
        pp = _hgrn_prep(q_ref, fl_ref, lb_ref, pad_ref, r0, n)
        rows = pl.ds(r0, n)
        for key, ref in keep:
            ref[rows, :] = pp[key]
        qin_ref[rows, :] = (pp["q"] * pp["e_in"]).astype(MXU)
        kin_ref[rows, :] = (pp["k"] * pp["e_inv"]).astype(MXU)
        kout_ref[rows, :] = (pp["k"] * pp["e_out"]).astype(MXU)
        vp_ref[rows, :] = v_ref[rows, :].astype(MXU)
        dec_ref[rows, :] = pp["dec"]

    _for_row_blocks(L, prep)
    _fill_tail(((qin_ref, 0.0), (kin_ref, 0.0), (kout_ref, 0.0), (vp_ref, 0.0), (dec_ref, 1.0)), L)
    mask = _chunk_mask(GROUP)

    def scores(g, carry):
        rows = pl.ds(pl.multiple_of(g * GROUP, GROUP), GROUP)
        a_ref[rows, :] = jnp.where(mask, _dot_nt(qin_ref[rows, :], kin_ref[rows, :]), 0.0).astype(MXU)
        return carry

    lax.fori_loop(0, n_groups, scores, 0, unroll=GROUP_UNROLL)

    def intra(g, carry):
        rows = pl.ds(pl.multiple_of(g * GROUP, GROUP), GROUP)
        o_ref[rows, :] = _dot(a_ref[rows, :], vp_ref[rows, :])
        kv = _dot_tn(vp_ref[rows, :], _expand(kout_ref[rows, :]))
        for j in range(CPG):
            s_ref[g * CPG + j] = kv[:, j * hd:(j + 1) * hd]
        return carry

    lax.fori_loop(0, n_groups, intra, 0, unroll=GROUP_UNROLL)

    def rec(n, st):
        kv = s_ref[n]
        s_ref[n] = st
        dec = dec_ref[pl.ds(pl.multiple_of(n * CHUNK, CHUNK), SUBLANES), :][0:1]
        return st * dec + kv

    _chain_loop(L // CHUNK, rec, jnp.zeros((hd, hd), F32))

    def inter(g, carry):
        rows = pl.ds(pl.multiple_of(g * GROUP, GROUP), GROUP)
        scat = jnp.concatenate([s_ref[g * CPG + j] for j in range(CPG)], axis=1)
        o_ref[rows, :] += _dot_nt(_expand(qin_ref[rows, :]), scat)
        return carry

    lax.fori_loop(0, n_groups, inter, 0, unroll=GROUP_UNROLL)


def _hgrn_scratch(L, hd):
    lp = -(-L // GROUP) * GROUP
    return lp, [pltpu.VMEM((GROUP + 2 * PAD, hd), F32), pltpu.VMEM((lp, hd), MXU), pltpu.VMEM((lp, hd), MXU),
                pltpu.VMEM((lp, hd), MXU), pltpu.VMEM((lp, hd), MXU), pltpu.VMEM((lp, hd), F32),
                pltpu.VMEM((lp, hd), F32), pltpu.VMEM((lp // CHUNK, hd, hd), F32), pltpu.VMEM((lp, GROUP), MXU)]


def _hgrn_fwd_call(p3, lb, ng, n_heads, col_q, name):
    B, L, _ = p3.shape
    hd = ng.shape[1]
    _, scratch = _hgrn_scratch(L, hd)

    def body(q_ref, fl_ref, v_ref, og_ref, lb_ref, ng_ref, yb_ref,
             pad_ref, qin_ref, kin_ref, kout_ref, vp_ref, dec_ref, o_ref, s_ref, a_ref):
        _hgrn_forward_core(q_ref, fl_ref, v_ref, lb_ref, pad_ref, qin_ref, kin_ref, kout_ref, vp_ref, dec_ref,
                           o_ref, s_ref, a_ref, L)

        def out(r0, n):
            rows = pl.ds(r0, n)
            yb_ref[rows, :] = _hg_out(o_ref[rows, :], og_ref[rows, :].astype(F32), ng_ref[...]).astype(ACT)

        _for_row_blocks(L, out)

    order = lambda b, h: (b, h)
    return pl.pallas_call(
        body, name=name, out_shape=jax.ShapeDtypeStruct((B, L, n_heads * hd), ACT), grid=(B, n_heads),
        in_specs=_hgrn_specs(L, hd, col_q, n_heads, order) + [
            pl.BlockSpec((1, hd), lambda b, h: (0, h)), pl.BlockSpec((1, hd), lambda b, h: (0, 0))],
        out_specs=pl.BlockSpec((None, L, hd), lambda b, h: (b, 0, h)),
        scratch_shapes=scratch, compiler_params=_cparams())(p3, p3, p3, p3, lb, ng)


def _hgrn_bwd_call(p3, dyb, lb, ng, n_heads, col_q, name):
    B, L, _ = p3.shape
    hd = ng.shape[1]
    n_chunks = L // CHUNK
    lp, scratch = _hgrn_scratch(L, hd)
    n_groups = lp // GROUP

    def body(q_ref, fl_ref, v_ref, og_ref, dyb_ref, lb_ref, ng_ref,
             dq_ref, dfl_ref, dv_ref, dog_ref, dlb_ref, dng_ref,
             pad_ref, qin_ref, kin_ref, kout_ref, vp_ref, dec_ref, o_ref, s_ref, a_ref,
             do_ref, ds_ref, dqi_ref, dki_ref, dko_ref, dvv_ref, dct_ref,
             sig_ref, f_ref, ein_ref, einv_ref, eout_ref, da_ref):
        @pl.when(pl.program_id(1) == 0)
        def _():
            dlb_ref[...] = jnp.zeros_like(dlb_ref)

        @pl.when((pl.program_id(0) == 0) & (pl.program_id(1) == 0))
        def _():
            dng_ref[...] = jnp.zeros_like(dng_ref)

        _hgrn_forward_core(q_ref, fl_ref, v_ref, lb_ref, pad_ref, qin_ref, kin_ref, kout_ref, vp_ref, dec_ref,
                           o_ref, s_ref, a_ref, L, keep=(("sig", sig_ref), ("f", f_ref), ("e_in", ein_ref),
                                                  ("e_inv", einv_ref), ("e_out", eout_ref)))

        def out_bwd(r0, n):
            rows = pl.ds(r0, n)
            _, out_vjp = jax.vjp(_hg_out, o_ref[rows, :], og_ref[rows, :].astype(F32), ng_ref[...])
            d_o, d_og, d_ng = out_vjp(dyb_ref[rows, :].astype(F32))
            dog_ref[rows, :] = d_og.astype(ACT)
            dng_ref[...] += d_ng
            do_ref[rows, :] = d_o.astype(MXU)

        _for_row_blocks(L, out_bwd)
        _fill_tail(((do_ref, 0.0),), L)
        mask = _chunk_mask(GROUP)

        def score_grads(g, carry):
            rows = pl.ds(pl.multiple_of(g * GROUP, GROUP), GROUP)
            da_ref[rows, :] = jnp.where(mask, _dot_nt(do_ref[rows, :], vp_ref[rows, :]), 0.0).astype(MXU)
            return carry

        lax.fori_loop(0, n_groups, score_grads, 0, unroll=GROUP_UNROLL)

        def grads_a(g, carry):
            rows = pl.ds(pl.multiple_of(g * GROUP, GROUP), GROUP)
            qi, ki, do, da = qin_ref[rows, :], kin_ref[rows, :], do_ref[rows, :], da_ref[rows, :]
            sstack = s_ref[pl.ds(g * CPG, CPG)].reshape(CPG * hd, hd)
            dqi_ref[rows, :] = _dot(da, ki) + _dot(_expand(do), sstack)
            dki_ref[rows, :] = _dot_tn(da, qi)
            dvv_ref[rows, :] = _dot_tn(a_ref[rows, :], do)
            x = _dot_tn(do, _expand(qi))
            for j in range(CPG):
                ds_ref[g * CPG + j] = x[:, j * hd:(j + 1) * hd]
            return carry

        lax.fori_loop(0, n_groups, grads_a, 0, unroll=GROUP_UNROLL)

        def rec_bwd(k, dst):
            n = n_chunks - 1 - k
            r0 = pl.multiple_of(n * CHUNK, CHUNK)
            x = ds_ref[n]
            ds_ref[n] = dst
            dec = dec_ref[pl.ds(r0, SUBLANES), :][0:1]
            return dst * dec + x

        _chain_loop(n_chunks, rec_bwd, jnp.zeros((hd, hd), F32))

        def grads_b(g, carry):
            r0 = pl.multiple_of(g * GROUP, GROUP)
            rows = pl.ds(r0, GROUP)
            ds = [ds_ref[g * CPG + j] for j in range(CPG)]
            dscat = jnp.concatenate(ds, axis=1)
            dvv_ref[rows, :] += _dot_nt(_expand(kout_ref[rows, :]), dscat)
            dstack = ds_ref[pl.ds(g * CPG, CPG)].reshape(CPG * hd, hd)
            dko_ref[rows, :] = _dot(_expand(vp_ref[rows, :]), dstack)
            for j in range(CPG):
                dec = dec_ref[pl.ds(r0 + j * CHUNK, SUBLANES), :][0:1]
                ddec = dec * jnp.sum(ds[j] * s_ref[g * CPG + j], axis=0, keepdims=True)
                dct_ref[pl.ds(r0 + j * CHUNK, CHUNK), :] = jnp.broadcast_to(ddec, (CHUNK, hd))
            return carry

        lax.fori_loop(0, n_groups, grads_b, 0, unroll=GROUP_UNROLL)

        def finish(r0, n):
            rows = pl.ds(r0, n)
            sig, f, e_in, e_inv, e_out = [r[rows, :] for r in (sig_ref, f_ref, ein_ref, einv_ref, eout_ref)]
            q, k = q_ref[rows, :].astype(F32), 1.0 - f
            dqi, dki, dko = dqi_ref[rows, :], dki_ref[rows, :], dko_ref[rows, :]
            dq = dqi * e_in
            dk = dki * e_inv + dko * e_out
            dq_ref[rows, :] = dq.astype(ACT)
            dv_ref[rows, :] = dvv_ref[rows, :].astype(ACT)
            t_out = k * e_out * dko
            dc = q * dq - k * e_inv * dki - t_out
            _, dc_later = _chunk_cumsums(dc, pad_ref, n)
            t_incl, t_later = _chunk_cumsums(t_out, pad_ref, n)
            dlogf = dc + dc_later + t_incl + t_later + dct_ref[rows, :]
            df = dlogf / f - dk
            dfl_ref[rows, :] = (df * (1.0 - lb_ref[...]) * sig * (1.0 - sig)).astype(ACT)
            dlb_ref[...] += jnp.sum(df * (1.0 - sig), axis=0, keepdims=True)

        _for_row_blocks(L, finish)

    order = lambda h, b: (b, h)
    W = n_heads * hd
    act_out = jax.ShapeDtypeStruct((B, L, W), ACT)
    blk_out = pl.BlockSpec((None, L, hd), lambda h, b: (b, 0, h))
    return pl.pallas_call(
        body, name=name,
        out_shape=(act_out, act_out, act_out, act_out, jax.ShapeDtypeStruct((1, W), F32),
                   jax.ShapeDtypeStruct((1, hd), F32)),
        grid=(n_heads, B),
        in_specs=_hgrn_specs(L, hd, col_q, n_heads, order) + [
            pl.BlockSpec((None, L, hd), lambda h, b: (b, 0, h)),
            pl.BlockSpec((1, hd), lambda h, b: (0, h)), pl.BlockSpec((1, hd), lambda h, b: (0, 0))],
        out_specs=(blk_out, blk_out, blk_out, blk_out, pl.BlockSpec((1, hd), lambda h, b: (0, h)),
                   pl.BlockSpec((1, hd), lambda h, b: (0, 0))),
        scratch_shapes=scratch + [
            pltpu.VMEM((lp, hd), MXU), pltpu.VMEM((lp // CHUNK, hd, hd), F32)] + [pltpu.VMEM((lp, hd), F32)] * 10 + [
            pltpu.VMEM((lp, GROUP), MXU)],
        compiler_params=_cparams())(p3, p3, p3, p3, dyb, lb, ng)


def _merge_fn(a, bm, ga, gb):
    return jax.nn.sigmoid(ga) * a + jax.nn.sigmoid(gb) * bm


def _merge_call(yb, a, p, h0, whp, wout, g2, col_ga, tm, name):
    T, D = h0.shape

    def body(yb_ref, a_ref, ga_ref, gb_ref, h0_ref, whp_ref, wout_ref, g2_ref, h1_ref, mg_ref, bm_ref, z2_ref):
        bm = _dot(yb_ref[...], whp_ref[...])
        mg = _merge_fn(a_ref[...].astype(F32), bm, ga_ref[...].astype(F32), gb_ref[...].astype(F32))
        h1 = h0_ref[...] + _dot(mg, wout_ref[...])
        h1_ref[...] = h1
        mg_ref[...] = mg.astype(ACT)
        bm_ref[...] = bm.astype(ACT)
        z2_ref[...] = _rms(h1, g2_ref[...]).astype(ACT)

    tile = pl.BlockSpec((tm, D), lambda i: (i, 0))
    full = pl.BlockSpec((D, D), lambda i: (0, 0))
    act = jax.ShapeDtypeStruct((T, D), ACT)
    return pl.pallas_call(
        body, name=name, out_shape=(jax.ShapeDtypeStruct((T, D), F32), act, act, act), grid=(T // tm,),
        in_specs=[tile, tile, pl.BlockSpec((tm, D), lambda i: (i, col_ga)),
                  pl.BlockSpec((tm, D), lambda i: (i, col_ga + 1)), tile, full, full,
                  pl.BlockSpec((1, D), lambda i: (0, 0))],
        out_specs=(tile, tile, tile, tile), compiler_params=_cparams())(yb, a, p, p, h0, whp, wout, g2)


def _merge_bwd_call(dmg, a, bm, p, col_ga, tm, name):
    T, D = dmg.shape

    def body(dmg_ref, a_ref, bm_ref, ga_ref, gb_ref, da_ref, dbm_ref, dga_ref, dgb_ref):
        args = [r[...].astype(F32) for r in (a_ref, bm_ref, ga_ref, gb_ref)]
        _, vjp = jax.vjp(_merge_fn, *args)
        for r, o in zip((da_ref, dbm_ref, dga_ref, dgb_ref), vjp(dmg_ref[...].astype(F32))):
            r[...] = o.astype(ACT)

    tile = pl.BlockSpec((tm, D), lambda i: (i, 0))
    act = jax.ShapeDtypeStruct((T, D), ACT)
    return pl.pallas_call(
        body, name=name, out_shape=(act, act, act, act), grid=(T // tm,),
        in_specs=[tile, tile, tile, pl.BlockSpec((tm, D), lambda i: (i, col_ga)),
                  pl.BlockSpec((tm, D), lambda i: (i, col_ga + 1))],
        out_specs=(tile, tile, tile, tile), compiler_params=_cparams())(dmg, a, bm, p, p)


def _conv_taps(x_ref, halo_ref, ext_ref, edge, tm, before):
    halo = jnp.where(edge, 0.0, halo_ref[...].astype(F32))
    x = x_ref[...].astype(F32)
    if before:
        ext_ref[0:PAD, :] = halo
        ext_ref[PAD:PAD + tm, :] = x
        return [ext_ref[PAD - 2 + k:PAD - 2 + k + tm, :] for k in range(3)]
    ext_ref[0:tm, :] = x
    ext_ref[tm:tm + PAD, :] = halo
    return [ext_ref[k:k + tm, :] for k in range(3)]


def _conv(taps, cw, cb):
    return cb + cw[0:1] * taps[0] + cw[1:2] * taps[1] + cw[2:3] * taps[2]


def _ffn_pair_specs(tm, F, T, n_pairs, order, before):
    hb = tm // PAD
    last = T // PAD - 1

    def halo_row(i):
        return jnp.maximum(i * hb - 1, 0) if before else jnp.minimum((i + 1) * hb, last)

    specs = []
    for off in (0, n_pairs):
        specs.append(pl.BlockSpec((None, tm, F), lambda *g, off=off: (order(*g)[1] + off, order(*g)[0], 0)))
        specs.append(pl.BlockSpec((None, PAD, F), lambda *g, off=off: (order(*g)[1] + off, halo_row(order(*g)[0]), 0)))
    return specs


def _ffn_fwd_call(up, cw, cb, wd, h1, tgt, g3, tm, tps, name):
    S, T, F = up.shape
    n_pairs = S // 2
    D = h1.shape[1]

    def body(ua_ref, ha_ref, ub_ref, hb_ref, cwa_ref, cwb_ref, cba_ref, cbb_ref, wd_ref, h1_ref, tgt_ref, g3_ref,
             ca_ref, cb_ref, dh2_ref, loss_ref, dg3_ref, acc_ref, ext_ref):
        i, j = pl.program_id(0), pl.program_id(1)
        edge = (i % tps) == 0
        ua = _conv(_conv_taps(ua_ref, ha_ref, ext_ref, edge, tm, True), cwa_ref[...], cba_ref[...])
        ub = _conv(_conv_taps(ub_ref, hb_ref, ext_ref, edge, tm, True), cwb_ref[...], cbb_ref[...])
        ca_ref[...] = ua.astype(ACT)
        cb_ref[...] = ub.astype(ACT)
        contrib = _dot(_silu(ua) * ub, wd_ref[...])

        @pl.when(j == 0)
        def _():
            acc_ref[...] = h1_ref[...] + contrib

        @pl.when(j > 0)
        def _():
            acc_ref[...] += contrib

        @pl.when((i == 0) & (j == 0))
        def _():
            loss_ref[...] = jnp.zeros_like(loss_ref)
            dg3_ref[...] = jnp.zeros_like(dg3_ref)

        @pl.when(j == n_pairs - 1)
        def _():
            row = lax.broadcasted_iota(jnp.int32, (tm, 1), 0) + (i % tps) * tm
            valid = row >= N_META
            tgt = tgt_ref[...]

            def loss_fn(h2, g):
                err = _rms(h2, g) - tgt
                return 0.5 * jnp.sum(jnp.where(valid, err * err, 0.0)) / D

            loss, vjp = jax.vjp(loss_fn, acc_ref[...], g3_ref[...])
            dh2, dg3 = vjp(jnp.ones((), F32))
            dh2_ref[...] = dh2
            loss_ref[...] += loss
            dg3_ref[...] += dg3

    order = lambda i, j: (i, j)
    tile = pl.BlockSpec((tm, D), lambda i, j: (i, 0))
    vec = pl.BlockSpec((1, D), lambda i, j: (0, 0))
    return pl.pallas_call(
        body, name=name,
        out_shape=(jax.ShapeDtypeStruct((n_pairs, T, F), ACT), jax.ShapeDtypeStruct((n_pairs, T, F), ACT),
                   jax.ShapeDtypeStruct((T, D), F32), jax.ShapeDtypeStruct((1, LANES), F32),
                   jax.ShapeDtypeStruct((1, D), F32)),
        grid=(T // tm, n_pairs),
        in_specs=_ffn_pair_specs(tm, F, T, n_pairs, order, True) + [
            pl.BlockSpec((None, 3, F), lambda i, j: (j, 0, 0)), pl.BlockSpec((None, 3, F), lambda i, j: (j + n_pairs, 0, 0)),
            pl.BlockSpec((None, 1, F), lambda i, j: (j, 0, 0)), pl.BlockSpec((None, 1, F), lambda i, j: (j + n_pairs, 0, 0)),
            pl.BlockSpec((None, F, D), lambda i, j: (j, 0, 0)), tile, tile, vec],
        out_specs=(pl.BlockSpec((None, tm, F), lambda i, j: (j, i, 0)), pl.BlockSpec((None, tm, F), lambda i, j: (j, i, 0)),
                   tile, pl.BlockSpec((1, LANES), lambda i, j: (0, 0)), vec),
        scratch_shapes=[pltpu.VMEM((tm, D), F32), pltpu.VMEM((tm + PAD, F), F32)],
        compiler_params=_cparams())(up, up, up, up, cw, cw, cb, cb, wd, h1, tgt, g3)


def _ffn_bwd_a_call(dh2, ca, cb, wd, tm, name):
    n_pairs, T, F = ca.shape
    D = dh2.shape[1]

    def body(dh2_ref, ca_ref, cb_ref, wd_ref, dua_ref, dub_ref, dwd_ref, dcba_ref, dcbb_ref):
        @pl.when(pl.program_id(1) == 0)
        def _():
            for r in (dwd_ref, dcba_ref, dcbb_ref):
                r[...] = jnp.zeros_like(r)

        dh2 = dh2_ref[...]
        ua, ub = ca_ref[...].astype(F32), cb_ref[...].astype(F32)
        sa = jax.nn.sigmoid(ua)
        gate = ua * sa
        dact = _dot_nt(dh2, wd_ref[...])
        dwd_ref[...] += _dot_tn(gate * ub, dh2)
        dub = dact * gate
        dua = dact * ub * sa * (1.0 + ua * (1.0 - sa))
        dcba_ref[...] += jnp.sum(dua, axis=0, keepdims=True)
        dcbb_ref[...] += jnp.sum(dub, axis=0, keepdims=True)
        dua_ref[...] = dua.astype(ACT)
        dub_ref[...] = dub.astype(ACT)

    blk = pl.BlockSpec((None, tm, F), lambda j, i: (j, i, 0))
    vec = pl.BlockSpec((None, 1, F), lambda j, i: (j, 0, 0))
    return pl.pallas_call(
        body, name=name,
        out_shape=(jax.ShapeDtypeStruct((n_pairs, T, F), ACT), jax.ShapeDtypeStruct((n_pairs, T, F), ACT),
                   jax.ShapeDtypeStruct((n_pairs, F, D), F32), jax.ShapeDtypeStruct((n_pairs, 1, F), F32),
                   jax.ShapeDtypeStruct((n_pairs, 1, F), F32)),
        grid=(n_pairs, T // tm),
        in_specs=[pl.BlockSpec((tm, D), lambda j, i: (i, 0)), blk, blk, pl.BlockSpec((None, F, D), lambda j, i: (j, 0, 0))],
        out_specs=(blk, blk, pl.BlockSpec((None, F, D), lambda j, i: (j, 0, 0)), vec, vec),
        compiler_params=_cparams())(dh2, ca, cb, wd)


def _ffn_bwd_b_call(dua, dub, up, cw, wup, h1, g2, dh2, tm, tps, name):
    n_pairs, T, F = dua.shape
    D = h1.shape[1]
    hb = tm // PAD
    last = T // PAD - 1

    def body(da_ref, na_ref, db_ref, nb_ref, ua_ref, ub_ref, cwa_ref, cwb_ref, wa_ref, wb_ref, h1_ref, g2_ref, dh2_ref,
             dupa_ref, dupb_ref, dh1_ref, dg2_ref, dcwa_ref, dcwb_ref, acc_ref, ext_ref):
        i, j = pl.program_id(0), pl.program_id(1)
        edge = (i % tps) == tps - 1

        @pl.when((i == 0) & (j == 0))
        def _():
            dcwa_ref[...] = jnp.zeros_like(dcwa_ref)
            dcwb_ref[...] = jnp.zeros_like(dcwb_ref)

        outs = []
        for d_ref, n_ref, u_ref, cw_ref, o_ref, dcw_ref in (
                (da_ref, na_ref, ua_ref, cwa_ref, dupa_ref, dcwa_ref),
                (db_ref, nb_ref, ub_ref, cwb_ref, dupb_ref, dcwb_ref)):
            t = _conv_taps(d_ref, n_ref, ext_ref, edge, tm, False)
            cwv = cw_ref[...]
            dup = cwv[2:3] * t[0] + cwv[1:2] * t[1] + cwv[0:1] * t[2]
            o_ref[...] = dup.astype(ACT)
            outs.append(dup)
            u = u_ref[...].astype(F32)
            dcw_ref[j] += jnp.concatenate([jnp.sum(u * t[2 - k], axis=0, keepdims=True) for k in range(3)], axis=0)
        contrib = _dot_nt(outs[0], wa_ref[...]) + _dot_nt(outs[1], wb_ref[...])

        @pl.when(j == 0)
        def _():
            acc_ref[...] = contrib

        @pl.when(j > 0)
        def _():
            acc_ref[...] += contrib

        @pl.when((i == 0) & (j == 0))
        def _():
            dg2_ref[...] = jnp.zeros_like(dg2_ref)

        @pl.when(j == n_pairs - 1)
        def _():
            _, vjp = jax.vjp(_rms, h1_ref[...], g2_ref[...])
            dh, dg = vjp(acc_ref[...])
            dh1_ref[...] = dh2_ref[...] + dh
            dg2_ref[...] += dg

    tile = pl.BlockSpec((tm, D), lambda i, j: (i, 0))
    vec = pl.BlockSpec((1, D), lambda i, j: (0, 0))
    pair = lambda: [pl.BlockSpec((None, tm, F), lambda i, j: (j, i, 0)),
                    pl.BlockSpec((None, PAD, F), lambda i, j: (j, jnp.minimum((i + 1) * hb, last), 0))]
    act = jax.ShapeDtypeStruct((n_pairs, T, F), ACT)
    dcw = jax.ShapeDtypeStruct((n_pairs, 3, F), F32)
    dcw_spec = pl.BlockSpec((n_pairs, 3, F), lambda i, j: (0, 0, 0))
    return pl.pallas_call(
        body, name=name,
        out_shape=(act, act, jax.ShapeDtypeStruct((T, D), F32), jax.ShapeDtypeStruct((1, D), F32), dcw, dcw),
        grid=(T // tm, n_pairs),
        in_specs=pair() + pair() + [
            pl.BlockSpec((None, tm, F), lambda i, j: (j, i, 0)), pl.BlockSpec((None, tm, F), lambda i, j: (j + n_pairs, i, 0)),
            pl.BlockSpec((None, 3, F), lambda i, j: (j, 0, 0)), pl.BlockSpec((None, 3, F), lambda i, j: (j + n_pairs, 0, 0)),
            pl.BlockSpec((None, D, F), lambda i, j: (j, 0, 0)), pl.BlockSpec((None, D, F), lambda i, j: (j + n_pairs, 0, 0)),
            tile, vec, tile],
        out_specs=(pl.BlockSpec((None, tm, F), lambda i, j: (j, i, 0)), pl.BlockSpec((None, tm, F), lambda i, j: (j, i, 0)),
                   tile, vec, dcw_spec, dcw_spec),
        scratch_shapes=[pltpu.VMEM((tm, D), F32), pltpu.VMEM((tm + PAD, F), F32)],
        compiler_params=_cparams())(dua, dua, dub, dub, up, up, cw, cw, wup, wup, h1, g2, dh2)


def _in_bwd_call(dp, w_in, h0, g1, dh1, tm, name):
    T, D = h0.shape
    S, _, N = w_in.shape

    def body(dp_ref, w_ref, h0_ref, g1_ref, dh1_ref, dh0_ref, dg1_ref, acc_ref):
        i, j = pl.program_id(0), pl.program_id(1)
        contrib = _dot_nt(dp_ref[...], w_ref[...])

        @pl.when(j == 0)
        def _():
            acc_ref[...] = contrib

        @pl.when(j > 0)
        def _():
            acc_ref[...] += contrib

        @pl.when((i == 0) & (j == 0))
        def _():
            dg1_ref[...] = jnp.zeros_like(dg1_ref)

        @pl.when(j == S - 1)
        def _():
            _, vjp = jax.vjp(_rms, h0_ref[...], g1_ref[...])
            dh, dg = vjp(acc_ref[...])
            dh0_ref[...] = dh1_ref[...] + dh
            dg1_ref[...] += dg

    tile = pl.BlockSpec((tm, D), lambda i, j: (i, 0))
    vec = pl.BlockSpec((1, D), lambda i, j: (0, 0))
    return pl.pallas_call(
        body, name=name, out_shape=(jax.ShapeDtypeStruct((T, D), F32), jax.ShapeDtypeStruct((1, D), F32)),
        grid=(T // tm, S),
        in_specs=[pl.BlockSpec((tm, N), lambda i, j: (i, j)), pl.BlockSpec((None, D, N), lambda i, j: (j, 0, 0)),
                  tile, vec, tile],
        out_specs=(tile, vec), scratch_shapes=[pltpu.VMEM((tm, D), F32)],
        compiler_params=_cparams())(dp, w_in, h0, g1, dh1)


def _meta_grad_call(dh0_3, name):
    B, L, D = dh0_3.shape

    def body(d_ref, o_ref):
        o_ref[...] = jnp.sum(d_ref[...], axis=0)

    return pl.pallas_call(
        body, name=name, out_shape=jax.ShapeDtypeStruct((N_META, D), F32), grid=(1,),
        in_specs=[pl.BlockSpec((B, N_META, D), lambda i: (0, 0, 0))],
        out_specs=pl.BlockSpec((N_META, D), lambda i: (0, 0)), compiler_params=_cparams())(dh0_3)


_RELS = [(dx, dy, dc) for dx in (0, 1) for dy in (0, 1) for dc in (0, 1)][1:]


def _exchange_call(arrs, scatter, name):
    n = len(arrs)
    n_rel = len(_RELS)

    def body(*refs):
        ins, outs = refs[:n], refs[n:2 * n]
        send_sems, recv_sems, loc_sems = refs[2 * n:]
        x, y, c = lax.axis_index("x"), lax.axis_index("y"), lax.axis_index("c")
        me = 4 * x + 2 * y + c
        started = []
        for k in range(n):
            src_me = ins[k].at[me] if scatter else ins[k]
            loc = pltpu.make_async_copy(src_me, outs[k].at[me], loc_sems.at[k])
            loc.start()
            started.append(loc)
        waits = []
        for r, (dx, dy, dc) in enumerate(_RELS):
            px, py, pc = (x + dx) % 2, (y + dy) % 2, (c + dc) % 2
            pid = 4 * px + 2 * py + pc
            for k in range(n):
                s = k * n_rel + r
                src = ins[k].at[pid] if scatter else ins[k]
                cp = pltpu.make_async_remote_copy(
                    src_ref=src, dst_ref=outs[k].at[me], send_sem=send_sems.at[s], recv_sem=recv_sems.at[s],
                    device_id=(px, py, pc), device_id_type=pl.DeviceIdType.MESH)
                cp.start()
                waits.append(pltpu.make_async_remote_copy(
                    src_ref=src, dst_ref=outs[k].at[pid], send_sem=send_sems.at[s], recv_sem=recv_sems.at[s],
                    device_id=(px, py, pc), device_id_type=pl.DeviceIdType.MESH))
        for w in waits:
            w.wait_send()
            w.wait_recv()
        for loc in started:
            loc.wait()

    out_shape = tuple(jax.ShapeDtypeStruct(a.shape if scatter else (N_DEV,) + a.shape, a.dtype) for a in arrs)
    hbm = pl.BlockSpec(memory_space=pl.ANY)
    return pl.pallas_call(
        body, name=name, out_shape=out_shape, in_specs=[hbm] * n, out_specs=tuple([hbm] * n),
        scratch_shapes=[pltpu.SemaphoreType.DMA((n * n_rel,)), pltpu.SemaphoreType.DMA((n * n_rel,)),
                        pltpu.SemaphoreType.DMA((n,))],
        compiler_params=pltpu.CompilerParams(has_side_effects=True))(*arrs)


_HBM = pl.BlockSpec(memory_space=pltpu.HBM)
_SEM = pl.BlockSpec(memory_space=pltpu.SEMAPHORE)
_DATAFLOW = pltpu.SideEffectType.DATAFLOW_SIDE_EFFECTING


def _peer_copies(ins, lands, send_sems, recv_sems, scatter):
    n = len(ins)
    x, y, c = lax.axis_index("x"), lax.axis_index("y"), lax.axis_index("c")
    me = 4 * x + 2 * y + c
    sends, arrivals = [], []
    for r, (dx, dy, dc) in enumerate(_RELS):
        px, py, pc = (x + dx) % 2, (y + dy) % 2, (c + dc) % 2
        pid = 4 * px + 2 * py + pc
        for k in range(n):
            s = k * len(_RELS) + r
            src = ins[k].at[pid] if scatter else ins[k]
            for dst, out in ((lands[k].at[me], sends), (lands[k].at[pid], arrivals)):
                out.append(pltpu.make_async_remote_copy(
                    src_ref=src, dst_ref=dst, send_sem=send_sems.at[s], recv_sem=recv_sems.at[s],
                    device_id=(px, py, pc), device_id_type=pl.DeviceIdType.MESH))
    return sends, arrivals


def _exchange_start(arrs, scatter, name):
    n = len(arrs)
    n_sem = n * len(_RELS)

    def body(*refs):
        ins, lands = refs[:n], refs[n:2 * n]
        send_sems, recv_sems = refs[2 * n], refs[2 * n + 1]
        token = refs[-1]
        sends, _ = _peer_copies(ins, lands, send_sems, recv_sems, scatter)
        for cp in sends:
            cp.start()
        token[...] = jnp.zeros_like(token)

    land_shapes = [a.shape if scatter else (N_DEV,) + a.shape for a in arrs]
    ops = [pltpu.with_memory_space_constraint(a, pltpu.HBM) for a in arrs]
    ops += [pltpu.with_memory_space_constraint(lax.empty(s, a.dtype), pltpu.HBM) for s, a in zip(land_shapes, arrs)]
    out = pl.pallas_call(
        body, name=name,
        out_shape=(pltpu.SemaphoreType.DMA((n_sem,)), pltpu.SemaphoreType.DMA((n_sem,)),
                   *[pltpu.HBM(a.shape, a.dtype) for a in arrs],
                   *[pltpu.HBM(s, a.dtype) for s, a in zip(land_shapes, arrs)],
                   jax.ShapeDtypeStruct((SUBLANES, LANES), F32)),
        in_specs=[_HBM] * (2 * n),
        out_specs=(_SEM, _SEM, *[_HBM] * (2 * n), pl.BlockSpec(memory_space=pltpu.VMEM)),
        input_output_aliases={i: 2 + i for i in range(2 * n)},
        compiler_params=pltpu.CompilerParams(has_side_effects=_DATAFLOW))(*ops)
    return out[0], out[1], list(out[2:2 + n]), list(out[2 + n:2 + 2 * n]), out[-1]


def _exchange_wait(started, after, scatter, name):
    send_sems, recv_sems, srcs, lands, _ = started
    n = len(srcs)

    def body(*refs):
        ins, lands_ = refs[:n], refs[n:2 * n]
        _, arrivals = _peer_copies(ins, lands_, refs[2 * n], refs[2 * n + 1], scatter)
        for cp in arrivals:
            cp.wait_send()
            cp.wait_recv()

    out = pl.pallas_call(
        body, name=name,
        out_shape=(*[pltpu.HBM(a.shape, a.dtype) for a in srcs], *[pltpu.HBM(a.shape, a.dtype) for a in lands]),
        in_specs=[_HBM] * (2 * n) + [_SEM, _SEM, pl.BlockSpec(memory_space=pl.ANY)],
        out_specs=tuple([_HBM] * (2 * n)), input_output_aliases={i: i for i in range(2 * n)},
        compiler_params=pltpu.CompilerParams(has_side_effects=_DATAFLOW))(*srcs, *lands, send_sems, recv_sems, after)
    return list(out[:n]), list(out[n:])


def _place_own_call(srcs, lands, scatter, me, name):
    outs = []
    for k, (src, land) in enumerate(zip(srcs, lands)):
        R, C = land.shape[1:]
        tr = R
        while tr % 32 == 0 and tr * C * land.dtype.itemsize > 2 * 1024 * 1024:
            tr //= 2

        def body(me_ref, s_ref, l_ref, o_ref):
            o_ref[...] = s_ref[...]

        src_spec = (pl.BlockSpec((None, tr, C), lambda i, me_ref: (me_ref[0], i, 0)) if scatter
                    else pl.BlockSpec((tr, C), lambda i, me_ref: (i, 0)))
        outs.append(pl.pallas_call(
            body, name=f"{name}_{k}", out_shape=jax.ShapeDtypeStruct(land.shape, land.dtype),
            grid_spec=pltpu.PrefetchScalarGridSpec(
                num_scalar_prefetch=1, grid=(R // tr,),
                in_specs=[src_spec, pl.BlockSpec(memory_space=pl.ANY)],
                out_specs=pl.BlockSpec((None, tr, C), lambda i, me_ref: (me_ref[0], i, 0))),
            input_output_aliases={2: 0}, compiler_params=_cparams())(me, src, land))
    return outs


def _adamw_shard_call(w, parts, m, v, name):
    R, C = w.shape
    tr = _tile(R, 128) if R % 16 == 0 else R

    def body(w_ref, p_ref, m_ref, v_ref, g_ref, d_ref, nm_ref, nv_ref):
        g = p_ref[0].astype(F32)
        for s in range(1, N_DEV):
            g = g + p_ref[s].astype(F32)
        d, nm, nv = _adamw(w_ref[...], g, m_ref[...], v_ref[...])
        g_ref[...] = g
        d_ref[...] = d
        nm_ref[...] = nm
        nv_ref[...] = nv

    tile = pl.BlockSpec((tr, C), lambda i: (i, 0))
    sh = jax.ShapeDtypeStruct((R, C), F32)
    return pl.pallas_call(
        body, name=name, out_shape=(sh, sh, sh, sh), grid=(R // tr,),
        in_specs=[tile, pl.BlockSpec((N_DEV, tr, C), lambda i: (0, i, 0)), tile, tile],
        out_specs=(tile, tile, tile, tile), compiler_params=_cparams())(w, parts, m, v)


def _pack(arrs, rows_mult=SUBLANES):
    flat = jnp.concatenate([a.reshape(-1).astype(F32) for a in arrs])
    n = flat.shape[0]
    per = rows_mult * LANES
    total = -(-n // per) * per
    return jnp.pad(flat, (0, total - n)).reshape(total // LANES, LANES)


def _unpack(pack, shapes):
    flat = pack.reshape(-1)
    out, off = [], 0
    for s in shapes:
        n = 1
        for d in s:
            n *= d
        out.append(flat[off:off + n].reshape(s))
        off += n
    return out


def kernel(x, meta_tokens, mix_norm_g, w_in, ssm_lambda_re, ssm_lambda_im, ssm_log_dt, ssm_b_re, ssm_b_im, ssm_c_re, ssm_c_im, ssm_d, ssm_w_glu, w_ssm_proj, hgrn_lb_logits, hgrn_norm_g, w_hgrn_proj, w_out, ffn_norm_g, w_up, conv_w, conv_b, w_down, final_norm_g, loss_target, m_meta_tokens, m_mix_norm_g, m_w_in, m_ssm_lambda_re, m_ssm_lambda_im, m_ssm_log_dt, m_ssm_b_re, m_ssm_b_im, m_ssm_c_re, m_ssm_c_im, m_ssm_d, m_ssm_w_glu, m_w_ssm_proj, m_hgrn_lb_logits, m_hgrn_norm_g, m_w_hgrn_proj, m_w_out, m_ffn_norm_g, m_w_up, m_conv_w, m_conv_b, m_w_down, m_final_norm_g, v_meta_tokens, v_mix_norm_g, v_w_in, v_ssm_lambda_re, v_ssm_lambda_im, v_ssm_log_dt, v_ssm_b_re, v_ssm_b_im, v_ssm_c_re, v_ssm_c_im, v_ssm_d, v_ssm_w_glu, v_w_ssm_proj, v_hgrn_lb_logits, v_hgrn_norm_g, v_w_hgrn_proj, v_w_out, v_ffn_norm_g, v_w_up, v_conv_w, v_conv_b, v_w_down, v_final_norm_g):
    args = dict(locals())
    B, S_len, D = x.shape
    L = S_len + N_META
    T = B * L
    tm = _tile(L, ROW_TILE_CAP)
    tps = L // tm
    G, P = ssm_lambda_re.shape[1:]
    H = ssm_b_re.shape[-1]
    W = G * H
    n_cb = W // LANES
    gpb = G // n_cb
    hd = hgrn_norm_g.shape[1]
    n_heads = D // hd
    n_in = w_in.shape[2]
    F = w_up.shape[2]
    assert W == D and n_in % LANES == 0

    me = (4 * lax.axis_index("x") + 2 * lax.axis_index("y") + lax.axis_index("c")).astype(jnp.int32).reshape(1)
    meta_g, cw_g = _exchange_call([meta_tokens, conv_w[0]], False, "gather_small_params")
    ga = _exchange_start([w_in[0].astype(MXU)], False, "gather_a_start")
    gb = _exchange_start(
        [w_up[0].astype(MXU), ssm_w_glu[0].astype(MXU), w_ssm_proj[0].astype(MXU), w_hgrn_proj[0].astype(MXU),
         w_out[0].astype(MXU), w_down[0].astype(MXU)], False, "gather_b_start")
    started_tok = (ga[4] + gb[4])[0:1, 0:1]
    meta_full = meta_g.transpose(1, 0, 2).reshape(N_META, D)
    cb_g = conv_b.reshape(N_DEV, 1, F)

    h0 = jnp.concatenate([jnp.broadcast_to(meta_full[None], (B, N_META, D)), x], axis=1).reshape(T, D)
    tgt = jnp.concatenate([jnp.zeros((B, N_META, D), F32), loss_target], axis=1).reshape(T, D)

    lr, li = ssm_lambda_re[0], ssm_lambda_im[0]
    ldt = ssm_log_dt[0].reshape(G, 1)
    bt_re = ssm_b_re[0].transpose(2, 0, 1).reshape(H, G * P)
    bt_im = ssm_b_im[0].transpose(2, 0, 1).reshape(H, G * P)
    seg = _seg_len(L)
    a_re, a_im, as_re, as_im, coef_re, coef_im = _small_call(
        _disc_a_power(seg), [lr, li, ldt], [((G, P), F32)] * 6, "s5_discretise")
    bbt_re, bbt_im = _small_call(
        _disc_b, [coef_re.reshape(1, G * P), coef_im.reshape(1, G * P), bt_re, bt_im],
        [((H, G * P), F32)] * 2, "s5_input_matrix")
    eye = jnp.eye(gpb, dtype=F32)
    hw = gpb * P

    def expand_b(bbt):
        t = bbt.reshape(H, n_cb, gpb, P).transpose(1, 0, 2, 3)[:, None]
        return (t * eye[None, :, None, :, None]).reshape(n_cb, gpb * H, hw)

    def expand_c(cm):
        t = cm.reshape(n_cb, gpb, H, P).transpose(0, 1, 3, 2)[:, :, :, None]
        return (t * eye[None, :, None, :, None]).reshape(n_cb, hw, gpb * H)

    wb = jnp.concatenate([expand_b(bbt_re), expand_b(bbt_im)], axis=2).astype(MXU)
    wc = jnp.concatenate([expand_c(ssm_c_re[0]), -expand_c(ssm_c_im[0])], axis=1).astype(MXU)
    tab = jnp.stack([jnp.concatenate([a_re.reshape(n_cb, hw), a_im.reshape(n_cb, hw)], axis=1),
                     jnp.concatenate([as_re.reshape(n_cb, hw), as_im.reshape(n_cb, hw)], axis=1)], axis=1)
    tab = jnp.broadcast_to(tab[:, :, None, :], (n_cb, 2, SUBLANES, 2 * hw))
    dsk = ssm_d.reshape(n_cb, 1, LANES)
    lb = _small_call(_lb_fn, [hgrn_lb_logits], [((1, D), F32)], "hgrn_lower_bound")[0]

    z1 = _norm_call(h0, mix_norm_g + started_tok, tm, "mix_norm")
    ready = jnp.concatenate([t[(0,) * (t.ndim - 1)][0:1].astype(F32) for t in (z1, wb, wc, tab, lb, tgt)])
    ga_src, ga_land = _exchange_wait(ga, ready, False, "gather_a_wait")
    win_g = _place_own_call(ga_src, ga_land, False, me, "gather_a_own")[0]
    p = _mm_shard(z1, win_g, tm, "in_proj", False)
    p3 = p.reshape(B, L, p.shape[1])
    u_seg = _to_segments(p3[:, :, :W], seg)
    ya_seg, s_all = _s5_fwd_call(u_seg, wb, wc, tab, dsk, "s5_fwd")
    ya = _from_segments(ya_seg, seg, L).reshape(T, W)
    gb_src, gb_land = _exchange_wait(gb, ya, False, "gather_b_wait")
    gathered = _place_own_call(gb_src, gb_land, False, me, "gather_b_own")
    wup_g = gathered[0]
    wglu_g, wsp_g, whp_g, wout_g = [g.reshape(D, D) for g in gathered[1:5]]
    wdn_g = gathered[5].reshape(N_DEV // 2, 2 * w_down.shape[1], D)
    yo, a_br = _glu_proj_call(ya, wglu_g, wsp_g, tm, "s5_glu_proj")
    yb = _hgrn_fwd_call(p3, lb, hgrn_norm_g, n_heads, n_cb, "hgrn_fwd").reshape(T, D)
    col_ga = 5
    h1, mg, bm, z2 = _merge_call(yb, a_br, p, h0, whp_g, wout_g, ffn_norm_g, col_ga, tm, "merge")
    up = _mm_shard(z2, wup_g, tm, "up_proj", True)
    conv_a, conv_b_out, dh2, loss_part, dg3 = _ffn_fwd_call(up, cw_g, cb_g, wdn_g, h1, tgt, final_norm_g.reshape(1, D),
                                                            tm, tps, "ffn_out_loss")

    dua, dub, dwd, dcba, dcbb = _ffn_bwd_a_call(dh2, conv_a, conv_b_out, wdn_g, tm, "ffn_bwd_gate")
    dupa, dupb, dh1, dg2, dcwa, dcwb = _ffn_bwd_b_call(dua, dub, up, cw_g, wup_g, h1, ffn_norm_g, dh2, tm, tps,
                                                       "ffn_bwd_up")
    dwup = jnp.concatenate([_mm_tn(z2, dupa, N_DEV // 2, tm, "dw_up_a", True),
                            _mm_tn(z2, dupb, N_DEV // 2, tm, "dw_up_b", True)], axis=0)
    sh_rows = D // N_DEV
    sa = _exchange_start([dwup, dwd.reshape(N_DEV, w_down.shape[1], D)], True, "scatter_a_start")
    dmg, dwout = _lin_bwd(mg, dh1, wout_g + sa[4][0:1, 0:1].astype(MXU), tm, "out_proj_bwd")
    da_br, dbm, dga, dgb = _merge_bwd_call(dmg, a_br, bm, p, col_ga, tm, "merge_bwd")
    dyo, dwsp = _lin_bwd(yo, da_br, wsp_g, tm, "ssm_proj_bwd")
    dyb, dwhp = _lin_bwd(yb, dbm, whp_g, tm, "hgrn_proj_bwd")
    dya, dwglu = _glu_bwd_call(ya, dyo, wglu_g, tm, "s5_glu_bwd")
    sb = _exchange_start([t.reshape(N_DEV, sh_rows, D) for t in (dwglu, dwsp, dwhp, dwout)], True, "scatter_b_start")
    tok_b = sb[4][0:1, :]
    du_seg, dwb, dwc, dab, ddsk = _s5_bwd_call(u_seg, s_all, _to_segments(dya.reshape(B, L, W), seg), wb, wc, tab,
                                               dsk + tok_b[None], "s5_bwd")
    du = _from_segments(du_seg, seg, L)

    def diag_b(dw):
        t = (dw.reshape(n_cb, gpb, H, gpb, P) * eye[None, :, None, :, None]).sum(axis=1)
        return t.transpose(1, 0, 2, 3).reshape(H, G * P)

    def diag_c(dw):
        t = (dw.reshape(n_cb, gpb, P, gpb, H) * eye[None, :, None, :, None]).sum(axis=3)
        return t.transpose(0, 1, 3, 2).reshape(G, H, P)

    early_parts = [dab[:, 0, :hw].reshape(G, P), dab[:, 0, hw:].reshape(G, P), ddsk.reshape(1, D)]
    early = [_pack(early_parts), diag_b(dwb[:, :, :hw]), diag_b(dwb[:, :, hw:]),
             diag_c(dwc[:, :hw]).reshape(G * H, P), -diag_c(dwc[:, hw:]).reshape(G * H, P)]
    se = _exchange_start(early, False, "gather_s5_grads_start")
    dq, dfl, di, dog, dlb, dng = _hgrn_bwd_call(p3, dyb.reshape(B, L, D), lb, hgrn_norm_g + tok_b + se[4][0:1, :],
                                                n_heads, n_cb, "hgrn_bwd")
    dp = jnp.concatenate([du.reshape(T, W), dq.reshape(T, D), dfl.reshape(T, D), di.reshape(T, D),
                          dog.reshape(T, D), dga, dgb], axis=1)
    dwin = _mm_tn(z1, dp, N_DEV, tm, "dw_in", False)
    sc = _exchange_start([dwin.astype(WIRE)], True, "scatter_c_start")
    dh0, dg1 = _in_bwd_call(dp, win_g, h0, mix_norm_g + sc[4][0:1, 0:1], dh1, tm, "in_proj_bwd")
    dh0_3 = dh0.reshape(B, L, D)
    grad_x = dh0_3[:, N_META:]
    dmeta = _meta_grad_call(dh0_3, "meta_grad")

    late_parts = [dg1, dlb, dng, dg2, jnp.concatenate([dcba, dcbb], axis=0).reshape(1, N_DEV * F), dg3, loss_part]
    late_pack = _pack(late_parts)

    dcw = jnp.concatenate([dcwa, dcwb], axis=0)
    dmeta_s = dmeta.reshape(N_META, N_DEV, D // N_DEV).transpose(1, 0, 2)
    parts_d = _exchange_call([dmeta_s, dcw], True, "scatter_small_grads")
    late_all = _exchange_call([late_pack], False, "gather_small_grads")[0]
    early_all = _place_own_call(*_exchange_wait(se, late_all, False, "gather_s5_grads_wait"), False, me,
                                "gather_s5_grads_own")
    parts_a = _place_own_call(*_exchange_wait(sa, late_all, True, "scatter_a_wait"), True, me, "scatter_a_own")
    parts_b = _place_own_call(*_exchange_wait(sb, late_all, True, "scatter_b_wait"), True, me, "scatter_b_own")
    parts_c = _place_own_call(*_exchange_wait(sc, late_all, True, "scatter_c_wait"), True, me, "scatter_c_own")
    parts = [parts_c[0], parts_a[0], *parts_b, parts_a[1], parts_d[0], parts_d[1]]

    def sum8(*gathered):
        out = []
        for a in gathered:
            t = a[0]
            for s in range(1, N_DEV):
                t = t + a[s]
            out.append(t)
        return tuple(out)

    sums = _small_call(sum8, [*early_all, late_all], [(a.shape, F32) for a in (*early, late_pack)], "sum_small_grads")
    t_abr, t_abi, g_dsk = _unpack(sums[0], [a.shape for a in early_parts])
    t_bbr, t_bbi = sums[1], sums[2]
    g_cre, g_cim = sums[3].reshape(G, H, P), sums[4].reshape(G, H, P)
    g_g1, t_lb, g_ng, g_g2, g_cb, g_g3, loss_v = _unpack(sums[5], [a.shape for a in late_parts])

    def disc_b_bwd(cr, ci, br, bi, dbr, dbi):
        _, vjp = jax.vjp(_disc_b, cr, ci, br, bi)
        return vjp((dbr, dbi))

    t_cr, t_ci, g_btr, g_bti = _small_call(
        disc_b_bwd, [coef_re.reshape(1, G * P), coef_im.reshape(1, G * P), bt_re, bt_im, t_bbr, t_bbi],
        [((1, G * P), F32)] * 2 + [((H, G * P), F32)] * 2, "s5_input_matrix_bwd")

    def disc_a_bwd(lr_, li_, ldt_, dar, dai, dcr, dci):
        _, vjp = jax.vjp(_disc_a, lr_, li_, ldt_)
        return vjp((dar, dai, dcr, dci))

    g_lr, g_li, g_ldt = _small_call(
        disc_a_bwd, [lr, li, ldt, t_abr, t_abi, t_cr.reshape(G, P), t_ci.reshape(G, P)],
        [((G, P), F32)] * 2 + [((G, 1), F32)], "s5_discretise_bwd")

    def lb_bwd(logits, d):
        _, vjp = jax.vjp(_lb_fn, logits)
        return vjp(d)

    g_lbl = _small_call(lb_bwd, [hgrn_lb_logits, t_lb], [(hgrn_lb_logits.shape, F32)], "hgrn_lower_bound_bwd")[0]

    grads = dict(
        mix_norm_g=g_g1, ssm_lambda_re=g_lr[None], ssm_lambda_im=g_li[None], ssm_log_dt=g_ldt.reshape(1, G),
        ssm_b_re=g_btr.reshape(H, G, P).transpose(1, 2, 0)[None], ssm_b_im=g_bti.reshape(H, G, P).transpose(1, 2, 0)[None],
        ssm_c_re=g_cre[None], ssm_c_im=g_cim[None], ssm_d=g_dsk, hgrn_lb_logits=g_lbl, hgrn_norm_g=g_ng,
        ffn_norm_g=g_g2, conv_b=g_cb.reshape(1, N_DEV * F), final_norm_g=g_g3.reshape(D))
    loss = loss_v[0, 0]

    delta, new_m, new_v = {}, {}, {}
    sharded = [("w_in", parts[0], (D, n_in)), ("w_up", parts[1], (D, F)), ("ssm_w_glu", parts[2], (sh_rows, D)),
               ("w_ssm_proj", parts[3], (sh_rows, D)), ("w_hgrn_proj", parts[4], (sh_rows, D)),
               ("w_out", parts[5], (sh_rows, D)), ("w_down", parts[6], (w_down.shape[1], D)),
               ("meta_tokens", parts[7], (N_META, D // N_DEV)), ("conv_w", parts[8], (3, F))]
    for name, part, shp in sharded:
        full = args[name].shape
        g, d_, nm, nv = _adamw_shard_call(args[name].reshape(shp), part, args["m_" + name].reshape(shp),
                                          args["v_" + name].reshape(shp), "adamw_" + name)
        grads[name], delta[name], new_m[name], new_v[name] = [t.reshape(full) for t in (g, d_, nm, nv)]

    for n, shp in (("ssm_b_re", (G * P, H)), ("ssm_b_im", (G * P, H)), ("ssm_c_re", (G * H, P)), ("ssm_c_im", (G * H, P))):
        outs = _small_call(_adamw, [t.reshape(shp) for t in (args[n], grads[n], args["m_" + n], args["v_" + n])],
                           [(shp, F32)] * 3, "adamw_" + n)
        delta[n], new_m[n], new_v[n] = [o.reshape(args[n].shape) for o in outs]
    rep = ["mix_norm_g", "ssm_lambda_re", "ssm_lambda_im", "ssm_log_dt", "ssm_d", "hgrn_lb_logits", "hgrn_norm_g",
           "ffn_norm_g", "conv_b", "final_norm_g"]
    rep_shapes = [args[n].shape for n in rep]
    packs = [_pack([args[pre + n] for n in rep]) for pre in ("", "m_", "v_")]
    g_pack = _pack([grads[n] for n in rep])
    outs = _small_call(lambda w, g, m, v: _adamw(w, g, m, v), [packs[0], g_pack, packs[1], packs[2]],
                       [(g_pack.shape, F32)] * 3, "adamw_replicated")
    for n, d_, nm, nv in zip(rep, *[_unpack(o, rep_shapes) for o in outs]):
        delta[n], new_m[n], new_v[n] = d_, nm, nv

    names = ["meta_tokens", "mix_norm_g", "w_in", "ssm_lambda_re", "ssm_lambda_im", "ssm_log_dt", "ssm_b_re",
             "ssm_b_im", "ssm_c_re", "ssm_c_im", "ssm_d", "ssm_w_glu", "w_ssm_proj", "hgrn_lb_logits", "hgrn_norm_g",
             "w_hgrn_proj", "w_out", "ffn_norm_g", "w_up", "conv_w", "conv_b", "w_down", "final_norm_g"]
    return (loss, grad_x, *[grads[n] for n in names], *[delta[n] for n in names],
            *[new_m[n] for n in names], *[new_v[n] for n in names])
```

```python
import jax
import jax.numpy as jnp
from jax import lax
from jax.experimental import pallas as pl
from jax.experimental.pallas import tpu as pltpu

F32 = jnp.float32
MXU = jnp.bfloat16
ACT = jnp.bfloat16
WIRE = jnp.bfloat16
N_DEV = 8
N_META = 16
CHUNK = 16
EPS = 1e-6
ADAM_LR, ADAM_B1, ADAM_B2, ADAM_EPS, ADAM_WD, ADAM_STEP = 0.001, 0.9, 0.999, 1e-08, 0.01, 10
SUBLANES = 8
LANES = 128
ROW_TILE_CAP = 700
VMEM_LIMIT = 60 * 1024 * 1024


def _cparams(**kw):
    return pltpu.CompilerParams(vmem_limit_bytes=VMEM_LIMIT, **kw)


def _tile(n, cap):
    best = None
    for t in range(16, min(n, cap) + 1, 16):
        if n % t == 0:
            best = t
    assert best is not None, (n, cap)
    return best


def _dot(a, b):
    return lax.dot_general(a.astype(MXU), b.astype(MXU), (((1,), (0,)), ((), ())), preferred_element_type=F32)


def _dot_nt(a, b):
    return lax.dot_general(a.astype(MXU), b.astype(MXU), (((1,), (1,)), ((), ())), preferred_element_type=F32)


def _dot_tn(a, b):
    return lax.dot_general(a.astype(MXU), b.astype(MXU), (((0,), (0,)), ((), ())), preferred_element_type=F32)


def _rms(x, g):
    return x * lax.rsqrt(jnp.mean(x * x, axis=-1, keepdims=True) + EPS) * g


def _silu(x):
    return x * jax.nn.sigmoid(x)


def _small_call(fn, ins, out_shapes, name):
    n_in = len(ins)

    def body(*refs):
        outs = fn(*[r[...] for r in refs[:n_in]])
        outs = outs if isinstance(outs, (tuple, list)) else (outs,)
        for r, o in zip(refs[n_in:], outs):
            r[...] = o.astype(r.dtype)

    vm = pl.BlockSpec(memory_space=pltpu.VMEM)
    return pl.pallas_call(
        body, name=name, out_shape=tuple(jax.ShapeDtypeStruct(s, d) for s, d in out_shapes),
        in_specs=[vm] * n_in, out_specs=tuple([vm] * len(out_shapes)), compiler_params=_cparams())(*ins)


def _disc_a(lr, li, ldt):
    dt = jnp.exp(ldt)
    mag = jnp.exp(lr * dt)
    ab_re = mag * jnp.cos(li * dt)
    ab_im = mag * jnp.sin(li * dt)
    den = lr * lr + li * li
    nr = ab_re - 1.0
    coef_re = (nr * lr + ab_im * li) / den
    coef_im = (ab_im * lr - nr * li) / den
    return ab_re, ab_im, coef_re, coef_im


def _disc_a_power(n):
    def fn(lr, li, ldt):
        ab_re, ab_im, coef_re, coef_im = _disc_a(lr, li, ldt)
        pr, pi, sr, si, m = None, None, ab_re, ab_im, n
        while m:
            if m & 1:
                pr, pi = (sr, si) if pr is None else (pr * sr - pi * si, pr * si + pi * sr)
            m >>= 1
            if m:
                sr, si = sr * sr - si * si, 2.0 * sr * si
        return ab_re, ab_im, pr, pi, coef_re, coef_im
    return fn


def _disc_b(coef_re, coef_im, bt_re, bt_im):
    return coef_re * bt_re - coef_im * bt_im, coef_re * bt_im + coef_im * bt_re


def _lb_fn(logits):
    return jax.nn.softmax(logits, axis=0)[0:1]


def _adamw(w, g, m, v):
    m = ADAM_B1 * m + (1.0 - ADAM_B1) * g
    v = ADAM_B2 * v + (1.0 - ADAM_B2) * jnp.square(g)
    m_hat = m / (1.0 - ADAM_B1 ** ADAM_STEP)
    v_hat = v / (1.0 - ADAM_B2 ** ADAM_STEP)
    delta = -ADAM_LR * (m_hat / (jnp.sqrt(v_hat) + ADAM_EPS) + ADAM_WD * w)
    return delta, m, v


def _norm_call(h, g, tm, name):
    T, D = h.shape

    def body(h_ref, g_ref, z_ref):
        z_ref[...] = _rms(h_ref[...], g_ref[...]).astype(ACT)

    return pl.pallas_call(
        body, name=name, out_shape=jax.ShapeDtypeStruct((T, D), ACT), grid=(T // tm,),
        in_specs=[pl.BlockSpec((tm, D), lambda i: (i, 0)), pl.BlockSpec((1, D), lambda i: (0, 0))],
        out_specs=pl.BlockSpec((tm, D), lambda i: (i, 0)), compiler_params=_cparams())(h, g)


def _mm_shard(x, w, tm, name, major):
    T, K = x.shape
    S, _, N = w.shape

    def body(x_ref, w_ref, o_ref):
        o_ref[...] = _dot(x_ref[...], w_ref[...]).astype(o_ref.dtype)

    if major:
        out_shape = jax.ShapeDtypeStruct((S, T, N), ACT)
        out_spec = pl.BlockSpec((None, tm, N), lambda j, i: (j, i, 0))
    else:
        out_shape = jax.ShapeDtypeStruct((T, S * N), ACT)
        out_spec = pl.BlockSpec((tm, N), lambda j, i: (i, j))
    return pl.pallas_call(
        body, name=name, out_shape=out_shape, grid=(S, T // tm),
        in_specs=[pl.BlockSpec((tm, K), lambda j, i: (i, 0)), pl.BlockSpec((None, K, N), lambda j, i: (j, 0, 0))],
        out_specs=out_spec, compiler_params=_cparams())(x, w)


def _mm_tn(x, y, n_shards, tm, name, major):
    T, K = x.shape
    S = n_shards
    N = y.shape[-1] if major else y.shape[-1] // S

    def body(x_ref, y_ref, o_ref):
        @pl.when(pl.program_id(1) == 0)
        def _():
            o_ref[...] = jnp.zeros_like(o_ref)
        o_ref[...] += _dot_tn(x_ref[...], y_ref[...])

    y_spec = (pl.BlockSpec((None, tm, N), lambda j, i: (j, i, 0)) if major
              else pl.BlockSpec((tm, N), lambda j, i: (i, j)))
    return pl.pallas_call(
        body, name=name, out_shape=jax.ShapeDtypeStruct((S, K, N), F32), grid=(S, T // tm),
        in_specs=[pl.BlockSpec((tm, K), lambda j, i: (i, 0)), y_spec],
        out_specs=pl.BlockSpec((None, K, N), lambda j, i: (j, 0, 0)), compiler_params=_cparams())(x, y)


def _lin_bwd(x, dy, w, tm, name):
    T, K = x.shape
    N = dy.shape[1]

    def body(x_ref, dy_ref, w_ref, dx_ref, dw_ref):
        @pl.when(pl.program_id(0) == 0)
        def _():
            dw_ref[...] = jnp.zeros_like(dw_ref)
        dy = dy_ref[...]
        dx_ref[...] = _dot_nt(dy, w_ref[...]).astype(dx_ref.dtype)
        dw_ref[...] += _dot_tn(x_ref[...], dy)

    return pl.pallas_call(
        body, name=name,
        out_shape=(jax.ShapeDtypeStruct((T, K), ACT), jax.ShapeDtypeStruct((K, N), F32)), grid=(T // tm,),
        in_specs=[pl.BlockSpec((tm, K), lambda i: (i, 0)), pl.BlockSpec((tm, N), lambda i: (i, 0)),
                  pl.BlockSpec((K, N), lambda i: (0, 0))],
        out_specs=(pl.BlockSpec((tm, K), lambda i: (i, 0)), pl.BlockSpec((K, N), lambda i: (0, 0))),
        compiler_params=_cparams())(x, dy, w)


N_SEG = SUBLANES
CHAIN_STEPS = 8


def _chain_loop(n, step, init):
    per = min(CHAIN_STEPS, n)

    def trip(t, carry):
        for u in range(per):
            carry = step(t * per + u, carry)
        return carry

    carry = lax.fori_loop(0, n // per, trip, init)
    for i in range(n // per * per, n):
        carry = step(jnp.int32(i), carry)
    return carry


def _seg_len(L):
    return -(-L // (N_SEG * SUBLANES)) * SUBLANES


def _to_segments(a3, seg):
    b, length, c = a3.shape
    a = jnp.pad(a3, ((0, 0), (0, N_SEG * seg - length), (0, 0)))
    return a.reshape(b, N_SEG, seg, c).transpose(0, 2, 1, 3).reshape(b, N_SEG * seg, c)


def _from_segments(a3, seg, length):
    b, _, c = a3.shape
    return a3.reshape(b, seg, N_SEG, c).transpose(0, 2, 1, 3).reshape(b, N_SEG * seg, c)[:, :length]


def _seg_scan(x_ref, tab_ref, n_slabs, reverse):
    hw = x_ref.shape[1] // 2
    sign = -1.0 if reverse else 1.0
    ar, ai = tab_ref[0][:, :hw], sign * tab_ref[0][:, hw:]
    br, bi = tab_ref[1][:, :hw], sign * tab_ref[1][:, hw:]

    def slab(k):
        kk = (n_slabs - 1 - k) if reverse else k
        return pl.ds(pl.multiple_of(kk * SUBLANES, SUBLANES), SUBLANES)

    def horner(k, carry):
        cr, ci = carry
        x = x_ref[slab(k), :]
        return ar * cr - ai * ci + x[:, :hw], ar * ci + ai * cr + x[:, hw:]

    z = jnp.zeros((SUBLANES, hw), F32)
    fr, fi = _chain_loop(n_slabs, horner, (z, z))

    row = lax.broadcasted_iota(jnp.int32, (SUBLANES, hw), 0)
    edge = (row == SUBLANES - 1) if reverse else (row == 0)
    shift = SUBLANES - 1 if reverse else 1
    sr, si = z, z
    for _ in range(N_SEG - 1):
        er, ei = fr + br * sr - bi * si, fi + br * si + bi * sr
        sr = jnp.where(edge, 0.0, pltpu.roll(er, shift, 0))
        si = jnp.where(edge, 0.0, pltpu.roll(ei, shift, 0))

    def scan(k, carry):
        cr, ci = carry
        rows = slab(k)
        x = x_ref[rows, :]
        nr, ni = ar * cr - ai * ci + x[:, :hw], ar * ci + ai * cr + x[:, hw:]
        x_ref[rows, 0:hw] = nr
        x_ref[rows, hw:2 * hw] = ni
        return nr, ni

    _chain_loop(n_slabs, scan, (sr, si))


def _s5_fwd_call(p3, wb, wc, tab_f, dsk, name):
    B, L, _ = p3.shape
    n_cb, cw, sw = wb.shape

    def body(u_ref, wb_ref, wc_ref, tab_ref, d_ref, ya_ref, so_ref, s_ref):
        u = u_ref[...]
        s_ref[...] = _dot(u, wb_ref[...])
        _seg_scan(s_ref, tab_ref, L // SUBLANES, False)
        s = s_ref[...].astype(MXU)
        so_ref[...] = s
        y = _dot(s, wc_ref[...]) + d_ref[...] * u.astype(F32)
        ya_ref[...] = jax.nn.gelu(y).astype(ACT)

    return pl.pallas_call(
        body, name=name,
        out_shape=(jax.ShapeDtypeStruct((B, L, n_cb * cw), ACT), jax.ShapeDtypeStruct((B, n_cb, L, sw), MXU)),
        grid=(B, n_cb),
        in_specs=[pl.BlockSpec((None, L, cw), lambda b, c: (b, 0, c)),
                  pl.BlockSpec((None, cw, sw), lambda b, c: (c, 0, 0)),
                  pl.BlockSpec((None, sw, cw), lambda b, c: (c, 0, 0)),
                  pl.BlockSpec((None, 2, SUBLANES, sw), lambda b, c: (c, 0, 0, 0)),
                  pl.BlockSpec((None, 1, cw), lambda b, c: (c, 0, 0))],
        out_specs=(pl.BlockSpec((None, L, cw), lambda b, c: (b, 0, c)),
                   pl.BlockSpec((None, None, L, sw), lambda b, c: (b, c, 0, 0))),
        scratch_shapes=[pltpu.VMEM((L, sw), F32)], compiler_params=_cparams())(p3, wb, wc, tab_f, dsk)


def _s5_bwd_call(p3, s_all, dya, wb, wc, tab_r, dsk, name):
    B, L, _ = p3.shape
    n_cb, cw, sw = wb.shape
    hw = sw // 2
    n_slabs = L // SUBLANES

    def body(u_ref, si_ref, dya_ref, wb_ref, wc_ref, tr_ref, d_ref,
             du_ref, dwb_ref, dwc_ref, da_ref, dd_ref, s_ref, l_ref):
        @pl.when(pl.program_id(1) == 0)
        def _():
            dwb_ref[...] = jnp.zeros_like(dwb_ref)
            dwc_ref[...] = jnp.zeros_like(dwc_ref)
            da_ref[...] = jnp.zeros_like(da_ref)
            dd_ref[...] = jnp.zeros_like(dd_ref)

        u = u_ref[...]
        uf = u.astype(F32)
        s_in = si_ref[...]
        s_ref[...] = s_in.astype(F32)
        y = _dot(s_in, wc_ref[...]) + d_ref[...] * uf
        _, gelu_vjp = jax.vjp(jax.nn.gelu, y)
        dy = gelu_vjp(dya_ref[...].astype(F32))[0]
        dd_ref[...] += jnp.sum(dy * uf, axis=0, keepdims=True)
        l_ref[...] = _dot_nt(dy, wc_ref[...])
        _seg_scan(l_ref, tr_ref, n_slabs, True)
        du_ref[...] = (_dot_nt(l_ref[...], wb_ref[...]) + d_ref[...] * dy).astype(ACT)
        dwb_ref[...] += _dot_tn(u, l_ref[...])
        dwc_ref[...] += _dot_tn(s_in, dy)

        row = lax.broadcasted_iota(jnp.int32, (SUBLANES, hw), 0)
        last = s_ref[pl.ds((n_slabs - 1) * SUBLANES, SUBLANES), :]
        p0r = jnp.where(row == 0, 0.0, pltpu.roll(last[:, :hw], 1, 0))
        p0i = jnp.where(row == 0, 0.0, pltpu.roll(last[:, hw:], 1, 0))

        def step(k, carry):
            qr, qi, accr, acci = carry
            r0 = pl.multiple_of(k * SUBLANES, SUBLANES)
            s = s_ref[pl.ds(r0, SUBLANES), :]
            lam = l_ref[pl.ds(r0, SUBLANES), :]
            lr, li = lam[:, :hw], lam[:, hw:]
            accr = accr + lr * qr + li * qi
            acci = acci + li * qr - lr * qi
            return s[:, :hw], s[:, hw:], accr, acci

        z8 = jnp.zeros((SUBLANES, hw), F32)
        _, _, accr, acci = _chain_loop(n_slabs, step, (p0r, p0i, z8, z8))
        da_ref[...] += jnp.concatenate([jnp.sum(accr, axis=0, keepdims=True),
                                        jnp.sum(acci, axis=0, keepdims=True)], axis=1)

    W = n_cb * cw
    return pl.pallas_call(
        body, name=name,
        out_shape=(jax.ShapeDtypeStruct((B, L, W), ACT), jax.ShapeDtypeStruct((n_cb, cw, sw), F32),
                   jax.ShapeDtypeStruct((n_cb, sw, cw), F32), jax.ShapeDtypeStruct((n_cb, 1, sw), F32),
                   jax.ShapeDtypeStruct((n_cb, 1, cw), F32)),
        grid=(n_cb, B),
        in_specs=[pl.BlockSpec((None, L, cw), lambda c, b: (b, 0, c)),
                  pl.BlockSpec((None, None, L, sw), lambda c, b: (b, c, 0, 0)),
                  pl.BlockSpec((None, L, cw), lambda c, b: (b, 0, c)),
                  pl.BlockSpec((None, cw, sw), lambda c, b: (c, 0, 0)),
                  pl.BlockSpec((None, sw, cw), lambda c, b: (c, 0, 0)),
                  pl.BlockSpec((None, 2, SUBLANES, sw), lambda c, b: (c, 0, 0, 0)),
                  pl.BlockSpec((None, 1, cw), lambda c, b: (c, 0, 0))],
        out_specs=(pl.BlockSpec((None, L, cw), lambda c, b: (b, 0, c)),
                   pl.BlockSpec((None, cw, sw), lambda c, b: (c, 0, 0)),
                   pl.BlockSpec((None, sw, cw), lambda c, b: (c, 0, 0)),
                   pl.BlockSpec((None, 1, sw), lambda c, b: (c, 0, 0)),
                   pl.BlockSpec((None, 1, cw), lambda c, b: (c, 0, 0))),
        scratch_shapes=[pltpu.VMEM((L, sw), F32), pltpu.VMEM((L, sw), F32)],
        compiler_params=_cparams())(p3, s_all, dya, wb, wc, tab_r, dsk)


def _glu_proj_call(ya, wglu, wproj, tm, name):
    T, W = ya.shape
    D = wproj.shape[1]

    def body(ya_ref, wg_ref, wp_ref, yo_ref, a_ref):
        ya = ya_ref[...]
        yo = ya.astype(F32) * jax.nn.sigmoid(_dot(ya, wg_ref[...]))
        yo_ref[...] = yo.astype(ACT)
        a_ref[...] = _dot(yo, wp_ref[...]).astype(ACT)

    return pl.pallas_call(
        body, name=name, out_shape=(jax.ShapeDtypeStruct((T, W), ACT), jax.ShapeDtypeStruct((T, D), ACT)),
        grid=(T // tm,),
        in_specs=[pl.BlockSpec((tm, W), lambda i: (i, 0)), pl.BlockSpec((W, W), lambda i: (0, 0)),
                  pl.BlockSpec((W, D), lambda i: (0, 0))],
        out_specs=(pl.BlockSpec((tm, W), lambda i: (i, 0)), pl.BlockSpec((tm, D), lambda i: (i, 0))),
        compiler_params=_cparams())(ya, wglu, wproj)


def _glu_bwd_call(ya, dyo, wglu, tm, name):
    T, W = ya.shape

    def body(ya_ref, dyo_ref, wg_ref, dya_ref, dwg_ref):
        @pl.when(pl.program_id(0) == 0)
        def _():
            dwg_ref[...] = jnp.zeros_like(dwg_ref)
        ya = ya_ref[...]
        yaf = ya.astype(F32)
        dyo = dyo_ref[...].astype(F32)
        sg = jax.nn.sigmoid(_dot(ya, wg_ref[...]))
        dt = dyo * yaf * sg * (1.0 - sg)
        dya_ref[...] = (dyo * sg + _dot_nt(dt, wg_ref[...])).astype(ACT)
        dwg_ref[...] += _dot_tn(ya, dt)

    return pl.pallas_call(
        body, name=name, out_shape=(jax.ShapeDtypeStruct((T, W), ACT), jax.ShapeDtypeStruct((W, W), F32)),
        grid=(T // tm,),
        in_specs=[pl.BlockSpec((tm, W), lambda i: (i, 0)), pl.BlockSpec((tm, W), lambda i: (i, 0)),
                  pl.BlockSpec((W, W), lambda i: (0, 0))],
        out_specs=(pl.BlockSpec((tm, W), lambda i: (i, 0)), pl.BlockSpec((W, W), lambda i: (0, 0))),
        compiler_params=_cparams())(ya, dyo, wglu)


PAD = 16


def _chunk_cumsums(x, pad_ref, L):
    row = lax.broadcasted_iota(jnp.int32, x.shape, 0) % CHUNK
    zeros = jnp.zeros((PAD, x.shape[1]), F32)
    pad_ref[0:PAD, :] = zeros
    pad_ref[PAD + L:2 * PAD + L, :] = zeros
    c = x
    r = x
    d = 1
    while d < CHUNK:
        pad_ref[PAD:PAD + L, :] = c
        c = c + jnp.where(row >= d, pad_ref[PAD - d:PAD - d + L, :], 0.0)
        pad_ref[PAD:PAD + L, :] = r
        r = r + jnp.where(row + d < CHUNK, pad_ref[PAD + d:PAD + d + L, :], 0.0)
        d *= 2
    return c, r - x


def _hgrn_prep(q_ref, fl_ref, lb_ref, pad_ref, r0, n):
    rows = pl.ds(r0, n)
    lb = lb_ref[...]
    sig = jax.nn.sigmoid(fl_ref[rows, :].astype(F32))
    f = lb + (1.0 - lb) * sig
    k = 1.0 - f
    c, rc = _chunk_cumsums(jnp.log(f), pad_ref, n)
    e_in, e_inv, e_out = jnp.exp(c), jnp.exp(-c), jnp.exp(rc)
    q = q_ref[rows, :].astype(F32)
    return dict(sig=sig, f=f, k=k, q=q, e_in=e_in, e_inv=e_inv, e_out=e_out, dec=jnp.exp(c + rc))


def _for_row_blocks(L, fn):
    full = L // GROUP
    if full:
        def step(g, carry):
            fn(pl.multiple_of(g * GROUP, GROUP), GROUP)
            return carry
        lax.fori_loop(0, full, step, 0)
    if L % GROUP:
        fn(full * GROUP, L % GROUP)


def _chunk_mask(rb):
    r = lax.broadcasted_iota(jnp.int32, (rb, rb), 0)
    c = lax.broadcasted_iota(jnp.int32, (rb, rb), 1)
    return (r // CHUNK == c // CHUNK) & (c <= r)


def _hg_out(o, og, g):
    on = o * lax.rsqrt(jnp.mean(o * o, axis=-1, keepdims=True) + EPS) * g
    return on * _silu(og)


def _hgrn_specs(L, hd, col_q, n_heads, order):
    def spec(sec):
        return pl.BlockSpec((None, L, hd), lambda *g: (order(*g)[0], 0, col_q + sec * n_heads + order(*g)[1]))
    return [spec(0), spec(1), spec(2), spec(3)]


GROUP = 128
CPG = GROUP // CHUNK


def _expand(x):
    xf = x.astype(F32)
    chunk = lax.broadcasted_iota(jnp.int32, xf.shape, 0) // CHUNK
    return jnp.concatenate([jnp.where(chunk == j, xf, 0.0) for j in range(CPG)], axis=1)


def _fill_tail(refs_fills, L):
    for ref, fill in refs_fills:
        if ref.shape[0] > L:
            ref[L:ref.shape[0], :] = jnp.full((ref.shape[0] - L, ref.shape[1]), fill, ref.dtype)


GROUP_UNROLL = 17


def _hgrn_forward_core(q_ref, fl_ref, v_ref, lb_ref, pad_ref, qin_ref, kin_ref, kout_ref, vp_ref, dec_ref, o_ref,
                       s_ref, a_ref, L, keep=()):
    hd = qin_ref.shape[1]
    n_groups = qin_ref.shape[0] // GROUP

    def prep(r0, n):
        pp = _hgrn_prep(q_ref, fl_ref, lb_ref, pad_ref, r0, n)
        rows = pl.ds(r0, n)
        for key, ref in keep:
            ref[rows, :] = pp[key]
        qin_ref[rows, :] = (pp["q"] * pp["e_in"]).astype(MXU)
        kin_ref[rows, :] = (pp["k"] * pp["e_inv"]).astype(MXU)
        kout_ref[rows, :] = (pp["k"] * pp["e_out"]).astype(MXU)
        vp_ref[rows, :] = v_ref[rows, :].astype(MXU)
        dec_ref[rows, :] = pp["dec"]

    _for_row_blocks(L, prep)
    _fill_tail(((qin_ref, 0.0), (kin_ref, 0.0), (kout_ref, 0.0), (vp_ref, 0.0), (dec_ref, 1.0)), L)
    mask = _chunk_mask(GROUP)

    def scores(g, carry):
        rows = pl.ds(pl.multiple_of(g * GROUP, GROUP), GROUP)
        a_ref[rows, :] = jnp.where(mask, _dot_nt(qin_ref[rows, :], kin_ref[rows, :]), 0.0).astype(MXU)
        return carry

    lax.fori_loop(0, n_groups, scores, 0, unroll=GROUP_UNROLL)

    def intra(g, carry):
        rows = pl.ds(pl.multiple_of(g * GROUP, GROUP), GROUP)
        o_ref[rows, :] = _dot(a_ref[rows, :], vp_ref[rows, :])
        kv = _dot_tn(vp_ref[rows, :], _expand(kout_ref[rows, :]))
        for j in range(CPG):
            s_ref[g * CPG + j] = kv[:, j * hd:(j + 1) * hd]
        return carry

    lax.fori_loop(0, n_groups, intra, 0, unroll=GROUP_UNROLL)

    def rec(n, st):
        kv = s_ref[n]
        s_ref[n] = st
        dec = dec_ref[pl.ds(pl.multiple_of(n * CHUNK, CHUNK), SUBLANES), :][0:1]
        return st * dec + kv

    _chain_loop(L // CHUNK, rec, jnp.zeros((hd, hd), F32))

    def inter(g, carry):
        rows = pl.ds(pl.multiple_of(g * GROUP, GROUP), GROUP)
        scat = jnp.concatenate([s_ref[g * CPG + j] for j in range(CPG)], axis=1)
        o_ref[rows, :] += _dot_nt(_expand(qin_ref[rows, :]), scat)
        return carry

    lax.fori_loop(0, n_groups, inter, 0, unroll=GROUP_UNROLL)


def _hgrn_scratch(L, hd):
    lp = -(-L // GROUP) * GROUP
    return lp, [pltpu.VMEM((GROUP + 2 * PAD, hd), F32), pltpu.VMEM((lp, hd), MXU), pltpu.VMEM((lp, hd), MXU),
                pltpu.VMEM((lp, hd), MXU), pltpu.VMEM((lp, hd), MXU), pltpu.VMEM((lp, hd), F32),
                pltpu.VMEM((lp, hd), F32), pltpu.VMEM((lp // CHUNK, hd, hd), F32), pltpu.VMEM((lp, GROUP), MXU)]


def _hgrn_fwd_call(p3, lb, ng, n_heads, col_q, name):
    B, L, _ = p3.shape
    hd = ng.shape[1]
    _, scratch = _hgrn_scratch(L, hd)

    def body(q_ref, fl_ref, v_ref, og_ref, lb_ref, ng_ref, yb_ref,
             pad_ref, qin_ref, kin_ref, kout_ref, vp_ref, dec_ref, o_ref, s_ref, a_ref):
        _hgrn_forward_core(q_ref, fl_ref, v_ref, lb_ref, pad_ref, qin_ref, kin_ref, kout_ref, vp_ref, dec_ref,
                           o_ref, s_ref, a_ref, L)

        def out(r0, n):
            rows = pl.ds(r0, n)
            yb_ref[rows, :] = _hg_out(o_ref[rows, :], og_ref[rows, :].astype(F32), ng_ref[...]).astype(ACT)

        _for_row_blocks(L, out)

    order = lambda b, h: (b, h)
    return pl.pallas_call(
        body, name=name, out_shape=jax.ShapeDtypeStruct((B, L, n_heads * hd), ACT), grid=(B, n_heads),
        in_specs=_hgrn_specs(L, hd, col_q, n_heads, order) + [
            pl.BlockSpec((1, hd), lambda b, h: (0, h)), pl.BlockSpec((1, hd), lambda b, h: (0, 0))],
        out_specs=pl.BlockSpec((None, L, hd), lambda b, h: (b, 0, h)),
        scratch_shapes=scratch, compiler_params=_cparams())(p3, p3, p3, p3, lb, ng)


def _hgrn_bwd_call(p3, dyb, lb, ng, n_heads, col_q, name):
    B, L, _ = p3.shape
    hd = ng.shape[1]
    n_chunks = L // CHUNK
    lp, scratch = _hgrn_scratch(L, hd)
    n_groups = lp // GROUP

    def body(q_ref, fl_ref, v_ref, og_ref, dyb_ref, lb_ref, ng_ref,
             dq_ref, dfl_ref, dv_ref, dog_ref, dlb_ref, dng_ref,
             pad_ref, qin_ref, kin_ref, kout_ref, vp_ref, dec_ref, o_ref, s_ref, a_ref,
             do_ref, ds_ref, dqi_ref, dki_ref, dko_ref, dvv_ref, dct_ref,
             sig_ref, f_ref, ein_ref, einv_ref, eout_ref, da_ref):
        @pl.when(pl.program_id(1) == 0)
        def _():
            dlb_ref[...] = jnp.zeros_like(dlb_ref)

        @pl.when((pl.program_id(0) == 0) & (pl.program_id(1) == 0))
        def _():
            dng_ref[...] = jnp.zeros_like(dng_ref)

        _hgrn_forward_core(q_ref, fl_ref, v_ref, lb_ref, pad_ref, qin_ref, kin_ref, kout_ref, vp_ref, dec_ref,
                           o_ref, s_ref, a_ref, L, keep=(("sig", sig_ref), ("f", f_ref), ("e_in", ein_ref),
                                                  ("e_inv", einv_ref), ("e_out", eout_ref)))

        def out_bwd(r0, n):
            rows = pl.ds(r0, n)
            _, out_vjp = jax.vjp(_hg_out, o_ref[rows, :], og_ref[rows, :].astype(F32), ng_ref[...])
            d_o, d_og, d_ng = out_vjp(dyb_ref[rows, :].astype(F32))
            dog_ref[rows, :] = d_og.astype(ACT)
            dng_ref[...] += d_ng
            do_ref[rows, :] = d_o.astype(MXU)

        _for_row_blocks(L, out_bwd)
        _fill_tail(((do_ref, 0.0),), L)
        mask = _chunk_mask(GROUP)

        def score_grads(g, carry):
            rows = pl.ds(pl.multiple_of(g * GROUP, GROUP), GROUP)
            da_ref[rows, :] = jnp.where(mask, _dot_nt(do_ref[rows, :], vp_ref[rows, :]), 0.0).astype(MXU)
            return carry

        lax.fori_loop(0, n_groups, score_grads, 0, unroll=GROUP_UNROLL)

        def grads_a(g, carry):
            rows = pl.ds(pl.multiple_of(g * GROUP, GROUP), GROUP)
            qi, ki, do, da = qin_ref[rows, :], kin_ref[rows, :], do_ref[rows, :], da_ref[rows, :]
            sstack = s_ref[pl.ds(g * CPG, CPG)].reshape(CPG * hd, hd)
            dqi_ref[rows, :] = _dot(da, ki) + _dot(_expand(do), sstack)
            dki_ref[rows, :] = _dot_tn(da, qi)
            dvv_ref[rows, :] = _dot_tn(a_ref[rows, :], do)
            x = _dot_tn(do, _expand(qi))
            for j in range(CPG):
                ds_ref[g * CPG + j] = x[:, j * hd:(j + 1) * hd]
            return carry

        lax.fori_loop(0, n_groups, grads_a, 0, unroll=GROUP_UNROLL)

        def rec_bwd(k, dst):
            n = n_chunks - 1 - k
            r0 = pl.multiple_of(n * CHUNK, CHUNK)
            x = ds_ref[n]
            ds_ref[n] = dst
            dec = dec_ref[pl.ds(r0, SUBLANES), :][0:1]
            return dst * dec + x

        _chain_loop(n_chunks, rec_bwd, jnp.zeros((hd, hd), F32))

        def grads_b(g, carry):
            r0 = pl.multiple_of(g * GROUP, GROUP)
            rows = pl.ds(r0, GROUP)
            ds = [ds_ref[g * CPG + j] for j in range(CPG)]
            dscat = jnp.concatenate(ds, axis=1)
            dvv_ref[rows, :] += _dot_nt(_expand(kout_ref[rows, :]), dscat)
            dstack = ds_ref[pl.ds(g * CPG, CPG)].reshape(CPG * hd, hd)
            dko_ref[rows, :] = _dot(_expand(vp_ref[rows, :]), dstack)
            for j in range(CPG):
                dec = dec_ref[pl.ds(r0 + j * CHUNK, SUBLANES), :][0:1]
                ddec = dec * jnp.sum(ds[j] * s_ref[g * CPG + j], axis=0, keepdims=True)
                dct_ref[pl.ds(r0 + j * CHUNK, CHUNK), :] = jnp.broadcast_to(ddec, (CHUNK, hd))
            return carry

        lax.fori_loop(0, n_groups, grads_b, 0, unroll=GROUP_UNROLL)

        def finish(r0, n):
            rows = pl.ds(r0, n)
            sig, f, e_in, e_inv, e_out = [r[rows, :] for r in (sig_ref, f_ref, ein_ref, einv_ref, eout_ref)]
            q, k = q_ref[rows, :].astype(F32), 1.0 - f
            dqi, dki, dko = dqi_ref[rows, :], dki_ref[rows, :], dko_ref[rows, :]
            dq = dqi * e_in
            dk = dki * e_inv + dko * e_out
            dq_ref[rows, :] = dq.astype(ACT)
            dv_ref[rows, :] = dvv_ref[rows, :].astype(ACT)
            t_out = k * e_out * dko
            dc = q * dq - k * e_inv * dki - t_out
            _, dc_later = _chunk_cumsums(dc, pad_ref, n)
            t_incl, t_later = _chunk_cumsums(t_out, pad_ref, n)
            dlogf = dc + dc_later + t_incl + t_later + dct_ref[rows, :]
            df = dlogf / f - dk
            dfl_ref[rows, :] = (df * (1.0 - lb_ref[...]) * sig * (1.0 - sig)).astype(ACT)
            dlb_ref[...] += jnp.sum(df * (1.0 - sig), axis=0, keepdims=True)

        _for_row_blocks(L, finish)

    order = lambda h, b: (b, h)
    W = n_heads * hd
    act_out = jax.ShapeDtypeStruct((B, L, W), ACT)
    blk_out = pl.BlockSpec((None, L, hd), lambda h, b: (b, 0, h))
    return pl.pallas_call(
        body, name=name,
        out_shape=(act_out, act_out, act_out, act_out, jax.ShapeDtypeStruct((1, W), F32),
                   jax.ShapeDtypeStruct((1, hd), F32)),
        grid=(n_heads, B),
        in_specs=_hgrn_specs(L, hd, col_q, n_heads, order) + [
            pl.BlockSpec((None, L, hd), lambda h, b: (b, 0, h)),
            pl.BlockSpec((1, hd), lambda h, b: (0, h)), pl.BlockSpec((1, hd), lambda h, b: (0, 0))],
        out_specs=(blk_out, blk_out, blk_out, blk_out, pl.BlockSpec((1, hd), lambda h, b: (0, h)),
                   pl.BlockSpec((1, hd), lambda h, b: (0, 0))),
        scratch_shapes=scratch + [
            pltpu.VMEM((lp, hd), MXU), pltpu.VMEM((lp // CHUNK, hd, hd), F32)] + [pltpu.VMEM((lp, hd), F32)] * 10 + [
            pltpu.VMEM((lp, GROUP), MXU)],
        compiler_params=_cparams())(p3, p3, p3, p3, dyb, lb, ng)


def _merge_fn(a, bm, ga, gb):
    return jax.nn.sigmoid(ga) * a + jax.nn.sigmoid(gb) * bm


def _merge_call(yb, a, p, h0, whp, wout, g2, col_ga, tm, name):
    T, D = h0.shape

    def body(yb_ref, a_ref, ga_ref, gb_ref, h0_ref, whp_ref, wout_ref, g2_ref, h1_ref, mg_ref, bm_ref, z2_ref):
        bm = _dot(yb_ref[...], whp_ref[...])
        mg = _merge_fn(a_ref[...].astype(F32), bm, ga_ref[...].astype(F32), gb_ref[...].astype(F32))
        h1 = h0_ref[...] + _dot(mg, wout_ref[...])
        h1_ref[...] = h1
        mg_ref[...] = mg.astype(ACT)
        bm_ref[...] = bm.astype(ACT)
        z2_ref[...] = _rms(h1, g2_ref[...]).astype(ACT)

    tile = pl.BlockSpec((tm, D), lambda i: (i, 0))
    full = pl.BlockSpec((D, D), lambda i: (0, 0))
    act = jax.ShapeDtypeStruct((T, D), ACT)
    return pl.pallas_call(
        body, name=name, out_shape=(jax.ShapeDtypeStruct((T, D), F32), act, act, act), grid=(T // tm,),
        in_specs=[tile, tile, pl.BlockSpec((tm, D), lambda i: (i, col_ga)),
                  pl.BlockSpec((tm, D), lambda i: (i, col_ga + 1)), tile, full, full,
                  pl.BlockSpec((1, D), lambda i: (0, 0))],
        out_specs=(tile, tile, tile, tile), compiler_params=_cparams())(yb, a, p, p, h0, whp, wout, g2)


def _merge_bwd_call(dmg, a, bm, p, col_ga, tm, name):
    T, D = dmg.shape

    def body(dmg_ref, a_ref, bm_ref, ga_ref, gb_ref, da_ref, dbm_ref, dga_ref, dgb_ref):
        args = [r[...].astype(F32) for r in (a_ref, bm_ref, ga_ref, gb_ref)]
        _, vjp = jax.vjp(_merge_fn, *args)
        for r, o in zip((da_ref, dbm_ref, dga_ref, dgb_ref), vjp(dmg_ref[...].astype(F32))):
            r[...] = o.astype(ACT)

    tile = pl.BlockSpec((tm, D), lambda i: (i, 0))
    act = jax.ShapeDtypeStruct((T, D), ACT)
    return pl.pallas_call(
        body, name=name, out_shape=(act, act, act, act), grid=(T // tm,),
        in_specs=[tile, tile, tile, pl.BlockSpec((tm, D), lambda i: (i, col_ga)),
                  pl.BlockSpec((tm, D), lambda i: (i, col_ga + 1))],
        out_specs=(tile, tile, tile, tile), compiler_params=_cparams())(dmg, a, bm, p, p)


def _conv_taps(x_ref, halo_ref, ext_ref, edge, tm, before):
    halo = jnp.where(edge, 0.0, halo_ref[...].astype(F32))
    x = x_ref[...].astype(F32)
    if before:
        ext_ref[0:PAD, :] = halo
        ext_ref[PAD:PAD + tm, :] = x
        return [ext_ref[PAD - 2 + k:PAD - 2 + k + tm, :] for k in range(3)]
    ext_ref[0:tm, :] = x
    ext_ref[tm:tm + PAD, :] = halo
    return [ext_ref[k:k + tm, :] for k in range(3)]


def _conv(taps, cw, cb):
    return cb + cw[0:1] * taps[0] + cw[1:2] * taps[1] + cw[2:3] * taps[2]


def _ffn_pair_specs(tm, F, T, n_pairs, order, before):
    hb = tm // PAD
    last = T // PAD - 1

    def halo_row(i):
        return jnp.maximum(i * hb - 1, 0) if before else jnp.minimum((i + 1) * hb, last)

    specs = []
    for off in (0, n_pairs):
        specs.append(pl.BlockSpec((None, tm, F), lambda *g, off=off: (order(*g)[1] + off, order(*g)[0], 0)))
        specs.append(pl.BlockSpec((None, PAD, F), lambda *g, off=off: (order(*g)[1] + off, halo_row(order(*g)[0]), 0)))
    return specs


def _ffn_fwd_call(up, cw, cb, wd, h1, tgt, g3, tm, tps, name):
    S, T, F = up.shape
    n_pairs = S // 2
    D = h1.shape[1]

    def body(ua_ref, ha_ref, ub_ref, hb_ref, cwa_ref, cwb_ref, cba_ref, cbb_ref, wd_ref, h1_ref, tgt_ref, g3_ref,
             ca_ref, cb_ref, dh2_ref, loss_ref, dg3_ref, acc_ref, ext_ref):
        i, j = pl.program_id(0), pl.program_id(1)
        edge = (i % tps) == 0
        ua = _conv(_conv_taps(ua_ref, ha_ref, ext_ref, edge, tm, True), cwa_ref[...], cba_ref[...])
        ub = _conv(_conv_taps(ub_ref, hb_ref, ext_ref, edge, tm, True), cwb_ref[...], cbb_ref[...])
        ca_ref[...] = ua.astype(ACT)
        cb_ref[...] = ub.astype(ACT)
        contrib = _dot(_silu(ua) * ub, wd_ref[...])

        @pl.when(j == 0)
        def _():
            acc_ref[...] = h1_ref[...] + contrib

        @pl.when(j > 0)
        def _():
            acc_ref[...] += contrib

        @pl.when((i == 0) & (j == 0))
        def _():
            loss_ref[...] = jnp.zeros_like(loss_ref)
            dg3_ref[...] = jnp.zeros_like(dg3_ref)

        @pl.when(j == n_pairs - 1)
        def _():
            row = lax.broadcasted_iota(jnp.int32, (tm, 1), 0) + (i % tps) * tm
            valid = row >= N_META
            tgt = tgt_ref[...]

            def loss_fn(h2, g):
                err = _rms(h2, g) - tgt
                return 0.5 * jnp.sum(jnp.where(valid, err * err, 0.0)) / D

            loss, vjp = jax.vjp(loss_fn, acc_ref[...], g3_ref[...])
            dh2, dg3 = vjp(jnp.ones((), F32))
            dh2_ref[...] = dh2
            loss_ref[...] += loss
            dg3_ref[...] += dg3

    order = lambda i, j: (i, j)
    tile = pl.BlockSpec((tm, D), lambda i, j: (i, 0))
    vec = pl.BlockSpec((1, D), lambda i, j: (0, 0))
    return pl.pallas_call(
        body, name=name,
        out_shape=(jax.ShapeDtypeStruct((n_pairs, T, F), ACT), jax.ShapeDtypeStruct((n_pairs, T, F), ACT),
                   jax.ShapeDtypeStruct((T, D), F32), jax.ShapeDtypeStruct((1, LANES), F32),
                   jax.ShapeDtypeStruct((1, D), F32)),
        grid=(T // tm, n_pairs),
        in_specs=_ffn_pair_specs(tm, F, T, n_pairs, order, True) + [
            pl.BlockSpec((None, 3, F), lambda i, j: (j, 0, 0)), pl.BlockSpec((None, 3, F), lambda i, j: (j + n_pairs, 0, 0)),
            pl.BlockSpec((None, 1, F), lambda i, j: (j, 0, 0)), pl.BlockSpec((None, 1, F), lambda i, j: (j + n_pairs, 0, 0)),
            pl.BlockSpec((None, F, D), lambda i, j: (j, 0, 0)), tile, tile, vec],
        out_specs=(pl.BlockSpec((None, tm, F), lambda i, j: (j, i, 0)), pl.BlockSpec((None, tm, F), lambda i, j: (j, i, 0)),
                   tile, pl.BlockSpec((1, LANES), lambda i, j: (0, 0)), vec),
        scratch_shapes=[pltpu.VMEM((tm, D), F32), pltpu.VMEM((tm + PAD, F), F32)],
        compiler_params=_cparams())(up, up, up, up, cw, cw, cb, cb, wd, h1, tgt, g3)


def _ffn_bwd_a_call(dh2, ca, cb, wd, tm, name):
    n_pairs, T, F = ca.shape
    D = dh2.shape[1]

    def body(dh2_ref, ca_ref, cb_ref, wd_ref, dua_ref, dub_ref, dwd_ref, dcba_ref, dcbb_ref):
        @pl.when(pl.program_id(1) == 0)
        def _():
            for r in (dwd_ref, dcba_ref, dcbb_ref):
                r[...] = jnp.zeros_like(r)

        dh2 = dh2_ref[...]
        ua, ub = ca_ref[...].astype(F32), cb_ref[...].astype(F32)
        sa = jax.nn.sigmoid(ua)
        gate = ua * sa
        dact = _dot_nt(dh2, wd_ref[...])
        dwd_ref[...] += _dot_tn(gate * ub, dh2)
        dub = dact * gate
        dua = dact * ub * sa * (1.0 + ua * (1.0 - sa))
        dcba_ref[...] += jnp.sum(dua, axis=0, keepdims=True)
        dcbb_ref[...] += jnp.sum(dub, axis=0, keepdims=True)
        dua_ref[...] = dua.astype(ACT)
        dub_ref[...] = dub.astype(ACT)

    blk = pl.BlockSpec((None, tm, F), lambda j, i: (j, i, 0))
    vec = pl.BlockSpec((None, 1, F), lambda j, i: (j, 0, 0))
    return pl.pallas_call(
        body, name=name,
        out_shape=(jax.ShapeDtypeStruct((n_pairs, T, F), ACT), jax.ShapeDtypeStruct((n_pairs, T, F), ACT),
                   jax.ShapeDtypeStruct((n_pairs, F, D), F32), jax.ShapeDtypeStruct((n_pairs, 1, F), F32),
                   jax.ShapeDtypeStruct((n_pairs, 1, F), F32)),
        grid=(n_pairs, T // tm),
        in_specs=[pl.BlockSpec((tm, D), lambda j, i: (i, 0)), blk, blk, pl.BlockSpec((None, F, D), lambda j, i: (j, 0, 0))],
        out_specs=(blk, blk, pl.BlockSpec((None, F, D), lambda j, i: (j, 0, 0)), vec, vec),
        compiler_params=_cparams())(dh2, ca, cb, wd)


def _ffn_bwd_b_call(dua, dub, up, cw, wup, h1, g2, dh2, tm, tps, name):
    n_pairs, T, F = dua.shape
    D = h1.shape[1]
    hb = tm // PAD
    last = T // PAD - 1

    def body(da_ref, na_ref, db_ref, nb_ref, ua_ref, ub_ref, cwa_ref, cwb_ref, wa_ref, wb_ref, h1_ref, g2_ref, dh2_ref,
             dupa_ref, dupb_ref, dh1_ref, dg2_ref, dcwa_ref, dcwb_ref, acc_ref, ext_ref):
        i, j = pl.program_id(0), pl.program_id(1)
        edge = (i % tps) == tps - 1

        @pl.when((i == 0) & (j == 0))
        def _():
            dcwa_ref[...] = jnp.zeros_like(dcwa_ref)
            dcwb_ref[...] = jnp.zeros_like(dcwb_ref)

        outs = []
        for d_ref, n_ref, u_ref, cw_ref, o_ref, dcw_ref in (
                (da_ref, na_ref, ua_ref, cwa_ref, dupa_ref, dcwa_ref),
                (db_ref, nb_ref, ub_ref, cwb_ref, dupb_ref, dcwb_ref)):
            t = _conv_taps(d_ref, n_ref, ext_ref, edge, tm, False)
            cwv = cw_ref[...]
            dup = cwv[2:3] * t[0] + cwv[1:2] * t[1] + cwv[0:1] * t[2]
            o_ref[...] = dup.astype(ACT)
            outs.append(dup)
            u = u_ref[...].astype(F32)
            dcw_ref[j] += jnp.concatenate([jnp.sum(u * t[2 - k], axis=0, keepdims=True) for k in range(3)], axis=0)
        contrib = _dot_nt(outs[0], wa_ref[...]) + _dot_nt(outs[1], wb_ref[...])

        @pl.when(j == 0)
        def _():
            acc_ref[...] = contrib

        @pl.when(j > 0)
        def _():
            acc_ref[...] += contrib

        @pl.when((i == 0) & (j == 0))
        def _():
            dg2_ref[...] = jnp.zeros_like(dg2_ref)

        @pl.when(j == n_pairs - 1)
        def _():
            _, vjp = jax.vjp(_rms, h1_ref[...], g2_ref[...])
            dh, dg = vjp(acc_ref[...])
            dh1_ref[...] = dh2_ref[...] + dh
            dg2_ref[...] += dg

    tile = pl.BlockSpec((tm, D), lambda i, j: (i, 0))
    vec = pl.BlockSpec((1, D), lambda i, j: (0, 0))
    pair = lambda: [pl.BlockSpec((None, tm, F), lambda i, j: (j, i, 0)),
                    pl.BlockSpec((None, PAD, F), lambda i, j: (j, jnp.minimum((i + 1) * hb, last), 0))]
    act = jax.ShapeDtypeStruct((n_pairs, T, F), ACT)
    dcw = jax.ShapeDtypeStruct((n_pairs, 3, F), F32)
    dcw_spec = pl.BlockSpec((n_pairs, 3, F), lambda i, j: (0, 0, 0))
    return pl.pallas_call(
        body, name=name,
        out_shape=(act, act, jax.ShapeDtypeStruct((T, D), F32), jax.ShapeDtypeStruct((1, D), F32), dcw, dcw),
        grid=(T // tm, n_pairs),
        in_specs=pair() + pair() + [
            pl.BlockSpec((None, tm, F), lambda i, j: (j, i, 0)), pl.BlockSpec((None, tm, F), lambda i, j: (j + n_pairs, i, 0)),
            pl.BlockSpec((None, 3, F), lambda i, j: (j, 0, 0)), pl.BlockSpec((None, 3, F), lambda i, j: (j + n_pairs, 0, 0)),
            pl.BlockSpec((None, D, F), lambda i, j: (j, 0, 0)), pl.BlockSpec((None, D, F), lambda i, j: (j + n_pairs, 0, 0)),
            tile, vec, tile],
        out_specs=(pl.BlockSpec((None, tm, F), lambda i, j: (j, i, 0)), pl.BlockSpec((None, tm, F), lambda i, j: (j, i, 0)),
                   tile, vec, dcw_spec, dcw_spec),
        scratch_shapes=[pltpu.VMEM((tm, D), F32), pltpu.VMEM((tm + PAD, F), F32)],
        compiler_params=_cparams())(dua, dua, dub, dub, up, up, cw, cw, wup, wup, h1, g2, dh2)


def _in_bwd_call(dp, w_in, h0, g1, dh1, tm, name):
    T, D = h0.shape
    S, _, N = w_in.shape

    def body(dp_ref, w_ref, h0_ref, g1_ref, dh1_ref, dh0_ref, dg1_ref, acc_ref):
        i, j = pl.program_id(0), pl.program_id(1)
        contrib = _dot_nt(dp_ref[...], w_ref[...])

        @pl.when(j == 0)
        def _():
            acc_ref[...] = contrib

        @pl.when(j > 0)
        def _():
            acc_ref[...] += contrib

        @pl.when((i == 0) & (j == 0))
        def _():
            dg1_ref[...] = jnp.zeros_like(dg1_ref)

        @pl.when(j == S - 1)
        def _():
            _, vjp = jax.vjp(_rms, h0_ref[...], g1_ref[...])
            dh, dg = vjp(acc_ref[...])
            dh0_ref[...] = dh1_ref[...] + dh
            dg1_ref[...] += dg

    tile = pl.BlockSpec((tm, D), lambda i, j: (i, 0))
    vec = pl.BlockSpec((1, D), lambda i, j: (0, 0))
    return pl.pallas_call(
        body, name=name, out_shape=(jax.ShapeDtypeStruct((T, D), F32), jax.ShapeDtypeStruct((1, D), F32)),
        grid=(T // tm, S),
        in_specs=[pl.BlockSpec((tm, N), lambda i, j: (i, j)), pl.BlockSpec((None, D, N), lambda i, j: (j, 0, 0)),
                  tile, vec, tile],
        out_specs=(tile, vec), scratch_shapes=[pltpu.VMEM((tm, D), F32)],
        compiler_params=_cparams())(dp, w_in, h0, g1, dh1)


def _meta_grad_call(dh0_3, name):
    B, L, D = dh0_3.shape

    def body(d_ref, o_ref):
        o_ref[...] = jnp.sum(d_ref[...], axis=0)

    return pl.pallas_call(
        body, name=name, out_shape=jax.ShapeDtypeStruct((N_META, D), F32), grid=(1,),
        in_specs=[pl.BlockSpec((B, N_META, D), lambda i: (0, 0, 0))],
        out_specs=pl.BlockSpec((N_META, D), lambda i: (0, 0)), compiler_params=_cparams())(dh0_3)


_RELS = [(dx, dy, dc) for dx in (0, 1) for dy in (0, 1) for dc in (0, 1)][1:]


def _exchange_call(arrs, scatter, name):
    n = len(arrs)
    n_rel = len(_RELS)

    def body(*refs):
        ins, outs = refs[:n], refs[n:2 * n]
        send_sems, recv_sems, loc_sems = refs[2 * n:]
        x, y, c = lax.axis_index("x"), lax.axis_index("y"), lax.axis_index("c")
        me = 4 * x + 2 * y + c
        started = []
        for k in range(n):
            src_me = ins[k].at[me] if scatter else ins[k]
            loc = pltpu.make_async_copy(src_me, outs[k].at[me], loc_sems.at[k])
            loc.start()
            started.append(loc)
        waits = []
        for r, (dx, dy, dc) in enumerate(_RELS):
            px, py, pc = (x + dx) % 2, (y + dy) % 2, (c + dc) % 2
            pid = 4 * px + 2 * py + pc
            for k in range(n):
                s = k * n_rel + r
                src = ins[k].at[pid] if scatter else ins[k]
                cp = pltpu.make_async_remote_copy(
                    src_ref=src, dst_ref=outs[k].at[me], send_sem=send_sems.at[s], recv_sem=recv_sems.at[s],
                    device_id=(px, py, pc), device_id_type=pl.DeviceIdType.MESH)
                cp.start()
                waits.append(pltpu.make_async_remote_copy(
                    src_ref=src, dst_ref=outs[k].at[pid], send_sem=send_sems.at[s], recv_sem=recv_sems.at[s],
                    device_id=(px, py, pc), device_id_type=pl.DeviceIdType.MESH))
        for w in waits:
            w.wait_send()
            w.wait_recv()
        for loc in started:
            loc.wait()

    out_shape = tuple(jax.ShapeDtypeStruct(a.shape if scatter else (N_DEV,) + a.shape, a.dtype) for a in arrs)
    hbm = pl.BlockSpec(memory_space=pl.ANY)
    return pl.pallas_call(
        body, name=name, out_shape=out_shape, in_specs=[hbm] * n, out_specs=tuple([hbm] * n),
        scratch_shapes=[pltpu.SemaphoreType.DMA((n * n_rel,)), pltpu.SemaphoreType.DMA((n * n_rel,)),
                        pltpu.SemaphoreType.DMA((n,))],
        compiler_params=pltpu.CompilerParams(has_side_effects=True))(*arrs)


_HBM = pl.BlockSpec(memory_space=pltpu.HBM)
_SEM = pl.BlockSpec(memory_space=pltpu.SEMAPHORE)
_DATAFLOW = pltpu.SideEffectType.DATAFLOW_SIDE_EFFECTING


def _peer_copies(ins, lands, send_sems, recv_sems, scatter):
    n = len(ins)
    x, y, c = lax.axis_index("x"), lax.axis_index("y"), lax.axis_index("c")
    me = 4 * x + 2 * y + c
    sends, arrivals = [], []
    for r, (dx, dy, dc) in enumerate(_RELS):
        px, py, pc = (x + dx) % 2, (y + dy) % 2, (c + dc) % 2
        pid = 4 * px + 2 * py + pc
        for k in range(n):
            s = k * len(_RELS) + r
            src = ins[k].at[pid] if scatter else ins[k]
            for dst, out in ((lands[k].at[me], sends), (lands[k].at[pid], arrivals)):
                out.append(pltpu.make_async_remote_copy(
                    src_ref=src, dst_ref=dst, send_sem=send_sems.at[s], recv_sem=recv_sems.at[s],
                    device_id=(px, py, pc), device_id_type=pl.DeviceIdType.MESH))
    return sends, arrivals


def _exchange_start(arrs, scatter, name):
    n = len(arrs)
    n_sem = n * len(_RELS)

    def body(*refs):
        ins, lands = refs[:n], refs[n:2 * n]
        send_sems, recv_sems = refs[2 * n], refs[2 * n + 1]
        token = refs[-1]
        sends, _ = _peer_copies(ins, lands, send_sems, recv_sems, scatter)
        for cp in sends:
            cp.start()
        token[...] = jnp.zeros_like(token)

    land_shapes = [a.shape if scatter else (N_DEV,) + a.shape for a in arrs]
    ops = [pltpu.with_memory_space_constraint(a, pltpu.HBM) for a in arrs]
    ops += [pltpu.with_memory_space_constraint(lax.empty(s, a.dtype), pltpu.HBM) for s, a in zip(land_shapes, arrs)]
    out = pl.pallas_call(
        body, name=name,
        out_shape=(pltpu.SemaphoreType.DMA((n_sem,)), pltpu.SemaphoreType.DMA((n_sem,)),
                   *[pltpu.HBM(a.shape, a.dtype) for a in arrs],
                   *[pltpu.HBM(s, a.dtype) for s, a in zip(land_shapes, arrs)],
                   jax.ShapeDtypeStruct((SUBLANES, LANES), F32)),
        in_specs=[_HBM] * (2 * n),
        out_specs=(_SEM, _SEM, *[_HBM] * (2 * n), pl.BlockSpec(memory_space=pltpu.VMEM)),
        input_output_aliases={i: 2 + i for i in range(2 * n)},
        compiler_params=pltpu.CompilerParams(has_side_effects=_DATAFLOW))(*ops)
    return out[0], out[1], list(out[2:2 + n]), list(out[2 + n:2 + 2 * n]), out[-1]


def _exchange_wait(started, after, scatter, name):
    send_sems, recv_sems, srcs, lands, _ = started
    n = len(srcs)

    def body(*refs):
        ins, lands_ = refs[:n], refs[n:2 * n]
        _, arrivals = _peer_copies(ins, lands_, refs[2 * n], refs[2 * n + 1], scatter)
        for cp in arrivals:
            cp.wait_send()
            cp.wait_recv()

    out = pl.pallas_call(
        body, name=name,
        out_shape=(*[pltpu.HBM(a.shape, a.dtype) for a in srcs], *[pltpu.HBM(a.shape, a.dtype) for a in lands]),
        in_specs=[_HBM] * (2 * n) + [_SEM, _SEM, pl.BlockSpec(memory_space=pl.ANY)],
        out_specs=tuple([_HBM] * (2 * n)), input_output_aliases={i: i for i in range(2 * n)},
        compiler_params=pltpu.CompilerParams(has_side_effects=_DATAFLOW))(*srcs, *lands, send_sems, recv_sems, after)
    return list(out[:n]), list(out[n:])


def _place_own_call(srcs, lands, scatter, me, name):
    outs = []
    for k, (src, land) in enumerate(zip(srcs, lands)):
        R, C = land.shape[1:]
        tr = R
        while tr % 32 == 0 and tr * C * land.dtype.itemsize > 2 * 1024 * 1024:
            tr //= 2

        def body(me_ref, s_ref, l_ref, o_ref):
            o_ref[...] = s_ref[...]

        src_spec = (pl.BlockSpec((None, tr, C), lambda i, me_ref: (me_ref[0], i, 0)) if scatter
                    else pl.BlockSpec((tr, C), lambda i, me_ref: (i, 0)))
        outs.append(pl.pallas_call(
            body, name=f"{name}_{k}", out_shape=jax.ShapeDtypeStruct(land.shape, land.dtype),
            grid_spec=pltpu.PrefetchScalarGridSpec(
                num_scalar_prefetch=1, grid=(R // tr,),
                in_specs=[src_spec, pl.BlockSpec(memory_space=pl.ANY)],
                out_specs=pl.BlockSpec((None, tr, C), lambda i, me_ref: (me_ref[0], i, 0))),
            input_output_aliases={2: 0}, compiler_params=_cparams())(me, src, land))
    return outs


def _adamw_shard_call(w, parts, m, v, name):
    R, C = w.shape
    tr = _tile(R, 128) if R % 16 == 0 else R

    def body(w_ref, p_ref, m_ref, v_ref, g_ref, d_ref, nm_ref, nv_ref):
        g = p_ref[0].astype(F32)
        for s in range(1, N_DEV):
            g = g + p_ref[s].astype(F32)
        d, nm, nv = _adamw(w_ref[...], g, m_ref[...], v_ref[...])
        g_ref[...] = g
        d_ref[...] = d
        nm_ref[...] = nm
        nv_ref[...] = nv

    tile = pl.BlockSpec((tr, C), lambda i: (i, 0))
    sh = jax.ShapeDtypeStruct((R, C), F32)
    return pl.pallas_call(
        body, name=name, out_shape=(sh, sh, sh, sh), grid=(R // tr,),
        in_specs=[tile, pl.BlockSpec((N_DEV, tr, C), lambda i: (0, i, 0)), tile, tile],
        out_specs=(tile, tile, tile, tile), compiler_params=_cparams())(w, parts, m, v)


def _pack(arrs, rows_mult=SUBLANES):
    flat = jnp.concatenate([a.reshape(-1).astype(F32) for a in arrs])
    n = flat.shape[0]
    per = rows_mult * LANES
    total = -(-n // per) * per
    return jnp.pad(flat, (0, total - n)).reshape(total // LANES, LANES)


def _unpack(pack, shapes):
    flat = pack.reshape(-1)
    out, off = [], 0
    for s in shapes:
        n = 1
        for d in s:
            n *= d
        out.append(flat[off:off + n].reshape(s))
        off += n
    return out


def kernel(x, meta_tokens, mix_norm_g, w_in, ssm_lambda_re, ssm_lambda_im, ssm_log_dt, ssm_b_re, ssm_b_im, ssm_c_re, ssm_c_im, ssm_d, ssm_w_glu, w_ssm_proj, hgrn_lb_logits, hgrn_norm_g, w_hgrn_proj, w_out, ffn_norm_g, w_up, conv_w, conv_b, w_down, final_norm_g, loss_target, m_meta_tokens, m_mix_norm_g, m_w_in, m_ssm_lambda_re, m_ssm_lambda_im, m_ssm_log_dt, m_ssm_b_re, m_ssm_b_im, m_ssm_c_re, m_ssm_c_im, m_ssm_d, m_ssm_w_glu, m_w_ssm_proj, m_hgrn_lb_logits, m_hgrn_norm_g, m_w_hgrn_proj, m_w_out, m_ffn_norm_g, m_w_up, m_conv_w, m_conv_b, m_w_down, m_final_norm_g, v_meta_tokens, v_mix_norm_g, v_w_in, v_ssm_lambda_re, v_ssm_lambda_im, v_ssm_log_dt, v_ssm_b_re, v_ssm_b_im, v_ssm_c_re, v_ssm_c_im, v_ssm_d, v_ssm_w_glu, v_w_ssm_proj, v_hgrn_lb_logits, v_hgrn_norm_g, v_w_hgrn_proj, v_w_out, v_ffn_norm_g, v_w_up, v_conv_w, v_conv_b, v_w_down, v_final_norm_g):
    args = dict(locals())
    B, S_len, D = x.shape
    L = S_len + N_META
    T = B * L
    tm = _tile(L, ROW_TILE_CAP)
    tps = L // tm
    G, P = ssm_lambda_re.shape[1:]
    H = ssm_b_re.shape[-1]
    W = G * H
    n_cb = W // LANES
    gpb = G // n_cb
    hd = hgrn_norm_g.shape[1]
    n_heads = D // hd
    n_in = w_in.shape[2]
    F = w_up.shape[2]
    assert W == D and n_in % LANES == 0

    me = (4 * lax.axis_index("x") + 2 * lax.axis_index("y") + lax.axis_index("c")).astype(jnp.int32).reshape(1)
    meta_g, cw_g = _exchange_call([meta_tokens, conv_w[0]], False, "gather_small_params")
    ga = _exchange_start([w_in[0].astype(MXU)], False, "gather_a_start")
    gb = _exchange_start(
        [w_up[0].astype(MXU), ssm_w_glu[0].astype(MXU), w_ssm_proj[0].astype(MXU), w_hgrn_proj[0].astype(MXU),
         w_out[0].astype(MXU), w_down[0].astype(MXU)], False, "gather_b_start")
    started_tok = (ga[4] + gb[4])[0:1, 0:1]
    meta_full = meta_g.transpose(1, 0, 2).reshape(N_META, D)
    cb_g = conv_b.reshape(N_DEV, 1, F)

    h0 = jnp.concatenate([jnp.broadcast_to(meta_full[None], (B, N_META, D)), x], axis=1).reshape(T, D)
    tgt = jnp.concatenate([jnp.zeros((B, N_META, D), F32), loss_target], axis=1).reshape(T, D)

    lr, li = ssm_lambda_re[0], ssm_lambda_im[0]
    ldt = ssm_log_dt[0].reshape(G, 1)
    bt_re = ssm_b_re[0].transpose(2, 0, 1).reshape(H, G * P)
    bt_im = ssm_b_im[0].transpose(2, 0, 1).reshape(H, G * P)
    seg = _seg_len(L)
    a_re, a_im, as_re, as_im, coef_re, coef_im = _small_call(
        _disc_a_power(seg), [lr, li, ldt], [((G, P), F32)] * 6, "s5_discretise")
    bbt_re, bbt_im = _small_call(
        _disc_b, [coef_re.reshape(1, G * P), coef_im.reshape(1, G * P), bt_re, bt_im],
        [((H, G * P), F32)] * 2, "s5_input_matrix")
    eye = jnp.eye(gpb, dtype=F32)
    hw = gpb * P

    def expand_b(bbt):
        t = bbt.reshape(H, n_cb, gpb, P).transpose(1, 0, 2, 3)[:, None]
        return (t * eye[None, :, None, :, None]).reshape(n_cb, gpb * H, hw)

    def expand_c(cm):
        t = cm.reshape(n_cb, gpb, H, P).transpose(0, 1, 3, 2)[:, :, :, None]
        return (t * eye[None, :, None, :, None]).reshape(n_cb, hw, gpb * H)

    wb = jnp.concatenate([expand_b(bbt_re), expand_b(bbt_im)], axis=2).astype(MXU)
    wc = jnp.concatenate([expand_c(ssm_c_re[0]), -expand_c(ssm_c_im[0])], axis=1).astype(MXU)
    tab = jnp.stack([jnp.concatenate([a_re.reshape(n_cb, hw), a_im.reshape(n_cb, hw)], axis=1),
                     jnp.concatenate([as_re.reshape(n_cb, hw), as_im.reshape(n_cb, hw)], axis=1)], axis=1)
    tab = jnp.broadcast_to(tab[:, :, None, :], (n_cb, 2, SUBLANES, 2 * hw))
    dsk = ssm_d.reshape(n_cb, 1, LANES)
    lb = _small_call(_lb_fn, [hgrn_lb_logits], [((1, D), F32)], "hgrn_lower_bound")[0]

    z1 = _norm_call(h0, mix_norm_g + started_tok, tm, "mix_norm")
    ready = jnp.concatenate([t[(0,) * (t.ndim - 1)][0:1].astype(F32) for t in (z1, wb, wc, tab, lb, tgt)])
    ga_src, ga_land = _exchange_wait(ga, ready, False, "gather_a_wait")
    win_g = _place_own_call(ga_src, ga_land, False, me, "gather_a_own")[0]
    p = _mm_shard(z1, win_g, tm, "in_proj", False)
    p3 = p.reshape(B, L, p.shape[1])
    u_seg = _to_segments(p3[:, :, :W], seg)
    ya_seg, s_all = _s5_fwd_call(u_seg, wb, wc, tab, dsk, "s5_fwd")
    ya = _from_segments(ya_seg, seg, L).reshape(T, W)
    gb_src, gb_land = _exchange_wait(gb, ya, False, "gather_b_wait")
    gathered = _place_own_call(gb_src, gb_land, False, me, "gather_b_own")
    wup_g = gathered[0]
    wglu_g, wsp_g, whp_g, wout_g = [g.reshape(D, D) for g in gathered[1:5]]
    wdn_g = gathered[5].reshape(N_DEV // 2, 2 * w_down.shape[1], D)
    yo, a_br = _glu_proj_call(ya, wglu_g, wsp_g, tm, "s5_glu_proj")
    yb = _hgrn_fwd_call(p3, lb, hgrn_norm_g, n_heads, n_cb, "hgrn_fwd").reshape(T, D)
    col_ga = 5
    h1, mg, bm, z2 = _merge_call(yb, a_br, p, h0, whp_g, wout_g, ffn_norm_g, col_ga, tm, "merge")
    up = _mm_shard(z2, wup_g, tm, "up_proj", True)
    conv_a, conv_b_out, dh2, loss_part, dg3 = _ffn_fwd_call(up, cw_g, cb_g, wdn_g, h1, tgt, final_norm_g.reshape(1, D),
                                                            tm, tps, "ffn_out_loss")

    dua, dub, dwd, dcba, dcbb = _ffn_bwd_a_call(dh2, conv_a, conv_b_out, wdn_g, tm, "ffn_bwd_gate")
    dupa, dupb, dh1, dg2, dcwa, dcwb = _ffn_bwd_b_call(dua, dub, up, cw_g, wup_g, h1, ffn_norm_g, dh2, tm, tps,
                                                       "ffn_bwd_up")
    dwup = jnp.concatenate([_mm_tn(z2, dupa, N_DEV // 2, tm, "dw_up_a", True),
                            _mm_tn(z2, dupb, N_DEV // 2, tm, "dw_up_b", True)], axis=0)
    sh_rows = D // N_DEV
    sa = _exchange_start([dwup, dwd.reshape(N_DEV, w_down.shape[1], D)], True, "scatter_a_start")
    dmg, dwout = _lin_bwd(mg, dh1, wout_g + sa[4][0:1, 0:1].astype(MXU), tm, "out_proj_bwd")
    da_br, dbm, dga, dgb = _merge_bwd_call(dmg, a_br, bm, p, col_ga, tm, "merge_bwd")
    dyo, dwsp = _lin_bwd(yo, da_br, wsp_g, tm, "ssm_proj_bwd")
    dyb, dwhp = _lin_bwd(yb, dbm, whp_g, tm, "hgrn_proj_bwd")
    dya, dwglu = _glu_bwd_call(ya, dyo, wglu_g, tm, "s5_glu_bwd")
    sb = _exchange_start([t.reshape(N_DEV, sh_rows, D) for t in (dwglu, dwsp, dwhp, dwout)], True, "scatter_b_start")
    tok_b = sb[4][0:1, :]
    du_seg, dwb, dwc, dab, ddsk = _s5_bwd_call(u_seg, s_all, _to_segments(dya.reshape(B, L, W), seg), wb, wc, tab,
                                               dsk + tok_b[None], "s5_bwd")
    du = _from_segments(du_seg, seg, L)

    def diag_b(dw):
        t = (dw.reshape(n_cb, gpb, H, gpb, P) * eye[None, :, None, :, None]).sum(axis=1)
        return t.transpose(1, 0, 2, 3).reshape(H, G * P)

    def diag_c(dw):
        t = (dw.reshape(n_cb, gpb, P, gpb, H) * eye[None, :, None, :, None]).sum(axis=3)
        return t.transpose(0, 1, 3, 2).reshape(G, H, P)

    early_parts = [dab[:, 0, :hw].reshape(G, P), dab[:, 0, hw:].reshape(G, P), ddsk.reshape(1, D)]
    early = [_pack(early_parts), diag_b(dwb[:, :, :hw]), diag_b(dwb[:, :, hw:]),
             diag_c(dwc[:, :hw]).reshape(G * H, P), -diag_c(dwc[:, hw:]).reshape(G * H, P)]
    se = _exchange_start(early, False, "gather_s5_grads_start")
    dq, dfl, di, dog, dlb, dng = _hgrn_bwd_call(p3, dyb.reshape(B, L, D), lb, hgrn_norm_g + tok_b + se[4][0:1, :],
                                                n_heads, n_cb, "hgrn_bwd")
    dp = jnp.concatenate([du.reshape(T, W), dq.reshape(T, D), dfl.reshape(T, D), di.reshape(T, D),
                          dog.reshape(T, D), dga, dgb], axis=1)
    dwin = _mm_tn(z1, dp, N_DEV, tm, "dw_in", False)
    sc = _exchange_start([dwin.astype(WIRE)], True, "scatter_c_start")
    dh0, dg1 = _in_bwd_call(dp, win_g, h0, mix_norm_g + sc[4][0:1, 0:1], dh1, tm, "in_proj_bwd")
    dh0_3 = dh0.reshape(B, L, D)
    grad_x = dh0_3[:, N_META:]
    dmeta = _meta_grad_call(dh0_3, "meta_grad")

    late_parts = [dg1, dlb, dng, dg2, jnp.concatenate([dcba, dcbb], axis=0).reshape(1, N_DEV * F), dg3, loss_part]
    late_pack = _pack(late_parts)

    dcw = jnp.concatenate([dcwa, dcwb], axis=0)
    dmeta_s = dmeta.reshape(N_META, N_DEV, D // N_DEV).transpose(1, 0, 2)
    parts_d = _exchange_call([dmeta_s, dcw], True, "scatter_small_grads")
    late_all = _exchange_call([late_pack], False, "gather_small_grads")[0]
    early_all = _place_own_call(*_exchange_wait(se, late_all, False, "gather_s5_grads_wait"), False, me,
                                "gather_s5_grads_own")
    parts_a = _place_own_call(*_exchange_wait(sa, late_all, True, "scatter_a_wait"), True, me, "scatter_a_own")
    parts_b = _place_own_call(*_exchange_wait(sb, late_all, True, "scatter_b_wait"), True, me, "scatter_b_own")
    parts_c = _place_own_call(*_exchange_wait(sc, late_all, True, "scatter_c_wait"), True, me, "scatter_c_own")
    parts = [parts_c[0], parts_a[0], *parts_b, parts_a[1], parts_d[0], parts_d[1]]

    def sum8(*gathered):
        out = []
        for a in gathered:
            t = a[0]
            for s in range(1, N_DEV):
                t = t + a[s]
            out.append(t)
        return tuple(out)

    sums = _small_call(sum8, [*early_all, late_all], [(a.shape, F32) for a in (*early, late_pack)], "sum_small_grads")
    t_abr, t_abi, g_dsk = _unpack(sums[0], [a.shape for a in early_parts])
    t_bbr, t_bbi = sums[1], sums[2]
    g_cre, g_cim = sums[3].reshape(G, H, P), sums[4].reshape(G, H, P)
    g_g1, t_lb, g_ng, g_g2, g_cb, g_g3, loss_v = _unpack(sums[5], [a.shape for a in late_parts])

    def disc_b_bwd(cr, ci, br, bi, dbr, dbi):
        _, vjp = jax.vjp(_disc_b, cr, ci, br, bi)
        return vjp((dbr, dbi))

    t_cr, t_ci, g_btr, g_bti = _small_call(
        disc_b_bwd, [coef_re.reshape(1, G * P), coef_im.reshape(1, G * P), bt_re, bt_im, t_bbr, t_bbi],
        [((1, G * P), F32)] * 2 + [((H, G * P), F32)] * 2, "s5_input_matrix_bwd")

    def disc_a_bwd(lr_, li_, ldt_, dar, dai, dcr, dci):
        _, vjp = jax.vjp(_disc_a, lr_, li_, ldt_)
        return vjp((dar, dai, dcr, dci))

    g_lr, g_li, g_ldt = _small_call(
        disc_a_bwd, [lr, li, ldt, t_abr, t_abi, t_cr.reshape(G, P), t_ci.reshape(G, P)],
        [((G, P), F32)] * 2 + [((G, 1), F32)], "s5_discretise_bwd")

    def lb_bwd(logits, d):
        _, vjp = jax.vjp(_lb_fn, logits)
        return vjp(d)

    g_lbl = _small_call(lb_bwd, [hgrn_lb_logits, t_lb], [(hgrn_lb_logits.shape, F32)], "hgrn_lower_bound_bwd")[0]

    grads = dict(
        mix_norm_g=g_g1, ssm_lambda_re=g_lr[None], ssm_lambda_im=g_li[None], ssm_log_dt=g_ldt.reshape(1, G),
        ssm_b_re=g_btr.reshape(H, G, P).transpose(1, 2, 0)[None], ssm_b_im=g_bti.reshape(H, G, P).transpose(1, 2, 0)[None],
        ssm_c_re=g_cre[None], ssm_c_im=g_cim[None], ssm_d=g_dsk, hgrn_lb_logits=g_lbl, hgrn_norm_g=g_ng,
        ffn_norm_g=g_g2, conv_b=g_cb.reshape(1, N_DEV * F), final_norm_g=g_g3.reshape(D))
    loss = loss_v[0, 0]

    delta, new_m, new_v = {}, {}, {}
    sharded = [("w_in", parts[0], (D, n_in)), ("w_up", parts[1], (D, F)), ("ssm_w_glu", parts[2], (sh_rows, D)),
               ("w_ssm_proj", parts[3], (sh_rows, D)), ("w_hgrn_proj", parts[4], (sh_rows, D)),
               ("w_out", parts[5], (sh_rows, D)), ("w_down", parts[6], (w_down.shape[1], D)),
               ("meta_tokens", parts[7], (N_META, D // N_DEV)), ("conv_w", parts[8], (3, F))]
    for name, part, shp in sharded:
        full = args[name].shape
        g, d_, nm, nv = _adamw_shard_call(args[name].reshape(shp), part, args["m_" + name].reshape(shp),
                                          args["v_" + name].reshape(shp), "adamw_" + name)
        grads[name], delta[name], new_m[name], new_v[name] = [t.reshape(full) for t in (g, d_, nm, nv)]

    for n, shp in (("ssm_b_re", (G * P, H)), ("ssm_b_im", (G * P, H)), ("ssm_c_re", (G * H, P)), ("ssm_c_im", (G * H, P))):
        outs = _small_call(_adamw, [t.reshape(shp) for t in (args[n], grads[n], args["m_" + n], args["v_" + n])],
                           [(shp, F32)] * 3, "adamw_" + n)
        delta[n], new_m[n], new_v[n] = [o.reshape(args[n].shape) for o in outs]
    rep = ["mix_norm_g", "ssm_lambda_re", "ssm_lambda_im", "ssm_log_dt", "ssm_d", "hgrn_lb_logits", "hgrn_norm_g",
           "ffn_norm_g", "conv_b", "final_norm_g"]
    rep_shapes = [args[n].shape for n in rep]
    packs = [_pack([args[pre + n] for n in rep]) for pre in ("", "m_", "v_")]
    g_pack = _pack([grads[n] for n in rep])
    outs = _small_call(lambda w, g, m, v: _adamw(w, g, m, v), [packs[0], g_pack, packs[1], packs[2]],
                       [(g_pack.shape, F32)] * 3, "adamw_replicated")
    for n, d_, nm, nv in zip(rep, *[_unpack(o, rep_shapes) for o in outs]):
        delta[n], new_m[n], new_v[n] = d_, nm, nv

    names = ["meta_tokens", "mix_norm_g", "w_in", "ssm_lambda_re", "ssm_lambda_im", "ssm_log_dt", "ssm_b_re",
             "ssm_b_im", "ssm_c_re", "ssm_c_im", "ssm_d", "ssm_w_glu", "w_ssm_proj", "hgrn_lb_logits", "hgrn_norm_g",
             "w_hgrn_proj", "w_out", "ffn_norm_g", "w_up", "conv_w", "conv_b", "w_down", "final_norm_g"]
    return (loss, grad_x, *[grads[n] for n in names], *[delta[n] for n in names],
            *[new_m[n] for n in names], *[new_v[n] for n in names])
```

```python
import jax
import jax.numpy as jnp
from jax import lax
from jax.experimental import pallas as pl
from jax.experimental.pallas import tpu as pltpu

F32 = jnp.float32
MXU = jnp.bfloat16
ACT = jnp.bfloat16
WIRE = jnp.bfloat16
N_DEV = 8
N_META = 16
CHUNK = 16
EPS = 1e-6
ADAM_LR, ADAM_B1, ADAM_B2, ADAM_EPS, ADAM_WD, ADAM_STEP = 0.001, 0.9, 0.999, 1e-08, 0.01, 10
SUBLANES = 8
LANES = 128
ROW_TILE_CAP = 700
VMEM_LIMIT = 60 * 1024 * 1024


def _cparams(**kw):
    return pltpu.CompilerParams(vmem_limit_bytes=VMEM_LIMIT, **kw)


def _tile(n, cap):
    best = None
    for t in range(16, min(n, cap) + 1, 16):
        if n % t == 0:
            best = t
    assert best is not None, (n, cap)
    return best


def _dot(a, b):
    return lax.dot_general(a.astype(MXU), b.astype(MXU), (((1,), (0,)), ((), ())), preferred_element_type=F32)


def _dot_nt(a, b):
    return lax.dot_general(a.astype(MXU), b.astype(MXU), (((1,), (1,)), ((), ())), preferred_element_type=F32)


def _dot_tn(a, b):
    return lax.dot_general(a.astype(MXU), b.astype(MXU), (((0,), (0,)), ((), ())), preferred_element_type=F32)


def _rms(x, g):
    return x * lax.rsqrt(jnp.mean(x * x, axis=-1, keepdims=True) + EPS) * g


def _silu(x):
    return x * jax.nn.sigmoid(x)


def _small_call(fn, ins, out_shapes, name):
    n_in = len(ins)

    def body(*refs):
        outs = fn(*[r[...] for r in refs[:n_in]])
        outs = outs if isinstance(outs, (tuple, list)) else (outs,)
        for r, o in zip(refs[n_in:], outs):
            r[...] = o.astype(r.dtype)

    vm = pl.BlockSpec(memory_space=pltpu.VMEM)
    return pl.pallas_call(
        body, name=name, out_shape=tuple(jax.ShapeDtypeStruct(s, d) for s, d in out_shapes),
        in_specs=[vm] * n_in, out_specs=tuple([vm] * len(out_shapes)), compiler_params=_cparams())(*ins)


def _disc_a(lr, li, ldt):
    dt = jnp.exp(ldt)
    mag = jnp.exp(lr * dt)
    ab_re = mag * jnp.cos(li * dt)
    ab_im = mag * jnp.sin(li * dt)
    den = lr * lr + li * li
    nr = ab_re - 1.0
    coef_re = (nr * lr + ab_im * li) / den
    coef_im = (ab_im * lr - nr * li) / den
    return ab_re, ab_im, coef_re, coef_im


def _disc_a_power(n):
    def fn(lr, li, ldt):
        ab_re, ab_im, coef_re, coef_im = _disc_a(lr, li, ldt)
        pr, pi, sr, si, m = None, None, ab_re, ab_im, n
        while m:
            if m & 1:
                pr, pi = (sr, si) if pr is None else (pr * sr - pi * si, pr * si + pi * sr)
            m >>= 1
            if m:
                sr, si = sr * sr - si * si, 2.0 * sr * si
        return ab_re, ab_im, pr, pi, coef_re, coef_im
    return fn


def _disc_b(coef_re, coef_im, bt_re, bt_im):
    return coef_re * bt_re - coef_im * bt_im, coef_re * bt_im + coef_im * bt_re


def _lb_fn(logits):
    return jax.nn.softmax(logits, axis=0)[0:1]


def _adamw(w, g, m, v):
    m = ADAM_B1 * m + (1.0 - ADAM_B1) * g
    v = ADAM_B2 * v + (1.0 - ADAM_B2) * jnp.square(g)
    m_hat = m / (1.0 - ADAM_B1 ** ADAM_STEP)
    v_hat = v / (1.0 - ADAM_B2 ** ADAM_STEP)
    delta = -ADAM_LR * (m_hat / (jnp.sqrt(v_hat) + ADAM_EPS) + ADAM_WD * w)
    return delta, m, v


def _norm_call(h, g, tm, name):
    T, D = h.shape

    def body(h_ref, g_ref, z_ref):
        z_ref[...] = _rms(h_ref[...], g_ref[...]).astype(ACT)

    return pl.pallas_call(
        body, name=name, out_shape=jax.ShapeDtypeStruct((T, D), ACT), grid=(T // tm,),
        in_specs=[pl.BlockSpec((tm, D), lambda i: (i, 0)), pl.BlockSpec((1, D), lambda i: (0, 0))],
        out_specs=pl.BlockSpec((tm, D), lambda i: (i, 0)), compiler_params=_cparams())(h, g)


def _mm_shard(x, w, tm, name, major):
    T, K = x.shape
    S, _, N = w.shape

    def body(x_ref, w_ref, o_ref):
        o_ref[...] = _dot(x_ref[...], w_ref[...]).astype(o_ref.dtype)

    if major:
        out_shape = jax.ShapeDtypeStruct((S, T, N), ACT)
        out_spec = pl.BlockSpec((None, tm, N), lambda j, i: (j, i, 0))
    else:
        out_shape = jax.ShapeDtypeStruct((T, S * N), ACT)
        out_spec = pl.BlockSpec((tm, N), lambda j, i: (i, j))
    return pl.pallas_call(
        body, name=name, out_shape=out_shape, grid=(S, T // tm),
        in_specs=[pl.BlockSpec((tm, K), lambda j, i: (i, 0)), pl.BlockSpec((None, K, N), lambda j, i: (j, 0, 0))],
        out_specs=out_spec, compiler_params=_cparams())(x, w)


def _mm_tn(x, y, n_shards, tm, name, major):
    T, K = x.shape
    S = n_shards
    N = y.shape[-1] if major else y.shape[-1] // S

    def body(x_ref, y_ref, o_ref):
        @pl.when(pl.program_id(1) == 0)
        def _():
            o_ref[...] = jnp.zeros_like(o_ref)
        o_ref[...] += _dot_tn(x_ref[...], y_ref[...])

    y_spec = (pl.BlockSpec((None, tm, N), lambda j, i: (j, i, 0)) if major
              else pl.BlockSpec((tm, N), lambda j, i: (i, j)))
    return pl.pallas_call(
        body, name=name, out_shape=jax.ShapeDtypeStruct((S, K, N), F32), grid=(S, T // tm),
        in_specs=[pl.BlockSpec((tm, K), lambda j, i: (i, 0)), y_spec],
        out_specs=pl.BlockSpec((None, K, N), lambda j, i: (j, 0, 0)), compiler_params=_cparams())(x, y)


def _lin_bwd(x, dy, w, tm, name):
    T, K = x.shape
    N = dy.shape[1]

    def body(x_ref, dy_ref, w_ref, dx_ref, dw_ref):
        @pl.when(pl.program_id(0) == 0)
        def _():
            dw_ref[...] = jnp.zeros_like(dw_ref)
        dy = dy_ref[...]
        dx_ref[...] = _dot_nt(dy, w_ref[...]).astype(dx_ref.dtype)
        dw_ref[...] += _dot_tn(x_ref[...], dy)

    return pl.pallas_call(
        body, name=name,
        out_shape=(jax.ShapeDtypeStruct((T, K), ACT), jax.ShapeDtypeStruct((K, N), F32)), grid=(T // tm,),
        in_specs=[pl.BlockSpec((tm, K), lambda i: (i, 0)), pl.BlockSpec((tm, N), lambda i: (i, 0)),
                  pl.BlockSpec((K, N), lambda i: (0, 0))],
        out_specs=(pl.BlockSpec((tm, K), lambda i: (i, 0)), pl.BlockSpec((K, N), lambda i: (0, 0))),
        compiler_params=_cparams())(x, dy, w)


N_SEG = SUBLANES
CHAIN_STEPS = 8


def _chain_loop(n, step, init):
    per = min(CHAIN_STEPS, n)

    def trip(t, carry):
        for u in range(per):
            carry = step(t * per + u, carry)
        return carry

    carry = lax.fori_loop(0, n // per, trip, init)
    for i in range(n // per * per, n):
        carry = step(jnp.int32(i), carry)
    return carry


def _seg_len(L):
    return -(-L // (N_SEG * SUBLANES)) * SUBLANES


def _to_segments(a3, seg):
    b, length, c = a3.shape
    a = jnp.pad(a3, ((0, 0), (0, N_SEG * seg - length), (0, 0)))
    return a.reshape(b, N_SEG, seg, c).transpose(0, 2, 1, 3).reshape(b, N_SEG * seg, c)


def _from_segments(a3, seg, length):
    b, _, c = a3.shape
    return a3.reshape(b, seg, N_SEG, c).transpose(0, 2, 1, 3).reshape(b, N_SEG * seg, c)[:, :length]


def _seg_scan(x_ref, tab_ref, n_slabs, reverse):
    hw = x_ref.shape[1] // 2
    sign = -1.0 if reverse else 1.0
    ar, ai = tab_ref[0][:, :hw], sign * tab_ref[0][:, hw:]
    br, bi = tab_ref[1][:, :hw], sign * tab_ref[1][:, hw:]

    def slab(k):
        kk = (n_slabs - 1 - k) if reverse else k
        return pl.ds(pl.multiple_of(kk * SUBLANES, SUBLANES), SUBLANES)

    def horner(k, carry):
        cr, ci = carry
        x = x_ref[slab(k), :]
        return ar * cr - ai * ci + x[:, :hw], ar * ci + ai * cr + x[:, hw:]

    z = jnp.zeros((SUBLANES, hw), F32)
    fr, fi = _chain_loop(n_slabs, horner, (z, z))

    row = lax.broadcasted_iota(jnp.int32, (SUBLANES, hw), 0)
    edge = (row == SUBLANES - 1) if reverse else (row == 0)
    shift = SUBLANES - 1 if reverse else 1
    sr, si = z, z
    for _ in range(N_SEG - 1):
        er, ei = fr + br * sr - bi * si, fi + br * si + bi * sr
        sr = jnp.where(edge, 0.0, pltpu.roll(er, shift, 0))
        si = jnp.where(edge, 0.0, pltpu.roll(ei, shift, 0))

    def scan(k, carry):
        cr, ci = carry
        rows = slab(k)
        x = x_ref[rows, :]
        nr, ni = ar * cr - ai * ci + x[:, :hw], ar * ci + ai * cr + x[:, hw:]
        x_ref[rows, 0:hw] = nr
        x_ref[rows, hw:2 * hw] = ni
        return nr, ni

    _chain_loop(n_slabs, scan, (sr, si))


def _s5_fwd_call(p3, wb, wc, tab_f, dsk, name):
    B, L, _ = p3.shape
    n_cb, cw, sw = wb.shape

    def body(u_ref, wb_ref, wc_ref, tab_ref, d_ref, ya_ref, so_ref, s_ref):
        u = u_ref[...]
        s_ref[...] = _dot(u, wb_ref[...])
        _seg_scan(s_ref, tab_ref, L // SUBLANES, False)
        s = s_ref[...].astype(MXU)
        so_ref[...] = s
        y = _dot(s, wc_ref[...]) + d_ref[...] * u.astype(F32)
        ya_ref[...] = jax.nn.gelu(y).astype(ACT)

    return pl.pallas_call(
        body, name=name,
        out_shape=(jax.ShapeDtypeStruct((B, L, n_cb * cw), ACT), jax.ShapeDtypeStruct((B, n_cb, L, sw), MXU)),
        grid=(B, n_cb),
        in_specs=[pl.BlockSpec((None, L, cw), lambda b, c: (b, 0, c)),
                  pl.BlockSpec((None, cw, sw), lambda b, c: (c, 0, 0)),
                  pl.BlockSpec((None, sw, cw), lambda b, c: (c, 0, 0)),
                  pl.BlockSpec((None, 2, SUBLANES, sw), lambda b, c: (c, 0, 0, 0)),
                  pl.BlockSpec((None, 1, cw), lambda b, c: (c, 0, 0))],
        out_specs=(pl.BlockSpec((None, L, cw), lambda b, c: (b, 0, c)),
                   pl.BlockSpec((None, None, L, sw), lambda b, c: (b, c, 0, 0))),
        scratch_shapes=[pltpu.VMEM((L, sw), F32)], compiler_params=_cparams())(p3, wb, wc, tab_f, dsk)


def _s5_bwd_call(p3, s_all, dya, wb, wc, tab_r, dsk, name):
    B, L, _ = p3.shape
    n_cb, cw, sw = wb.shape
    hw = sw // 2
    n_slabs = L // SUBLANES

    def body(u_ref, si_ref, dya_ref, wb_ref, wc_ref, tr_ref, d_ref,
             du_ref, dwb_ref, dwc_ref, da_ref, dd_ref, s_ref, l_ref):
        @pl.when(pl.program_id(1) == 0)
        def _():
            dwb_ref[...] = jnp.zeros_like(dwb_ref)
            dwc_ref[...] = jnp.zeros_like(dwc_ref)
            da_ref[...] = jnp.zeros_like(da_ref)
            dd_ref[...] = jnp.zeros_like(dd_ref)

        u = u_ref[...]
        uf = u.astype(F32)
        s_in = si_ref[...]
        s_ref[...] = s_in.astype(F32)
        y = _dot(s_in, wc_ref[...]) + d_ref[...] * uf
        _, gelu_vjp = jax.vjp(jax.nn.gelu, y)
        dy = gelu_vjp(dya_ref[...].astype(F32))[0]
        dd_ref[...] += jnp.sum(dy * uf, axis=0, keepdims=True)
        l_ref[...] = _dot_nt(dy, wc_ref[...])
        _seg_scan(l_ref, tr_ref, n_slabs, True)
        du_ref[...] = (_dot_nt(l_ref[...], wb_ref[...]) + d_ref[...] * dy).astype(ACT)
        dwb_ref[...] += _dot_tn(u, l_ref[...])
        dwc_ref[...] += _dot_tn(s_in, dy)

        row = lax.broadcasted_iota(jnp.int32, (SUBLANES, hw), 0)
        last = s_ref[pl.ds((n_slabs - 1) * SUBLANES, SUBLANES), :]
        p0r = jnp.where(row == 0, 0.0, pltpu.roll(last[:, :hw], 1, 0))
        p0i = jnp.where(row == 0, 0.0, pltpu.roll(last[:, hw:], 1, 0))

        def step(k, carry):
            qr, qi, accr, acci = carry
            r0 = pl.multiple_of(k * SUBLANES, SUBLANES)
            s = s_ref[pl.ds(r0, SUBLANES), :]
            lam = l_ref[pl.ds(r0, SUBLANES), :]
            lr, li = lam[:, :hw], lam[:, hw:]
            accr = accr + lr * qr + li * qi
            acci = acci + li * qr - lr * qi
            return s[:, :hw], s[:, hw:], accr, acci

        z8 = jnp.zeros((SUBLANES, hw), F32)
        _, _, accr, acci = _chain_loop(n_slabs, step, (p0r, p0i, z8, z8))
        da_ref[...] += jnp.concatenate([jnp.sum(accr, axis=0, keepdims=True),
                                        jnp.sum(acci, axis=0, keepdims=True)], axis=1)

    W = n_cb * cw
    return pl.pallas_call(
        body, name=name,
        out_shape=(jax.ShapeDtypeStruct((B, L, W), ACT), jax.ShapeDtypeStruct((n_cb, cw, sw), F32),
                   jax.ShapeDtypeStruct((n_cb, sw, cw), F32), jax.ShapeDtypeStruct((n_cb, 1, sw), F32),
                   jax.ShapeDtypeStruct((n_cb, 1, cw), F32)),
        grid=(n_cb, B),
        in_specs=[pl.BlockSpec((None, L, cw), lambda c, b: (b, 0, c)),
                  pl.BlockSpec((None, None, L, sw), lambda c, b: (b, c, 0, 0)),
                  pl.BlockSpec((None, L, cw), lambda c, b: (b, 0, c)),
                  pl.BlockSpec((None, cw, sw), lambda c, b: (c, 0, 0)),
                  pl.BlockSpec((None, sw, cw), lambda c, b: (c, 0, 0)),
                  pl.BlockSpec((None, 2, SUBLANES, sw), lambda c, b: (c, 0, 0, 0)),
                  pl.BlockSpec((None, 1, cw), lambda c, b: (c, 0, 0))],
        out_specs=(pl.BlockSpec((None, L, cw), lambda c, b: (b, 0, c)),
                   pl.BlockSpec((None, cw, sw), lambda c, b: (c, 0, 0)),
                   pl.BlockSpec((None, sw, cw), lambda c, b: (c, 0, 0)),
                   pl.BlockSpec((None, 1, sw), lambda c, b: (c, 0, 0)),
                   pl.BlockSpec((None, 1, cw), lambda c, b: (c, 0, 0))),
        scratch_shapes=[pltpu.VMEM((L, sw), F32), pltpu.VMEM((L, sw), F32)],
        compiler_params=_cparams())(p3, s_all, dya, wb, wc, tab_r, dsk)


def _glu_proj_call(ya, wglu, wproj, tm, name):
    T, W = ya.shape
    D = wproj.shape[1]

    def body(ya_ref, wg_ref, wp_ref, yo_ref, a_ref):
        ya = ya_ref[...]
        yo = ya.astype(F32) * jax.nn.sigmoid(_dot(ya, wg_ref[...]))
        yo_ref[...] = yo.astype(ACT)
        a_ref[...] = _dot(yo, wp_ref[...]).astype(ACT)

    return pl.pallas_call(
        body, name=name, out_shape=(jax.ShapeDtypeStruct((T, W), ACT), jax.ShapeDtypeStruct((T, D), ACT)),
        grid=(T // tm,),
        in_specs=[pl.BlockSpec((tm, W), lambda i: (i, 0)), pl.BlockSpec((W, W), lambda i: (0, 0)),
                  pl.BlockSpec((W, D), lambda i: (0, 0))],
        out_specs=(pl.BlockSpec((tm, W), lambda i: (i, 0)), pl.BlockSpec((tm, D), lambda i: (i, 0))),
        compiler_params=_cparams())(ya, wglu, wproj)


def _glu_bwd_call(ya, dyo, wglu, tm, name):
    T, W = ya.shape

    def body(ya_ref, dyo_ref, wg_ref, dya_ref, dwg_ref):
        @pl.when(pl.program_id(0) == 0)
        def _():
            dwg_ref[...] = jnp.zeros_like(dwg_ref)
        ya = ya_ref[...]
        yaf = ya.astype(F32)
        dyo = dyo_ref[...].astype(F32)
        sg = jax.nn.sigmoid(_dot(ya, wg_ref[...]))
        dt = dyo * yaf * sg * (1.0 - sg)
        dya_ref[...] = (dyo * sg + _dot_nt(dt, wg_ref[...])).astype(ACT)
        dwg_ref[...] += _dot_tn(ya, dt)

    return pl.pallas_call(
        body, name=name, out_shape=(jax.ShapeDtypeStruct((T, W), ACT), jax.ShapeDtypeStruct((W, W), F32)),
        grid=(T // tm,),
        in_specs=[pl.BlockSpec((tm, W), lambda i: (i, 0)), pl.BlockSpec((tm, W), lambda i: (i, 0)),
                  pl.BlockSpec((W, W), lambda i: (0, 0))],
        out_specs=(pl.BlockSpec((tm, W), lambda i: (i, 0)), pl.BlockSpec((W, W), lambda i: (0, 0))),
        compiler_params=_cparams())(ya, dyo, wglu)


PAD = 16


def _chunk_cumsums(x, pad_ref, L):
    row = lax.broadcasted_iota(jnp.int32, x.shape, 0) % CHUNK
    zeros = jnp.zeros((PAD, x.shape[1]), F32)
    pad_ref[0:PAD, :] = zeros
    pad_ref[PAD + L:2 * PAD + L, :] = zeros
    c = x
    r = x
    d = 1
    while d < CHUNK:
        pad_ref[PAD:PAD + L, :] = c
        c = c + jnp.where(row >= d, pad_ref[PAD - d:PAD - d + L, :], 0.0)
        pad_ref[PAD:PAD + L, :] = r
        r = r + jnp.where(row + d < CHUNK, pad_ref[PAD + d:PAD + d + L, :], 0.0)
        d *= 2
    return c, r - x


def _hgrn_prep(q_ref, fl_ref, lb_ref, pad_ref, r0, n):
    rows = pl.ds(r0, n)
    lb = lb_ref[...]
    sig = jax.nn.sigmoid(fl_ref[rows, :].astype(F32))
    f = lb + (1.0 - lb) * sig
    k = 1.0 - f
    c, rc = _chunk_cumsums(jnp.log(f), pad_ref, n)
    e_in, e_inv, e_out = jnp.exp(c), jnp.exp(-c), jnp.exp(rc)
    q = q_ref[rows, :].astype(F32)
    return dict(sig=sig, f=f, k=k, q=q, e_in=e_in, e_inv=e_inv, e_out=e_out, dec=jnp.exp(c + rc))


def _for_row_blocks(L, fn):
    full = L // GROUP
    if full:
        def step(g, carry):
            fn(pl.multiple_of(g * GROUP, GROUP), GROUP)
            return carry
        lax.fori_loop(0, full, step, 0)
    if L % GROUP:
        fn(full * GROUP, L % GROUP)


def _chunk_mask(rb):
    r = lax.broadcasted_iota(jnp.int32, (rb, rb), 0)
    c = lax.broadcasted_iota(jnp.int32, (rb, rb), 1)
    return (r // CHUNK == c // CHUNK) & (c <= r)


def _hg_out(o, og, g):
    on = o * lax.rsqrt(jnp.mean(o * o, axis=-1, keepdims=True) + EPS) * g
    return on * _silu(og)


def _hgrn_specs(L, hd, col_q, n_heads, order):
    def spec(sec):
        return pl.BlockSpec((None, L, hd), lambda *g: (order(*g)[0], 0, col_q + sec * n_heads + order(*g)[1]))
    return [spec(0), spec(1), spec(2), spec(3)]


GROUP = 128
CPG = GROUP // CHUNK


def _expand(x):
    xf = x.astype(F32)
    chunk = lax.broadcasted_iota(jnp.int32, xf.shape, 0) // CHUNK
    return jnp.concatenate([jnp.where(chunk == j, xf, 0.0) for j in range(CPG)], axis=1)


def _fill_tail(refs_fills, L):
    for ref, fill in refs_fills:
        if ref.shape[0] > L:
            ref[L:ref.shape[0], :] = jnp.full((ref.shape[0] - L, ref.shape[1]), fill, ref.dtype)


GROUP_UNROLL = 17


def _hgrn_forward_core(q_ref, fl_ref, v_ref, lb_ref, pad_ref, qin_ref, kin_ref, kout_ref, vp_ref, dec_ref, o_ref,
                       s_ref, a_ref, L, keep=()):
    hd = qin_ref.shape[1]
    n_groups = qin_ref.shape[0] // GROUP

    def prep(r0, n):
        pp = _hgrn_prep(q_ref, fl_ref, lb_ref, pad_ref, r0, n)
        rows = pl.ds(r0, n)
        for key, ref in keep:
            ref[rows, :] = pp[key]
        qin_ref[rows, :] = (pp["q"] * pp["e_in"]).astype(MXU)
        kin_ref[rows, :] = (pp["k"] * pp["e_inv"]).astype(MXU)
        kout_ref[rows, :] = (pp["k"] * pp["e_out"]).astype(MXU)
        vp_ref[rows, :] = v_ref[rows, :].astype(MXU)
        dec_ref[rows, :] = pp["dec"]

    _for_row_blocks(L, prep)
    _fill_tail(((qin_ref, 0.0), (kin_ref, 0.0), (kout_ref, 0.0), (vp_ref, 0.0), (dec_ref, 1.0)), L)
    mask = _chunk_mask(GROUP)

    def scores(g, carry):
        rows = pl.ds(pl.multiple_of(g * GROUP, GROUP), GROUP)
        a_ref[rows, :] = jnp.where(mask, _dot_nt(qin_ref[rows, :], kin_ref[rows, :]), 0.0).astype(MXU)
        return carry

    lax.fori_loop(0, n_groups, scores, 0, unroll=GROUP_UNROLL)

    def intra(g, carry):
        rows = pl.ds(pl.multiple_of(g * GROUP, GROUP), GROUP)
        o_ref[rows, :] = _dot(a_ref[rows, :], vp_ref[rows, :])
        kv = _dot_tn(vp_ref[rows, :], _expand(kout_ref[rows, :]))
        for j in range(CPG):
            s_ref[g * CPG + j] = kv[:, j * hd:(j + 1) * hd]
        return carry

    lax.fori_loop(0, n_groups, intra, 0, unroll=GROUP_UNROLL)

    def rec(n, st):
        kv = s_ref[n]
        s_ref[n] = st
        dec = dec_ref[pl.ds(pl.multiple_of(n * CHUNK, CHUNK), SUBLANES), :][0:1]
        return st * dec + kv

    _chain_loop(L // CHUNK, rec, jnp.zeros((hd, hd), F32))

    def inter(g, carry):
        rows = pl.ds(pl.multiple_of(g * GROUP, GROUP), GROUP)
        scat = jnp.concatenate([s_ref[g * CPG + j] for j in range(CPG)], axis=1)
        o_ref[rows, :] += _dot_nt(_expand(qin_ref[rows, :]), scat)
        return carry

    lax.fori_loop(0, n_groups, inter, 0, unroll=GROUP_UNROLL)


def _hgrn_scratch(L, hd):
    lp = -(-L // GROUP) * GROUP
    return lp, [pltpu.VMEM((GROUP + 2 * PAD, hd), F32), pltpu.VMEM((lp, hd), MXU), pltpu.VMEM((lp, hd), MXU),
                pltpu.VMEM((lp, hd), MXU), pltpu.VMEM((lp, hd), MXU), pltpu.VMEM((lp, hd), F32),
                pltpu.VMEM((lp, hd), F32), pltpu.VMEM((lp // CHUNK, hd, hd), F32), pltpu.VMEM((lp, GROUP), MXU)]


def _hgrn_fwd_call(p3, lb, ng, n_heads, col_q, name):
    B, L, _ = p3.shape
    hd = ng.shape[1]
    _, scratch = _hgrn_scratch(L, hd)

    def body(q_ref, fl_ref, v_ref, og_ref, lb_ref, ng_ref, yb_ref,
             pad_ref, qin_ref, kin_ref, kout_ref, vp_ref, dec_ref, o_ref, s_ref, a_ref):
        _hgrn_forward_core(q_ref, fl_ref, v_ref, lb_ref, pad_ref, qin_ref, kin_ref, kout_ref, vp_ref, dec_ref,
                           o_ref, s_ref, a_ref, L)

        def out(r0, n):
            rows = pl.ds(r0, n)
            yb_ref[rows, :] = _hg_out(o_ref[rows, :], og_ref[rows, :].astype(F32), ng_ref[...]).astype(ACT)

        _for_row_blocks(L, out)

    order = lambda b, h: (b, h)
    return pl.pallas_call(
        body, name=name, out_shape=jax.ShapeDtypeStruct((B, L, n_heads * hd), ACT), grid=(B, n_heads),
        in_specs=_hgrn_specs(L, hd, col_q, n_heads, order) + [
            pl.BlockSpec((1, hd), lambda b, h: (0, h)), pl.BlockSpec((1, hd), lambda b, h: (0, 0))],
        out_specs=pl.BlockSpec((None, L, hd), lambda b, h: (b, 0, h)),
        scratch_shapes=scratch, compiler_params=_cparams())(p3, p3, p3, p3, lb, ng)


def _hgrn_bwd_call(p3, dyb, lb, ng, n_heads, col_q, name):
    B, L, _ = p3.shape
    hd = ng.shape[1]
    n_chunks = L // CHUNK
    lp, scratch = _hgrn_scratch(L, hd)
    n_groups = lp // GROUP

    def body(q_ref, fl_ref, v_ref, og_ref, dyb_ref, lb_ref, ng_ref,
             dq_ref, dfl_ref, dv_ref, dog_ref, dlb_ref, dng_ref,
             pad_ref, qin_ref, kin_ref, kout_ref, vp_ref, dec_ref, o_ref, s_ref, a_ref,
             do_ref, ds_ref, dqi_ref, dki_ref, dko_ref, dvv_ref, dct_ref,
             sig_ref, f_ref, ein_ref, einv_ref, eout_ref, da_ref):
        @pl.when(pl.program_id(1) == 0)
        def _():
            dlb_ref[...] = jnp.zeros_like(dlb_ref)

        @pl.when((pl.program_id(0) == 0) & (pl.program_id(1) == 0))
        def _():
            dng_ref[...] = jnp.zeros_like(dng_ref)

        _hgrn_forward_core(q_ref, fl_ref, v_ref, lb_ref, pad_ref, qin_ref, kin_ref, kout_ref, vp_ref, dec_ref,
                           o_ref, s_ref, a_ref, L, keep=(("sig", sig_ref), ("f", f_ref), ("e_in", ein_ref),
                                                  ("e_inv", einv_ref), ("e_out", eout_ref)))

        def out_bwd(r0, n):
            rows = pl.ds(r0, n)
            _, out_vjp = jax.vjp(_hg_out, o_ref[rows, :], og_ref[rows, :].astype(F32), ng_ref[...])
            d_o, d_og, d_ng = out_vjp(dyb_ref[rows, :].astype(F32))
            dog_ref[rows, :] = d_og.astype(ACT)
            dng_ref[...] += d_ng
            do_ref[rows, :] = d_o.astype(MXU)

        _for_row_blocks(L, out_bwd)
        _fill_tail(((do_ref, 0.0),), L)
        mask = _chunk_mask(GROUP)

        def score_grads(g, carry):
            rows = pl.ds(pl.multiple_of(g * GROUP, GROUP), GROUP)
            da_ref[rows, :] = jnp.where(mask, _dot_nt(do_ref[rows, :], vp_ref[rows, :]), 0.0).astype(MXU)
            return carry

        lax.fori_loop(0, n_groups, score_grads, 0, unroll=GROUP_UNROLL)

        def grads_a(g, carry):
            rows = pl.ds(pl.multiple_of(g * GROUP, GROUP), GROUP)
            qi, ki, do, da = qin_ref[rows, :], kin_ref[rows, :], do_ref[rows, :], da_ref[rows, :]
            sstack = s_ref[pl.ds(g * CPG, CPG)].reshape(CPG * hd, hd)
            dqi_ref[rows, :] = _dot(da, ki) + _dot(_expand(do), sstack)
            dki_ref[rows, :] = _dot_tn(da, qi)
            dvv_ref[rows, :] = _dot_tn(a_ref[rows, :], do)
            x = _dot_tn(do, _expand(qi))
            for j in range(CPG):
                ds_ref[g * CPG + j] = x[:, j * hd:(j + 1) * hd]
            return carry

        lax.fori_loop(0, n_groups, grads_a, 0, unroll=GROUP_UNROLL)

        def rec_bwd(k, dst):
            n = n_chunks - 1 - k
            r0 = pl.multiple_of(n * CHUNK, CHUNK)
            x = ds_ref[n]
            ds_ref[n] = dst
            dec = dec_ref[pl.ds(r0, SUBLANES), :][0:1]
            return dst * dec + x

        _chain_loop(n_chunks, rec_bwd, jnp.zeros((hd, hd), F32))

        def grads_b(g, carry):
            r0 = pl.multiple_of(g * GROUP, GROUP)
            rows = pl.ds(r0, GROUP)
            ds = [ds_ref[g * CPG + j] for j in range(CPG)]
            dscat = jnp.concatenate(ds, axis=1)
            dvv_ref[rows, :] += _dot_nt(_expand(kout_ref[rows, :]), dscat)
            dstack = ds_ref[pl.ds(g * CPG, CPG)].reshape(CPG * hd, hd)
            dko_ref[rows, :] = _dot(_expand(vp_ref[rows, :]), dstack)
            for j in range(CPG):
                dec = dec_ref[pl.ds(r0 + j * CHUNK, SUBLANES), :][0:1]
                ddec = dec * jnp.sum(ds[j] * s_ref[g * CPG + j], axis=0, keepdims=True)
                dct_ref[pl.ds(r0 + j * CHUNK, CHUNK), :] = jnp.broadcast_to(ddec, (CHUNK, hd))
            return carry

        lax.fori_loop(0, n_groups, grads_b, 0, unroll=GROUP_UNROLL)

        def finish(r0, n):
            rows = pl.ds(r0, n)
            sig, f, e_in, e_inv, e_out = [r[rows, :] for r in (sig_ref, f_ref, ein_ref, einv_ref, eout_ref)]
            q, k = q_ref[rows, :].astype(F32), 1.0 - f
            dqi, dki, dko = dqi_ref[rows, :], dki_ref[rows, :], dko_ref[rows, :]
            dq = dqi * e_in
            dk = dki * e_inv + dko * e_out
            dq_ref[rows, :] = dq.astype(ACT)
            dv_ref[rows, :] = dvv_ref[rows, :].astype(ACT)
            t_out = k * e_out * dko
            dc = q * dq - k * e_inv * dki - t_out
            _, dc_later = _chunk_cumsums(dc, pad_ref, n)
            t_incl, t_later = _chunk_cumsums(t_out, pad_ref, n)
            dlogf = dc + dc_later + t_incl + t_later + dct_ref[rows, :]
            df = dlogf / f - dk
            dfl_ref[rows, :] = (df * (1.0 - lb_ref[...]) * sig * (1.0 - sig)).astype(ACT)
            dlb_ref[...] += jnp.sum(df * (1.0 - sig), axis=0, keepdims=True)

        _for_row_blocks(L, finish)

    order = lambda h, b: (b, h)
    W = n_heads * hd
    act_out = jax.ShapeDtypeStruct((B, L, W), ACT)
    blk_out = pl.BlockSpec((None, L, hd), lambda h, b: (b, 0, h))
    return pl.pallas_call(
        body, name=name,
        out_shape=(act_out, act_out, act_out, act_out, jax.ShapeDtypeStruct((1, W), F32),
                   jax.ShapeDtypeStruct((1, hd), F32)),
        grid=(n_heads, B),
        in_specs=_hgrn_specs(L, hd, col_q, n_heads, order) + [
            pl.BlockSpec((None, L, hd), lambda h, b: (b, 0, h)),
            pl.BlockSpec((1, hd), lambda h, b: (0, h)), pl.BlockSpec((1, hd), lambda h, b: (0, 0))],
        out_specs=(blk_out, blk_out, blk_out, blk_out, pl.BlockSpec((1, hd), lambda h, b: (0, h)),
                   pl.BlockSpec((1, hd), lambda h, b: (0, 0))),
        scratch_shapes=scratch + [
            pltpu.VMEM((lp, hd), MXU), pltpu.VMEM((lp // CHUNK, hd, hd), F32)] + [pltpu.VMEM((lp, hd), F32)] * 10 + [
            pltpu.VMEM((lp, GROUP), MXU)],
        compiler_params=_cparams())(p3, p3, p3, p3, dyb, lb, ng)


def _merge_fn(a, bm, ga, gb):
    return jax.nn.sigmoid(ga) * a + jax.nn.sigmoid(gb) * bm


def _merge_call(yb, a, p, h0, whp, wout, g2, col_ga, tm, name):
    T, D = h0.shape

    def body(yb_ref, a_ref, ga_ref, gb_ref, h0_ref, whp_ref, wout_ref, g2_ref, h1_ref, mg_ref, bm_ref, z2_ref):
        bm = _dot(yb_ref[...], whp_ref[...])
        mg = _merge_fn(a_ref[...].astype(F32), bm, ga_ref[...].astype(F32), gb_ref[...].astype(F32))
        h1 = h0_ref[...] + _dot(mg, wout_ref[...])
        h1_ref[...] = h1
        mg_ref[...] = mg.astype(ACT)
        bm_ref[...] = bm.astype(ACT)
        z2_ref[...] = _rms(h1, g2_ref[...]).astype(ACT)

    tile = pl.BlockSpec((tm, D), lambda i: (i, 0))
    full = pl.BlockSpec((D, D), lambda i: (0, 0))
    act = jax.ShapeDtypeStruct((T, D), ACT)
    return pl.pallas_call(
        body, name=name, out_shape=(jax.ShapeDtypeStruct((T, D), F32), act, act, act), grid=(T // tm,),
        in_specs=[tile, tile, pl.BlockSpec((tm, D), lambda i: (i, col_ga)),
                  pl.BlockSpec((tm, D), lambda i: (i, col_ga + 1)), tile, full, full,
                  pl.BlockSpec((1, D), lambda i: (0, 0))],
        out_specs=(tile, tile, tile, tile), compiler_params=_cparams())(yb, a, p, p, h0, whp, wout, g2)


def _merge_bwd_call(dmg, a, bm, p, col_ga, tm, name):
    T, D = dmg.shape

    def body(dmg_ref, a_ref, bm_ref, ga_ref, gb_ref, da_ref, dbm_ref, dga_ref, dgb_ref):
        args = [r[...].astype(F32) for r in (a_ref, bm_ref, ga_ref, gb_ref)]
        _, vjp = jax.vjp(_merge_fn, *args)
        for r, o in zip((da_ref, dbm_ref, dga_ref, dgb_ref), vjp(dmg_ref[...].astype(F32))):
            r[...] = o.astype(ACT)

    tile = pl.BlockSpec((tm, D), lambda i: (i, 0))
    act = jax.ShapeDtypeStruct((T, D), ACT)
    return pl.pallas_call(
        body, name=name, out_shape=(act, act, act, act), grid=(T // tm,),
        in_specs=[tile, tile, tile, pl.BlockSpec((tm, D), lambda i: (i, col_ga)),
                  pl.BlockSpec((tm, D), lambda i: (i, col_ga + 1))],
        out_specs=(tile, tile, tile, tile), compiler_params=_cparams())(dmg, a, bm, p, p)


def _conv_taps(x_ref, halo_ref, ext_ref, edge, tm, before):
    halo = jnp.where(edge, 0.0, halo_ref[...].astype(F32))
    x = x_ref[...].astype(F32)
    if before:
        ext_ref[0:PAD, :] = halo
        ext_ref[PAD:PAD + tm, :] = x
        return [ext_ref[PAD - 2 + k:PAD - 2 + k + tm, :] for k in range(3)]
    ext_ref[0:tm, :] = x
    ext_ref[tm:tm + PAD, :] = halo
    return [ext_ref[k:k + tm, :] for k in range(3)]


def _conv(taps, cw, cb):
    return cb + cw[0:1] * taps[0] + cw[1:2] * taps[1] + cw[2:3] * taps[2]


def _ffn_pair_specs(tm, F, T, n_pairs, order, before):
    hb = tm // PAD
    last = T // PAD - 1

    def halo_row(i):
        return jnp.maximum(i * hb - 1, 0) if before else jnp.minimum((i + 1) * hb, last)

    specs = []
    for off in (0, n_pairs):
        specs.append(pl.BlockSpec((None, tm, F), lambda *g, off=off: (order(*g)[1] + off, order(*g)[0], 0)))
        specs.append(pl.BlockSpec((None, PAD, F), lambda *g, off=off: (order(*g)[1] + off, halo_row(order(*g)[0]), 0)))
    return specs


def _ffn_fwd_call(up, cw, cb, wd, h1, tgt, g3, tm, tps, name):
    S, T, F = up.shape
    n_pairs = S // 2
    D = h1.shape[1]

    def body(ua_ref, ha_ref, ub_ref, hb_ref, cwa_ref, cwb_ref, cba_ref, cbb_ref, wd_ref, h1_ref, tgt_ref, g3_ref,
             ca_ref, cb_ref, dh2_ref, loss_ref, dg3_ref, acc_ref, ext_ref):
        i, j = pl.program_id(0), pl.program_id(1)
        edge = (i % tps) == 0
        ua = _conv(_conv_taps(ua_ref, ha_ref, ext_ref, edge, tm, True), cwa_ref[...], cba_ref[...])
        ub = _conv(_conv_taps(ub_ref, hb_ref, ext_ref, edge, tm, True), cwb_ref[...], cbb_ref[...])
        ca_ref[...] = ua.astype(ACT)
        cb_ref[...] = ub.astype(ACT)
        contrib = _dot(_silu(ua) * ub, wd_ref[...])

        @pl.when(j == 0)
        def _():
            acc_ref[...] = h1_ref[...] + contrib

        @pl.when(j > 0)
        def _():
            acc_ref[...] += contrib

        @pl.when((i == 0) & (j == 0))
        def _():
            loss_ref[...] = jnp.zeros_like(loss_ref)
            dg3_ref[...] = jnp.zeros_like(dg3_ref)

        @pl.when(j == n_pairs - 1)
        def _():
            row = lax.broadcasted_iota(jnp.int32, (tm, 1), 0) + (i % tps) * tm
            valid = row >= N_META
            tgt = tgt_ref[...]

            def loss_fn(h2, g):
                err = _rms(h2, g) - tgt
                return 0.5 * jnp.sum(jnp.where(valid, err * err, 0.0)) / D

            loss, vjp = jax.vjp(loss_fn, acc_ref[...], g3_ref[...])
            dh2, dg3 = vjp(jnp.ones((), F32))
            dh2_ref[...] = dh2
            loss_ref[...] += loss
            dg3_ref[...] += dg3

    order = lambda i, j: (i, j)
    tile = pl.BlockSpec((tm, D), lambda i, j: (i, 0))
    vec = pl.BlockSpec((1, D), lambda i, j: (0, 0))
    return pl.pallas_call(
        body, name=name,
        out_shape=(jax.ShapeDtypeStruct((n_pairs, T, F), ACT), jax.ShapeDtypeStruct((n_pairs, T, F), ACT),
                   jax.ShapeDtypeStruct((T, D), F32), jax.ShapeDtypeStruct((1, LANES), F32),
                   jax.ShapeDtypeStruct((1, D), F32)),
        grid=(T // tm, n_pairs),
        in_specs=_ffn_pair_specs(tm, F, T, n_pairs, order, True) + [
            pl.BlockSpec((None, 3, F), lambda i, j: (j, 0, 0)), pl.BlockSpec((None, 3, F), lambda i, j: (j + n_pairs, 0, 0)),
            pl.BlockSpec((None, 1, F), lambda i, j: (j, 0, 0)), pl.BlockSpec((None, 1, F), lambda i, j: (j + n_pairs, 0, 0)),
            pl.BlockSpec((None, F, D), lambda i, j: (j, 0, 0)), tile, tile, vec],
        out_specs=(pl.BlockSpec((None, tm, F), lambda i, j: (j, i, 0)), pl.BlockSpec((None, tm, F), lambda i, j: (j, i, 0)),
                   tile, pl.BlockSpec((1, LANES), lambda i, j: (0, 0)), vec),
        scratch_shapes=[pltpu.VMEM((tm, D), F32), pltpu.VMEM((tm + PAD, F), F32)],
        compiler_params=_cparams())(up, up, up, up, cw, cw, cb, cb, wd, h1, tgt, g3)


def _ffn_bwd_a_call(dh2, ca, cb, wd, tm, name):
    n_pairs, T, F = ca.shape
    D = dh2.shape[1]

    def body(dh2_ref, ca_ref, cb_ref, wd_ref, dua_ref, dub_ref, dwd_ref, dcba_ref, dcbb_ref):
        @pl.when(pl.program_id(1) == 0)
        def _():
            for r in (dwd_ref, dcba_ref, dcbb_ref):
                r[...] = jnp.zeros_like(r)

        dh2 = dh2_ref[...]
        ua, ub = ca_ref[...].astype(F32), cb_ref[...].astype(F32)
        sa = jax.nn.sigmoid(ua)
        gate = ua * sa
        dact = _dot_nt(dh2, wd_ref[...])
        dwd_ref[...] += _dot_tn(gate * ub, dh2)
        dub = dact * gate
        dua = dact * ub * sa * (1.0 + ua * (1.0 - sa))
        dcba_ref[...] += jnp.sum(dua, axis=0, keepdims=True)
        dcbb_ref[...] += jnp.sum(dub, axis=0, keepdims=True)
        dua_ref[...] = dua.astype(ACT)
        dub_ref[...] = dub.astype(ACT)

    blk = pl.BlockSpec((None, tm, F), lambda j, i: (j, i, 0))
    vec = pl.BlockSpec((None, 1, F), lambda j, i: (j, 0, 0))
    return pl.pallas_call(
        body, name=name,
        out_shape=(jax.ShapeDtypeStruct((n_pairs, T, F), ACT), jax.ShapeDtypeStruct((n_pairs, T, F), ACT),
                   jax.ShapeDtypeStruct((n_pairs, F, D), F32), jax.ShapeDtypeStruct((n_pairs, 1, F), F32),
                   jax.ShapeDtypeStruct((n_pairs, 1, F), F32)),
        grid=(n_pairs, T // tm),
        in_specs=[pl.BlockSpec((tm, D), lambda j, i: (i, 0)), blk, blk, pl.BlockSpec((None, F, D), lambda j, i: (j, 0, 0))],
        out_specs=(blk, blk, pl.BlockSpec((None, F, D), lambda j, i: (j, 0, 0)), vec, vec),
        compiler_params=_cparams())(dh2, ca, cb, wd)


def _ffn_bwd_b_call(dua, dub, up, cw, wup, h1, g2, dh2, tm, tps, name):
    n_pairs, T, F = dua.shape
    D = h1.shape[1]
    hb = tm // PAD
    last = T // PAD - 1

    def body(da_ref, na_ref, db_ref, nb_ref, ua_ref, ub_ref, cwa_ref, cwb_ref, wa_ref, wb_ref, h1_ref, g2_ref, dh2_ref,
             dupa_ref, dupb_ref, dh1_ref, dg2_ref, dcwa_ref, dcwb_ref, acc_ref, ext_ref):
        i, j = pl.program_id(0), pl.program_id(1)
        edge = (i % tps) == tps - 1

        @pl.when((i == 0) & (j == 0))
        def _():
            dcwa_ref[...] = jnp.zeros_like(dcwa_ref)
            dcwb_ref[...] = jnp.zeros_like(dcwb_ref)

        outs = []
        for d_ref, n_ref, u_ref, cw_ref, o_ref, dcw_ref in (
                (da_ref, na_ref, ua_ref, cwa_ref, dupa_ref, dcwa_ref),
                (db_ref, nb_ref, ub_ref, cwb_ref, dupb_ref, dcwb_ref)):
            t = _conv_taps(d_ref, n_ref, ext_ref, edge, tm, False)
            cwv = cw_ref[...]
            dup = cwv[2:3] * t[0] + cwv[1:2] * t[1] + cwv[0:1] * t[2]
            o_ref[...] = dup.astype(ACT)
            outs.append(dup)
            u = u_ref[...].astype(F32)
            dcw_ref[j] += jnp.concatenate([jnp.sum(u * t[2 - k], axis=0, keepdims=True) for k in range(3)], axis=0)
        contrib = _dot_nt(outs[0], wa_ref[...]) + _dot_nt(outs[1], wb_ref[...])

        @pl.when(j == 0)
        def _():
            acc_ref[...] = contrib

        @pl.when(j > 0)
        def _():
            acc_ref[...] += contrib

        @pl.when((i == 0) & (j == 0))
        def _():
            dg2_ref[...] = jnp.zeros_like(dg2_ref)

        @pl.when(j == n_pairs - 1)
        def _():
            _, vjp = jax.vjp(_rms, h1_ref[...], g2_ref[...])
            dh, dg = vjp(acc_ref[...])
            dh1_ref[...] = dh2_ref[...] + dh
            dg2_ref[...] += dg

    tile = pl.BlockSpec((tm, D), lambda i, j: (i, 0))
    vec = pl.BlockSpec((1, D), lambda i, j: (0, 0))
    pair = lambda: [pl.BlockSpec((None, tm, F), lambda i, j: (j, i, 0)),
                    pl.BlockSpec((None, PAD, F), lambda i, j: (j, jnp.minimum((i + 1) * hb, last), 0))]
    act = jax.ShapeDtypeStruct((n_pairs, T, F), ACT)
    dcw = jax.ShapeDtypeStruct((n_pairs, 3, F), F32)
    dcw_spec = pl.BlockSpec((n_pairs, 3, F), lambda i, j: (0, 0, 0))
    return pl.pallas_call(
        body, name=name,
        out_shape=(act, act, jax.ShapeDtypeStruct((T, D), F32), jax.ShapeDtypeStruct((1, D), F32), dcw, dcw),
        grid=(T // tm, n_pairs),
        in_specs=pair() + pair() + [
            pl.BlockSpec((None, tm, F), lambda i, j: (j, i, 0)), pl.BlockSpec((None, tm, F), lambda i, j: (j + n_pairs, i, 0)),
            pl.BlockSpec((None, 3, F), lambda i, j: (j, 0, 0)), pl.BlockSpec((None, 3, F), lambda i, j: (j + n_pairs, 0, 0)),
            pl.BlockSpec((None, D, F), lambda i, j: (j, 0, 0)), pl.BlockSpec((None, D, F), lambda i, j: (j + n_pairs, 0, 0)),
            tile, vec, tile],
        out_specs=(pl.BlockSpec((None, tm, F), lambda i, j: (j, i, 0)), pl.BlockSpec((None, tm, F), lambda i, j: (j, i, 0)),
                   tile, vec, dcw_spec, dcw_spec),
        scratch_shapes=[pltpu.VMEM((tm, D), F32), pltpu.VMEM((tm + PAD, F), F32)],
        compiler_params=_cparams())(dua, dua, dub, dub, up, up, cw, cw, wup, wup, h1, g2, dh2)


def _in_bwd_call(dp, w_in, h0, g1, dh1, tm, name):
    T, D = h0.shape
    S, _, N = w_in.shape

    def body(dp_ref, w_ref, h0_ref, g1_ref, dh1_ref, dh0_ref, dg1_ref, acc_ref):
        i, j = pl.program_id(0), pl.program_id(1)
        contrib = _dot_nt(dp_ref[...], w_ref[...])

        @pl.when(j == 0)
        def _():
            acc_ref[...] = contrib

        @pl.when(j > 0)
        def _():
            acc_ref[...] += contrib

        @pl.when((i == 0) & (j == 0))
        def _():
            dg1_ref[...] = jnp.zeros_like(dg1_ref)

        @pl.when(j == S - 1)
        def _():
            _, vjp = jax.vjp(_rms, h0_ref[...], g1_ref[...])
            dh, dg = vjp(acc_ref[...])
            dh0_ref[...] = dh1_ref[...] + dh
            dg1_ref[...] += dg

    tile = pl.BlockSpec((tm, D), lambda i, j: (i, 0))
    vec = pl.BlockSpec((1, D), lambda i, j: (0, 0))
    return pl.pallas_call(
        body, name=name, out_shape=(jax.ShapeDtypeStruct((T, D), F32), jax.ShapeDtypeStruct((1, D), F32)),
        grid=(T // tm, S),
        in_specs=[pl.BlockSpec((tm, N), lambda i, j: (i, j)), pl.BlockSpec((None, D, N), lambda i, j: (j, 0, 0)),
                  tile, vec, tile],
        out_specs=(tile, vec), scratch_shapes=[pltpu.VMEM((tm, D), F32)],
        compiler_params=_cparams())(dp, w_in, h0, g1, dh1)


def _meta_grad_call(dh0_3, name):
    B, L, D = dh0_3.shape

    def body(d_ref, o_ref):
        o_ref[...] = jnp.sum(d_ref[...], axis=0)

    return pl.pallas_call(
        body, name=name, out_shape=jax.ShapeDtypeStruct((N_META, D), F32), grid=(1,),
        in_specs=[pl.BlockSpec((B, N_META, D), lambda i: (0, 0, 0))],
        out_specs=pl.BlockSpec((N_META, D), lambda i: (0, 0)), compiler_params=_cparams())(dh0_3)


_RELS = [(dx, dy, dc) for dx in (0, 1) for dy in (0, 1) for dc in (0, 1)][1:]


def _exchange_call(arrs, scatter, name):
    n = len(arrs)
    n_rel = len(_RELS)

    def body(*refs):
        ins, outs = refs[:n], refs[n:2 * n]
        send_sems, recv_sems, loc_sems = refs[2 * n:]
        x, y, c = lax.axis_index("x"), lax.axis_index("y"), lax.axis_index("c")
        me = 4 * x + 2 * y + c
        started = []
        for k in range(n):
            src_me = ins[k].at[me] if scatter else ins[k]
            loc = pltpu.make_async_copy(src_me, outs[k].at[me], loc_sems.at[k])
            loc.start()
            started.append(loc)
        waits = []
        for r, (dx, dy, dc) in enumerate(_RELS):
            px, py, pc = (x + dx) % 2, (y + dy) % 2, (c + dc) % 2
            pid = 4 * px + 2 * py + pc
            for k in range(n):
                s = k * n_rel + r
                src = ins[k].at[pid] if scatter else ins[k]
                cp = pltpu.make_async_remote_copy(
                    src_ref=src, dst_ref=outs[k].at[me], send_sem=send_sems.at[s], recv_sem=recv_sems.at[s],
                    device_id=(px, py, pc), device_id_type=pl.DeviceIdType.MESH)
                cp.start()
                waits.append(pltpu.make_async_remote_copy(
                    src_ref=src, dst_ref=outs[k].at[pid], send_sem=send_sems.at[s], recv_sem=recv_sems.at[s],
                    device_id=(px, py, pc), device_id_type=pl.DeviceIdType.MESH))
        for w in waits:
            w.wait_send()
            w.wait_recv()
        for loc in started:
            loc.wait()

    out_shape = tuple(jax.ShapeDtypeStruct(a.shape if scatter else (N_DEV,) + a.shape, a.dtype) for a in arrs)
    hbm = pl.BlockSpec(memory_space=pl.ANY)
    return pl.pallas_call(
        body, name=name, out_shape=out_shape, in_specs=[hbm] * n, out_specs=tuple([hbm] * n),
        scratch_shapes=[pltpu.SemaphoreType.DMA((n * n_rel,)), pltpu.SemaphoreType.DMA((n * n_rel,)),
                        pltpu.SemaphoreType.DMA((n,))],
        compiler_params=pltpu.CompilerParams(has_side_effects=True))(*arrs)


_HBM = pl.BlockSpec(memory_space=pltpu.HBM)
_SEM = pl.BlockSpec(memory_space=pltpu.SEMAPHORE)
_DATAFLOW = pltpu.SideEffectType.DATAFLOW_SIDE_EFFECTING


def _peer_copies(ins, lands, send_sems, recv_sems, scatter):
    n = len(ins)
    x, y, c = lax.axis_index("x"), lax.axis_index("y"), lax.axis_index("c")
    me = 4 * x + 2 * y + c
    sends, arrivals = [], []
    for r, (dx, dy, dc) in enumerate(_RELS):
        px, py, pc = (x + dx) % 2, (y + dy) % 2, (c + dc) % 2
        pid = 4 * px + 2 * py + pc
        for k in range(n):
            s = k * len(_RELS) + r
            src = ins[k].at[pid] if scatter else ins[k]
            for dst, out in ((lands[k].at[me], sends), (lands[k].at[pid], arrivals)):
                out.append(pltpu.make_async_remote_copy(
                    src_ref=src, dst_ref=dst, send_sem=send_sems.at[s], recv_sem=recv_sems.at[s],
                    device_id=(px, py, pc), device_id_type=pl.DeviceIdType.MESH))
    return sends, arrivals


def _exchange_start(arrs, scatter, name):
    n = len(arrs)
    n_sem = n * len(_RELS)

    def body(*refs):
        ins, lands = refs[:n], refs[n:2 * n]
        send_sems, recv_sems = refs[2 * n], refs[2 * n + 1]
        token = refs[-1]
        sends, _ = _peer_copies(ins, lands, send_sems, recv_sems, scatter)
        for cp in sends:
            cp.start()
        token[...] = jnp.zeros_like(token)

    land_shapes = [a.shape if scatter else (N_DEV,) + a.shape for a in arrs]
    ops = [pltpu.with_memory_space_constraint(a, pltpu.HBM) for a in arrs]
    ops += [pltpu.with_memory_space_constraint(lax.empty(s, a.dtype), pltpu.HBM) for s, a in zip(land_shapes, arrs)]
    out = pl.pallas_call(
        body, name=name,
        out_shape=(pltpu.SemaphoreType.DMA((n_sem,)), pltpu.SemaphoreType.DMA((n_sem,)),
                   *[pltpu.HBM(a.shape, a.dtype) for a in arrs],
                   *[pltpu.HBM(s, a.dtype) for s, a in zip(land_shapes, arrs)],
                   jax.ShapeDtypeStruct((SUBLANES, LANES), F32)),
        in_specs=[_HBM] * (2 * n),
        out_specs=(_SEM, _SEM, *[_HBM] * (2 * n), pl.BlockSpec(memory_space=pltpu.VMEM)),
        input_output_aliases={i: 2 + i for i in range(2 * n)},
        compiler_params=pltpu.CompilerParams(has_side_effects=_DATAFLOW))(*ops)
    return out[0], out[1], list(out[2:2 + n]), list(out[2 + n:2 + 2 * n]), out[-1]


def _exchange_wait(started, after, scatter, name):
    send_sems, recv_sems, srcs, lands, _ = started
    n = len(srcs)

    def body(*refs):
        ins, lands_ = refs[:n], refs[n:2 * n]
        _, arrivals = _peer_copies(ins, lands_, refs[2 * n], refs[2 * n + 1], scatter)
        for cp in arrivals:
            cp.wait_send()
            cp.wait_recv()

    out = pl.pallas_call(
        body, name=name,
        out_shape=(*[pltpu.HBM(a.shape, a.dtype) for a in srcs], *[pltpu.HBM(a.shape, a.dtype) for a in lands]),
        in_specs=[_HBM] * (2 * n) + [_SEM, _SEM, pl.BlockSpec(memory_space=pl.ANY)],
        out_specs=tuple([_HBM] * (2 * n)), input_output_aliases={i: i for i in range(2 * n)},
        compiler_params=pltpu.CompilerParams(has_side_effects=_DATAFLOW))(*srcs, *lands, send_sems, recv_sems, after)
    return list(out[:n]), list(out[n:])


def _place_own_call(srcs, lands, scatter, me, name):
    outs = []
    for k, (src, land) in enumerate(zip(srcs, lands)):
        R, C = land.shape[1:]
        tr = R
        while tr % 32 == 0 and tr * C * land.dtype.itemsize > 2 * 1024 * 1024:
            tr //= 2

        def body(me_ref, s_ref, l_ref, o_ref):
            o_ref[...] = s_ref[...]

        src_spec = (pl.BlockSpec((None, tr, C), lambda i, me_ref: (me_ref[0], i, 0)) if scatter
                    else pl.BlockSpec((tr, C), lambda i, me_ref: (i, 0)))
        outs.append(pl.pallas_call(
            body, name=f"{name}_{k}", out_shape=jax.ShapeDtypeStruct(land.shape, land.dtype),
            grid_spec=pltpu.PrefetchScalarGridSpec(
                num_scalar_prefetch=1, grid=(R // tr,),
                in_specs=[src_spec, pl.BlockSpec(memory_space=pl.ANY)],
                out_specs=pl.BlockSpec((None, tr, C), lambda i, me_ref: (me_ref[0], i, 0))),
            input_output_aliases={2: 0}, compiler_params=_cparams())(me, src, land))
    return outs


def _adamw_shard_call(w, parts, m, v, name):
    R, C = w.shape
    tr = _tile(R, 128) if R % 16 == 0 else R

    def body(w_ref, p_ref, m_ref, v_ref, g_ref, d_ref, nm_ref, nv_ref):
        g = p_ref[0].astype(F32)
        for s in range(1, N_DEV):
            g = g + p_ref[s].astype(F32)
        d, nm, nv = _adamw(w_ref[...], g, m_ref[...], v_ref[...])
        g_ref[...] = g
        d_ref[...] = d
        nm_ref[...] = nm
        nv_ref[...] = nv

    tile = pl.BlockSpec((tr, C), lambda i: (i, 0))
    sh = jax.ShapeDtypeStruct((R, C), F32)
    return pl.pallas_call(
        body, name=name, out_shape=(sh, sh, sh, sh), grid=(R // tr,),
        in_specs=[tile, pl.BlockSpec((N_DEV, tr, C), lambda i: (0, i, 0)), tile, tile],
        out_specs=(tile, tile, tile, tile), compiler_params=_cparams())(w, parts, m, v)


def _pack(arrs, rows_mult=SUBLANES):
    flat = jnp.concatenate([a.reshape(-1).astype(F32) for a in arrs])
    n = flat.shape[0]
    per = rows_mult * LANES
    total = -(-n // per) * per
    return jnp.pad(flat, (0, total - n)).reshape(total // LANES, LANES)


def _unpack(pack, shapes):
    flat = pack.reshape(-1)
    out, off = [], 0
    for s in shapes:
        n = 1
        for d in s:
            n *= d
        out.append(flat[off:off + n].reshape(s))
        off += n
    return out


def kernel(x, meta_tokens, mix_norm_g, w_in, ssm_lambda_re, ssm_lambda_im, ssm_log_dt, ssm_b_re, ssm_b_im, ssm_c_re, ssm_c_im, ssm_d, ssm_w_glu, w_ssm_proj, hgrn_lb_logits, hgrn_norm_g, w_hgrn_proj, w_out, ffn_norm_g, w_up, conv_w, conv_b, w_down, final_norm_g, loss_target, m_meta_tokens, m_mix_norm_g, m_w_in, m_ssm_lambda_re, m_ssm_lambda_im, m_ssm_log_dt, m_ssm_b_re, m_ssm_b_im, m_ssm_c_re, m_ssm_c_im, m_ssm_d, m_ssm_w_glu, m_w_ssm_proj, m_hgrn_lb_logits, m_hgrn_norm_g, m_w_hgrn_proj, m_w_out, m_ffn_norm_g, m_w_up, m_conv_w, m_conv_b, m_w_down, m_final_norm_g, v_meta_tokens, v_mix_norm_g, v_w_in, v_ssm_lambda_re, v_ssm_lambda_im, v_ssm_log_dt, v_ssm_b_re, v_ssm_b_im, v_ssm_c_re, v_ssm_c_im, v_ssm_d, v_ssm_w_glu, v_w_ssm_proj, v_hgrn_lb_logits, v_hgrn_norm_g, v_w_hgrn_proj, v_w_out, v_ffn_norm_g, v_w_up, v_conv_w, v_conv_b, v_w_down, v_final_norm_g):
    args = dict(locals())
    B, S_len, D = x.shape
    L = S_len + N_META
    T = B * L
    tm = _tile(L, ROW_TILE_CAP)
    tps = L // tm
    G, P = ssm_lambda_re.shape[1:]
    H = ssm_b_re.shape[-1]
    W = G * H
    n_cb = W // LANES
    gpb = G // n_cb
    hd = hgrn_norm_g.shape[1]
    n_heads = D // hd
    n_in = w_in.shape[2]
    F = w_up.shape[2]
    assert W == D and n_in % LANES == 0

    me = (4 * lax.axis_index("x") + 2 * lax.axis_index("y") + lax.axis_index("c")).astype(jnp.int32).reshape(1)
    meta_g, cw_g = _exchange_call([meta_tokens, conv_w[0]], False, "gather_small_params")
    ga = _exchange_start([w_in[0].astype(MXU)], False, "gather_a_start")
    gb = _exchange_start(
        [w_up[0].astype(MXU), ssm_w_glu[0].astype(MXU), w_ssm_proj[0].astype(MXU), w_hgrn_proj[0].astype(MXU),
         w_out[0].astype(MXU), w_down[0].astype(MXU)], False, "gather_b_start")
    started_tok = (ga[4] + gb[4])[0:1, 0:1]
    meta_full = meta_g.transpose(1, 0, 2).reshape(N_META, D)
    cb_g = conv_b.reshape(N_DEV, 1, F)

    h0 = jnp.concatenate([jnp.broadcast_to(meta_full[None], (B, N_META, D)), x], axis=1).reshape(T, D)
    tgt = jnp.concatenate([jnp.zeros((B, N_META, D), F32), loss_target], axis=1).reshape(T, D)

    lr, li = ssm_lambda_re[0], ssm_lambda_im[0]
    ldt = ssm_log_dt[0].reshape(G, 1)
    bt_re = ssm_b_re[0].transpose(2, 0, 1).reshape(H, G * P)
    bt_im = ssm_b_im[0].transpose(2, 0, 1).reshape(H, G * P)
    seg = _seg_len(L)
    a_re, a_im, as_re, as_im, coef_re, coef_im = _small_call(
        _disc_a_power(seg), [lr, li, ldt], [((G, P), F32)] * 6, "s5_discretise")
    bbt_re, bbt_im = _small_call(
        _disc_b, [coef_re.reshape(1, G * P), coef_im.reshape(1, G * P), bt_re, bt_im],
        [((H, G * P), F32)] * 2, "s5_input_matrix")
    eye = jnp.eye(gpb, dtype=F32)
    hw = gpb * P

    def expand_b(bbt):
        t = bbt.reshape(H, n_cb, gpb, P).transpose(1, 0, 2, 3)[:, None]
        return (t * eye[None, :, None, :, None]).reshape(n_cb, gpb * H, hw)

    def expand_c(cm):
        t = cm.reshape(n_cb, gpb, H, P).transpose(0, 1, 3, 2)[:, :, :, None]
        return (t * eye[None, :, None, :, None]).reshape(n_cb, hw, gpb * H)

    wb = jnp.concatenate([expand_b(bbt_re), expand_b(bbt_im)], axis=2).astype(MXU)
    wc = jnp.concatenate([expand_c(ssm_c_re[0]), -expand_c(ssm_c_im[0])], axis=1).astype(MXU)
    tab = jnp.stack([jnp.concatenate([a_re.reshape(n_cb, hw), a_im.reshape(n_cb, hw)], axis=1),
                     jnp.concatenate([as_re.reshape(n_cb, hw), as_im.reshape(n_cb, hw)], axis=1)], axis=1)
    tab = jnp.broadcast_to(tab[:, :, None, :], (n_cb, 2, SUBLANES, 2 * hw))
    dsk = ssm_d.reshape(n_cb, 1, LANES)
    lb = _small_call(_lb_fn, [hgrn_lb_logits], [((1, D), F32)], "hgrn_lower_bound")[0]

    z1 = _norm_call(h0, mix_norm_g + started_tok, tm, "mix_norm")
    ready = jnp.concatenate([t[(0,) * (t.ndim - 1)][0:1].astype(F32) for t in (z1, wb, wc, tab, lb, tgt)])
    ga_src, ga_land = _exchange_wait(ga, ready, False, "gather_a_wait")
    win_g = _place_own_call(ga_src, ga_land, False, me, "gather_a_own")[0]
    p = _mm_shard(z1, win_g, tm, "in_proj", False)
    p3 = p.reshape(B, L, p.shape[1])
    u_seg = _to_segments(p3[:, :, :W], seg)
    ya_seg, s_all = _s5_fwd_call(u_seg, wb, wc, tab, dsk, "s5_fwd")
    ya = _from_segments(ya_seg, seg, L).reshape(T, W)
    yb = _hgrn_fwd_call(p3, lb, hgrn_norm_g, n_heads, n_cb, "hgrn_fwd").reshape(T, D)
    gb_src, gb_land = _exchange_wait(gb, yb, False, "gather_b_wait")
    gathered = _place_own_call(gb_src, gb_land, False, me, "gather_b_own")
    wup_g = gathered[0]
    wglu_g, wsp_g, whp_g, wout_g = [g.reshape(D, D) for g in gathered[1:5]]
    wdn_g = gathered[5].reshape(N_DEV // 2, 2 * w_down.shape[1], D)
    yo, a_br = _glu_proj_call(ya, wglu_g, wsp_g, tm, "s5_glu_proj")
    col_ga = 5
    h1, mg, bm, z2 = _merge_call(yb, a_br, p, h0, whp_g, wout_g, ffn_norm_g, col_ga, tm, "merge")
    up = _mm_shard(z2, wup_g, tm, "up_proj", True)
    conv_a, conv_b_out, dh2, loss_part, dg3 = _ffn_fwd_call(up, cw_g, cb_g, wdn_g, h1, tgt, final_norm_g.reshape(1, D),
                                                            tm, tps, "ffn_out_loss")

    dua, dub, dwd, dcba, dcbb = _ffn_bwd_a_call(dh2, conv_a, conv_b_out, wdn_g, tm, "ffn_bwd_gate")
    dupa, dupb, dh1, dg2, dcwa, dcwb = _ffn_bwd_b_call(dua, dub, up, cw_g, wup_g, h1, ffn_norm_g, dh2, tm, tps,
                                                       "ffn_bwd_up")
    dwup = jnp.concatenate([_mm_tn(z2, dupa, N_DEV // 2, tm, "dw_up_a", True),
                            _mm_tn(z2, dupb, N_DEV // 2, tm, "dw_up_b", True)], axis=0)
    sh_rows = D // N_DEV
    sa = _exchange_start([dwup, dwd.reshape(N_DEV, w_down.shape[1], D)], True, "scatter_a_start")
    dmg, dwout = _lin_bwd(mg, dh1, wout_g + sa[4][0:1, 0:1].astype(MXU), tm, "out_proj_bwd")
    da_br, dbm, dga, dgb = _merge_bwd_call(dmg, a_br, bm, p, col_ga, tm, "merge_bwd")
    dyo, dwsp = _lin_bwd(yo, da_br, wsp_g, tm, "ssm_proj_bwd")
    dyb, dwhp = _lin_bwd(yb, dbm, whp_g, tm, "hgrn_proj_bwd")
    dya, dwglu = _glu_bwd_call(ya, dyo, wglu_g, tm, "s5_glu_bwd")
    sb = _exchange_start([t.reshape(N_DEV, sh_rows, D) for t in (dwglu, dwsp, dwhp, dwout)], True, "scatter_b_start")
    tok_b = sb[4][0:1, :]
    du_seg, dwb, dwc, dab, ddsk = _s5_bwd_call(u_seg, s_all, _to_segments(dya.reshape(B, L, W), seg), wb, wc, tab,
                                               dsk + tok_b[None], "s5_bwd")
    du = _from_segments(du_seg, seg, L)

    def diag_b(dw):
        t = (dw.reshape(n_cb, gpb, H, gpb, P) * eye[None, :, None, :, None]).sum(axis=1)
        return t.transpose(1, 0, 2, 3).reshape(H, G * P)

    def diag_c(dw):
        t = (dw.reshape(n_cb, gpb, P, gpb, H) * eye[None, :, None, :, None]).sum(axis=3)
        return t.transpose(0, 1, 3, 2).reshape(G, H, P)

    early_parts = [dab[:, 0, :hw].reshape(G, P), dab[:, 0, hw:].reshape(G, P), ddsk.reshape(1, D)]
    early = [_pack(early_parts), diag_b(dwb[:, :, :hw]), diag_b(dwb[:, :, hw:]),
             diag_c(dwc[:, :hw]).reshape(G * H, P), -diag_c(dwc[:, hw:]).reshape(G * H, P)]
    se = _exchange_start(early, False, "gather_s5_grads_start")
    dq, dfl, di, dog, dlb, dng = _hgrn_bwd_call(p3, dyb.reshape(B, L, D), lb, hgrn_norm_g + tok_b + se[4][0:1, :],
                                                n_heads, n_cb, "hgrn_bwd")
    dp = jnp.concatenate([du.reshape(T, W), dq.reshape(T, D), dfl.reshape(T, D), di.reshape(T, D),
                          dog.reshape(T, D), dga, dgb], axis=1)
    dwin = _mm_tn(z1, dp, N_DEV, tm, "dw_in", False)
    sc = _exchange_start([dwin.astype(WIRE)], True, "scatter_c_start")
    dh0, dg1 = _in_bwd_call(dp, win_g, h0, mix_norm_g + sc[4][0:1, 0:1], dh1, tm, "in_proj_bwd")
    dh0_3 = dh0.reshape(B, L, D)
    grad_x = dh0_3[:, N_META:]
    dmeta = _meta_grad_call(dh0_3, "meta_grad")

    late_parts = [dg1, dlb, dng, dg2, jnp.concatenate([dcba, dcbb], axis=0).reshape(1, N_DEV * F), dg3, loss_part]
    late_pack = _pack(late_parts)

    dcw = jnp.concatenate([dcwa, dcwb], axis=0)
    dmeta_s = dmeta.reshape(N_META, N_DEV, D // N_DEV).transpose(1, 0, 2)
    parts_d = _exchange_call([dmeta_s, dcw], True, "scatter_small_grads")
    late_all = _exchange_call([late_pack], False, "gather_small_grads")[0]
    early_all = _place_own_call(*_exchange_wait(se, late_all, False, "gather_s5_grads_wait"), False, me,
                                "gather_s5_grads_own")
    parts_a = _place_own_call(*_exchange_wait(sa, late_all, True, "scatter_a_wait"), True, me, "scatter_a_own")
    parts_b = _place_own_call(*_exchange_wait(sb, late_all, True, "scatter_b_wait"), True, me, "scatter_b_own")
    parts_c = _place_own_call(*_exchange_wait(sc, late_all, True, "scatter_c_wait"), True, me, "scatter_c_own")
    parts = [parts_c[0], parts_a[0], *parts_b, parts_a[1], parts_d[0], parts_d[1]]

    def sum8(*gathered):
        out = []
        for a in gathered:
            t = a[0]
            for s in range(1, N_DEV):
                t = t + a[s]
            out.append(t)
        return tuple(out)

    sums = _small_call(sum8, [*early_all, late_all], [(a.shape, F32) for a in (*early, late_pack)], "sum_small_grads")
    t_abr, t_abi, g_dsk = _unpack(sums[0], [a.shape for a in early_parts])
    t_bbr, t_bbi = sums[1], sums[2]
    g_cre, g_cim = sums[3].reshape(G, H, P), sums[4].reshape(G, H, P)
    g_g1, t_lb, g_ng, g_g2, g_cb, g_g3, loss_v = _unpack(sums[5], [a.shape for a in late_parts])

    def disc_b_bwd(cr, ci, br, bi, dbr, dbi):
        _, vjp = jax.vjp(_disc_b, cr, ci, br, bi)
        return vjp((dbr, dbi))

    t_cr, t_ci, g_btr, g_bti = _small_call(
        disc_b_bwd, [coef_re.reshape(1, G * P), coef_im.reshape(1, G * P), bt_re, bt_im, t_bbr, t_bbi],
        [((1, G * P), F32)] * 2 + [((H, G * P), F32)] * 2, "s5_input_matrix_bwd")

    def disc_a_bwd(lr_, li_, ldt_, dar, dai, dcr, dci):
        _, vjp = jax.vjp(_disc_a, lr_, li_, ldt_)
        return vjp((dar, dai, dcr, dci))

    g_lr, g_li, g_ldt = _small_call(
        disc_a_bwd, [lr, li, ldt, t_abr, t_abi, t_cr.reshape(G, P), t_ci.reshape(G, P)],
        [((G, P), F32)] * 2 + [((G, 1), F32)], "s5_discretise_bwd")

    def lb_bwd(logits, d):
        _, vjp = jax.vjp(_lb_fn, logits)
        return vjp(d)

    g_lbl = _small_call(lb_bwd, [hgrn_lb_logits, t_lb], [(hgrn_lb_logits.shape, F32)], "hgrn_lower_bound_bwd")[0]

    grads = dict(
        mix_norm_g=g_g1, ssm_lambda_re=g_lr[None], ssm_lambda_im=g_li[None], ssm_log_dt=g_ldt.reshape(1, G),
        ssm_b_re=g_btr.reshape(H, G, P).transpose(1, 2, 0)[None], ssm_b_im=g_bti.reshape(H, G, P).transpose(1, 2, 0)[None],
        ssm_c_re=g_cre[None], ssm_c_im=g_cim[None], ssm_d=g_dsk, hgrn_lb_logits=g_lbl, hgrn_norm_g=g_ng,
        ffn_norm_g=g_g2, conv_b=g_cb.reshape(1, N_DEV * F), final_norm_g=g_g3.reshape(D))
    loss = loss_v[0, 0]

    delta, new_m, new_v = {}, {}, {}
    sharded = [("w_in", parts[0], (D, n_in)), ("w_up", parts[1], (D, F)), ("ssm_w_glu", parts[2], (sh_rows, D)),
               ("w_ssm_proj", parts[3], (sh_rows, D)), ("w_hgrn_proj", parts[4], (sh_rows, D)),
               ("w_out", parts[5], (sh_rows, D)), ("w_down", parts[6], (w_down.shape[1], D)),
               ("meta_tokens", parts[7], (N_META, D // N_DEV)), ("conv_w", parts[8], (3, F))]
    for name, part, shp in sharded:
        full = args[name].shape
        g, d_, nm, nv = _adamw_shard_call(args[name].reshape(shp), part, args["m_" + name].reshape(shp),
                                          args["v_" + name].reshape(shp), "adamw_" + name)
        grads[name], delta[name], new_m[name], new_v[name] = [t.reshape(full) for t in (g, d_, nm, nv)]

    for n, shp in (("ssm_b_re", (G * P, H)), ("ssm_b_im", (G * P, H)), ("ssm_c_re", (G * H, P)), ("ssm_c_im", (G * H, P))):
        outs = _small_call(_adamw, [t.reshape(shp) for t in (args[n], grads[n], args["m_" + n], args["v_" + n])],
                           [(shp, F32)] * 3, "adamw_" + n)
        delta[n], new_m[n], new_v[n] = [o.reshape(args[n].shape) for o in outs]
    rep = ["mix_norm_g", "ssm_lambda_re", "ssm_lambda_im", "ssm_log_dt", "ssm_d", "hgrn_lb_logits", "hgrn_norm_g",
           "ffn_norm_g", "conv_b", "final_norm_g"]
    rep_shapes = [args[n].shape for n in rep]
    packs = [_pack([args[pre + n] for n in rep]) for pre in ("", "m_", "v_")]
    g_pack = _pack([grads[n] for n in rep])
    outs = _small_call(lambda w, g, m, v: _adamw(w, g, m, v), [packs[0], g_pack, packs[1], packs[2]],
                       [(g_pack.shape, F32)] * 3, "adamw_replicated")
    for n, d_, nm, nv in zip(rep, *[_unpack(o, rep_shapes) for o in outs]):
        delta[n], new_m[n], new_v[n] = d_, nm, nv

    names = ["meta_tokens", "mix_norm_g", "w_in", "ssm_lambda_re", "ssm_lambda_im", "ssm_log_dt", "ssm_b_re",
             "ssm_b_im", "ssm_c_re", "ssm_c_im", "ssm_d", "ssm_w_glu", "w_ssm_proj", "hgrn_lb_logits", "hgrn_norm_g",
             "w_hgrn_proj", "w_out", "ffn_norm_g", "w_up", "conv_w", "conv_b", "w_down", "final_norm_g"]
    return (loss, grad_x, *[grads[n] for n in names], *[delta[n] for n in names],
            *[new_m[n] for n in names], *[new_v[n] for n in names])
```

```python
import jax
import jax.numpy as jnp
from jax import lax
from jax.experimental import pallas as pl
from jax.experimental.pallas import tpu as pltpu

F32 = jnp.float32
MXU = jnp.bfloat16
ACT = jnp.bfloat16
WIRE = jnp.bfloat16
N_DEV = 8
N_META = 16
CHUNK = 16
EPS = 1e-6
ADAM_LR, ADAM_B1, ADAM_B2, ADAM_EPS, ADAM_WD, ADAM_STEP = 0.001, 0.9, 0.999, 1e-08, 0.01, 10
SUBLANES = 8
LANES = 128
ROW_TILE_CAP = 700
VMEM_LIMIT = 60 * 1024 * 1024


def _cparams(**kw):
    return pltpu.CompilerParams(vmem_limit_bytes=VMEM_LIMIT, **kw)


def _tile(n, cap):
    best = None
    for t in range(16, min(n, cap) + 1, 16):
        if n % t == 0:
            best = t
    assert best is not None, (n, cap)
    return best


def _dot(a, b):
    return lax.dot_general(a.astype(MXU), b.astype(MXU), (((1,), (0,)), ((), ())), preferred_element_type=F32)


def _dot_nt(a, b):
    return lax.dot_general(a.astype(MXU), b.astype(MXU), (((1,), (1,)), ((), ())), preferred_element_type=F32)


def _dot_tn(a, b):
    return lax.dot_general(a.astype(MXU), b.astype(MXU), (((0,), (0,)), ((), ())), preferred_element_type=F32)


def _rms(x, g):
    return x * lax.rsqrt(jnp.mean(x * x, axis=-1, keepdims=True) + EPS) * g


def _silu(x):
    return x * jax.nn.sigmoid(x)


def _small_call(fn, ins, out_shapes, name):
    n_in = len(ins)

    def body(*refs):
        outs = fn(*[r[...] for r in refs[:n_in]])
        outs = outs if isinstance(outs, (tuple, list)) else (outs,)
        for r, o in zip(refs[n_in:], outs):
            r[...] = o.astype(r.dtype)

    vm = pl.BlockSpec(memory_space=pltpu.VMEM)
    return pl.pallas_call(
        body, name=name, out_shape=tuple(jax.ShapeDtypeStruct(s, d) for s, d in out_shapes),
        in_specs=[vm] * n_in, out_specs=tuple([vm] * len(out_shapes)), compiler_params=_cparams())(*ins)


def _disc_a(lr, li, ldt):
    dt = jnp.exp(ldt)
    mag = jnp.exp(lr * dt)
    ab_re = mag * jnp.cos(li * dt)
    ab_im = mag * jnp.sin(li * dt)
    den = lr * lr + li * li
    nr = ab_re - 1.0
    coef_re = (nr * lr + ab_im * li) / den
    coef_im = (ab_im * lr - nr * li) / den
    return ab_re, ab_im, coef_re, coef_im


def _disc_a_power(n):
    def fn(lr, li, ldt):
        ab_re, ab_im, coef_re, coef_im = _disc_a(lr, li, ldt)
        pr, pi, sr, si, m = None, None, ab_re, ab_im, n
        while m:
            if m & 1:
                pr, pi = (sr, si) if pr is None else (pr * sr - pi * si, pr * si + pi * sr)
            m >>= 1
            if m:
                sr, si = sr * sr - si * si, 2.0 * sr * si
        return ab_re, ab_im, pr, pi, coef_re, coef_im
    return fn


def _disc_b(coef_re, coef_im, bt_re, bt_im):
    return coef_re * bt_re - coef_im * bt_im, coef_re * bt_im + coef_im * bt_re


def _lb_fn(logits):
    return jax.nn.softmax(logits, axis=0)[0:1]


def _adamw(w, g, m, v):
    m = ADAM_B1 * m + (1.0 - ADAM_B1) * g
    v = ADAM_B2 * v + (1.0 - ADAM_B2) * jnp.square(g)
    m_hat = m / (1.0 - ADAM_B1 ** ADAM_STEP)
    v_hat = v / (1.0 - ADAM_B2 ** ADAM_STEP)
    delta = -ADAM_LR * (m_hat / (jnp.sqrt(v_hat) + ADAM_EPS) + ADAM_WD * w)
    return delta, m, v


def _norm_call(h, g, tm, name):
    T, D = h.shape

    def body(h_ref, g_ref, z_ref):
        z_ref[...] = _rms(h_ref[...], g_ref[...]).astype(ACT)

    return pl.pallas_call(
        body, name=name, out_shape=jax.ShapeDtypeStruct((T, D), ACT), grid=(T // tm,),
        in_specs=[pl.BlockSpec((tm, D), lambda i: (i, 0)), pl.BlockSpec((1, D), lambda i: (0, 0))],
        out_specs=pl.BlockSpec((tm, D), lambda i: (i, 0)), compiler_params=_cparams())(h, g)


def _mm_shard(x, w, tm, name, major):
    T, K = x.shape
    S, _, N = w.shape

    def body(x_ref, w_ref, o_ref):
        o_ref[...] = _dot(x_ref[...], w_ref[...]).astype(o_ref.dtype)

    if major:
        out_shape = jax.ShapeDtypeStruct((S, T, N), ACT)
        out_spec = pl.BlockSpec((None, tm, N), lambda j, i: (j, i, 0))
    else:
        out_shape = jax.ShapeDtypeStruct((T, S * N), ACT)
        out_spec = pl.BlockSpec((tm, N), lambda j, i: (i, j))
    return pl.pallas_call(
        body, name=name, out_shape=out_shape, grid=(S, T // tm),
        in_specs=[pl.BlockSpec((tm, K), lambda j, i: (i, 0)), pl.BlockSpec((None, K, N), lambda j, i: (j, 0, 0))],
        out_specs=out_spec, compiler_params=_cparams())(x, w)


def _mm_tn(x, y, n_shards, tm, name, major):
    T, K = x.shape
    S = n_shards
    N = y.shape[-1] if major else y.shape[-1] // S

    def body(x_ref, y_ref, o_ref):
        @pl.when(pl.program_id(1) == 0)
        def _():
            o_ref[...] = jnp.zeros_like(o_ref)
        o_ref[...] += _dot_tn(x_ref[...], y_ref[...])

    y_spec = (pl.BlockSpec((None, tm, N), lambda j, i: (j, i, 0)) if major
              else pl.BlockSpec((tm, N), lambda j, i: (i, j)))
    return pl.pallas_call(
        body, name=name, out_shape=jax.ShapeDtypeStruct((S, K, N), F32), grid=(S, T // tm),
        in_specs=[pl.BlockSpec((tm, K), lambda j, i: (i, 0)), y_spec],
        out_specs=pl.BlockSpec((None, K, N), lambda j, i: (j, 0, 0)), compiler_params=_cparams())(x, y)


def _lin_bwd(x, dy, w, tm, name):
    T, K = x.shape
    N = dy.shape[1]

    def body(x_ref, dy_ref, w_ref, dx_ref, dw_ref):
        @pl.when(pl.program_id(0) == 0)
        def _():
            dw_ref[...] = jnp.zeros_like(dw_ref)
        dy = dy_ref[...]
        dx_ref[...] = _dot_nt(dy, w_ref[...]).astype(dx_ref.dtype)
        dw_ref[...] += _dot_tn(x_ref[...], dy)

    return pl.pallas_call(
        body, name=name,
        out_shape=(jax.ShapeDtypeStruct((T, K), ACT), jax.ShapeDtypeStruct((K, N), F32)), grid=(T // tm,),
        in_specs=[pl.BlockSpec((tm, K), lambda i: (i, 0)), pl.BlockSpec((tm, N), lambda i: (i, 0)),
                  pl.BlockSpec((K, N), lambda i: (0, 0))],
        out_specs=(pl.BlockSpec((tm, K), lambda i: (i, 0)), pl.BlockSpec((K, N), lambda i: (0, 0))),
        compiler_params=_cparams())(x, dy, w)


N_SEG = SUBLANES
CHAIN_STEPS = 8


def _chain_loop(n, step, init):
    per = min(CHAIN_STEPS, n)

    def trip(t, carry):
        for u in range(per):
            carry = step(t * per + u, carry)
        return carry

    carry = lax.fori_loop(0, n // per, trip, init)
    for i in range(n // per * per, n):
        carry = step(jnp.int32(i), carry)
    return carry


def _seg_len(L):
    return -(-L // (N_SEG * SUBLANES)) * SUBLANES


def _to_segments(a3, seg):
    b, length, c = a3.shape
    a = jnp.pad(a3, ((0, 0), (0, N_SEG * seg - length), (0, 0)))
    return a.reshape(b, N_SEG, seg, c).transpose(0, 2, 1, 3).reshape(b, N_SEG * seg, c)


def _from_segments(a3, seg, length):
    b, _, c = a3.shape
    return a3.reshape(b, seg, N_SEG, c).transpose(0, 2, 1, 3).reshape(b, N_SEG * seg, c)[:, :length]


def _seg_scan(x_ref, tab_ref, n_slabs, reverse):
    hw = x_ref.shape[1] // 2
    sign = -1.0 if reverse else 1.0
    ar, ai = tab_ref[0][:, :hw], sign * tab_ref[0][:, hw:]
    br, bi = tab_ref[1][:, :hw], sign * tab_ref[1][:, hw:]

    def slab(k):
        kk = (n_slabs - 1 - k) if reverse else k
        return pl.ds(pl.multiple_of(kk * SUBLANES, SUBLANES), SUBLANES)

    def horner(k, carry):
        cr, ci = carry
        x = x_ref[slab(k), :]
        return ar * cr - ai * ci + x[:, :hw], ar * ci + ai * cr + x[:, hw:]

    z = jnp.zeros((SUBLANES, hw), F32)
    fr, fi = _chain_loop(n_slabs, horner, (z, z))

    row = lax.broadcasted_iota(jnp.int32, (SUBLANES, hw), 0)
    edge = (row == SUBLANES - 1) if reverse else (row == 0)
    shift = SUBLANES - 1 if reverse else 1
    sr, si = z, z
    for _ in range(N_SEG - 1):
        er, ei = fr + br * sr - bi * si, fi + br * si + bi * sr
        sr = jnp.where(edge, 0.0, pltpu.roll(er, shift, 0))
        si = jnp.where(edge, 0.0, pltpu.roll(ei, shift, 0))

    def scan(k, carry):
        cr, ci = carry
        rows = slab(k)
        x = x_ref[rows, :]
        nr, ni = ar * cr - ai * ci + x[:, :hw], ar * ci + ai * cr + x[:, hw:]
        x_ref[rows, 0:hw] = nr
        x_ref[rows, hw:2 * hw] = ni
        return nr, ni

    _chain_loop(n_slabs, scan, (sr, si))


def _s5_fwd_call(p3, wb, wc, tab_f, dsk, name):
    B, L, _ = p3.shape
    n_cb, cw, sw = wb.shape

    def body(u_ref, wb_ref, wc_ref, tab_ref, d_ref, ya_ref, so_ref, s_ref):
        u = u_ref[...]
        s_ref[...] = _dot(u, wb_ref[...])
        _seg_scan(s_ref, tab_ref, L // SUBLANES, False)
        s = s_ref[...].astype(MXU)
        so_ref[...] = s
        y = _dot(s, wc_ref[...]) + d_ref[...] * u.astype(F32)
        ya_ref[...] = jax.nn.gelu(y).astype(ACT)

    return pl.pallas_call(
        body, name=name,
        out_shape=(jax.ShapeDtypeStruct((B, L, n_cb * cw), ACT), jax.ShapeDtypeStruct((B, n_cb, L, sw), MXU)),
        grid=(B, n_cb),
        in_specs=[pl.BlockSpec((None, L, cw), lambda b, c: (b, 0, c)),
                  pl.BlockSpec((None, cw, sw), lambda b, c: (c, 0, 0)),
                  pl.BlockSpec((None, sw, cw), lambda b, c: (c, 0, 0)),
                  pl.BlockSpec((None, 2, SUBLANES, sw), lambda b, c: (c, 0, 0, 0)),
                  pl.BlockSpec((None, 1, cw), lambda b, c: (c, 0, 0))],
        out_specs=(pl.BlockSpec((None, L, cw), lambda b, c: (b, 0, c)),
                   pl.BlockSpec((None, None, L, sw), lambda b, c: (b, c, 0, 0))),
        scratch_shapes=[pltpu.VMEM((L, sw), F32)], compiler_params=_cparams())(p3, wb, wc, tab_f, dsk)


def _s5_bwd_call(p3, s_all, dya, wb, wc, tab_r, dsk, name):
    B, L, _ = p3.shape
    n_cb, cw, sw = wb.shape
    hw = sw // 2
    n_slabs = L // SUBLANES

    def body(u_ref, si_ref, dya_ref, wb_ref, wc_ref, tr_ref, d_ref,
             du_ref, dwb_ref, dwc_ref, da_ref, dd_ref, s_ref, l_ref):
        @pl.when(pl.program_id(1) == 0)
        def _():
            dwb_ref[...] = jnp.zeros_like(dwb_ref)
            dwc_ref[...] = jnp.zeros_like(dwc_ref)
            da_ref[...] = jnp.zeros_like(da_ref)
            dd_ref[...] = jnp.zeros_like(dd_ref)

        u = u_ref[...]
        uf = u.astype(F32)
        s_in = si_ref[...]
        s_ref[...] = s_in.astype(F32)
        y = _dot(s_in, wc_ref[...]) + d_ref[...] * uf
        _, gelu_vjp = jax.vjp(jax.nn.gelu, y)
        dy = gelu_vjp(dya_ref[...].astype(F32))[0]
        dd_ref[...] += jnp.sum(dy * uf, axis=0, keepdims=True)
        l_ref[...] = _dot_nt(dy, wc_ref[...])
        _seg_scan(l_ref, tr_ref, n_slabs, True)
        du_ref[...] = (_dot_nt(l_ref[...], wb_ref[...]) + d_ref[...] * dy).astype(ACT)
        dwb_ref[...] += _dot_tn(u, l_ref[...])
        dwc_ref[...] += _dot_tn(s_in, dy)

        row = lax.broadcasted_iota(jnp.int32, (SUBLANES, hw), 0)
        last = s_ref[pl.ds((n_slabs - 1) * SUBLANES, SUBLANES), :]
        p0r = jnp.where(row == 0, 0.0, pltpu.roll(last[:, :hw], 1, 0))
        p0i = jnp.where(row == 0, 0.0, pltpu.roll(last[:, hw:], 1, 0))

        def step(k, carry):
            qr, qi, accr, acci = carry
            r0 = pl.multiple_of(k * SUBLANES, SUBLANES)
            s = s_ref[pl.ds(r0, SUBLANES), :]
            lam = l_ref[pl.ds(r0, SUBLANES), :]
            lr, li = lam[:, :hw], lam[:, hw:]
            accr = accr + lr * qr + li * qi
            acci = acci + li * qr - lr * qi
            return s[:, :hw], s[:, hw:], accr, acci

        z8 = jnp.zeros((SUBLANES, hw), F32)
        _, _, accr, acci = _chain_loop(n_slabs, step, (p0r, p0i, z8, z8))
        da_ref[...] += jnp.concatenate([jnp.sum(accr, axis=0, keepdims=True),
                                        jnp.sum(acci, axis=0, keepdims=True)], axis=1)

    W = n_cb * cw
    return pl.pallas_call(
        body, name=name,
        out_shape=(jax.ShapeDtypeStruct((B, L, W), ACT), jax.ShapeDtypeStruct((n_cb, cw, sw), F32),
                   jax.ShapeDtypeStruct((n_cb, sw, cw), F32), jax.ShapeDtypeStruct((n_cb, 1, sw), F32),
                   jax.ShapeDtypeStruct((n_cb, 1, cw), F32)),
        grid=(n_cb, B),
        in_specs=[pl.BlockSpec((None, L, cw), lambda c, b: (b, 0, c)),
                  pl.BlockSpec((None, None, L, sw), lambda c, b: (b, c, 0, 0)),
                  pl.BlockSpec((None, L, cw), lambda c, b: (b, 0, c)),
                  pl.BlockSpec((None, cw, sw), lambda c, b: (c, 0, 0)),
                  pl.BlockSpec((None, sw, cw), lambda c, b: (c, 0, 0)),
                  pl.BlockSpec((None, 2, SUBLANES, sw), lambda c, b: (c, 0, 0, 0)),
                  pl.BlockSpec((None, 1, cw), lambda c, b: (c, 0, 0))],
        out_specs=(pl.BlockSpec((None, L, cw), lambda c, b: (b, 0, c)),
                   pl.BlockSpec((None, cw, sw), lambda c, b: (c, 0, 0)),
                   pl.BlockSpec((None, sw, cw), lambda c, b: (c, 0, 0)),
                   pl.BlockSpec((None, 1, sw), lambda c, b: (c, 0, 0)),
                   pl.BlockSpec((None, 1, cw), lambda c, b: (c, 0, 0))),
        scratch_shapes=[pltpu.VMEM((L, sw), F32), pltpu.VMEM((L, sw), F32)],
        compiler_params=_cparams())(p3, s_all, dya, wb, wc, tab_r, dsk)


def _glu_proj_call(ya, wglu, wproj, tm, name):
    T, W = ya.shape
    D = wproj.shape[1]

    def body(ya_ref, wg_ref, wp_ref, yo_ref, a_ref):
        ya = ya_ref[...]
        yo = ya.astype(F32) * jax.nn.sigmoid(_dot(ya, wg_ref[...]))
        yo_ref[...] = yo.astype(ACT)
        a_ref[...] = _dot(yo, wp_ref[...]).astype(ACT)

    return pl.pallas_call(
        body, name=name, out_shape=(jax.ShapeDtypeStruct((T, W), ACT), jax.ShapeDtypeStruct((T, D), ACT)),
        grid=(T // tm,),
        in_specs=[pl.BlockSpec((tm, W), lambda i: (i, 0)), pl.BlockSpec((W, W), lambda i: (0, 0)),
                  pl.BlockSpec((W, D), lambda i: (0, 0))],
        out_specs=(pl.BlockSpec((tm, W), lambda i: (i, 0)), pl.BlockSpec((tm, D), lambda i: (i, 0))),
        compiler_params=_cparams())(ya, wglu, wproj)


def _glu_bwd_call(ya, dyo, wglu, tm, name):
    T, W = ya.shape

    def body(ya_ref, dyo_ref, wg_ref, dya_ref, dwg_ref):
        @pl.when(pl.program_id(0) == 0)
        def _():
            dwg_ref[...] = jnp.zeros_like(dwg_ref)
        ya = ya_ref[...]
        yaf = ya.astype(F32)
        dyo = dyo_ref[...].astype(F32)
        sg = jax.nn.sigmoid(_dot(ya, wg_ref[...]))
        dt = dyo * yaf * sg * (1.0 - sg)
        dya_ref[...] = (dyo * sg + _dot_nt(dt, wg_ref[...])).astype(ACT)
        dwg_ref[...] += _dot_tn(ya, dt)

    return pl.pallas_call(
        body, name=name, out_shape=(jax.ShapeDtypeStruct((T, W), ACT), jax.ShapeDtypeStruct((W, W), F32)),
        grid=(T // tm,),
        in_specs=[pl.BlockSpec((tm, W), lambda i: (i, 0)), pl.BlockSpec((tm, W), lambda i: (i, 0)),
                  pl.BlockSpec((W, W), lambda i: (0, 0))],
        out_specs=(pl.BlockSpec((tm, W), lambda i: (i, 0)), pl.BlockSpec((W, W), lambda i: (0, 0))),
        compiler_params=_cparams())(ya, dyo, wglu)


PAD = 16


def _chunk_cumsums(x, pad_ref, L):
    row = lax.broadcasted_iota(jnp.int32, x.shape, 0) % CHUNK
    zeros = jnp.zeros((PAD, x.shape[1]), F32)
    pad_ref[0:PAD, :] = zeros
    pad_ref[PAD + L:2 * PAD + L, :] = zeros
    c = x
    r = x
    d = 1
    while d < CHUNK:
        pad_ref[PAD:PAD + L, :] = c
        c = c + jnp.where(row >= d, pad_ref[PAD - d:PAD - d + L, :], 0.0)
        pad_ref[PAD:PAD + L, :] = r
        r = r + jnp.where(row + d < CHUNK, pad_ref[PAD + d:PAD + d + L, :], 0.0)
        d *= 2
    return c, r - x


def _hgrn_prep(q_ref, fl_ref, lb_ref, pad_ref, r0, n):
    rows = pl.ds(r0, n)
    lb = lb_ref[...]
    sig = jax.nn.sigmoid(fl_ref[rows, :].astype(F32))
    f = lb + (1.0 - lb) * sig
    k = 1.0 - f
    c, rc = _chunk_cumsums(jnp.log(f), pad_ref, n)
    e_in, e_inv, e_out = jnp.exp(c), jnp.exp(-c), jnp.exp(rc)
    q = q_ref[rows, :].astype(F32)
    return dict(sig=sig, f=f, k=k, q=q, e_in=e_in, e_inv=e_inv, e_out=e_out, dec=jnp.exp(c + rc))


def _for_row_blocks(L, fn):
    full = L // GROUP
    if full:
        def step(g, carry):
            fn(pl.multiple_of(g * GROUP, GROUP), GROUP)
            return carry
        lax.fori_loop(0, full, step, 0)
    if L % GROUP:
        fn(full * GROUP, L % GROUP)


def _chunk_mask(rb):
    r = lax.broadcasted_iota(jnp.int32, (rb, rb), 0)
    c = lax.broadcasted_iota(jnp.int32, (rb, rb), 1)
    return (r // CHUNK == c // CHUNK) & (c <= r)


def _hg_out(o, og, g):
    on = o * lax.rsqrt(jnp.mean(o * o, axis=-1, keepdims=True) + EPS) * g
    return on * _silu(og)


def _hgrn_specs(L, hd, col_q, n_heads, order):
    def spec(sec):
        return pl.BlockSpec((None, L, hd), lambda *g: (order(*g)[0], 0, col_q + sec * n_heads + order(*g)[1]))
    return [spec(0), spec(1), spec(2), spec(3)]


GROUP = 128
CPG = GROUP // CHUNK


def _expand(x):
    xf = x.astype(F32)
    chunk = lax.broadcasted_iota(jnp.int32, xf.shape, 0) // CHUNK
    return jnp.concatenate([jnp.where(chunk == j, xf, 0.0) for j in range(CPG)], axis=1)


def _fill_tail(refs_fills, L):
    for ref, fill in refs_fills:
        if ref.shape[0] > L:
            ref[L:ref.shape[0], :] = jnp.full((ref.shape[0] - L, ref.shape[1]), fill, ref.dtype)


GROUP_UNROLL = 17


def _hgrn_forward_core(q_ref, fl_ref, v_ref, lb_ref, pad_ref, qin_ref, kin_ref, kout_ref, vp_ref, dec_ref, o_ref,
                       s_ref, a_ref, L, keep=()):
    hd = qin_ref.shape[1]
    n_groups = qin_ref.shape[0] // GROUP

    def prep(r0, n):
        pp = _hgrn_prep(q_ref, fl_ref, lb_ref, pad_ref, r0, n)
        rows = pl.ds(r0, n)
        for key, ref in keep:
            ref[rows, :] = pp[key]
        qin_ref[rows, :] = (pp["q"] * pp["e_in"]).astype(MXU)
        kin_ref[rows, :] = (pp["k"] * pp["e_inv"]).astype(MXU)
        kout_ref[rows, :] = (pp["k"] * pp["e_out"]).astype(MXU)
        vp_ref[rows, :] = v_ref[rows, :].astype(MXU)
        dec_ref[rows, :] = pp["dec"]

    _for_row_blocks(L, prep)
    _fill_tail(((qin_ref, 0.0), (kin_ref, 0.0), (kout_ref, 0.0), (vp_ref, 0.0), (dec_ref, 1.0)), L)
    mask = _chunk_mask(GROUP)

    def scores(g, carry):
        rows = pl.ds(pl.multiple_of(g * GROUP, GROUP), GROUP)
        a_ref[rows, :] = jnp.where(mask, _dot_nt(qin_ref[rows, :], kin_ref[rows, :]), 0.0).astype(MXU)
        return carry

    lax.fori_loop(0, n_groups, scores, 0, unroll=GROUP_UNROLL)

    def intra(g, carry):
        rows = pl.ds(pl.multiple_of(g * GROUP, GROUP), GROUP)
        o_ref[rows, :] = _dot(a_ref[rows, :], vp_ref[rows, :])
        kv = _dot_tn(vp_ref[rows, :], _expand(kout_ref[rows, :]))
        for j in range(CPG):
            s_ref[g * CPG + j] = kv[:, j * hd:(j + 1) * hd]
        return carry

    lax.fori_loop(0, n_groups, intra, 0, unroll=GROUP_UNROLL)

    def rec(n, st):
        kv = s_ref[n]
        s_ref[n] = st
        dec = dec_ref[pl.ds(pl.multiple_of(n * CHUNK, CHUNK), SUBLANES), :][0:1]
        return st * dec + kv

    _chain_loop(L // CHUNK, rec, jnp.zeros((hd, hd), F32))

    def inter(g, carry):
        rows = pl.ds(pl.multiple_of(g * GROUP, GROUP), GROUP)
        scat = jnp.concatenate([s_ref[g * CPG + j] for j in range(CPG)], axis=1)
        o_ref[rows, :] += _dot_nt(_expand(qin_ref[rows, :]), scat)
        return carry

    lax.fori_loop(0, n_groups, inter, 0, unroll=GROUP_UNROLL)


def _hgrn_scratch(L, hd):
    lp = -(-L // GROUP) * GROUP
    return lp, [pltpu.VMEM((GROUP + 2 * PAD, hd), F32), pltpu.VMEM((lp, hd), MXU), pltpu.VMEM((lp, hd), MXU),
                pltpu.VMEM((lp, hd), MXU), pltpu.VMEM((lp, hd), MXU), pltpu.VMEM((lp, hd), F32),
                pltpu.VMEM((lp, hd), F32), pltpu.VMEM((lp // CHUNK, hd, hd), F32), pltpu.VMEM((lp, GROUP), MXU)]


def _hgrn_fwd_call(p3, lb, ng, n_heads, col_q, name):
    B, L, _ = p3.shape
    hd = ng.shape[1]
    _, scratch = _hgrn_scratch(L, hd)

    def body(q_ref, fl_ref, v_ref, og_ref, lb_ref, ng_ref, yb_ref,
             pad_ref, qin_ref, kin_ref, kout_ref, vp_ref, dec_ref, o_ref, s_ref, a_ref):
        _hgrn_forward_core(q_ref, fl_ref, v_ref, lb_ref, pad_ref, qin_ref, kin_ref, kout_ref, vp_ref, dec_ref,
                           o_ref, s_ref, a_ref, L)

        def out(r0, n):
            rows = pl.ds(r0, n)
            yb_ref[rows, :] = _hg_out(o_ref[rows, :], og_ref[rows, :].astype(F32), ng_ref[...]).astype(ACT)

        _for_row_blocks(L, out)

    order = lambda b, h: (b, h)
    return pl.pallas_call(
        body, name=name, out_shape=jax.ShapeDtypeStruct((B, L, n_heads * hd), ACT), grid=(B, n_heads),
        in_specs=_hgrn_specs(L, hd, col_q, n_heads, order) + [
            pl.BlockSpec((1, hd), lambda b, h: (0, h)), pl.BlockSpec((1, hd), lambda b, h: (0, 0))],
        out_specs=pl.BlockSpec((None, L, hd), lambda b, h: (b, 0, h)),
        scratch_shapes=scratch, compiler_params=_cparams())(p3, p3, p3, p3, lb, ng)


def _hgrn_bwd_call(p3, dyb, lb, ng, n_heads, col_q, name):
    B, L, _ = p3.shape
    hd = ng.shape[1]
    n_chunks = L // CHUNK
    lp, scratch = _hgrn_scratch(L, hd)
    n_groups = lp // GROUP

    def body(q_ref, fl_ref, v_ref, og_ref, dyb_ref, lb_ref, ng_ref,
             dq_ref, dfl_ref, dv_ref, dog_ref, dlb_ref, dng_ref,
             pad_ref, qin_ref, kin_ref, kout_ref, vp_ref, dec_ref, o_ref, s_ref, a_ref,
             do_ref, ds_ref, dqi_ref, dki_ref, dko_ref, dvv_ref, dct_ref,
             sig_ref, f_ref, ein_ref, einv_ref, eout_ref, da_ref):
        @pl.when(pl.program_id(1) == 0)
        def _():
            dlb_ref[...] = jnp.zeros_like(dlb_ref)

        @pl.when((pl.program_id(0) == 0) & (pl.program_id(1) == 0))
        def _():
            dng_ref[...] = jnp.zeros_like(dng_ref)

        _hgrn_forward_core(q_ref, fl_ref, v_ref, lb_ref, pad_ref, qin_ref, kin_ref, kout_ref, vp_ref, dec_ref,
                           o_ref, s_ref, a_ref, L, keep=(("sig", sig_ref), ("f", f_ref), ("e_in", ein_ref),
                                                  ("e_inv", einv_ref), ("e_out", eout_ref)))

        def out_bwd(r0, n):
            rows = pl.ds(r0, n)
            _, out_vjp = jax.vjp(_hg_out, o_ref[rows, :], og_ref[rows, :].astype(F32), ng_ref[...])
            d_o, d_og, d_ng = out_vjp(dyb_ref[rows, :].astype(F32))
            dog_ref[rows, :] = d_og.astype(ACT)
            dng_ref[...] += d_ng
            do_ref[rows, :] = d_o.astype(MXU)

        _for_row_blocks(L, out_bwd)
        _fill_tail(((do_ref, 0.0),), L)
        mask = _chunk_mask(GROUP)

        def score_grads(g, carry):
            rows = pl.ds(pl.multiple_of(g * GROUP, GROUP), GROUP)
            da_ref[rows, :] = jnp.where(mask, _dot_nt(do_ref[rows, :], vp_ref[rows, :]), 0.0).astype(MXU)
            return carry

        lax.fori_loop(0, n_groups, score_grads, 0, unroll=GROUP_UNROLL)

        def grads_a(g, carry):
            rows = pl.ds(pl.multiple_of(g * GROUP, GROUP), GROUP)
            qi, ki, do, da = qin_ref[rows, :], kin_ref[rows, :], do_ref[rows, :], da_ref[rows, :]
            sstack = s_ref[pl.ds(g * CPG, CPG)].reshape(CPG * hd, hd)
            dqi_ref[rows, :] = _dot(da, ki) + _dot(_expand(do), sstack)
            dki_ref[rows, :] = _dot_tn(da, qi)
            dvv_ref[rows, :] = _dot_tn(a_ref[rows, :], do)
            x = _dot_tn(do, _expand(qi))
            for j in range(CPG):
                ds_ref[g * CPG + j] = x[:, j * hd:(j + 1) * hd]
            return carry

        lax.fori_loop(0, n_groups, grads_a, 0, unroll=GROUP_UNROLL)

        def rec_bwd(k, dst):
            n = n_chunks - 1 - k
            r0 = pl.multiple_of(n * CHUNK, CHUNK)
            x = ds_ref[n]
            ds_ref[n] = dst
            dec = dec_ref[pl.ds(r0, SUBLANES), :][0:1]
            return dst * dec + x

        _chain_loop(n_chunks, rec_bwd, jnp.zeros((hd, hd), F32))

        def grads_b(g, carry):
            r0 = pl.multiple_of(g * GROUP, GROUP)
            rows = pl.ds(r0, GROUP)
            ds = [ds_ref[g * CPG + j] for j in range(CPG)]
            dscat = jnp.concatenate(ds, axis=1)
            dvv_ref[rows, :] += _dot_nt(_expand(kout_ref[rows, :]), dscat)
            dstack = ds_ref[pl.ds(g * CPG, CPG)].reshape(CPG * hd, hd)
            dko_ref[rows, :] = _dot(_expand(vp_ref[rows, :]), dstack)
            for j in range(CPG):
                dec = dec_ref[pl.ds(r0 + j * CHUNK, SUBLANES), :][0:1]
                ddec = dec * jnp.sum(ds[j] * s_ref[g * CPG + j], axis=0, keepdims=True)
                dct_ref[pl.ds(r0 + j * CHUNK, CHUNK), :] = jnp.broadcast_to(ddec, (CHUNK, hd))
            return carry

        lax.fori_loop(0, n_groups, grads_b, 0, unroll=GROUP_UNROLL)

        def finish(r0, n):
            rows = pl.ds(r0, n)
            sig, f, e_in, e_inv, e_out = [r[rows, :] for r in (sig_ref, f_ref, ein_ref, einv_ref, eout_ref)]
            q, k = q_ref[rows, :].astype(F32), 1.0 - f
            dqi, dki, dko = dqi_ref[rows, :], dki_ref[rows, :], dko_ref[rows, :]
            dq = dqi * e_in
            dk = dki * e_inv + dko * e_out
            dq_ref[rows, :] = dq.astype(ACT)
            dv_ref[rows, :] = dvv_ref[rows, :].astype(ACT)
            t_out = k * e_out * dko
            dc = q * dq - k * e_inv * dki - t_out
            _, dc_later = _chunk_cumsums(dc, pad_ref, n)
            t_incl, t_later = _chunk_cumsums(t_out, pad_ref, n)
            dlogf = dc + dc_later + t_incl + t_later + dct_ref[rows, :]
            df = dlogf / f - dk
            dfl_ref[rows, :] = (df * (1.0 - lb_ref[...]) * sig * (1.0 - sig)).astype(ACT)
            dlb_ref[...] += jnp.sum(df * (1.0 - sig), axis=0, keepdims=True)

        _for_row_blocks(L, finish)

    order = lambda h, b: (b, h)
    W = n_heads * hd
    act_out = jax.ShapeDtypeStruct((B, L, W), ACT)
    blk_out = pl.BlockSpec((None, L, hd), lambda h, b: (b, 0, h))
    return pl.pallas_call(
        body, name=name,
        out_shape=(act_out, act_out, act_out, act_out, jax.ShapeDtypeStruct((1, W), F32),
                   jax.ShapeDtypeStruct((1, hd), F32)),
        grid=(n_heads, B),
        in_specs=_hgrn_specs(L, hd, col_q, n_heads, order) + [
            pl.BlockSpec((None, L, hd), lambda h, b: (b, 0, h)),
            pl.BlockSpec((1, hd), lambda h, b: (0, h)), pl.BlockSpec((1, hd), lambda h, b: (0, 0))],
        out_specs=(blk_out, blk_out, blk_out, blk_out, pl.BlockSpec((1, hd), lambda h, b: (0, h)),
                   pl.BlockSpec((1, hd), lambda h, b: (0, 0))),
        scratch_shapes=scratch + [
            pltpu.VMEM((lp, hd), MXU), pltpu.VMEM((lp // CHUNK, hd, hd), F32)] + [pltpu.VMEM((lp, hd), F32)] * 10 + [
            pltpu.VMEM((lp, GROUP), MXU)],
        compiler_params=_cparams())(p3, p3, p3, p3, dyb, lb, ng)


def _merge_fn(a, bm, ga, gb):
    return jax.nn.sigmoid(ga) * a + jax.nn.sigmoid(gb) * bm


def _merge_call(yb, a, p, h0, whp, wout, g2, col_ga, tm, name):
    T, D = h0.shape

    def body(yb_ref, a_ref, ga_ref, gb_ref, h0_ref, whp_ref, wout_ref, g2_ref, h1_ref, mg_ref, bm_ref, z2_ref):
        bm = _dot(yb_ref[...], whp_ref[...])
        mg = _merge_fn(a_ref[...].astype(F32), bm, ga_ref[...].astype(F32), gb_ref[...].astype(F32))
        h1 = h0_ref[...] + _dot(mg, wout_ref[...])
        h1_ref[...] = h1
        mg_ref[...] = mg.astype(ACT)
        bm_ref[...] = bm.astype(ACT)
        z2_ref[...] = _rms(h1, g2_ref[...]).astype(ACT)

    tile = pl.BlockSpec((tm, D), lambda i: (i, 0))
    full = pl.BlockSpec((D, D), lambda i: (0, 0))
    act = jax.ShapeDtypeStruct((T, D), ACT)
    return pl.pallas_call(
        body, name=name, out_shape=(jax.ShapeDtypeStruct((T, D), F32), act, act, act), grid=(T // tm,),
        in_specs=[tile, tile, pl.BlockSpec((tm, D), lambda i: (i, col_ga)),
                  pl.BlockSpec((tm, D), lambda i: (i, col_ga + 1)), tile, full, full,
                  pl.BlockSpec((1, D), lambda i: (0, 0))],
        out_specs=(tile, tile, tile, tile), compiler_params=_cparams())(yb, a, p, p, h0, whp, wout, g2)


def _out_merge_bwd_call(mg, dh1, wout, a, bm, p, col_ga, tm, name):
    T, D = dh1.shape

    def body(mg_ref, dh1_ref, w_ref, a_ref, bm_ref, ga_ref, gb_ref, da_ref, dbm_ref, dga_ref, dgb_ref, dw_ref):
        @pl.when(pl.program_id(0) == 0)
        def _():
            dw_ref[...] = jnp.zeros_like(dw_ref)
        dy = dh1_ref[...]
        dw_ref[...] += _dot_tn(mg_ref[...], dy)
        dmg = _dot_nt(dy, w_ref[...])
        args = [r[...].astype(F32) for r in (a_ref, bm_ref, ga_ref, gb_ref)]
        _, vjp = jax.vjp(_merge_fn, *args)
        for r, o in zip((da_ref, dbm_ref, dga_ref, dgb_ref), vjp(dmg)):
            r[...] = o.astype(ACT)

    tile = pl.BlockSpec((tm, D), lambda i: (i, 0))
    full = pl.BlockSpec((D, D), lambda i: (0, 0))
    act = jax.ShapeDtypeStruct((T, D), ACT)
    return pl.pallas_call(
        body, name=name, out_shape=(act, act, act, act, jax.ShapeDtypeStruct((D, D), F32)), grid=(T // tm,),
        in_specs=[tile, tile, full, tile, tile, pl.BlockSpec((tm, D), lambda i: (i, col_ga)),
                  pl.BlockSpec((tm, D), lambda i: (i, col_ga + 1))],
        out_specs=(tile, tile, tile, tile, full), compiler_params=_cparams())(mg, dh1, wout, a, bm, p, p)


def _conv_taps(x_ref, halo_ref, ext_ref, edge, tm, before):
    halo = jnp.where(edge, 0.0, halo_ref[...].astype(F32))
    x = x_ref[...].astype(F32)
    if before:
        ext_ref[0:PAD, :] = halo
        ext_ref[PAD:PAD + tm, :] = x
        return [ext_ref[PAD - 2 + k:PAD - 2 + k + tm, :] for k in range(3)]
    ext_ref[0:tm, :] = x
    ext_ref[tm:tm + PAD, :] = halo
    return [ext_ref[k:k + tm, :] for k in range(3)]


def _conv(taps, cw, cb):
    return cb + cw[0:1] * taps[0] + cw[1:2] * taps[1] + cw[2:3] * taps[2]


def _ffn_pair_specs(tm, F, T, n_pairs, order, before):
    hb = tm // PAD
    last = T // PAD - 1

    def halo_row(i):
        return jnp.maximum(i * hb - 1, 0) if before else jnp.minimum((i + 1) * hb, last)

    specs = []
    for off in (0, n_pairs):
        specs.append(pl.BlockSpec((None, tm, F), lambda *g, off=off: (order(*g)[1] + off, order(*g)[0], 0)))
        specs.append(pl.BlockSpec((None, PAD, F), lambda *g, off=off: (order(*g)[1] + off, halo_row(order(*g)[0]), 0)))
    return specs


def _ffn_fwd_call(up, cw, cb, wd, h1, tgt, g3, tm, tps, name):
    S, T, F = up.shape
    n_pairs = S // 2
    D = h1.shape[1]

    def body(ua_ref, ha_ref, ub_ref, hb_ref, cwa_ref, cwb_ref, cba_ref, cbb_ref, wd_ref, h1_ref, tgt_ref, g3_ref,
             ca_ref, cb_ref, dh2_ref, loss_ref, dg3_ref, acc_ref, ext_ref):
        i, j = pl.program_id(0), pl.program_id(1)
        edge = (i % tps) == 0
        ua = _conv(_conv_taps(ua_ref, ha_ref, ext_ref, edge, tm, True), cwa_ref[...], cba_ref[...])
        ub = _conv(_conv_taps(ub_ref, hb_ref, ext_ref, edge, tm, True), cwb_ref[...], cbb_ref[...])
        ca_ref[...] = ua.astype(ACT)
        cb_ref[...] = ub.astype(ACT)
        contrib = _dot(_silu(ua) * ub, wd_ref[...])

        @pl.when(j == 0)
        def _():
            acc_ref[...] = h1_ref[...] + contrib

        @pl.when(j > 0)
        def _():
            acc_ref[...] += contrib

        @pl.when((i == 0) & (j == 0))
        def _():
            loss_ref[...] = jnp.zeros_like(loss_ref)
            dg3_ref[...] = jnp.zeros_like(dg3_ref)

        @pl.when(j == n_pairs - 1)
        def _():
            row = lax.broadcasted_iota(jnp.int32, (tm, 1), 0) + (i % tps) * tm
            valid = row >= N_META
            tgt = tgt_ref[...]

            def loss_fn(h2, g):
                err = _rms(h2, g) - tgt
                return 0.5 * jnp.sum(jnp.where(valid, err * err, 0.0)) / D

            loss, vjp = jax.vjp(loss_fn, acc_ref[...], g3_ref[...])
            dh2, dg3 = vjp(jnp.ones((), F32))
            dh2_ref[...] = dh2
            loss_ref[...] += loss
            dg3_ref[...] += dg3

    order = lambda i, j: (i, j)
    tile = pl.BlockSpec((tm, D), lambda i, j: (i, 0))
    vec = pl.BlockSpec((1, D), lambda i, j: (0, 0))
    return pl.pallas_call(
        body, name=name,
        out_shape=(jax.ShapeDtypeStruct((n_pairs, T, F), ACT), jax.ShapeDtypeStruct((n_pairs, T, F), ACT),
                   jax.ShapeDtypeStruct((T, D), F32), jax.ShapeDtypeStruct((1, LANES), F32),
                   jax.ShapeDtypeStruct((1, D), F32)),
        grid=(T // tm, n_pairs),
        in_specs=_ffn_pair_specs(tm, F, T, n_pairs, order, True) + [
            pl.BlockSpec((None, 3, F), lambda i, j: (j, 0, 0)), pl.BlockSpec((None, 3, F), lambda i, j: (j + n_pairs, 0, 0)),
            pl.BlockSpec((None, 1, F), lambda i, j: (j, 0, 0)), pl.BlockSpec((None, 1, F), lambda i, j: (j + n_pairs, 0, 0)),
            pl.BlockSpec((None, F, D), lambda i, j: (j, 0, 0)), tile, tile, vec],
        out_specs=(pl.BlockSpec((None, tm, F), lambda i, j: (j, i, 0)), pl.BlockSpec((None, tm, F), lambda i, j: (j, i, 0)),
                   tile, pl.BlockSpec((1, LANES), lambda i, j: (0, 0)), vec),
        scratch_shapes=[pltpu.VMEM((tm, D), F32), pltpu.VMEM((tm + PAD, F), F32)],
        compiler_params=_cparams())(up, up, up, up, cw, cw, cb, cb, wd, h1, tgt, g3)


def _ffn_bwd_a_call(dh2, ca, cb, wd, tm, name):
    n_pairs, T, F = ca.shape
    D = dh2.shape[1]

    def body(dh2_ref, ca_ref, cb_ref, wd_ref, dua_ref, dub_ref, dwd_ref, dcba_ref, dcbb_ref):
        @pl.when(pl.program_id(1) == 0)
        def _():
            for r in (dwd_ref, dcba_ref, dcbb_ref):
                r[...] = jnp.zeros_like(r)

        dh2 = dh2_ref[...]
        ua, ub = ca_ref[...].astype(F32), cb_ref[...].astype(F32)
        sa = jax.nn.sigmoid(ua)
        gate = ua * sa
        dact = _dot_nt(dh2, wd_ref[...])
        dwd_ref[...] += _dot_tn(gate * ub, dh2)
        dub = dact * gate
        dua = dact * ub * sa * (1.0 + ua * (1.0 - sa))
        dcba_ref[...] += jnp.sum(dua, axis=0, keepdims=True)
        dcbb_ref[...] += jnp.sum(dub, axis=0, keepdims=True)
        dua_ref[...] = dua.astype(ACT)
        dub_ref[...] = dub.astype(ACT)

    blk = pl.BlockSpec((None, tm, F), lambda j, i: (j, i, 0))
    vec = pl.BlockSpec((None, 1, F), lambda j, i: (j, 0, 0))
    return pl.pallas_call(
        body, name=name,
        out_shape=(jax.ShapeDtypeStruct((n_pairs, T, F), ACT), jax.ShapeDtypeStruct((n_pairs, T, F), ACT),
                   jax.ShapeDtypeStruct((n_pairs, F, D), F32), jax.ShapeDtypeStruct((n_pairs, 1, F), F32),
                   jax.ShapeDtypeStruct((n_pairs, 1, F), F32)),
        grid=(n_pairs, T // tm),
        in_specs=[pl.BlockSpec((tm, D), lambda j, i: (i, 0)), blk, blk, pl.BlockSpec((None, F, D), lambda j, i: (j, 0, 0))],
        out_specs=(blk, blk, pl.BlockSpec((None, F, D), lambda j, i: (j, 0, 0)), vec, vec),
        compiler_params=_cparams())(dh2, ca, cb, wd)


def _ffn_bwd_b_call(dua, dub, up, cw, wup, h1, g2, dh2, tm, tps, name):
    n_pairs, T, F = dua.shape
    D = h1.shape[1]
    hb = tm // PAD
    last = T // PAD - 1

    def body(da_ref, na_ref, db_ref, nb_ref, ua_ref, ub_ref, cwa_ref, cwb_ref, wa_ref, wb_ref, h1_ref, g2_ref, dh2_ref,
             dupa_ref, dupb_ref, dh1_ref, dg2_ref, dcwa_ref, dcwb_ref, acc_ref, ext_ref):
        i, j = pl.program_id(0), pl.program_id(1)
        edge = (i % tps) == tps - 1

        @pl.when((i == 0) & (j == 0))
        def _():
            dcwa_ref[...] = jnp.zeros_like(dcwa_ref)
            dcwb_ref[...] = jnp.zeros_like(dcwb_ref)

        outs = []
        for d_ref, n_ref, u_ref, cw_ref, o_ref, dcw_ref in (
                (da_ref, na_ref, ua_ref, cwa_ref, dupa_ref, dcwa_ref),
                (db_ref, nb_ref, ub_ref, cwb_ref, dupb_ref, dcwb_ref)):
            t = _conv_taps(d_ref, n_ref, ext_ref, edge, tm, False)
            cwv = cw_ref[...]
            dup = cwv[2:3] * t[0] + cwv[1:2] * t[1] + cwv[0:1] * t[2]
            o_ref[...] = dup.astype(ACT)
            outs.append(dup)
            u = u_ref[...].astype(F32)
            dcw_ref[j] += jnp.concatenate([jnp.sum(u * t[2 - k], axis=0, keepdims=True) for k in range(3)], axis=0)
        contrib = _dot_nt(outs[0], wa_ref[...]) + _dot_nt(outs[1], wb_ref[...])

        @pl.when(j == 0)
        def _():
            acc_ref[...] = contrib

        @pl.when(j > 0)
        def _():
            acc_ref[...] += contrib

        @pl.when((i == 0) & (j == 0))
        def _():
            dg2_ref[...] = jnp.zeros_like(dg2_ref)

        @pl.when(j == n_pairs - 1)
        def _():
            _, vjp = jax.vjp(_rms, h1_ref[...], g2_ref[...])
            dh, dg = vjp(acc_ref[...])
            dh1_ref[...] = dh2_ref[...] + dh
            dg2_ref[...] += dg

    tile = pl.BlockSpec((tm, D), lambda i, j: (i, 0))
    vec = pl.BlockSpec((1, D), lambda i, j: (0, 0))
    pair = lambda: [pl.BlockSpec((None, tm, F), lambda i, j: (j, i, 0)),
                    pl.BlockSpec((None, PAD, F), lambda i, j: (j, jnp.minimum((i + 1) * hb, last), 0))]
    act = jax.ShapeDtypeStruct((n_pairs, T, F), ACT)
    dcw = jax.ShapeDtypeStruct((n_pairs, 3, F), F32)
    dcw_spec = pl.BlockSpec((n_pairs, 3, F), lambda i, j: (0, 0, 0))
    return pl.pallas_call(
        body, name=name,
        out_shape=(act, act, jax.ShapeDtypeStruct((T, D), F32), jax.ShapeDtypeStruct((1, D), F32), dcw, dcw),
        grid=(T // tm, n_pairs),
        in_specs=pair() + pair() + [
            pl.BlockSpec((None, tm, F), lambda i, j: (j, i, 0)), pl.BlockSpec((None, tm, F), lambda i, j: (j + n_pairs, i, 0)),
            pl.BlockSpec((None, 3, F), lambda i, j: (j, 0, 0)), pl.BlockSpec((None, 3, F), lambda i, j: (j + n_pairs, 0, 0)),
            pl.BlockSpec((None, D, F), lambda i, j: (j, 0, 0)), pl.BlockSpec((None, D, F), lambda i, j: (j + n_pairs, 0, 0)),
            tile, vec, tile],
        out_specs=(pl.BlockSpec((None, tm, F), lambda i, j: (j, i, 0)), pl.BlockSpec((None, tm, F), lambda i, j: (j, i, 0)),
                   tile, vec, dcw_spec, dcw_spec),
        scratch_shapes=[pltpu.VMEM((tm, D), F32), pltpu.VMEM((tm + PAD, F), F32)],
        compiler_params=_cparams())(dua, dua, dub, dub, up, up, cw, cw, wup, wup, h1, g2, dh2)


def _in_bwd_call(dp, w_in, h0, g1, dh1, tm, name):
    T, D = h0.shape
    S, _, N = w_in.shape

    def body(dp_ref, w_ref, h0_ref, g1_ref, dh1_ref, dh0_ref, dg1_ref, acc_ref):
        i, j = pl.program_id(0), pl.program_id(1)
        contrib = _dot_nt(dp_ref[...], w_ref[...])

        @pl.when(j == 0)
        def _():
            acc_ref[...] = contrib

        @pl.when(j > 0)
        def _():
            acc_ref[...] += contrib

        @pl.when((i == 0) & (j == 0))
        def _():
            dg1_ref[...] = jnp.zeros_like(dg1_ref)

        @pl.when(j == S - 1)
        def _():
            _, vjp = jax.vjp(_rms, h0_ref[...], g1_ref[...])
            dh, dg = vjp(acc_ref[...])
            dh0_ref[...] = dh1_ref[...] + dh
            dg1_ref[...] += dg

    tile = pl.BlockSpec((tm, D), lambda i, j: (i, 0))
    vec = pl.BlockSpec((1, D), lambda i, j: (0, 0))
    return pl.pallas_call(
        body, name=name, out_shape=(jax.ShapeDtypeStruct((T, D), F32), jax.ShapeDtypeStruct((1, D), F32)),
        grid=(T // tm, S),
        in_specs=[pl.BlockSpec((tm, N), lambda i, j: (i, j)), pl.BlockSpec((None, D, N), lambda i, j: (j, 0, 0)),
                  tile, vec, tile],
        out_specs=(tile, vec), scratch_shapes=[pltpu.VMEM((tm, D), F32)],
        compiler_params=_cparams())(dp, w_in, h0, g1, dh1)


def _meta_grad_call(dh0_3, name):
    B, L, D = dh0_3.shape

    def body(d_ref, o_ref):
        o_ref[...] = jnp.sum(d_ref[...], axis=0)

    return pl.pallas_call(
        body, name=name, out_shape=jax.ShapeDtypeStruct((N_META, D), F32), grid=(1,),
        in_specs=[pl.BlockSpec((B, N_META, D), lambda i: (0, 0, 0))],
        out_specs=pl.BlockSpec((N_META, D), lambda i: (0, 0)), compiler_params=_cparams())(dh0_3)


_RELS = [(dx, dy, dc) for dx in (0, 1) for dy in (0, 1) for dc in (0, 1)][1:]


def _exchange_call(arrs, scatter, name):
    n = len(arrs)
    n_rel = len(_RELS)

    def body(*refs):
        ins, outs = refs[:n], refs[n:2 * n]
        send_sems, recv_sems, loc_sems = refs[2 * n:]
        x, y, c = lax.axis_index("x"), lax.axis_index("y"), lax.axis_index("c")
        me = 4 * x + 2 * y + c
        started = []
        for k in range(n):
            src_me = ins[k].at[me] if scatter else ins[k]
            loc = pltpu.make_async_copy(src_me, outs[k].at[me], loc_sems.at[k])
            loc.start()
            started.append(loc)
        waits = []
        for r, (dx, dy, dc) in enumerate(_RELS):
            px, py, pc = (x + dx) % 2, (y + dy) % 2, (c + dc) % 2
            pid = 4 * px + 2 * py + pc
            for k in range(n):
                s = k * n_rel + r
                src = ins[k].at[pid] if scatter else ins[k]
                cp = pltpu.make_async_remote_copy(
                    src_ref=src, dst_ref=outs[k].at[me], send_sem=send_sems.at[s], recv_sem=recv_sems.at[s],
                    device_id=(px, py, pc), device_id_type=pl.DeviceIdType.MESH)
                cp.start()
                waits.append(pltpu.make_async_remote_copy(
                    src_ref=src, dst_ref=outs[k].at[pid], send_sem=send_sems.at[s], recv_sem=recv_sems.at[s],
                    device_id=(px, py, pc), device_id_type=pl.DeviceIdType.MESH))
        for w in waits:
            w.wait_send()
            w.wait_recv()
        for loc in started:
            loc.wait()

    out_shape = tuple(jax.ShapeDtypeStruct(a.shape if scatter else (N_DEV,) + a.shape, a.dtype) for a in arrs)
    hbm = pl.BlockSpec(memory_space=pl.ANY)
    return pl.pallas_call(
        body, name=name, out_shape=out_shape, in_specs=[hbm] * n, out_specs=tuple([hbm] * n),
        scratch_shapes=[pltpu.SemaphoreType.DMA((n * n_rel,)), pltpu.SemaphoreType.DMA((n * n_rel,)),
                        pltpu.SemaphoreType.DMA((n,))],
        compiler_params=pltpu.CompilerParams(has_side_effects=True))(*arrs)


_HBM = pl.BlockSpec(memory_space=pltpu.HBM)
_SEM = pl.BlockSpec(memory_space=pltpu.SEMAPHORE)
_DATAFLOW = pltpu.SideEffectType.DATAFLOW_SIDE_EFFECTING


def _peer_copies(ins, lands, send_sems, recv_sems, scatter):
    n = len(ins)
    x, y, c = lax.axis_index("x"), lax.axis_index("y"), lax.axis_index("c")
    me = 4 * x + 2 * y + c
    sends, arrivals = [], []
    for r, (dx, dy, dc) in enumerate(_RELS):
        px, py, pc = (x + dx) % 2, (y + dy) % 2, (c + dc) % 2
        pid = 4 * px + 2 * py + pc
        for k in range(n):
            s = k * len(_RELS) + r
            src = ins[k].at[pid] if scatter else ins[k]
            for dst, out in ((lands[k].at[me], sends), (lands[k].at[pid], arrivals)):
                out.append(pltpu.make_async_remote_copy(
                    src_ref=src, dst_ref=dst, send_sem=send_sems.at[s], recv_sem=recv_sems.at[s],
                    device_id=(px, py, pc), device_id_type=pl.DeviceIdType.MESH))
    return sends, arrivals


def _exchange_start(arrs, scatter, name):
    n = len(arrs)
    n_sem = n * len(_RELS)

    def body(*refs):
        ins, lands = refs[:n], refs[n:2 * n]
        send_sems, recv_sems = refs[2 * n], refs[2 * n + 1]
        token = refs[-1]
        sends, _ = _peer_copies(ins, lands, send_sems, recv_sems, scatter)
        for cp in sends:
            cp.start()
        token[...] = jnp.zeros_like(token)

    land_shapes = [a.shape if scatter else (N_DEV,) + a.shape for a in arrs]
    ops = [pltpu.with_memory_space_constraint(a, pltpu.HBM) for a in arrs]
    ops += [pltpu.with_memory_space_constraint(lax.empty(s, a.dtype), pltpu.HBM) for s, a in zip(land_shapes, arrs)]
    out = pl.pallas_call(
        body, name=name,
        out_shape=(pltpu.SemaphoreType.DMA((n_sem,)), pltpu.SemaphoreType.DMA((n_sem,)),
                   *[pltpu.HBM(a.shape, a.dtype) for a in arrs],
                   *[pltpu.HBM(s, a.dtype) for s, a in zip(land_shapes, arrs)],
                   jax.ShapeDtypeStruct((SUBLANES, LANES), F32)),
        in_specs=[_HBM] * (2 * n),
        out_specs=(_SEM, _SEM, *[_HBM] * (2 * n), pl.BlockSpec(memory_space=pltpu.VMEM)),
        input_output_aliases={i: 2 + i for i in range(2 * n)},
        compiler_params=pltpu.CompilerParams(has_side_effects=_DATAFLOW))(*ops)
    return out[0], out[1], list(out[2:2 + n]), list(out[2 + n:2 + 2 * n]), out[-1]


def _exchange_wait(started, after, scatter, name):
    send_sems, recv_sems, srcs, lands, _ = started
    n = len(srcs)

    def body(*refs):
        ins, lands_ = refs[:n], refs[n:2 * n]
        _, arrivals = _peer_copies(ins, lands_, refs[2 * n], refs[2 * n + 1], scatter)
        for cp in arrivals:
            cp.wait_send()
            cp.wait_recv()

    out = pl.pallas_call(
        body, name=name,
        out_shape=(*[pltpu.HBM(a.shape, a.dtype) for a in srcs], *[pltpu.HBM(a.shape, a.dtype) for a in lands]),
        in_specs=[_HBM] * (2 * n) + [_SEM, _SEM, pl.BlockSpec(memory_space=pl.ANY)],
        out_specs=tuple([_HBM] * (2 * n)), input_output_aliases={i: i for i in range(2 * n)},
        compiler_params=pltpu.CompilerParams(has_side_effects=_DATAFLOW))(*srcs, *lands, send_sems, recv_sems, after)
    return list(out[:n]), list(out[n:])


def _place_own_call(srcs, lands, scatter, me, name):
    outs = []
    for k, (src, land) in enumerate(zip(srcs, lands)):
        R, C = land.shape[1:]
        tr = R
        while tr % 32 == 0 and tr * C * land.dtype.itemsize > 2 * 1024 * 1024:
            tr //= 2

        def body(me_ref, s_ref, l_ref, o_ref):
            o_ref[...] = s_ref[...]

        src_spec = (pl.BlockSpec((None, tr, C), lambda i, me_ref: (me_ref[0], i, 0)) if scatter
                    else pl.BlockSpec((tr, C), lambda i, me_ref: (i, 0)))
        outs.append(pl.pallas_call(
            body, name=f"{name}_{k}", out_shape=jax.ShapeDtypeStruct(land.shape, land.dtype),
            grid_spec=pltpu.PrefetchScalarGridSpec(
                num_scalar_prefetch=1, grid=(R // tr,),
                in_specs=[src_spec, pl.BlockSpec(memory_space=pl.ANY)],
                out_specs=pl.BlockSpec((None, tr, C), lambda i, me_ref: (me_ref[0], i, 0))),
            input_output_aliases={2: 0}, compiler_params=_cparams())(me, src, land))
    return outs


def _adamw_shard_call(w, parts, m, v, name):
    R, C = w.shape
    tr = _tile(R, 128) if R % 16 == 0 else R

    def body(w_ref, p_ref, m_ref, v_ref, g_ref, d_ref, nm_ref, nv_ref):
        g = p_ref[0].astype(F32)
        for s in range(1, N_DEV):
            g = g + p_ref[s].astype(F32)
        d, nm, nv = _adamw(w_ref[...], g, m_ref[...], v_ref[...])
        g_ref[...] = g
        d_ref[...] = d
        nm_ref[...] = nm
        nv_ref[...] = nv

    tile = pl.BlockSpec((tr, C), lambda i: (i, 0))
    sh = jax.ShapeDtypeStruct((R, C), F32)
    return pl.pallas_call(
        body, name=name, out_shape=(sh, sh, sh, sh), grid=(R // tr,),
        in_specs=[tile, pl.BlockSpec((N_DEV, tr, C), lambda i: (0, i, 0)), tile, tile],
        out_specs=(tile, tile, tile, tile), compiler_params=_cparams())(w, parts, m, v)


def _pack(arrs, rows_mult=SUBLANES):
    flat = jnp.concatenate([a.reshape(-1).astype(F32) for a in arrs])
    n = flat.shape[0]
    per = rows_mult * LANES
    total = -(-n // per) * per
    return jnp.pad(flat, (0, total - n)).reshape(total // LANES, LANES)


def _unpack(pack, shapes):
    flat = pack.reshape(-1)
    out, off = [], 0
    for s in shapes:
        n = 1
        for d in s:
            n *= d
        out.append(flat[off:off + n].reshape(s))
        off += n
    return out


def kernel(x, meta_tokens, mix_norm_g, w_in, ssm_lambda_re, ssm_lambda_im, ssm_log_dt, ssm_b_re, ssm_b_im, ssm_c_re, ssm_c_im, ssm_d, ssm_w_glu, w_ssm_proj, hgrn_lb_logits, hgrn_norm_g, w_hgrn_proj, w_out, ffn_norm_g, w_up, conv_w, conv_b, w_down, final_norm_g, loss_target, m_meta_tokens, m_mix_norm_g, m_w_in, m_ssm_lambda_re, m_ssm_lambda_im, m_ssm_log_dt, m_ssm_b_re, m_ssm_b_im, m_ssm_c_re, m_ssm_c_im, m_ssm_d, m_ssm_w_glu, m_w_ssm_proj, m_hgrn_lb_logits, m_hgrn_norm_g, m_w_hgrn_proj, m_w_out, m_ffn_norm_g, m_w_up, m_conv_w, m_conv_b, m_w_down, m_final_norm_g, v_meta_tokens, v_mix_norm_g, v_w_in, v_ssm_lambda_re, v_ssm_lambda_im, v_ssm_log_dt, v_ssm_b_re, v_ssm_b_im, v_ssm_c_re, v_ssm_c_im, v_ssm_d, v_ssm_w_glu, v_w_ssm_proj, v_hgrn_lb_logits, v_hgrn_norm_g, v_w_hgrn_proj, v_w_out, v_ffn_norm_g, v_w_up, v_conv_w, v_conv_b, v_w_down, v_final_norm_g):
    args = dict(locals())
    B, S_len, D = x.shape
    L = S_len + N_META
    T = B * L
    tm = _tile(L, ROW_TILE_CAP)
    tps = L // tm
    G, P = ssm_lambda_re.shape[1:]
    H = ssm_b_re.shape[-1]
    W = G * H
    n_cb = W // LANES
    gpb = G // n_cb
    hd = hgrn_norm_g.shape[1]
    n_heads = D // hd
    n_in = w_in.shape[2]
    F = w_up.shape[2]
    assert W == D and n_in % LANES == 0

    me = (4 * lax.axis_index("x") + 2 * lax.axis_index("y") + lax.axis_index("c")).astype(jnp.int32).reshape(1)
    meta_g, cw_g = _exchange_call([meta_tokens, conv_w[0]], False, "gather_small_params")
    ga = _exchange_start([w_in[0].astype(MXU)], False, "gather_a_start")
    gb = _exchange_start(
        [w_up[0].astype(MXU), ssm_w_glu[0].astype(MXU), w_ssm_proj[0].astype(MXU), w_hgrn_proj[0].astype(MXU),
         w_out[0].astype(MXU), w_down[0].astype(MXU)], False, "gather_b_start")
    started_tok = (ga[4] + gb[4])[0:1, 0:1]
    meta_full = meta_g.transpose(1, 0, 2).reshape(N_META, D)
    cb_g = conv_b.reshape(N_DEV, 1, F)

    h0 = jnp.concatenate([jnp.broadcast_to(meta_full[None], (B, N_META, D)), x], axis=1).reshape(T, D)
    tgt = jnp.concatenate([jnp.zeros((B, N_META, D), F32), loss_target], axis=1).reshape(T, D)

    lr, li = ssm_lambda_re[0], ssm_lambda_im[0]
    ldt = ssm_log_dt[0].reshape(G, 1)
    bt_re = ssm_b_re[0].transpose(2, 0, 1).reshape(H, G * P)
    bt_im = ssm_b_im[0].transpose(2, 0, 1).reshape(H, G * P)
    seg = _seg_len(L)
    a_re, a_im, as_re, as_im, coef_re, coef_im = _small_call(
        _disc_a_power(seg), [lr, li, ldt], [((G, P), F32)] * 6, "s5_discretise")
    bbt_re, bbt_im = _small_call(
        _disc_b, [coef_re.reshape(1, G * P), coef_im.reshape(1, G * P), bt_re, bt_im],
        [((H, G * P), F32)] * 2, "s5_input_matrix")
    eye = jnp.eye(gpb, dtype=F32)
    hw = gpb * P

    def expand_b(bbt):
        t = bbt.reshape(H, n_cb, gpb, P).transpose(1, 0, 2, 3)[:, None]
        return (t * eye[None, :, None, :, None]).reshape(n_cb, gpb * H, hw)

    def expand_c(cm):
        t = cm.reshape(n_cb, gpb, H, P).transpose(0, 1, 3, 2)[:, :, :, None]
        return (t * eye[None, :, None, :, None]).reshape(n_cb, hw, gpb * H)

    wb = jnp.concatenate([expand_b(bbt_re), expand_b(bbt_im)], axis=2).astype(MXU)
    wc = jnp.concatenate([expand_c(ssm_c_re[0]), -expand_c(ssm_c_im[0])], axis=1).astype(MXU)
    tab = jnp.stack([jnp.concatenate([a_re.reshape(n_cb, hw), a_im.reshape(n_cb, hw)], axis=1),
                     jnp.concatenate([as_re.reshape(n_cb, hw), as_im.reshape(n_cb, hw)], axis=1)], axis=1)
    tab = jnp.broadcast_to(tab[:, :, None, :], (n_cb, 2, SUBLANES, 2 * hw))
    dsk = ssm_d.reshape(n_cb, 1, LANES)
    lb = _small_call(_lb_fn, [hgrn_lb_logits], [((1, D), F32)], "hgrn_lower_bound")[0]

    z1 = _norm_call(h0, mix_norm_g + started_tok, tm, "mix_norm")
    ready = jnp.concatenate([t[(0,) * (t.ndim - 1)][0:1].astype(F32) for t in (z1, wb, wc, tab, lb, tgt)])
    ga_src, ga_land = _exchange_wait(ga, ready, False, "gather_a_wait")
    win_g = _place_own_call(ga_src, ga_land, False, me, "gather_a_own")[0]
    p = _mm_shard(z1, win_g, tm, "in_proj", False)
    p3 = p.reshape(B, L, p.shape[1])
    u_seg = _to_segments(p3[:, :, :W], seg)
    ya_seg, s_all = _s5_fwd_call(u_seg, wb, wc, tab, dsk, "s5_fwd")
    ya = _from_segments(ya_seg, seg, L).reshape(T, W)
    yb = _hgrn_fwd_call(p3, lb, hgrn_norm_g, n_heads, n_cb, "hgrn_fwd").reshape(T, D)
    gb_src, gb_land = _exchange_wait(gb, yb, False, "gather_b_wait")
    gathered = _place_own_call(gb_src, gb_land, False, me, "gather_b_own")
    wup_g = gathered[0]
    wglu_g, wsp_g, whp_g, wout_g = [g.reshape(D, D) for g in gathered[1:5]]
    wdn_g = gathered[5].reshape(N_DEV // 2, 2 * w_down.shape[1], D)
    yo, a_br = _glu_proj_call(ya, wglu_g, wsp_g, tm, "s5_glu_proj")
    col_ga = 5
    h1, mg, bm, z2 = _merge_call(yb, a_br, p, h0, whp_g, wout_g, ffn_norm_g, col_ga, tm, "merge")
    up = _mm_shard(z2, wup_g, tm, "up_proj", True)
    conv_a, conv_b_out, dh2, loss_part, dg3 = _ffn_fwd_call(up, cw_g, cb_g, wdn_g, h1, tgt, final_norm_g.reshape(1, D),
                                                            tm, tps, "ffn_out_loss")

    dua, dub, dwd, dcba, dcbb = _ffn_bwd_a_call(dh2, conv_a, conv_b_out, wdn_g, tm, "ffn_bwd_gate")
    dupa, dupb, dh1, dg2, dcwa, dcwb = _ffn_bwd_b_call(dua, dub, up, cw_g, wup_g, h1, ffn_norm_g, dh2, tm, tps,
                                                       "ffn_bwd_up")
    dwup = jnp.concatenate([_mm_tn(z2, dupa, N_DEV // 2, tm, "dw_up_a", True),
                            _mm_tn(z2, dupb, N_DEV // 2, tm, "dw_up_b", True)], axis=0)
    sh_rows = D // N_DEV
    sa = _exchange_start([dwup, dwd.reshape(N_DEV, w_down.shape[1], D)], True, "scatter_a_start")
    da_br, dbm, dga, dgb, dwout = _out_merge_bwd_call(mg, dh1, wout_g + sa[4][0:1, 0:1].astype(MXU), a_br, bm, p,
                                                      col_ga, tm, "out_merge_bwd")
    dyo, dwsp = _lin_bwd(yo, da_br, wsp_g, tm, "ssm_proj_bwd")
    dyb, dwhp = _lin_bwd(yb, dbm, whp_g, tm, "hgrn_proj_bwd")
    dya, dwglu = _glu_bwd_call(ya, dyo, wglu_g, tm, "s5_glu_bwd")
    sb = _exchange_start([t.reshape(N_DEV, sh_rows, D) for t in (dwglu, dwsp, dwhp, dwout)], True, "scatter_b_start")
    tok_b = sb[4][0:1, :]
    du_seg, dwb, dwc, dab, ddsk = _s5_bwd_call(u_seg, s_all, _to_segments(dya.reshape(B, L, W), seg), wb, wc, tab,
                                               dsk + tok_b[None], "s5_bwd")
    du = _from_segments(du_seg, seg, L)

    def diag_b(dw):
        t = (dw.reshape(n_cb, gpb, H, gpb, P) * eye[None, :, None, :, None]).sum(axis=1)
        return t.transpose(1, 0, 2, 3).reshape(H, G * P)

    def diag_c(dw):
        t = (dw.reshape(n_cb, gpb, P, gpb, H) * eye[None, :, None, :, None]).sum(axis=3)
        return t.transpose(0, 1, 3, 2).reshape(G, H, P)

    early_parts = [dab[:, 0, :hw].reshape(G, P), dab[:, 0, hw:].reshape(G, P), ddsk.reshape(1, D)]
    early = [_pack(early_parts), diag_b(dwb[:, :, :hw]), diag_b(dwb[:, :, hw:]),
             diag_c(dwc[:, :hw]).reshape(G * H, P), -diag_c(dwc[:, hw:]).reshape(G * H, P)]
    se = _exchange_start(early, False, "gather_s5_grads_start")
    dq, dfl, di, dog, dlb, dng = _hgrn_bwd_call(p3, dyb.reshape(B, L, D), lb, hgrn_norm_g + tok_b + se[4][0:1, :],
                                                n_heads, n_cb, "hgrn_bwd")
    dp = jnp.concatenate([du.reshape(T, W), dq.reshape(T, D), dfl.reshape(T, D), di.reshape(T, D),
                          dog.reshape(T, D), dga, dgb], axis=1)
    dwin = _mm_tn(z1, dp, N_DEV, tm, "dw_in", False)
    sc = _exchange_start([dwin.astype(WIRE)], True, "scatter_c_start")
    dh0, dg1 = _in_bwd_call(dp, win_g, h0, mix_norm_g + sc[4][0:1, 0:1], dh1, tm, "in_proj_bwd")
    dh0_3 = dh0.reshape(B, L, D)
    grad_x = dh0_3[:, N_META:]
    dmeta = _meta_grad_call(dh0_3, "meta_grad")

    late_parts = [dg1, dlb, dng, dg2, jnp.concatenate([dcba, dcbb], axis=0).reshape(1, N_DEV * F), dg3, loss_part]
    late_pack = _pack(late_parts)

    dcw = jnp.concatenate([dcwa, dcwb], axis=0)
    dmeta_s = dmeta.reshape(N_META, N_DEV, D // N_DEV).transpose(1, 0, 2)
    parts_d = _exchange_call([dmeta_s, dcw], True, "scatter_small_grads")
    late_all = _exchange_call([late_pack], False, "gather_small_grads")[0]
    early_all = _place_own_call(*_exchange_wait(se, late_all, False, "gather_s5_grads_wait"), False, me,
                                "gather_s5_grads_own")
    parts_a = _place_own_call(*_exchange_wait(sa, late_all, True, "scatter_a_wait"), True, me, "scatter_a_own")
    parts_b = _place_own_call(*_exchange_wait(sb, late_all, True, "scatter_b_wait"), True, me, "scatter_b_own")
    parts_c = _place_own_call(*_exchange_wait(sc, late_all, True, "scatter_c_wait"), True, me, "scatter_c_own")
    parts = [parts_c[0], parts_a[0], *parts_b, parts_a[1], parts_d[0], parts_d[1]]

    def sum8(*gathered):
        out = []
        for a in gathered:
            t = a[0]
            for s in range(1, N_DEV):
                t = t + a[s]
            out.append(t)
        return tuple(out)

    sums = _small_call(sum8, [*early_all, late_all], [(a.shape, F32) for a in (*early, late_pack)], "sum_small_grads")
    t_abr, t_abi, g_dsk = _unpack(sums[0], [a.shape for a in early_parts])
    t_bbr, t_bbi = sums[1], sums[2]
    g_cre, g_cim = sums[3].reshape(G, H, P), sums[4].reshape(G, H, P)
    g_g1, t_lb, g_ng, g_g2, g_cb, g_g3, loss_v = _unpack(sums[5], [a.shape for a in late_parts])

    def disc_b_bwd(cr, ci, br, bi, dbr, dbi):
        _, vjp = jax.vjp(_disc_b, cr, ci, br, bi)
        return vjp((dbr, dbi))

    t_cr, t_ci, g_btr, g_bti = _small_call(
        disc_b_bwd, [coef_re.reshape(1, G * P), coef_im.reshape(1, G * P), bt_re, bt_im, t_bbr, t_bbi],
        [((1, G * P), F32)] * 2 + [((H, G * P), F32)] * 2, "s5_input_matrix_bwd")

    def disc_a_bwd(lr_, li_, ldt_, dar, dai, dcr, dci):
        _, vjp = jax.vjp(_disc_a, lr_, li_, ldt_)
        return vjp((dar, dai, dcr, dci))

    g_lr, g_li, g_ldt = _small_call(
        disc_a_bwd, [lr, li, ldt, t_abr, t_abi, t_cr.reshape(G, P), t_ci.reshape(G, P)],
        [((G, P), F32)] * 2 + [((G, 1), F32)], "s5_discretise_bwd")

    def lb_bwd(logits, d):
        _, vjp = jax.vjp(_lb_fn, logits)
        return vjp(d)

    g_lbl = _small_call(lb_bwd, [hgrn_lb_logits, t_lb], [(hgrn_lb_logits.shape, F32)], "hgrn_lower_bound_bwd")[0]

    grads = dict(
        mix_norm_g=g_g1, ssm_lambda_re=g_lr[None], ssm_lambda_im=g_li[None], ssm_log_dt=g_ldt.reshape(1, G),
        ssm_b_re=g_btr.reshape(H, G, P).transpose(1, 2, 0)[None], ssm_b_im=g_bti.reshape(H, G, P).transpose(1, 2, 0)[None],
        ssm_c_re=g_cre[None], ssm_c_im=g_cim[None], ssm_d=g_dsk, hgrn_lb_logits=g_lbl, hgrn_norm_g=g_ng,
        ffn_norm_g=g_g2, conv_b=g_cb.reshape(1, N_DEV * F), final_norm_g=g_g3.reshape(D))
    loss = loss_v[0, 0]

    delta, new_m, new_v = {}, {}, {}
    sharded = [("w_in", parts[0], (D, n_in)), ("w_up", parts[1], (D, F)), ("ssm_w_glu", parts[2], (sh_rows, D)),
               ("w_ssm_proj", parts[3], (sh_rows, D)), ("w_hgrn_proj", parts[4], (sh_rows, D)),
               ("w_out", parts[5], (sh_rows, D)), ("w_down", parts[6], (w_down.shape[1], D)),
               ("meta_tokens", parts[7], (N_META, D // N_DEV)), ("conv_w", parts[8], (3, F))]
    for name, part, shp in sharded:
        full = args[name].shape
        g, d_, nm, nv = _adamw_shard_call(args[name].reshape(shp), part, args["m_" + name].reshape(shp),
                                          args["v_" + name].reshape(shp), "adamw_" + name)
        grads[name], delta[name], new_m[name], new_v[name] = [t.reshape(full) for t in (g, d_, nm, nv)]

    for n, shp in (("ssm_b_re", (G * P, H)), ("ssm_b_im", (G * P, H)), ("ssm_c_re", (G * H, P)), ("ssm_c_im", (G * H, P))):
        outs = _small_call(_adamw, [t.reshape(shp) for t in (args[n], grads[n], args["m_" + n], args["v_" + n])],
                           [(shp, F32)] * 3, "adamw_" + n)
        delta[n], new_m[n], new_v[n] = [o.reshape(args[n].shape) for o in outs]
    rep = ["mix_norm_g", "ssm_lambda_re", "ssm_lambda_im", "ssm_log_dt", "ssm_d", "hgrn_lb_logits", "hgrn_norm_g",
           "ffn_norm_g", "conv_b", "final_norm_g"]
    rep_shapes = [args[n].shape for n in rep]
    packs = [_pack([args[pre + n] for n in rep]) for pre in ("", "m_", "v_")]
    g_pack = _pack([grads[n] for n in rep])
    outs = _small_call(lambda w, g, m, v: _adamw(w, g, m, v), [packs[0], g_pack, packs[1], packs[2]],
                       [(g_pack.shape, F32)] * 3, "adamw_replicated")
    for n, d_, nm, nv in zip(rep, *[_unpack(o, rep_shapes) for o in outs]):
        delta[n], new_m[n], new_v[n] = d_, nm, nv

    names = ["meta_tokens", "mix_norm_g", "w_in", "ssm_lambda_re", "ssm_lambda_im", "ssm_log_dt", "ssm_b_re",
             "ssm_b_im", "ssm_c_re", "ssm_c_im", "ssm_d", "ssm_w_glu", "w_ssm_proj", "hgrn_lb_logits", "hgrn_norm_g",
             "w_hgrn_proj", "w_out", "ffn_norm_g", "w_up", "conv_w", "conv_b", "w_down", "final_norm_g"]
    return (loss, grad_x, *[grads[n] for n in names], *[delta[n] for n in names],
            *[new_m[n] for n in names], *[new_v[n] for n in names])
```
